```python
import math
import jax, jax.numpy as jnp
from jax import lax
import numpy as np

D_MODEL = 1024
BATCH = 8
SEQ = 8192
DEPTH = 4

D_FF = 2816
D_POOL = 256
POOL_WINDOWS = (2, 4, 8, 16)
POOL_GROUP = D_POOL // len(POOL_WINDOWS)
D_CONV = 256
CONV_WIDTH = 3
NA_HEADS = 8
NA_HEAD_DIM = 64
D_NA = NA_HEADS * NA_HEAD_DIM
D_MIX = D_POOL + D_CONV + D_NA
GRID_W = 64
NA_ROWS = 8
NA_COLS = 16
D_IN = D_POOL + 3 * D_CONV + 3 * D_NA
ALPHA = (2.0 * DEPTH) ** 0.25
BETA = (8.0 * DEPTH) ** -0.25
LN_EPS = 1e-5
NEG_INF = -1e30

kernel_name = "hybrid_pool_conv_natten_encoder"


def layer_norm(x, g, b):
    xf = x.astype(jnp.float32)
    mu = jnp.mean(xf, axis=-1, keepdims=True)
    var = jnp.mean(jnp.square(xf - mu), axis=-1, keepdims=True)
    y = (xf - mu) * lax.rsqrt(var + LN_EPS)
    return (y * g.astype(jnp.float32) + b.astype(jnp.float32)).astype(x.dtype)


def swiglu(x, w_gate, w_up, w_down):
    return (jax.nn.silu(x @ w_gate) * (x @ w_up)) @ w_down


def pool_mixer(u, pool_w, pool_scale):
    bsz, s, _ = u.shape
    ng = len(POOL_WINDOWS)
    uf = u.astype(jnp.float32).reshape(bsz, s, ng, POOL_GROUP)
    cs = jnp.concatenate([jnp.zeros((bsz, 1, ng, POOL_GROUP), jnp.float32),
                          jnp.cumsum(uf, axis=1)], axis=1)
    t = jnp.arange(s)
    outs = []
    for g, w in enumerate(POOL_WINDOWS):
        lo = jnp.clip(t - w // 2, 0, s)
        hi = jnp.clip(t - w // 2 + w, 0, s)
        cnt = (hi - lo).astype(jnp.float32)[None, :, None]
        mean = (cs[:, hi, g] - cs[:, lo, g]) / cnt
        outs.append(mean - uf[:, :, g])
    p = jnp.stack(outs, axis=2).astype(u.dtype)
    y = jnp.einsum('bsgc,gcd->bsgd', p, pool_w) * pool_scale.reshape(ng, POOL_GROUP)
    return y.reshape(bsz, s, D_POOL)


def gated_conv_mixer(gate_b, gate_c, h, conv_w):
    z = gate_c * h
    zp = jnp.pad(z, ((0, 0), (1, 1), (0, 0)))
    y = conv_w[0] * zp[:, :-2] + conv_w[1] * zp[:, 1:-1] + conv_w[2] * zp[:, 2:]
    return gate_b * y


def neighbourhood_attention(q, k, v, rpb):
    bsz, s, _ = q.shape
    rows = s // GRID_W
    kr = min(NA_ROWS, rows)
    shp = (bsz, rows, GRID_W, NA_HEADS, NA_HEAD_DIM)
    q, k, v = q.reshape(shp), k.reshape(shp), v.reshape(shp)
    r = jnp.arange(rows)
    row_start = jnp.clip(r - kr // 2, 0, rows - kr)
    row_idx = row_start[:, None] + jnp.arange(kr)[None, :]
    kb = k[:, row_idx]
    vb = v[:, row_idx]
    c = jnp.arange(GRID_W)
    col_start = jnp.clip(c - NA_COLS // 2, 0, GRID_W - NA_COLS)
    col_valid = (c[None, :] >= col_start[:, None]) & (c[None, :] < col_start[:, None] + NA_COLS)
    dr = row_idx - r[:, None] + (NA_ROWS - 1)
    dc = jnp.clip(c[None, :] - c[:, None], -(NA_COLS - 1), NA_COLS - 1) + (NA_COLS - 1)
    bias = rpb[:, dr[:, None, :, None], dc[None, :, None, :]]
    scores = jnp.einsum('brqhd,brikhd->bhrqik', q, kb).astype(jnp.float32) * (NA_HEAD_DIM ** -0.5)
    scores = scores + bias.astype(jnp.float32)
    scores = jnp.where(col_valid[:, None, :], scores, NEG_INF)
    p = jax.nn.softmax(scores, axis=(-2, -1)).astype(v.dtype)
    o = jnp.einsum('bhrqik,brikhd->brqhd', p, vb)
    return o.reshape(bsz, s, D_NA)


def _fwd_setup_inputs(seed: int = 0) -> dict:
    key = jax.random.key(seed)
    ks = jax.random.split(key, 16)
    L, D, F = DEPTH, D_MODEL, D_FF
    nrm = lambda k, shp: jax.random.normal(k, shp, jnp.float32)
    x = nrm(ks[0], (BATCH, SEQ, D))
    ffn1_w_gate = nrm(ks[1], (L, D, F)) * D ** -0.5
    ffn1_w_up = nrm(ks[2], (L, D, F)) * D ** -0.5
    ffn1_w_down = nrm(ks[3], (L, F, D)) * (BETA * F ** -0.5)
    ffn2_w_gate = nrm(ks[4], (L, D, F)) * D ** -0.5
    ffn2_w_up = nrm(ks[5], (L, D, F)) * D ** -0.5
    ffn2_w_down = nrm(ks[6], (L, F, D)) * (BETA * F ** -0.5)
    col_scale = jnp.concatenate([
        jnp.ones((D_POOL + 2 * D_CONV,), jnp.float32),
        jnp.full((D_CONV,), BETA, jnp.float32),
        jnp.ones((2 * D_NA,), jnp.float32),
        jnp.full((D_NA,), BETA, jnp.float32)])
    w_in = nrm(ks[7], (L, D, D_IN)) * D ** -0.5 * col_scale
    pool_w = nrm(ks[8], (L, len(POOL_WINDOWS), POOL_GROUP, POOL_GROUP)) * POOL_GROUP ** -0.5
    pool_scale = 1.0 + 0.1 * nrm(ks[9], (L, D_POOL))
    conv_w = nrm(ks[10], (L, CONV_WIDTH, D_CONV)) * CONV_WIDTH ** -0.5
    rpb = 0.02 * nrm(ks[11], (L, NA_HEADS, 2 * NA_ROWS - 1, 2 * NA_COLS - 1))
    w_out = nrm(ks[12], (L, D_MIX, D)) * (BETA * D_MIX ** -0.5)
    ln_g = 1.0 + 0.05 * nrm(ks[13], (L, 3, D))
    ln_b = 0.02 * nrm(ks[14], (L, 3, D))
    return {"x": x,
            "ffn1_w_gate": ffn1_w_gate, "ffn1_w_up": ffn1_w_up, "ffn1_w_down": ffn1_w_down,
            "ffn2_w_gate": ffn2_w_gate, "ffn2_w_up": ffn2_w_up, "ffn2_w_down": ffn2_w_down,
            "w_in": w_in, "pool_w": pool_w, "pool_scale": pool_scale, "conv_w": conv_w,
            "rpb": rpb, "w_out": w_out, "ln_g": ln_g, "ln_b": ln_b}


def _fwd_reference(x, ffn1_w_gate, ffn1_w_up, ffn1_w_down, ffn2_w_gate, ffn2_w_up, ffn2_w_down,
              w_in, pool_w, pool_scale, conv_w, rpb, w_out, ln_g, ln_b):
    splits = np.cumsum([D_POOL, D_CONV, D_CONV, D_CONV, D_NA, D_NA])
    for l in range(DEPTH):
        x = layer_norm(ALPHA * x + 0.5 * swiglu(x, ffn1_w_gate[l], ffn1_w_up[l], ffn1_w_down[l]),
                       ln_g[l, 0], ln_b[l, 0])
        proj = x @ w_in[l]
        u, gb, gc, h, q, k, v = jnp.split(proj, splits, axis=-1)
        y_a = pool_mixer(u, pool_w[l], pool_scale[l])
        y_b = gated_conv_mixer(gb, gc, h, conv_w[l])
        y_c = neighbourhood_attention(q, k, v, rpb[l])
        y = jnp.concatenate([y_a, y_b, y_c], axis=-1) @ w_out[l]
        x = layer_norm(ALPHA * x + y, ln_g[l, 1], ln_b[l, 1])
        x = layer_norm(ALPHA * x + 0.5 * swiglu(x, ffn2_w_gate[l], ffn2_w_up[l], ffn2_w_down[l]),
                       ln_g[l, 2], ln_b[l, 2])
    return x


import jax as _jax
import jax.numpy as _jnp

TWIN_FORMAT = 'train_step'
FWD_PARAMS = ['x', 'ffn1_w_gate', 'ffn1_w_up', 'ffn1_w_down', 'ffn2_w_gate', 'ffn2_w_up', 'ffn2_w_down', 'w_in', 'pool_w', 'pool_scale', 'conv_w', 'rpb', 'w_out', 'ln_g', 'ln_b']
TWIN_WEIGHTS = ['ffn1_w_gate', 'ffn1_w_up', 'ffn1_w_down', 'ffn2_w_gate', 'ffn2_w_up', 'ffn2_w_down', 'w_in', 'pool_w', 'pool_scale', 'conv_w', 'rpb', 'w_out', 'ln_g', 'ln_b']
TWIN_DIFF_INPUT = 'x'
TWIN_INPUTS = ['x', 'ffn1_w_gate', 'ffn1_w_up', 'ffn1_w_down', 'ffn2_w_gate', 'ffn2_w_up', 'ffn2_w_down', 'w_in', 'pool_w', 'pool_scale', 'conv_w', 'rpb', 'w_out', 'ln_g', 'ln_b', 'loss_target', 'm_ffn1_w_gate', 'm_ffn1_w_up', 'm_ffn1_w_down', 'm_ffn2_w_gate', 'm_ffn2_w_up', 'm_ffn2_w_down', 'm_w_in', 'm_pool_w', 'm_pool_scale', 'm_conv_w', 'm_rpb', 'm_w_out', 'm_ln_g', 'm_ln_b', 'v_ffn1_w_gate', 'v_ffn1_w_up', 'v_ffn1_w_down', 'v_ffn2_w_gate', 'v_ffn2_w_up', 'v_ffn2_w_down', 'v_w_in', 'v_pool_w', 'v_pool_scale', 'v_conv_w', 'v_rpb', 'v_w_out', 'v_ln_g', 'v_ln_b']
TWIN_OUTPUTS = ['loss', 'grad_x', 'grad_ffn1_w_gate', 'grad_ffn1_w_up', 'grad_ffn1_w_down', 'grad_ffn2_w_gate', 'grad_ffn2_w_up', 'grad_ffn2_w_down', 'grad_w_in', 'grad_pool_w', 'grad_pool_scale', 'grad_conv_w', 'grad_rpb', 'grad_w_out', 'grad_ln_g', 'grad_ln_b', 'delta_ffn1_w_gate', 'delta_ffn1_w_up', 'delta_ffn1_w_down', 'delta_ffn2_w_gate', 'delta_ffn2_w_up', 'delta_ffn2_w_down', 'delta_w_in', 'delta_pool_w', 'delta_pool_scale', 'delta_conv_w', 'delta_rpb', 'delta_w_out', 'delta_ln_g', 'delta_ln_b', 'new_m_ffn1_w_gate', 'new_m_ffn1_w_up', 'new_m_ffn1_w_down', 'new_m_ffn2_w_gate', 'new_m_ffn2_w_up', 'new_m_ffn2_w_down', 'new_m_w_in', 'new_m_pool_w', 'new_m_pool_scale', 'new_m_conv_w', 'new_m_rpb', 'new_m_w_out', 'new_m_ln_g', 'new_m_ln_b', 'new_v_ffn1_w_gate', 'new_v_ffn1_w_up', 'new_v_ffn1_w_down', 'new_v_ffn2_w_gate', 'new_v_ffn2_w_up', 'new_v_ffn2_w_down', 'new_v_w_in', 'new_v_pool_w', 'new_v_pool_scale', 'new_v_conv_w', 'new_v_rpb', 'new_v_w_out', 'new_v_ln_g', 'new_v_ln_b']
TWIN_LEAF_KINDS = {'loss': 'loss', 'grad_x': 'grad_x', 'grad_ffn1_w_gate': 'grad_w', 'grad_ffn1_w_up': 'grad_w', 'grad_ffn1_w_down': 'grad_w', 'grad_ffn2_w_gate': 'grad_w', 'grad_ffn2_w_up': 'grad_w', 'grad_ffn2_w_down': 'grad_w', 'grad_w_in': 'grad_w', 'grad_pool_w': 'grad_w', 'grad_pool_scale': 'grad_w', 'grad_conv_w': 'grad_w', 'grad_rpb': 'grad_w', 'grad_w_out': 'grad_w', 'grad_ln_g': 'grad_w', 'grad_ln_b': 'grad_w', 'delta_ffn1_w_gate': 'delta_w', 'delta_ffn1_w_up': 'delta_w', 'delta_ffn1_w_down': 'delta_w', 'delta_ffn2_w_gate': 'delta_w', 'delta_ffn2_w_up': 'delta_w', 'delta_ffn2_w_down': 'delta_w', 'delta_w_in': 'delta_w', 'delta_pool_w': 'delta_w', 'delta_pool_scale': 'delta_w', 'delta_conv_w': 'delta_w', 'delta_rpb': 'delta_w', 'delta_w_out': 'delta_w', 'delta_ln_g': 'delta_w', 'delta_ln_b': 'delta_w', 'new_m_ffn1_w_gate': 'new_m', 'new_m_ffn1_w_up': 'new_m', 'new_m_ffn1_w_down': 'new_m', 'new_m_ffn2_w_gate': 'new_m', 'new_m_ffn2_w_up': 'new_m', 'new_m_ffn2_w_down': 'new_m', 'new_m_w_in': 'new_m', 'new_m_pool_w': 'new_m', 'new_m_pool_scale': 'new_m', 'new_m_conv_w': 'new_m', 'new_m_rpb': 'new_m', 'new_m_w_out': 'new_m', 'new_m_ln_g': 'new_m', 'new_m_ln_b': 'new_m', 'new_v_ffn1_w_gate': 'new_v', 'new_v_ffn1_w_up': 'new_v', 'new_v_ffn1_w_down': 'new_v', 'new_v_ffn2_w_gate': 'new_v', 'new_v_ffn2_w_up': 'new_v', 'new_v_ffn2_w_down': 'new_v', 'new_v_w_in': 'new_v', 'new_v_pool_w': 'new_v', 'new_v_pool_scale': 'new_v', 'new_v_conv_w': 'new_v', 'new_v_rpb': 'new_v', 'new_v_w_out': 'new_v', 'new_v_ln_g': 'new_v', 'new_v_ln_b': 'new_v'}


def _forward(args):
    return _fwd_reference(*[args[k] for k in FWD_PARAMS])


def _output_shape():
    def fwd():
        inp = _fwd_setup_inputs(0)
        return _fwd_reference(*[inp[k] for k in FWD_PARAMS])
    out = _jax.eval_shape(fwd)
    return out.shape, out.dtype

N_MICROBATCH = 1
ADAM_LR = 0.001
ADAM_B1 = 0.9
ADAM_B2 = 0.999
ADAM_EPS = 1e-08
ADAM_WD = 0.01
ADAM_STEP = 10
PER_EXAMPLE_BATCH_AXIS = {'x': 0, 'loss_target': 0}
SHARED_INPUTS = []
_WEIGHT_DTYPES = {'ffn1_w_gate': _jnp.float32, 'ffn1_w_up': _jnp.float32, 'ffn1_w_down': _jnp.float32, 'ffn2_w_gate': _jnp.float32, 'ffn2_w_up': _jnp.float32, 'ffn2_w_down': _jnp.float32, 'w_in': _jnp.float32, 'pool_w': _jnp.float32, 'pool_scale': _jnp.float32, 'conv_w': _jnp.float32, 'rpb': _jnp.float32, 'w_out': _jnp.float32, 'ln_g': _jnp.float32, 'ln_b': _jnp.float32}
MOMENT_SCALE = {'ffn1_w_gate': 1.190619e-02, 'ffn1_w_up': 1.167113e-02, 'ffn1_w_down': 4.611286e-02, 'ffn2_w_gate': 1.185588e-02, 'ffn2_w_up': 1.161151e-02, 'ffn2_w_down': 4.588516e-02, 'w_in': 3.002023e-02, 'pool_w': 6.032346e-02, 'pool_scale': 5.907649e-02, 'conv_w': 2.641974e-02, 'rpb': 1.813736e-03, 'w_out': 7.878232e-02, 'ln_g': 2.032720e+01, 'ln_b': 1.498912e+00}


def _to_microbatches(a, axis):
    t = _jnp.moveaxis(a, axis, 0)
    t = t.reshape((N_MICROBATCH, t.shape[0] // N_MICROBATCH) + t.shape[1:])
    return _jnp.moveaxis(t, 1, axis + 1)


def setup_inputs(seed: int = 0) -> dict:
    inp = _fwd_setup_inputs(seed)
    key = _jax.random.fold_in(_jax.random.key(seed), 7919)
    shape, _ = _output_shape()
    out = dict(inp)
    out["loss_target"] = _jax.random.normal(_jax.random.fold_in(key, 0), shape, _jnp.float32)
    for i, name in enumerate(TWIN_WEIGHTS):
        w = inp[name].astype(_jnp.float32)
        if MOMENT_SCALE is None:
            s = _jnp.sqrt(_jnp.mean(_jnp.square(w)) + 1e-30)
        else:
            s = MOMENT_SCALE[name]
        km, kv = _jax.random.split(_jax.random.fold_in(key, i + 1))
        out[name] = w
        out["m_" + name] = s * _jax.random.normal(km, w.shape, _jnp.float32)
        out["v_" + name] = (s * s) * _jax.random.uniform(kv, w.shape, _jnp.float32, 0.5, 1.5)
    if N_MICROBATCH > 1:
        for name, axis in PER_EXAMPLE_BATCH_AXIS.items():
            out[name] = _to_microbatches(out[name], axis)
    return {'x': out['x'], 'ffn1_w_gate': out['ffn1_w_gate'], 'ffn1_w_up': out['ffn1_w_up'], 'ffn1_w_down': out['ffn1_w_down'], 'ffn2_w_gate': out['ffn2_w_gate'], 'ffn2_w_up': out['ffn2_w_up'], 'ffn2_w_down': out['ffn2_w_down'], 'w_in': out['w_in'], 'pool_w': out['pool_w'], 'pool_scale': out['pool_scale'], 'conv_w': out['conv_w'], 'rpb': out['rpb'], 'w_out': out['w_out'], 'ln_g': out['ln_g'], 'ln_b': out['ln_b'], 'loss_target': out['loss_target'], 'm_ffn1_w_gate': out['m_ffn1_w_gate'], 'm_ffn1_w_up': out['m_ffn1_w_up'], 'm_ffn1_w_down': out['m_ffn1_w_down'], 'm_ffn2_w_gate': out['m_ffn2_w_gate'], 'm_ffn2_w_up': out['m_ffn2_w_up'], 'm_ffn2_w_down': out['m_ffn2_w_down'], 'm_w_in': out['m_w_in'], 'm_pool_w': out['m_pool_w'], 'm_pool_scale': out['m_pool_scale'], 'm_conv_w': out['m_conv_w'], 'm_rpb': out['m_rpb'], 'm_w_out': out['m_w_out'], 'm_ln_g': out['m_ln_g'], 'm_ln_b': out['m_ln_b'], 'v_ffn1_w_gate': out['v_ffn1_w_gate'], 'v_ffn1_w_up': out['v_ffn1_w_up'], 'v_ffn1_w_down': out['v_ffn1_w_down'], 'v_ffn2_w_gate': out['v_ffn2_w_gate'], 'v_ffn2_w_up': out['v_ffn2_w_up'], 'v_ffn2_w_down': out['v_ffn2_w_down'], 'v_w_in': out['v_w_in'], 'v_pool_w': out['v_pool_w'], 'v_pool_scale': out['v_pool_scale'], 'v_conv_w': out['v_conv_w'], 'v_rpb': out['v_rpb'], 'v_w_out': out['v_w_out'], 'v_ln_g': out['v_ln_g'], 'v_ln_b': out['v_ln_b']}


def _loss(weights, diff, rest, loss_target):
    with _jax.named_scope("forward"):
        args = {**rest, TWIN_DIFF_INPUT: diff, **{k: w.astype(_WEIGHT_DTYPES[k]) for k, w in weights.items()}}
        y = _forward(args)
    with _jax.named_scope("loss_head"):
        err = _jnp.square(y.astype(_jnp.float32) - loss_target)
        return 0.5 * _jnp.sum(_jnp.mean(err, axis=-1)) if err.ndim else 0.5 * err


def _adamw(w, g, m, v):
    m = ADAM_B1 * m + (1.0 - ADAM_B1) * g
    v = ADAM_B2 * v + (1.0 - ADAM_B2) * _jnp.square(g)
    m_hat = m / (1.0 - ADAM_B1 ** ADAM_STEP)
    v_hat = v / (1.0 - ADAM_B2 ** ADAM_STEP)
    delta = -ADAM_LR * (m_hat / (_jnp.sqrt(v_hat) + ADAM_EPS) + ADAM_WD * w)
    return delta, m, v


def reference(x, ffn1_w_gate, ffn1_w_up, ffn1_w_down, ffn2_w_gate, ffn2_w_up, ffn2_w_down, w_in, pool_w, pool_scale, conv_w, rpb, w_out, ln_g, ln_b, loss_target, m_ffn1_w_gate, m_ffn1_w_up, m_ffn1_w_down, m_ffn2_w_gate, m_ffn2_w_up, m_ffn2_w_down, m_w_in, m_pool_w, m_pool_scale, m_conv_w, m_rpb, m_w_out, m_ln_g, m_ln_b, v_ffn1_w_gate, v_ffn1_w_up, v_ffn1_w_down, v_ffn2_w_gate, v_ffn2_w_up, v_ffn2_w_down, v_w_in, v_pool_w, v_pool_scale, v_conv_w, v_rpb, v_w_out, v_ln_g, v_ln_b):
    given = dict(x=x, ffn1_w_gate=ffn1_w_gate, ffn1_w_up=ffn1_w_up, ffn1_w_down=ffn1_w_down, ffn2_w_gate=ffn2_w_gate, ffn2_w_up=ffn2_w_up, ffn2_w_down=ffn2_w_down, w_in=w_in, pool_w=pool_w, pool_scale=pool_scale, conv_w=conv_w, rpb=rpb, w_out=w_out, ln_g=ln_g, ln_b=ln_b, loss_target=loss_target, m_ffn1_w_gate=m_ffn1_w_gate, m_ffn1_w_up=m_ffn1_w_up, m_ffn1_w_down=m_ffn1_w_down, m_ffn2_w_gate=m_ffn2_w_gate, m_ffn2_w_up=m_ffn2_w_up, m_ffn2_w_down=m_ffn2_w_down, m_w_in=m_w_in, m_pool_w=m_pool_w, m_pool_scale=m_pool_scale, m_conv_w=m_conv_w, m_rpb=m_rpb, m_w_out=m_w_out, m_ln_g=m_ln_g, m_ln_b=m_ln_b, v_ffn1_w_gate=v_ffn1_w_gate, v_ffn1_w_up=v_ffn1_w_up, v_ffn1_w_down=v_ffn1_w_down, v_ffn2_w_gate=v_ffn2_w_gate, v_ffn2_w_up=v_ffn2_w_up, v_ffn2_w_down=v_ffn2_w_down, v_w_in=v_w_in, v_pool_w=v_pool_w, v_pool_scale=v_pool_scale, v_conv_w=v_conv_w, v_rpb=v_rpb, v_w_out=v_w_out, v_ln_g=v_ln_g, v_ln_b=v_ln_b)
    weights = {n: given[n] for n in TWIN_WEIGHTS}
    shared = {n: given[n] for n in SHARED_INPUTS}
    per_example = {n: given[n] for n in ['x']}
    grad_fn = _jax.value_and_grad(_loss, argnums=(0, 1))

    def one_microbatch(ex, loss_target):
        ex = dict(ex)
        diff = ex.pop(TWIN_DIFF_INPUT)
        return grad_fn(weights, diff, {**shared, **ex}, loss_target)

    if N_MICROBATCH == 1:
        loss, (grad_w, grad_x) = one_microbatch(per_example, given["loss_target"])
    else:
        def body(carry, xs):
            loss_sum, grad_sum = carry
            l_k, (gw_k, gx_k) = one_microbatch(xs[0], xs[1])
            with _jax.named_scope("update"):
                return (loss_sum + l_k, _jax.tree.map(_jnp.add, grad_sum, gw_k)), gx_k

        init = (_jnp.zeros((), _jnp.float32), _jax.tree.map(_jnp.zeros_like, weights))
        (loss, grad_w), grad_x = _jax.lax.scan(body, init, (per_example, given["loss_target"]))
    with _jax.named_scope("update"):
        delta_w, new_m, new_v = {}, {}, {}
        for n in TWIN_WEIGHTS:
            delta_w[n], new_m[n], new_v[n] = _adamw(weights[n], grad_w[n], given["m_" + n], given["v_" + n])
    return (loss, grad_x, *[grad_w[n] for n in TWIN_WEIGHTS], *[delta_w[n] for n in TWIN_WEIGHTS],
            *[new_m[n] for n in TWIN_WEIGHTS], *[new_v[n] for n in TWIN_WEIGHTS])
```

```python
import functools
import numpy as np
import jax
import jax.numpy as jnp
from jax import lax
from jax.experimental import pallas as pl
from jax.experimental.pallas import tpu as pltpu

BF = jnp.bfloat16
F32 = jnp.float32
MESH = pl.DeviceIdType.MESH

DEPTH = 4
ALPHA = (2.0 * DEPTH) ** 0.25
LN_EPS = 1e-5
NEG_INF = -1e30
GRID_W = 64
NA_ROWS = 8
NA_COLS = 16
NA_HEADS = 8
HEAD_DIM = 64
D_POOL = 256
D_CONV = 256
D_NA = 512
POOL_WINDOWS = (2, 4, 8, 16)
HALO = 8
ADAM_LR, ADAM_B1, ADAM_B2, ADAM_EPS, ADAM_WD, ADAM_STEP = 0.001, 0.9, 0.999, 1e-08, 0.01, 10
VMEM_LIMIT = 56 * 1024 * 1024
FLAT_W = 1024


def _cp(n_axes):
    return pltpu.CompilerParams(dimension_semantics=("arbitrary",) * n_axes, vmem_limit_bytes=VMEM_LIMIT)


def _chunks(n, step):
    return [(c0, min(step, n - c0)) for c0 in range(0, n, step)]


def _full(shape):
    nd = len(shape)
    return pl.BlockSpec(shape, lambda *_: (0,) * nd)


def _resident(shape):
    nd = len(shape)
    return pl.BlockSpec(shape, lambda *_: (0,) * nd, pipeline_mode=pl.Buffered(1))


def _nt(a, b):
    return lax.dot_general(a, b, (((1,), (1,)), ((), ())), preferred_element_type=F32)


def _tn(a, b):
    return lax.dot_general(a, b, (((0,), (0,)), ((), ())), preferred_element_type=F32)


def _nn(a, b):
    return jnp.dot(a, b, preferred_element_type=F32)


def _ln_fwd(z, g, b):
    mu = jnp.mean(z, axis=-1, keepdims=True)
    zc = z - mu
    var = jnp.mean(zc * zc, axis=-1, keepdims=True)
    return zc * lax.rsqrt(var + LN_EPS) * g + b


def _ln_bwd(dy, z, g):
    mu = jnp.mean(z, axis=-1, keepdims=True)
    zc = z - mu
    var = jnp.mean(zc * zc, axis=-1, keepdims=True)
    rstd = lax.rsqrt(var + LN_EPS)
    xhat = zc * rstd
    gdy = dy * g
    m1 = jnp.mean(gdy, axis=-1, keepdims=True)
    m2 = jnp.mean(gdy * xhat, axis=-1, keepdims=True)
    return rstd * (gdy - m1 - xhat * m2), xhat


def _acc_ln_grads(acc_ref, first, dy, xhat):
    @pl.when(first)
    def _():
        acc_ref[...] = jnp.zeros_like(acc_ref)
    acc_ref[0:1, :] += jnp.sum(dy * xhat, axis=0, keepdims=True)
    acc_ref[1:2, :] += jnp.sum(dy, axis=0, keepdims=True)


def _ffn_fwd(x, wg, wu, wd, lg, lb):
    S, D = x.shape
    Fd = wg.shape[1]
    tm = min(256, S)
    chunks = _chunks(Fd, 1024)

    def body(x_ref, wg_ref, wu_ref, wd_ref, lg_ref, lb_ref, xo_ref, xb_ref, z_ref, g_ref, u_ref):
        x = x_ref[...]
        xb = x.astype(BF)
        acc = jnp.zeros((tm, D), F32)
        for c0, cw in chunks:
            g = _nn(xb, wg_ref[:, c0:c0 + cw])
            u = _nn(xb, wu_ref[:, c0:c0 + cw])
            g_ref[:, c0:c0 + cw] = g.astype(BF)
            u_ref[:, c0:c0 + cw] = u.astype(BF)
            a = g * jax.nn.sigmoid(g) * u
            acc = acc + _nn(a.astype(BF), wd_ref[c0:c0 + cw, :])
        z = ALPHA * x + 0.5 * acc
        xo = _ln_fwd(z, lg_ref[...], lb_ref[...])
        z_ref[...] = z
        xo_ref[...] = xo
        xb_ref[...] = xo.astype(BF)

    row = lambda w: pl.BlockSpec((tm, w), lambda i: (i, 0))
    return pl.pallas_call(
        body, name="ffn_fwd", grid=(S // tm,),
        in_specs=[row(D), _resident((D, Fd)), _resident((D, Fd)), _resident((Fd, D)), _full((1, D)), _full((1, D))],
        out_specs=[row(D), row(D), row(D), row(Fd), row(Fd)],
        out_shape=[jax.ShapeDtypeStruct((S, D), F32), jax.ShapeDtypeStruct((S, D), BF),
                   jax.ShapeDtypeStruct((S, D), F32), jax.ShapeDtypeStruct((S, Fd), BF),
                   jax.ShapeDtypeStruct((S, Fd), BF)],
        compiler_params=_cp(1),
    )(x, wg, wu, wd, lg, lb)


def _ffn_bwd(dxo, z, g, u, wg, wu, wd, lg):
    S, D = dxo.shape
    Fd = wg.shape[1]
    tm = min(256, S)
    chunks = _chunks(Fd, 1024)

    def body(dxo_ref, z_ref, g_ref, u_ref, wg_ref, wu_ref, wd_ref, lg_ref,
             dx_ref, df_ref, dg_ref, du_ref, a_ref, ln_ref):
        dy = dxo_ref[...]
        dz, xhat = _ln_bwd(dy, z_ref[...], lg_ref[...])
        _acc_ln_grads(ln_ref, pl.program_id(0) == 0, dy, xhat)
        dfb = (0.5 * dz).astype(BF)
        df_ref[...] = dfb
        acc = ALPHA * dz
        for c0, cw in chunks:
            da = _nt(dfb, wd_ref[c0:c0 + cw, :])
            gg = g_ref[:, c0:c0 + cw].astype(F32)
            uu = u_ref[:, c0:c0 + cw].astype(F32)
            sg = jax.nn.sigmoid(gg)
            silu = gg * sg
            a_ref[:, c0:c0 + cw] = (silu * uu).astype(BF)
            dgb = (da * uu * (sg * (1.0 + gg * (1.0 - sg)))).astype(BF)
            dub = (da * silu).astype(BF)
            dg_ref[:, c0:c0 + cw] = dgb
            du_ref[:, c0:c0 + cw] = dub
            acc = acc + _nt(dgb, wg_ref[:, c0:c0 + cw]) + _nt(dub, wu_ref[:, c0:c0 + cw])
        dx_ref[...] = acc

    row = lambda w: pl.BlockSpec((tm, w), lambda i: (i, 0))
    return pl.pallas_call(
        body, name="ffn_bwd", grid=(S // tm,),
        in_specs=[row(D), row(D), row(Fd), row(Fd), _resident((D, Fd)), _resident((D, Fd)), _resident((Fd, D)), _full((1, D))],
        out_specs=[row(D), row(D), row(Fd), row(Fd), row(Fd), _full((8, D))],
        out_shape=[jax.ShapeDtypeStruct((S, D), F32), jax.ShapeDtypeStruct((S, D), BF),
                   jax.ShapeDtypeStruct((S, Fd), BF), jax.ShapeDtypeStruct((S, Fd), BF),
                   jax.ShapeDtypeStruct((S, Fd), BF), jax.ShapeDtypeStruct((8, D), F32)],
        compiler_params=_cp(1),
    )(dxo, z, g, u, wg, wu, wd, lg)


def _wgrad(a, b, tn):
    S, K = a.shape
    N = b.shape[1]
    ts = min(512, S)

    def body(a_ref, b_ref, o_ref):
        @pl.when(pl.program_id(1) == 0)
        def _():
            o_ref[...] = jnp.zeros_like(o_ref)
        o_ref[...] += _tn(a_ref[...], b_ref[...])

    return pl.pallas_call(
        body, name=f"wgrad_{K}x{N}", grid=(N // tn, S // ts),
        in_specs=[pl.BlockSpec((ts, K), lambda j, s: (s, 0)), pl.BlockSpec((ts, tn), lambda j, s: (s, j))],
        out_specs=pl.BlockSpec((K, tn), lambda j, s: (0, j)),
        out_shape=jax.ShapeDtypeStruct((K, N), F32),
        compiler_params=_cp(2),
    )(a, b)


def _proj(xb, win):
    S, D = xb.shape
    N = win.shape[1]
    n1 = D_POOL + 3 * D_CONV
    tm = min(512, S)

    def body(x_ref, w_ref, p_ref, qkv_ref):
        x = x_ref[...]
        p_ref[...] = _nn(x, w_ref[:, 0:n1])
        qkv_ref[...] = _nn(x, w_ref[:, n1:N]).astype(BF)

    row = lambda w: pl.BlockSpec((tm, w), lambda i: (i, 0))
    return pl.pallas_call(
        body, name="mix_proj", grid=(S // tm,),
        in_specs=[row(D), _full((D, N))],
        out_specs=[row(n1), row(N - n1)],
        out_shape=[jax.ShapeDtypeStruct((S, n1), F32), jax.ShapeDtypeStruct((S, N - n1), BF)],
        compiler_params=_cp(1),
    )(xb, win)


def _mm_exact(a, b, name):
    def body(a_ref, b_ref, o_ref):
        o_ref[...] = jnp.dot(a_ref[...], b_ref[...], preferred_element_type=F32, precision=lax.Precision.HIGHEST)

    return pl.pallas_call(
        body, name=name, in_specs=[_full(a.shape), _full(b.shape)], out_specs=_full((a.shape[0], b.shape[1])),
        out_shape=jax.ShapeDtypeStruct((a.shape[0], b.shape[1]), F32),
        compiler_params=pltpu.CompilerParams(vmem_limit_bytes=VMEM_LIMIT),
    )(a, b)


def _bias_constants():
    c = np.arange(GRID_W)
    col_start = np.clip(c - NA_COLS // 2, 0, GRID_W - NA_COLS)
    valid = (c[None, :] >= col_start[:, None]) & (c[None, :] < col_start[:, None] + NA_COLS)
    dc = np.clip(c[None, :] - c[:, None], -(NA_COLS - 1), NA_COLS - 1) + (NA_COLS - 1)
    onehot = np.zeros((32, GRID_W * GRID_W), np.float32)
    onehot[dc.reshape(-1), np.arange(GRID_W * GRID_W)] = 1.0
    mask_kq = np.where(valid.T, 0.0, NEG_INF).astype(np.float32)
    mask = np.tile(mask_kq, (2 * NA_ROWS - 1, 2))
    return onehot, mask


def _bias_table(rpb):
    onehot, mask = _bias_constants()
    nr = 2 * NA_ROWS - 1
    r2 = jnp.pad(rpb.reshape(NA_HEADS * nr, 2 * NA_COLS - 1), ((0, 0), (0, 1)))
    t = _mm_exact(r2, jnp.asarray(onehot), "bias_expand")
    t = t.reshape(NA_HEADS // 2, 2, nr, GRID_W, GRID_W).transpose(0, 2, 4, 1, 3)
    return t.reshape(NA_HEADS // 2, nr * GRID_W, 2 * GRID_W) + jnp.asarray(mask)[None]


def _bias_grad(dt):
    onehot, _ = _bias_constants()
    nr = 2 * NA_ROWS - 1
    d = dt.reshape(NA_HEADS // 2, nr, GRID_W, 2, GRID_W).transpose(0, 3, 1, 4, 2).reshape(NA_HEADS * nr, -1)
    g = _mm_exact(d, jnp.asarray(onehot.T.copy()), "bias_reduce")
    return g[:, :2 * NA_COLS - 1].reshape(NA_HEADS, nr, 2 * NA_COLS - 1)


def _attn_rows(S):
    rows = S // GRID_W
    rb = min(16, rows)
    return rows, rb


def _attn_step(r, rows, q, k_ref, v_ref, b_ref, m_a):
    rs = jnp.clip(r - NA_ROWS // 2, 0, rows - NA_ROWS)
    s0 = rs - r + (NA_ROWS - 1)
    zero = jnp.zeros_like(q)
    q2 = jnp.concatenate([jnp.where(m_a, q, zero), jnp.where(m_a, zero, q)], axis=0)
    ks = pl.ds(pl.multiple_of(rs * GRID_W, GRID_W), NA_ROWS * GRID_W)
    kb = k_ref[ks, :]
    vb = v_ref[ks, :]
    bs = pl.ds(pl.multiple_of(s0 * GRID_W, GRID_W), NA_ROWS * GRID_W)
    s = _nt(kb, q2) * (HEAD_DIM ** -0.5) + b_ref[0, bs, :]
    m = jnp.max(s, axis=0, keepdims=True)
    p = jnp.exp(s - m)
    p = p / jnp.sum(p, axis=0, keepdims=True)
    return p, q2, kb, vb, ks, bs


def _attn_fwd(qkv, bias):
    S = qkv.shape[0]
    rows, rb = _attn_rows(S)
    tq = rb * GRID_W
    hp = NA_HEADS // 2

    def body(q_ref, k_ref, v_ref, b_ref, o_ref):
        base = pl.program_id(1) * rb
        m_a = lax.broadcasted_iota(jnp.int32, (GRID_W, 128), 1) < HEAD_DIM

        def step(i, carry):
            qs = pl.ds(pl.multiple_of(i * GRID_W, GRID_W), GRID_W)
            p, _, _, vb, _, _ = _attn_step(base + i, rows, q_ref[qs, :], k_ref, v_ref, b_ref, m_a)
            o2 = _tn(p.astype(BF), vb)
            o_ref[qs, :] = jnp.where(m_a, o2[:GRID_W], o2[GRID_W:]).astype(BF)
            return carry

        lax.fori_loop(0, rb, step, 0)

    return pl.pallas_call(
        body, name="attn_fwd", grid=(hp, rows // rb),
        in_specs=[pl.BlockSpec((tq, 128), lambda h, r: (r, h)),
                  pl.BlockSpec((S, 128), lambda h, r: (0, hp + h)),
                  pl.BlockSpec((S, 128), lambda h, r: (0, 2 * hp + h)),
                  pl.BlockSpec((1, bias.shape[1], 128), lambda h, r: (h, 0, 0))],
        out_specs=pl.BlockSpec((tq, 128), lambda h, r: (r, h)),
        out_shape=jax.ShapeDtypeStruct((S, D_NA), BF),
        compiler_params=_cp(2),
    )(qkv, qkv, qkv, bias)


def _attn_bwd(qkv, bias, dycat):
    S = qkv.shape[0]
    rows, rb = _attn_rows(S)
    tq = rb * GRID_W
    hp = NA_HEADS // 2
    scale = HEAD_DIM ** -0.5

    def body(q_ref, k_ref, v_ref, b_ref, do_ref, dq_ref, dk_ref, dv_ref, db_ref):
        base = pl.program_id(1) * rb
        m_a = lax.broadcasted_iota(jnp.int32, (GRID_W, 128), 1) < HEAD_DIM

        @pl.when(pl.program_id(1) == 0)
        def _():
            dk_ref[...] = jnp.zeros_like(dk_ref)
            dv_ref[...] = jnp.zeros_like(dv_ref)
            db_ref[...] = jnp.zeros_like(db_ref)

        def step(i, carry):
            qs = pl.ds(pl.multiple_of(i * GRID_W, GRID_W), GRID_W)
            p, q2, kb, vb, ks, bs = _attn_step(base + i, rows, q_ref[qs, :], k_ref, v_ref, b_ref, m_a)
            do = do_ref[qs, :].astype(BF)
            zero = jnp.zeros_like(do)
            do2 = jnp.concatenate([jnp.where(m_a, do, zero), jnp.where(m_a, zero, do)], axis=0)
            dp = _nt(vb, do2)
            ds = p * (dp - jnp.sum(p * dp, axis=0, keepdims=True))
            db_ref[0, bs, :] += ds
            dsb = ds.astype(BF)
            dq2 = _tn(dsb, kb) * scale
            dq_ref[qs, :] = jnp.where(m_a, dq2[:GRID_W], dq2[GRID_W:]).astype(BF)
            dk_ref[ks, :] += _nn(dsb, q2) * scale
            dv_ref[ks, :] += _nn(p.astype(BF), do2)
            return carry

        lax.fori_loop(0, rb, step, 0)

    nb = bias.shape[1]
    return pl.pallas_call(
        body, name="attn_bwd", grid=(hp, rows // rb),
        in_specs=[pl.BlockSpec((tq, 128), lambda h, r: (r, h)),
                  pl.BlockSpec((S, 128), lambda h, r: (0, hp + h)),
                  pl.BlockSpec((S, 128), lambda h, r: (0, 2 * hp + h)),
                  pl.BlockSpec((1, nb, 128), lambda h, r: (h, 0, 0)),
                  pl.BlockSpec((tq, 128), lambda h, r: (r, hp + h))],
        out_specs=[pl.BlockSpec((tq, 128), lambda h, r: (r, h)),
                   pl.BlockSpec((S, 128), lambda h, r: (0, h)),
                   pl.BlockSpec((S, 128), lambda h, r: (0, h)),
                   pl.BlockSpec((1, nb, 128), lambda h, r: (h, 0, 0))],
        out_shape=[jax.ShapeDtypeStruct((S, D_NA), BF), jax.ShapeDtypeStruct((S, D_NA), F32),
                   jax.ShapeDtypeStruct((S, D_NA), F32), jax.ShapeDtypeStruct((hp, nb, 128), F32)],
        compiler_params=_cp(2),
    )(qkv, qkv, qkv, bias, dycat)


def _halo_specs(tm, width, S):
    hb = tm // HALO
    last = S // HALO - 1
    return [pl.BlockSpec((tm, width), lambda i: (i, 0)),
            pl.BlockSpec((HALO, width), lambda i: (jnp.maximum(i * hb - 1, 0), 0)),
            pl.BlockSpec((HALO, width), lambda i: (jnp.minimum((i + 1) * hb, last), 0))]


def _with_halo(cur_ref, prev_ref, next_ref, i, nt):
    prev = jnp.where(i > 0, prev_ref[...], 0.0)
    nxt = jnp.where(i < nt - 1, next_ref[...], 0.0)
    return jnp.concatenate([prev, cur_ref[...], nxt], axis=0)


def _shift(a, k):
    n = a.shape[0]
    return pltpu.roll(a, k % n, 0)


def _pool_lanes(n):
    lane = lax.broadcasted_iota(jnp.int32, (n, D_POOL), 1)
    group = D_POOL // len(POOL_WINDOWS)
    return [lane < group * (j + 1) for j in range(len(POOL_WINDOWS) - 1)]


def _by_window(lanes, vals):
    return jnp.where(lanes[0], vals[0], jnp.where(lanes[1], vals[1], jnp.where(lanes[2], vals[2], vals[3])))


def _pool_count(lanes, t, S):
    back = _by_window(lanes, tuple(w // 2 for w in POOL_WINDOWS))
    lo = jnp.maximum(t - back, 0)
    hi = jnp.minimum(t + back, S)
    return jnp.maximum(hi - lo, 1).astype(F32)


def _pool_p(u, lanes, cnt):
    a = u + _shift(u, 1)
    b = _shift(a, 1) + _shift(a, -1)
    c = _shift(b, 2) + _shift(b, -2)
    d = _shift(c, 4) + _shift(c, -4)
    return _by_window(lanes, (a, b, c, d)) / cnt - u


def _mixab_fwd(pabc, wblk, vec):
    S = pabc.shape[0]
    tm = min(512, S)
    nt = S // tm
    n = tm + 2 * HALO
    tile = slice(HALO, HALO + tm)

    def body(cur_ref, prev_ref, next_ref, w_ref, vec_ref, o_ref):
        i = pl.program_id(0)
        ext = _with_halo(cur_ref, prev_ref, next_ref, i, nt)
        lanes = _pool_lanes(n)
        t = i * tm - HALO + lax.broadcasted_iota(jnp.int32, (n, D_POOL), 0)
        p = _pool_p(ext[:, 0:D_POOL], lanes, _pool_count(lanes, t, S))[tile]
        o_ref[:, 0:D_POOL] = (_nn(p.astype(BF), w_ref[...]) * vec_ref[0:1, :]).astype(BF)
        zc = ext[:, 512:768] * ext[:, 768:1024]
        conv = vec_ref[1:2, :] * _shift(zc, 1) + vec_ref[2:3, :] * zc + vec_ref[3:4, :] * _shift(zc, -1)
        o_ref[:, D_POOL:D_POOL + D_CONV] = (ext[tile, 256:512] * conv[tile]).astype(BF)

    return pl.pallas_call(
        body, name="mixab_fwd", grid=(nt,),
        in_specs=_halo_specs(tm, 1024, S) + [_full((D_POOL, D_POOL)), _full((8, D_POOL))],
        out_specs=pl.BlockSpec((tm, D_POOL + D_CONV), lambda i: (i, 0)),
        out_shape=jax.ShapeDtypeStruct((S, D_POOL + D_CONV), BF),
        compiler_params=_cp(1),
    )(pabc, pabc, pabc, wblk, vec)


def _mixab_bwd(pabc, dycat, wblk, vec):
    S = pabc.shape[0]
    tm = min(512, S)
    nt = S // tm
    n = tm + 2 * HALO
    tile = slice(HALO, HALO + tm)

    def body(cur_ref, prev_ref, next_ref, dcur_ref, dprev_ref, dnext_ref, w_ref, vec_ref, o_ref, dw_ref, dvec_ref):
        i = pl.program_id(0)

        @pl.when(i == 0)
        def _():
            dw_ref[...] = jnp.zeros_like(dw_ref)
            dvec_ref[...] = jnp.zeros_like(dvec_ref)

        ext = _with_halo(cur_ref, prev_ref, next_ref, i, nt)
        dext = _with_halo(dcur_ref, dprev_ref, dnext_ref, i, nt)
        lanes = _pool_lanes(n)
        t = i * tm - HALO + lax.broadcasted_iota(jnp.int32, (n, D_POOL), 0)
        cnt = _pool_count(lanes, t, S)
        w = w_ref[...]
        scale = vec_ref[0:1, :]
        pb = _pool_p(ext[:, 0:D_POOL], lanes, cnt)[tile].astype(BF)
        dya = dext[:, 0:D_POOL]
        dvec_ref[0:1, :] += jnp.sum(dya[tile] * _nn(pb, w), axis=0, keepdims=True)
        dqb = (dya * scale).astype(BF)
        dw_ref[...] += _tn(pb, dqb[tile])
        dp = _nt(dqb, w)
        r = dp / cnt
        a = r + _shift(r, -1)
        b = _shift(a, 1) + _shift(a, -1)
        c = _shift(b, 2) + _shift(b, -2)
        d = _shift(c, 4) + _shift(c, -4)
        o_ref[:, 0:256] = (_by_window(lanes, (a, b, c, d)) - dp)[tile].astype(BF)
        gb, gc, hh = ext[:, 256:512], ext[:, 512:768], ext[:, 768:1024]
        zc = gc * hh
        zm, zp = _shift(zc, 1), _shift(zc, -1)
        w0, w1, w2 = vec_ref[1:2, :], vec_ref[2:3, :], vec_ref[3:4, :]
        dyb = dext[:, D_POOL:D_POOL + D_CONV]
        dconv = dyb * gb
        o_ref[:, 256:512] = (dyb * (w0 * zm + w1 * zc + w2 * zp))[tile].astype(BF)
        dzc = w0 * _shift(dconv, -1) + w1 * dconv + w2 * _shift(dconv, 1)
        o_ref[:, 512:768] = (dzc * hh)[tile].astype(BF)
        o_ref[:, 768:1024] = (dzc * gc)[tile].astype(BF)
        dct = dconv[tile]
        dvec_ref[1:2, :] += jnp.sum(dct * zm[tile], axis=0, keepdims=True)
        dvec_ref[2:3, :] += jnp.sum(dct * zc[tile], axis=0, keepdims=True)
        dvec_ref[3:4, :] += jnp.sum(dct * zp[tile], axis=0, keepdims=True)

    return pl.pallas_call(
        body, name="mixab_bwd", grid=(nt,),
        in_specs=_halo_specs(tm, 1024, S) + _halo_specs(tm, 512, S) + [_full((D_POOL, D_POOL)), _full((8, D_POOL))],
        out_specs=[pl.BlockSpec((tm, 1024), lambda i: (i, 0)), _full((D_POOL, D_POOL)), _full((8, D_POOL))],
        out_shape=[jax.ShapeDtypeStruct((S, 1024), BF), jax.ShapeDtypeStruct((D_POOL, D_POOL), F32),
                   jax.ShapeDtypeStruct((8, D_POOL), F32)],
        compiler_params=_cp(1),
    )(pabc, pabc, pabc, dycat, dycat, dycat, wblk, vec)


def _mixout_fwd(yab, yc, x, wout, lg, lb):
    S, D = x.shape
    tm = min(512, S)
    h = yab.shape[1]

    def body(yab_ref, yc_ref, x_ref, w_ref, lg_ref, lb_ref, xo_ref, xb_ref, z_ref):
        y = _nn(yab_ref[...], w_ref[0:h, :]) + _nn(yc_ref[...], w_ref[h:2 * h, :])
        z = ALPHA * x_ref[...] + y
        xo = _ln_fwd(z, lg_ref[...], lb_ref[...])
        z_ref[...] = z
        xo_ref[...] = xo
        xb_ref[...] = xo.astype(BF)

    row = lambda w: pl.BlockSpec((tm, w), lambda i: (i, 0))
    return pl.pallas_call(
        body, name="mixout_fwd", grid=(S // tm,),
        in_specs=[row(h), row(h), row(D), _full((2 * h, D)), _full((1, D)), _full((1, D))],
        out_specs=[row(D), row(D), row(D)],
        out_shape=[jax.ShapeDtypeStruct((S, D), F32), jax.ShapeDtypeStruct((S, D), BF),
                   jax.ShapeDtypeStruct((S, D), F32)],
        compiler_params=_cp(1),
    )(yab, yc, x, wout, lg, lb)


def _mixout_bwd(dxo, z, wout, lg):
    S, D = dxo.shape
    M = wout.shape[0]
    tm = min(512, S)

    def body(dxo_ref, z_ref, w_ref, lg_ref, dres_ref, dzb_ref, dy_ref, ln_ref):
        dy = dxo_ref[...]
        dz, xhat = _ln_bwd(dy, z_ref[...], lg_ref[...])
        _acc_ln_grads(ln_ref, pl.program_id(0) == 0, dy, xhat)
        dzb = dz.astype(BF)
        dres_ref[...] = ALPHA * dz
        dzb_ref[...] = dzb
        dy_ref[...] = _nt(dzb, w_ref[...])

    row = lambda w: pl.BlockSpec((tm, w), lambda i: (i, 0))
    return pl.pallas_call(
        body, name="mixout_bwd", grid=(S // tm,),
        in_specs=[row(D), row(D), _full((M, D)), _full((1, D))],
        out_specs=[row(D), row(D), row(M), _full((8, D))],
        out_shape=[jax.ShapeDtypeStruct((S, D), F32), jax.ShapeDtypeStruct((S, D), BF),
                   jax.ShapeDtypeStruct((S, M), F32), jax.ShapeDtypeStruct((8, D), F32)],
        compiler_params=_cp(1),
    )(dxo, z, wout, lg)


def _proj_bwd(dres, dproj, win):
    S, D = dres.shape
    N = win.shape[1]
    tm = min(512, S)

    def body(dres_ref, dp_ref, w_ref, dx_ref):
        dx_ref[...] = dres_ref[...] + _nt(dp_ref[...], w_ref[...])

    row = lambda w: pl.BlockSpec((tm, w), lambda i: (i, 0))
    return pl.pallas_call(
        body, name="mix_proj_bwd", grid=(S // tm,),
        in_specs=[row(D), row(N), _full((D, N))],
        out_specs=row(D),
        out_shape=jax.ShapeDtypeStruct((S, D), F32),
        compiler_params=_cp(1),
    )(dres, dproj, win)


def _loss_head(y, target):
    S, D = y.shape
    tm = min(512, S)

    def body(y_ref, t_ref, l_ref, dy_ref):
        @pl.when(pl.program_id(0) == 0)
        def _():
            l_ref[...] = jnp.zeros_like(l_ref)
        e = y_ref[...] - t_ref[...]
        dy_ref[...] = e * (1.0 / D)
        part = jnp.sum(jnp.sum(e * e, axis=1, keepdims=True) * (1.0 / D), axis=0, keepdims=True)
        l_ref[...] += 0.5 * part

    row = pl.BlockSpec((tm, D), lambda i: (i, 0))
    return pl.pallas_call(
        body, name="loss_head", grid=(S // tm,),
        in_specs=[row, row], out_specs=[_full((8, 128)), row],
        out_shape=[jax.ShapeDtypeStruct((8, 128), F32), jax.ShapeDtypeStruct((S, D), F32)],
        compiler_params=_cp(1),
    )(y, target)


def _adamw(w, g, m, v):
    shape = w.shape
    cols = shape[-1]
    rows = int(np.prod(shape[:-1]))
    w2, g2, m2, v2 = (a.reshape(rows, cols) for a in (w, g, m, v))
    tr = rows
    for cand in (512, 352, 256):
        if rows > cand and rows % cand == 0:
            tr = cand
            break

    def body(w_ref, g_ref, m_ref, v_ref, d_ref, mo_ref, vo_ref):
        g = g_ref[...]
        mn = ADAM_B1 * m_ref[...] + (1.0 - ADAM_B1) * g
        vn = ADAM_B2 * v_ref[...] + (1.0 - ADAM_B2) * (g * g)
        m_hat = mn / (1.0 - ADAM_B1 ** ADAM_STEP)
        v_hat = vn / (1.0 - ADAM_B2 ** ADAM_STEP)
        d_ref[...] = -ADAM_LR * (m_hat / (jnp.sqrt(v_hat) + ADAM_EPS) + ADAM_WD * w_ref[...])
        mo_ref[...] = mn
        vo_ref[...] = vn

    spec = pl.BlockSpec((tr, cols), lambda i: (i, 0))
    outs = pl.pallas_call(
        body, name=f"adamw_{rows}x{cols}", grid=(rows // tr,),
        in_specs=[spec] * 4, out_specs=[spec] * 3,
        out_shape=[jax.ShapeDtypeStruct((rows, cols), F32)] * 3,
        compiler_params=_cp(1),
    )(w2, g2, m2, v2)
    return tuple(o.reshape(shape) for o in outs)


def _add_chip(keep, recv):
    q, R, W = keep.shape
    tr = 512

    def body(a_ref, b_ref, o_ref, ob_ref):
        s = a_ref[...] + b_ref[...].astype(F32)
        o_ref[...] = s
        ob_ref[...] = s.astype(BF)

    spec = pl.BlockSpec((1, tr, W), lambda j, i: (j, i, 0))
    return pl.pallas_call(
        body, name="rs_add_chip", grid=(q, R // tr), in_specs=[spec, spec], out_specs=[spec, spec],
        out_shape=[jax.ShapeDtypeStruct((q, R, W), F32), jax.ShapeDtypeStruct((q, R, W), BF)],
        compiler_params=_cp(2),
    )(keep, recv)


def _add_final(qidx, chip, recv):
    _, R, W = chip.shape
    tr = 512

    def body(q_ref, a_ref, b_ref, o_ref):
        s = a_ref[0]
        for j in range(3):
            s = s + b_ref[j].astype(F32)
        o_ref[...] = s

    grid_spec = pltpu.PrefetchScalarGridSpec(
        num_scalar_prefetch=1, grid=(R // tr,),
        in_specs=[pl.BlockSpec((1, tr, W), lambda i, q: (q[0], i, 0)), pl.BlockSpec((3, tr, W), lambda i, q: (0, i, 0))],
        out_specs=pl.BlockSpec((tr, W), lambda i, q: (i, 0)))
    return pl.pallas_call(
        body, name="rs_add_final", grid_spec=grid_spec, out_shape=jax.ShapeDtypeStruct((R, W), F32),
        compiler_params=_cp(1),
    )(qidx, chip, recv)


def _place():
    x, y, c = lax.axis_index("x"), lax.axis_index("y"), lax.axis_index("c")
    chips = [(1 - x, y), (x, 1 - y), (1 - x, 1 - y)]
    return x, y, c, chips


def _gather_weights(wflat, small):
    R, W = wflat.shape
    half = R // 2
    sr = small.shape[0]

    def body(w_ref, s_ref, o_ref, so_ref, send_sems, recv_sems, local_sem):
        x, y, c, chips = _place()
        q = 2 * x + y
        sibling = (x, y, 1 - c)
        mine = pl.ds(c * half, half)
        theirs = pl.ds((1 - c) * half, half)

        def big(k, quarter, rows, to, src=None):
            dst = o_ref.at[quarter, rows, :]
            return pltpu.make_async_remote_copy(
                src_ref=dst if src is None else src, dst_ref=dst, send_sem=send_sems.at[k], recv_sem=recv_sems.at[k],
                device_id=to, device_id_type=MESH)

        def sml(k, quarter, to):
            return pltpu.make_async_remote_copy(
                src_ref=s_ref, dst_ref=so_ref.at[quarter], send_sem=send_sems.at[k], recv_sem=recv_sems.at[k],
                device_id=to, device_id_type=MESH)

        own = pltpu.make_async_copy(w_ref, o_ref.at[q], local_sem)
        own.start()
        so_ref[q] = s_ref[...]
        sends = [big(j, q, mine, (*chip, c), src=w_ref.at[mine, :]) for j, chip in enumerate(chips)]
        sends += [sml(6 + j, q, (*chip, c)) for j, chip in enumerate(chips)]
        for cp in sends:
            cp.start()
        passed = []
        for j, (cx, cy) in enumerate(chips):
            big(j, 2 * cx + cy, mine, (x, y, c)).wait_recv()
            fwd = big(3 + j, 2 * cx + cy, mine, sibling)
            fwd.start()
            passed.append(fwd)
        for j, (cx, cy) in enumerate(chips):
            big(3 + j, 2 * cx + cy, theirs, (x, y, c)).wait_recv()
            sml(6 + j, 2 * cx + cy, (x, y, c)).wait_recv()
        for cp in sends + passed:
            cp.wait_send()
        own.wait()

    return pl.pallas_call(
        body, name="gather_weights",
        in_specs=[pl.BlockSpec(memory_space=pl.ANY), pl.BlockSpec(memory_space=pltpu.VMEM)],
        out_specs=[pl.BlockSpec(memory_space=pl.ANY), pl.BlockSpec(memory_space=pltpu.VMEM)],
        out_shape=[jax.ShapeDtypeStruct((4, R, W), BF), jax.ShapeDtypeStruct((4, sr, 128), F32)],
        scratch_shapes=[pltpu.SemaphoreType.DMA((9,)), pltpu.SemaphoreType.DMA((9,)), pltpu.SemaphoreType.DMA],
        compiler_params=pltpu.CompilerParams(has_side_effects=True),
    )(wflat, small)


def _swap_sibling(send):
    def body(s_ref, o_ref, send_sem, recv_sem):
        x, y, c, _ = _place()
        cp = pltpu.make_async_remote_copy(src_ref=s_ref, dst_ref=o_ref, send_sem=send_sem, recv_sem=recv_sem,
                                          device_id=(x, y, 1 - c), device_id_type=MESH)
        cp.start()
        cp.wait()

    return pl.pallas_call(
        body, name="rs_swap_sibling",
        in_specs=[pl.BlockSpec(memory_space=pl.ANY)], out_specs=pl.BlockSpec(memory_space=pl.ANY),
        out_shape=jax.ShapeDtypeStruct(send.shape, send.dtype),
        scratch_shapes=[pltpu.SemaphoreType.DMA, pltpu.SemaphoreType.DMA],
        compiler_params=pltpu.CompilerParams(has_side_effects=True),
    )(send)


def _scatter_chips(chip_b):
    _, R, W = chip_b.shape

    def body(s_ref, o_ref, send_sems, recv_sems):
        x, y, c, chips = _place()
        cps = [pltpu.make_async_remote_copy(
            src_ref=s_ref.at[2 * cx + cy], dst_ref=o_ref.at[j], send_sem=send_sems.at[j], recv_sem=recv_sems.at[j],
            device_id=(cx, cy, c), device_id_type=MESH) for j, (cx, cy) in enumerate(chips)]
        for cp in cps:
            cp.start()
        for cp in cps:
            cp.wait()

    return pl.pallas_call(
        body, name="rs_scatter_chips",
        in_specs=[pl.BlockSpec(memory_space=pl.ANY)], out_specs=pl.BlockSpec(memory_space=pl.ANY),
        out_shape=jax.ShapeDtypeStruct((3, R, W), BF),
        scratch_shapes=[pltpu.SemaphoreType.DMA((3,)), pltpu.SemaphoreType.DMA((3,))],
        compiler_params=pltpu.CompilerParams(has_side_effects=True),
    )(chip_b)


def _join_halves(mine):
    R, W = mine.shape

    def body(s_ref, o_ref, send_sem, recv_sem, local_sem):
        x, y, c, _ = _place()
        own = pltpu.make_async_copy(s_ref, o_ref.at[c], local_sem)
        own.start()
        cp = pltpu.make_async_remote_copy(src_ref=s_ref, dst_ref=o_ref.at[c], send_sem=send_sem, recv_sem=recv_sem,
                                          device_id=(x, y, 1 - c), device_id_type=MESH)
        cp.start()
        pltpu.make_async_remote_copy(src_ref=s_ref, dst_ref=o_ref.at[1 - c], send_sem=send_sem, recv_sem=recv_sem,
                                     device_id=(x, y, 1 - c), device_id_type=MESH).wait_recv()
        cp.wait_send()
        own.wait()

    return pl.pallas_call(
        body, name="rs_join_halves",
        in_specs=[pl.BlockSpec(memory_space=pl.ANY)], out_specs=pl.BlockSpec(memory_space=pl.ANY),
        out_shape=jax.ShapeDtypeStruct((2, R, W), F32),
        scratch_shapes=[pltpu.SemaphoreType.DMA, pltpu.SemaphoreType.DMA, pltpu.SemaphoreType.DMA],
        compiler_params=pltpu.CompilerParams(has_side_effects=True),
    )(mine)


def _allreduce_small(v):
    r, W = v.shape

    def body(v_ref, o_ref, land_ref, send_sems, recv_sems):
        x, y, c, _ = _place()
        me = 4 * x + 2 * y + c
        cps = []
        for m in range(1, 8):
            to = (x ^ (m >> 2), y ^ ((m >> 1) & 1), c ^ (m & 1))
            cps.append(pltpu.make_async_remote_copy(
                src_ref=v_ref, dst_ref=land_ref.at[m - 1], send_sem=send_sems.at[m - 1], recv_sem=recv_sems.at[m - 1],
                device_id=to, device_id_type=MESH))
        for cp in cps:
            cp.start()
        for cp in cps:
            cp.wait()
        total = jnp.zeros((r, W), F32)
        for d in range(8):
            slot = jnp.maximum((me ^ d) - 1, 0)
            total = total + jnp.where(me == d, v_ref[...], land_ref[slot])
        o_ref[...] = total

    return pl.pallas_call(
        body, name="allreduce_small",
        in_specs=[pl.BlockSpec(memory_space=pltpu.VMEM)], out_specs=pl.BlockSpec(memory_space=pltpu.VMEM),
        out_shape=jax.ShapeDtypeStruct((r, W), F32),
        scratch_shapes=[pltpu.VMEM((7, r, W), F32), pltpu.SemaphoreType.DMA((7,)), pltpu.SemaphoreType.DMA((7,))],
        compiler_params=pltpu.CompilerParams(has_side_effects=True, vmem_limit_bytes=VMEM_LIMIT),
    )(v)


BIG = ("ffn1_w_gate", "ffn1_w_up", "ffn1_w_down", "ffn2_w_gate", "ffn2_w_up", "ffn2_w_down", "w_in", "w_out")
COL_SHARDED = ("ffn1_w_gate", "ffn1_w_up", "ffn2_w_gate", "ffn2_w_up", "w_in")


def _flat_rows(shard_shape):
    return shard_shape[-2] * shard_shape[-1] // FLAT_W


def _to_quarters(name, full):
    K, N = full.shape
    if name in COL_SHARDED:
        full = full.reshape(K, 4, N // 4).transpose(1, 0, 2)
    return full.reshape(4, K * N // 4 // FLAT_W, FLAT_W)


def _from_quarters(name, quarters, K, N):
    if name in COL_SHARDED:
        return quarters.reshape(4, K, N // 4).transpose(1, 0, 2).reshape(K, N)
    return quarters.reshape(K, N)


def kernel(x, ffn1_w_gate, ffn1_w_up, ffn1_w_down, ffn2_w_gate, ffn2_w_up, ffn2_w_down, w_in, pool_w, pool_scale, conv_w, rpb, w_out, ln_g, ln_b, loss_target, m_ffn1_w_gate, m_ffn1_w_up, m_ffn1_w_down, m_ffn2_w_gate, m_ffn2_w_up, m_ffn2_w_down, m_w_in, m_pool_w, m_pool_scale, m_conv_w, m_rpb, m_w_out, m_ln_g, m_ln_b, v_ffn1_w_gate, v_ffn1_w_up, v_ffn1_w_down, v_ffn2_w_gate, v_ffn2_w_up, v_ffn2_w_down, v_w_in, v_pool_w, v_pool_scale, v_conv_w, v_rpb, v_w_out, v_ln_g, v_ln_b):
    weights = dict(ffn1_w_gate=ffn1_w_gate, ffn1_w_up=ffn1_w_up, ffn1_w_down=ffn1_w_down, ffn2_w_gate=ffn2_w_gate,
                   ffn2_w_up=ffn2_w_up, ffn2_w_down=ffn2_w_down, w_in=w_in, pool_w=pool_w, pool_scale=pool_scale,
                   conv_w=conv_w, rpb=rpb, w_out=w_out, ln_g=ln_g, ln_b=ln_b)
    mom_m = dict(ffn1_w_gate=m_ffn1_w_gate, ffn1_w_up=m_ffn1_w_up, ffn1_w_down=m_ffn1_w_down, ffn2_w_gate=m_ffn2_w_gate,
                 ffn2_w_up=m_ffn2_w_up, ffn2_w_down=m_ffn2_w_down, w_in=m_w_in, pool_w=m_pool_w,
                 pool_scale=m_pool_scale, conv_w=m_conv_w, rpb=m_rpb, w_out=m_w_out, ln_g=m_ln_g, ln_b=m_ln_b)
    mom_v = dict(ffn1_w_gate=v_ffn1_w_gate, ffn1_w_up=v_ffn1_w_up, ffn1_w_down=v_ffn1_w_down, ffn2_w_gate=v_ffn2_w_gate,
                 ffn2_w_up=v_ffn2_w_up, ffn2_w_down=v_ffn2_w_down, w_in=v_w_in, pool_w=v_pool_w,
                 pool_scale=v_pool_scale, conv_w=v_conv_w, rpb=v_rpb, w_out=v_w_out, ln_g=v_ln_g, ln_b=v_ln_b)
    order = list(weights)
    L = ffn1_w_gate.shape[0]
    xi, yi, ci = lax.axis_index("x"), lax.axis_index("y"), lax.axis_index("c")
    q_me = 2 * xi + yi
    x2 = x[0]
    target = loss_target[0]
    D = x2.shape[1]

    rows_of = {n: _flat_rows(weights[n].shape) for n in BIG}
    per_layer = sum(rows_of.values())
    wflat = jnp.concatenate([weights[n].astype(BF).reshape(L, rows_of[n], FLAT_W) for n in BIG], axis=1)
    wflat = wflat.reshape(L * per_layer, FLAT_W)
    small = jnp.concatenate([ln_g.reshape(-1), ln_b.reshape(-1), conv_w.reshape(-1)])
    n_small = small.shape[0]
    small_rows = -(-n_small // (8 * 128)) * 8
    small = jnp.pad(small, (0, small_rows * 128 - n_small)).reshape(small_rows, 128)
    gathered, small_all = _gather_weights(wflat, small)
    gathered = gathered.reshape(4, L, per_layer, FLAT_W)
    full_shape = {}
    for n in BIG:
        k, nn = weights[n].shape[1:]
        full_shape[n] = (k, nn * 4) if n in COL_SHARDED else (k * 4, nn)
    offs = dict(zip(BIG, np.cumsum([0] + [rows_of[n] for n in BIG])[:-1]))

    def layer_weight(n, l):
        return _from_quarters(n, gathered[:, l, offs[n]:offs[n] + rows_of[n]], *full_shape[n])

    small_all = small_all.reshape(4, small_rows * 128)[:, :n_small]
    dq4 = D // 4
    n_ln = L * 3 * dq4
    ln_g_all = small_all[:, :n_ln].reshape(4, L, 3, dq4).transpose(1, 2, 0, 3).reshape(L, 3, D)
    ln_b_all = small_all[:, n_ln:2 * n_ln].reshape(4, L, 3, dq4).transpose(1, 2, 0, 3).reshape(L, 3, D)
    conv_all = small_all[:, 2 * n_ln:].reshape(4, L, 3, D_CONV // 4).transpose(1, 2, 0, 3).reshape(L, 3, D_CONV)

    ng = len(POOL_WINDOWS)
    pg = D_POOL // ng
    saved = []
    h = x2
    for l in range(L):
        W = {n: layer_weight(n, l) for n in BIG}
        wblk = jnp.zeros((D_POOL, D_POOL), F32)
        for gi in range(ng):
            wblk = wblk.at[gi * pg:(gi + 1) * pg, gi * pg:(gi + 1) * pg].set(pool_w[l, gi])
        wblk = wblk.astype(BF)
        vec = jnp.concatenate([pool_scale[l][None], conv_all[l], jnp.zeros((4, D_POOL), F32)], axis=0)
        bias = _bias_table(rpb[l])
        lg = [ln_g_all[l, j][None] for j in range(3)]
        lb = [ln_b_all[l, j][None] for j in range(3)]
        x1, x1b, z1, g1, u1 = _ffn_fwd(h, W["ffn1_w_gate"], W["ffn1_w_up"], W["ffn1_w_down"], lg[0], lb[0])
        pabc, qkv = _proj(x1b, W["w_in"])
        yab = _mixab_fwd(pabc, wblk, vec)
        yc = _attn_fwd(qkv, bias)
        xm, xmb, zm = _mixout_fwd(yab, yc, x1, W["w_out"], lg[1], lb[1])
        x3, x3b, z3, g3, u3 = _ffn_fwd(xm, W["ffn2_w_gate"], W["ffn2_w_up"], W["ffn2_w_down"], lg[2], lb[2])
        saved.append(dict(W=W, wblk=wblk, vec=vec, bias=bias, lg=lg, hb=h.astype(BF) if l == 0 else hb_prev,
                          z1=z1, g1=g1, u1=u1, x1b=x1b, pabc=pabc, qkv=qkv, yab=yab, yc=yc, zm=zm, xmb=xmb,
                          z3=z3, g3=g3, u3=u3))
        h, hb_prev = x3, x3b

    loss_tile, dh = _loss_head(h, target)
    loss = lax.psum(loss_tile[0, 0], ("x", "y", "c"))

    grads_big = [None] * L
    g_small = dict(pool_w=[None] * L, pool_scale=[None] * L, conv_w=[None] * L, rpb=[None] * L, ln_g=[None] * L,
                   ln_b=[None] * L)
    for l in reversed(range(L)):
        sv = saved[l]
        W = sv["W"]
        G = {}
        dxm, df, dg, du, a, ln3 = _ffn_bwd(dh, sv["z3"], sv["g3"], sv["u3"], W["ffn2_w_gate"], W["ffn2_w_up"],
                                           W["ffn2_w_down"], sv["lg"][2])
        G["ffn2_w_gate"] = _wgrad(sv["xmb"], dg, 1408)
        G["ffn2_w_up"] = _wgrad(sv["xmb"], du, 1408)
        G["ffn2_w_down"] = _wgrad(a, df, 512)
        dres, dzb, dycat, ln2 = _mixout_bwd(dxm, sv["zm"], W["w_out"], sv["lg"][1])
        G["w_out"] = _wgrad(jnp.concatenate([sv["yab"], sv["yc"]], axis=1), dzb, 1024)
        dpabc, dwblk, dvec = _mixab_bwd(sv["pabc"], dycat, sv["wblk"], sv["vec"])
        dq, dk, dv, dbias = _attn_bwd(sv["qkv"], sv["bias"], dycat)
        dproj = jnp.concatenate([dpabc, dq, dk.astype(BF), dv.astype(BF)], axis=1)
        G["w_in"] = _wgrad(sv["x1b"], dproj, 1280)
        dx1 = _proj_bwd(dres, dproj, W["w_in"])
        dh, df, dg, du, a, ln1 = _ffn_bwd(dx1, sv["z1"], sv["g1"], sv["u1"], W["ffn1_w_gate"], W["ffn1_w_up"],
                                          W["ffn1_w_down"], sv["lg"][0])
        G["ffn1_w_gate"] = _wgrad(sv["hb"], dg, 1408)
        G["ffn1_w_up"] = _wgrad(sv["hb"], du, 1408)
        G["ffn1_w_down"] = _wgrad(a, df, 512)
        grads_big[l] = jnp.concatenate([_to_quarters(n, G[n]) for n in BIG], axis=1)
        g_small["pool_w"][l] = jnp.stack([dwblk[gi * pg:(gi + 1) * pg, gi * pg:(gi + 1) * pg] for gi in range(ng)])
        g_small["pool_scale"][l] = dvec[0]
        g_small["conv_w"][l] = dvec[1:4]
        g_small["rpb"][l] = _bias_grad(dbias)
        g_small["ln_g"][l] = jnp.stack([ln1[0], ln2[0], ln3[0]])
        g_small["ln_b"][l] = jnp.stack([ln1[1], ln2[1], ln3[1]])
    grad_x = dh[None]

    hl = L // 2
    pair = [jnp.concatenate(grads_big[i * hl:(i + 1) * hl], axis=1) for i in range(2)]
    keep = jnp.where(ci == 0, pair[0], pair[1])
    send = jnp.where(ci == 0, pair[1], pair[0]).astype(BF)
    recv = _swap_sibling(send)
    chip_f, chip_b = _add_chip(keep, recv)
    from_chips = _scatter_chips(chip_b)
    mine = _add_final(q_me.reshape(1).astype(jnp.int32), chip_f, from_chips)
    both = _join_halves(mine).reshape(L, per_layer, FLAT_W)

    small_names = ("pool_w", "pool_scale", "conv_w", "rpb", "ln_g", "ln_b")
    small_full = {n: jnp.stack(g_small[n]) for n in small_names}
    vflat = jnp.concatenate([small_full[n].reshape(-1) for n in small_names])
    n_v = vflat.shape[0]
    v_rows = -(-n_v // (8 * FLAT_W)) * 8
    vsum = _allreduce_small(jnp.pad(vflat, (0, v_rows * FLAT_W - n_v)).reshape(v_rows, FLAT_W)).reshape(-1)
    grads = {}
    off = 0
    for n in small_names:
        sz = int(np.prod(small_full[n].shape))
        grads[n] = vsum[off:off + sz].reshape(small_full[n].shape)
        off += sz
    for n in ("conv_w", "ln_g", "ln_b"):
        width = weights[n].shape[-1]
        grads[n] = lax.dynamic_slice_in_dim(grads[n], q_me * width, width, axis=2)
    for n in BIG:
        grads[n] = both[:, offs[n]:offs[n] + rows_of[n]].reshape(weights[n].shape)

    delta, new_m, new_v = {}, {}, {}
    for n in order:
        delta[n], new_m[n], new_v[n] = _adamw(weights[n], grads[n], mom_m[n], mom_v[n])
    return (loss, grad_x, *[grads[n] for n in order], *[delta[n] for n in order], *[new_m[n] for n in order],
            *[new_v[n] for n in order])
```

```python
import functools
import numpy as np
import jax
import jax.numpy as jnp
from jax import lax
from jax.experimental import pallas as pl
from jax.experimental.pallas import tpu as pltpu

BF = jnp.bfloat16
F32 = jnp.float32
MESH = pl.DeviceIdType.MESH

DEPTH = 4
ALPHA = (2.0 * DEPTH) ** 0.25
LN_EPS = 1e-5
NEG_INF = -1e30
GRID_W = 64
NA_ROWS = 8
NA_COLS = 16
NA_HEADS = 8
HEAD_DIM = 64
D_POOL = 256
D_CONV = 256
D_NA = 512
POOL_WINDOWS = (2, 4, 8, 16)
HALO = 8
ADAM_LR, ADAM_B1, ADAM_B2, ADAM_EPS, ADAM_WD, ADAM_STEP = 0.001, 0.9, 0.999, 1e-08, 0.01, 10
VMEM_LIMIT = 56 * 1024 * 1024
FLAT_W = 1024


def _cp(n_axes):
    return pltpu.CompilerParams(dimension_semantics=("arbitrary",) * n_axes, vmem_limit_bytes=VMEM_LIMIT)


def _chunks(n, step):
    return [(c0, min(step, n - c0)) for c0 in range(0, n, step)]


def _full(shape):
    nd = len(shape)
    return pl.BlockSpec(shape, lambda *_: (0,) * nd)


def _resident(shape):
    nd = len(shape)
    return pl.BlockSpec(shape, lambda *_: (0,) * nd, pipeline_mode=pl.Buffered(1))


def _nt(a, b):
    return lax.dot_general(a, b, (((1,), (1,)), ((), ())), preferred_element_type=F32)


def _tn(a, b):
    return lax.dot_general(a, b, (((0,), (0,)), ((), ())), preferred_element_type=F32)


def _nn(a, b):
    return jnp.dot(a, b, preferred_element_type=F32)


def _ln_fwd(z, g, b):
    mu = jnp.mean(z, axis=-1, keepdims=True)
    zc = z - mu
    var = jnp.mean(zc * zc, axis=-1, keepdims=True)
    return zc * lax.rsqrt(var + LN_EPS) * g + b


def _ln_bwd(dy, z, g):
    mu = jnp.mean(z, axis=-1, keepdims=True)
    zc = z - mu
    var = jnp.mean(zc * zc, axis=-1, keepdims=True)
    rstd = lax.rsqrt(var + LN_EPS)
    xhat = zc * rstd
    gdy = dy * g
    m1 = jnp.mean(gdy, axis=-1, keepdims=True)
    m2 = jnp.mean(gdy * xhat, axis=-1, keepdims=True)
    return rstd * (gdy - m1 - xhat * m2), xhat


def _acc_ln_grads(acc_ref, first, dy, xhat):
    @pl.when(first)
    def _():
        acc_ref[...] = jnp.zeros_like(acc_ref)
    acc_ref[0:1, :] += jnp.sum(dy * xhat, axis=0, keepdims=True)
    acc_ref[1:2, :] += jnp.sum(dy, axis=0, keepdims=True)


def _ffn_fwd(x, wg, wu, wd, lg, lb):
    S, D = x.shape
    Fd = wg.shape[1]
    tm = min(256, S)
    chunks = _chunks(Fd, 1024)

    def body(x_ref, wg_ref, wu_ref, wd_ref, lg_ref, lb_ref, xo_ref, xb_ref, z_ref, g_ref, u_ref):
        x = x_ref[...]
        xb = x.astype(BF)
        acc = jnp.zeros((tm, D), F32)
        for c0, cw in chunks:
            g = _nn(xb, wg_ref[:, c0:c0 + cw])
            u = _nn(xb, wu_ref[:, c0:c0 + cw])
            g_ref[:, c0:c0 + cw] = g.astype(BF)
            u_ref[:, c0:c0 + cw] = u.astype(BF)
            a = g * jax.nn.sigmoid(g) * u
            acc = acc + _nn(a.astype(BF), wd_ref[c0:c0 + cw, :])
        z = ALPHA * x + 0.5 * acc
        xo = _ln_fwd(z, lg_ref[...], lb_ref[...])
        z_ref[...] = z
        xo_ref[...] = xo
        xb_ref[...] = xo.astype(BF)

    row = lambda w: pl.BlockSpec((tm, w), lambda i: (i, 0))
    return pl.pallas_call(
        body, name="ffn_fwd", grid=(S // tm,),
        in_specs=[row(D), _resident((D, Fd)), _resident((D, Fd)), _resident((Fd, D)), _full((1, D)), _full((1, D))],
        out_specs=[row(D), row(D), row(D), row(Fd), row(Fd)],
        out_shape=[jax.ShapeDtypeStruct((S, D), F32), jax.ShapeDtypeStruct((S, D), BF),
                   jax.ShapeDtypeStruct((S, D), F32), jax.ShapeDtypeStruct((S, Fd), BF),
                   jax.ShapeDtypeStruct((S, Fd), BF)],
        compiler_params=_cp(1),
    )(x, wg, wu, wd, lg, lb)


def _ffn_bwd(dxo, z, g, u, wg, wu, wd, lg):
    S, D = dxo.shape
    Fd = wg.shape[1]
    tm = min(256, S)
    chunks = _chunks(Fd, 1024)

    def body(dxo_ref, z_ref, g_ref, u_ref, wg_ref, wu_ref, wd_ref, lg_ref,
             dx_ref, df_ref, dg_ref, du_ref, a_ref, ln_ref):
        dy = dxo_ref[...]
        dz, xhat = _ln_bwd(dy, z_ref[...], lg_ref[...])
        _acc_ln_grads(ln_ref, pl.program_id(0) == 0, dy, xhat)
        dfb = (0.5 * dz).astype(BF)
        df_ref[...] = dfb
        acc = ALPHA * dz
        for c0, cw in chunks:
            da = _nt(dfb, wd_ref[c0:c0 + cw, :])
            gg = g_ref[:, c0:c0 + cw].astype(F32)
            uu = u_ref[:, c0:c0 + cw].astype(F32)
            sg = jax.nn.sigmoid(gg)
            silu = gg * sg
            a_ref[:, c0:c0 + cw] = (silu * uu).astype(BF)
            dgb = (da * uu * (sg * (1.0 + gg * (1.0 - sg)))).astype(BF)
            dub = (da * silu).astype(BF)
            dg_ref[:, c0:c0 + cw] = dgb
            du_ref[:, c0:c0 + cw] = dub
            acc = acc + _nt(dgb, wg_ref[:, c0:c0 + cw]) + _nt(dub, wu_ref[:, c0:c0 + cw])
        dx_ref[...] = acc

    row = lambda w: pl.BlockSpec((tm, w), lambda i: (i, 0))
    return pl.pallas_call(
        body, name="ffn_bwd", grid=(S // tm,),
        in_specs=[row(D), row(D), row(Fd), row(Fd), _resident((D, Fd)), _resident((D, Fd)), _resident((Fd, D)), _full((1, D))],
        out_specs=[row(D), row(D), row(Fd), row(Fd), row(Fd), _full((8, D))],
        out_shape=[jax.ShapeDtypeStruct((S, D), F32), jax.ShapeDtypeStruct((S, D), BF),
                   jax.ShapeDtypeStruct((S, Fd), BF), jax.ShapeDtypeStruct((S, Fd), BF),
                   jax.ShapeDtypeStruct((S, Fd), BF), jax.ShapeDtypeStruct((8, D), F32)],
        compiler_params=_cp(1),
    )(dxo, z, g, u, wg, wu, wd, lg)


def _wgrad(a, b, tn):
    S, K = a.shape
    N = b.shape[1]
    ts = min(512, S)

    def body(a_ref, b_ref, o_ref):
        @pl.when(pl.program_id(1) == 0)
        def _():
            o_ref[...] = jnp.zeros_like(o_ref)
        o_ref[...] += _tn(a_ref[...], b_ref[...])

    return pl.pallas_call(
        body, name=f"wgrad_{K}x{N}", grid=(N // tn, S // ts),
        in_specs=[pl.BlockSpec((ts, K), lambda j, s: (s, 0)), pl.BlockSpec((ts, tn), lambda j, s: (s, j))],
        out_specs=pl.BlockSpec((K, tn), lambda j, s: (0, j)),
        out_shape=jax.ShapeDtypeStruct((K, N), F32),
        compiler_params=_cp(2),
    )(a, b)


def _proj(xb, win):
    S, D = xb.shape
    N = win.shape[1]
    n1 = D_POOL + 3 * D_CONV
    tm = min(512, S)

    def body(x_ref, w_ref, p_ref, qkv_ref):
        x = x_ref[...]
        p_ref[...] = _nn(x, w_ref[:, 0:n1])
        qkv_ref[...] = _nn(x, w_ref[:, n1:N]).astype(BF)

    row = lambda w: pl.BlockSpec((tm, w), lambda i: (i, 0))
    return pl.pallas_call(
        body, name="mix_proj", grid=(S // tm,),
        in_specs=[row(D), _full((D, N))],
        out_specs=[row(n1), row(N - n1)],
        out_shape=[jax.ShapeDtypeStruct((S, n1), F32), jax.ShapeDtypeStruct((S, N - n1), BF)],
        compiler_params=_cp(1),
    )(xb, win)


def _mm_exact(a, b, name):
    def body(a_ref, b_ref, o_ref):
        o_ref[...] = jnp.dot(a_ref[...], b_ref[...], preferred_element_type=F32, precision=lax.Precision.HIGHEST)

    return pl.pallas_call(
        body, name=name, in_specs=[_full(a.shape), _full(b.shape)], out_specs=_full((a.shape[0], b.shape[1])),
        out_shape=jax.ShapeDtypeStruct((a.shape[0], b.shape[1]), F32),
        compiler_params=pltpu.CompilerParams(vmem_limit_bytes=VMEM_LIMIT),
    )(a, b)


def _bias_constants():
    c = np.arange(GRID_W)
    col_start = np.clip(c - NA_COLS // 2, 0, GRID_W - NA_COLS)
    valid = (c[None, :] >= col_start[:, None]) & (c[None, :] < col_start[:, None] + NA_COLS)
    dc = np.clip(c[None, :] - c[:, None], -(NA_COLS - 1), NA_COLS - 1) + (NA_COLS - 1)
    onehot = np.zeros((32, GRID_W * GRID_W), np.float32)
    onehot[dc.reshape(-1), np.arange(GRID_W * GRID_W)] = 1.0
    mask_kq = np.where(valid.T, 0.0, NEG_INF).astype(np.float32)
    mask = np.tile(mask_kq, (2 * NA_ROWS - 1, 2))
    return onehot, mask


def _bias_table(rpb):
    onehot, mask = _bias_constants()
    nr = 2 * NA_ROWS - 1
    r2 = jnp.pad(rpb.reshape(NA_HEADS * nr, 2 * NA_COLS - 1), ((0, 0), (0, 1)))
    t = _mm_exact(r2, jnp.asarray(onehot), "bias_expand")
    t = t.reshape(NA_HEADS // 2, 2, nr, GRID_W, GRID_W).transpose(0, 2, 4, 1, 3)
    return t.reshape(NA_HEADS // 2, nr * GRID_W, 2 * GRID_W) + jnp.asarray(mask)[None]


def _bias_grad(dt):
    onehot, _ = _bias_constants()
    nr = 2 * NA_ROWS - 1
    d = dt.reshape(NA_HEADS // 2, nr, GRID_W, 2, GRID_W).transpose(0, 3, 1, 4, 2).reshape(NA_HEADS * nr, -1)
    g = _mm_exact(d, jnp.asarray(onehot.T.copy()), "bias_reduce")
    return g[:, :2 * NA_COLS - 1].reshape(NA_HEADS, nr, 2 * NA_COLS - 1)


def _attn_rows(S):
    rows = S // GRID_W
    rb = min(16, rows)
    return rows, rb


def _attn_step(r, rows, q, k_ref, v_ref, b_ref, m_a):
    rs = jnp.clip(r - NA_ROWS // 2, 0, rows - NA_ROWS)
    s0 = rs - r + (NA_ROWS - 1)
    zero = jnp.zeros_like(q)
    q2 = jnp.concatenate([jnp.where(m_a, q, zero), jnp.where(m_a, zero, q)], axis=0)
    ks = pl.ds(pl.multiple_of(rs * GRID_W, GRID_W), NA_ROWS * GRID_W)
    kb = k_ref[ks, :]
    vb = v_ref[ks, :]
    bs = pl.ds(pl.multiple_of(s0 * GRID_W, GRID_W), NA_ROWS * GRID_W)
    s = _nt(kb, q2) * (HEAD_DIM ** -0.5) + b_ref[0, bs, :]
    m = jnp.max(s, axis=0, keepdims=True)
    p = jnp.exp(s - m)
    p = p / jnp.sum(p, axis=0, keepdims=True)
    return p, q2, kb, vb, ks, bs


def _attn_fwd(qkv, bias):
    S = qkv.shape[0]
    rows, rb = _attn_rows(S)
    tq = rb * GRID_W
    hp = NA_HEADS // 2

    def body(q_ref, k_ref, v_ref, b_ref, o_ref):
        base = pl.program_id(1) * rb
        m_a = lax.broadcasted_iota(jnp.int32, (GRID_W, 128), 1) < HEAD_DIM

        def step(i, carry):
            qs = pl.ds(pl.multiple_of(i * GRID_W, GRID_W), GRID_W)
            p, _, _, vb, _, _ = _attn_step(base + i, rows, q_ref[qs, :], k_ref, v_ref, b_ref, m_a)
            o2 = _tn(p.astype(BF), vb)
            o_ref[qs, :] = jnp.where(m_a, o2[:GRID_W], o2[GRID_W:]).astype(BF)
            return carry

        lax.fori_loop(0, rb, step, 0, unroll=2)

    return pl.pallas_call(
        body, name="attn_fwd", grid=(hp, rows // rb),
        in_specs=[pl.BlockSpec((tq, 128), lambda h, r: (r, h)),
                  pl.BlockSpec((S, 128), lambda h, r: (0, hp + h)),
                  pl.BlockSpec((S, 128), lambda h, r: (0, 2 * hp + h)),
                  pl.BlockSpec((1, bias.shape[1], 128), lambda h, r: (h, 0, 0))],
        out_specs=pl.BlockSpec((tq, 128), lambda h, r: (r, h)),
        out_shape=jax.ShapeDtypeStruct((S, D_NA), BF),
        compiler_params=_cp(2),
    )(qkv, qkv, qkv, bias)


def _attn_bwd(qkv, bias, dycat):
    S = qkv.shape[0]
    rows, rb = _attn_rows(S)
    tq = rb * GRID_W
    hp = NA_HEADS // 2
    scale = HEAD_DIM ** -0.5

    def body(q_ref, k_ref, v_ref, b_ref, do_ref, dq_ref, dk_ref, dv_ref, db_ref):
        base = pl.program_id(1) * rb
        m_a = lax.broadcasted_iota(jnp.int32, (GRID_W, 128), 1) < HEAD_DIM

        @pl.when(pl.program_id(1) == 0)
        def _():
            dk_ref[...] = jnp.zeros_like(dk_ref)
            dv_ref[...] = jnp.zeros_like(dv_ref)
            db_ref[...] = jnp.zeros_like(db_ref)

        def step(i, carry):
            qs = pl.ds(pl.multiple_of(i * GRID_W, GRID_W), GRID_W)
            p, q2, kb, vb, ks, bs = _attn_step(base + i, rows, q_ref[qs, :], k_ref, v_ref, b_ref, m_a)
            do = do_ref[qs, :].astype(BF)
            zero = jnp.zeros_like(do)
            do2 = jnp.concatenate([jnp.where(m_a, do, zero), jnp.where(m_a, zero, do)], axis=0)
            dp = _nt(vb, do2)
            ds = p * (dp - jnp.sum(p * dp, axis=0, keepdims=True))
            db_ref[0, bs, :] += ds
            dsb = ds.astype(BF)
            dq2 = _tn(dsb, kb) * scale
            dq_ref[qs, :] = jnp.where(m_a, dq2[:GRID_W], dq2[GRID_W:]).astype(BF)
            dk_ref[ks, :] += _nn(dsb, q2) * scale
            dv_ref[ks, :] += _nn(p.astype(BF), do2)
            return carry

        lax.fori_loop(0, rb, step, 0, unroll=2)

    nb = bias.shape[1]
    return pl.pallas_call(
        body, name="attn_bwd", grid=(hp, rows // rb),
        in_specs=[pl.BlockSpec((tq, 128), lambda h, r: (r, h)),
                  pl.BlockSpec((S, 128), lambda h, r: (0, hp + h)),
                  pl.BlockSpec((S, 128), lambda h, r: (0, 2 * hp + h)),
                  pl.BlockSpec((1, nb, 128), lambda h, r: (h, 0, 0)),
                  pl.BlockSpec((tq, 128), lambda h, r: (r, hp + h))],
        out_specs=[pl.BlockSpec((tq, 128), lambda h, r: (r, h)),
                   pl.BlockSpec((S, 128), lambda h, r: (0, h)),
                   pl.BlockSpec((S, 128), lambda h, r: (0, h)),
                   pl.BlockSpec((1, nb, 128), lambda h, r: (h, 0, 0))],
        out_shape=[jax.ShapeDtypeStruct((S, D_NA), BF), jax.ShapeDtypeStruct((S, D_NA), F32),
                   jax.ShapeDtypeStruct((S, D_NA), F32), jax.ShapeDtypeStruct((hp, nb, 128), F32)],
        compiler_params=_cp(2),
    )(qkv, qkv, qkv, bias, dycat)


def _halo_specs(tm, width, S):
    hb = tm // HALO
    last = S // HALO - 1
    return [pl.BlockSpec((tm, width), lambda i: (i, 0)),
            pl.BlockSpec((HALO, width), lambda i: (jnp.maximum(i * hb - 1, 0), 0)),
            pl.BlockSpec((HALO, width), lambda i: (jnp.minimum((i + 1) * hb, last), 0))]


def _with_halo(cur_ref, prev_ref, next_ref, i, nt):
    prev = jnp.where(i > 0, prev_ref[...], 0.0)
    nxt = jnp.where(i < nt - 1, next_ref[...], 0.0)
    return jnp.concatenate([prev, cur_ref[...], nxt], axis=0)


def _shift(a, k):
    n = a.shape[0]
    return pltpu.roll(a, k % n, 0)


def _pool_lanes(n):
    lane = lax.broadcasted_iota(jnp.int32, (n, D_POOL), 1)
    group = D_POOL // len(POOL_WINDOWS)
    return [lane < group * (j + 1) for j in range(len(POOL_WINDOWS) - 1)]


def _by_window(lanes, vals):
    return jnp.where(lanes[0], vals[0], jnp.where(lanes[1], vals[1], jnp.where(lanes[2], vals[2], vals[3])))


def _pool_count(lanes, t, S):
    back = _by_window(lanes, tuple(w // 2 for w in POOL_WINDOWS))
    lo = jnp.maximum(t - back, 0)
    hi = jnp.minimum(t + back, S)
    return jnp.maximum(hi - lo, 1).astype(F32)


def _pool_p(u, lanes, cnt):
    a = u + _shift(u, 1)
    b = _shift(a, 1) + _shift(a, -1)
    c = _shift(b, 2) + _shift(b, -2)
    d = _shift(c, 4) + _shift(c, -4)
    return _by_window(lanes, (a, b, c, d)) / cnt - u


def _mixab_fwd(pabc, wblk, vec):
    S = pabc.shape[0]
    tm = min(512, S)
    nt = S // tm
    n = tm + 2 * HALO
    tile = slice(HALO, HALO + tm)

    def body(cur_ref, prev_ref, next_ref, w_ref, vec_ref, o_ref):
        i = pl.program_id(0)
        ext = _with_halo(cur_ref, prev_ref, next_ref, i, nt)
        lanes = _pool_lanes(n)
        t = i * tm - HALO + lax.broadcasted_iota(jnp.int32, (n, D_POOL), 0)
        p = _pool_p(ext[:, 0:D_POOL], lanes, _pool_count(lanes, t, S))[tile]
        o_ref[:, 0:D_POOL] = (_nn(p.astype(BF), w_ref[...]) * vec_ref[0:1, :]).astype(BF)
        zc = ext[:, 512:768] * ext[:, 768:1024]
        conv = vec_ref[1:2, :] * _shift(zc, 1) + vec_ref[2:3, :] * zc + vec_ref[3:4, :] * _shift(zc, -1)
        o_ref[:, D_POOL:D_POOL + D_CONV] = (ext[tile, 256:512] * conv[tile]).astype(BF)

    return pl.pallas_call(
        body, name="mixab_fwd", grid=(nt,),
        in_specs=_halo_specs(tm, 1024, S) + [_full((D_POOL, D_POOL)), _full((8, D_POOL))],
        out_specs=pl.BlockSpec((tm, D_POOL + D_CONV), lambda i: (i, 0)),
        out_shape=jax.ShapeDtypeStruct((S, D_POOL + D_CONV), BF),
        compiler_params=_cp(1),
    )(pabc, pabc, pabc, wblk, vec)


def _mixab_bwd(pabc, dycat, wblk, vec):
    S = pabc.shape[0]
    tm = min(512, S)
    nt = S // tm
    n = tm + 2 * HALO
    tile = slice(HALO, HALO + tm)

    def body(cur_ref, prev_ref, next_ref, dcur_ref, dprev_ref, dnext_ref, w_ref, vec_ref, o_ref, dw_ref, dvec_ref):
        i = pl.program_id(0)

        @pl.when(i == 0)
        def _():
            dw_ref[...] = jnp.zeros_like(dw_ref)
            dvec_ref[...] = jnp.zeros_like(dvec_ref)

        ext = _with_halo(cur_ref, prev_ref, next_ref, i, nt)
        dext = _with_halo(dcur_ref, dprev_ref, dnext_ref, i, nt)
        lanes = _pool_lanes(n)
        t = i * tm - HALO + lax.broadcasted_iota(jnp.int32, (n, D_POOL), 0)
        cnt = _pool_count(lanes, t, S)
        w = w_ref[...]
        scale = vec_ref[0:1, :]
        pb = _pool_p(ext[:, 0:D_POOL], lanes, cnt)[tile].astype(BF)
        dya = dext[:, 0:D_POOL]
        dvec_ref[0:1, :] += jnp.sum(dya[tile] * _nn(pb, w), axis=0, keepdims=True)
        dqb = (dya * scale).astype(BF)
        dw_ref[...] += _tn(pb, dqb[tile])
        dp = _nt(dqb, w)
        r = dp / cnt
        a = r + _shift(r, -1)
        b = _shift(a, 1) + _shift(a, -1)
        c = _shift(b, 2) + _shift(b, -2)
        d = _shift(c, 4) + _shift(c, -4)
        o_ref[:, 0:256] = (_by_window(lanes, (a, b, c, d)) - dp)[tile].astype(BF)
        gb, gc, hh = ext[:, 256:512], ext[:, 512:768], ext[:, 768:1024]
        zc = gc * hh
        zm, zp = _shift(zc, 1), _shift(zc, -1)
        w0, w1, w2 = vec_ref[1:2, :], vec_ref[2:3, :], vec_ref[3:4, :]
        dyb = dext[:, D_POOL:D_POOL + D_CONV]
        dconv = dyb * gb
        o_ref[:, 256:512] = (dyb * (w0 * zm + w1 * zc + w2 * zp))[tile].astype(BF)
        dzc = w0 * _shift(dconv, -1) + w1 * dconv + w2 * _shift(dconv, 1)
        o_ref[:, 512:768] = (dzc * hh)[tile].astype(BF)
        o_ref[:, 768:1024] = (dzc * gc)[tile].astype(BF)
        dct = dconv[tile]
        dvec_ref[1:2, :] += jnp.sum(dct * zm[tile], axis=0, keepdims=True)
        dvec_ref[2:3, :] += jnp.sum(dct * zc[tile], axis=0, keepdims=True)
        dvec_ref[3:4, :] += jnp.sum(dct * zp[tile], axis=0, keepdims=True)

    return pl.pallas_call(
        body, name="mixab_bwd", grid=(nt,),
        in_specs=_halo_specs(tm, 1024, S) + _halo_specs(tm, 512, S) + [_full((D_POOL, D_POOL)), _full((8, D_POOL))],
        out_specs=[pl.BlockSpec((tm, 1024), lambda i: (i, 0)), _full((D_POOL, D_POOL)), _full((8, D_POOL))],
        out_shape=[jax.ShapeDtypeStruct((S, 1024), BF), jax.ShapeDtypeStruct((D_POOL, D_POOL), F32),
                   jax.ShapeDtypeStruct((8, D_POOL), F32)],
        compiler_params=_cp(1),
    )(pabc, pabc, pabc, dycat, dycat, dycat, wblk, vec)


def _mixout_fwd(yab, yc, x, wout, lg, lb):
    S, D = x.shape
    tm = min(512, S)
    h = yab.shape[1]

    def body(yab_ref, yc_ref, x_ref, w_ref, lg_ref, lb_ref, xo_ref, xb_ref, z_ref):
        y = _nn(yab_ref[...], w_ref[0:h, :]) + _nn(yc_ref[...], w_ref[h:2 * h, :])
        z = ALPHA * x_ref[...] + y
        xo = _ln_fwd(z, lg_ref[...], lb_ref[...])
        z_ref[...] = z
        xo_ref[...] = xo
        xb_ref[...] = xo.astype(BF)

    row = lambda w: pl.BlockSpec((tm, w), lambda i: (i, 0))
    return pl.pallas_call(
        body, name="mixout_fwd", grid=(S // tm,),
        in_specs=[row(h), row(h), row(D), _full((2 * h, D)), _full((1, D)), _full((1, D))],
        out_specs=[row(D), row(D), row(D)],
        out_shape=[jax.ShapeDtypeStruct((S, D), F32), jax.ShapeDtypeStruct((S, D), BF),
                   jax.ShapeDtypeStruct((S, D), F32)],
        compiler_params=_cp(1),
    )(yab, yc, x, wout, lg, lb)


def _mixout_bwd(dxo, z, wout, lg):
    S, D = dxo.shape
    M = wout.shape[0]
    tm = min(512, S)

    def body(dxo_ref, z_ref, w_ref, lg_ref, dres_ref, dzb_ref, dy_ref, ln_ref):
        dy = dxo_ref[...]
        dz, xhat = _ln_bwd(dy, z_ref[...], lg_ref[...])
        _acc_ln_grads(ln_ref, pl.program_id(0) == 0, dy, xhat)
        dzb = dz.astype(BF)
        dres_ref[...] = ALPHA * dz
        dzb_ref[...] = dzb
        dy_ref[...] = _nt(dzb, w_ref[...])

    row = lambda w: pl.BlockSpec((tm, w), lambda i: (i, 0))
    return pl.pallas_call(
        body, name="mixout_bwd", grid=(S // tm,),
        in_specs=[row(D), row(D), _full((M, D)), _full((1, D))],
        out_specs=[row(D), row(D), row(M), _full((8, D))],
        out_shape=[jax.ShapeDtypeStruct((S, D), F32), jax.ShapeDtypeStruct((S, D), BF),
                   jax.ShapeDtypeStruct((S, M), F32), jax.ShapeDtypeStruct((8, D), F32)],
        compiler_params=_cp(1),
    )(dxo, z, wout, lg)


def _proj_bwd(dres, dproj, win):
    S, D = dres.shape
    N = win.shape[1]
    tm = min(512, S)

    def body(dres_ref, dp_ref, w_ref, dx_ref):
        dx_ref[...] = dres_ref[...] + _nt(dp_ref[...], w_ref[...])

    row = lambda w: pl.BlockSpec((tm, w), lambda i: (i, 0))
    return pl.pallas_call(
        body, name="mix_proj_bwd", grid=(S // tm,),
        in_specs=[row(D), row(N), _full((D, N))],
        out_specs=row(D),
        out_shape=jax.ShapeDtypeStruct((S, D), F32),
        compiler_params=_cp(1),
    )(dres, dproj, win)


def _loss_head(y, target):
    S, D = y.shape
    tm = min(512, S)

    def body(y_ref, t_ref, l_ref, dy_ref):
        @pl.when(pl.program_id(0) == 0)
        def _():
            l_ref[...] = jnp.zeros_like(l_ref)
        e = y_ref[...] - t_ref[...]
        dy_ref[...] = e * (1.0 / D)
        part = jnp.sum(jnp.sum(e * e, axis=1, keepdims=True) * (1.0 / D), axis=0, keepdims=True)
        l_ref[...] += 0.5 * part

    row = pl.BlockSpec((tm, D), lambda i: (i, 0))
    return pl.pallas_call(
        body, name="loss_head", grid=(S // tm,),
        in_specs=[row, row], out_specs=[_full((8, 128)), row],
        out_shape=[jax.ShapeDtypeStruct((8, 128), F32), jax.ShapeDtypeStruct((S, D), F32)],
        compiler_params=_cp(1),
    )(y, target)


def _adamw(w, g, m, v):
    shape = w.shape
    cols = shape[-1]
    rows = int(np.prod(shape[:-1]))
    w2, g2, m2, v2 = (a.reshape(rows, cols) for a in (w, g, m, v))
    tr = rows
    for cand in (512, 352, 256):
        if rows > cand and rows % cand == 0:
            tr = cand
            break

    def body(w_ref, g_ref, m_ref, v_ref, d_ref, mo_ref, vo_ref):
        g = g_ref[...]
        mn = ADAM_B1 * m_ref[...] + (1.0 - ADAM_B1) * g
        vn = ADAM_B2 * v_ref[...] + (1.0 - ADAM_B2) * (g * g)
        m_hat = mn / (1.0 - ADAM_B1 ** ADAM_STEP)
        v_hat = vn / (1.0 - ADAM_B2 ** ADAM_STEP)
        d_ref[...] = -ADAM_LR * (m_hat / (jnp.sqrt(v_hat) + ADAM_EPS) + ADAM_WD * w_ref[...])
        mo_ref[...] = mn
        vo_ref[...] = vn

    spec = pl.BlockSpec((tr, cols), lambda i: (i, 0))
    outs = pl.pallas_call(
        body, name=f"adamw_{rows}x{cols}", grid=(rows // tr,),
        in_specs=[spec] * 4, out_specs=[spec] * 3,
        out_shape=[jax.ShapeDtypeStruct((rows, cols), F32)] * 3,
        compiler_params=_cp(1),
    )(w2, g2, m2, v2)
    return tuple(o.reshape(shape) for o in outs)


def _add_chip(keep, recv):
    q, R, W = keep.shape
    tr = 512

    def body(a_ref, b_ref, o_ref, ob_ref):
        s = a_ref[...] + b_ref[...].astype(F32)
        o_ref[...] = s
        ob_ref[...] = s.astype(BF)

    spec = pl.BlockSpec((1, tr, W), lambda j, i: (j, i, 0))
    return pl.pallas_call(
        body, name="rs_add_chip", grid=(q, R // tr), in_specs=[spec, spec], out_specs=[spec, spec],
        out_shape=[jax.ShapeDtypeStruct((q, R, W), F32), jax.ShapeDtypeStruct((q, R, W), BF)],
        compiler_params=_cp(2),
    )(keep, recv)


def _add_final(qidx, chip, recv):
    _, R, W = chip.shape
    tr = 512

    def body(q_ref, a_ref, b_ref, o_ref):
        s = a_ref[0]
        for j in range(3):
            s = s + b_ref[j].astype(F32)
        o_ref[...] = s

    grid_spec = pltpu.PrefetchScalarGridSpec(
        num_scalar_prefetch=1, grid=(R // tr,),
        in_specs=[pl.BlockSpec((1, tr, W), lambda i, q: (q[0], i, 0)), pl.BlockSpec((3, tr, W), lambda i, q: (0, i, 0))],
        out_specs=pl.BlockSpec((tr, W), lambda i, q: (i, 0)))
    return pl.pallas_call(
        body, name="rs_add_final", grid_spec=grid_spec, out_shape=jax.ShapeDtypeStruct((R, W), F32),
        compiler_params=_cp(1),
    )(qidx, chip, recv)


def _place():
    x, y, c = lax.axis_index("x"), lax.axis_index("y"), lax.axis_index("c")
    chips = [(1 - x, y), (x, 1 - y), (1 - x, 1 - y)]
    return x, y, c, chips


def _gather_weights(wflat, small):
    R, W = wflat.shape
    half = R // 2
    sr = small.shape[0]

    def body(w_ref, s_ref, o_ref, so_ref, send_sems, recv_sems):
        x, y, c, chips = _place()
        q = 2 * x + y
        sibling = (x, y, 1 - c)
        mine = pl.ds(c * half, half)
        theirs = pl.ds((1 - c) * half, half)

        def big(k, quarter, rows, to, src=None):
            dst = o_ref.at[quarter, rows, :]
            return pltpu.make_async_remote_copy(
                src_ref=dst if src is None else src, dst_ref=dst, send_sem=send_sems.at[k], recv_sem=recv_sems.at[k],
                device_id=to, device_id_type=MESH)

        def sml(k, quarter, to):
            return pltpu.make_async_remote_copy(
                src_ref=s_ref, dst_ref=so_ref.at[quarter], send_sem=send_sems.at[k], recv_sem=recv_sems.at[k],
                device_id=to, device_id_type=MESH)

        so_ref[q] = s_ref[...]
        sends = [big(j, q, mine, (*chip, c), src=w_ref.at[mine, :]) for j, chip in enumerate(chips)]
        sends += [sml(6 + j, q, (*chip, c)) for j, chip in enumerate(chips)]
        for cp in sends:
            cp.start()
        passed = []
        for j, (cx, cy) in enumerate(chips):
            big(j, 2 * cx + cy, mine, (x, y, c)).wait_recv()
            fwd = big(3 + j, 2 * cx + cy, mine, sibling)
            fwd.start()
            passed.append(fwd)
        for j, (cx, cy) in enumerate(chips):
            big(3 + j, 2 * cx + cy, theirs, (x, y, c)).wait_recv()
            sml(6 + j, 2 * cx + cy, (x, y, c)).wait_recv()
        for cp in sends + passed:
            cp.wait_send()

    return pl.pallas_call(
        body, name="gather_weights",
        in_specs=[pl.BlockSpec(memory_space=pl.ANY), pl.BlockSpec(memory_space=pltpu.VMEM)],
        out_specs=[pl.BlockSpec(memory_space=pl.ANY), pl.BlockSpec(memory_space=pltpu.VMEM)],
        out_shape=[jax.ShapeDtypeStruct((4, R, W), BF), jax.ShapeDtypeStruct((4, sr, 128), F32)],
        scratch_shapes=[pltpu.SemaphoreType.DMA((9,)), pltpu.SemaphoreType.DMA((9,))],
        compiler_params=pltpu.CompilerParams(has_side_effects=True),
    )(wflat, small)


def _swap_sibling(send):
    def body(s_ref, o_ref, send_sem, recv_sem):
        x, y, c, _ = _place()
        cp = pltpu.make_async_remote_copy(src_ref=s_ref, dst_ref=o_ref, send_sem=send_sem, recv_sem=recv_sem,
                                          device_id=(x, y, 1 - c), device_id_type=MESH)
        cp.start()
        cp.wait()

    return pl.pallas_call(
        body, name="rs_swap_sibling",
        in_specs=[pl.BlockSpec(memory_space=pl.ANY)], out_specs=pl.BlockSpec(memory_space=pl.ANY),
        out_shape=jax.ShapeDtypeStruct(send.shape, send.dtype),
        scratch_shapes=[pltpu.SemaphoreType.DMA, pltpu.SemaphoreType.DMA],
        compiler_params=pltpu.CompilerParams(has_side_effects=True),
    )(send)


def _scatter_chips(chip_b):
    _, R, W = chip_b.shape

    def body(s_ref, o_ref, send_sems, recv_sems):
        x, y, c, chips = _place()
        cps = [pltpu.make_async_remote_copy(
            src_ref=s_ref.at[2 * cx + cy], dst_ref=o_ref.at[j], send_sem=send_sems.at[j], recv_sem=recv_sems.at[j],
            device_id=(cx, cy, c), device_id_type=MESH) for j, (cx, cy) in enumerate(chips)]
        for cp in cps:
            cp.start()
        for cp in cps:
            cp.wait()

    return pl.pallas_call(
        body, name="rs_scatter_chips",
        in_specs=[pl.BlockSpec(memory_space=pl.ANY)], out_specs=pl.BlockSpec(memory_space=pl.ANY),
        out_shape=jax.ShapeDtypeStruct((3, R, W), BF),
        scratch_shapes=[pltpu.SemaphoreType.DMA((3,)), pltpu.SemaphoreType.DMA((3,))],
        compiler_params=pltpu.CompilerParams(has_side_effects=True),
    )(chip_b)


def _join_halves(mine):
    R, W = mine.shape

    def body(s_ref, o_ref, send_sem, recv_sem):
        x, y, c, _ = _place()
        cp = pltpu.make_async_remote_copy(src_ref=s_ref, dst_ref=o_ref.at[c], send_sem=send_sem, recv_sem=recv_sem,
                                          device_id=(x, y, 1 - c), device_id_type=MESH)
        cp.start()
        pltpu.make_async_remote_copy(src_ref=s_ref, dst_ref=o_ref.at[1 - c], send_sem=send_sem, recv_sem=recv_sem,
                                     device_id=(x, y, 1 - c), device_id_type=MESH).wait_recv()
        cp.wait_send()

    return pl.pallas_call(
        body, name="rs_join_halves",
        in_specs=[pl.BlockSpec(memory_space=pl.ANY)], out_specs=pl.BlockSpec(memory_space=pl.ANY),
        out_shape=jax.ShapeDtypeStruct((2, R, W), F32),
        scratch_shapes=[pltpu.SemaphoreType.DMA, pltpu.SemaphoreType.DMA],
        compiler_params=pltpu.CompilerParams(has_side_effects=True),
    )(mine)


def _allreduce_small(v):
    r, W = v.shape

    def body(v_ref, o_ref, land_ref, send_sems, recv_sems):
        x, y, c, _ = _place()
        me = 4 * x + 2 * y + c
        cps = []
        for m in range(1, 8):
            to = (x ^ (m >> 2), y ^ ((m >> 1) & 1), c ^ (m & 1))
            cps.append(pltpu.make_async_remote_copy(
                src_ref=v_ref, dst_ref=land_ref.at[m - 1], send_sem=send_sems.at[m - 1], recv_sem=recv_sems.at[m - 1],
                device_id=to, device_id_type=MESH))
        for cp in cps:
            cp.start()
        for cp in cps:
            cp.wait()
        total = jnp.zeros((r, W), F32)
        for d in range(8):
            slot = jnp.maximum((me ^ d) - 1, 0)
            total = total + jnp.where(me == d, v_ref[...], land_ref[slot])
        o_ref[...] = total

    return pl.pallas_call(
        body, name="allreduce_small",
        in_specs=[pl.BlockSpec(memory_space=pltpu.VMEM)], out_specs=pl.BlockSpec(memory_space=pltpu.VMEM),
        out_shape=jax.ShapeDtypeStruct((r, W), F32),
        scratch_shapes=[pltpu.VMEM((7, r, W), F32), pltpu.SemaphoreType.DMA((7,)), pltpu.SemaphoreType.DMA((7,))],
        compiler_params=pltpu.CompilerParams(has_side_effects=True, vmem_limit_bytes=VMEM_LIMIT),
    )(v)


BIG = ("ffn1_w_gate", "ffn1_w_up", "ffn1_w_down", "ffn2_w_gate", "ffn2_w_up", "ffn2_w_down", "w_in", "w_out")
COL_SHARDED = ("ffn1_w_gate", "ffn1_w_up", "ffn2_w_gate", "ffn2_w_up", "w_in")


def _flat_rows(shard_shape):
    return shard_shape[-2] * shard_shape[-1] // FLAT_W


def _to_quarters(name, full):
    K, N = full.shape
    if name in COL_SHARDED:
        full = full.reshape(K, 4, N // 4).transpose(1, 0, 2)
    return full.reshape(4, K * N // 4 // FLAT_W, FLAT_W)


def _from_quarters(name, quarters, K, N):
    if name in COL_SHARDED:
        return quarters.reshape(4, K, N // 4).transpose(1, 0, 2).reshape(K, N)
    return quarters.reshape(K, N)


def kernel(x, ffn1_w_gate, ffn1_w_up, ffn1_w_down, ffn2_w_gate, ffn2_w_up, ffn2_w_down, w_in, pool_w, pool_scale, conv_w, rpb, w_out, ln_g, ln_b, loss_target, m_ffn1_w_gate, m_ffn1_w_up, m_ffn1_w_down, m_ffn2_w_gate, m_ffn2_w_up, m_ffn2_w_down, m_w_in, m_pool_w, m_pool_scale, m_conv_w, m_rpb, m_w_out, m_ln_g, m_ln_b, v_ffn1_w_gate, v_ffn1_w_up, v_ffn1_w_down, v_ffn2_w_gate, v_ffn2_w_up, v_ffn2_w_down, v_w_in, v_pool_w, v_pool_scale, v_conv_w, v_rpb, v_w_out, v_ln_g, v_ln_b):
    weights = dict(ffn1_w_gate=ffn1_w_gate, ffn1_w_up=ffn1_w_up, ffn1_w_down=ffn1_w_down, ffn2_w_gate=ffn2_w_gate,
                   ffn2_w_up=ffn2_w_up, ffn2_w_down=ffn2_w_down, w_in=w_in, pool_w=pool_w, pool_scale=pool_scale,
                   conv_w=conv_w, rpb=rpb, w_out=w_out, ln_g=ln_g, ln_b=ln_b)
    mom_m = dict(ffn1_w_gate=m_ffn1_w_gate, ffn1_w_up=m_ffn1_w_up, ffn1_w_down=m_ffn1_w_down, ffn2_w_gate=m_ffn2_w_gate,
                 ffn2_w_up=m_ffn2_w_up, ffn2_w_down=m_ffn2_w_down, w_in=m_w_in, pool_w=m_pool_w,
                 pool_scale=m_pool_scale, conv_w=m_conv_w, rpb=m_rpb, w_out=m_w_out, ln_g=m_ln_g, ln_b=m_ln_b)
    mom_v = dict(ffn1_w_gate=v_ffn1_w_gate, ffn1_w_up=v_ffn1_w_up, ffn1_w_down=v_ffn1_w_down, ffn2_w_gate=v_ffn2_w_gate,
                 ffn2_w_up=v_ffn2_w_up, ffn2_w_down=v_ffn2_w_down, w_in=v_w_in, pool_w=v_pool_w,
                 pool_scale=v_pool_scale, conv_w=v_conv_w, rpb=v_rpb, w_out=v_w_out, ln_g=v_ln_g, ln_b=v_ln_b)
    order = list(weights)
    L = ffn1_w_gate.shape[0]
    xi, yi, ci = lax.axis_index("x"), lax.axis_index("y"), lax.axis_index("c")
    q_me = 2 * xi + yi
    x2 = x[0]
    target = loss_target[0]
    D = x2.shape[1]

    rows_of = {n: _flat_rows(weights[n].shape) for n in BIG}
    per_layer = sum(rows_of.values())
    wflat = jnp.concatenate([weights[n].astype(BF).reshape(L, rows_of[n], FLAT_W) for n in BIG], axis=1)
    wflat = wflat.reshape(L * per_layer, FLAT_W)
    small = jnp.concatenate([ln_g.reshape(-1), ln_b.reshape(-1), conv_w.reshape(-1)])
    n_small = small.shape[0]
    small_rows = -(-n_small // (8 * 128)) * 8
    small = jnp.pad(small, (0, small_rows * 128 - n_small)).reshape(small_rows, 128)
    gathered, small_all = _gather_weights(wflat, small)
    gathered = lax.dynamic_update_slice(gathered, wflat[None], (q_me, 0, 0))
    gathered = gathered.reshape(4, L, per_layer, FLAT_W)
    full_shape = {}
    for n in BIG:
        k, nn = weights[n].shape[1:]
        full_shape[n] = (k, nn * 4) if n in COL_SHARDED else (k * 4, nn)
    offs = dict(zip(BIG, np.cumsum([0] + [rows_of[n] for n in BIG])[:-1]))

    def layer_weight(n, l):
        return _from_quarters(n, gathered[:, l, offs[n]:offs[n] + rows_of[n]], *full_shape[n])

    small_all = small_all.reshape(4, small_rows * 128)[:, :n_small]
    dq4 = D // 4
    n_ln = L * 3 * dq4
    ln_g_all = small_all[:, :n_ln].reshape(4, L, 3, dq4).transpose(1, 2, 0, 3).reshape(L, 3, D)
    ln_b_all = small_all[:, n_ln:2 * n_ln].reshape(4, L, 3, dq4).transpose(1, 2, 0, 3).reshape(L, 3, D)
    conv_all = small_all[:, 2 * n_ln:].reshape(4, L, 3, D_CONV // 4).transpose(1, 2, 0, 3).reshape(L, 3, D_CONV)

    ng = len(POOL_WINDOWS)
    pg = D_POOL // ng
    saved = []
    h = x2
    for l in range(L):
        W = {n: layer_weight(n, l) for n in BIG}
        wblk = jnp.zeros((D_POOL, D_POOL), F32)
        for gi in range(ng):
            wblk = wblk.at[gi * pg:(gi + 1) * pg, gi * pg:(gi + 1) * pg].set(pool_w[l, gi])
        wblk = wblk.astype(BF)
        vec = jnp.concatenate([pool_scale[l][None], conv_all[l], jnp.zeros((4, D_POOL), F32)], axis=0)
        bias = _bias_table(rpb[l])
        lg = [ln_g_all[l, j][None] for j in range(3)]
        lb = [ln_b_all[l, j][None] for j in range(3)]
        x1, x1b, z1, g1, u1 = _ffn_fwd(h, W["ffn1_w_gate"], W["ffn1_w_up"], W["ffn1_w_down"], lg[0], lb[0])
        pabc, qkv = _proj(x1b, W["w_in"])
        yab = _mixab_fwd(pabc, wblk, vec)
        yc = _attn_fwd(qkv, bias)
        xm, xmb, zm = _mixout_fwd(yab, yc, x1, W["w_out"], lg[1], lb[1])
        x3, x3b, z3, g3, u3 = _ffn_fwd(xm, W["ffn2_w_gate"], W["ffn2_w_up"], W["ffn2_w_down"], lg[2], lb[2])
        saved.append(dict(W=W, wblk=wblk, vec=vec, bias=bias, lg=lg, hb=h.astype(BF) if l == 0 else hb_prev,
                          z1=z1, g1=g1, u1=u1, x1b=x1b, pabc=pabc, qkv=qkv, yab=yab, yc=yc, zm=zm, xmb=xmb,
                          z3=z3, g3=g3, u3=u3))
        h, hb_prev = x3, x3b

    loss_tile, dh = _loss_head(h, target)
    loss = lax.psum(loss_tile[0, 0], ("x", "y", "c"))

    grads_big = [None] * L
    g_small = dict(pool_w=[None] * L, pool_scale=[None] * L, conv_w=[None] * L, rpb=[None] * L, ln_g=[None] * L,
                   ln_b=[None] * L)
    for l in reversed(range(L)):
        sv = saved[l]
        W = sv["W"]
        G = {}
        dxm, df, dg, du, a, ln3 = _ffn_bwd(dh, sv["z3"], sv["g3"], sv["u3"], W["ffn2_w_gate"], W["ffn2_w_up"],
                                           W["ffn2_w_down"], sv["lg"][2])
        G["ffn2_w_gate"] = _wgrad(sv["xmb"], dg, 1408)
        G["ffn2_w_up"] = _wgrad(sv["xmb"], du, 1408)
        G["ffn2_w_down"] = _wgrad(a, df, 512)
        dres, dzb, dycat, ln2 = _mixout_bwd(dxm, sv["zm"], W["w_out"], sv["lg"][1])
        G["w_out"] = _wgrad(jnp.concatenate([sv["yab"], sv["yc"]], axis=1), dzb, 1024)
        dpabc, dwblk, dvec = _mixab_bwd(sv["pabc"], dycat, sv["wblk"], sv["vec"])
        dq, dk, dv, dbias = _attn_bwd(sv["qkv"], sv["bias"], dycat)
        dproj = jnp.concatenate([dpabc, dq, dk.astype(BF), dv.astype(BF)], axis=1)
        G["w_in"] = _wgrad(sv["x1b"], dproj, 1280)
        dx1 = _proj_bwd(dres, dproj, W["w_in"])
        dh, df, dg, du, a, ln1 = _ffn_bwd(dx1, sv["z1"], sv["g1"], sv["u1"], W["ffn1_w_gate"], W["ffn1_w_up"],
                                          W["ffn1_w_down"], sv["lg"][0])
        G["ffn1_w_gate"] = _wgrad(sv["hb"], dg, 1408)
        G["ffn1_w_up"] = _wgrad(sv["hb"], du, 1408)
        G["ffn1_w_down"] = _wgrad(a, df, 512)
        grads_big[l] = jnp.concatenate([_to_quarters(n, G[n]) for n in BIG], axis=1)
        g_small["pool_w"][l] = jnp.stack([dwblk[gi * pg:(gi + 1) * pg, gi * pg:(gi + 1) * pg] for gi in range(ng)])
        g_small["pool_scale"][l] = dvec[0]
        g_small["conv_w"][l] = dvec[1:4]
        g_small["rpb"][l] = _bias_grad(dbias)
        g_small["ln_g"][l] = jnp.stack([ln1[0], ln2[0], ln3[0]])
        g_small["ln_b"][l] = jnp.stack([ln1[1], ln2[1], ln3[1]])
    grad_x = dh[None]

    hl = L // 2
    pair = [jnp.concatenate(grads_big[i * hl:(i + 1) * hl], axis=1) for i in range(2)]
    keep = jnp.where(ci == 0, pair[0], pair[1])
    send = jnp.where(ci == 0, pair[1], pair[0]).astype(BF)
    recv = _swap_sibling(send)
    chip_f, chip_b = _add_chip(keep, recv)
    from_chips = _scatter_chips(chip_b)
    mine = _add_final(q_me.reshape(1).astype(jnp.int32), chip_f, from_chips)
    both = lax.dynamic_update_slice(_join_halves(mine), mine[None], (ci, 0, 0)).reshape(L, per_layer, FLAT_W)

    small_names = ("pool_w", "pool_scale", "conv_w", "rpb", "ln_g", "ln_b")
    small_full = {n: jnp.stack(g_small[n]) for n in small_names}
    vflat = jnp.concatenate([small_full[n].reshape(-1) for n in small_names])
    n_v = vflat.shape[0]
    v_rows = -(-n_v // (8 * FLAT_W)) * 8
    vsum = _allreduce_small(jnp.pad(vflat, (0, v_rows * FLAT_W - n_v)).reshape(v_rows, FLAT_W)).reshape(-1)
    grads = {}
    off = 0
    for n in small_names:
        sz = int(np.prod(small_full[n].shape))
        grads[n] = vsum[off:off + sz].reshape(small_full[n].shape)
        off += sz
    for n in ("conv_w", "ln_g", "ln_b"):
        width = weights[n].shape[-1]
        grads[n] = lax.dynamic_slice_in_dim(grads[n], q_me * width, width, axis=2)
    for n in BIG:
        grads[n] = both[:, offs[n]:offs[n] + rows_of[n]].reshape(weights[n].shape)

    delta, new_m, new_v = {}, {}, {}
    for n in order:
        delta[n], new_m[n], new_v[n] = _adamw(weights[n], grads[n], mom_m[n], mom_v[n])
    return (loss, grad_x, *[grads[n] for n in order], *[delta[n] for n in order], *[new_m[n] for n in order],
            *[new_v[n] for n in order])
```

```python
import numpy as np
import jax
import jax.numpy as jnp
from jax import lax
from jax.experimental import pallas as pl
from jax.experimental.pallas import tpu as pltpu

BF = jnp.bfloat16
F32 = jnp.float32
MESH = pl.DeviceIdType.MESH

DEPTH = 4
ALPHA = (2.0 * DEPTH) ** 0.25
LN_EPS = 1e-5
NEG_INF = -1e30
GRID_W = 64
NA_ROWS = 8
NA_COLS = 16
NA_HEADS = 8
HEAD_DIM = 64
D_POOL = 256
D_CONV = 256
D_NA = 512
POOL_WINDOWS = (2, 4, 8, 16)
HALO = 8
ADAM_LR, ADAM_B1, ADAM_B2, ADAM_EPS, ADAM_WD, ADAM_STEP = 0.001, 0.9, 0.999, 1e-08, 0.01, 10
VMEM_LIMIT = 56 * 1024 * 1024
NQ = 4


def _cp(n_axes):
    return pltpu.CompilerParams(dimension_semantics=("arbitrary",) * n_axes, vmem_limit_bytes=VMEM_LIMIT)


def _full(shape):
    nd = len(shape)
    return pl.BlockSpec(shape, lambda *_: (0,) * nd)


def _quarters(arr, l, k=None):
    if k is None:
        return pl.BlockSpec((NQ, None) + arr.shape[2:], lambda *_: (0, l, 0, 0), pipeline_mode=pl.Buffered(1))
    return pl.BlockSpec((NQ, None, None) + arr.shape[3:], lambda *_: (0, l, k, 0, 0), pipeline_mode=pl.Buffered(1))


def _nt(a, b):
    return lax.dot_general(a, b, (((1,), (1,)), ((), ())), preferred_element_type=F32)


def _tn(a, b):
    return lax.dot_general(a, b, (((0,), (0,)), ((), ())), preferred_element_type=F32)


def _nn(a, b):
    return jnp.dot(a, b, preferred_element_type=F32)


def _ln_fwd(z, g, b):
    mu = jnp.mean(z, axis=-1, keepdims=True)
    zc = z - mu
    var = jnp.mean(zc * zc, axis=-1, keepdims=True)
    return zc * lax.rsqrt(var + LN_EPS) * g + b


def _ln_bwd(dy, z, g):
    mu = jnp.mean(z, axis=-1, keepdims=True)
    zc = z - mu
    var = jnp.mean(zc * zc, axis=-1, keepdims=True)
    rstd = lax.rsqrt(var + LN_EPS)
    xhat = zc * rstd
    gdy = dy * g
    m1 = jnp.mean(gdy, axis=-1, keepdims=True)
    m2 = jnp.mean(gdy * xhat, axis=-1, keepdims=True)
    return rstd * (gdy - m1 - xhat * m2), xhat


def _acc_ln_grads(acc_ref, first, dy, xhat):
    @pl.when(first)
    def _():
        acc_ref[...] = jnp.zeros_like(acc_ref)
    acc_ref[0:1, :] += jnp.sum(dy * xhat, axis=0, keepdims=True)
    acc_ref[1:2, :] += jnp.sum(dy, axis=0, keepdims=True)


def _ffn_fwd(x, wa, wb, lg, lb, l, kg, ku, kd):
    S, D = x.shape
    fq = wa.shape[-1]
    tm = min(256, S)

    def body(x_ref, wg_ref, wu_ref, wd_ref, lg_ref, lb_ref, xo_ref, xb_ref, z_ref, g_ref, u_ref):
        x = x_ref[...]
        xb = x.astype(BF)
        acc = jnp.zeros((tm, D), F32)
        for q in range(NQ):
            g = _nn(xb, wg_ref[q])
            u = _nn(xb, wu_ref[q])
            g_ref[q] = g.astype(BF)
            u_ref[q] = u.astype(BF)
            a = g * jax.nn.sigmoid(g) * u
            acc = acc + _nn(a.astype(BF), wd_ref[q])
        z = ALPHA * x + 0.5 * acc
        xo = _ln_fwd(z, lg_ref[...], lb_ref[...])
        z_ref[...] = z
        xo_ref[...] = xo
        xb_ref[...] = xo.astype(BF)

    row = pl.BlockSpec((tm, D), lambda i: (i, 0))
    qrow = pl.BlockSpec((NQ, tm, fq), lambda i: (0, i, 0))
    return pl.pallas_call(
        body, name=f"ffn_fwd_l{l}k{kd}", grid=(S // tm,),
        in_specs=[row, _quarters(wa, l, kg), _quarters(wa, l, ku), _quarters(wb, l, kd), _full((1, D)), _full((1, D))],
        out_specs=[row, row, row, qrow, qrow],
        out_shape=[jax.ShapeDtypeStruct((S, D), F32), jax.ShapeDtypeStruct((S, D), BF),
                   jax.ShapeDtypeStruct((S, D), F32), jax.ShapeDtypeStruct((NQ, S, fq), BF),
                   jax.ShapeDtypeStruct((NQ, S, fq), BF)],
        compiler_params=_cp(1),
    )(x, wa, wa, wb, lg, lb)


def _ffn_bwd(dxo, z, g, u, wa, wb, lg, l, kg, ku, kd):
    S, D = dxo.shape
    fq = wa.shape[-1]
    tm = min(256, S)

    def body(dxo_ref, z_ref, g_ref, u_ref, wg_ref, wu_ref, wd_ref, lg_ref,
             dx_ref, df_ref, dg_ref, du_ref, a_ref, ln_ref):
        dy = dxo_ref[...]
        dz, xhat = _ln_bwd(dy, z_ref[...], lg_ref[...])
        _acc_ln_grads(ln_ref, pl.program_id(0) == 0, dy, xhat)
        dfb = (0.5 * dz).astype(BF)
        df_ref[...] = dfb
        acc = ALPHA * dz
        for q in range(NQ):
            da = _nt(dfb, wd_ref[q])
            gg = g_ref[q].astype(F32)
            uu = u_ref[q].astype(F32)
            sg = jax.nn.sigmoid(gg)
            silu = gg * sg
            a_ref[q] = (silu * uu).astype(BF)
            dgb = (da * uu * (sg * (1.0 + gg * (1.0 - sg)))).astype(BF)
            dub = (da * silu).astype(BF)
            dg_ref[q] = dgb
            du_ref[q] = dub
            acc = acc + _nt(dgb, wg_ref[q]) + _nt(dub, wu_ref[q])
        dx_ref[...] = acc

    row = pl.BlockSpec((tm, D), lambda i: (i, 0))
    qrow = pl.BlockSpec((NQ, tm, fq), lambda i: (0, i, 0))
    qshape = jax.ShapeDtypeStruct((NQ, S, fq), BF)
    return pl.pallas_call(
        body, name=f"ffn_bwd_l{l}k{kd}", grid=(S // tm,),
        in_specs=[row, row, qrow, qrow, _quarters(wa, l, kg), _quarters(wa, l, ku), _quarters(wb, l, kd),
                  _full((1, D))],
        out_specs=[row, row, qrow, qrow, qrow, _full((8, D))],
        out_shape=[jax.ShapeDtypeStruct((S, D), F32), jax.ShapeDtypeStruct((S, D), BF), qshape, qshape, qshape,
                   jax.ShapeDtypeStruct((8, D), F32)],
        compiler_params=_cp(1),
    )(dxo, z, g, u, wa, wa, wb, lg)


def _wgrad_gate_up(a, dg, du):
    S, K = a.shape
    n = dg.shape[-1]
    ts = min(512, S)

    def body(a_ref, g_ref, u_ref, o_ref):
        @pl.when(pl.program_id(1) == 0)
        def _():
            o_ref[...] = jnp.zeros_like(o_ref)
        av = a_ref[...]
        o_ref[0:K, :] += _tn(av, g_ref[...])
        o_ref[K:2 * K, :] += _tn(av, u_ref[...])

    bspec = pl.BlockSpec((None, ts, n), lambda q, s: (q, s, 0))
    return pl.pallas_call(
        body, name="wgrad_gate_up", grid=(NQ, S // ts),
        in_specs=[pl.BlockSpec((ts, K), lambda q, s: (s, 0)), bspec, bspec],
        out_specs=pl.BlockSpec((None, 2 * K, n), lambda q, s: (q, 0, 0)),
        out_shape=jax.ShapeDtypeStruct((NQ, 2 * K, n), F32),
        compiler_params=_cp(2),
    )(a, dg, du)


def _wgrad_down(a, df):
    _, S, k = a.shape
    N = df.shape[1]
    ts = min(512, S)

    def body(a_ref, b_ref, o_ref):
        @pl.when(pl.program_id(1) == 0)
        def _():
            o_ref[...] = jnp.zeros_like(o_ref)
        o_ref[...] += _tn(a_ref[...], b_ref[...])

    return pl.pallas_call(
        body, name="wgrad_down", grid=(NQ, S // ts),
        in_specs=[pl.BlockSpec((None, ts, k), lambda q, s: (q, s, 0)), pl.BlockSpec((ts, N), lambda q, s: (s, 0))],
        out_specs=pl.BlockSpec((None, k, N), lambda q, s: (q, 0, 0)),
        out_shape=jax.ShapeDtypeStruct((NQ, k, N), F32),
        compiler_params=_cp(2),
    )(a, df)


def _wgrad_cols(a, b, name):
    S, K = a.shape
    n = b.shape[1] // NQ
    ts = min(512, S)

    def body(a_ref, b_ref, o_ref):
        @pl.when(pl.program_id(1) == 0)
        def _():
            o_ref[...] = jnp.zeros_like(o_ref)
        o_ref[...] += _tn(a_ref[...], b_ref[...])

    return pl.pallas_call(
        body, name=name, grid=(NQ, S // ts),
        in_specs=[pl.BlockSpec((ts, K), lambda q, s: (s, 0)), pl.BlockSpec((ts, n), lambda q, s: (s, q))],
        out_specs=pl.BlockSpec((None, K, n), lambda q, s: (q, 0, 0)),
        out_shape=jax.ShapeDtypeStruct((NQ, K, n), F32),
        compiler_params=_cp(2),
    )(a, b)


def _wgrad_out(yab, yc, dzb):
    S, h = yab.shape
    D = dzb.shape[1]
    k = h // 2
    ts = min(512, S)

    def body(yab_ref, yc_ref, b_ref, o_ref):
        @pl.when(pl.program_id(0) == 0)
        def _():
            o_ref[...] = jnp.zeros_like(o_ref)
        b = b_ref[...]
        o_ref[0] += _tn(yab_ref[:, 0:k], b)
        o_ref[1] += _tn(yab_ref[:, k:h], b)
        o_ref[2] += _tn(yc_ref[:, 0:k], b)
        o_ref[3] += _tn(yc_ref[:, k:h], b)

    row = lambda w: pl.BlockSpec((ts, w), lambda s: (s, 0))
    return pl.pallas_call(
        body, name="wgrad_out", grid=(S // ts,),
        in_specs=[row(h), row(h), row(D)], out_specs=_full((NQ, k, D)),
        out_shape=jax.ShapeDtypeStruct((NQ, k, D), F32),
        compiler_params=_cp(1),
    )(yab, yc, dzb)


def _proj(xb, wc, l):
    S, D = xb.shape
    n = wc.shape[-1]
    n1 = D_POOL + 3 * D_CONV
    n2 = NQ * n - n1
    tm = min(512, S)

    def body(x_ref, w_ref, p_ref, qkv_ref):
        x = x_ref[...]
        for q in range(NQ):
            r = _nn(x, w_ref[q])
            lo, hi = q * n, (q + 1) * n
            if hi <= n1:
                p_ref[:, lo:hi] = r
            elif lo >= n1:
                qkv_ref[:, lo - n1:hi - n1] = r.astype(BF)
            else:
                p_ref[:, lo:n1] = r[:, 0:n1 - lo]
                qkv_ref[:, 0:hi - n1] = r[:, n1 - lo:n].astype(BF)

    row = lambda w: pl.BlockSpec((tm, w), lambda i: (i, 0))
    return pl.pallas_call(
        body, name=f"mix_proj_l{l}", grid=(S // tm,),
        in_specs=[row(D), _quarters(wc, l)],
        out_specs=[row(n1), row(n2)],
        out_shape=[jax.ShapeDtypeStruct((S, n1), F32), jax.ShapeDtypeStruct((S, n2), BF)],
        compiler_params=_cp(1),
    )(xb, wc)


def _mm_exact(a, b, name):
    def body(a_ref, b_ref, o_ref):
        o_ref[...] = jnp.dot(a_ref[...], b_ref[...], preferred_element_type=F32, precision=lax.Precision.HIGHEST)

    return pl.pallas_call(
        body, name=name, in_specs=[_full(a.shape), _full(b.shape)], out_specs=_full((a.shape[0], b.shape[1])),
        out_shape=jax.ShapeDtypeStruct((a.shape[0], b.shape[1]), F32),
        compiler_params=pltpu.CompilerParams(vmem_limit_bytes=VMEM_LIMIT),
    )(a, b)


def _bias_constants():
    c = np.arange(GRID_W)
    col_start = np.clip(c - NA_COLS // 2, 0, GRID_W - NA_COLS)
    valid = (c[None, :] >= col_start[:, None]) & (c[None, :] < col_start[:, None] + NA_COLS)
    dc = np.clip(c[None, :] - c[:, None], -(NA_COLS - 1), NA_COLS - 1) + (NA_COLS - 1)
    onehot = np.zeros((32, GRID_W * GRID_W), np.float32)
    onehot[dc.reshape(-1), np.arange(GRID_W * GRID_W)] = 1.0
    mask_kq = np.where(valid.T, 0.0, NEG_INF).astype(np.float32)
    mask = np.tile(mask_kq, (2 * NA_ROWS - 1, 2))
    return onehot, mask


def _bias_table(rpb):
    onehot, mask = _bias_constants()
    nr = 2 * NA_ROWS - 1
    r2 = jnp.pad(rpb.reshape(NA_HEADS * nr, 2 * NA_COLS - 1), ((0, 0), (0, 1)))
    t = _mm_exact(r2, jnp.asarray(onehot), "bias_expand")
    t = t.reshape(NA_HEADS // 2, 2, nr, GRID_W, GRID_W).transpose(0, 2, 4, 1, 3)
    return t.reshape(NA_HEADS // 2, nr * GRID_W, 2 * GRID_W) + jnp.asarray(mask)[None]


def _bias_grad(dt):
    onehot, _ = _bias_constants()
    nr = 2 * NA_ROWS - 1
    d = dt.reshape(NA_HEADS // 2, nr, GRID_W, 2, GRID_W).transpose(0, 3, 1, 4, 2).reshape(NA_HEADS * nr, -1)
    g = _mm_exact(d, jnp.asarray(onehot.T.copy()), "bias_reduce")
    return g[:, :2 * NA_COLS - 1].reshape(NA_HEADS, nr, 2 * NA_COLS - 1)


def _attn_rows(S):
    rows = S // GRID_W
    rb = min(16, rows)
    return rows, rb


def _attn_step(r, rows, q, k_ref, v_ref, b_ref, m_a):
    rs = jnp.clip(r - NA_ROWS // 2, 0, rows - NA_ROWS)
    s0 = rs - r + (NA_ROWS - 1)
    zero = jnp.zeros_like(q)
    q2 = jnp.concatenate([jnp.where(m_a, q, zero), jnp.where(m_a, zero, q)], axis=0)
    ks = pl.ds(pl.multiple_of(rs * GRID_W, GRID_W), NA_ROWS * GRID_W)
    kb = k_ref[ks, :]
    vb = v_ref[ks, :]
    bs = pl.ds(pl.multiple_of(s0 * GRID_W, GRID_W), NA_ROWS * GRID_W)
    s = _nt(kb, q2) * (HEAD_DIM ** -0.5) + b_ref[0, bs, :]
    m = jnp.max(s, axis=0, keepdims=True)
    p = jnp.exp(s - m)
    p = p / jnp.sum(p, axis=0, keepdims=True)
    return p, q2, kb, vb, ks, bs


def _attn_fwd(qkv, bias):
    S = qkv.shape[0]
    rows, rb = _attn_rows(S)
    tq = rb * GRID_W
    hp = NA_HEADS // 2

    def body(q_ref, k_ref, v_ref, b_ref, o_ref):
        base = pl.program_id(1) * rb
        m_a = lax.broadcasted_iota(jnp.int32, (GRID_W, 128), 1) < HEAD_DIM

        def step(i, carry):
            qs = pl.ds(pl.multiple_of(i * GRID_W, GRID_W), GRID_W)
            p, _, _, vb, _, _ = _attn_step(base + i, rows, q_ref[qs, :], k_ref, v_ref, b_ref, m_a)
            o2 = _tn(p.astype(BF), vb)
            o_ref[qs, :] = jnp.where(m_a, o2[:GRID_W], o2[GRID_W:]).astype(BF)
            return carry

        lax.fori_loop(0, rb, step, 0, unroll=2)

    return pl.pallas_call(
        body, name="attn_fwd", grid=(hp, rows // rb),
        in_specs=[pl.BlockSpec((tq, 128), lambda h, r: (r, h)),
                  pl.BlockSpec((S, 128), lambda h, r: (0, hp + h)),
                  pl.BlockSpec((S, 128), lambda h, r: (0, 2 * hp + h)),
                  pl.BlockSpec((1, bias.shape[1], 128), lambda h, r: (h, 0, 0))],
        out_specs=pl.BlockSpec((tq, 128), lambda h, r: (r, h)),
        out_shape=jax.ShapeDtypeStruct((S, D_NA), BF),
        compiler_params=_cp(2),
    )(qkv, qkv, qkv, bias)


def _attn_bwd(qkv, bias, dycat):
    S = qkv.shape[0]
    rows, rb = _attn_rows(S)
    tq = rb * GRID_W
    hp = NA_HEADS // 2
    scale = HEAD_DIM ** -0.5

    def body(q_ref, k_ref, v_ref, b_ref, do_ref, dq_ref, dk_ref, dv_ref, db_ref, dka_ref, dva_ref):
        base = pl.program_id(1) * rb
        last = pl.program_id(1) == pl.num_programs(1) - 1
        m_a = lax.broadcasted_iota(jnp.int32, (GRID_W, 128), 1) < HEAD_DIM

        @pl.when(pl.program_id(1) == 0)
        def _():
            dka_ref[...] = jnp.zeros_like(dka_ref)
            dva_ref[...] = jnp.zeros_like(dva_ref)
            db_ref[...] = jnp.zeros_like(db_ref)

        def step(i, carry):
            qs = pl.ds(pl.multiple_of(i * GRID_W, GRID_W), GRID_W)
            p, q2, kb, vb, ks, bs = _attn_step(base + i, rows, q_ref[qs, :], k_ref, v_ref, b_ref, m_a)
            do = do_ref[qs, :].astype(BF)
            zero = jnp.zeros_like(do)
            do2 = jnp.concatenate([jnp.where(m_a, do, zero), jnp.where(m_a, zero, do)], axis=0)
            dp = _nt(vb, do2)
            ds = p * (dp - jnp.sum(p * dp, axis=0, keepdims=True))
            db_ref[0, bs, :] += ds
            dsb = ds.astype(BF)
            dq2 = _tn(dsb, kb) * scale
            dq_ref[qs, :] = jnp.where(m_a, dq2[:GRID_W], dq2[GRID_W:]).astype(BF)
            dka_ref[ks, :] += _nn(dsb, q2) * scale
            dva_ref[ks, :] += _nn(p.astype(BF), do2)
            return carry

        lax.fori_loop(0, rb, step, 0, unroll=2)

        @pl.when(last)
        def _():
            dk_ref[...] = dka_ref[...].astype(BF)
            dv_ref[...] = dva_ref[...].astype(BF)

    nb = bias.shape[1]
    return pl.pallas_call(
        body, name="attn_bwd", grid=(hp, rows // rb),
        in_specs=[pl.BlockSpec((tq, 128), lambda h, r: (r, h)),
                  pl.BlockSpec((S, 128), lambda h, r: (0, hp + h)),
                  pl.BlockSpec((S, 128), lambda h, r: (0, 2 * hp + h)),
                  pl.BlockSpec((1, nb, 128), lambda h, r: (h, 0, 0)),
                  pl.BlockSpec((tq, 128), lambda h, r: (r, hp + h))],
        out_specs=[pl.BlockSpec((tq, 128), lambda h, r: (r, h)),
                   pl.BlockSpec((S, 128), lambda h, r: (0, h)),
                   pl.BlockSpec((S, 128), lambda h, r: (0, h)),
                   pl.BlockSpec((1, nb, 128), lambda h, r: (h, 0, 0))],
        out_shape=[jax.ShapeDtypeStruct((S, D_NA), BF)] * 3 + [jax.ShapeDtypeStruct((hp, nb, 128), F32)],
        scratch_shapes=[pltpu.VMEM((S, 128), F32), pltpu.VMEM((S, 128), F32)],
        compiler_params=_cp(2),
    )(qkv, qkv, qkv, bias, dycat)


def _halo_specs(tm, width, S):
    hb = tm // HALO
    last = S // HALO - 1
    return [pl.BlockSpec((tm, width), lambda i: (i, 0)),
            pl.BlockSpec((HALO, width), lambda i: (jnp.maximum(i * hb - 1, 0), 0)),
            pl.BlockSpec((HALO, width), lambda i: (jnp.minimum((i + 1) * hb, last), 0))]


def _with_halo(cur_ref, prev_ref, next_ref, i, nt):
    prev = jnp.where(i > 0, prev_ref[...], 0.0)
    nxt = jnp.where(i < nt - 1, next_ref[...], 0.0)
    return jnp.concatenate([prev, cur_ref[...], nxt], axis=0)


def _shift(a, k):
    n = a.shape[0]
    return pltpu.roll(a, k % n, 0)


def _pool_lanes(n):
    lane = lax.broadcasted_iota(jnp.int32, (n, D_POOL), 1)
    group = D_POOL // len(POOL_WINDOWS)
    return [lane < group * (j + 1) for j in range(len(POOL_WINDOWS) - 1)]


def _by_window(lanes, vals):
    return jnp.where(lanes[0], vals[0], jnp.where(lanes[1], vals[1], jnp.where(lanes[2], vals[2], vals[3])))


def _pool_count(lanes, t, S):
    back = _by_window(lanes, tuple(w // 2 for w in POOL_WINDOWS))
    lo = jnp.maximum(t - back, 0)
    hi = jnp.minimum(t + back, S)
    return jnp.maximum(hi - lo, 1).astype(F32)


def _pool_p(u, lanes, cnt):
    a = u + _shift(u, 1)
    b = _shift(a, 1) + _shift(a, -1)
    c = _shift(b, 2) + _shift(b, -2)
    d = _shift(c, 4) + _shift(c, -4)
    return _by_window(lanes, (a, b, c, d)) / cnt - u


def _mixab_fwd(pabc, wblk, vec):
    S = pabc.shape[0]
    tm = min(512, S)
    nt = S // tm
    n = tm + 2 * HALO
    tile = slice(HALO, HALO + tm)

    def body(cur_ref, prev_ref, next_ref, w_ref, vec_ref, o_ref):
        i = pl.program_id(0)
        ext = _with_halo(cur_ref, prev_ref, next_ref, i, nt)
        lanes = _pool_lanes(n)
        t = i * tm - HALO + lax.broadcasted_iota(jnp.int32, (n, D_POOL), 0)
        p = _pool_p(ext[:, 0:D_POOL], lanes, _pool_count(lanes, t, S))[tile]
        o_ref[:, 0:D_POOL] = (_nn(p.astype(BF), w_ref[...]) * vec_ref[0:1, :]).astype(BF)
        zc = ext[:, 512:768] * ext[:, 768:1024]
        conv = vec_ref[1:2, :] * _shift(zc, 1) + vec_ref[2:3, :] * zc + vec_ref[3:4, :] * _shift(zc, -1)
        o_ref[:, D_POOL:D_POOL + D_CONV] = (ext[tile, 256:512] * conv[tile]).astype(BF)

    return pl.pallas_call(
        body, name="mixab_fwd", grid=(nt,),
        in_specs=_halo_specs(tm, 1024, S) + [_full((D_POOL, D_POOL)), _full((8, D_POOL))],
        out_specs=pl.BlockSpec((tm, D_POOL + D_CONV), lambda i: (i, 0)),
        out_shape=jax.ShapeDtypeStruct((S, D_POOL + D_CONV), BF),
        compiler_params=_cp(1),
    )(pabc, pabc, pabc, wblk, vec)


def _mixab_bwd(pabc, dycat, wblk, vec):
    S = pabc.shape[0]
    tm = min(512, S)
    nt = S // tm
    n = tm + 2 * HALO
    tile = slice(HALO, HALO + tm)

    def body(cur_ref, prev_ref, next_ref, dcur_ref, dprev_ref, dnext_ref, w_ref, vec_ref, o_ref, dw_ref, dvec_ref):
        i = pl.program_id(0)

        @pl.when(i == 0)
        def _():
            dw_ref[...] = jnp.zeros_like(dw_ref)
            dvec_ref[...] = jnp.zeros_like(dvec_ref)

        ext = _with_halo(cur_ref, prev_ref, next_ref, i, nt)
        dext = _with_halo(dcur_ref, dprev_ref, dnext_ref, i, nt)
        lanes = _pool_lanes(n)
        t = i * tm - HALO + lax.broadcasted_iota(jnp.int32, (n, D_POOL), 0)
        cnt = _pool_count(lanes, t, S)
        w = w_ref[...]
        scale = vec_ref[0:1, :]
        pb = _pool_p(ext[:, 0:D_POOL], lanes, cnt)[tile].astype(BF)
        dya = dext[:, 0:D_POOL]
        dvec_ref[0:1, :] += jnp.sum(dya[tile] * _nn(pb, w), axis=0, keepdims=True)
        dqb = (dya * scale).astype(BF)
        dw_ref[...] += _tn(pb, dqb[tile])
        dp = _nt(dqb, w)
        r = dp / cnt
        a = r + _shift(r, -1)
        b = _shift(a, 1) + _shift(a, -1)
        c = _shift(b, 2) + _shift(b, -2)
        d = _shift(c, 4) + _shift(c, -4)
        o_ref[:, 0:256] = (_by_window(lanes, (a, b, c, d)) - dp)[tile].astype(BF)
        gb, gc, hh = ext[:, 256:512], ext[:, 512:768], ext[:, 768:1024]
        zc = gc * hh
        zm, zp = _shift(zc, 1), _shift(zc, -1)
        w0, w1, w2 = vec_ref[1:2, :], vec_ref[2:3, :], vec_ref[3:4, :]
        dyb = dext[:, D_POOL:D_POOL + D_CONV]
        dconv = dyb * gb
        o_ref[:, 256:512] = (dyb * (w0 * zm + w1 * zc + w2 * zp))[tile].astype(BF)
        dzc = w0 * _shift(dconv, -1) + w1 * dconv + w2 * _shift(dconv, 1)
        o_ref[:, 512:768] = (dzc * hh)[tile].astype(BF)
        o_ref[:, 768:1024] = (dzc * gc)[tile].astype(BF)
        dct = dconv[tile]
        dvec_ref[1:2, :] += jnp.sum(dct * zm[tile], axis=0, keepdims=True)
        dvec_ref[2:3, :] += jnp.sum(dct * zc[tile], axis=0, keepdims=True)
        dvec_ref[3:4, :] += jnp.sum(dct * zp[tile], axis=0, keepdims=True)

    return pl.pallas_call(
        body, name="mixab_bwd", grid=(nt,),
        in_specs=_halo_specs(tm, 1024, S) + _halo_specs(tm, 512, S) + [_full((D_POOL, D_POOL)), _full((8, D_POOL))],
        out_specs=[pl.BlockSpec((tm, 1024), lambda i: (i, 0)), _full((D_POOL, D_POOL)), _full((8, D_POOL))],
        out_shape=[jax.ShapeDtypeStruct((S, 1024), BF), jax.ShapeDtypeStruct((D_POOL, D_POOL), F32),
                   jax.ShapeDtypeStruct((8, D_POOL), F32)],
        compiler_params=_cp(1),
    )(pabc, pabc, pabc, dycat, dycat, dycat, wblk, vec)


def _mixout_fwd(yab, yc, x, wo, lg, lb, l):
    S, D = x.shape
    tm = min(512, S)
    h = yab.shape[1]
    k = h // 2

    def body(yab_ref, yc_ref, x_ref, w_ref, lg_ref, lb_ref, xo_ref, xb_ref, z_ref):
        y = (_nn(yab_ref[:, 0:k], w_ref[0]) + _nn(yab_ref[:, k:h], w_ref[1])
             + _nn(yc_ref[:, 0:k], w_ref[2]) + _nn(yc_ref[:, k:h], w_ref[3]))
        z = ALPHA * x_ref[...] + y
        xo = _ln_fwd(z, lg_ref[...], lb_ref[...])
        z_ref[...] = z
        xo_ref[...] = xo
        xb_ref[...] = xo.astype(BF)

    row = lambda w: pl.BlockSpec((tm, w), lambda i: (i, 0))
    return pl.pallas_call(
        body, name=f"mixout_fwd_l{l}", grid=(S // tm,),
        in_specs=[row(h), row(h), row(D), _quarters(wo, l), _full((1, D)), _full((1, D))],
        out_specs=[row(D), row(D), row(D)],
        out_shape=[jax.ShapeDtypeStruct((S, D), F32), jax.ShapeDtypeStruct((S, D), BF),
                   jax.ShapeDtypeStruct((S, D), F32)],
        compiler_params=_cp(1),
    )(yab, yc, x, wo, lg, lb)


def _mixout_bwd(dxo, z, wo, lg, l):
    S, D = dxo.shape
    k = wo.shape[-2]
    tm = min(512, S)

    def body(dxo_ref, z_ref, w_ref, lg_ref, dres_ref, dzb_ref, dy_ref, ln_ref):
        dy = dxo_ref[...]
        dz, xhat = _ln_bwd(dy, z_ref[...], lg_ref[...])
        _acc_ln_grads(ln_ref, pl.program_id(0) == 0, dy, xhat)
        dzb = dz.astype(BF)
        dres_ref[...] = ALPHA * dz
        dzb_ref[...] = dzb
        for q in range(NQ):
            dy_ref[:, q * k:(q + 1) * k] = _nt(dzb, w_ref[q])

    row = lambda w: pl.BlockSpec((tm, w), lambda i: (i, 0))
    return pl.pallas_call(
        body, name=f"mixout_bwd_l{l}", grid=(S // tm,),
        in_specs=[row(D), row(D), _quarters(wo, l), _full((1, D))],
        out_specs=[row(D), row(D), row(NQ * k), _full((8, D))],
        out_shape=[jax.ShapeDtypeStruct((S, D), F32), jax.ShapeDtypeStruct((S, D), BF),
                   jax.ShapeDtypeStruct((S, NQ * k), F32), jax.ShapeDtypeStruct((8, D), F32)],
        compiler_params=_cp(1),
    )(dxo, z, wo, lg)


def _take_cols(refs, lo, hi):
    parts, off = [], 0
    for r in refs:
        w = r.shape[1]
        a, b = max(lo, off), min(hi, off + w)
        if a < b:
            parts.append(r[:, a - off:b - off])
        off += w
    return parts[0] if len(parts) == 1 else jnp.concatenate(parts, axis=1)


def _proj_bwd(dres, dparts, wc, l):
    S, D = dres.shape
    n = wc.shape[-1]
    tm = min(512, S)
    np_ = len(dparts)

    def body(*refs):
        dres_ref, d_refs, w_ref, dx_ref = refs[0], refs[1:1 + np_], refs[1 + np_], refs[2 + np_]
        acc = dres_ref[...]
        for q in range(NQ):
            acc = acc + _nt(_take_cols(d_refs, q * n, (q + 1) * n), w_ref[q])
        dx_ref[...] = acc

    row = lambda w: pl.BlockSpec((tm, w), lambda i: (i, 0))
    return pl.pallas_call(
        body, name=f"mix_proj_bwd_l{l}", grid=(S // tm,),
        in_specs=[row(D)] + [row(d.shape[1]) for d in dparts] + [_quarters(wc, l)],
        out_specs=row(D),
        out_shape=jax.ShapeDtypeStruct((S, D), F32),
        compiler_params=_cp(1),
    )(dres, *dparts, wc)


def _wgrad_in(a, dparts, n):
    S, K = a.shape
    ts = min(512, S)
    np_ = len(dparts)

    def body(*refs):
        a_ref, d_refs, o_ref = refs[0], refs[1:1 + np_], refs[1 + np_]

        @pl.when(pl.program_id(0) == 0)
        def _():
            o_ref[...] = jnp.zeros_like(o_ref)
        av = a_ref[...]
        for q in range(NQ):
            o_ref[q] += _tn(av, _take_cols(d_refs, q * n, (q + 1) * n))

    row = lambda w: pl.BlockSpec((ts, w), lambda s: (s, 0))
    return pl.pallas_call(
        body, name="wgrad_in", grid=(S // ts,),
        in_specs=[row(K)] + [row(d.shape[1]) for d in dparts], out_specs=_full((NQ, K, n)),
        out_shape=jax.ShapeDtypeStruct((NQ, K, n), F32),
        compiler_params=_cp(1),
    )(a, *dparts)


def _loss_head(y, target):
    S, D = y.shape
    tm = min(512, S)

    def body(y_ref, t_ref, l_ref, dy_ref):
        @pl.when(pl.program_id(0) == 0)
        def _():
            l_ref[...] = jnp.zeros_like(l_ref)
        e = y_ref[...] - t_ref[...]
        dy_ref[...] = e * (1.0 / D)
        part = jnp.sum(jnp.sum(e * e, axis=1, keepdims=True) * (1.0 / D), axis=0, keepdims=True)
        l_ref[...] += 0.5 * part

    row = pl.BlockSpec((tm, D), lambda i: (i, 0))
    return pl.pallas_call(
        body, name="loss_head", grid=(S // tm,),
        in_specs=[row, row], out_specs=[_full((8, 128)), row],
        out_shape=[jax.ShapeDtypeStruct((8, 128), F32), jax.ShapeDtypeStruct((S, D), F32)],
        compiler_params=_cp(1),
    )(y, target)


def _adamw(w, g, m, v):
    shape = w.shape
    cols = shape[-1]
    rows = int(np.prod(shape[:-1]))
    w2, g2, m2, v2 = (a.reshape(rows, cols) for a in (w, g, m, v))
    tr = rows
    for cand in (512, 352, 256):
        if rows > cand and rows % cand == 0:
            tr = cand
            break

    def body(w_ref, g_ref, m_ref, v_ref, d_ref, mo_ref, vo_ref):
        g = g_ref[...]
        mn = ADAM_B1 * m_ref[...] + (1.0 - ADAM_B1) * g
        vn = ADAM_B2 * v_ref[...] + (1.0 - ADAM_B2) * (g * g)
        m_hat = mn / (1.0 - ADAM_B1 ** ADAM_STEP)
        v_hat = vn / (1.0 - ADAM_B2 ** ADAM_STEP)
        d_ref[...] = -ADAM_LR * (m_hat / (jnp.sqrt(v_hat) + ADAM_EPS) + ADAM_WD * w_ref[...])
        mo_ref[...] = mn
        vo_ref[...] = vn

    spec = pl.BlockSpec((tr, cols), lambda i: (i, 0))
    outs = pl.pallas_call(
        body, name=f"adamw_{rows}x{cols}", grid=(rows // tr,),
        in_specs=[spec] * 4, out_specs=[spec] * 3,
        out_shape=[jax.ShapeDtypeStruct((rows, cols), F32)] * 3,
        compiler_params=_cp(1),
    )(w2, g2, m2, v2)
    return tuple(o.reshape(shape) for o in outs)


def _half_tile(h):
    return h if h <= 512 else 512


def _add_chip(g, recv):
    _, R, C = g.shape
    h = R // 2
    tr = _half_tile(h)
    nb = h // tr

    def body(a_ref, b_ref, o_ref, ob_ref):
        s = a_ref[...] + b_ref[...]
        o_ref[...] = s
        ob_ref[...] = s.astype(BF)

    half = pl.BlockSpec((1, tr, C), lambda q, i: (q, i, 0))
    mine = pl.BlockSpec((1, tr, C), lambda q, i: (q, lax.axis_index("c") * nb + i, 0))
    return pl.pallas_call(
        body, name=f"rs_add_chip_{R}x{C}", grid=(NQ, nb), in_specs=[mine, half], out_specs=[half, half],
        out_shape=[jax.ShapeDtypeStruct((NQ, h, C), F32), jax.ShapeDtypeStruct((NQ, h, C), BF)],
        compiler_params=_cp(2),
    )(g, recv)


def _add_final(chip, recv):
    _, h, C = chip.shape
    tr = _half_tile(h)
    nb = h // tr

    def body(a_ref, b_ref, o_ref):
        s = a_ref[0]
        for j in range(3):
            s = s + b_ref[j].astype(F32)
        o_ref[...] = s

    return pl.pallas_call(
        body, name=f"rs_add_final_{h}x{C}", grid=(nb,),
        in_specs=[pl.BlockSpec((1, tr, C), lambda i: (2 * lax.axis_index("x") + lax.axis_index("y"), i, 0)),
                  pl.BlockSpec((3, tr, C), lambda i: (0, i, 0))],
        out_specs=pl.BlockSpec((tr, C), lambda i: (lax.axis_index("c") * nb + i, 0)),
        out_shape=jax.ShapeDtypeStruct((2 * h, C), F32),
        compiler_params=_cp(1),
    )(chip, recv)


ANY = pl.BlockSpec(memory_space=pl.ANY)
COMM = pltpu.CompilerParams(has_side_effects=True)


def _place():
    x, y, c = lax.axis_index("x"), lax.axis_index("y"), lax.axis_index("c")
    chips = [(1 - x, y), (x, 1 - y), (1 - x, 1 - y)]
    return x, y, c, chips


def _gather_weights(shards, small):
    n = len(shards)
    hl = shards[0].shape[0] // 2
    sr = small.shape[0]

    def body(*refs):
        s_refs, sm_ref = refs[:n], refs[n]
        o_refs, so_ref = refs[n + 1:2 * n + 1], refs[2 * n + 1]
        send_sems, recv_sems = refs[2 * n + 2:]
        x, y, c, chips = _place()
        q = 2 * x + y
        sibling = (x, y, 1 - c)
        mine = pl.ds(c * hl, hl)
        theirs = pl.ds((1 - c) * hl, hl)

        def big(i, k, quarter, layers, to, src=None):
            dst = o_refs[i].at[quarter, layers]
            return pltpu.make_async_remote_copy(
                src_ref=dst if src is None else src, dst_ref=dst, send_sem=send_sems.at[6 * i + k],
                recv_sem=recv_sems.at[6 * i + k], device_id=to, device_id_type=MESH)

        def sml(j, quarter, to):
            return pltpu.make_async_remote_copy(
                src_ref=sm_ref, dst_ref=so_ref.at[quarter], send_sem=send_sems.at[6 * n + j],
                recv_sem=recv_sems.at[6 * n + j], device_id=to, device_id_type=MESH)

        so_ref[q] = sm_ref[...]
        sends = [big(i, j, q, mine, (*chip, c), src=s_refs[i].at[mine])
                 for i in range(n) for j, chip in enumerate(chips)]
        sends += [sml(j, q, (*chip, c)) for j, chip in enumerate(chips)]
        for cp in sends:
            cp.start()
        passed = []
        for i in range(n):
            for j, (cx, cy) in enumerate(chips):
                big(i, j, 2 * cx + cy, mine, (x, y, c)).wait_recv()
                fwd = big(i, 3 + j, 2 * cx + cy, mine, sibling)
                fwd.start()
                passed.append(fwd)
        for i in range(n):
            for j, (cx, cy) in enumerate(chips):
                big(i, 3 + j, 2 * cx + cy, theirs, (x, y, c)).wait_recv()
        for j, (cx, cy) in enumerate(chips):
            sml(j, 2 * cx + cy, (x, y, c)).wait_recv()
        for cp in sends + passed:
            cp.wait_send()

    vm = pl.BlockSpec(memory_space=pltpu.VMEM)
    outs = pl.pallas_call(
        body, name="gather_weights", in_specs=[ANY] * n + [vm], out_specs=[ANY] * n + [vm],
        out_shape=[jax.ShapeDtypeStruct((NQ,) + s.shape, BF) for s in shards]
        + [jax.ShapeDtypeStruct((NQ, sr, 128), F32)],
        scratch_shapes=[pltpu.SemaphoreType.DMA((6 * n + 3,)), pltpu.SemaphoreType.DMA((6 * n + 3,))],
        compiler_params=COMM,
    )(*shards, small)
    return outs[:n], outs[n]


def _swap_halves(gs):
    n = len(gs)

    def body(*refs):
        g_refs, o_refs, send_sems, recv_sems = refs[:n], refs[n:2 * n], refs[2 * n], refs[2 * n + 1]
        x, y, c, _ = _place()
        cps = []
        for i in range(n):
            h = g_refs[i].shape[1] // 2
            cps.append(pltpu.make_async_remote_copy(
                src_ref=g_refs[i].at[:, pl.ds((1 - c) * h, h), :], dst_ref=o_refs[i], send_sem=send_sems.at[i],
                recv_sem=recv_sems.at[i], device_id=(x, y, 1 - c), device_id_type=MESH))
        for cp in cps:
            cp.start()
        for cp in cps:
            cp.wait()

    return pl.pallas_call(
        body, name="rs_swap_halves", in_specs=[ANY] * n, out_specs=[ANY] * n,
        out_shape=[jax.ShapeDtypeStruct((NQ, g.shape[1] // 2, g.shape[2]), F32) for g in gs],
        scratch_shapes=[pltpu.SemaphoreType.DMA((n,)), pltpu.SemaphoreType.DMA((n,))],
        compiler_params=COMM,
    )(*gs)


def _scatter_chips(chips_b):
    n = len(chips_b)

    def body(*refs):
        s_refs, o_refs, send_sems, recv_sems = refs[:n], refs[n:2 * n], refs[2 * n], refs[2 * n + 1]
        x, y, c, chips = _place()
        cps = [pltpu.make_async_remote_copy(
            src_ref=s_refs[i].at[2 * cx + cy], dst_ref=o_refs[i].at[j], send_sem=send_sems.at[3 * i + j],
            recv_sem=recv_sems.at[3 * i + j], device_id=(cx, cy, c), device_id_type=MESH)
            for i in range(n) for j, (cx, cy) in enumerate(chips)]
        for cp in cps:
            cp.start()
        for cp in cps:
            cp.wait()

    return pl.pallas_call(
        body, name="rs_scatter_chips", in_specs=[ANY] * n, out_specs=[ANY] * n,
        out_shape=[jax.ShapeDtypeStruct((3,) + s.shape[1:], BF) for s in chips_b],
        scratch_shapes=[pltpu.SemaphoreType.DMA((3 * n,)), pltpu.SemaphoreType.DMA((3 * n,))],
        compiler_params=COMM,
    )(*chips_b)


def _join_halves(fs):
    n = len(fs)

    def body(*refs):
        f_refs, o_refs, send_sems, recv_sems = refs[:n], refs[n:2 * n], refs[2 * n], refs[2 * n + 1]
        x, y, c, _ = _place()
        cps = []
        for i in range(n):
            h = f_refs[i].shape[0] // 2
            rows = pl.ds(c * h, h)
            cps.append(pltpu.make_async_remote_copy(
                src_ref=f_refs[i].at[rows, :], dst_ref=o_refs[i].at[rows, :], send_sem=send_sems.at[i],
                recv_sem=recv_sems.at[i], device_id=(x, y, 1 - c), device_id_type=MESH))
        for cp in cps:
            cp.start()
        for i in range(n):
            h = f_refs[i].shape[0] // 2
            theirs = o_refs[i].at[pl.ds((1 - c) * h, h), :]
            pltpu.make_async_remote_copy(
                src_ref=theirs, dst_ref=theirs, send_sem=send_sems.at[i], recv_sem=recv_sems.at[i],
                device_id=(x, y, 1 - c), device_id_type=MESH).wait_recv()
        for cp in cps:
            cp.wait_send()

    return pl.pallas_call(
        body, name="rs_join_halves", in_specs=[ANY] * n, out_specs=[ANY] * n,
        out_shape=[jax.ShapeDtypeStruct(f.shape, F32) for f in fs],
        input_output_aliases={i: i for i in range(n)},
        scratch_shapes=[pltpu.SemaphoreType.DMA((n,)), pltpu.SemaphoreType.DMA((n,))],
        compiler_params=COMM,
    )(*fs)


def _allreduce_small(v):
    r, W = v.shape

    def body(v_ref, o_ref, land_ref, send_sems, recv_sems):
        x, y, c, _ = _place()
        me = 4 * x + 2 * y + c
        cps = []
        for m in range(1, 8):
            to = (x ^ (m >> 2), y ^ ((m >> 1) & 1), c ^ (m & 1))
            cps.append(pltpu.make_async_remote_copy(
                src_ref=v_ref, dst_ref=land_ref.at[m - 1], send_sem=send_sems.at[m - 1], recv_sem=recv_sems.at[m - 1],
                device_id=to, device_id_type=MESH))
        for cp in cps:
            cp.start()
        for cp in cps:
            cp.wait()
        total = jnp.zeros((r, W), F32)
        for d in range(8):
            slot = jnp.maximum((me ^ d) - 1, 0)
            total = total + jnp.where(me == d, v_ref[...], land_ref[slot])
        o_ref[...] = total

    return pl.pallas_call(
        body, name="allreduce_small",
        in_specs=[pl.BlockSpec(memory_space=pltpu.VMEM)], out_specs=pl.BlockSpec(memory_space=pltpu.VMEM),
        out_shape=jax.ShapeDtypeStruct((r, W), F32),
        scratch_shapes=[pltpu.VMEM((7, r, W), F32), pltpu.SemaphoreType.DMA((7,)), pltpu.SemaphoreType.DMA((7,))],
        compiler_params=pltpu.CompilerParams(has_side_effects=True, vmem_limit_bytes=VMEM_LIMIT),
    )(v)


def kernel(x, ffn1_w_gate, ffn1_w_up, ffn1_w_down, ffn2_w_gate, ffn2_w_up, ffn2_w_down, w_in, pool_w, pool_scale, conv_w, rpb, w_out, ln_g, ln_b, loss_target, m_ffn1_w_gate, m_ffn1_w_up, m_ffn1_w_down, m_ffn2_w_gate, m_ffn2_w_up, m_ffn2_w_down, m_w_in, m_pool_w, m_pool_scale, m_conv_w, m_rpb, m_w_out, m_ln_g, m_ln_b, v_ffn1_w_gate, v_ffn1_w_up, v_ffn1_w_down, v_ffn2_w_gate, v_ffn2_w_up, v_ffn2_w_down, v_w_in, v_pool_w, v_pool_scale, v_conv_w, v_rpb, v_w_out, v_ln_g, v_ln_b):
    weights = dict(ffn1_w_gate=ffn1_w_gate, ffn1_w_up=ffn1_w_up, ffn1_w_down=ffn1_w_down, ffn2_w_gate=ffn2_w_gate,
                   ffn2_w_up=ffn2_w_up, ffn2_w_down=ffn2_w_down, w_in=w_in, pool_w=pool_w, pool_scale=pool_scale,
                   conv_w=conv_w, rpb=rpb, w_out=w_out, ln_g=ln_g, ln_b=ln_b)
    mom_m = dict(ffn1_w_gate=m_ffn1_w_gate, ffn1_w_up=m_ffn1_w_up, ffn1_w_down=m_ffn1_w_down, ffn2_w_gate=m_ffn2_w_gate,
                 ffn2_w_up=m_ffn2_w_up, ffn2_w_down=m_ffn2_w_down, w_in=m_w_in, pool_w=m_pool_w,
                 pool_scale=m_pool_scale, conv_w=m_conv_w, rpb=m_rpb, w_out=m_w_out, ln_g=m_ln_g, ln_b=m_ln_b)
    mom_v = dict(ffn1_w_gate=v_ffn1_w_gate, ffn1_w_up=v_ffn1_w_up, ffn1_w_down=v_ffn1_w_down, ffn2_w_gate=v_ffn2_w_gate,
                 ffn2_w_up=v_ffn2_w_up, ffn2_w_down=v_ffn2_w_down, w_in=v_w_in, pool_w=v_pool_w,
                 pool_scale=v_pool_scale, conv_w=v_conv_w, rpb=v_rpb, w_out=v_w_out, ln_g=v_ln_g, ln_b=v_ln_b)
    order = list(weights)
    L = ffn1_w_gate.shape[0]
    xi, yi, ci = lax.axis_index("x"), lax.axis_index("y"), lax.axis_index("c")
    q_me = 2 * xi + yi
    x2 = x[0]
    target = loss_target[0]
    D = x2.shape[1]
    n_in = w_in.shape[-1]

    shards = [jnp.stack([ffn1_w_gate, ffn1_w_up, ffn2_w_gate, ffn2_w_up], axis=1).astype(BF),
              jnp.stack([ffn1_w_down, ffn2_w_down], axis=1).astype(BF),
              w_in.astype(BF),
              w_out.astype(BF)]
    small = jnp.concatenate([ln_g.reshape(-1), ln_b.reshape(-1), conv_w.reshape(-1)])
    n_small = small.shape[0]
    small_rows = -(-n_small // (8 * 128)) * 8
    small = jnp.pad(small, (0, small_rows * 128 - n_small)).reshape(small_rows, 128)
    gathered, small_all = _gather_weights(shards, small)
    wa, wb, wc, wo = [lax.dynamic_update_slice(g, s[None], (q_me,) + (0,) * s.ndim) for g, s in zip(gathered, shards)]

    small_all = small_all.reshape(NQ, small_rows * 128)[:, :n_small]
    dq4 = D // NQ
    n_ln = L * 3 * dq4
    ln_g_all = small_all[:, :n_ln].reshape(NQ, L, 3, dq4).transpose(1, 2, 0, 3).reshape(L, 3, D)
    ln_b_all = small_all[:, n_ln:2 * n_ln].reshape(NQ, L, 3, dq4).transpose(1, 2, 0, 3).reshape(L, 3, D)
    conv_all = small_all[:, 2 * n_ln:].reshape(NQ, L, 3, D_CONV // NQ).transpose(1, 2, 0, 3).reshape(L, 3, D_CONV)

    ng = len(POOL_WINDOWS)
    pg = D_POOL // ng
    saved = []
    h = x2
    hb = x2.astype(BF)
    for l in range(L):
        eye = jnp.eye(ng, dtype=F32)
        wblk = (pool_w[l][:, :, None, :] * eye[:, None, :, None]).reshape(D_POOL, D_POOL).astype(BF)
        vec = jnp.concatenate([pool_scale[l][None], conv_all[l], jnp.zeros((4, D_POOL), F32)], axis=0)
        bias = _bias_table(rpb[l])
        lg = [ln_g_all[l, j][None] for j in range(3)]
        lb = [ln_b_all[l, j][None] for j in range(3)]
        x1, x1b, z1, g1, u1 = _ffn_fwd(h, wa, wb, lg[0], lb[0], l, 0, 1, 0)
        pabc, qkv = _proj(x1b, wc, l)
        yab = _mixab_fwd(pabc, wblk, vec)
        yc = _attn_fwd(qkv, bias)
        xm, xmb, zm = _mixout_fwd(yab, yc, x1, wo, lg[1], lb[1], l)
        x3, x3b, z3, g3, u3 = _ffn_fwd(xm, wa, wb, lg[2], lb[2], l, 2, 3, 1)
        saved.append(dict(wblk=wblk, vec=vec, bias=bias, lg=lg, hb=hb, z1=z1, g1=g1, u1=u1, x1b=x1b, pabc=pabc,
                          qkv=qkv, yab=yab, yc=yc, zm=zm, xmb=xmb, z3=z3, g3=g3, u3=u3))
        h, hb = x3, x3b

    loss_tile, dh = _loss_head(h, target)
    loss = lax.psum(loss_tile[0, 0], ("x", "y", "c"))

    grads_big = [None] * L
    g_small = dict(pool_w=[None] * L, pool_scale=[None] * L, conv_w=[None] * L, rpb=[None] * L, ln_g=[None] * L,
                   ln_b=[None] * L)
    for l in reversed(range(L)):
        sv = saved[l]
        dxm, df, dg, du, a, ln3 = _ffn_bwd(dh, sv["z3"], sv["g3"], sv["u3"], wa, wb, sv["lg"][2], l, 2, 3, 1)
        g_gu2 = _wgrad_gate_up(sv["xmb"], dg, du)
        g_d2 = _wgrad_down(a, df)
        dres, dzb, dycat, ln2 = _mixout_bwd(dxm, sv["zm"], wo, sv["lg"][1], l)
        g_o = _wgrad_out(sv["yab"], sv["yc"], dzb)
        dpabc, dwblk, dvec = _mixab_bwd(sv["pabc"], dycat, sv["wblk"], sv["vec"])
        dq, dk, dv, dbias = _attn_bwd(sv["qkv"], sv["bias"], dycat)
        dparts = [dpabc, dq, dk, dv]
        g_in = _wgrad_in(sv["x1b"], dparts, n_in)
        dx1 = _proj_bwd(dres, dparts, wc, l)
        dh, df, dg, du, a, ln1 = _ffn_bwd(dx1, sv["z1"], sv["g1"], sv["u1"], wa, wb, sv["lg"][0], l, 0, 1, 0)
        g_gu1 = _wgrad_gate_up(sv["hb"], dg, du)
        g_d1 = _wgrad_down(a, df)
        grads_big[l] = [g_gu1, g_d1, g_gu2, g_d2, g_in, g_o]
        g_small["pool_w"][l] = jnp.stack([dwblk[gi * pg:(gi + 1) * pg, gi * pg:(gi + 1) * pg] for gi in range(ng)])
        g_small["pool_scale"][l] = dvec[0]
        g_small["conv_w"][l] = dvec[1:4]
        g_small["rpb"][l] = _bias_grad(dbias)
        g_small["ln_g"][l] = jnp.stack([ln1[0], ln2[0], ln3[0]])
        g_small["ln_b"][l] = jnp.stack([ln1[1], ln2[1], ln3[1]])
    grad_x = dh[None]

    flat = [g for l in range(L) for g in grads_big[l]]
    recv = _swap_halves(flat)
    chip = [_add_chip(g, r) for g, r in zip(flat, recv)]
    from_chips = _scatter_chips([cb for _, cb in chip])
    finals = _join_halves([_add_final(cf, r) for (cf, _), r in zip(chip, from_chips)])
    per_layer = [finals[6 * l:6 * l + 6] for l in range(L)]

    def stacked(i, rows=None):
        parts = [per_layer[l][i] if rows is None else per_layer[l][i][rows[0]:rows[1]] for l in range(L)]
        return jnp.stack(parts)

    grads = dict(ffn1_w_gate=stacked(0, (0, D)), ffn1_w_up=stacked(0, (D, 2 * D)), ffn1_w_down=stacked(1),
                 ffn2_w_gate=stacked(2, (0, D)), ffn2_w_up=stacked(2, (D, 2 * D)), ffn2_w_down=stacked(3),
                 w_in=stacked(4), w_out=stacked(5))

    small_names = ("pool_w", "pool_scale", "conv_w", "rpb", "ln_g", "ln_b")
    small_full = {n: jnp.stack(g_small[n]) for n in small_names}
    vflat = jnp.concatenate([small_full[n].reshape(-1) for n in small_names])
    n_v = vflat.shape[0]
    v_cols = 1024
    v_rows = -(-n_v // (8 * v_cols)) * 8
    vsum = _allreduce_small(jnp.pad(vflat, (0, v_rows * v_cols - n_v)).reshape(v_rows, v_cols)).reshape(-1)
    off = 0
    for n in small_names:
        sz = int(np.prod(small_full[n].shape))
        grads[n] = vsum[off:off + sz].reshape(small_full[n].shape)
        off += sz
    for n in ("conv_w", "ln_g", "ln_b"):
        width = weights[n].shape[-1]
        grads[n] = lax.dynamic_slice_in_dim(grads[n], q_me * width, width, axis=2)

    delta, new_m, new_v = {}, {}, {}
    for n in order:
        delta[n], new_m[n], new_v[n] = _adamw(weights[n], grads[n], mom_m[n], mom_v[n])
    return (loss, grad_x, *[grads[n] for n in order], *[delta[n] for n in order], *[new_m[n] for n in order],
            *[new_v[n] for n in order])
```

```python
import numpy as np
import jax
import jax.numpy as jnp
from jax import lax
from jax.experimental import pallas as pl
from jax.experimental.pallas import tpu as pltpu

BF = jnp.bfloat16
F32 = jnp.float32
MESH = pl.DeviceIdType.MESH

DEPTH = 4
ALPHA = (2.0 * DEPTH) ** 0.25
LN_EPS = 1e-5
NEG_INF = -1e30
GRID_W = 64
NA_ROWS = 8
NA_COLS = 16
NA_HEADS = 8
HEAD_DIM = 64
D_POOL = 256
D_CONV = 256
D_NA = 512
POOL_WINDOWS = (2, 4, 8, 16)
HALO = 8
ADAM_LR, ADAM_B1, ADAM_B2, ADAM_EPS, ADAM_WD, ADAM_STEP = 0.001, 0.9, 0.999, 1e-08, 0.01, 10
VMEM_LIMIT = 56 * 1024 * 1024
NQ = 4


def _cp(n_axes):
    return pltpu.CompilerParams(dimension_semantics=("arbitrary",) * n_axes, vmem_limit_bytes=VMEM_LIMIT)


def _full(shape):
    nd = len(shape)
    return pl.BlockSpec(shape, lambda *_: (0,) * nd)


def _quarters(arr, k=None):
    if k is None:
        return pl.BlockSpec(arr.shape, lambda *_: (0, 0, 0), pipeline_mode=pl.Buffered(1))
    return pl.BlockSpec((NQ, None) + arr.shape[2:], lambda *_: (0, k, 0, 0), pipeline_mode=pl.Buffered(1))


ANY = pl.BlockSpec(memory_space=pl.ANY)


class _Rider:
    def __init__(self, tag, ins, outs, aliases, n, copies):
        self.tag, self.ins, self.outs, self.aliases, self.n, self.copies = tag, list(ins), list(outs), aliases, n, copies


def _merge(*riders):
    ins, outs, aliases, spans, n = [], [], {}, [], 0
    for r in riders:
        spans.append((len(ins), len(outs), n))
        aliases.update({len(ins) + i: len(outs) + j for i, j in r.aliases.items()})
        ins += r.ins
        outs += r.outs
        n += r.n

    def copies(r_in, r_out, ssem, rsem, base):
        cps = []
        for r, (i0, o0, s0) in zip(riders, spans):
            cps += r.copies(r_in[i0:i0 + len(r.ins)], r_out[o0:o0 + len(r.outs)], ssem, rsem, base + s0)
        return cps

    return _Rider("_".join(r.tag for r in riders), ins, outs, aliases, n, copies)


def _pcall(body, operands, *, name, grid, in_specs, out_specs, out_shape, scratch=(), rider=None, edges=None):
    n_in, n_out = len(in_specs), len(out_specs)
    params = dict(dimension_semantics=("arbitrary",) * len(grid), vmem_limit_bytes=VMEM_LIMIT)
    if rider is None:
        outs = pl.pallas_call(body, name=name, grid=grid, in_specs=in_specs, out_specs=out_specs, out_shape=out_shape,
                              scratch_shapes=list(scratch), compiler_params=pltpu.CompilerParams(**params))(*operands)
        return list(outs), []
    ni, no = len(rider.ins), len(rider.outs)
    first, last = edges

    def riding(*refs):
        rest = refs[n_in + ni + n_out + no:]
        cps = rider.copies(refs[n_in:n_in + ni], refs[n_in + ni + n_out:n_in + ni + n_out + no], rest[-2], rest[-1], 0)

        @pl.when(first())
        def _():
            for cp in cps:
                cp.start()

        body(*refs[:n_in], *refs[n_in + ni:n_in + ni + n_out], *rest[:-2])

        @pl.when(last())
        def _():
            for cp in cps:
                cp.wait()

    outs = pl.pallas_call(
        riding, name=f"{name}_{rider.tag}", grid=grid, in_specs=list(in_specs) + [ANY] * ni,
        out_specs=list(out_specs) + [ANY] * no, out_shape=list(out_shape) + rider.outs,
        scratch_shapes=list(scratch) + [pltpu.SemaphoreType.DMA((rider.n,)), pltpu.SemaphoreType.DMA((rider.n,))],
        input_output_aliases={n_in + i: n_out + j for i, j in rider.aliases.items()},
        compiler_params=pltpu.CompilerParams(has_side_effects=True, **params),
    )(*operands, *rider.ins)
    return list(outs[:n_out]), list(outs[n_out:])


def _edges_1d(n):
    return (lambda: pl.program_id(0) == 0), (lambda: pl.program_id(0) == n - 1)


def _edges_2d(n0, n1):
    return ((lambda: (pl.program_id(0) == 0) & (pl.program_id(1) == 0)),
            (lambda: (pl.program_id(0) == n0 - 1) & (pl.program_id(1) == n1 - 1)))


def _nt(a, b):
    return lax.dot_general(a, b, (((1,), (1,)), ((), ())), preferred_element_type=F32)


def _tn(a, b):
    return lax.dot_general(a, b, (((0,), (0,)), ((), ())), preferred_element_type=F32)


def _nn(a, b):
    return jnp.dot(a, b, preferred_element_type=F32)


def _ln_fwd(z, g, b):
    mu = jnp.mean(z, axis=-1, keepdims=True)
    zc = z - mu
    var = jnp.mean(zc * zc, axis=-1, keepdims=True)
    return zc * lax.rsqrt(var + LN_EPS) * g + b


def _ln_bwd(dy, z, g):
    mu = jnp.mean(z, axis=-1, keepdims=True)
    zc = z - mu
    var = jnp.mean(zc * zc, axis=-1, keepdims=True)
    rstd = lax.rsqrt(var + LN_EPS)
    xhat = zc * rstd
    gdy = dy * g
    m1 = jnp.mean(gdy, axis=-1, keepdims=True)
    m2 = jnp.mean(gdy * xhat, axis=-1, keepdims=True)
    return rstd * (gdy - m1 - xhat * m2), xhat


def _acc_ln_grads(acc_ref, first, dy, xhat):
    @pl.when(first)
    def _():
        acc_ref[...] = jnp.zeros_like(acc_ref)
    acc_ref[0:1, :] += jnp.sum(dy * xhat, axis=0, keepdims=True)
    acc_ref[1:2, :] += jnp.sum(dy, axis=0, keepdims=True)


def _ffn_fwd(x, wa, wb, lg, lb, kg, ku, kd, rider=None):
    S, D = x.shape
    fq = wa.shape[-1]
    tm = min(256, S)

    def body(x_ref, wg_ref, wu_ref, wd_ref, lg_ref, lb_ref, xo_ref, xb_ref, z_ref, g_ref, u_ref):
        x = x_ref[...]
        xb = x.astype(BF)
        acc = jnp.zeros((tm, D), F32)
        for q in range(NQ):
            g = _nn(xb, wg_ref[q])
            u = _nn(xb, wu_ref[q])
            g_ref[q] = g.astype(BF)
            u_ref[q] = u.astype(BF)
            a = g * jax.nn.sigmoid(g) * u
            acc = acc + _nn(a.astype(BF), wd_ref[q])
        z = ALPHA * x + 0.5 * acc
        xo = _ln_fwd(z, lg_ref[...], lb_ref[...])
        z_ref[...] = z
        xo_ref[...] = xo
        xb_ref[...] = xo.astype(BF)

    row = pl.BlockSpec((tm, D), lambda i: (i, 0))
    qrow = pl.BlockSpec((NQ, tm, fq), lambda i: (0, i, 0))
    return _pcall(
        body, [x, wa, wa, wb, lg, lb], name=f"ffn_fwd_k{kd}", grid=(S // tm,),
        in_specs=[row, _quarters(wa, kg), _quarters(wa, ku), _quarters(wb, kd), _full((1, D)), _full((1, D))],
        out_specs=[row, row, row, qrow, qrow],
        out_shape=[jax.ShapeDtypeStruct((S, D), F32), jax.ShapeDtypeStruct((S, D), BF),
                   jax.ShapeDtypeStruct((S, D), F32), jax.ShapeDtypeStruct((NQ, S, fq), BF),
                   jax.ShapeDtypeStruct((NQ, S, fq), BF)],
        rider=rider, edges=_edges_1d(S // tm))


def _ffn_bwd(dxo, z, g, u, wa, wb, lg, kg, ku, kd, rider=None):
    S, D = dxo.shape
    fq = wa.shape[-1]
    tm = min(256, S)

    def body(dxo_ref, z_ref, g_ref, u_ref, wg_ref, wu_ref, wd_ref, lg_ref,
             dx_ref, df_ref, dg_ref, du_ref, a_ref, ln_ref):
        dy = dxo_ref[...]
        dz, xhat = _ln_bwd(dy, z_ref[...], lg_ref[...])
        _acc_ln_grads(ln_ref, pl.program_id(0) == 0, dy, xhat)
        dfb = (0.5 * dz).astype(BF)
        df_ref[...] = dfb
        acc = ALPHA * dz
        for q in range(NQ):
            da = _nt(dfb, wd_ref[q])
            gg = g_ref[q].astype(F32)
            uu = u_ref[q].astype(F32)
            sg = jax.nn.sigmoid(gg)
            silu = gg * sg
            a_ref[q] = (silu * uu).astype(BF)
            dgb = (da * uu * (sg * (1.0 + gg * (1.0 - sg)))).astype(BF)
            dub = (da * silu).astype(BF)
            dg_ref[q] = dgb
            du_ref[q] = dub
            acc = acc + _nt(dgb, wg_ref[q]) + _nt(dub, wu_ref[q])
        dx_ref[...] = acc

    row = pl.BlockSpec((tm, D), lambda i: (i, 0))
    qrow = pl.BlockSpec((NQ, tm, fq), lambda i: (0, i, 0))
    qshape = jax.ShapeDtypeStruct((NQ, S, fq), BF)
    return _pcall(
        body, [dxo, z, g, u, wa, wa, wb, lg], name=f"ffn_bwd_k{kd}", grid=(S // tm,),
        in_specs=[row, row, qrow, qrow, _quarters(wa, kg), _quarters(wa, ku), _quarters(wb, kd), _full((1, D))],
        out_specs=[row, row, qrow, qrow, qrow, _full((8, D))],
        out_shape=[jax.ShapeDtypeStruct((S, D), F32), jax.ShapeDtypeStruct((S, D), BF), qshape, qshape, qshape,
                   jax.ShapeDtypeStruct((8, D), F32)],
        rider=rider, edges=_edges_1d(S // tm))


def _wgrad_gate_up(a, dg, du):
    S, K = a.shape
    n = dg.shape[-1]
    ts = min(512, S)

    def body(a_ref, g_ref, u_ref, o_ref):
        @pl.when(pl.program_id(1) == 0)
        def _():
            o_ref[...] = jnp.zeros_like(o_ref)
        av = a_ref[...]
        o_ref[0:K, :] += _tn(av, g_ref[...])
        o_ref[K:2 * K, :] += _tn(av, u_ref[...])

    bspec = pl.BlockSpec((None, ts, n), lambda q, s: (q, s, 0))
    return pl.pallas_call(
        body, name="wgrad_gate_up", grid=(NQ, S // ts),
        in_specs=[pl.BlockSpec((ts, K), lambda q, s: (s, 0)), bspec, bspec],
        out_specs=pl.BlockSpec((None, 2 * K, n), lambda q, s: (q, 0, 0)),
        out_shape=jax.ShapeDtypeStruct((NQ, 2 * K, n), F32),
        compiler_params=_cp(2),
    )(a, dg, du)


def _wgrad_down(a, df):
    _, S, k = a.shape
    N = df.shape[1]
    ts = min(512, S)

    def body(a_ref, b_ref, o_ref):
        @pl.when(pl.program_id(1) == 0)
        def _():
            o_ref[...] = jnp.zeros_like(o_ref)
        o_ref[...] += _tn(a_ref[...], b_ref[...])

    return pl.pallas_call(
        body, name="wgrad_down", grid=(NQ, S // ts),
        in_specs=[pl.BlockSpec((None, ts, k), lambda q, s: (q, s, 0)), pl.BlockSpec((ts, N), lambda q, s: (s, 0))],
        out_specs=pl.BlockSpec((None, k, N), lambda q, s: (q, 0, 0)),
        out_shape=jax.ShapeDtypeStruct((NQ, k, N), F32),
        compiler_params=_cp(2),
    )(a, df)


def _wgrad_out(yab, yc, dzb):
    S, h = yab.shape
    D = dzb.shape[1]
    k = h // 2
    ts = min(512, S)

    def body(yab_ref, yc_ref, b_ref, o_ref):
        @pl.when(pl.program_id(0) == 0)
        def _():
            o_ref[...] = jnp.zeros_like(o_ref)
        b = b_ref[...]
        o_ref[0] += _tn(yab_ref[:, 0:k], b)
        o_ref[1] += _tn(yab_ref[:, k:h], b)
        o_ref[2] += _tn(yc_ref[:, 0:k], b)
        o_ref[3] += _tn(yc_ref[:, k:h], b)

    row = lambda w: pl.BlockSpec((ts, w), lambda s: (s, 0))
    return pl.pallas_call(
        body, name="wgrad_out", grid=(S // ts,),
        in_specs=[row(h), row(h), row(D)], out_specs=_full((NQ, k, D)),
        out_shape=jax.ShapeDtypeStruct((NQ, k, D), F32),
        compiler_params=_cp(1),
    )(yab, yc, dzb)


def _proj(xb, wc):
    S, D = xb.shape
    n = wc.shape[-1]
    n1 = D_POOL + 3 * D_CONV
    n2 = NQ * n - n1
    tm = min(512, S)

    def body(x_ref, w_ref, p_ref, qkv_ref):
        x = x_ref[...]
        for q in range(NQ):
            r = _nn(x, w_ref[q])
            lo, hi = q * n, (q + 1) * n
            if hi <= n1:
                p_ref[:, lo:hi] = r
            elif lo >= n1:
                qkv_ref[:, lo - n1:hi - n1] = r.astype(BF)
            else:
                p_ref[:, lo:n1] = r[:, 0:n1 - lo]
                qkv_ref[:, 0:hi - n1] = r[:, n1 - lo:n].astype(BF)

    row = lambda w: pl.BlockSpec((tm, w), lambda i: (i, 0))
    return pl.pallas_call(
        body, name="mix_proj", grid=(S // tm,),
        in_specs=[row(D), _quarters(wc)],
        out_specs=[row(n1), row(n2)],
        out_shape=[jax.ShapeDtypeStruct((S, n1), F32), jax.ShapeDtypeStruct((S, n2), BF)],
        compiler_params=_cp(1),
    )(xb, wc)


def _mm_exact(a, b, name):
    def body(a_ref, b_ref, o_ref):
        o_ref[...] = jnp.dot(a_ref[...], b_ref[...], preferred_element_type=F32, precision=lax.Precision.HIGHEST)

    return pl.pallas_call(
        body, name=name, in_specs=[_full(a.shape), _full(b.shape)], out_specs=_full((a.shape[0], b.shape[1])),
        out_shape=jax.ShapeDtypeStruct((a.shape[0], b.shape[1]), F32),
        compiler_params=pltpu.CompilerParams(vmem_limit_bytes=VMEM_LIMIT),
    )(a, b)


def _bias_constants():
    c = np.arange(GRID_W)
    col_start = np.clip(c - NA_COLS // 2, 0, GRID_W - NA_COLS)
    valid = (c[None, :] >= col_start[:, None]) & (c[None, :] < col_start[:, None] + NA_COLS)
    dc = np.clip(c[None, :] - c[:, None], -(NA_COLS - 1), NA_COLS - 1) + (NA_COLS - 1)
    onehot = np.zeros((32, GRID_W * GRID_W), np.float32)
    onehot[dc.reshape(-1), np.arange(GRID_W * GRID_W)] = 1.0
    mask_kq = np.where(valid.T, 0.0, NEG_INF).astype(np.float32)
    mask = np.tile(mask_kq, (2 * NA_ROWS - 1, 2))
    return onehot, mask


def _bias_table(rpb):
    onehot, mask = _bias_constants()
    nr = 2 * NA_ROWS - 1
    r2 = jnp.pad(rpb.reshape(NA_HEADS * nr, 2 * NA_COLS - 1), ((0, 0), (0, 1)))
    t = _mm_exact(r2, jnp.asarray(onehot), "bias_expand")
    t = t.reshape(NA_HEADS // 2, 2, nr, GRID_W, GRID_W).transpose(0, 2, 4, 1, 3)
    return t.reshape(NA_HEADS // 2, nr * GRID_W, 2 * GRID_W) + jnp.asarray(mask)[None]


def _bias_grad(dt):
    onehot, _ = _bias_constants()
    nr = 2 * NA_ROWS - 1
    d = dt.reshape(NA_HEADS // 2, nr, GRID_W, 2, GRID_W).transpose(0, 3, 1, 4, 2).reshape(NA_HEADS * nr, -1)
    g = _mm_exact(d, jnp.asarray(onehot.T.copy()), "bias_reduce")
    return g[:, :2 * NA_COLS - 1].reshape(NA_HEADS, nr, 2 * NA_COLS - 1)


def _attn_rows(S):
    rows = S // GRID_W
    rb = min(16, rows)
    return rows, rb


def _attn_step(r, rows, q, k_ref, v_ref, b_ref, m_a):
    rs = jnp.clip(r - NA_ROWS // 2, 0, rows - NA_ROWS)
    s0 = rs - r + (NA_ROWS - 1)
    zero = jnp.zeros_like(q)
    q2 = jnp.concatenate([jnp.where(m_a, q, zero), jnp.where(m_a, zero, q)], axis=0)
    ks = pl.ds(pl.multiple_of(rs * GRID_W, GRID_W), NA_ROWS * GRID_W)
    kb = k_ref[ks, :]
    vb = v_ref[ks, :]
    bs = pl.ds(pl.multiple_of(s0 * GRID_W, GRID_W), NA_ROWS * GRID_W)
    s = _nt(kb, q2) * (HEAD_DIM ** -0.5) + b_ref[0, bs, :]
    m = jnp.max(s, axis=0, keepdims=True)
    p = jnp.exp(s - m)
    p = p / jnp.sum(p, axis=0, keepdims=True)
    return p, q2, kb, vb, ks, bs


def _attn_fwd(qkv, bias, rider=None):
    S = qkv.shape[0]
    rows, rb = _attn_rows(S)
    tq = rb * GRID_W
    hp = NA_HEADS // 2

    def body(q_ref, k_ref, v_ref, b_ref, o_ref):
        base = pl.program_id(1) * rb
        m_a = lax.broadcasted_iota(jnp.int32, (GRID_W, 128), 1) < HEAD_DIM

        def step(i, carry):
            qs = pl.ds(pl.multiple_of(i * GRID_W, GRID_W), GRID_W)
            p, _, _, vb, _, _ = _attn_step(base + i, rows, q_ref[qs, :], k_ref, v_ref, b_ref, m_a)
            o2 = _tn(p.astype(BF), vb)
            o_ref[qs, :] = jnp.where(m_a, o2[:GRID_W], o2[GRID_W:]).astype(BF)
            return carry

        lax.fori_loop(0, rb, step, 0, unroll=2)

    return _pcall(
        body, [qkv, qkv, qkv, bias], name="attn_fwd", grid=(hp, rows // rb),
        in_specs=[pl.BlockSpec((tq, 128), lambda h, r: (r, h)),
                  pl.BlockSpec((S, 128), lambda h, r: (0, hp + h)),
                  pl.BlockSpec((S, 128), lambda h, r: (0, 2 * hp + h)),
                  pl.BlockSpec((1, bias.shape[1], 128), lambda h, r: (h, 0, 0))],
        out_specs=[pl.BlockSpec((tq, 128), lambda h, r: (r, h))],
        out_shape=[jax.ShapeDtypeStruct((S, D_NA), BF)],
        rider=rider, edges=_edges_2d(hp, rows // rb))


def _attn_bwd(qkv, bias, dycat, rider=None):
    S = qkv.shape[0]
    rows, rb = _attn_rows(S)
    tq = rb * GRID_W
    hp = NA_HEADS // 2
    scale = HEAD_DIM ** -0.5

    def body(q_ref, k_ref, v_ref, b_ref, do_ref, dq_ref, dk_ref, dv_ref, db_ref, dka_ref, dva_ref):
        base = pl.program_id(1) * rb
        last = pl.program_id(1) == pl.num_programs(1) - 1
        m_a = lax.broadcasted_iota(jnp.int32, (GRID_W, 128), 1) < HEAD_DIM

        @pl.when(pl.program_id(1) == 0)
        def _():
            dka_ref[...] = jnp.zeros_like(dka_ref)
            dva_ref[...] = jnp.zeros_like(dva_ref)
            db_ref[...] = jnp.zeros_like(db_ref)

        def step(i, carry):
            qs = pl.ds(pl.multiple_of(i * GRID_W, GRID_W), GRID_W)
            p, q2, kb, vb, ks, bs = _attn_step(base + i, rows, q_ref[qs, :], k_ref, v_ref, b_ref, m_a)
            do = do_ref[qs, :].astype(BF)
            zero = jnp.zeros_like(do)
            do2 = jnp.concatenate([jnp.where(m_a, do, zero), jnp.where(m_a, zero, do)], axis=0)
            dp = _nt(vb, do2)
            ds = p * (dp - jnp.sum(p * dp, axis=0, keepdims=True))
            db_ref[0, bs, :] += ds
            dsb = ds.astype(BF)
            dq2 = _tn(dsb, kb) * scale
            dq_ref[qs, :] = jnp.where(m_a, dq2[:GRID_W], dq2[GRID_W:]).astype(BF)
            dka_ref[ks, :] += _nn(dsb, q2) * scale
            dva_ref[ks, :] += _nn(p.astype(BF), do2)
            return carry

        lax.fori_loop(0, rb, step, 0, unroll=2)

        @pl.when(last)
        def _():
            dk_ref[...] = dka_ref[...].astype(BF)
            dv_ref[...] = dva_ref[...].astype(BF)

    nb = bias.shape[1]
    return _pcall(
        body, [qkv, qkv, qkv, bias, dycat], name="attn_bwd", grid=(hp, rows // rb),
        in_specs=[pl.BlockSpec((tq, 128), lambda h, r: (r, h)),
                  pl.BlockSpec((S, 128), lambda h, r: (0, hp + h)),
                  pl.BlockSpec((S, 128), lambda h, r: (0, 2 * hp + h)),
                  pl.BlockSpec((1, nb, 128), lambda h, r: (h, 0, 0)),
                  pl.BlockSpec((tq, 128), lambda h, r: (r, hp + h))],
        out_specs=[pl.BlockSpec((tq, 128), lambda h, r: (r, h)),
                   pl.BlockSpec((S, 128), lambda h, r: (0, h)),
                   pl.BlockSpec((S, 128), lambda h, r: (0, h)),
                   pl.BlockSpec((1, nb, 128), lambda h, r: (h, 0, 0))],
        out_shape=[jax.ShapeDtypeStruct((S, D_NA), BF)] * 3 + [jax.ShapeDtypeStruct((hp, nb, 128), F32)],
        scratch=[pltpu.VMEM((S, 128), F32), pltpu.VMEM((S, 128), F32)],
        rider=rider, edges=_edges_2d(hp, rows // rb))


def _halo_specs(tm, width, S):
    hb = tm // HALO
    last = S // HALO - 1
    return [pl.BlockSpec((tm, width), lambda i: (i, 0)),
            pl.BlockSpec((HALO, width), lambda i: (jnp.maximum(i * hb - 1, 0), 0)),
            pl.BlockSpec((HALO, width), lambda i: (jnp.minimum((i + 1) * hb, last), 0))]


def _with_halo(cur_ref, prev_ref, next_ref, i, nt):
    prev = jnp.where(i > 0, prev_ref[...], 0.0)
    nxt = jnp.where(i < nt - 1, next_ref[...], 0.0)
    return jnp.concatenate([prev, cur_ref[...], nxt], axis=0)


def _shift(a, k):
    n = a.shape[0]
    return pltpu.roll(a, k % n, 0)


def _pool_lanes(n):
    lane = lax.broadcasted_iota(jnp.int32, (n, D_POOL), 1)
    group = D_POOL // len(POOL_WINDOWS)
    return [lane < group * (j + 1) for j in range(len(POOL_WINDOWS) - 1)]


def _by_window(lanes, vals):
    return jnp.where(lanes[0], vals[0], jnp.where(lanes[1], vals[1], jnp.where(lanes[2], vals[2], vals[3])))


def _pool_count(lanes, t, S):
    back = _by_window(lanes, tuple(w // 2 for w in POOL_WINDOWS))
    lo = jnp.maximum(t - back, 0)
    hi = jnp.minimum(t + back, S)
    return jnp.maximum(hi - lo, 1).astype(F32)


def _pool_p(u, lanes, cnt):
    a = u + _shift(u, 1)
    b = _shift(a, 1) + _shift(a, -1)
    c = _shift(b, 2) + _shift(b, -2)
    d = _shift(c, 4) + _shift(c, -4)
    return _by_window(lanes, (a, b, c, d)) / cnt - u


def _mixab_fwd(pabc, wblk, vec):
    S = pabc.shape[0]
    tm = min(512, S)
    nt = S // tm
    n = tm + 2 * HALO
    tile = slice(HALO, HALO + tm)

    def body(cur_ref, prev_ref, next_ref, w_ref, vec_ref, o_ref):
        i = pl.program_id(0)
        ext = _with_halo(cur_ref, prev_ref, next_ref, i, nt)
        lanes = _pool_lanes(n)
        t = i * tm - HALO + lax.broadcasted_iota(jnp.int32, (n, D_POOL), 0)
        p = _pool_p(ext[:, 0:D_POOL], lanes, _pool_count(lanes, t, S))[tile]
        o_ref[:, 0:D_POOL] = (_nn(p.astype(BF), w_ref[...]) * vec_ref[0:1, :]).astype(BF)
        zc = ext[:, 512:768] * ext[:, 768:1024]
        conv = vec_ref[1:2, :] * _shift(zc, 1) + vec_ref[2:3, :] * zc + vec_ref[3:4, :] * _shift(zc, -1)
        o_ref[:, D_POOL:D_POOL + D_CONV] = (ext[tile, 256:512] * conv[tile]).astype(BF)

    return pl.pallas_call(
        body, name="mixab_fwd", grid=(nt,),
        in_specs=_halo_specs(tm, 1024, S) + [_full((D_POOL, D_POOL)), _full((8, D_POOL))],
        out_specs=pl.BlockSpec((tm, D_POOL + D_CONV), lambda i: (i, 0)),
        out_shape=jax.ShapeDtypeStruct((S, D_POOL + D_CONV), BF),
        compiler_params=_cp(1),
    )(pabc, pabc, pabc, wblk, vec)


def _mixab_bwd(pabc, dycat, wblk, vec):
    S = pabc.shape[0]
    tm = min(512, S)
    nt = S // tm
    n = tm + 2 * HALO
    tile = slice(HALO, HALO + tm)

    def body(cur_ref, prev_ref, next_ref, dcur_ref, dprev_ref, dnext_ref, w_ref, vec_ref, o_ref, dw_ref, dvec_ref):
        i = pl.program_id(0)

        @pl.when(i == 0)
        def _():
            dw_ref[...] = jnp.zeros_like(dw_ref)
            dvec_ref[...] = jnp.zeros_like(dvec_ref)

        ext = _with_halo(cur_ref, prev_ref, next_ref, i, nt)
        dext = _with_halo(dcur_ref, dprev_ref, dnext_ref, i, nt)
        lanes = _pool_lanes(n)
        t = i * tm - HALO + lax.broadcasted_iota(jnp.int32, (n, D_POOL), 0)
        cnt = _pool_count(lanes, t, S)
        w = w_ref[...]
        scale = vec_ref[0:1, :]
        pb = _pool_p(ext[:, 0:D_POOL], lanes, cnt)[tile].astype(BF)
        dya = dext[:, 0:D_POOL]
        dvec_ref[0:1, :] += jnp.sum(dya[tile] * _nn(pb, w), axis=0, keepdims=True)
        dqb = (dya * scale).astype(BF)
        dw_ref[...] += _tn(pb, dqb[tile])
        dp = _nt(dqb, w)
        r = dp / cnt
        a = r + _shift(r, -1)
        b = _shift(a, 1) + _shift(a, -1)
        c = _shift(b, 2) + _shift(b, -2)
        d = _shift(c, 4) + _shift(c, -4)
        o_ref[:, 0:256] = (_by_window(lanes, (a, b, c, d)) - dp)[tile].astype(BF)
        gb, gc, hh = ext[:, 256:512], ext[:, 512:768], ext[:, 768:1024]
        zc = gc * hh
        zm, zp = _shift(zc, 1), _shift(zc, -1)
        w0, w1, w2 = vec_ref[1:2, :], vec_ref[2:3, :], vec_ref[3:4, :]
        dyb = dext[:, D_POOL:D_POOL + D_CONV]
        dconv = dyb * gb
        o_ref[:, 256:512] = (dyb * (w0 * zm + w1 * zc + w2 * zp))[tile].astype(BF)
        dzc = w0 * _shift(dconv, -1) + w1 * dconv + w2 * _shift(dconv, 1)
        o_ref[:, 512:768] = (dzc * hh)[tile].astype(BF)
        o_ref[:, 768:1024] = (dzc * gc)[tile].astype(BF)
        dct = dconv[tile]
        dvec_ref[1:2, :] += jnp.sum(dct * zm[tile], axis=0, keepdims=True)
        dvec_ref[2:3, :] += jnp.sum(dct * zc[tile], axis=0, keepdims=True)
        dvec_ref[3:4, :] += jnp.sum(dct * zp[tile], axis=0, keepdims=True)

    return pl.pallas_call(
        body, name="mixab_bwd", grid=(nt,),
        in_specs=_halo_specs(tm, 1024, S) + _halo_specs(tm, 512, S) + [_full((D_POOL, D_POOL)), _full((8, D_POOL))],
        out_specs=[pl.BlockSpec((tm, 1024), lambda i: (i, 0)), _full((D_POOL, D_POOL)), _full((8, D_POOL))],
        out_shape=[jax.ShapeDtypeStruct((S, 1024), BF), jax.ShapeDtypeStruct((D_POOL, D_POOL), F32),
                   jax.ShapeDtypeStruct((8, D_POOL), F32)],
        compiler_params=_cp(1),
    )(pabc, pabc, pabc, dycat, dycat, dycat, wblk, vec)


def _mixout_fwd(yab, yc, x, wo, lg, lb):
    S, D = x.shape
    tm = min(512, S)
    h = yab.shape[1]
    k = h // 2

    def body(yab_ref, yc_ref, x_ref, w_ref, lg_ref, lb_ref, xo_ref, xb_ref, z_ref):
        y = (_nn(yab_ref[:, 0:k], w_ref[0]) + _nn(yab_ref[:, k:h], w_ref[1])
             + _nn(yc_ref[:, 0:k], w_ref[2]) + _nn(yc_ref[:, k:h], w_ref[3]))
        z = ALPHA * x_ref[...] + y
        xo = _ln_fwd(z, lg_ref[...], lb_ref[...])
        z_ref[...] = z
        xo_ref[...] = xo
        xb_ref[...] = xo.astype(BF)

    row = lambda w: pl.BlockSpec((tm, w), lambda i: (i, 0))
    return pl.pallas_call(
        body, name="mixout_fwd", grid=(S // tm,),
        in_specs=[row(h), row(h), row(D), _quarters(wo), _full((1, D)), _full((1, D))],
        out_specs=[row(D), row(D), row(D)],
        out_shape=[jax.ShapeDtypeStruct((S, D), F32), jax.ShapeDtypeStruct((S, D), BF),
                   jax.ShapeDtypeStruct((S, D), F32)],
        compiler_params=_cp(1),
    )(yab, yc, x, wo, lg, lb)


def _mixout_bwd(dxo, z, wo, lg):
    S, D = dxo.shape
    k = wo.shape[-2]
    tm = min(512, S)

    def body(dxo_ref, z_ref, w_ref, lg_ref, dres_ref, dzb_ref, dy_ref, ln_ref):
        dy = dxo_ref[...]
        dz, xhat = _ln_bwd(dy, z_ref[...], lg_ref[...])
        _acc_ln_grads(ln_ref, pl.program_id(0) == 0, dy, xhat)
        dzb = dz.astype(BF)
        dres_ref[...] = ALPHA * dz
        dzb_ref[...] = dzb
        for q in range(NQ):
            dy_ref[:, q * k:(q + 1) * k] = _nt(dzb, w_ref[q])

    row = lambda w: pl.BlockSpec((tm, w), lambda i: (i, 0))
    return pl.pallas_call(
        body, name="mixout_bwd", grid=(S // tm,),
        in_specs=[row(D), row(D), _quarters(wo), _full((1, D))],
        out_specs=[row(D), row(D), row(NQ * k), _full((8, D))],
        out_shape=[jax.ShapeDtypeStruct((S, D), F32), jax.ShapeDtypeStruct((S, D), BF),
                   jax.ShapeDtypeStruct((S, NQ * k), F32), jax.ShapeDtypeStruct((8, D), F32)],
        compiler_params=_cp(1),
    )(dxo, z, wo, lg)


def _take_cols(refs, lo, hi):
    parts, off = [], 0
    for r in refs:
        w = r.shape[1]
        a, b = max(lo, off), min(hi, off + w)
        if a < b:
            parts.append(r[:, a - off:b - off])
        off += w
    return parts[0] if len(parts) == 1 else jnp.concatenate(parts, axis=1)


def _proj_bwd(dres, dparts, wc):
    S, D = dres.shape
    n = wc.shape[-1]
    tm = min(512, S)
    np_ = len(dparts)

    def body(*refs):
        dres_ref, d_refs, w_ref, dx_ref = refs[0], refs[1:1 + np_], refs[1 + np_], refs[2 + np_]
        acc = dres_ref[...]
        for q in range(NQ):
            acc = acc + _nt(_take_cols(d_refs, q * n, (q + 1) * n), w_ref[q])
        dx_ref[...] = acc

    row = lambda w: pl.BlockSpec((tm, w), lambda i: (i, 0))
    return pl.pallas_call(
        body, name="mix_proj_bwd", grid=(S // tm,),
        in_specs=[row(D)] + [row(d.shape[1]) for d in dparts] + [_quarters(wc)],
        out_specs=row(D),
        out_shape=jax.ShapeDtypeStruct((S, D), F32),
        compiler_params=_cp(1),
    )(dres, *dparts, wc)


def _wgrad_in(a, dparts, n):
    S, K = a.shape
    ts = min(512, S)
    np_ = len(dparts)

    def body(*refs):
        a_ref, d_refs, o_ref = refs[0], refs[1:1 + np_], refs[1 + np_]

        @pl.when(pl.program_id(0) == 0)
        def _():
            o_ref[...] = jnp.zeros_like(o_ref)
        av = a_ref[...]
        for q in range(NQ):
            o_ref[q] += _tn(av, _take_cols(d_refs, q * n, (q + 1) * n))

    row = lambda w: pl.BlockSpec((ts, w), lambda s: (s, 0))
    return pl.pallas_call(
        body, name="wgrad_in", grid=(S // ts,),
        in_specs=[row(K)] + [row(d.shape[1]) for d in dparts], out_specs=_full((NQ, K, n)),
        out_shape=jax.ShapeDtypeStruct((NQ, K, n), F32),
        compiler_params=_cp(1),
    )(a, *dparts)


def _loss_head(y, target):
    S, D = y.shape
    tm = min(512, S)

    def body(y_ref, t_ref, l_ref, dy_ref):
        @pl.when(pl.program_id(0) == 0)
        def _():
            l_ref[...] = jnp.zeros_like(l_ref)
        e = y_ref[...] - t_ref[...]
        dy_ref[...] = e * (1.0 / D)
        part = jnp.sum(jnp.sum(e * e, axis=1, keepdims=True) * (1.0 / D), axis=0, keepdims=True)
        l_ref[...] += 0.5 * part

    row = pl.BlockSpec((tm, D), lambda i: (i, 0))
    return pl.pallas_call(
        body, name="loss_head", grid=(S // tm,),
        in_specs=[row, row], out_specs=[_full((8, 128)), row],
        out_shape=[jax.ShapeDtypeStruct((8, 128), F32), jax.ShapeDtypeStruct((S, D), F32)],
        compiler_params=_cp(1),
    )(y, target)


def _adamw(w, g, m, v):
    shape = w.shape
    cols = shape[-1]
    rows = int(np.prod(shape[:-1]))
    w2, g2, m2, v2 = (a.reshape(rows, cols) for a in (w, g, m, v))
    tr = rows
    for cand in (512, 352, 256):
        if rows > cand and rows % cand == 0:
            tr = cand
            break

    def body(w_ref, g_ref, m_ref, v_ref, d_ref, mo_ref, vo_ref):
        g = g_ref[...]
        mn = ADAM_B1 * m_ref[...] + (1.0 - ADAM_B1) * g
        vn = ADAM_B2 * v_ref[...] + (1.0 - ADAM_B2) * (g * g)
        m_hat = mn / (1.0 - ADAM_B1 ** ADAM_STEP)
        v_hat = vn / (1.0 - ADAM_B2 ** ADAM_STEP)
        d_ref[...] = -ADAM_LR * (m_hat / (jnp.sqrt(v_hat) + ADAM_EPS) + ADAM_WD * w_ref[...])
        mo_ref[...] = mn
        vo_ref[...] = vn

    spec = pl.BlockSpec((tr, cols), lambda i: (i, 0))
    outs = pl.pallas_call(
        body, name=f"adamw_{rows}x{cols}", grid=(rows // tr,),
        in_specs=[spec] * 4, out_specs=[spec] * 3,
        out_shape=[jax.ShapeDtypeStruct((rows, cols), F32)] * 3,
        compiler_params=_cp(1),
    )(w2, g2, m2, v2)
    return tuple(o.reshape(shape) for o in outs)


def _half_tile(h):
    return h if h <= 512 else 512


def _add_chip(g, recv):
    _, R, C = g.shape
    h = R // 2
    tr = _half_tile(h)
    nb = h // tr

    def body(a_ref, b_ref, o_ref, ob_ref):
        s = a_ref[...] + b_ref[...]
        o_ref[...] = s
        ob_ref[...] = s.astype(BF)

    half = pl.BlockSpec((1, tr, C), lambda q, i: (q, i, 0))
    mine = pl.BlockSpec((1, tr, C), lambda q, i: (q, lax.axis_index("c") * nb + i, 0))
    return pl.pallas_call(
        body, name=f"rs_add_chip_{R}x{C}", grid=(NQ, nb), in_specs=[mine, half], out_specs=[half, half],
        out_shape=[jax.ShapeDtypeStruct((NQ, h, C), F32), jax.ShapeDtypeStruct((NQ, h, C), BF)],
        compiler_params=_cp(2),
    )(g, recv)


def _add_final(chip, recv):
    _, h, C = chip.shape
    tr = _half_tile(h)
    nb = h // tr

    def body(a_ref, b_ref, o_ref):
        s = a_ref[0]
        for j in range(3):
            s = s + b_ref[j].astype(F32)
        o_ref[...] = s

    return pl.pallas_call(
        body, name=f"rs_add_final_{h}x{C}", grid=(nb,),
        in_specs=[pl.BlockSpec((1, tr, C), lambda i: (2 * lax.axis_index("x") + lax.axis_index("y"), i, 0)),
                  pl.BlockSpec((3, tr, C), lambda i: (0, i, 0))],
        out_specs=pl.BlockSpec((tr, C), lambda i: (lax.axis_index("c") * nb + i, 0)),
        out_shape=jax.ShapeDtypeStruct((2 * h, C), F32),
        compiler_params=_cp(1),
    )(chip, recv)


COMM = pltpu.CompilerParams(has_side_effects=True)


def _place():
    x, y, c = lax.axis_index("x"), lax.axis_index("y"), lax.axis_index("c")
    chips = [(1 - x, y), (x, 1 - y), (1 - x, 1 - y)]
    return x, y, c, chips


def _half0(ref, c):
    n = ref.shape[0] // 2
    return ref.at[pl.ds(c * n, n)]


def _gather_ici(shards):
    n = len(shards)

    def copies(r_in, r_out, ssem, rsem, base):
        x, y, c, chips = _place()
        q = 2 * x + y
        return [pltpu.make_async_remote_copy(
            src_ref=_half0(r_in[i], c), dst_ref=_half0(r_out[i].at[q], c), send_sem=ssem.at[base + 3 * i + j],
            recv_sem=rsem.at[base + 3 * i + j], device_id=(*chip, c), device_id_type=MESH)
            for i in range(n) for j, chip in enumerate(chips)]

    return _Rider("ici", shards, [jax.ShapeDtypeStruct((NQ,) + s.shape, BF) for s in shards], {}, 3 * n, copies)


def _gather_d2d(bufs):
    n = len(bufs)

    def copies(r_in, r_out, ssem, rsem, base):
        x, y, c, chips = _place()
        return [pltpu.make_async_remote_copy(
            src_ref=_half0(r_in[i].at[2 * cx + cy], c), dst_ref=_half0(r_out[i].at[2 * cx + cy], c),
            send_sem=ssem.at[base + 3 * i + j], recv_sem=rsem.at[base + 3 * i + j], device_id=(x, y, 1 - c),
            device_id_type=MESH) for i in range(n) for j, (cx, cy) in enumerate(chips)]

    return _Rider("d2d", bufs, [jax.ShapeDtypeStruct(b.shape, b.dtype) for b in bufs], {i: i for i in range(n)},
                  3 * n, copies)


def _gather_small(small):
    sr = small.shape[0]

    def body(s_ref, o_ref, send_sems, recv_sems):
        x, y, c, chips = _place()
        o_ref[2 * x + y] = s_ref[...]
        cps = [pltpu.make_async_remote_copy(
            src_ref=s_ref, dst_ref=o_ref.at[2 * x + y], send_sem=send_sems.at[j], recv_sem=recv_sems.at[j],
            device_id=(*chip, c), device_id_type=MESH) for j, chip in enumerate(chips)]
        for cp in cps:
            cp.start()
        for j, (cx, cy) in enumerate(chips):
            pltpu.make_async_remote_copy(
                src_ref=s_ref, dst_ref=o_ref.at[2 * cx + cy], send_sem=send_sems.at[j], recv_sem=recv_sems.at[j],
                device_id=(cx, cy, c), device_id_type=MESH).wait_recv()
        for cp in cps:
            cp.wait_send()

    vm = pl.BlockSpec(memory_space=pltpu.VMEM)
    return pl.pallas_call(
        body, name="gather_small", in_specs=[vm], out_specs=vm,
        out_shape=jax.ShapeDtypeStruct((NQ, sr, 128), F32),
        scratch_shapes=[pltpu.SemaphoreType.DMA((3,)), pltpu.SemaphoreType.DMA((3,))],
        compiler_params=COMM,
    )(small)


def _swap_halves(gs):
    n = len(gs)

    def body(*refs):
        g_refs, o_refs, send_sems, recv_sems = refs[:n], refs[n:2 * n], refs[2 * n], refs[2 * n + 1]
        x, y, c, _ = _place()
        cps = []
        for i in range(n):
            h = g_refs[i].shape[1] // 2
            cps.append(pltpu.make_async_remote_copy(
                src_ref=g_refs[i].at[:, pl.ds((1 - c) * h, h), :], dst_ref=o_refs[i], send_sem=send_sems.at[i],
                recv_sem=recv_sems.at[i], device_id=(x, y, 1 - c), device_id_type=MESH))
        for cp in cps:
            cp.start()
        for cp in cps:
            cp.wait()

    return pl.pallas_call(
        body, name="rs_swap_halves", in_specs=[ANY] * n, out_specs=[ANY] * n,
        out_shape=[jax.ShapeDtypeStruct((NQ, g.shape[1] // 2, g.shape[2]), F32) for g in gs],
        scratch_shapes=[pltpu.SemaphoreType.DMA((n,)), pltpu.SemaphoreType.DMA((n,))],
        compiler_params=COMM,
    )(*gs)


def _scatter_chips(chips_b):
    n = len(chips_b)

    def copies(r_in, r_out, ssem, rsem, base):
        x, y, c, chips = _place()
        return [pltpu.make_async_remote_copy(
            src_ref=r_in[i].at[2 * cx + cy], dst_ref=r_out[i].at[j], send_sem=ssem.at[base + 3 * i + j],
            recv_sem=rsem.at[base + 3 * i + j], device_id=(cx, cy, c), device_id_type=MESH)
            for i in range(n) for j, (cx, cy) in enumerate(chips)]

    return _Rider("scatter", chips_b, [jax.ShapeDtypeStruct((3,) + s.shape[1:], BF) for s in chips_b], {}, 3 * n,
                  copies)


def _run_alone(rider, name):
    ni, no = len(rider.ins), len(rider.outs)

    def body(*refs):
        cps = rider.copies(refs[:ni], refs[ni:ni + no], refs[ni + no], refs[ni + no + 1], 0)
        for cp in cps:
            cp.start()
        for cp in cps:
            cp.wait()

    return list(pl.pallas_call(
        body, name=name, in_specs=[ANY] * ni, out_specs=[ANY] * no, out_shape=rider.outs,
        input_output_aliases=dict(rider.aliases),
        scratch_shapes=[pltpu.SemaphoreType.DMA((rider.n,)), pltpu.SemaphoreType.DMA((rider.n,))],
        compiler_params=COMM,
    )(*rider.ins))


def _join_halves(fs):
    n = len(fs)

    def body(*refs):
        f_refs, o_refs, send_sems, recv_sems = refs[:n], refs[n:2 * n], refs[2 * n], refs[2 * n + 1]
        x, y, c, _ = _place()
        cps = []
        for i in range(n):
            h = f_refs[i].shape[0] // 2
            rows = pl.ds(c * h, h)
            cps.append(pltpu.make_async_remote_copy(
                src_ref=f_refs[i].at[rows, :], dst_ref=o_refs[i].at[rows, :], send_sem=send_sems.at[i],
                recv_sem=recv_sems.at[i], device_id=(x, y, 1 - c), device_id_type=MESH))
        for cp in cps:
            cp.start()
        for i in range(n):
            h = f_refs[i].shape[0] // 2
            theirs = o_refs[i].at[pl.ds((1 - c) * h, h), :]
            pltpu.make_async_remote_copy(
                src_ref=theirs, dst_ref=theirs, send_sem=send_sems.at[i], recv_sem=recv_sems.at[i],
                device_id=(x, y, 1 - c), device_id_type=MESH).wait_recv()
        for cp in cps:
            cp.wait_send()

    return pl.pallas_call(
        body, name="rs_join_halves", in_specs=[ANY] * n, out_specs=[ANY] * n,
        out_shape=[jax.ShapeDtypeStruct(f.shape, F32) for f in fs],
        input_output_aliases={i: i for i in range(n)},
        scratch_shapes=[pltpu.SemaphoreType.DMA((n,)), pltpu.SemaphoreType.DMA((n,))],
        compiler_params=COMM,
    )(*fs)


def _allreduce_small(v):
    r, W = v.shape

    def body(v_ref, o_ref, land_ref, send_sems, recv_sems):
        x, y, c, _ = _place()
        me = 4 * x + 2 * y + c
        cps = []
        for m in range(1, 8):
            to = (x ^ (m >> 2), y ^ ((m >> 1) & 1), c ^ (m & 1))
            cps.append(pltpu.make_async_remote_copy(
                src_ref=v_ref, dst_ref=land_ref.at[m - 1], send_sem=send_sems.at[m - 1], recv_sem=recv_sems.at[m - 1],
                device_id=to, device_id_type=MESH))
        for cp in cps:
            cp.start()
        for cp in cps:
            cp.wait()
        total = jnp.zeros((r, W), F32)
        for d in range(8):
            slot = jnp.maximum((me ^ d) - 1, 0)
            total = total + jnp.where(me == d, v_ref[...], land_ref[slot])
        o_ref[...] = total

    return pl.pallas_call(
        body, name="allreduce_small",
        in_specs=[pl.BlockSpec(memory_space=pltpu.VMEM)], out_specs=pl.BlockSpec(memory_space=pltpu.VMEM),
        out_shape=jax.ShapeDtypeStruct((r, W), F32),
        scratch_shapes=[pltpu.VMEM((7, r, W), F32), pltpu.SemaphoreType.DMA((7,)), pltpu.SemaphoreType.DMA((7,))],
        compiler_params=pltpu.CompilerParams(has_side_effects=True, vmem_limit_bytes=VMEM_LIMIT),
    )(v)


def kernel(x, ffn1_w_gate, ffn1_w_up, ffn1_w_down, ffn2_w_gate, ffn2_w_up, ffn2_w_down, w_in, pool_w, pool_scale, conv_w, rpb, w_out, ln_g, ln_b, loss_target, m_ffn1_w_gate, m_ffn1_w_up, m_ffn1_w_down, m_ffn2_w_gate, m_ffn2_w_up, m_ffn2_w_down, m_w_in, m_pool_w, m_pool_scale, m_conv_w, m_rpb, m_w_out, m_ln_g, m_ln_b, v_ffn1_w_gate, v_ffn1_w_up, v_ffn1_w_down, v_ffn2_w_gate, v_ffn2_w_up, v_ffn2_w_down, v_w_in, v_pool_w, v_pool_scale, v_conv_w, v_rpb, v_w_out, v_ln_g, v_ln_b):
    weights = dict(ffn1_w_gate=ffn1_w_gate, ffn1_w_up=ffn1_w_up, ffn1_w_down=ffn1_w_down, ffn2_w_gate=ffn2_w_gate,
                   ffn2_w_up=ffn2_w_up, ffn2_w_down=ffn2_w_down, w_in=w_in, pool_w=pool_w, pool_scale=pool_scale,
                   conv_w=conv_w, rpb=rpb, w_out=w_out, ln_g=ln_g, ln_b=ln_b)
    mom_m = dict(ffn1_w_gate=m_ffn1_w_gate, ffn1_w_up=m_ffn1_w_up, ffn1_w_down=m_ffn1_w_down, ffn2_w_gate=m_ffn2_w_gate,
                 ffn2_w_up=m_ffn2_w_up, ffn2_w_down=m_ffn2_w_down, w_in=m_w_in, pool_w=m_pool_w,
                 pool_scale=m_pool_scale, conv_w=m_conv_w, rpb=m_rpb, w_out=m_w_out, ln_g=m_ln_g, ln_b=m_ln_b)
    mom_v = dict(ffn1_w_gate=v_ffn1_w_gate, ffn1_w_up=v_ffn1_w_up, ffn1_w_down=v_ffn1_w_down, ffn2_w_gate=v_ffn2_w_gate,
                 ffn2_w_up=v_ffn2_w_up, ffn2_w_down=v_ffn2_w_down, w_in=v_w_in, pool_w=v_pool_w,
                 pool_scale=v_pool_scale, conv_w=v_conv_w, rpb=v_rpb, w_out=v_w_out, ln_g=v_ln_g, ln_b=v_ln_b)
    order = list(weights)
    L = ffn1_w_gate.shape[0]
    xi, yi, ci = lax.axis_index("x"), lax.axis_index("y"), lax.axis_index("c")
    q_me = 2 * xi + yi
    x2 = x[0]
    target = loss_target[0]
    D = x2.shape[1]
    n_in = w_in.shape[-1]

    small = jnp.concatenate([ln_g.reshape(-1), ln_b.reshape(-1), conv_w.reshape(-1)])
    n_small = small.shape[0]
    small_rows = -(-n_small // (8 * 128)) * 8
    small = jnp.pad(small, (0, small_rows * 128 - n_small)).reshape(small_rows, 128)
    small_all = _gather_small(small).reshape(NQ, small_rows * 128)[:, :n_small]
    dq4 = D // NQ
    n_ln = L * 3 * dq4
    ln_g_all = small_all[:, :n_ln].reshape(NQ, L, 3, dq4).transpose(1, 2, 0, 3).reshape(L, 3, D)
    ln_b_all = small_all[:, n_ln:2 * n_ln].reshape(NQ, L, 3, dq4).transpose(1, 2, 0, 3).reshape(L, 3, D)
    conv_all = small_all[:, 2 * n_ln:].reshape(NQ, L, 3, D_CONV // NQ).transpose(1, 2, 0, 3).reshape(L, 3, D_CONV)

    def layer_shards(l):
        return [jnp.stack([ffn1_w_gate[l], ffn1_w_up[l], ffn2_w_gate[l], ffn2_w_up[l]]).astype(BF),
                jnp.stack([ffn1_w_down[l], ffn2_w_down[l]]).astype(BF),
                w_in[l].astype(BF),
                w_out[l].astype(BF)]

    def own_quarter(bufs, shards):
        return [lax.dynamic_update_slice(b, s[None], (q_me,) + (0,) * s.ndim) for b, s in zip(bufs, shards)]

    first = layer_shards(0)
    landed = _run_alone(_gather_ici(first), "gather_ici")
    weights_of = [own_quarter(_run_alone(_gather_d2d(landed), "gather_d2d"), first)] + [None] * (L - 1)

    ng = len(POOL_WINDOWS)
    pg = D_POOL // ng
    saved = []
    h = x2
    hb = x2.astype(BF)
    for l in range(L):
        wa, wb, wc, wo = weights_of[l]
        nxt = layer_shards(l + 1) if l + 1 < L else None
        eye = jnp.eye(ng, dtype=F32)
        wblk = (pool_w[l][:, :, None, :] * eye[:, None, :, None]).reshape(D_POOL, D_POOL).astype(BF)
        vec = jnp.concatenate([pool_scale[l][None], conv_all[l], jnp.zeros((4, D_POOL), F32)], axis=0)
        bias = _bias_table(rpb[l])
        lg = [ln_g_all[l, j][None] for j in range(3)]
        lb = [ln_b_all[l, j][None] for j in range(3)]
        (x1, x1b, z1, g1, u1), got_a = _ffn_fwd(h, wa, wb, lg[0], lb[0], 0, 1, 0,
                                                rider=_gather_ici(nxt[:1]) if nxt else None)
        pabc, qkv = _proj(x1b, wc)
        yab = _mixab_fwd(pabc, wblk, vec)
        (yc,), got = _attn_fwd(qkv, bias, rider=_merge(_gather_d2d(got_a), _gather_ici(nxt[1:])) if nxt else None)
        xm, xmb, zm = _mixout_fwd(yab, yc, x1, wo, lg[1], lb[1])
        (x3, x3b, z3, g3, u3), got_rest = _ffn_fwd(xm, wa, wb, lg[2], lb[2], 2, 3, 1,
                                                   rider=_gather_d2d(got[1:]) if nxt else None)
        if nxt:
            weights_of[l + 1] = own_quarter(got[:1] + got_rest, nxt)
        saved.append(dict(wblk=wblk, vec=vec, bias=bias, lg=lg, hb=hb, z1=z1, g1=g1, u1=u1, x1b=x1b, pabc=pabc,
                          qkv=qkv, yab=yab, yc=yc, zm=zm, xmb=xmb, z3=z3, g3=g3, u3=u3))
        h, hb = x3, x3b

    loss_tile, dh = _loss_head(h, target)
    loss = lax.psum(loss_tile[0, 0], ("x", "y", "c"))

    def rs_begin(arrs):
        chip = [_add_chip(g, r) for g, r in zip(arrs, _swap_halves(arrs))]
        return [cf for cf, _ in chip], [cb for _, cb in chip]

    def rs_end(chip_f, from_chips):
        return _join_halves([_add_final(cf, r) for cf, r in zip(chip_f, from_chips)])

    per_layer = [[None] * 6 for _ in range(L)]
    g_small = dict(pool_w=[None] * L, pool_scale=[None] * L, conv_w=[None] * L, rpb=[None] * L, ln_g=[None] * L,
                   ln_b=[None] * L)
    pending = None
    for l in reversed(range(L)):
        sv = saved[l]
        wa, wb, wc, wo = weights_of[l]
        (dxm, df, dg, du, a, ln3), got = _ffn_bwd(dh, sv["z3"], sv["g3"], sv["u3"], wa, wb, sv["lg"][2], 2, 3, 1,
                                                  rider=_scatter_chips(pending[1]) if pending else None)
        if pending:
            per_layer[l + 1][0:2] = rs_end(pending[0], got)
        ffn2_f, ffn2_b = rs_begin([_wgrad_gate_up(sv["xmb"], dg, du), _wgrad_down(a, df)])
        dres, dzb, dycat, ln2 = _mixout_bwd(dxm, sv["zm"], wo, sv["lg"][1])
        g_o = _wgrad_out(sv["yab"], sv["yc"], dzb)
        dpabc, dwblk, dvec = _mixab_bwd(sv["pabc"], dycat, sv["wblk"], sv["vec"])
        (dq, dk, dv, dbias), got = _attn_bwd(sv["qkv"], sv["bias"], dycat, rider=_scatter_chips(ffn2_b))
        per_layer[l][2:4] = rs_end(ffn2_f, got)
        dparts = [dpabc, dq, dk, dv]
        mix_f, mix_b = rs_begin([_wgrad_in(sv["x1b"], dparts, n_in), g_o])
        dx1 = _proj_bwd(dres, dparts, wc)
        (dh, df, dg, du, a, ln1), got = _ffn_bwd(dx1, sv["z1"], sv["g1"], sv["u1"], wa, wb, sv["lg"][0], 0, 1, 0,
                                                 rider=_scatter_chips(mix_b))
        per_layer[l][4:6] = rs_end(mix_f, got)
        pending = rs_begin([_wgrad_gate_up(sv["hb"], dg, du), _wgrad_down(a, df)])
        g_small["pool_w"][l] = jnp.stack([dwblk[gi * pg:(gi + 1) * pg, gi * pg:(gi + 1) * pg] for gi in range(ng)])
        g_small["pool_scale"][l] = dvec[0]
        g_small["conv_w"][l] = dvec[1:4]
        g_small["rpb"][l] = _bias_grad(dbias)
        g_small["ln_g"][l] = jnp.stack([ln1[0], ln2[0], ln3[0]])
        g_small["ln_b"][l] = jnp.stack([ln1[1], ln2[1], ln3[1]])
    per_layer[0][0:2] = rs_end(pending[0], _run_alone(_scatter_chips(pending[1]), "rs_scatter_chips"))
    grad_x = dh[None]

    def stacked(i, rows=None):
        parts = [per_layer[l][i] if rows is None else per_layer[l][i][rows[0]:rows[1]] for l in range(L)]
        return jnp.stack(parts)

    grads = dict(ffn1_w_gate=stacked(0, (0, D)), ffn1_w_up=stacked(0, (D, 2 * D)), ffn1_w_down=stacked(1),
                 ffn2_w_gate=stacked(2, (0, D)), ffn2_w_up=stacked(2, (D, 2 * D)), ffn2_w_down=stacked(3),
                 w_in=stacked(4), w_out=stacked(5))

    small_names = ("pool_w", "pool_scale", "conv_w", "rpb", "ln_g", "ln_b")
    small_full = {n: jnp.stack(g_small[n]) for n in small_names}
    vflat = jnp.concatenate([small_full[n].reshape(-1) for n in small_names])
    n_v = vflat.shape[0]
    v_cols = 1024
    v_rows = -(-n_v // (8 * v_cols)) * 8
    vsum = _allreduce_small(jnp.pad(vflat, (0, v_rows * v_cols - n_v)).reshape(v_rows, v_cols)).reshape(-1)
    off = 0
    for n in small_names:
        sz = int(np.prod(small_full[n].shape))
        grads[n] = vsum[off:off + sz].reshape(small_full[n].shape)
        off += sz
    for n in ("conv_w", "ln_g", "ln_b"):
        width = weights[n].shape[-1]
        grads[n] = lax.dynamic_slice_in_dim(grads[n], q_me * width, width, axis=2)

    delta, new_m, new_v = {}, {}, {}
    for n in order:
        delta[n], new_m[n], new_v[n] = _adamw(weights[n], grads[n], mom_m[n], mom_v[n])
    return (loss, grad_x, *[grads[n] for n in order], *[delta[n] for n in order], *[new_m[n] for n in order],
            *[new_v[n] for n in order])
```

```python
import numpy as np
import jax
import jax.numpy as jnp
from jax import lax
from jax.experimental import pallas as pl
from jax.experimental.pallas import tpu as pltpu

BF = jnp.bfloat16
F32 = jnp.float32
MESH = pl.DeviceIdType.MESH

DEPTH = 4
ALPHA = (2.0 * DEPTH) ** 0.25
LN_EPS = 1e-5
NEG_INF = -1e30
GRID_W = 64
NA_ROWS = 8
NA_COLS = 16
NA_HEADS = 8
HEAD_DIM = 64
D_POOL = 256
D_CONV = 256
D_NA = 512
HG = 4
LW = HG * HEAD_DIM
POOL_WINDOWS = (2, 4, 8, 16)
HALO = 8
ADAM_LR, ADAM_B1, ADAM_B2, ADAM_EPS, ADAM_WD, ADAM_STEP = 0.001, 0.9, 0.999, 1e-08, 0.01, 10
VMEM_LIMIT = 56 * 1024 * 1024
NQ = 4


def _cp(n_axes):
    return pltpu.CompilerParams(dimension_semantics=("arbitrary",) * n_axes, vmem_limit_bytes=VMEM_LIMIT)


def _full(shape):
    nd = len(shape)
    return pl.BlockSpec(shape, lambda *_: (0,) * nd)


def _quarters(arr, k=None):
    if k is None:
        return pl.BlockSpec(arr.shape, lambda *_: (0, 0, 0), pipeline_mode=pl.Buffered(1))
    return pl.BlockSpec((NQ, None) + arr.shape[2:], lambda *_: (0, k, 0, 0), pipeline_mode=pl.Buffered(1))


ANY = pl.BlockSpec(memory_space=pl.ANY)


class _Rider:
    def __init__(self, tag, ins, outs, aliases, n, copies):
        self.tag, self.ins, self.outs, self.aliases, self.n, self.copies = tag, list(ins), list(outs), aliases, n, copies


def _merge(*riders):
    ins, outs, aliases, spans, n = [], [], {}, [], 0
    for r in riders:
        spans.append((len(ins), len(outs), n))
        aliases.update({len(ins) + i: len(outs) + j for i, j in r.aliases.items()})
        ins += r.ins
        outs += r.outs
        n += r.n

    def copies(r_in, r_out, ssem, rsem, base):
        cps = []
        for r, (i0, o0, s0) in zip(riders, spans):
            cps += r.copies(r_in[i0:i0 + len(r.ins)], r_out[o0:o0 + len(r.outs)], ssem, rsem, base + s0)
        return cps

    return _Rider("_".join(r.tag for r in riders), ins, outs, aliases, n, copies)


def _pcall(body, operands, *, name, grid, in_specs, out_specs, out_shape, scratch=(), rider=None, edges=None):
    n_in, n_out = len(in_specs), len(out_specs)
    params = dict(dimension_semantics=("arbitrary",) * len(grid), vmem_limit_bytes=VMEM_LIMIT)
    if rider is None:
        outs = pl.pallas_call(body, name=name, grid=grid, in_specs=in_specs, out_specs=out_specs, out_shape=out_shape,
                              scratch_shapes=list(scratch), compiler_params=pltpu.CompilerParams(**params))(*operands)
        return list(outs), []
    ni, no = len(rider.ins), len(rider.outs)
    first, last = edges

    def riding(*refs):
        rest = refs[n_in + ni + n_out + no:]
        cps = rider.copies(refs[n_in:n_in + ni], refs[n_in + ni + n_out:n_in + ni + n_out + no], rest[-2], rest[-1], 0)

        @pl.when(first())
        def _():
            for cp in cps:
                cp.start()

        body(*refs[:n_in], *refs[n_in + ni:n_in + ni + n_out], *rest[:-2])

        @pl.when(last())
        def _():
            for cp in cps:
                cp.wait()

    outs = pl.pallas_call(
        riding, name=f"{name}_{rider.tag}", grid=grid, in_specs=list(in_specs) + [ANY] * ni,
        out_specs=list(out_specs) + [ANY] * no, out_shape=list(out_shape) + rider.outs,
        scratch_shapes=list(scratch) + [pltpu.SemaphoreType.DMA((rider.n,)), pltpu.SemaphoreType.DMA((rider.n,))],
        input_output_aliases={n_in + i: n_out + j for i, j in rider.aliases.items()},
        compiler_params=pltpu.CompilerParams(has_side_effects=True, **params),
    )(*operands, *rider.ins)
    return list(outs[:n_out]), list(outs[n_out:])


def _edges_1d(n):
    return (lambda: pl.program_id(0) == 0), (lambda: pl.program_id(0) == n - 1)


def _edges_2d(n0, n1):
    return ((lambda: (pl.program_id(0) == 0) & (pl.program_id(1) == 0)),
            (lambda: (pl.program_id(0) == n0 - 1) & (pl.program_id(1) == n1 - 1)))


def _nt(a, b):
    return lax.dot_general(a, b, (((1,), (1,)), ((), ())), preferred_element_type=F32)


def _tn(a, b):
    return lax.dot_general(a, b, (((0,), (0,)), ((), ())), preferred_element_type=F32)


def _nn(a, b):
    return jnp.dot(a, b, preferred_element_type=F32)


def _ln_fwd(z, g, b):
    mu = jnp.mean(z, axis=-1, keepdims=True)
    zc = z - mu
    var = jnp.mean(zc * zc, axis=-1, keepdims=True)
    return zc * lax.rsqrt(var + LN_EPS) * g + b


def _ln_bwd(dy, z, g):
    mu = jnp.mean(z, axis=-1, keepdims=True)
    zc = z - mu
    var = jnp.mean(zc * zc, axis=-1, keepdims=True)
    rstd = lax.rsqrt(var + LN_EPS)
    xhat = zc * rstd
    gdy = dy * g
    m1 = jnp.mean(gdy, axis=-1, keepdims=True)
    m2 = jnp.mean(gdy * xhat, axis=-1, keepdims=True)
    return rstd * (gdy - m1 - xhat * m2), xhat


def _acc_ln_grads(acc_ref, first, dy, xhat):
    @pl.when(first)
    def _():
        acc_ref[...] = jnp.zeros_like(acc_ref)
    acc_ref[0:1, :] += jnp.sum(dy * xhat, axis=0, keepdims=True)
    acc_ref[1:2, :] += jnp.sum(dy, axis=0, keepdims=True)


def _ffn_fwd(x, wa, wb, lg, lb, kg, ku, kd, rider=None):
    S, D = x.shape
    fq = wa.shape[-1]
    tm = min(256, S)

    def body(x_ref, wg_ref, wu_ref, wd_ref, lg_ref, lb_ref, xo_ref, xb_ref, z_ref, g_ref, u_ref):
        x = x_ref[...]
        xb = x.astype(BF)
        acc = jnp.zeros((tm, D), F32)
        for q in range(NQ):
            g = _nn(xb, wg_ref[q])
            u = _nn(xb, wu_ref[q])
            g_ref[q] = g.astype(BF)
            u_ref[q] = u.astype(BF)
            a = g * jax.nn.sigmoid(g) * u
            acc = acc + _nn(a.astype(BF), wd_ref[q])
        z = ALPHA * x + 0.5 * acc
        xo = _ln_fwd(z, lg_ref[...], lb_ref[...])
        z_ref[...] = z
        xo_ref[...] = xo
        xb_ref[...] = xo.astype(BF)

    row = pl.BlockSpec((tm, D), lambda i: (i, 0))
    qrow = pl.BlockSpec((NQ, tm, fq), lambda i: (0, i, 0))
    return _pcall(
        body, [x, wa, wa, wb, lg, lb], name=f"ffn_fwd_k{kd}", grid=(S // tm,),
        in_specs=[row, _quarters(wa, kg), _quarters(wa, ku), _quarters(wb, kd), _full((1, D)), _full((1, D))],
        out_specs=[row, row, row, qrow, qrow],
        out_shape=[jax.ShapeDtypeStruct((S, D), F32), jax.ShapeDtypeStruct((S, D), BF),
                   jax.ShapeDtypeStruct((S, D), F32), jax.ShapeDtypeStruct((NQ, S, fq), BF),
                   jax.ShapeDtypeStruct((NQ, S, fq), BF)],
        rider=rider, edges=_edges_1d(S // tm))


def _ffn_bwd(dxo, z, g, u, wa, wb, lg, kg, ku, kd, rider=None):
    S, D = dxo.shape
    fq = wa.shape[-1]
    tm = min(256, S)

    def body(dxo_ref, z_ref, g_ref, u_ref, wg_ref, wu_ref, wd_ref, lg_ref,
             dx_ref, df_ref, dg_ref, du_ref, a_ref, ln_ref):
        dy = dxo_ref[...]
        dz, xhat = _ln_bwd(dy, z_ref[...], lg_ref[...])
        _acc_ln_grads(ln_ref, pl.program_id(0) == 0, dy, xhat)
        dfb = (0.5 * dz).astype(BF)
        df_ref[...] = dfb
        acc = ALPHA * dz
        for q in range(NQ):
            da = _nt(dfb, wd_ref[q])
            gg = g_ref[q].astype(F32)
            uu = u_ref[q].astype(F32)
            sg = jax.nn.sigmoid(gg)
            silu = gg * sg
            a_ref[q] = (silu * uu).astype(BF)
            dgb = (da * uu * (sg * (1.0 + gg * (1.0 - sg)))).astype(BF)
            dub = (da * silu).astype(BF)
            dg_ref[q] = dgb
            du_ref[q] = dub
            acc = acc + _nt(dgb, wg_ref[q]) + _nt(dub, wu_ref[q])
        dx_ref[...] = acc

    row = pl.BlockSpec((tm, D), lambda i: (i, 0))
    qrow = pl.BlockSpec((NQ, tm, fq), lambda i: (0, i, 0))
    qshape = jax.ShapeDtypeStruct((NQ, S, fq), BF)
    return _pcall(
        body, [dxo, z, g, u, wa, wa, wb, lg], name=f"ffn_bwd_k{kd}", grid=(S // tm,),
        in_specs=[row, row, qrow, qrow, _quarters(wa, kg), _quarters(wa, ku), _quarters(wb, kd), _full((1, D))],
        out_specs=[row, row, qrow, qrow, qrow, _full((8, D))],
        out_shape=[jax.ShapeDtypeStruct((S, D), F32), jax.ShapeDtypeStruct((S, D), BF), qshape, qshape, qshape,
                   jax.ShapeDtypeStruct((8, D), F32)],
        rider=rider, edges=_edges_1d(S // tm))


def _wgrad_gate_up(a, dg, du):
    S, K = a.shape
    n = dg.shape[-1]
    ts = min(512, S)

    def body(a_ref, g_ref, u_ref, o_ref):
        @pl.when(pl.program_id(1) == 0)
        def _():
            o_ref[...] = jnp.zeros_like(o_ref)
        av = a_ref[...]
        o_ref[0:K, :] += _tn(av, g_ref[...])
        o_ref[K:2 * K, :] += _tn(av, u_ref[...])

    bspec = pl.BlockSpec((None, ts, n), lambda q, s: (q, s, 0))
    return pl.pallas_call(
        body, name="wgrad_gate_up", grid=(NQ, S // ts),
        in_specs=[pl.BlockSpec((ts, K), lambda q, s: (s, 0)), bspec, bspec],
        out_specs=pl.BlockSpec((None, 2 * K, n), lambda q, s: (q, 0, 0)),
        out_shape=jax.ShapeDtypeStruct((NQ, 2 * K, n), F32),
        compiler_params=_cp(2),
    )(a, dg, du)


def _wgrad_down(a, df):
    _, S, k = a.shape
    N = df.shape[1]
    ts = min(512, S)

    def body(a_ref, b_ref, o_ref):
        @pl.when(pl.program_id(1) == 0)
        def _():
            o_ref[...] = jnp.zeros_like(o_ref)
        o_ref[...] += _tn(a_ref[...], b_ref[...])

    return pl.pallas_call(
        body, name="wgrad_down", grid=(NQ, S // ts),
        in_specs=[pl.BlockSpec((None, ts, k), lambda q, s: (q, s, 0)), pl.BlockSpec((ts, N), lambda q, s: (s, 0))],
        out_specs=pl.BlockSpec((None, k, N), lambda q, s: (q, 0, 0)),
        out_shape=jax.ShapeDtypeStruct((NQ, k, N), F32),
        compiler_params=_cp(2),
    )(a, df)


def _wgrad_out(yab, yc, dzb):
    S, h = yab.shape
    D = dzb.shape[1]
    k = h // 2
    ts = min(512, S)

    def body(yab_ref, yc_ref, b_ref, o_ref):
        @pl.when(pl.program_id(0) == 0)
        def _():
            o_ref[...] = jnp.zeros_like(o_ref)
        b = b_ref[...]
        o_ref[0] += _tn(yab_ref[:, 0:k], b)
        o_ref[1] += _tn(yab_ref[:, k:h], b)
        o_ref[2] += _tn(yc_ref[:, 0:k], b)
        o_ref[3] += _tn(yc_ref[:, k:h], b)

    row = lambda w: pl.BlockSpec((ts, w), lambda s: (s, 0))
    return pl.pallas_call(
        body, name="wgrad_out", grid=(S // ts,),
        in_specs=[row(h), row(h), row(D)], out_specs=_full((NQ, k, D)),
        out_shape=jax.ShapeDtypeStruct((NQ, k, D), F32),
        compiler_params=_cp(1),
    )(yab, yc, dzb)


def _proj(xb, wc):
    S, D = xb.shape
    n = wc.shape[-1]
    n1 = D_POOL + 3 * D_CONV
    n2 = NQ * n - n1
    tm = min(512, S)

    def body(x_ref, w_ref, p_ref, qkv_ref):
        x = x_ref[...]
        for q in range(NQ):
            r = _nn(x, w_ref[q])
            lo, hi = q * n, (q + 1) * n
            if hi <= n1:
                p_ref[:, lo:hi] = r
            elif lo >= n1:
                qkv_ref[:, lo - n1:hi - n1] = r.astype(BF)
            else:
                p_ref[:, lo:n1] = r[:, 0:n1 - lo]
                qkv_ref[:, 0:hi - n1] = r[:, n1 - lo:n].astype(BF)

    row = lambda w: pl.BlockSpec((tm, w), lambda i: (i, 0))
    return pl.pallas_call(
        body, name="mix_proj", grid=(S // tm,),
        in_specs=[row(D), _quarters(wc)],
        out_specs=[row(n1), row(n2)],
        out_shape=[jax.ShapeDtypeStruct((S, n1), F32), jax.ShapeDtypeStruct((S, n2), BF)],
        compiler_params=_cp(1),
    )(xb, wc)


def _mm_exact(a, b, name):
    def body(a_ref, b_ref, o_ref):
        o_ref[...] = jnp.dot(a_ref[...], b_ref[...], preferred_element_type=F32, precision=lax.Precision.HIGHEST)

    return pl.pallas_call(
        body, name=name, in_specs=[_full(a.shape), _full(b.shape)], out_specs=_full((a.shape[0], b.shape[1])),
        out_shape=jax.ShapeDtypeStruct((a.shape[0], b.shape[1]), F32),
        compiler_params=pltpu.CompilerParams(vmem_limit_bytes=VMEM_LIMIT),
    )(a, b)


def _bias_constants():
    c = np.arange(GRID_W)
    col_start = np.clip(c - NA_COLS // 2, 0, GRID_W - NA_COLS)
    valid = (c[None, :] >= col_start[:, None]) & (c[None, :] < col_start[:, None] + NA_COLS)
    dc = np.clip(c[None, :] - c[:, None], -(NA_COLS - 1), NA_COLS - 1) + (NA_COLS - 1)
    onehot = np.zeros((32, GRID_W * GRID_W), np.float32)
    onehot[dc.reshape(-1), np.arange(GRID_W * GRID_W)] = 1.0
    mask_kq = np.where(valid.T, 0.0, NEG_INF).astype(np.float32)
    mask = np.tile(mask_kq, (2 * NA_ROWS - 1, HG))
    return onehot, mask


def _bias_table(rpb):
    onehot, mask = _bias_constants()
    nr = 2 * NA_ROWS - 1
    r2 = jnp.pad(rpb.reshape(NA_HEADS * nr, 2 * NA_COLS - 1), ((0, 0), (0, 1)))
    t = _mm_exact(r2, jnp.asarray(onehot), "bias_expand")
    t = t.reshape(NA_HEADS // HG, HG, nr, GRID_W, GRID_W).transpose(0, 2, 4, 1, 3)
    return t.reshape(NA_HEADS // HG, nr * GRID_W, LW) + jnp.asarray(mask)[None]


def _bias_grad(dt):
    onehot, _ = _bias_constants()
    nr = 2 * NA_ROWS - 1
    d = dt.reshape(NA_HEADS // HG, nr, GRID_W, HG, GRID_W).transpose(0, 3, 1, 4, 2).reshape(NA_HEADS * nr, -1)
    g = _mm_exact(d, jnp.asarray(onehot.T.copy()), "bias_reduce")
    return g[:, :2 * NA_COLS - 1].reshape(NA_HEADS, nr, 2 * NA_COLS - 1)


def _attn_rows(S):
    rows = S // GRID_W
    rb = min(16, rows)
    return rows, rb


def _head_masks():
    lane = lax.broadcasted_iota(jnp.int32, (GRID_W, LW), 1)
    return [(lane >= HEAD_DIM * h) & (lane < HEAD_DIM * (h + 1)) for h in range(HG)]


def _stack_heads(x, masks):
    zero = jnp.zeros_like(x)
    return jnp.concatenate([jnp.where(m, x, zero) for m in masks], axis=0)


def _unstack_heads(x2, masks):
    out = x2[0:GRID_W]
    for h in range(1, HG):
        out = jnp.where(masks[h], x2[h * GRID_W:(h + 1) * GRID_W], out)
    return out


def _attn_step(r, rows, q, k_ref, v_ref, b_ref, masks):
    rs = jnp.clip(r - NA_ROWS // 2, 0, rows - NA_ROWS)
    s0 = rs - r + (NA_ROWS - 1)
    q2 = _stack_heads(q, masks)
    ks = pl.ds(pl.multiple_of(rs * GRID_W, GRID_W), NA_ROWS * GRID_W)
    kb = k_ref[ks, :]
    vb = v_ref[ks, :]
    bs = pl.ds(pl.multiple_of(s0 * GRID_W, GRID_W), NA_ROWS * GRID_W)
    s = _nt(kb, q2) * (HEAD_DIM ** -0.5) + b_ref[0, bs, :]
    m = jnp.max(s, axis=0, keepdims=True)
    p = jnp.exp(s - m)
    p = p / jnp.sum(p, axis=0, keepdims=True)
    return p, q2, kb, vb, ks, bs


def _attn_fwd(qkv, bias, rider=None):
    S = qkv.shape[0]
    rows, rb = _attn_rows(S)
    tq = rb * GRID_W
    ngr = NA_HEADS // HG

    def body(q_ref, k_ref, v_ref, b_ref, o_ref):
        base = pl.program_id(1) * rb
        masks = _head_masks()

        def step(i, carry):
            qs = pl.ds(pl.multiple_of(i * GRID_W, GRID_W), GRID_W)
            p, _, _, vb, _, _ = _attn_step(base + i, rows, q_ref[qs, :], k_ref, v_ref, b_ref, masks)
            o_ref[qs, :] = _unstack_heads(_tn(p.astype(BF), vb), masks).astype(BF)
            return carry

        lax.fori_loop(0, rb, step, 0, unroll=2)

    return _pcall(
        body, [qkv, qkv, qkv, bias], name="attn_fwd", grid=(ngr, rows // rb),
        in_specs=[pl.BlockSpec((tq, LW), lambda h, r: (r, h)),
                  pl.BlockSpec((S, LW), lambda h, r: (0, ngr + h)),
                  pl.BlockSpec((S, LW), lambda h, r: (0, 2 * ngr + h)),
                  pl.BlockSpec((1, bias.shape[1], LW), lambda h, r: (h, 0, 0))],
        out_specs=[pl.BlockSpec((tq, LW), lambda h, r: (r, h))],
        out_shape=[jax.ShapeDtypeStruct((S, D_NA), BF)],
        rider=rider, edges=_edges_2d(ngr, rows // rb))


def _attn_bwd(qkv, bias, dycat, rider=None):
    S = qkv.shape[0]
    rows, rb = _attn_rows(S)
    tq = rb * GRID_W
    ngr = NA_HEADS // HG
    scale = HEAD_DIM ** -0.5

    def body(q_ref, k_ref, v_ref, b_ref, do_ref, dq_ref, dk_ref, dv_ref, db_ref, dka_ref, dva_ref):
        base = pl.program_id(1) * rb
        last = pl.program_id(1) == pl.num_programs(1) - 1
        masks = _head_masks()

        @pl.when(pl.program_id(1) == 0)
        def _():
            dka_ref[...] = jnp.zeros_like(dka_ref)
            dva_ref[...] = jnp.zeros_like(dva_ref)
            db_ref[...] = jnp.zeros_like(db_ref)

        def step(i, carry):
            qs = pl.ds(pl.multiple_of(i * GRID_W, GRID_W), GRID_W)
            p, q2, kb, vb, ks, bs = _attn_step(base + i, rows, q_ref[qs, :], k_ref, v_ref, b_ref, masks)
            do2 = _stack_heads(do_ref[qs, :].astype(BF), masks)
            dp = _nt(vb, do2)
            ds = p * (dp - jnp.sum(p * dp, axis=0, keepdims=True))
            db_ref[0, bs, :] += ds
            dsb = ds.astype(BF)
            dq_ref[qs, :] = _unstack_heads(_tn(dsb, kb) * scale, masks).astype(BF)
            dka_ref[ks, :] += _nn(dsb, q2) * scale
            dva_ref[ks, :] += _nn(p.astype(BF), do2)
            return carry

        lax.fori_loop(0, rb, step, 0, unroll=2)

        @pl.when(last)
        def _():
            dk_ref[...] = dka_ref[...].astype(BF)
            dv_ref[...] = dva_ref[...].astype(BF)

    nb = bias.shape[1]
    once = dict(pipeline_mode=pl.Buffered(1))
    nd = D_NA // LW
    return _pcall(
        body, [qkv, qkv, qkv, bias, dycat], name="attn_bwd", grid=(ngr, rows // rb),
        in_specs=[pl.BlockSpec((tq, LW), lambda h, r: (r, h)),
                  pl.BlockSpec((S, LW), lambda h, r: (0, ngr + h), **once),
                  pl.BlockSpec((S, LW), lambda h, r: (0, 2 * ngr + h), **once),
                  pl.BlockSpec((1, nb, LW), lambda h, r: (h, 0, 0)),
                  pl.BlockSpec((tq, LW), lambda h, r: (r, nd + h))],
        out_specs=[pl.BlockSpec((tq, LW), lambda h, r: (r, h)),
                   pl.BlockSpec((S, LW), lambda h, r: (0, h)),
                   pl.BlockSpec((S, LW), lambda h, r: (0, h)),
                   pl.BlockSpec((1, nb, LW), lambda h, r: (h, 0, 0))],
        out_shape=[jax.ShapeDtypeStruct((S, D_NA), BF)] * 3 + [jax.ShapeDtypeStruct((ngr, nb, LW), F32)],
        scratch=[pltpu.VMEM((S, LW), F32), pltpu.VMEM((S, LW), F32)],
        rider=rider, edges=_edges_2d(ngr, rows // rb))


def _halo_specs(tm, width, S):
    hb = tm // HALO
    last = S // HALO - 1
    return [pl.BlockSpec((tm, width), lambda i: (i, 0)),
            pl.BlockSpec((HALO, width), lambda i: (jnp.maximum(i * hb - 1, 0), 0)),
            pl.BlockSpec((HALO, width), lambda i: (jnp.minimum((i + 1) * hb, last), 0))]


def _with_halo(cur_ref, prev_ref, next_ref, i, nt):
    prev = jnp.where(i > 0, prev_ref[...], 0.0)
    nxt = jnp.where(i < nt - 1, next_ref[...], 0.0)
    return jnp.concatenate([prev, cur_ref[...], nxt], axis=0)


def _shift(a, k):
    n = a.shape[0]
    return pltpu.roll(a, k % n, 0)


def _pool_lanes(n):
    lane = lax.broadcasted_iota(jnp.int32, (n, D_POOL), 1)
    group = D_POOL // len(POOL_WINDOWS)
    return [lane < group * (j + 1) for j in range(len(POOL_WINDOWS) - 1)]


def _by_window(lanes, vals):
    return jnp.where(lanes[0], vals[0], jnp.where(lanes[1], vals[1], jnp.where(lanes[2], vals[2], vals[3])))


def _pool_count(lanes, t, S):
    back = _by_window(lanes, tuple(w // 2 for w in POOL_WINDOWS))
    lo = jnp.maximum(t - back, 0)
    hi = jnp.minimum(t + back, S)
    return jnp.maximum(hi - lo, 1).astype(F32)


def _pool_p(u, lanes, cnt):
    a = u + _shift(u, 1)
    b = _shift(a, 1) + _shift(a, -1)
    c = _shift(b, 2) + _shift(b, -2)
    d = _shift(c, 4) + _shift(c, -4)
    return _by_window(lanes, (a, b, c, d)) / cnt - u


def _mixab_fwd(pabc, wblk, vec):
    S = pabc.shape[0]
    tm = min(512, S)
    nt = S // tm
    n = tm + 2 * HALO
    tile = slice(HALO, HALO + tm)

    def body(cur_ref, prev_ref, next_ref, w_ref, vec_ref, o_ref):
        i = pl.program_id(0)
        ext = _with_halo(cur_ref, prev_ref, next_ref, i, nt)
        lanes = _pool_lanes(n)
        t = i * tm - HALO + lax.broadcasted_iota(jnp.int32, (n, D_POOL), 0)
        p = _pool_p(ext[:, 0:D_POOL], lanes, _pool_count(lanes, t, S))[tile]
        o_ref[:, 0:D_POOL] = (_nn(p.astype(BF), w_ref[...]) * vec_ref[0:1, :]).astype(BF)
        zc = ext[:, 512:768] * ext[:, 768:1024]
        conv = vec_ref[1:2, :] * _shift(zc, 1) + vec_ref[2:3, :] * zc + vec_ref[3:4, :] * _shift(zc, -1)
        o_ref[:, D_POOL:D_POOL + D_CONV] = (ext[tile, 256:512] * conv[tile]).astype(BF)

    return pl.pallas_call(
        body, name="mixab_fwd", grid=(nt,),
        in_specs=_halo_specs(tm, 1024, S) + [_full((D_POOL, D_POOL)), _full((8, D_POOL))],
        out_specs=pl.BlockSpec((tm, D_POOL + D_CONV), lambda i: (i, 0)),
        out_shape=jax.ShapeDtypeStruct((S, D_POOL + D_CONV), BF),
        compiler_params=_cp(1),
    )(pabc, pabc, pabc, wblk, vec)


def _mixab_bwd(pabc, dycat, wblk, vec):
    S = pabc.shape[0]
    tm = min(512, S)
    nt = S // tm
    n = tm + 2 * HALO
    tile = slice(HALO, HALO + tm)

    def body(cur_ref, prev_ref, next_ref, dcur_ref, dprev_ref, dnext_ref, w_ref, vec_ref, o_ref, dw_ref, dvec_ref):
        i = pl.program_id(0)

        @pl.when(i == 0)
        def _():
            dw_ref[...] = jnp.zeros_like(dw_ref)
            dvec_ref[...] = jnp.zeros_like(dvec_ref)

        ext = _with_halo(cur_ref, prev_ref, next_ref, i, nt)
        dext = _with_halo(dcur_ref, dprev_ref, dnext_ref, i, nt)
        lanes = _pool_lanes(n)
        t = i * tm - HALO + lax.broadcasted_iota(jnp.int32, (n, D_POOL), 0)
        cnt = _pool_count(lanes, t, S)
        w = w_ref[...]
        scale = vec_ref[0:1, :]
        pb = _pool_p(ext[:, 0:D_POOL], lanes, cnt)[tile].astype(BF)
        dya = dext[:, 0:D_POOL]
        dvec_ref[0:1, :] += jnp.sum(dya[tile] * _nn(pb, w), axis=0, keepdims=True)
        dqb = (dya * scale).astype(BF)
        dw_ref[...] += _tn(pb, dqb[tile])
        dp = _nt(dqb, w)
        r = dp / cnt
        a = r + _shift(r, -1)
        b = _shift(a, 1) + _shift(a, -1)
        c = _shift(b, 2) + _shift(b, -2)
        d = _shift(c, 4) + _shift(c, -4)
        o_ref[:, 0:256] = (_by_window(lanes, (a, b, c, d)) - dp)[tile].astype(BF)
        gb, gc, hh = ext[:, 256:512], ext[:, 512:768], ext[:, 768:1024]
        zc = gc * hh
        zm, zp = _shift(zc, 1), _shift(zc, -1)
        w0, w1, w2 = vec_ref[1:2, :], vec_ref[2:3, :], vec_ref[3:4, :]
        dyb = dext[:, D_POOL:D_POOL + D_CONV]
        dconv = dyb * gb
        o_ref[:, 256:512] = (dyb * (w0 * zm + w1 * zc + w2 * zp))[tile].astype(BF)
        dzc = w0 * _shift(dconv, -1) + w1 * dconv + w2 * _shift(dconv, 1)
        o_ref[:, 512:768] = (dzc * hh)[tile].astype(BF)
        o_ref[:, 768:1024] = (dzc * gc)[tile].astype(BF)
        dct = dconv[tile]
        dvec_ref[1:2, :] += jnp.sum(dct * zm[tile], axis=0, keepdims=True)
        dvec_ref[2:3, :] += jnp.sum(dct * zc[tile], axis=0, keepdims=True)
        dvec_ref[3:4, :] += jnp.sum(dct * zp[tile], axis=0, keepdims=True)

    return pl.pallas_call(
        body, name="mixab_bwd", grid=(nt,),
        in_specs=_halo_specs(tm, 1024, S) + _halo_specs(tm, 512, S) + [_full((D_POOL, D_POOL)), _full((8, D_POOL))],
        out_specs=[pl.BlockSpec((tm, 1024), lambda i: (i, 0)), _full((D_POOL, D_POOL)), _full((8, D_POOL))],
        out_shape=[jax.ShapeDtypeStruct((S, 1024), BF), jax.ShapeDtypeStruct((D_POOL, D_POOL), F32),
                   jax.ShapeDtypeStruct((8, D_POOL), F32)],
        compiler_params=_cp(1),
    )(pabc, pabc, pabc, dycat, dycat, dycat, wblk, vec)


def _mixout_fwd(yab, yc, x, wo, lg, lb):
    S, D = x.shape
    tm = min(512, S)
    h = yab.shape[1]
    k = h // 2

    def body(yab_ref, yc_ref, x_ref, w_ref, lg_ref, lb_ref, xo_ref, xb_ref, z_ref):
        y = (_nn(yab_ref[:, 0:k], w_ref[0]) + _nn(yab_ref[:, k:h], w_ref[1])
             + _nn(yc_ref[:, 0:k], w_ref[2]) + _nn(yc_ref[:, k:h], w_ref[3]))
        z = ALPHA * x_ref[...] + y
        xo = _ln_fwd(z, lg_ref[...], lb_ref[...])
        z_ref[...] = z
        xo_ref[...] = xo
        xb_ref[...] = xo.astype(BF)

    row = lambda w: pl.BlockSpec((tm, w), lambda i: (i, 0))
    return pl.pallas_call(
        body, name="mixout_fwd", grid=(S // tm,),
        in_specs=[row(h), row(h), row(D), _quarters(wo), _full((1, D)), _full((1, D))],
        out_specs=[row(D), row(D), row(D)],
        out_shape=[jax.ShapeDtypeStruct((S, D), F32), jax.ShapeDtypeStruct((S, D), BF),
                   jax.ShapeDtypeStruct((S, D), F32)],
        compiler_params=_cp(1),
    )(yab, yc, x, wo, lg, lb)


def _mixout_bwd(dxo, z, wo, lg):
    S, D = dxo.shape
    k = wo.shape[-2]
    tm = min(512, S)

    def body(dxo_ref, z_ref, w_ref, lg_ref, dres_ref, dzb_ref, dy_ref, ln_ref):
        dy = dxo_ref[...]
        dz, xhat = _ln_bwd(dy, z_ref[...], lg_ref[...])
        _acc_ln_grads(ln_ref, pl.program_id(0) == 0, dy, xhat)
        dzb = dz.astype(BF)
        dres_ref[...] = ALPHA * dz
        dzb_ref[...] = dzb
        for q in range(NQ):
            dy_ref[:, q * k:(q + 1) * k] = _nt(dzb, w_ref[q])

    row = lambda w: pl.BlockSpec((tm, w), lambda i: (i, 0))
    return pl.pallas_call(
        body, name="mixout_bwd", grid=(S // tm,),
        in_specs=[row(D), row(D), _quarters(wo), _full((1, D))],
        out_specs=[row(D), row(D), row(NQ * k), _full((8, D))],
        out_shape=[jax.ShapeDtypeStruct((S, D), F32), jax.ShapeDtypeStruct((S, D), BF),
                   jax.ShapeDtypeStruct((S, NQ * k), F32), jax.ShapeDtypeStruct((8, D), F32)],
        compiler_params=_cp(1),
    )(dxo, z, wo, lg)


def _take_cols(refs, lo, hi):
    parts, off = [], 0
    for r in refs:
        w = r.shape[1]
        a, b = max(lo, off), min(hi, off + w)
        if a < b:
            parts.append(r[:, a - off:b - off])
        off += w
    return parts[0] if len(parts) == 1 else jnp.concatenate(parts, axis=1)


def _proj_bwd(dres, dparts, wc):
    S, D = dres.shape
    n = wc.shape[-1]
    tm = min(512, S)
    np_ = len(dparts)

    def body(*refs):
        dres_ref, d_refs, w_ref, dx_ref = refs[0], refs[1:1 + np_], refs[1 + np_], refs[2 + np_]
        acc = dres_ref[...]
        for q in range(NQ):
            acc = acc + _nt(_take_cols(d_refs, q * n, (q + 1) * n), w_ref[q])
        dx_ref[...] = acc

    row = lambda w: pl.BlockSpec((tm, w), lambda i: (i, 0))
    return pl.pallas_call(
        body, name="mix_proj_bwd", grid=(S // tm,),
        in_specs=[row(D)] + [row(d.shape[1]) for d in dparts] + [_quarters(wc)],
        out_specs=row(D),
        out_shape=jax.ShapeDtypeStruct((S, D), F32),
        compiler_params=_cp(1),
    )(dres, *dparts, wc)


def _wgrad_in(a, dparts, n):
    S, K = a.shape
    ts = min(512, S)
    np_ = len(dparts)

    def body(*refs):
        a_ref, d_refs, o_ref = refs[0], refs[1:1 + np_], refs[1 + np_]

        @pl.when(pl.program_id(0) == 0)
        def _():
            o_ref[...] = jnp.zeros_like(o_ref)
        av = a_ref[...]
        for q in range(NQ):
            o_ref[q] += _tn(av, _take_cols(d_refs, q * n, (q + 1) * n))

    row = lambda w: pl.BlockSpec((ts, w), lambda s: (s, 0))
    return pl.pallas_call(
        body, name="wgrad_in", grid=(S // ts,),
        in_specs=[row(K)] + [row(d.shape[1]) for d in dparts], out_specs=_full((NQ, K, n)),
        out_shape=jax.ShapeDtypeStruct((NQ, K, n), F32),
        compiler_params=_cp(1),
    )(a, *dparts)


def _loss_head(y, target):
    S, D = y.shape
    tm = min(512, S)

    def body(y_ref, t_ref, l_ref, dy_ref):
        @pl.when(pl.program_id(0) == 0)
        def _():
            l_ref[...] = jnp.zeros_like(l_ref)
        e = y_ref[...] - t_ref[...]
        dy_ref[...] = e * (1.0 / D)
        part = jnp.sum(jnp.sum(e * e, axis=1, keepdims=True) * (1.0 / D), axis=0, keepdims=True)
        l_ref[...] += 0.5 * part

    row = pl.BlockSpec((tm, D), lambda i: (i, 0))
    return pl.pallas_call(
        body, name="loss_head", grid=(S // tm,),
        in_specs=[row, row], out_specs=[_full((8, 128)), row],
        out_shape=[jax.ShapeDtypeStruct((8, 128), F32), jax.ShapeDtypeStruct((S, D), F32)],
        compiler_params=_cp(1),
    )(y, target)


def _adamw(w, g, m, v):
    shape = w.shape
    cols = shape[-1]
    rows = int(np.prod(shape[:-1]))
    w2, g2, m2, v2 = (a.reshape(rows, cols) for a in (w, g, m, v))
    tr = rows
    for cand in (512, 352, 256):
        if rows > cand and rows % cand == 0:
            tr = cand
            break

    def body(w_ref, g_ref, m_ref, v_ref, d_ref, mo_ref, vo_ref):
        g = g_ref[...]
        mn = ADAM_B1 * m_ref[...] + (1.0 - ADAM_B1) * g
        vn = ADAM_B2 * v_ref[...] + (1.0 - ADAM_B2) * (g * g)
        m_hat = mn / (1.0 - ADAM_B1 ** ADAM_STEP)
        v_hat = vn / (1.0 - ADAM_B2 ** ADAM_STEP)
        d_ref[...] = -ADAM_LR * (m_hat / (jnp.sqrt(v_hat) + ADAM_EPS) + ADAM_WD * w_ref[...])
        mo_ref[...] = mn
        vo_ref[...] = vn

    spec = pl.BlockSpec((tr, cols), lambda i: (i, 0))
    outs = pl.pallas_call(
        body, name=f"adamw_{rows}x{cols}", grid=(rows // tr,),
        in_specs=[spec] * 4, out_specs=[spec] * 3,
        out_shape=[jax.ShapeDtypeStruct((rows, cols), F32)] * 3,
        compiler_params=_cp(1),
    )(w2, g2, m2, v2)
    return tuple(o.reshape(shape) for o in outs)


def _half_tile(h):
    return h if h <= 512 else 512


def _add_chip(g, recv):
    _, R, C = g.shape
    h = R // 2
    tr = _half_tile(h)
    nb = h // tr

    def body(a_ref, b_ref, o_ref, ob_ref):
        s = a_ref[...] + b_ref[...]
        o_ref[...] = s
        ob_ref[...] = s.astype(BF)

    half = pl.BlockSpec((1, tr, C), lambda q, i: (q, i, 0))
    mine = pl.BlockSpec((1, tr, C), lambda q, i: (q, lax.axis_index("c") * nb + i, 0))
    return pl.pallas_call(
        body, name=f"rs_add_chip_{R}x{C}", grid=(NQ, nb), in_specs=[mine, half], out_specs=[half, half],
        out_shape=[jax.ShapeDtypeStruct((NQ, h, C), F32), jax.ShapeDtypeStruct((NQ, h, C), BF)],
        compiler_params=_cp(2),
    )(g, recv)


def _add_final(chip, recv):
    _, h, C = chip.shape
    tr = _half_tile(h)
    nb = h // tr

    def body(a_ref, b_ref, o_ref):
        s = a_ref[0]
        for j in range(3):
            s = s + b_ref[j].astype(F32)
        o_ref[...] = s

    return pl.pallas_call(
        body, name=f"rs_add_final_{h}x{C}", grid=(nb,),
        in_specs=[pl.BlockSpec((1, tr, C), lambda i: (2 * lax.axis_index("x") + lax.axis_index("y"), i, 0)),
                  pl.BlockSpec((3, tr, C), lambda i: (0, i, 0))],
        out_specs=pl.BlockSpec((tr, C), lambda i: (lax.axis_index("c") * nb + i, 0)),
        out_shape=jax.ShapeDtypeStruct((2 * h, C), F32),
        compiler_params=_cp(1),
    )(chip, recv)


COMM = pltpu.CompilerParams(has_side_effects=True)


def _place():
    x, y, c = lax.axis_index("x"), lax.axis_index("y"), lax.axis_index("c")
    chips = [(1 - x, y), (x, 1 - y), (1 - x, 1 - y)]
    return x, y, c, chips


def _half0(ref, c):
    n = ref.shape[0] // 2
    return ref.at[pl.ds(c * n, n)]


def _gather_ici(shards):
    n = len(shards)

    def copies(r_in, r_out, ssem, rsem, base):
        x, y, c, chips = _place()
        q = 2 * x + y
        return [pltpu.make_async_remote_copy(
            src_ref=_half0(r_in[i], c), dst_ref=_half0(r_out[i].at[q], c), send_sem=ssem.at[base + 3 * i + j],
            recv_sem=rsem.at[base + 3 * i + j], device_id=(*chip, c), device_id_type=MESH)
            for i in range(n) for j, chip in enumerate(chips)]

    return _Rider("ici", shards, [jax.ShapeDtypeStruct((NQ,) + s.shape, BF) for s in shards], {}, 3 * n, copies)


def _gather_d2d(bufs):
    n = len(bufs)

    def copies(r_in, r_out, ssem, rsem, base):
        x, y, c, chips = _place()
        return [pltpu.make_async_remote_copy(
            src_ref=_half0(r_in[i].at[2 * cx + cy], c), dst_ref=_half0(r_out[i].at[2 * cx + cy], c),
            send_sem=ssem.at[base + 3 * i + j], recv_sem=rsem.at[base + 3 * i + j], device_id=(x, y, 1 - c),
            device_id_type=MESH) for i in range(n) for j, (cx, cy) in enumerate(chips)]

    return _Rider("d2d", bufs, [jax.ShapeDtypeStruct(b.shape, b.dtype) for b in bufs], {i: i for i in range(n)},
                  3 * n, copies)


def _gather_small(small):
    sr = small.shape[0]

    def body(s_ref, o_ref, send_sems, recv_sems):
        x, y, c, chips = _place()
        o_ref[2 * x + y] = s_ref[...]
        cps = [pltpu.make_async_remote_copy(
            src_ref=s_ref, dst_ref=o_ref.at[2 * x + y], send_sem=send_sems.at[j], recv_sem=recv_sems.at[j],
            device_id=(*chip, c), device_id_type=MESH) for j, chip in enumerate(chips)]
        for cp in cps:
            cp.start()
        for j, (cx, cy) in enumerate(chips):
            pltpu.make_async_remote_copy(
                src_ref=s_ref, dst_ref=o_ref.at[2 * cx + cy], send_sem=send_sems.at[j], recv_sem=recv_sems.at[j],
                device_id=(cx, cy, c), device_id_type=MESH).wait_recv()
        for cp in cps:
            cp.wait_send()

    vm = pl.BlockSpec(memory_space=pltpu.VMEM)
    return pl.pallas_call(
        body, name="gather_small", in_specs=[vm], out_specs=vm,
        out_shape=jax.ShapeDtypeStruct((NQ, sr, 128), F32),
        scratch_shapes=[pltpu.SemaphoreType.DMA((3,)), pltpu.SemaphoreType.DMA((3,))],
        compiler_params=COMM,
    )(small)


def _swap_halves(gs):
    n = len(gs)

    def body(*refs):
        g_refs, o_refs, send_sems, recv_sems = refs[:n], refs[n:2 * n], refs[2 * n], refs[2 * n + 1]
        x, y, c, _ = _place()
        cps = []
        for i in range(n):
            h = g_refs[i].shape[1] // 2
            cps.append(pltpu.make_async_remote_copy(
                src_ref=g_refs[i].at[:, pl.ds((1 - c) * h, h), :], dst_ref=o_refs[i], send_sem=send_sems.at[i],
                recv_sem=recv_sems.at[i], device_id=(x, y, 1 - c), device_id_type=MESH))
        for cp in cps:
            cp.start()
        for cp in cps:
            cp.wait()

    return pl.pallas_call(
        body, name="rs_swap_halves", in_specs=[ANY] * n, out_specs=[ANY] * n,
        out_shape=[jax.ShapeDtypeStruct((NQ, g.shape[1] // 2, g.shape[2]), F32) for g in gs],
        scratch_shapes=[pltpu.SemaphoreType.DMA((n,)), pltpu.SemaphoreType.DMA((n,))],
        compiler_params=COMM,
    )(*gs)


def _scatter_chips(chips_b):
    n = len(chips_b)

    def copies(r_in, r_out, ssem, rsem, base):
        x, y, c, chips = _place()
        return [pltpu.make_async_remote_copy(
            src_ref=r_in[i].at[2 * cx + cy], dst_ref=r_out[i].at[j], send_sem=ssem.at[base + 3 * i + j],
            recv_sem=rsem.at[base + 3 * i + j], device_id=(cx, cy, c), device_id_type=MESH)
            for i in range(n) for j, (cx, cy) in enumerate(chips)]

    return _Rider("scatter", chips_b, [jax.ShapeDtypeStruct((3,) + s.shape[1:], BF) for s in chips_b], {}, 3 * n,
                  copies)


def _run_alone(rider, name):
    ni, no = len(rider.ins), len(rider.outs)

    def body(*refs):
        cps = rider.copies(refs[:ni], refs[ni:ni + no], refs[ni + no], refs[ni + no + 1], 0)
        for cp in cps:
            cp.start()
        for cp in cps:
            cp.wait()

    return list(pl.pallas_call(
        body, name=name, in_specs=[ANY] * ni, out_specs=[ANY] * no, out_shape=rider.outs,
        input_output_aliases=dict(rider.aliases),
        scratch_shapes=[pltpu.SemaphoreType.DMA((rider.n,)), pltpu.SemaphoreType.DMA((rider.n,))],
        compiler_params=COMM,
    )(*rider.ins))


def _join_halves(fs):
    n = len(fs)

    def body(*refs):
        f_refs, o_refs, send_sems, recv_sems = refs[:n], refs[n:2 * n], refs[2 * n], refs[2 * n + 1]
        x, y, c, _ = _place()
        cps = []
        for i in range(n):
            h = f_refs[i].shape[0] // 2
            rows = pl.ds(c * h, h)
            cps.append(pltpu.make_async_remote_copy(
                src_ref=f_refs[i].at[rows, :], dst_ref=o_refs[i].at[rows, :], send_sem=send_sems.at[i],
                recv_sem=recv_sems.at[i], device_id=(x, y, 1 - c), device_id_type=MESH))
        for cp in cps:
            cp.start()
        for i in range(n):
            h = f_refs[i].shape[0] // 2
            theirs = o_refs[i].at[pl.ds((1 - c) * h, h), :]
            pltpu.make_async_remote_copy(
                src_ref=theirs, dst_ref=theirs, send_sem=send_sems.at[i], recv_sem=recv_sems.at[i],
                device_id=(x, y, 1 - c), device_id_type=MESH).wait_recv()
        for cp in cps:
            cp.wait_send()

    return pl.pallas_call(
        body, name="rs_join_halves", in_specs=[ANY] * n, out_specs=[ANY] * n,
        out_shape=[jax.ShapeDtypeStruct(f.shape, F32) for f in fs],
        input_output_aliases={i: i for i in range(n)},
        scratch_shapes=[pltpu.SemaphoreType.DMA((n,)), pltpu.SemaphoreType.DMA((n,))],
        compiler_params=COMM,
    )(*fs)


def _allreduce_small(v):
    r, W = v.shape

    def body(v_ref, o_ref, land_ref, send_sems, recv_sems):
        x, y, c, _ = _place()
        me = 4 * x + 2 * y + c
        cps = []
        for m in range(1, 8):
            to = (x ^ (m >> 2), y ^ ((m >> 1) & 1), c ^ (m & 1))
            cps.append(pltpu.make_async_remote_copy(
                src_ref=v_ref, dst_ref=land_ref.at[m - 1], send_sem=send_sems.at[m - 1], recv_sem=recv_sems.at[m - 1],
                device_id=to, device_id_type=MESH))
        for cp in cps:
            cp.start()
        for cp in cps:
            cp.wait()
        total = jnp.zeros((r, W), F32)
        for d in range(8):
            slot = jnp.maximum((me ^ d) - 1, 0)
            total = total + jnp.where(me == d, v_ref[...], land_ref[slot])
        o_ref[...] = total

    return pl.pallas_call(
        body, name="allreduce_small",
        in_specs=[pl.BlockSpec(memory_space=pltpu.VMEM)], out_specs=pl.BlockSpec(memory_space=pltpu.VMEM),
        out_shape=jax.ShapeDtypeStruct((r, W), F32),
        scratch_shapes=[pltpu.VMEM((7, r, W), F32), pltpu.SemaphoreType.DMA((7,)), pltpu.SemaphoreType.DMA((7,))],
        compiler_params=pltpu.CompilerParams(has_side_effects=True, vmem_limit_bytes=VMEM_LIMIT),
    )(v)


def kernel(x, ffn1_w_gate, ffn1_w_up, ffn1_w_down, ffn2_w_gate, ffn2_w_up, ffn2_w_down, w_in, pool_w, pool_scale, conv_w, rpb, w_out, ln_g, ln_b, loss_target, m_ffn1_w_gate, m_ffn1_w_up, m_ffn1_w_down, m_ffn2_w_gate, m_ffn2_w_up, m_ffn2_w_down, m_w_in, m_pool_w, m_pool_scale, m_conv_w, m_rpb, m_w_out, m_ln_g, m_ln_b, v_ffn1_w_gate, v_ffn1_w_up, v_ffn1_w_down, v_ffn2_w_gate, v_ffn2_w_up, v_ffn2_w_down, v_w_in, v_pool_w, v_pool_scale, v_conv_w, v_rpb, v_w_out, v_ln_g, v_ln_b):
    weights = dict(ffn1_w_gate=ffn1_w_gate, ffn1_w_up=ffn1_w_up, ffn1_w_down=ffn1_w_down, ffn2_w_gate=ffn2_w_gate,
                   ffn2_w_up=ffn2_w_up, ffn2_w_down=ffn2_w_down, w_in=w_in, pool_w=pool_w, pool_scale=pool_scale,
                   conv_w=conv_w, rpb=rpb, w_out=w_out, ln_g=ln_g, ln_b=ln_b)
    mom_m = dict(ffn1_w_gate=m_ffn1_w_gate, ffn1_w_up=m_ffn1_w_up, ffn1_w_down=m_ffn1_w_down, ffn2_w_gate=m_ffn2_w_gate,
                 ffn2_w_up=m_ffn2_w_up, ffn2_w_down=m_ffn2_w_down, w_in=m_w_in, pool_w=m_pool_w,
                 pool_scale=m_pool_scale, conv_w=m_conv_w, rpb=m_rpb, w_out=m_w_out, ln_g=m_ln_g, ln_b=m_ln_b)
    mom_v = dict(ffn1_w_gate=v_ffn1_w_gate, ffn1_w_up=v_ffn1_w_up, ffn1_w_down=v_ffn1_w_down, ffn2_w_gate=v_ffn2_w_gate,
                 ffn2_w_up=v_ffn2_w_up, ffn2_w_down=v_ffn2_w_down, w_in=v_w_in, pool_w=v_pool_w,
                 pool_scale=v_pool_scale, conv_w=v_conv_w, rpb=v_rpb, w_out=v_w_out, ln_g=v_ln_g, ln_b=v_ln_b)
    order = list(weights)
    L = ffn1_w_gate.shape[0]
    xi, yi, ci = lax.axis_index("x"), lax.axis_index("y"), lax.axis_index("c")
    q_me = 2 * xi + yi
    x2 = x[0]
    target = loss_target[0]
    D = x2.shape[1]
    n_in = w_in.shape[-1]

    small = jnp.concatenate([ln_g.reshape(-1), ln_b.reshape(-1), conv_w.reshape(-1)])
    n_small = small.shape[0]
    small_rows = -(-n_small // (8 * 128)) * 8
    small = jnp.pad(small, (0, small_rows * 128 - n_small)).reshape(small_rows, 128)
    small_all = _gather_small(small).reshape(NQ, small_rows * 128)[:, :n_small]
    dq4 = D // NQ
    n_ln = L * 3 * dq4
    ln_g_all = small_all[:, :n_ln].reshape(NQ, L, 3, dq4).transpose(1, 2, 0, 3).reshape(L, 3, D)
    ln_b_all = small_all[:, n_ln:2 * n_ln].reshape(NQ, L, 3, dq4).transpose(1, 2, 0, 3).reshape(L, 3, D)
    conv_all = small_all[:, 2 * n_ln:].reshape(NQ, L, 3, D_CONV // NQ).transpose(1, 2, 0, 3).reshape(L, 3, D_CONV)

    def layer_shards(l):
        return [jnp.stack([ffn1_w_gate[l], ffn1_w_up[l], ffn2_w_gate[l], ffn2_w_up[l]]).astype(BF),
                jnp.stack([ffn1_w_down[l], ffn2_w_down[l]]).astype(BF),
                w_in[l].astype(BF),
                w_out[l].astype(BF)]

    def own_quarter(bufs, shards):
        return [lax.dynamic_update_slice(b, s[None], (q_me,) + (0,) * s.ndim) for b, s in zip(bufs, shards)]

    first = layer_shards(0)
    landed = _run_alone(_gather_ici(first), "gather_ici")
    weights_of = [own_quarter(_run_alone(_gather_d2d(landed), "gather_d2d"), first)] + [None] * (L - 1)

    ng = len(POOL_WINDOWS)
    pg = D_POOL // ng
    saved = []
    h = x2
    hb = x2.astype(BF)
    for l in range(L):
        wa, wb, wc, wo = weights_of[l]
        nxt = layer_shards(l + 1) if l + 1 < L else None
        eye = jnp.eye(ng, dtype=F32)
        wblk = (pool_w[l][:, :, None, :] * eye[:, None, :, None]).reshape(D_POOL, D_POOL).astype(BF)
        vec = jnp.concatenate([pool_scale[l][None], conv_all[l], jnp.zeros((4, D_POOL), F32)], axis=0)
        bias = _bias_table(rpb[l])
        lg = [ln_g_all[l, j][None] for j in range(3)]
        lb = [ln_b_all[l, j][None] for j in range(3)]
        (x1, x1b, z1, g1, u1), got_a = _ffn_fwd(h, wa, wb, lg[0], lb[0], 0, 1, 0,
                                                rider=_gather_ici(nxt[:1]) if nxt else None)
        pabc, qkv = _proj(x1b, wc)
        yab = _mixab_fwd(pabc, wblk, vec)
        (yc,), got = _attn_fwd(qkv, bias, rider=_merge(_gather_d2d(got_a), _gather_ici(nxt[1:])) if nxt else None)
        xm, xmb, zm = _mixout_fwd(yab, yc, x1, wo, lg[1], lb[1])
        (x3, x3b, z3, g3, u3), got_rest = _ffn_fwd(xm, wa, wb, lg[2], lb[2], 2, 3, 1,
                                                   rider=_gather_d2d(got[1:]) if nxt else None)
        if nxt:
            weights_of[l + 1] = own_quarter(got[:1] + got_rest, nxt)
        saved.append(dict(wblk=wblk, vec=vec, bias=bias, lg=lg, hb=hb, z1=z1, g1=g1, u1=u1, x1b=x1b, pabc=pabc,
                          qkv=qkv, yab=yab, yc=yc, zm=zm, xmb=xmb, z3=z3, g3=g3, u3=u3))
        h, hb = x3, x3b

    loss_tile, dh = _loss_head(h, target)
    loss = lax.psum(loss_tile[0, 0], ("x", "y", "c"))

    def rs_begin(arrs):
        chip = [_add_chip(g, r) for g, r in zip(arrs, _swap_halves(arrs))]
        return [cf for cf, _ in chip], [cb for _, cb in chip]

    def rs_end(chip_f, from_chips):
        return _join_halves([_add_final(cf, r) for cf, r in zip(chip_f, from_chips)])

    per_layer = [[None] * 6 for _ in range(L)]
    g_small = dict(pool_w=[None] * L, pool_scale=[None] * L, conv_w=[None] * L, rpb=[None] * L, ln_g=[None] * L,
                   ln_b=[None] * L)
    pending = None
    for l in reversed(range(L)):
        sv = saved[l]
        wa, wb, wc, wo = weights_of[l]
        (dxm, df, dg, du, a, ln3), got = _ffn_bwd(dh, sv["z3"], sv["g3"], sv["u3"], wa, wb, sv["lg"][2], 2, 3, 1,
                                                  rider=_scatter_chips(pending[1]) if pending else None)
        if pending:
            per_layer[l + 1][0:2] = rs_end(pending[0], got)
        ffn2_f, ffn2_b = rs_begin([_wgrad_gate_up(sv["xmb"], dg, du), _wgrad_down(a, df)])
        dres, dzb, dycat, ln2 = _mixout_bwd(dxm, sv["zm"], wo, sv["lg"][1])
        g_o = _wgrad_out(sv["yab"], sv["yc"], dzb)
        dpabc, dwblk, dvec = _mixab_bwd(sv["pabc"], dycat, sv["wblk"], sv["vec"])
        (dq, dk, dv, dbias), got = _attn_bwd(sv["qkv"], sv["bias"], dycat, rider=_scatter_chips(ffn2_b))
        per_layer[l][2:4] = rs_end(ffn2_f, got)
        dparts = [dpabc, dq, dk, dv]
        mix_f, mix_b = rs_begin([_wgrad_in(sv["x1b"], dparts, n_in), g_o])
        dx1 = _proj_bwd(dres, dparts, wc)
        (dh, df, dg, du, a, ln1), got = _ffn_bwd(dx1, sv["z1"], sv["g1"], sv["u1"], wa, wb, sv["lg"][0], 0, 1, 0,
                                                 rider=_scatter_chips(mix_b))
        per_layer[l][4:6] = rs_end(mix_f, got)
        pending = rs_begin([_wgrad_gate_up(sv["hb"], dg, du), _wgrad_down(a, df)])
        g_small["pool_w"][l] = jnp.stack([dwblk[gi * pg:(gi + 1) * pg, gi * pg:(gi + 1) * pg] for gi in range(ng)])
        g_small["pool_scale"][l] = dvec[0]
        g_small["conv_w"][l] = dvec[1:4]
        g_small["rpb"][l] = _bias_grad(dbias)
        g_small["ln_g"][l] = jnp.stack([ln1[0], ln2[0], ln3[0]])
        g_small["ln_b"][l] = jnp.stack([ln1[1], ln2[1], ln3[1]])
    per_layer[0][0:2] = rs_end(pending[0], _run_alone(_scatter_chips(pending[1]), "rs_scatter_chips"))
    grad_x = dh[None]

    def stacked(i, rows=None):
        parts = [per_layer[l][i] if rows is None else per_layer[l][i][rows[0]:rows[1]] for l in range(L)]
        return jnp.stack(parts)

    grads = dict(ffn1_w_gate=stacked(0, (0, D)), ffn1_w_up=stacked(0, (D, 2 * D)), ffn1_w_down=stacked(1),
                 ffn2_w_gate=stacked(2, (0, D)), ffn2_w_up=stacked(2, (D, 2 * D)), ffn2_w_down=stacked(3),
                 w_in=stacked(4), w_out=stacked(5))

    small_names = ("pool_w", "pool_scale", "conv_w", "rpb", "ln_g", "ln_b")
    small_full = {n: jnp.stack(g_small[n]) for n in small_names}
    vflat = jnp.concatenate([small_full[n].reshape(-1) for n in small_names])
    n_v = vflat.shape[0]
    v_cols = 1024
    v_rows = -(-n_v // (8 * v_cols)) * 8
    vsum = _allreduce_small(jnp.pad(vflat, (0, v_rows * v_cols - n_v)).reshape(v_rows, v_cols)).reshape(-1)
    off = 0
    for n in small_names:
        sz = int(np.prod(small_full[n].shape))
        grads[n] = vsum[off:off + sz].reshape(small_full[n].shape)
        off += sz
    for n in ("conv_w", "ln_g", "ln_b"):
        width = weights[n].shape[-1]
        grads[n] = lax.dynamic_slice_in_dim(grads[n], q_me * width, width, axis=2)

    delta, new_m, new_v = {}, {}, {}
    for n in order:
        delta[n], new_m[n], new_v[n] = _adamw(weights[n], grads[n], mom_m[n], mom_v[n])
    return (loss, grad_x, *[grads[n] for n in order], *[delta[n] for n in order], *[new_m[n] for n in order],
            *[new_v[n] for n in order])
```

```python
import numpy as np
import jax
import jax.numpy as jnp
from jax import lax
from jax.experimental import pallas as pl
from jax.experimental.pallas import tpu as pltpu

BF = jnp.bfloat16
F32 = jnp.float32
MESH = pl.DeviceIdType.MESH

DEPTH = 4
ALPHA = (2.0 * DEPTH) ** 0.25
LN_EPS = 1e-5
NEG_INF = -1e30
GRID_W = 64
NA_ROWS = 8
NA_COLS = 16
NA_HEADS = 8
HEAD_DIM = 64
D_POOL = 256
D_CONV = 256
D_NA = 512
HG = 4
LW = HG * HEAD_DIM
POOL_WINDOWS = (2, 4, 8, 16)
HALO = 8
ADAM_LR, ADAM_B1, ADAM_B2, ADAM_EPS, ADAM_WD, ADAM_STEP = 0.001, 0.9, 0.999, 1e-08, 0.01, 10
VMEM_LIMIT = 56 * 1024 * 1024
NQ = 4
WGRAD_TOKENS = 2048


def _cp(n_axes):
    return pltpu.CompilerParams(dimension_semantics=("arbitrary",) * n_axes, vmem_limit_bytes=VMEM_LIMIT)


def _full(shape):
    nd = len(shape)
    return pl.BlockSpec(shape, lambda *_: (0,) * nd)


def _quarters(arr, k=None):
    if k is None:
        return pl.BlockSpec(arr.shape, lambda *_: (0, 0, 0), pipeline_mode=pl.Buffered(1))
    return pl.BlockSpec((NQ, None) + arr.shape[2:], lambda *_: (0, k, 0, 0), pipeline_mode=pl.Buffered(1))


ANY = pl.BlockSpec(memory_space=pl.ANY)


class _Rider:
    def __init__(self, tag, ins, outs, aliases, n, copies):
        self.tag, self.ins, self.outs, self.aliases, self.n, self.copies = tag, list(ins), list(outs), aliases, n, copies


def _merge(*riders):
    ins, outs, aliases, spans, n = [], [], {}, [], 0
    for r in riders:
        spans.append((len(ins), len(outs), n))
        aliases.update({len(ins) + i: len(outs) + j for i, j in r.aliases.items()})
        ins += r.ins
        outs += r.outs
        n += r.n

    def copies(r_in, r_out, ssem, rsem, base):
        cps = []
        for r, (i0, o0, s0) in zip(riders, spans):
            cps += r.copies(r_in[i0:i0 + len(r.ins)], r_out[o0:o0 + len(r.outs)], ssem, rsem, base + s0)
        return cps

    return _Rider("_".join(r.tag for r in riders), ins, outs, aliases, n, copies)


def _pcall(body, operands, *, name, grid, in_specs, out_specs, out_shape, scratch=(), rider=None, edges=None):
    n_in, n_out = len(in_specs), len(out_specs)
    params = dict(dimension_semantics=("arbitrary",) * len(grid), vmem_limit_bytes=VMEM_LIMIT)
    if rider is None:
        outs = pl.pallas_call(body, name=name, grid=grid, in_specs=in_specs, out_specs=out_specs, out_shape=out_shape,
                              scratch_shapes=list(scratch), compiler_params=pltpu.CompilerParams(**params))(*operands)
        return list(outs), []
    ni, no = len(rider.ins), len(rider.outs)
    first, last = edges

    def riding(*refs):
        rest = refs[n_in + ni + n_out + no:]
        cps = rider.copies(refs[n_in:n_in + ni], refs[n_in + ni + n_out:n_in + ni + n_out + no], rest[-2], rest[-1], 0)

        @pl.when(first())
        def _():
            for cp in cps:
                cp.start()

        body(*refs[:n_in], *refs[n_in + ni:n_in + ni + n_out], *rest[:-2])

        @pl.when(last())
        def _():
            for cp in cps:
                cp.wait()

    outs = pl.pallas_call(
        riding, name=f"{name}_{rider.tag}", grid=grid, in_specs=list(in_specs) + [ANY] * ni,
        out_specs=list(out_specs) + [ANY] * no, out_shape=list(out_shape) + rider.outs,
        scratch_shapes=list(scratch) + [pltpu.SemaphoreType.DMA((rider.n,)), pltpu.SemaphoreType.DMA((rider.n,))],
        input_output_aliases={n_in + i: n_out + j for i, j in rider.aliases.items()},
        compiler_params=pltpu.CompilerParams(has_side_effects=True, **params),
    )(*operands, *rider.ins)
    return list(outs[:n_out]), list(outs[n_out:])


def _edges_1d(n):
    return (lambda: pl.program_id(0) == 0), (lambda: pl.program_id(0) == n - 1)


def _edges_2d(n0, n1):
    return ((lambda: (pl.program_id(0) == 0) & (pl.program_id(1) == 0)),
            (lambda: (pl.program_id(0) == n0 - 1) & (pl.program_id(1) == n1 - 1)))


def _nt(a, b):
    return lax.dot_general(a, b, (((1,), (1,)), ((), ())), preferred_element_type=F32)


def _tn(a, b):
    return lax.dot_general(a, b, (((0,), (0,)), ((), ())), preferred_element_type=F32)


def _nn(a, b):
    return jnp.dot(a, b, preferred_element_type=F32)


def _ln_fwd(z, g, b):
    mu = jnp.mean(z, axis=-1, keepdims=True)
    zc = z - mu
    var = jnp.mean(zc * zc, axis=-1, keepdims=True)
    return zc * lax.rsqrt(var + LN_EPS) * g + b


def _ln_bwd(dy, z, g):
    mu = jnp.mean(z, axis=-1, keepdims=True)
    zc = z - mu
    var = jnp.mean(zc * zc, axis=-1, keepdims=True)
    rstd = lax.rsqrt(var + LN_EPS)
    xhat = zc * rstd
    gdy = dy * g
    m1 = jnp.mean(gdy, axis=-1, keepdims=True)
    m2 = jnp.mean(gdy * xhat, axis=-1, keepdims=True)
    return rstd * (gdy - m1 - xhat * m2), xhat


def _acc_ln_grads(acc_ref, first, dy, xhat):
    @pl.when(first)
    def _():
        acc_ref[...] = jnp.zeros_like(acc_ref)
    acc_ref[0:1, :] += jnp.sum(dy * xhat, axis=0, keepdims=True)
    acc_ref[1:2, :] += jnp.sum(dy, axis=0, keepdims=True)


def _ffn_fwd(x, wa, wb, lg, lb, kg, ku, kd, rider=None):
    S, D = x.shape
    fq = wa.shape[-1]
    tm = min(256, S)

    def body(x_ref, wg_ref, wu_ref, wd_ref, lg_ref, lb_ref, xo_ref, xb_ref, z_ref, g_ref, u_ref):
        x = x_ref[...]
        xb = x.astype(BF)
        acc = jnp.zeros((tm, D), F32)
        for q in range(NQ):
            g = _nn(xb, wg_ref[q])
            u = _nn(xb, wu_ref[q])
            g_ref[q] = g.astype(BF)
            u_ref[q] = u.astype(BF)
            a = g * jax.nn.sigmoid(g) * u
            acc = acc + _nn(a.astype(BF), wd_ref[q])
        z = ALPHA * x + 0.5 * acc
        xo = _ln_fwd(z, lg_ref[...], lb_ref[...])
        z_ref[...] = z
        xo_ref[...] = xo
        xb_ref[...] = xo.astype(BF)

    row = pl.BlockSpec((tm, D), lambda i: (i, 0))
    qrow = pl.BlockSpec((NQ, tm, fq), lambda i: (0, i, 0))
    return _pcall(
        body, [x, wa, wa, wb, lg, lb], name=f"ffn_fwd_k{kd}", grid=(S // tm,),
        in_specs=[row, _quarters(wa, kg), _quarters(wa, ku), _quarters(wb, kd), _full((1, D)), _full((1, D))],
        out_specs=[row, row, row, qrow, qrow],
        out_shape=[jax.ShapeDtypeStruct((S, D), F32), jax.ShapeDtypeStruct((S, D), BF),
                   jax.ShapeDtypeStruct((S, D), F32), jax.ShapeDtypeStruct((NQ, S, fq), BF),
                   jax.ShapeDtypeStruct((NQ, S, fq), BF)],
        rider=rider, edges=_edges_1d(S // tm))


def _ffn_bwd(dxo, z, g, u, wa, wb, lg, kg, ku, kd, rider=None):
    S, D = dxo.shape
    fq = wa.shape[-1]
    tm = min(256, S)

    def body(dxo_ref, z_ref, g_ref, u_ref, wg_ref, wu_ref, wd_ref, lg_ref,
             dx_ref, df_ref, dg_ref, du_ref, a_ref, ln_ref):
        dy = dxo_ref[...]
        dz, xhat = _ln_bwd(dy, z_ref[...], lg_ref[...])
        _acc_ln_grads(ln_ref, pl.program_id(0) == 0, dy, xhat)
        dfb = (0.5 * dz).astype(BF)
        df_ref[...] = dfb
        acc = ALPHA * dz
        for q in range(NQ):
            da = _nt(dfb, wd_ref[q])
            gg = g_ref[q].astype(F32)
            uu = u_ref[q].astype(F32)
            sg = jax.nn.sigmoid(gg)
            silu = gg * sg
            a_ref[q] = (silu * uu).astype(BF)
            dgb = (da * uu * (sg * (1.0 + gg * (1.0 - sg)))).astype(BF)
            dub = (da * silu).astype(BF)
            dg_ref[q] = dgb
            du_ref[q] = dub
            acc = acc + _nt(dgb, wg_ref[q]) + _nt(dub, wu_ref[q])
        dx_ref[...] = acc

    row = pl.BlockSpec((tm, D), lambda i: (i, 0))
    qrow = pl.BlockSpec((NQ, tm, fq), lambda i: (0, i, 0))
    qshape = jax.ShapeDtypeStruct((NQ, S, fq), BF)
    return _pcall(
        body, [dxo, z, g, u, wa, wa, wb, lg], name=f"ffn_bwd_k{kd}", grid=(S // tm,),
        in_specs=[row, row, qrow, qrow, _quarters(wa, kg), _quarters(wa, ku), _quarters(wb, kd), _full((1, D))],
        out_specs=[row, row, qrow, qrow, qrow, _full((8, D))],
        out_shape=[jax.ShapeDtypeStruct((S, D), F32), jax.ShapeDtypeStruct((S, D), BF), qshape, qshape, qshape,
                   jax.ShapeDtypeStruct((8, D), F32)],
        rider=rider, edges=_edges_1d(S // tm))


def _wgrad_gate_up(a, dg, du):
    S, K = a.shape
    n = dg.shape[-1]
    ts = min(WGRAD_TOKENS, S)

    def body(a_ref, g_ref, u_ref, o_ref):
        @pl.when(pl.program_id(1) == 0)
        def _():
            o_ref[...] = jnp.zeros_like(o_ref)
        av = a_ref[...]
        o_ref[0:K, :] += _tn(av, g_ref[...])
        o_ref[K:2 * K, :] += _tn(av, u_ref[...])

    bspec = pl.BlockSpec((None, ts, n), lambda q, s: (q, s, 0))
    return pl.pallas_call(
        body, name="wgrad_gate_up", grid=(NQ, S // ts),
        in_specs=[pl.BlockSpec((ts, K), lambda q, s: (s, 0)), bspec, bspec],
        out_specs=pl.BlockSpec((None, 2 * K, n), lambda q, s: (q, 0, 0)),
        out_shape=jax.ShapeDtypeStruct((NQ, 2 * K, n), F32),
        compiler_params=_cp(2),
    )(a, dg, du)


def _wgrad_down(a, df):
    _, S, k = a.shape
    N = df.shape[1]
    ts = min(WGRAD_TOKENS, S)

    def body(a_ref, b_ref, o_ref):
        @pl.when(pl.program_id(1) == 0)
        def _():
            o_ref[...] = jnp.zeros_like(o_ref)
        o_ref[...] += _tn(a_ref[...], b_ref[...])

    return pl.pallas_call(
        body, name="wgrad_down", grid=(NQ, S // ts),
        in_specs=[pl.BlockSpec((None, ts, k), lambda q, s: (q, s, 0)), pl.BlockSpec((ts, N), lambda q, s: (s, 0))],
        out_specs=pl.BlockSpec((None, k, N), lambda q, s: (q, 0, 0)),
        out_shape=jax.ShapeDtypeStruct((NQ, k, N), F32),
        compiler_params=_cp(2),
    )(a, df)


def _wgrad_out(yab, yc, dzb):
    S, h = yab.shape
    D = dzb.shape[1]
    k = h // 2
    ts = min(WGRAD_TOKENS, S)

    def body(yab_ref, yc_ref, b_ref, o_ref):
        @pl.when(pl.program_id(0) == 0)
        def _():
            o_ref[...] = jnp.zeros_like(o_ref)
        b = b_ref[...]
        o_ref[0] += _tn(yab_ref[:, 0:k], b)
        o_ref[1] += _tn(yab_ref[:, k:h], b)
        o_ref[2] += _tn(yc_ref[:, 0:k], b)
        o_ref[3] += _tn(yc_ref[:, k:h], b)

    row = lambda w: pl.BlockSpec((ts, w), lambda s: (s, 0))
    return pl.pallas_call(
        body, name="wgrad_out", grid=(S // ts,),
        in_specs=[row(h), row(h), row(D)], out_specs=_full((NQ, k, D)),
        out_shape=jax.ShapeDtypeStruct((NQ, k, D), F32),
        compiler_params=_cp(1),
    )(yab, yc, dzb)


def _proj(xb, wc):
    S, D = xb.shape
    n = wc.shape[-1]
    n1 = D_POOL + 3 * D_CONV
    n2 = NQ * n - n1
    tm = min(512, S)

    def body(x_ref, w_ref, p_ref, qkv_ref):
        x = x_ref[...]
        for q in range(NQ):
            r = _nn(x, w_ref[q])
            lo, hi = q * n, (q + 1) * n
            if hi <= n1:
                p_ref[:, lo:hi] = r
            elif lo >= n1:
                qkv_ref[:, lo - n1:hi - n1] = r.astype(BF)
            else:
                p_ref[:, lo:n1] = r[:, 0:n1 - lo]
                qkv_ref[:, 0:hi - n1] = r[:, n1 - lo:n].astype(BF)

    row = lambda w: pl.BlockSpec((tm, w), lambda i: (i, 0))
    return pl.pallas_call(
        body, name="mix_proj", grid=(S // tm,),
        in_specs=[row(D), _quarters(wc)],
        out_specs=[row(n1), row(n2)],
        out_shape=[jax.ShapeDtypeStruct((S, n1), F32), jax.ShapeDtypeStruct((S, n2), BF)],
        compiler_params=_cp(1),
    )(xb, wc)


def _mm_exact(a, b, name):
    def body(a_ref, b_ref, o_ref):
        o_ref[...] = jnp.dot(a_ref[...], b_ref[...], preferred_element_type=F32, precision=lax.Precision.HIGHEST)

    return pl.pallas_call(
        body, name=name, in_specs=[_full(a.shape), _full(b.shape)], out_specs=_full((a.shape[0], b.shape[1])),
        out_shape=jax.ShapeDtypeStruct((a.shape[0], b.shape[1]), F32),
        compiler_params=pltpu.CompilerParams(vmem_limit_bytes=VMEM_LIMIT),
    )(a, b)


def _bias_constants():
    c = np.arange(GRID_W)
    col_start = np.clip(c - NA_COLS // 2, 0, GRID_W - NA_COLS)
    valid = (c[None, :] >= col_start[:, None]) & (c[None, :] < col_start[:, None] + NA_COLS)
    dc = np.clip(c[None, :] - c[:, None], -(NA_COLS - 1), NA_COLS - 1) + (NA_COLS - 1)
    onehot = np.zeros((32, GRID_W * GRID_W), np.float32)
    onehot[dc.reshape(-1), np.arange(GRID_W * GRID_W)] = 1.0
    mask_kq = np.where(valid.T, 0.0, NEG_INF).astype(np.float32)
    mask = np.tile(mask_kq, (2 * NA_ROWS - 1, HG))
    return onehot, mask


def _bias_table(rpb):
    onehot, mask = _bias_constants()
    nr = 2 * NA_ROWS - 1
    r2 = jnp.pad(rpb.reshape(NA_HEADS * nr, 2 * NA_COLS - 1), ((0, 0), (0, 1)))
    t = _mm_exact(r2, jnp.asarray(onehot), "bias_expand")
    t = t.reshape(NA_HEADS // HG, HG, nr, GRID_W, GRID_W).transpose(0, 2, 4, 1, 3)
    return t.reshape(NA_HEADS // HG, nr * GRID_W, LW) + jnp.asarray(mask)[None]


def _bias_grad(dt):
    onehot, _ = _bias_constants()
    nr = 2 * NA_ROWS - 1
    d = dt.reshape(NA_HEADS // HG, nr, GRID_W, HG, GRID_W).transpose(0, 3, 1, 4, 2).reshape(NA_HEADS * nr, -1)
    g = _mm_exact(d, jnp.asarray(onehot.T.copy()), "bias_reduce")
    return g[:, :2 * NA_COLS - 1].reshape(NA_HEADS, nr, 2 * NA_COLS - 1)


def _attn_rows(S):
    rows = S // GRID_W
    rb = min(16, rows)
    return rows, rb


def _head_masks():
    lane = lax.broadcasted_iota(jnp.int32, (GRID_W, LW), 1)
    return [(lane >= HEAD_DIM * h) & (lane < HEAD_DIM * (h + 1)) for h in range(HG)]


def _stack_heads(x, masks):
    zero = jnp.zeros_like(x)
    return jnp.concatenate([jnp.where(m, x, zero) for m in masks], axis=0)


def _unstack_heads(x2, masks):
    out = x2[0:GRID_W]
    for h in range(1, HG):
        out = jnp.where(masks[h], x2[h * GRID_W:(h + 1) * GRID_W], out)
    return out


def _attn_step(r, rows, q, k_ref, v_ref, b_ref, masks):
    rs = jnp.clip(r - NA_ROWS // 2, 0, rows - NA_ROWS)
    s0 = rs - r + (NA_ROWS - 1)
    q2 = _stack_heads(q, masks)
    ks = pl.ds(pl.multiple_of(rs * GRID_W, GRID_W), NA_ROWS * GRID_W)
    kb = k_ref[ks, :]
    vb = v_ref[ks, :]
    bs = pl.ds(pl.multiple_of(s0 * GRID_W, GRID_W), NA_ROWS * GRID_W)
    s = _nt(kb, q2) * (HEAD_DIM ** -0.5) + b_ref[0, bs, :]
    m = jnp.max(s, axis=0, keepdims=True)
    p = jnp.exp(s - m)
    p = p / jnp.sum(p, axis=0, keepdims=True)
    return p, q2, kb, vb, ks, bs


def _attn_fwd(qkv, bias, rider=None):
    S = qkv.shape[0]
    rows, rb = _attn_rows(S)
    tq = rb * GRID_W
    ngr = NA_HEADS // HG

    def body(q_ref, k_ref, v_ref, b_ref, o_ref):
        base = pl.program_id(1) * rb
        masks = _head_masks()

        def step(i, carry):
            qs = pl.ds(pl.multiple_of(i * GRID_W, GRID_W), GRID_W)
            p, _, _, vb, _, _ = _attn_step(base + i, rows, q_ref[qs, :], k_ref, v_ref, b_ref, masks)
            o_ref[qs, :] = _unstack_heads(_tn(p.astype(BF), vb), masks).astype(BF)
            return carry

        lax.fori_loop(0, rb, step, 0, unroll=2)

    return _pcall(
        body, [qkv, qkv, qkv, bias], name="attn_fwd", grid=(ngr, rows // rb),
        in_specs=[pl.BlockSpec((tq, LW), lambda h, r: (r, h)),
                  pl.BlockSpec((S, LW), lambda h, r: (0, ngr + h)),
                  pl.BlockSpec((S, LW), lambda h, r: (0, 2 * ngr + h)),
                  pl.BlockSpec((1, bias.shape[1], LW), lambda h, r: (h, 0, 0))],
        out_specs=[pl.BlockSpec((tq, LW), lambda h, r: (r, h))],
        out_shape=[jax.ShapeDtypeStruct((S, D_NA), BF)],
        rider=rider, edges=_edges_2d(ngr, rows // rb))


def _attn_bwd(qkv, bias, dycat, rider=None):
    S = qkv.shape[0]
    rows, rb = _attn_rows(S)
    tq = rb * GRID_W
    ngr = NA_HEADS // HG
    scale = HEAD_DIM ** -0.5

    def body(q_ref, k_ref, v_ref, b_ref, do_ref, dq_ref, dk_ref, dv_ref, db_ref, dka_ref, dva_ref):
        base = pl.program_id(1) * rb
        last = pl.program_id(1) == pl.num_programs(1) - 1
        masks = _head_masks()

        @pl.when(pl.program_id(1) == 0)
        def _():
            dka_ref[...] = jnp.zeros_like(dka_ref)
            dva_ref[...] = jnp.zeros_like(dva_ref)
            db_ref[...] = jnp.zeros_like(db_ref)

        def step(i, carry):
            qs = pl.ds(pl.multiple_of(i * GRID_W, GRID_W), GRID_W)
            p, q2, kb, vb, ks, bs = _attn_step(base + i, rows, q_ref[qs, :], k_ref, v_ref, b_ref, masks)
            do2 = _stack_heads(do_ref[qs, :].astype(BF), masks)
            dp = _nt(vb, do2)
            ds = p * (dp - jnp.sum(p * dp, axis=0, keepdims=True))
            db_ref[0, bs, :] += ds
            dsb = ds.astype(BF)
            dq_ref[qs, :] = _unstack_heads(_tn(dsb, kb) * scale, masks).astype(BF)
            dka_ref[ks, :] += _nn(dsb, q2) * scale
            dva_ref[ks, :] += _nn(p.astype(BF), do2)
            return carry

        lax.fori_loop(0, rb, step, 0, unroll=2)

        @pl.when(last)
        def _():
            dk_ref[...] = dka_ref[...].astype(BF)
            dv_ref[...] = dva_ref[...].astype(BF)

    nb = bias.shape[1]
    once = dict(pipeline_mode=pl.Buffered(1))
    nd = D_NA // LW
    return _pcall(
        body, [qkv, qkv, qkv, bias, dycat], name="attn_bwd", grid=(ngr, rows // rb),
        in_specs=[pl.BlockSpec((tq, LW), lambda h, r: (r, h)),
                  pl.BlockSpec((S, LW), lambda h, r: (0, ngr + h), **once),
                  pl.BlockSpec((S, LW), lambda h, r: (0, 2 * ngr + h), **once),
                  pl.BlockSpec((1, nb, LW), lambda h, r: (h, 0, 0)),
                  pl.BlockSpec((tq, LW), lambda h, r: (r, nd + h))],
        out_specs=[pl.BlockSpec((tq, LW), lambda h, r: (r, h)),
                   pl.BlockSpec((S, LW), lambda h, r: (0, h)),
                   pl.BlockSpec((S, LW), lambda h, r: (0, h)),
                   pl.BlockSpec((1, nb, LW), lambda h, r: (h, 0, 0))],
        out_shape=[jax.ShapeDtypeStruct((S, D_NA), BF)] * 3 + [jax.ShapeDtypeStruct((ngr, nb, LW), F32)],
        scratch=[pltpu.VMEM((S, LW), F32), pltpu.VMEM((S, LW), F32)],
        rider=rider, edges=_edges_2d(ngr, rows // rb))


def _halo_specs(tm, width, S):
    hb = tm // HALO
    last = S // HALO - 1
    return [pl.BlockSpec((tm, width), lambda i: (i, 0)),
            pl.BlockSpec((HALO, width), lambda i: (jnp.maximum(i * hb - 1, 0), 0)),
            pl.BlockSpec((HALO, width), lambda i: (jnp.minimum((i + 1) * hb, last), 0))]


def _with_halo(cur_ref, prev_ref, next_ref, i, nt):
    prev = jnp.where(i > 0, prev_ref[...], 0.0)
    nxt = jnp.where(i < nt - 1, next_ref[...], 0.0)
    return jnp.concatenate([prev, cur_ref[...], nxt], axis=0)


def _shift(a, k):
    n = a.shape[0]
    return pltpu.roll(a, k % n, 0)


def _pool_lanes(n):
    lane = lax.broadcasted_iota(jnp.int32, (n, D_POOL), 1)
    group = D_POOL // len(POOL_WINDOWS)
    return [lane < group * (j + 1) for j in range(len(POOL_WINDOWS) - 1)]


def _by_window(lanes, vals):
    return jnp.where(lanes[0], vals[0], jnp.where(lanes[1], vals[1], jnp.where(lanes[2], vals[2], vals[3])))


def _pool_count(lanes, t, S):
    back = _by_window(lanes, tuple(w // 2 for w in POOL_WINDOWS))
    lo = jnp.maximum(t - back, 0)
    hi = jnp.minimum(t + back, S)
    return jnp.maximum(hi - lo, 1).astype(F32)


def _pool_p(u, lanes, cnt):
    a = u + _shift(u, 1)
    b = _shift(a, 1) + _shift(a, -1)
    c = _shift(b, 2) + _shift(b, -2)
    d = _shift(c, 4) + _shift(c, -4)
    return _by_window(lanes, (a, b, c, d)) / cnt - u


def _mixab_fwd(pabc, wblk, vec):
    S = pabc.shape[0]
    tm = min(512, S)
    nt = S // tm
    n = tm + 2 * HALO
    tile = slice(HALO, HALO + tm)

    def body(cur_ref, prev_ref, next_ref, w_ref, vec_ref, o_ref):
        i = pl.program_id(0)
        ext = _with_halo(cur_ref, prev_ref, next_ref, i, nt)
        lanes = _pool_lanes(n)
        t = i * tm - HALO + lax.broadcasted_iota(jnp.int32, (n, D_POOL), 0)
        p = _pool_p(ext[:, 0:D_POOL], lanes, _pool_count(lanes, t, S))[tile]
        o_ref[:, 0:D_POOL] = (_nn(p.astype(BF), w_ref[...]) * vec_ref[0:1, :]).astype(BF)
        zc = ext[:, 512:768] * ext[:, 768:1024]
        conv = vec_ref[1:2, :] * _shift(zc, 1) + vec_ref[2:3, :] * zc + vec_ref[3:4, :] * _shift(zc, -1)
        o_ref[:, D_POOL:D_POOL + D_CONV] = (ext[tile, 256:512] * conv[tile]).astype(BF)

    return pl.pallas_call(
        body, name="mixab_fwd", grid=(nt,),
        in_specs=_halo_specs(tm, 1024, S) + [_full((D_POOL, D_POOL)), _full((8, D_POOL))],
        out_specs=pl.BlockSpec((tm, D_POOL + D_CONV), lambda i: (i, 0)),
        out_shape=jax.ShapeDtypeStruct((S, D_POOL + D_CONV), BF),
        compiler_params=_cp(1),
    )(pabc, pabc, pabc, wblk, vec)


def _mixab_bwd(pabc, dycat, wblk, vec):
    S = pabc.shape[0]
    tm = min(512, S)
    nt = S // tm
    n = tm + 2 * HALO
    tile = slice(HALO, HALO + tm)

    def body(cur_ref, prev_ref, next_ref, dcur_ref, dprev_ref, dnext_ref, w_ref, vec_ref, o_ref, dw_ref, dvec_ref):
        i = pl.program_id(0)

        @pl.when(i == 0)
        def _():
            dw_ref[...] = jnp.zeros_like(dw_ref)
            dvec_ref[...] = jnp.zeros_like(dvec_ref)

        ext = _with_halo(cur_ref, prev_ref, next_ref, i, nt)
        dext = _with_halo(dcur_ref, dprev_ref, dnext_ref, i, nt)
        lanes = _pool_lanes(n)
        t = i * tm - HALO + lax.broadcasted_iota(jnp.int32, (n, D_POOL), 0)
        cnt = _pool_count(lanes, t, S)
        w = w_ref[...]
        scale = vec_ref[0:1, :]
        pb = _pool_p(ext[:, 0:D_POOL], lanes, cnt)[tile].astype(BF)
        dya = dext[:, 0:D_POOL]
        dvec_ref[0:1, :] += jnp.sum(dya[tile] * _nn(pb, w), axis=0, keepdims=True)
        dqb = (dya * scale).astype(BF)
        dw_ref[...] += _tn(pb, dqb[tile])
        dp = _nt(dqb, w)
        r = dp / cnt
        a = r + _shift(r, -1)
        b = _shift(a, 1) + _shift(a, -1)
        c = _shift(b, 2) + _shift(b, -2)
        d = _shift(c, 4) + _shift(c, -4)
        o_ref[:, 0:256] = (_by_window(lanes, (a, b, c, d)) - dp)[tile].astype(BF)
        gb, gc, hh = ext[:, 256:512], ext[:, 512:768], ext[:, 768:1024]
        zc = gc * hh
        zm, zp = _shift(zc, 1), _shift(zc, -1)
        w0, w1, w2 = vec_ref[1:2, :], vec_ref[2:3, :], vec_ref[3:4, :]
        dyb = dext[:, D_POOL:D_POOL + D_CONV]
        dconv = dyb * gb
        o_ref[:, 256:512] = (dyb * (w0 * zm + w1 * zc + w2 * zp))[tile].astype(BF)
        dzc = w0 * _shift(dconv, -1) + w1 * dconv + w2 * _shift(dconv, 1)
        o_ref[:, 512:768] = (dzc * hh)[tile].astype(BF)
        o_ref[:, 768:1024] = (dzc * gc)[tile].astype(BF)
        dct = dconv[tile]
        dvec_ref[1:2, :] += jnp.sum(dct * zm[tile], axis=0, keepdims=True)
        dvec_ref[2:3, :] += jnp.sum(dct * zc[tile], axis=0, keepdims=True)
        dvec_ref[3:4, :] += jnp.sum(dct * zp[tile], axis=0, keepdims=True)

    return pl.pallas_call(
        body, name="mixab_bwd", grid=(nt,),
        in_specs=_halo_specs(tm, 1024, S) + _halo_specs(tm, 512, S) + [_full((D_POOL, D_POOL)), _full((8, D_POOL))],
        out_specs=[pl.BlockSpec((tm, 1024), lambda i: (i, 0)), _full((D_POOL, D_POOL)), _full((8, D_POOL))],
        out_shape=[jax.ShapeDtypeStruct((S, 1024), BF), jax.ShapeDtypeStruct((D_POOL, D_POOL), F32),
                   jax.ShapeDtypeStruct((8, D_POOL), F32)],
        compiler_params=_cp(1),
    )(pabc, pabc, pabc, dycat, dycat, dycat, wblk, vec)


def _mixout_fwd(yab, yc, x, wo, lg, lb):
    S, D = x.shape
    tm = min(512, S)
    h = yab.shape[1]
    k = h // 2

    def body(yab_ref, yc_ref, x_ref, w_ref, lg_ref, lb_ref, xo_ref, xb_ref, z_ref):
        y = (_nn(yab_ref[:, 0:k], w_ref[0]) + _nn(yab_ref[:, k:h], w_ref[1])
             + _nn(yc_ref[:, 0:k], w_ref[2]) + _nn(yc_ref[:, k:h], w_ref[3]))
        z = ALPHA * x_ref[...] + y
        xo = _ln_fwd(z, lg_ref[...], lb_ref[...])
        z_ref[...] = z
        xo_ref[...] = xo
        xb_ref[...] = xo.astype(BF)

    row = lambda w: pl.BlockSpec((tm, w), lambda i: (i, 0))
    return pl.pallas_call(
        body, name="mixout_fwd", grid=(S // tm,),
        in_specs=[row(h), row(h), row(D), _quarters(wo), _full((1, D)), _full((1, D))],
        out_specs=[row(D), row(D), row(D)],
        out_shape=[jax.ShapeDtypeStruct((S, D), F32), jax.ShapeDtypeStruct((S, D), BF),
                   jax.ShapeDtypeStruct((S, D), F32)],
        compiler_params=_cp(1),
    )(yab, yc, x, wo, lg, lb)


def _mixout_bwd(dxo, z, wo, lg, rider=None):
    S, D = dxo.shape
    k = wo.shape[-2]
    tm = min(512, S)

    def body(dxo_ref, z_ref, w_ref, lg_ref, dres_ref, dzb_ref, dy_ref, ln_ref):
        dy = dxo_ref[...]
        dz, xhat = _ln_bwd(dy, z_ref[...], lg_ref[...])
        _acc_ln_grads(ln_ref, pl.program_id(0) == 0, dy, xhat)
        dzb = dz.astype(BF)
        dres_ref[...] = ALPHA * dz
        dzb_ref[...] = dzb
        for q in range(NQ):
            dy_ref[:, q * k:(q + 1) * k] = _nt(dzb, w_ref[q])

    row = lambda w: pl.BlockSpec((tm, w), lambda i: (i, 0))
    return _pcall(
        body, [dxo, z, wo, lg], name="mixout_bwd", grid=(S // tm,),
        in_specs=[row(D), row(D), _quarters(wo), _full((1, D))],
        out_specs=[row(D), row(D), row(NQ * k), _full((8, D))],
        out_shape=[jax.ShapeDtypeStruct((S, D), F32), jax.ShapeDtypeStruct((S, D), BF),
                   jax.ShapeDtypeStruct((S, NQ * k), F32), jax.ShapeDtypeStruct((8, D), F32)],
        rider=rider, edges=_edges_1d(S // tm))


def _take_cols(refs, lo, hi):
    parts, off = [], 0
    for r in refs:
        w = r.shape[1]
        a, b = max(lo, off), min(hi, off + w)
        if a < b:
            parts.append(r[:, a - off:b - off])
        off += w
    return parts[0] if len(parts) == 1 else jnp.concatenate(parts, axis=1)


def _proj_bwd(dres, dparts, wc, rider=None):
    S, D = dres.shape
    n = wc.shape[-1]
    tm = min(512, S)
    np_ = len(dparts)

    def body(*refs):
        dres_ref, d_refs, w_ref, dx_ref = refs[0], refs[1:1 + np_], refs[1 + np_], refs[2 + np_]
        acc = dres_ref[...]
        for q in range(NQ):
            acc = acc + _nt(_take_cols(d_refs, q * n, (q + 1) * n), w_ref[q])
        dx_ref[...] = acc

    row = lambda w: pl.BlockSpec((tm, w), lambda i: (i, 0))
    return _pcall(
        body, [dres, *dparts, wc], name="mix_proj_bwd", grid=(S // tm,),
        in_specs=[row(D)] + [row(d.shape[1]) for d in dparts] + [_quarters(wc)],
        out_specs=[row(D)],
        out_shape=[jax.ShapeDtypeStruct((S, D), F32)],
        rider=rider, edges=_edges_1d(S // tm))


def _wgrad_in(a, dparts, n):
    S, K = a.shape
    ts = min(WGRAD_TOKENS // 2, S)
    np_ = len(dparts)

    def body(*refs):
        a_ref, d_refs, o_ref = refs[0], refs[1:1 + np_], refs[1 + np_]

        @pl.when(pl.program_id(0) == 0)
        def _():
            o_ref[...] = jnp.zeros_like(o_ref)
        av = a_ref[...]
        for q in range(NQ):
            o_ref[q] += _tn(av, _take_cols(d_refs, q * n, (q + 1) * n))

    row = lambda w: pl.BlockSpec((ts, w), lambda s: (s, 0))
    return pl.pallas_call(
        body, name="wgrad_in", grid=(S // ts,),
        in_specs=[row(K)] + [row(d.shape[1]) for d in dparts], out_specs=_full((NQ, K, n)),
        out_shape=jax.ShapeDtypeStruct((NQ, K, n), F32),
        compiler_params=_cp(1),
    )(a, *dparts)


def _loss_head(y, target):
    S, D = y.shape
    tm = min(512, S)

    def body(y_ref, t_ref, l_ref, dy_ref):
        @pl.when(pl.program_id(0) == 0)
        def _():
            l_ref[...] = jnp.zeros_like(l_ref)
        e = y_ref[...] - t_ref[...]
        dy_ref[...] = e * (1.0 / D)
        part = jnp.sum(jnp.sum(e * e, axis=1, keepdims=True) * (1.0 / D), axis=0, keepdims=True)
        l_ref[...] += 0.5 * part

    row = pl.BlockSpec((tm, D), lambda i: (i, 0))
    return pl.pallas_call(
        body, name="loss_head", grid=(S // tm,),
        in_specs=[row, row], out_specs=[_full((8, 128)), row],
        out_shape=[jax.ShapeDtypeStruct((8, 128), F32), jax.ShapeDtypeStruct((S, D), F32)],
        compiler_params=_cp(1),
    )(y, target)


def _adamw(w, g, m, v):
    shape = w.shape
    cols = shape[-1]
    rows = int(np.prod(shape[:-1]))
    w2, g2, m2, v2 = (a.reshape(rows, cols) for a in (w, g, m, v))
    tr = rows
    for cand in (512, 352, 256):
        if rows > cand and rows % cand == 0:
            tr = cand
            break

    def body(w_ref, g_ref, m_ref, v_ref, d_ref, mo_ref, vo_ref):
        g = g_ref[...]
        mn = ADAM_B1 * m_ref[...] + (1.0 - ADAM_B1) * g
        vn = ADAM_B2 * v_ref[...] + (1.0 - ADAM_B2) * (g * g)
        m_hat = mn / (1.0 - ADAM_B1 ** ADAM_STEP)
        v_hat = vn / (1.0 - ADAM_B2 ** ADAM_STEP)
        d_ref[...] = -ADAM_LR * (m_hat / (jnp.sqrt(v_hat) + ADAM_EPS) + ADAM_WD * w_ref[...])
        mo_ref[...] = mn
        vo_ref[...] = vn

    spec = pl.BlockSpec((tr, cols), lambda i: (i, 0))
    outs = pl.pallas_call(
        body, name=f"adamw_{rows}x{cols}", grid=(rows // tr,),
        in_specs=[spec] * 4, out_specs=[spec] * 3,
        out_shape=[jax.ShapeDtypeStruct((rows, cols), F32)] * 3,
        compiler_params=_cp(1),
    )(w2, g2, m2, v2)
    return tuple(o.reshape(shape) for o in outs)


def _half_tile(h):
    return h if h <= 512 else 512


def _add_chip(g, recv):
    _, R, C = g.shape
    h = R // 2
    tr = _half_tile(h)
    nb = h // tr

    def body(a_ref, b_ref, o_ref, ob_ref):
        s = a_ref[...] + b_ref[...]
        o_ref[...] = s
        ob_ref[...] = s.astype(BF)

    half = pl.BlockSpec((1, tr, C), lambda q, i: (q, i, 0))
    mine = pl.BlockSpec((1, tr, C), lambda q, i: (q, lax.axis_index("c") * nb + i, 0))
    return pl.pallas_call(
        body, name=f"rs_add_chip_{R}x{C}", grid=(NQ, nb), in_specs=[mine, half], out_specs=[half, half],
        out_shape=[jax.ShapeDtypeStruct((NQ, h, C), F32), jax.ShapeDtypeStruct((NQ, h, C), BF)],
        compiler_params=_cp(2),
    )(g, recv)


def _add_final(chip, recv):
    _, h, C = chip.shape
    tr = _half_tile(h)
    nb = h // tr

    def body(a_ref, b_ref, o_ref):
        s = a_ref[0]
        for j in range(3):
            s = s + b_ref[j].astype(F32)
        o_ref[...] = s

    return pl.pallas_call(
        body, name=f"rs_add_final_{h}x{C}", grid=(nb,),
        in_specs=[pl.BlockSpec((1, tr, C), lambda i: (2 * lax.axis_index("x") + lax.axis_index("y"), i, 0)),
                  pl.BlockSpec((3, tr, C), lambda i: (0, i, 0))],
        out_specs=pl.BlockSpec((tr, C), lambda i: (lax.axis_index("c") * nb + i, 0)),
        out_shape=jax.ShapeDtypeStruct((2 * h, C), F32),
        compiler_params=_cp(1),
    )(chip, recv)


COMM = pltpu.CompilerParams(has_side_effects=True)


def _place():
    x, y, c = lax.axis_index("x"), lax.axis_index("y"), lax.axis_index("c")
    chips = [(1 - x, y), (x, 1 - y), (1 - x, 1 - y)]
    return x, y, c, chips


def _half0(ref, c):
    n = ref.shape[0] // 2
    return ref.at[pl.ds(c * n, n)]


def _gather_ici(shards):
    n = len(shards)

    def copies(r_in, r_out, ssem, rsem, base):
        x, y, c, chips = _place()
        q = 2 * x + y
        return [pltpu.make_async_remote_copy(
            src_ref=_half0(r_in[i], c), dst_ref=_half0(r_out[i].at[q], c), send_sem=ssem.at[base + 3 * i + j],
            recv_sem=rsem.at[base + 3 * i + j], device_id=(*chip, c), device_id_type=MESH)
            for i in range(n) for j, chip in enumerate(chips)]

    return _Rider("ici", shards, [jax.ShapeDtypeStruct((NQ,) + s.shape, BF) for s in shards], {}, 3 * n, copies)


def _gather_d2d(bufs):
    n = len(bufs)

    def copies(r_in, r_out, ssem, rsem, base):
        x, y, c, chips = _place()
        return [pltpu.make_async_remote_copy(
            src_ref=_half0(r_in[i].at[2 * cx + cy], c), dst_ref=_half0(r_out[i].at[2 * cx + cy], c),
            send_sem=ssem.at[base + 3 * i + j], recv_sem=rsem.at[base + 3 * i + j], device_id=(x, y, 1 - c),
            device_id_type=MESH) for i in range(n) for j, (cx, cy) in enumerate(chips)]

    return _Rider("d2d", bufs, [jax.ShapeDtypeStruct(b.shape, b.dtype) for b in bufs], {i: i for i in range(n)},
                  3 * n, copies)


def _gather_small(small):
    sr = small.shape[0]

    def body(s_ref, o_ref, send_sems, recv_sems):
        x, y, c, chips = _place()
        o_ref[2 * x + y] = s_ref[...]
        cps = [pltpu.make_async_remote_copy(
            src_ref=s_ref, dst_ref=o_ref.at[2 * x + y], send_sem=send_sems.at[j], recv_sem=recv_sems.at[j],
            device_id=(*chip, c), device_id_type=MESH) for j, chip in enumerate(chips)]
        for cp in cps:
            cp.start()
        for j, (cx, cy) in enumerate(chips):
            pltpu.make_async_remote_copy(
                src_ref=s_ref, dst_ref=o_ref.at[2 * cx + cy], send_sem=send_sems.at[j], recv_sem=recv_sems.at[j],
                device_id=(cx, cy, c), device_id_type=MESH).wait_recv()
        for cp in cps:
            cp.wait_send()

    vm = pl.BlockSpec(memory_space=pltpu.VMEM)
    return pl.pallas_call(
        body, name="gather_small", in_specs=[vm], out_specs=vm,
        out_shape=jax.ShapeDtypeStruct((NQ, sr, 128), F32),
        scratch_shapes=[pltpu.SemaphoreType.DMA((3,)), pltpu.SemaphoreType.DMA((3,))],
        compiler_params=COMM,
    )(small)


def _swap_halves(gs):
    n = len(gs)

    def copies(r_in, r_out, ssem, rsem, base):
        x, y, c, _ = _place()
        cps = []
        for i in range(n):
            h = r_in[i].shape[1] // 2
            cps.append(pltpu.make_async_remote_copy(
                src_ref=r_in[i].at[:, pl.ds((1 - c) * h, h), :], dst_ref=r_out[i], send_sem=ssem.at[base + i],
                recv_sem=rsem.at[base + i], device_id=(x, y, 1 - c), device_id_type=MESH))
        return cps

    return _Rider("swap", gs, [jax.ShapeDtypeStruct((NQ, g.shape[1] // 2, g.shape[2]), F32) for g in gs], {}, n,
                  copies)


def _scatter_chips(chips_b):
    n = len(chips_b)

    def copies(r_in, r_out, ssem, rsem, base):
        x, y, c, chips = _place()
        return [pltpu.make_async_remote_copy(
            src_ref=r_in[i].at[2 * cx + cy], dst_ref=r_out[i].at[j], send_sem=ssem.at[base + 3 * i + j],
            recv_sem=rsem.at[base + 3 * i + j], device_id=(cx, cy, c), device_id_type=MESH)
            for i in range(n) for j, (cx, cy) in enumerate(chips)]

    return _Rider("scatter", chips_b, [jax.ShapeDtypeStruct((3,) + s.shape[1:], BF) for s in chips_b], {}, 3 * n,
                  copies)


def _run_alone(rider, name):
    ni, no = len(rider.ins), len(rider.outs)

    def body(*refs):
        cps = rider.copies(refs[:ni], refs[ni:ni + no], refs[ni + no], refs[ni + no + 1], 0)
        for cp in cps:
            cp.start()
        for cp in cps:
            cp.wait()

    return list(pl.pallas_call(
        body, name=name, in_specs=[ANY] * ni, out_specs=[ANY] * no, out_shape=rider.outs,
        input_output_aliases=dict(rider.aliases),
        scratch_shapes=[pltpu.SemaphoreType.DMA((rider.n,)), pltpu.SemaphoreType.DMA((rider.n,))],
        compiler_params=COMM,
    )(*rider.ins))


def _join_halves(fs):
    n = len(fs)

    def body(*refs):
        f_refs, o_refs, send_sems, recv_sems = refs[:n], refs[n:2 * n], refs[2 * n], refs[2 * n + 1]
        x, y, c, _ = _place()
        cps = []
        for i in range(n):
            h = f_refs[i].shape[0] // 2
            rows = pl.ds(c * h, h)
            cps.append(pltpu.make_async_remote_copy(
                src_ref=f_refs[i].at[rows, :], dst_ref=o_refs[i].at[rows, :], send_sem=send_sems.at[i],
                recv_sem=recv_sems.at[i], device_id=(x, y, 1 - c), device_id_type=MESH))
        for cp in cps:
            cp.start()
        for i in range(n):
            h = f_refs[i].shape[0] // 2
            theirs = o_refs[i].at[pl.ds((1 - c) * h, h), :]
            pltpu.make_async_remote_copy(
                src_ref=theirs, dst_ref=theirs, send_sem=send_sems.at[i], recv_sem=recv_sems.at[i],
                device_id=(x, y, 1 - c), device_id_type=MESH).wait_recv()
        for cp in cps:
            cp.wait_send()

    return pl.pallas_call(
        body, name="rs_join_halves", in_specs=[ANY] * n, out_specs=[ANY] * n,
        out_shape=[jax.ShapeDtypeStruct(f.shape, F32) for f in fs],
        input_output_aliases={i: i for i in range(n)},
        scratch_shapes=[pltpu.SemaphoreType.DMA((n,)), pltpu.SemaphoreType.DMA((n,))],
        compiler_params=COMM,
    )(*fs)


def _allreduce_small(v):
    r, W = v.shape

    def body(v_ref, o_ref, land_ref, send_sems, recv_sems):
        x, y, c, _ = _place()
        me = 4 * x + 2 * y + c
        cps = []
        for m in range(1, 8):
            to = (x ^ (m >> 2), y ^ ((m >> 1) & 1), c ^ (m & 1))
            cps.append(pltpu.make_async_remote_copy(
                src_ref=v_ref, dst_ref=land_ref.at[m - 1], send_sem=send_sems.at[m - 1], recv_sem=recv_sems.at[m - 1],
                device_id=to, device_id_type=MESH))
        for cp in cps:
            cp.start()
        for cp in cps:
            cp.wait()
        total = jnp.zeros((r, W), F32)
        for d in range(8):
            slot = jnp.maximum((me ^ d) - 1, 0)
            total = total + jnp.where(me == d, v_ref[...], land_ref[slot])
        o_ref[...] = total

    return pl.pallas_call(
        body, name="allreduce_small",
        in_specs=[pl.BlockSpec(memory_space=pltpu.VMEM)], out_specs=pl.BlockSpec(memory_space=pltpu.VMEM),
        out_shape=jax.ShapeDtypeStruct((r, W), F32),
        scratch_shapes=[pltpu.VMEM((7, r, W), F32), pltpu.SemaphoreType.DMA((7,)), pltpu.SemaphoreType.DMA((7,))],
        compiler_params=pltpu.CompilerParams(has_side_effects=True, vmem_limit_bytes=VMEM_LIMIT),
    )(v)


def kernel(x, ffn1_w_gate, ffn1_w_up, ffn1_w_down, ffn2_w_gate, ffn2_w_up, ffn2_w_down, w_in, pool_w, pool_scale, conv_w, rpb, w_out, ln_g, ln_b, loss_target, m_ffn1_w_gate, m_ffn1_w_up, m_ffn1_w_down, m_ffn2_w_gate, m_ffn2_w_up, m_ffn2_w_down, m_w_in, m_pool_w, m_pool_scale, m_conv_w, m_rpb, m_w_out, m_ln_g, m_ln_b, v_ffn1_w_gate, v_ffn1_w_up, v_ffn1_w_down, v_ffn2_w_gate, v_ffn2_w_up, v_ffn2_w_down, v_w_in, v_pool_w, v_pool_scale, v_conv_w, v_rpb, v_w_out, v_ln_g, v_ln_b):
    weights = dict(ffn1_w_gate=ffn1_w_gate, ffn1_w_up=ffn1_w_up, ffn1_w_down=ffn1_w_down, ffn2_w_gate=ffn2_w_gate,
                   ffn2_w_up=ffn2_w_up, ffn2_w_down=ffn2_w_down, w_in=w_in, pool_w=pool_w, pool_scale=pool_scale,
                   conv_w=conv_w, rpb=rpb, w_out=w_out, ln_g=ln_g, ln_b=ln_b)
    mom_m = dict(ffn1_w_gate=m_ffn1_w_gate, ffn1_w_up=m_ffn1_w_up, ffn1_w_down=m_ffn1_w_down, ffn2_w_gate=m_ffn2_w_gate,
                 ffn2_w_up=m_ffn2_w_up, ffn2_w_down=m_ffn2_w_down, w_in=m_w_in, pool_w=m_pool_w,
                 pool_scale=m_pool_scale, conv_w=m_conv_w, rpb=m_rpb, w_out=m_w_out, ln_g=m_ln_g, ln_b=m_ln_b)
    mom_v = dict(ffn1_w_gate=v_ffn1_w_gate, ffn1_w_up=v_ffn1_w_up, ffn1_w_down=v_ffn1_w_down, ffn2_w_gate=v_ffn2_w_gate,
                 ffn2_w_up=v_ffn2_w_up, ffn2_w_down=v_ffn2_w_down, w_in=v_w_in, pool_w=v_pool_w,
                 pool_scale=v_pool_scale, conv_w=v_conv_w, rpb=v_rpb, w_out=v_w_out, ln_g=v_ln_g, ln_b=v_ln_b)
    order = list(weights)
    L = ffn1_w_gate.shape[0]
    xi, yi, ci = lax.axis_index("x"), lax.axis_index("y"), lax.axis_index("c")
    q_me = 2 * xi + yi
    x2 = x[0]
    target = loss_target[0]
    D = x2.shape[1]
    n_in = w_in.shape[-1]

    small = jnp.concatenate([ln_g.reshape(-1), ln_b.reshape(-1), conv_w.reshape(-1)])
    n_small = small.shape[0]
    small_rows = -(-n_small // (8 * 128)) * 8
    small = jnp.pad(small, (0, small_rows * 128 - n_small)).reshape(small_rows, 128)
    small_all = _gather_small(small).reshape(NQ, small_rows * 128)[:, :n_small]
    dq4 = D // NQ
    n_ln = L * 3 * dq4
    ln_g_all = small_all[:, :n_ln].reshape(NQ, L, 3, dq4).transpose(1, 2, 0, 3).reshape(L, 3, D)
    ln_b_all = small_all[:, n_ln:2 * n_ln].reshape(NQ, L, 3, dq4).transpose(1, 2, 0, 3).reshape(L, 3, D)
    conv_all = small_all[:, 2 * n_ln:].reshape(NQ, L, 3, D_CONV // NQ).transpose(1, 2, 0, 3).reshape(L, 3, D_CONV)

    def layer_shards(l):
        return [jnp.stack([ffn1_w_gate[l], ffn1_w_up[l], ffn2_w_gate[l], ffn2_w_up[l]]).astype(BF),
                jnp.stack([ffn1_w_down[l], ffn2_w_down[l]]).astype(BF),
                w_in[l].astype(BF),
                w_out[l].astype(BF)]

    def own_quarter(bufs, shards):
        return [lax.dynamic_update_slice(b, s[None], (q_me,) + (0,) * s.ndim) for b, s in zip(bufs, shards)]

    first = layer_shards(0)
    landed = _run_alone(_gather_ici(first), "gather_ici")
    weights_of = [own_quarter(_run_alone(_gather_d2d(landed), "gather_d2d"), first)] + [None] * (L - 1)

    ng = len(POOL_WINDOWS)
    pg = D_POOL // ng
    saved = []
    h = x2
    hb = x2.astype(BF)
    for l in range(L):
        wa, wb, wc, wo = weights_of[l]
        nxt = layer_shards(l + 1) if l + 1 < L else None
        eye = jnp.eye(ng, dtype=F32)
        wblk = (pool_w[l][:, :, None, :] * eye[:, None, :, None]).reshape(D_POOL, D_POOL).astype(BF)
        vec = jnp.concatenate([pool_scale[l][None], conv_all[l], jnp.zeros((4, D_POOL), F32)], axis=0)
        bias = _bias_table(rpb[l])
        lg = [ln_g_all[l, j][None] for j in range(3)]
        lb = [ln_b_all[l, j][None] for j in range(3)]
        (x1, x1b, z1, g1, u1), got_a = _ffn_fwd(h, wa, wb, lg[0], lb[0], 0, 1, 0,
                                                rider=_gather_ici(nxt[:1]) if nxt else None)
        pabc, qkv = _proj(x1b, wc)
        yab = _mixab_fwd(pabc, wblk, vec)
        (yc,), got = _attn_fwd(qkv, bias, rider=_merge(_gather_d2d(got_a), _gather_ici(nxt[1:])) if nxt else None)
        xm, xmb, zm = _mixout_fwd(yab, yc, x1, wo, lg[1], lb[1])
        (x3, x3b, z3, g3, u3), got_rest = _ffn_fwd(xm, wa, wb, lg[2], lb[2], 2, 3, 1,
                                                   rider=_gather_d2d(got[1:]) if nxt else None)
        if nxt:
            weights_of[l + 1] = own_quarter(got[:1] + got_rest, nxt)
        saved.append(dict(wblk=wblk, vec=vec, bias=bias, lg=lg, hb=hb, z1=z1, g1=g1, u1=u1, x1b=x1b, pabc=pabc,
                          qkv=qkv, yab=yab, yc=yc, zm=zm, xmb=xmb, z3=z3, g3=g3, u3=u3))
        h, hb = x3, x3b

    loss_tile, dh = _loss_head(h, target)
    loss = lax.psum(loss_tile[0, 0], ("x", "y", "c"))

    def add_chip(arrs, recv):
        chip = [_add_chip(g, r) for g, r in zip(arrs, recv)]
        return [cf for cf, _ in chip], [cb for _, cb in chip]

    def rs_end(chip_f, from_chips):
        return _join_halves([_add_final(cf, r) for cf, r in zip(chip_f, from_chips)])

    per_layer = [[None] * 6 for _ in range(L)]
    g_small = dict(pool_w=[None] * L, pool_scale=[None] * L, conv_w=[None] * L, rpb=[None] * L, ln_g=[None] * L,
                   ln_b=[None] * L)
    ffn1_g = None
    mix = None
    for l in reversed(range(L)):
        sv = saved[l]
        wa, wb, wc, wo = weights_of[l]
        riders = ([_swap_halves(ffn1_g)] if ffn1_g else []) + ([_scatter_chips(mix[1])] if mix else [])
        (dxm, df, dg, du, a, ln3), got = _ffn_bwd(dh, sv["z3"], sv["g3"], sv["u3"], wa, wb, sv["lg"][2], 2, 3, 1,
                                                  rider=_merge(*riders) if riders else None)
        if riders:
            ffn1_f, ffn1_b = add_chip(ffn1_g, got[:2])
            per_layer[l + 1][4:6] = rs_end(mix[0], got[2:])
        ffn2_g = [_wgrad_gate_up(sv["xmb"], dg, du), _wgrad_down(a, df)]
        (dres, dzb, dycat, ln2), got = _mixout_bwd(dxm, sv["zm"], wo, sv["lg"][1], rider=_swap_halves(ffn2_g))
        ffn2_f, ffn2_b = add_chip(ffn2_g, got)
        g_o = _wgrad_out(sv["yab"], sv["yc"], dzb)
        dpabc, dwblk, dvec = _mixab_bwd(sv["pabc"], dycat, sv["wblk"], sv["vec"])
        (dq, dk, dv, dbias), got = _attn_bwd(sv["qkv"], sv["bias"], dycat,
                                             rider=_scatter_chips(ffn1_b) if riders else None)
        if riders:
            per_layer[l + 1][0:2] = rs_end(ffn1_f, got)
        dparts = [dpabc, dq, dk, dv]
        mix_g = [_wgrad_in(sv["x1b"], dparts, n_in), g_o]
        (dx1,), got = _proj_bwd(dres, dparts, wc, rider=_swap_halves(mix_g))
        mix = add_chip(mix_g, got)
        (dh, df, dg, du, a, ln1), got = _ffn_bwd(dx1, sv["z1"], sv["g1"], sv["u1"], wa, wb, sv["lg"][0], 0, 1, 0,
                                                 rider=_scatter_chips(ffn2_b))
        per_layer[l][2:4] = rs_end(ffn2_f, got)
        ffn1_g = [_wgrad_gate_up(sv["hb"], dg, du), _wgrad_down(a, df)]
        g_small["pool_w"][l] = jnp.stack([dwblk[gi * pg:(gi + 1) * pg, gi * pg:(gi + 1) * pg] for gi in range(ng)])
        g_small["pool_scale"][l] = dvec[0]
        g_small["conv_w"][l] = dvec[1:4]
        g_small["rpb"][l] = _bias_grad(dbias)
        g_small["ln_g"][l] = jnp.stack([ln1[0], ln2[0], ln3[0]])
        g_small["ln_b"][l] = jnp.stack([ln1[1], ln2[1], ln3[1]])
    ffn1_f, ffn1_b = add_chip(ffn1_g, _run_alone(_swap_halves(ffn1_g), "rs_swap_halves"))
    got = _run_alone(_merge(_scatter_chips(ffn1_b), _scatter_chips(mix[1])), "rs_scatter_chips")
    per_layer[0][0:2] = rs_end(ffn1_f, got[:2])
    per_layer[0][4:6] = rs_end(mix[0], got[2:])
    grad_x = dh[None]

    def stacked(i, rows=None):
        parts = [per_layer[l][i] if rows is None else per_layer[l][i][rows[0]:rows[1]] for l in range(L)]
        return jnp.stack(parts)

    grads = dict(ffn1_w_gate=stacked(0, (0, D)), ffn1_w_up=stacked(0, (D, 2 * D)), ffn1_w_down=stacked(1),
                 ffn2_w_gate=stacked(2, (0, D)), ffn2_w_up=stacked(2, (D, 2 * D)), ffn2_w_down=stacked(3),
                 w_in=stacked(4), w_out=stacked(5))

    small_names = ("pool_w", "pool_scale", "conv_w", "rpb", "ln_g", "ln_b")
    small_full = {n: jnp.stack(g_small[n]) for n in small_names}
    vflat = jnp.concatenate([small_full[n].reshape(-1) for n in small_names])
    n_v = vflat.shape[0]
    v_cols = 1024
    v_rows = -(-n_v // (8 * v_cols)) * 8
    vsum = _allreduce_small(jnp.pad(vflat, (0, v_rows * v_cols - n_v)).reshape(v_rows, v_cols)).reshape(-1)
    off = 0
    for n in small_names:
        sz = int(np.prod(small_full[n].shape))
        grads[n] = vsum[off:off + sz].reshape(small_full[n].shape)
        off += sz
    for n in ("conv_w", "ln_g", "ln_b"):
        width = weights[n].shape[-1]
        grads[n] = lax.dynamic_slice_in_dim(grads[n], q_me * width, width, axis=2)

    delta, new_m, new_v = {}, {}, {}
    for n in order:
        delta[n], new_m[n], new_v[n] = _adamw(weights[n], grads[n], mom_m[n], mom_v[n])
    return (loss, grad_x, *[grads[n] for n in order], *[delta[n] for n in order], *[new_m[n] for n in order],
            *[new_v[n] for n in order])
```

```python
import numpy as np
import jax
import jax.numpy as jnp
from jax import lax
from jax.experimental import pallas as pl
from jax.experimental.pallas import tpu as pltpu

BF = jnp.bfloat16
F32 = jnp.float32
MESH = pl.DeviceIdType.MESH

DEPTH = 4
ALPHA = (2.0 * DEPTH) ** 0.25
LN_EPS = 1e-5
NEG_INF = -1e30
GRID_W = 64
NA_ROWS = 8
NA_COLS = 16
NA_HEADS = 8
HEAD_DIM = 64
D_POOL = 256
D_CONV = 256
D_NA = 512
HG = 4
LW = HG * HEAD_DIM
POOL_WINDOWS = (2, 4, 8, 16)
HALO = 8
ADAM_LR, ADAM_B1, ADAM_B2, ADAM_EPS, ADAM_WD, ADAM_STEP = 0.001, 0.9, 0.999, 1e-08, 0.01, 10
VMEM_LIMIT = 56 * 1024 * 1024
NQ = 4
WGRAD_TOKENS = 2048


def _cp(n_axes):
    return pltpu.CompilerParams(dimension_semantics=("arbitrary",) * n_axes, vmem_limit_bytes=VMEM_LIMIT)


def _full(shape):
    nd = len(shape)
    return pl.BlockSpec(shape, lambda *_: (0,) * nd)


def _quarters(arr):
    return pl.BlockSpec(arr.shape, lambda *_: (0, 0, 0), pipeline_mode=pl.Buffered(1))


ANY = pl.BlockSpec(memory_space=pl.ANY)


class _Rider:
    def __init__(self, tag, ins, outs, aliases, n, copies):
        self.tag, self.ins, self.outs, self.aliases, self.n, self.copies = tag, list(ins), list(outs), aliases, n, copies


def _merge(*riders):
    ins, outs, aliases, spans, n = [], [], {}, [], 0
    for r in riders:
        spans.append((len(ins), len(outs), n))
        aliases.update({len(ins) + i: len(outs) + j for i, j in r.aliases.items()})
        ins += r.ins
        outs += r.outs
        n += r.n

    def copies(r_in, r_out, ssem, rsem, base):
        cps = []
        for r, (i0, o0, s0) in zip(riders, spans):
            cps += r.copies(r_in[i0:i0 + len(r.ins)], r_out[o0:o0 + len(r.outs)], ssem, rsem, base + s0)
        return cps

    return _Rider("_".join(r.tag for r in riders), ins, outs, aliases, n, copies)


def _pcall(body, operands, *, name, grid, in_specs, out_specs, out_shape, scratch=(), rider=None, edges=None):
    n_in, n_out = len(in_specs), len(out_specs)
    params = dict(dimension_semantics=("arbitrary",) * len(grid), vmem_limit_bytes=VMEM_LIMIT)
    if rider is None:
        outs = pl.pallas_call(body, name=name, grid=grid, in_specs=in_specs, out_specs=out_specs, out_shape=out_shape,
                              scratch_shapes=list(scratch), compiler_params=pltpu.CompilerParams(**params))(*operands)
        return list(outs), []
    ni, no = len(rider.ins), len(rider.outs)
    first, last = edges

    def riding(*refs):
        rest = refs[n_in + ni + n_out + no:]
        cps = rider.copies(refs[n_in:n_in + ni], refs[n_in + ni + n_out:n_in + ni + n_out + no], rest[-2], rest[-1], 0)

        @pl.when(first())
        def _():
            for cp in cps:
                cp.start()

        body(*refs[:n_in], *refs[n_in + ni:n_in + ni + n_out], *rest[:-2])

        @pl.when(last())
        def _():
            for cp in cps:
                cp.wait()

    outs = pl.pallas_call(
        riding, name=f"{name}_{rider.tag}", grid=grid, in_specs=list(in_specs) + [ANY] * ni,
        out_specs=list(out_specs) + [ANY] * no, out_shape=list(out_shape) + rider.outs,
        scratch_shapes=list(scratch) + [pltpu.SemaphoreType.DMA((rider.n,)), pltpu.SemaphoreType.DMA((rider.n,))],
        input_output_aliases={n_in + i: n_out + j for i, j in rider.aliases.items()},
        compiler_params=pltpu.CompilerParams(has_side_effects=True, **params),
    )(*operands, *rider.ins)
    return list(outs[:n_out]), list(outs[n_out:])


def _edges_1d(n):
    return (lambda: pl.program_id(0) == 0), (lambda: pl.program_id(0) == n - 1)


def _edges_2d(n0, n1):
    return ((lambda: (pl.program_id(0) == 0) & (pl.program_id(1) == 0)),
            (lambda: (pl.program_id(0) == n0 - 1) & (pl.program_id(1) == n1 - 1)))


def _nt(a, b):
    return lax.dot_general(a, b, (((1,), (1,)), ((), ())), preferred_element_type=F32)


def _tn(a, b):
    return lax.dot_general(a, b, (((0,), (0,)), ((), ())), preferred_element_type=F32)


def _nn(a, b):
    return jnp.dot(a, b, preferred_element_type=F32)


def _ln_fwd(z, g, b):
    mu = jnp.mean(z, axis=-1, keepdims=True)
    zc = z - mu
    var = jnp.mean(zc * zc, axis=-1, keepdims=True)
    return zc * lax.rsqrt(var + LN_EPS) * g + b


def _ln_bwd(dy, z, g):
    mu = jnp.mean(z, axis=-1, keepdims=True)
    zc = z - mu
    var = jnp.mean(zc * zc, axis=-1, keepdims=True)
    rstd = lax.rsqrt(var + LN_EPS)
    xhat = zc * rstd
    gdy = dy * g
    m1 = jnp.mean(gdy, axis=-1, keepdims=True)
    m2 = jnp.mean(gdy * xhat, axis=-1, keepdims=True)
    return rstd * (gdy - m1 - xhat * m2), xhat


def _acc_ln_grads(acc_ref, first, dy, xhat):
    @pl.when(first)
    def _():
        acc_ref[...] = jnp.zeros_like(acc_ref)
    acc_ref[0:1, :] += jnp.sum(dy * xhat, axis=0, keepdims=True)
    acc_ref[1:2, :] += jnp.sum(dy, axis=0, keepdims=True)


def _ffn_fwd(x, wg, wu, wd, lg, lb, rider=None):
    S, D = x.shape
    fq = wg.shape[-1]
    tm = min(256, S)

    def body(x_ref, wg_ref, wu_ref, wd_ref, lg_ref, lb_ref, xo_ref, xb_ref, z_ref, g_ref, u_ref):
        x = x_ref[...]
        xb = x.astype(BF)
        acc = jnp.zeros((tm, D), F32)
        for q in range(NQ):
            g = _nn(xb, wg_ref[q])
            u = _nn(xb, wu_ref[q])
            g_ref[q] = g.astype(BF)
            u_ref[q] = u.astype(BF)
            a = g * jax.nn.sigmoid(g) * u
            acc = acc + _nn(a.astype(BF), wd_ref[q])
        z = ALPHA * x + 0.5 * acc
        xo = _ln_fwd(z, lg_ref[...], lb_ref[...])
        z_ref[...] = z
        xo_ref[...] = xo
        xb_ref[...] = xo.astype(BF)

    row = pl.BlockSpec((tm, D), lambda i: (i, 0))
    qrow = pl.BlockSpec((NQ, tm, fq), lambda i: (0, i, 0))
    return _pcall(
        body, [x, wg, wu, wd, lg, lb], name="ffn_fwd", grid=(S // tm,),
        in_specs=[row, _quarters(wg), _quarters(wu), _quarters(wd), _full((1, D)), _full((1, D))],
        out_specs=[row, row, row, qrow, qrow],
        out_shape=[jax.ShapeDtypeStruct((S, D), F32), jax.ShapeDtypeStruct((S, D), BF),
                   jax.ShapeDtypeStruct((S, D), F32), jax.ShapeDtypeStruct((NQ, S, fq), BF),
                   jax.ShapeDtypeStruct((NQ, S, fq), BF)],
        rider=rider, edges=_edges_1d(S // tm))


def _ffn_bwd(dxo, z, g, u, wg, wu, wd, lg, rider=None):
    S, D = dxo.shape
    fq = wg.shape[-1]
    tm = min(256, S)
    nt = S // tm

    def body(dxo0_ref, z0_ref, dxo1_ref, z1_ref, g_ref, u_ref, wg_ref, wu_ref, wd_ref, lg_ref,
             dx_ref, df_ref, dg_ref, du_ref, a_ref, ln_ref, dz_ref):
        i = pl.program_id(0)

        @pl.when(i == 0)
        def _():
            dy0 = dxo0_ref[...]
            dz0, xhat0 = _ln_bwd(dy0, z0_ref[...], lg_ref[...])
            dz_ref[...] = dz0
            ln_ref[...] = jnp.zeros_like(ln_ref)
            ln_ref[0:1, :] += jnp.sum(dy0 * xhat0, axis=0, keepdims=True)
            ln_ref[1:2, :] += jnp.sum(dy0, axis=0, keepdims=True)

        dz = dz_ref[...]
        dfb = (0.5 * dz).astype(BF)
        df_ref[...] = dfb
        acc = ALPHA * dz
        for q in range(NQ):
            da = _nt(dfb, wd_ref[q])
            gg = g_ref[q].astype(F32)
            uu = u_ref[q].astype(F32)
            sg = jax.nn.sigmoid(gg)
            silu = gg * sg
            a_ref[q] = (silu * uu).astype(BF)
            dgb = (da * uu * (sg * (1.0 + gg * (1.0 - sg)))).astype(BF)
            dub = (da * silu).astype(BF)
            dg_ref[q] = dgb
            du_ref[q] = dub
            acc = acc + _nt(dgb, wg_ref[q]) + _nt(dub, wu_ref[q])
        dx_ref[...] = acc
        dy1 = dxo1_ref[...]
        dz1, xhat1 = _ln_bwd(dy1, z1_ref[...], lg_ref[...])
        real = (i < nt - 1).astype(F32)
        ln_ref[0:1, :] += real * jnp.sum(dy1 * xhat1, axis=0, keepdims=True)
        ln_ref[1:2, :] += real * jnp.sum(dy1, axis=0, keepdims=True)
        dz_ref[...] = dz1

    row = pl.BlockSpec((tm, D), lambda i: (i, 0))
    first = pl.BlockSpec((tm, D), lambda i: (0, 0))
    nxt = pl.BlockSpec((tm, D), lambda i: (jnp.minimum(i + 1, nt - 1), 0))
    qrow = pl.BlockSpec((NQ, tm, fq), lambda i: (0, i, 0))
    qshape = jax.ShapeDtypeStruct((NQ, S, fq), BF)
    return _pcall(
        body, [dxo, z, dxo, z, g, u, wg, wu, wd, lg], name="ffn_bwd", grid=(nt,),
        in_specs=[first, first, nxt, nxt, qrow, qrow, _quarters(wg), _quarters(wu), _quarters(wd), _full((1, D))],
        out_specs=[row, row, qrow, qrow, qrow, _full((8, D))],
        out_shape=[jax.ShapeDtypeStruct((S, D), F32), jax.ShapeDtypeStruct((S, D), BF), qshape, qshape, qshape,
                   jax.ShapeDtypeStruct((8, D), F32)],
        scratch=[pltpu.VMEM((tm, D), F32)],
        rider=rider, edges=_edges_1d(nt))


def _wgrad_gate_up(a, dg, du):
    S, K = a.shape
    n = dg.shape[-1]
    ts = min(WGRAD_TOKENS, S)

    def body(a_ref, g_ref, u_ref, o_ref):
        @pl.when(pl.program_id(1) == 0)
        def _():
            o_ref[...] = jnp.zeros_like(o_ref)
        av = a_ref[...]
        o_ref[0:K, :] += _tn(av, g_ref[...])
        o_ref[K:2 * K, :] += _tn(av, u_ref[...])

    bspec = pl.BlockSpec((None, ts, n), lambda q, s: (q, s, 0))
    return pl.pallas_call(
        body, name="wgrad_gate_up", grid=(NQ, S // ts),
        in_specs=[pl.BlockSpec((ts, K), lambda q, s: (s, 0)), bspec, bspec],
        out_specs=pl.BlockSpec((None, 2 * K, n), lambda q, s: (q, 0, 0)),
        out_shape=jax.ShapeDtypeStruct((NQ, 2 * K, n), F32),
        compiler_params=_cp(2),
    )(a, dg, du)


def _wgrad_down(a, df):
    _, S, k = a.shape
    N = df.shape[1]
    ts = min(WGRAD_TOKENS, S)

    def body(a_ref, b_ref, o_ref):
        @pl.when(pl.program_id(1) == 0)
        def _():
            o_ref[...] = jnp.zeros_like(o_ref)
        o_ref[...] += _tn(a_ref[...], b_ref[...])

    return pl.pallas_call(
        body, name="wgrad_down", grid=(NQ, S // ts),
        in_specs=[pl.BlockSpec((None, ts, k), lambda q, s: (q, s, 0)), pl.BlockSpec((ts, N), lambda q, s: (s, 0))],
        out_specs=pl.BlockSpec((None, k, N), lambda q, s: (q, 0, 0)),
        out_shape=jax.ShapeDtypeStruct((NQ, k, N), F32),
        compiler_params=_cp(2),
    )(a, df)


def _wgrad_out(yab, yc, dzb):
    S, h = yab.shape
    D = dzb.shape[1]
    k = h // 2
    ts = min(WGRAD_TOKENS, S)

    def body(yab_ref, yc_ref, b_ref, o_ref):
        @pl.when(pl.program_id(0) == 0)
        def _():
            o_ref[...] = jnp.zeros_like(o_ref)
        b = b_ref[...]
        o_ref[0] += _tn(yab_ref[:, 0:k], b)
        o_ref[1] += _tn(yab_ref[:, k:h], b)
        o_ref[2] += _tn(yc_ref[:, 0:k], b)
        o_ref[3] += _tn(yc_ref[:, k:h], b)

    row = lambda w: pl.BlockSpec((ts, w), lambda s: (s, 0))
    return pl.pallas_call(
        body, name="wgrad_out", grid=(S // ts,),
        in_specs=[row(h), row(h), row(D)], out_specs=_full((NQ, k, D)),
        out_shape=jax.ShapeDtypeStruct((NQ, k, D), F32),
        compiler_params=_cp(1),
    )(yab, yc, dzb)


def _proj(xb, wc):
    S, D = xb.shape
    n = wc.shape[-1]
    n1 = D_POOL + 3 * D_CONV
    n2 = NQ * n - n1
    tm = min(512, S)

    def body(x_ref, w_ref, p_ref, qkv_ref):
        x = x_ref[...]
        for q in range(NQ):
            r = _nn(x, w_ref[q])
            lo, hi = q * n, (q + 1) * n
            if hi <= n1:
                p_ref[:, lo:hi] = r
            elif lo >= n1:
                qkv_ref[:, lo - n1:hi - n1] = r.astype(BF)
            else:
                p_ref[:, lo:n1] = r[:, 0:n1 - lo]
                qkv_ref[:, 0:hi - n1] = r[:, n1 - lo:n].astype(BF)

    row = lambda w: pl.BlockSpec((tm, w), lambda i: (i, 0))
    return pl.pallas_call(
        body, name="mix_proj", grid=(S // tm,),
        in_specs=[row(D), _quarters(wc)],
        out_specs=[row(n1), row(n2)],
        out_shape=[jax.ShapeDtypeStruct((S, n1), F32), jax.ShapeDtypeStruct((S, n2), BF)],
        compiler_params=_cp(1),
    )(xb, wc)


def _mm_exact(a, b, name):
    def body(a_ref, b_ref, o_ref):
        o_ref[...] = jnp.dot(a_ref[...], b_ref[...], preferred_element_type=F32, precision=lax.Precision.HIGHEST)

    return pl.pallas_call(
        body, name=name, in_specs=[_full(a.shape), _full(b.shape)], out_specs=_full((a.shape[0], b.shape[1])),
        out_shape=jax.ShapeDtypeStruct((a.shape[0], b.shape[1]), F32),
        compiler_params=pltpu.CompilerParams(vmem_limit_bytes=VMEM_LIMIT),
    )(a, b)


def _bias_constants():
    c = np.arange(GRID_W)
    col_start = np.clip(c - NA_COLS // 2, 0, GRID_W - NA_COLS)
    valid = (c[None, :] >= col_start[:, None]) & (c[None, :] < col_start[:, None] + NA_COLS)
    dc = np.clip(c[None, :] - c[:, None], -(NA_COLS - 1), NA_COLS - 1) + (NA_COLS - 1)
    onehot = np.zeros((32, GRID_W * GRID_W), np.float32)
    onehot[dc.reshape(-1), np.arange(GRID_W * GRID_W)] = 1.0
    mask_kq = np.where(valid.T, 0.0, NEG_INF).astype(np.float32)
    mask = np.tile(mask_kq, (2 * NA_ROWS - 1, HG))
    return onehot, mask


def _bias_table(rpb):
    onehot, mask = _bias_constants()
    nr = 2 * NA_ROWS - 1
    r2 = jnp.pad(rpb.reshape(NA_HEADS * nr, 2 * NA_COLS - 1), ((0, 0), (0, 1)))
    t = _mm_exact(r2, jnp.asarray(onehot), "bias_expand")
    t = t.reshape(NA_HEADS // HG, HG, nr, GRID_W, GRID_W).transpose(0, 2, 4, 1, 3)
    return t.reshape(NA_HEADS // HG, nr * GRID_W, LW) + jnp.asarray(mask)[None]


def _bias_grad(dt):
    onehot, _ = _bias_constants()
    nr = 2 * NA_ROWS - 1
    d = dt.reshape(NA_HEADS // HG, nr, GRID_W, HG, GRID_W).transpose(0, 3, 1, 4, 2).reshape(NA_HEADS * nr, -1)
    g = _mm_exact(d, jnp.asarray(onehot.T.copy()), "bias_reduce")
    return g[:, :2 * NA_COLS - 1].reshape(NA_HEADS, nr, 2 * NA_COLS - 1)


def _attn_rows(S):
    rows = S // GRID_W
    rb = min(16, rows)
    return rows, rb


def _head_masks():
    lane = lax.broadcasted_iota(jnp.int32, (GRID_W, LW), 1)
    return [(lane >= HEAD_DIM * h) & (lane < HEAD_DIM * (h + 1)) for h in range(HG)]


def _stack_heads(x, masks):
    zero = jnp.zeros_like(x)
    return jnp.concatenate([jnp.where(m, x, zero) for m in masks], axis=0)


def _unstack_heads(x2, masks):
    out = x2[0:GRID_W]
    for h in range(1, HG):
        out = jnp.where(masks[h], x2[h * GRID_W:(h + 1) * GRID_W], out)
    return out


def _attn_step(r, rows, q, k_ref, v_ref, b_ref, masks):
    rs = jnp.clip(r - NA_ROWS // 2, 0, rows - NA_ROWS)
    s0 = rs - r + (NA_ROWS - 1)
    q2 = _stack_heads(q, masks)
    ks = pl.ds(pl.multiple_of(rs * GRID_W, GRID_W), NA_ROWS * GRID_W)
    kb = k_ref[ks, :]
    vb = v_ref[ks, :]
    bs = pl.ds(pl.multiple_of(s0 * GRID_W, GRID_W), NA_ROWS * GRID_W)
    s = _nt(kb, q2) * (HEAD_DIM ** -0.5) + b_ref[0, bs, :]
    m = jnp.max(s, axis=0, keepdims=True)
    p = jnp.exp(s - m)
    p = p / jnp.sum(p, axis=0, keepdims=True)
    return p, q2, kb, vb, ks, bs


def _attn_fwd(qkv, bias, rider=None):
    S = qkv.shape[0]
    rows, rb = _attn_rows(S)
    tq = rb * GRID_W
    ngr = NA_HEADS // HG

    def body(q_ref, k_ref, v_ref, b_ref, o_ref):
        base = pl.program_id(1) * rb
        masks = _head_masks()

        def step(i, carry):
            qs = pl.ds(pl.multiple_of(i * GRID_W, GRID_W), GRID_W)
            p, _, _, vb, _, _ = _attn_step(base + i, rows, q_ref[qs, :], k_ref, v_ref, b_ref, masks)
            o_ref[qs, :] = _unstack_heads(_tn(p.astype(BF), vb), masks).astype(BF)
            return carry

        lax.fori_loop(0, rb, step, 0, unroll=2)

    return _pcall(
        body, [qkv, qkv, qkv, bias], name="attn_fwd", grid=(ngr, rows // rb),
        in_specs=[pl.BlockSpec((tq, LW), lambda h, r: (r, h)),
                  pl.BlockSpec((S, LW), lambda h, r: (0, ngr + h)),
                  pl.BlockSpec((S, LW), lambda h, r: (0, 2 * ngr + h)),
                  pl.BlockSpec((1, bias.shape[1], LW), lambda h, r: (h, 0, 0))],
        out_specs=[pl.BlockSpec((tq, LW), lambda h, r: (r, h))],
        out_shape=[jax.ShapeDtypeStruct((S, D_NA), BF)],
        rider=rider, edges=_edges_2d(ngr, rows // rb))


def _attn_bwd(qkv, bias, dycat, rider=None):
    S = qkv.shape[0]
    rows, rb = _attn_rows(S)
    tq = rb * GRID_W
    ngr = NA_HEADS // HG
    scale = HEAD_DIM ** -0.5

    def body(q_ref, k_ref, v_ref, b_ref, do_ref, dq_ref, dk_ref, dv_ref, db_ref, dka_ref, dva_ref):
        base = pl.program_id(1) * rb
        last = pl.program_id(1) == pl.num_programs(1) - 1
        masks = _head_masks()

        @pl.when(pl.program_id(1) == 0)
        def _():
            dka_ref[...] = jnp.zeros_like(dka_ref)
            dva_ref[...] = jnp.zeros_like(dva_ref)
            db_ref[...] = jnp.zeros_like(db_ref)

        def step(i, carry):
            qs = pl.ds(pl.multiple_of(i * GRID_W, GRID_W), GRID_W)
            p, q2, kb, vb, ks, bs = _attn_step(base + i, rows, q_ref[qs, :], k_ref, v_ref, b_ref, masks)
            do2 = _stack_heads(do_ref[qs, :].astype(BF), masks)
            dp = _nt(vb, do2)
            ds = p * (dp - jnp.sum(p * dp, axis=0, keepdims=True))
            db_ref[0, bs, :] += ds
            dsb = ds.astype(BF)
            dq_ref[qs, :] = _unstack_heads(_tn(dsb, kb) * scale, masks).astype(BF)
            dka_ref[ks, :] += _nn(dsb, q2) * scale
            dva_ref[ks, :] += _nn(p.astype(BF), do2)
            return carry

        lax.fori_loop(0, rb, step, 0, unroll=2)

        @pl.when(last)
        def _():
            dk_ref[...] = dka_ref[...].astype(BF)
            dv_ref[...] = dva_ref[...].astype(BF)

    nb = bias.shape[1]
    once = dict(pipeline_mode=pl.Buffered(1))
    nd = D_NA // LW
    return _pcall(
        body, [qkv, qkv, qkv, bias, dycat], name="attn_bwd", grid=(ngr, rows // rb),
        in_specs=[pl.BlockSpec((tq, LW), lambda h, r: (r, h)),
                  pl.BlockSpec((S, LW), lambda h, r: (0, ngr + h), **once),
                  pl.BlockSpec((S, LW), lambda h, r: (0, 2 * ngr + h), **once),
                  pl.BlockSpec((1, nb, LW), lambda h, r: (h, 0, 0)),
                  pl.BlockSpec((tq, LW), lambda h, r: (r, nd + h))],
        out_specs=[pl.BlockSpec((tq, LW), lambda h, r: (r, h)),
                   pl.BlockSpec((S, LW), lambda h, r: (0, h)),
                   pl.BlockSpec((S, LW), lambda h, r: (0, h)),
                   pl.BlockSpec((1, nb, LW), lambda h, r: (h, 0, 0))],
        out_shape=[jax.ShapeDtypeStruct((S, D_NA), BF)] * 3 + [jax.ShapeDtypeStruct((ngr, nb, LW), F32)],
        scratch=[pltpu.VMEM((S, LW), F32), pltpu.VMEM((S, LW), F32)],
        rider=rider, edges=_edges_2d(ngr, rows // rb))


def _halo_specs(tm, width, S):
    hb = tm // HALO
    last = S // HALO - 1
    return [pl.BlockSpec((tm, width), lambda i: (i, 0)),
            pl.BlockSpec((HALO, width), lambda i: (jnp.maximum(i * hb - 1, 0), 0)),
            pl.BlockSpec((HALO, width), lambda i: (jnp.minimum((i + 1) * hb, last), 0))]


def _with_halo(cur_ref, prev_ref, next_ref, i, nt):
    prev = jnp.where(i > 0, prev_ref[...], 0.0)
    nxt = jnp.where(i < nt - 1, next_ref[...], 0.0)
    return jnp.concatenate([prev, cur_ref[...], nxt], axis=0)


def _shift(a, k):
    n = a.shape[0]
    return pltpu.roll(a, k % n, 0)


def _pool_lanes(n):
    lane = lax.broadcasted_iota(jnp.int32, (n, D_POOL), 1)
    group = D_POOL // len(POOL_WINDOWS)
    return [lane < group * (j + 1) for j in range(len(POOL_WINDOWS) - 1)]


def _by_window(lanes, vals):
    return jnp.where(lanes[0], vals[0], jnp.where(lanes[1], vals[1], jnp.where(lanes[2], vals[2], vals[3])))


def _pool_count(lanes, t, S):
    back = _by_window(lanes, tuple(w // 2 for w in POOL_WINDOWS))
    lo = jnp.maximum(t - back, 0)
    hi = jnp.minimum(t + back, S)
    return jnp.maximum(hi - lo, 1).astype(F32)


def _pool_p(u, lanes, cnt):
    a = u + _shift(u, 1)
    b = _shift(a, 1) + _shift(a, -1)
    c = _shift(b, 2) + _shift(b, -2)
    d = _shift(c, 4) + _shift(c, -4)
    return _by_window(lanes, (a, b, c, d)) / cnt - u


def _mixab_fwd(pabc, wblk, vec):
    S = pabc.shape[0]
    tm = min(512, S)
    nt = S // tm
    n = tm + 2 * HALO
    tile = slice(HALO, HALO + tm)

    def body(cur_ref, prev_ref, next_ref, w_ref, vec_ref, o_ref):
        i = pl.program_id(0)
        ext = _with_halo(cur_ref, prev_ref, next_ref, i, nt)
        lanes = _pool_lanes(n)
        t = i * tm - HALO + lax.broadcasted_iota(jnp.int32, (n, D_POOL), 0)
        p = _pool_p(ext[:, 0:D_POOL], lanes, _pool_count(lanes, t, S))[tile]
        o_ref[:, 0:D_POOL] = (_nn(p.astype(BF), w_ref[...]) * vec_ref[0:1, :]).astype(BF)
        zc = ext[:, 512:768] * ext[:, 768:1024]
        conv = vec_ref[1:2, :] * _shift(zc, 1) + vec_ref[2:3, :] * zc + vec_ref[3:4, :] * _shift(zc, -1)
        o_ref[:, D_POOL:D_POOL + D_CONV] = (ext[tile, 256:512] * conv[tile]).astype(BF)

    return pl.pallas_call(
        body, name="mixab_fwd", grid=(nt,),
        in_specs=_halo_specs(tm, 1024, S) + [_full((D_POOL, D_POOL)), _full((8, D_POOL))],
        out_specs=pl.BlockSpec((tm, D_POOL + D_CONV), lambda i: (i, 0)),
        out_shape=jax.ShapeDtypeStruct((S, D_POOL + D_CONV), BF),
        compiler_params=_cp(1),
    )(pabc, pabc, pabc, wblk, vec)


def _mixab_bwd(pabc, dycat, wblk, vec):
    S = pabc.shape[0]
    tm = min(512, S)
    nt = S // tm
    n = tm + 2 * HALO
    tile = slice(HALO, HALO + tm)

    def body(cur_ref, prev_ref, next_ref, dcur_ref, dprev_ref, dnext_ref, w_ref, vec_ref, o_ref, dw_ref, dvec_ref):
        i = pl.program_id(0)

        @pl.when(i == 0)
        def _():
            dw_ref[...] = jnp.zeros_like(dw_ref)
            dvec_ref[...] = jnp.zeros_like(dvec_ref)

        ext = _with_halo(cur_ref, prev_ref, next_ref, i, nt)
        dext = _with_halo(dcur_ref, dprev_ref, dnext_ref, i, nt)
        lanes = _pool_lanes(n)
        t = i * tm - HALO + lax.broadcasted_iota(jnp.int32, (n, D_POOL), 0)
        cnt = _pool_count(lanes, t, S)
        w = w_ref[...]
        scale = vec_ref[0:1, :]
        pb = _pool_p(ext[:, 0:D_POOL], lanes, cnt)[tile].astype(BF)
        dya = dext[:, 0:D_POOL]
        dvec_ref[0:1, :] += jnp.sum(dya[tile] * _nn(pb, w), axis=0, keepdims=True)
        dqb = (dya * scale).astype(BF)
        dw_ref[...] += _tn(pb, dqb[tile])
        dp = _nt(dqb, w)
        r = dp / cnt
        a = r + _shift(r, -1)
        b = _shift(a, 1) + _shift(a, -1)
        c = _shift(b, 2) + _shift(b, -2)
        d = _shift(c, 4) + _shift(c, -4)
        o_ref[:, 0:256] = (_by_window(lanes, (a, b, c, d)) - dp)[tile].astype(BF)
        gb, gc, hh = ext[:, 256:512], ext[:, 512:768], ext[:, 768:1024]
        zc = gc * hh
        zm, zp = _shift(zc, 1), _shift(zc, -1)
        w0, w1, w2 = vec_ref[1:2, :], vec_ref[2:3, :], vec_ref[3:4, :]
        dyb = dext[:, D_POOL:D_POOL + D_CONV]
        dconv = dyb * gb
        o_ref[:, 256:512] = (dyb * (w0 * zm + w1 * zc + w2 * zp))[tile].astype(BF)
        dzc = w0 * _shift(dconv, -1) + w1 * dconv + w2 * _shift(dconv, 1)
        o_ref[:, 512:768] = (dzc * hh)[tile].astype(BF)
        o_ref[:, 768:1024] = (dzc * gc)[tile].astype(BF)
        dct = dconv[tile]
        dvec_ref[1:2, :] += jnp.sum(dct * zm[tile], axis=0, keepdims=True)
        dvec_ref[2:3, :] += jnp.sum(dct * zc[tile], axis=0, keepdims=True)
        dvec_ref[3:4, :] += jnp.sum(dct * zp[tile], axis=0, keepdims=True)

    return pl.pallas_call(
        body, name="mixab_bwd", grid=(nt,),
        in_specs=_halo_specs(tm, 1024, S) + _halo_specs(tm, 512, S) + [_full((D_POOL, D_POOL)), _full((8, D_POOL))],
        out_specs=[pl.BlockSpec((tm, 1024), lambda i: (i, 0)), _full((D_POOL, D_POOL)), _full((8, D_POOL))],
        out_shape=[jax.ShapeDtypeStruct((S, 1024), BF), jax.ShapeDtypeStruct((D_POOL, D_POOL), F32),
                   jax.ShapeDtypeStruct((8, D_POOL), F32)],
        compiler_params=_cp(1),
    )(pabc, pabc, pabc, dycat, dycat, dycat, wblk, vec)


def _mixout_fwd(yab, yc, x, wo, lg, lb):
    S, D = x.shape
    tm = min(512, S)
    h = yab.shape[1]
    k = h // 2

    def body(yab_ref, yc_ref, x_ref, w_ref, lg_ref, lb_ref, xo_ref, xb_ref, z_ref):
        y = (_nn(yab_ref[:, 0:k], w_ref[0]) + _nn(yab_ref[:, k:h], w_ref[1])
             + _nn(yc_ref[:, 0:k], w_ref[2]) + _nn(yc_ref[:, k:h], w_ref[3]))
        z = ALPHA * x_ref[...] + y
        xo = _ln_fwd(z, lg_ref[...], lb_ref[...])
        z_ref[...] = z
        xo_ref[...] = xo
        xb_ref[...] = xo.astype(BF)

    row = lambda w: pl.BlockSpec((tm, w), lambda i: (i, 0))
    return pl.pallas_call(
        body, name="mixout_fwd", grid=(S // tm,),
        in_specs=[row(h), row(h), row(D), _quarters(wo), _full((1, D)), _full((1, D))],
        out_specs=[row(D), row(D), row(D)],
        out_shape=[jax.ShapeDtypeStruct((S, D), F32), jax.ShapeDtypeStruct((S, D), BF),
                   jax.ShapeDtypeStruct((S, D), F32)],
        compiler_params=_cp(1),
    )(yab, yc, x, wo, lg, lb)


def _mixout_bwd(dxo, z, wo, lg, rider=None):
    S, D = dxo.shape
    k = wo.shape[-2]
    tm = min(512, S)

    def body(dxo_ref, z_ref, w_ref, lg_ref, dres_ref, dzb_ref, dy_ref, ln_ref):
        dy = dxo_ref[...]
        dz, xhat = _ln_bwd(dy, z_ref[...], lg_ref[...])
        _acc_ln_grads(ln_ref, pl.program_id(0) == 0, dy, xhat)
        dzb = dz.astype(BF)
        dres_ref[...] = ALPHA * dz
        dzb_ref[...] = dzb
        for q in range(NQ):
            dy_ref[:, q * k:(q + 1) * k] = _nt(dzb, w_ref[q])

    row = lambda w: pl.BlockSpec((tm, w), lambda i: (i, 0))
    return _pcall(
        body, [dxo, z, wo, lg], name="mixout_bwd", grid=(S // tm,),
        in_specs=[row(D), row(D), _quarters(wo), _full((1, D))],
        out_specs=[row(D), row(D), row(NQ * k), _full((8, D))],
        out_shape=[jax.ShapeDtypeStruct((S, D), F32), jax.ShapeDtypeStruct((S, D), BF),
                   jax.ShapeDtypeStruct((S, NQ * k), F32), jax.ShapeDtypeStruct((8, D), F32)],
        rider=rider, edges=_edges_1d(S // tm))


def _take_cols(refs, lo, hi):
    parts, off = [], 0
    for r in refs:
        w = r.shape[1]
        a, b = max(lo, off), min(hi, off + w)
        if a < b:
            parts.append(r[:, a - off:b - off])
        off += w
    return parts[0] if len(parts) == 1 else jnp.concatenate(parts, axis=1)


def _proj_bwd(dres, dparts, wc, rider=None):
    S, D = dres.shape
    n = wc.shape[-1]
    tm = min(512, S)
    np_ = len(dparts)

    def body(*refs):
        dres_ref, d_refs, w_ref, dx_ref = refs[0], refs[1:1 + np_], refs[1 + np_], refs[2 + np_]
        acc = dres_ref[...]
        for q in range(NQ):
            acc = acc + _nt(_take_cols(d_refs, q * n, (q + 1) * n), w_ref[q])
        dx_ref[...] = acc

    row = lambda w: pl.BlockSpec((tm, w), lambda i: (i, 0))
    return _pcall(
        body, [dres, *dparts, wc], name="mix_proj_bwd", grid=(S // tm,),
        in_specs=[row(D)] + [row(d.shape[1]) for d in dparts] + [_quarters(wc)],
        out_specs=[row(D)],
        out_shape=[jax.ShapeDtypeStruct((S, D), F32)],
        rider=rider, edges=_edges_1d(S // tm))


def _wgrad_in(a, dparts, n):
    S, K = a.shape
    ts = min(WGRAD_TOKENS // 2, S)
    np_ = len(dparts)

    def body(*refs):
        a_ref, d_refs, o_ref = refs[0], refs[1:1 + np_], refs[1 + np_]

        @pl.when(pl.program_id(0) == 0)
        def _():
            o_ref[...] = jnp.zeros_like(o_ref)
        av = a_ref[...]
        for q in range(NQ):
            o_ref[q] += _tn(av, _take_cols(d_refs, q * n, (q + 1) * n))

    row = lambda w: pl.BlockSpec((ts, w), lambda s: (s, 0))
    return pl.pallas_call(
        body, name="wgrad_in", grid=(S // ts,),
        in_specs=[row(K)] + [row(d.shape[1]) for d in dparts], out_specs=_full((NQ, K, n)),
        out_shape=jax.ShapeDtypeStruct((NQ, K, n), F32),
        compiler_params=_cp(1),
    )(a, *dparts)


def _loss_head(y, target):
    S, D = y.shape
    tm = min(512, S)

    def body(y_ref, t_ref, l_ref, dy_ref):
        @pl.when(pl.program_id(0) == 0)
        def _():
            l_ref[...] = jnp.zeros_like(l_ref)
        e = y_ref[...] - t_ref[...]
        dy_ref[...] = e * (1.0 / D)
        part = jnp.sum(jnp.sum(e * e, axis=1, keepdims=True) * (1.0 / D), axis=0, keepdims=True)
        l_ref[...] += 0.5 * part

    row = pl.BlockSpec((tm, D), lambda i: (i, 0))
    return pl.pallas_call(
        body, name="loss_head", grid=(S // tm,),
        in_specs=[row, row], out_specs=[_full((8, 128)), row],
        out_shape=[jax.ShapeDtypeStruct((8, 128), F32), jax.ShapeDtypeStruct((S, D), F32)],
        compiler_params=_cp(1),
    )(y, target)


def _adamw(w, g, m, v):
    shape = w.shape
    cols = shape[-1]
    rows = int(np.prod(shape[:-1]))
    w2, g2, m2, v2 = (a.reshape(rows, cols) for a in (w, g, m, v))
    tr = rows
    for cand in (512, 352, 256):
        if rows > cand and rows % cand == 0:
            tr = cand
            break

    def body(w_ref, g_ref, m_ref, v_ref, d_ref, mo_ref, vo_ref):
        g = g_ref[...]
        mn = ADAM_B1 * m_ref[...] + (1.0 - ADAM_B1) * g
        vn = ADAM_B2 * v_ref[...] + (1.0 - ADAM_B2) * (g * g)
        m_hat = mn / (1.0 - ADAM_B1 ** ADAM_STEP)
        v_hat = vn / (1.0 - ADAM_B2 ** ADAM_STEP)
        d_ref[...] = -ADAM_LR * (m_hat / (jnp.sqrt(v_hat) + ADAM_EPS) + ADAM_WD * w_ref[...])
        mo_ref[...] = mn
        vo_ref[...] = vn

    spec = pl.BlockSpec((tr, cols), lambda i: (i, 0))
    outs = pl.pallas_call(
        body, name=f"adamw_{rows}x{cols}", grid=(rows // tr,),
        in_specs=[spec] * 4, out_specs=[spec] * 3,
        out_shape=[jax.ShapeDtypeStruct((rows, cols), F32)] * 3,
        compiler_params=_cp(1),
    )(w2, g2, m2, v2)
    return tuple(o.reshape(shape) for o in outs)


def _half_tile(h):
    return h if h <= 512 else 512


def _add_chip(g, recv):
    _, R, C = g.shape
    h = R // 2
    tr = _half_tile(h)
    nb = h // tr

    def body(a_ref, b_ref, o_ref, ob_ref):
        s = a_ref[...] + b_ref[...]
        o_ref[...] = s
        ob_ref[...] = s.astype(BF)

    half = pl.BlockSpec((1, tr, C), lambda q, i: (q, i, 0))
    mine = pl.BlockSpec((1, tr, C), lambda q, i: (q, lax.axis_index("c") * nb + i, 0))
    return pl.pallas_call(
        body, name=f"rs_add_chip_{R}x{C}", grid=(NQ, nb), in_specs=[mine, half], out_specs=[half, half],
        out_shape=[jax.ShapeDtypeStruct((NQ, h, C), F32), jax.ShapeDtypeStruct((NQ, h, C), BF)],
        compiler_params=_cp(2),
    )(g, recv)


def _add_final(chip, recv):
    _, h, C = chip.shape
    tr = _half_tile(h)
    nb = h // tr

    def body(a_ref, b_ref, o_ref):
        s = a_ref[0]
        for j in range(3):
            s = s + b_ref[j].astype(F32)
        o_ref[...] = s

    return pl.pallas_call(
        body, name=f"rs_add_final_{h}x{C}", grid=(nb,),
        in_specs=[pl.BlockSpec((1, tr, C), lambda i: (2 * lax.axis_index("x") + lax.axis_index("y"), i, 0)),
                  pl.BlockSpec((3, tr, C), lambda i: (0, i, 0))],
        out_specs=pl.BlockSpec((tr, C), lambda i: (lax.axis_index("c") * nb + i, 0)),
        out_shape=jax.ShapeDtypeStruct((2 * h, C), F32),
        compiler_params=_cp(1),
    )(chip, recv)


COMM = pltpu.CompilerParams(has_side_effects=True)


def _place():
    x, y, c = lax.axis_index("x"), lax.axis_index("y"), lax.axis_index("c")
    chips = [(1 - x, y), (x, 1 - y), (1 - x, 1 - y)]
    return x, y, c, chips


def _half0(ref, c):
    n = ref.shape[0] // 2
    return ref.at[pl.ds(c * n, n)]


def _gather_ici(shards):
    n = len(shards)

    def copies(r_in, r_out, ssem, rsem, base):
        x, y, c, chips = _place()
        q = 2 * x + y
        return [pltpu.make_async_remote_copy(
            src_ref=_half0(r_in[i], c), dst_ref=_half0(r_out[i].at[q], c), send_sem=ssem.at[base + 3 * i + j],
            recv_sem=rsem.at[base + 3 * i + j], device_id=(*chip, c), device_id_type=MESH)
            for i in range(n) for j, chip in enumerate(chips)]

    return _Rider("ici", shards, [jax.ShapeDtypeStruct((NQ,) + s.shape, BF) for s in shards], {}, 3 * n, copies)


def _gather_d2d(bufs):
    n = len(bufs)

    def copies(r_in, r_out, ssem, rsem, base):
        x, y, c, chips = _place()
        return [pltpu.make_async_remote_copy(
            src_ref=_half0(r_in[i].at[2 * cx + cy], c), dst_ref=_half0(r_out[i].at[2 * cx + cy], c),
            send_sem=ssem.at[base + 3 * i + j], recv_sem=rsem.at[base + 3 * i + j], device_id=(x, y, 1 - c),
            device_id_type=MESH) for i in range(n) for j, (cx, cy) in enumerate(chips)]

    return _Rider("d2d", bufs, [jax.ShapeDtypeStruct(b.shape, b.dtype) for b in bufs], {i: i for i in range(n)},
                  3 * n, copies)


def _gather_small(small):
    sr = small.shape[0]

    def body(s_ref, o_ref, send_sems, recv_sems):
        x, y, c, chips = _place()
        o_ref[2 * x + y] = s_ref[...]
        cps = [pltpu.make_async_remote_copy(
            src_ref=s_ref, dst_ref=o_ref.at[2 * x + y], send_sem=send_sems.at[j], recv_sem=recv_sems.at[j],
            device_id=(*chip, c), device_id_type=MESH) for j, chip in enumerate(chips)]
        for cp in cps:
            cp.start()
        for j, (cx, cy) in enumerate(chips):
            pltpu.make_async_remote_copy(
                src_ref=s_ref, dst_ref=o_ref.at[2 * cx + cy], send_sem=send_sems.at[j], recv_sem=recv_sems.at[j],
                device_id=(cx, cy, c), device_id_type=MESH).wait_recv()
        for cp in cps:
            cp.wait_send()

    vm = pl.BlockSpec(memory_space=pltpu.VMEM)
    return pl.pallas_call(
        body, name="gather_small", in_specs=[vm], out_specs=vm,
        out_shape=jax.ShapeDtypeStruct((NQ, sr, 128), F32),
        scratch_shapes=[pltpu.SemaphoreType.DMA((3,)), pltpu.SemaphoreType.DMA((3,))],
        compiler_params=COMM,
    )(small)


def _swap_halves(gs):
    n = len(gs)

    def copies(r_in, r_out, ssem, rsem, base):
        x, y, c, _ = _place()
        cps = []
        for i in range(n):
            h = r_in[i].shape[1] // 2
            cps.append(pltpu.make_async_remote_copy(
                src_ref=r_in[i].at[:, pl.ds((1 - c) * h, h), :], dst_ref=r_out[i], send_sem=ssem.at[base + i],
                recv_sem=rsem.at[base + i], device_id=(x, y, 1 - c), device_id_type=MESH))
        return cps

    return _Rider("swap", gs, [jax.ShapeDtypeStruct((NQ, g.shape[1] // 2, g.shape[2]), F32) for g in gs], {}, n,
                  copies)


def _scatter_chips(chips_b):
    n = len(chips_b)

    def copies(r_in, r_out, ssem, rsem, base):
        x, y, c, chips = _place()
        return [pltpu.make_async_remote_copy(
            src_ref=r_in[i].at[2 * cx + cy], dst_ref=r_out[i].at[j], send_sem=ssem.at[base + 3 * i + j],
            recv_sem=rsem.at[base + 3 * i + j], device_id=(cx, cy, c), device_id_type=MESH)
            for i in range(n) for j, (cx, cy) in enumerate(chips)]

    return _Rider("scatter", chips_b, [jax.ShapeDtypeStruct((3,) + s.shape[1:], BF) for s in chips_b], {}, 3 * n,
                  copies)


def _run_alone(rider, name):
    ni, no = len(rider.ins), len(rider.outs)

    def body(*refs):
        cps = rider.copies(refs[:ni], refs[ni:ni + no], refs[ni + no], refs[ni + no + 1], 0)
        for cp in cps:
            cp.start()
        for cp in cps:
            cp.wait()

    return list(pl.pallas_call(
        body, name=name, in_specs=[ANY] * ni, out_specs=[ANY] * no, out_shape=rider.outs,
        input_output_aliases=dict(rider.aliases),
        scratch_shapes=[pltpu.SemaphoreType.DMA((rider.n,)), pltpu.SemaphoreType.DMA((rider.n,))],
        compiler_params=COMM,
    )(*rider.ins))


def _join_halves(fs):
    n = len(fs)

    def body(*refs):
        f_refs, o_refs, send_sems, recv_sems = refs[:n], refs[n:2 * n], refs[2 * n], refs[2 * n + 1]
        x, y, c, _ = _place()
        cps = []
        for i in range(n):
            h = f_refs[i].shape[0] // 2
            rows = pl.ds(c * h, h)
            cps.append(pltpu.make_async_remote_copy(
                src_ref=f_refs[i].at[rows, :], dst_ref=o_refs[i].at[rows, :], send_sem=send_sems.at[i],
                recv_sem=recv_sems.at[i], device_id=(x, y, 1 - c), device_id_type=MESH))
        for cp in cps:
            cp.start()
        for i in range(n):
            h = f_refs[i].shape[0] // 2
            theirs = o_refs[i].at[pl.ds((1 - c) * h, h), :]
            pltpu.make_async_remote_copy(
                src_ref=theirs, dst_ref=theirs, send_sem=send_sems.at[i], recv_sem=recv_sems.at[i],
                device_id=(x, y, 1 - c), device_id_type=MESH).wait_recv()
        for cp in cps:
            cp.wait_send()

    return pl.pallas_call(
        body, name="rs_join_halves", in_specs=[ANY] * n, out_specs=[ANY] * n,
        out_shape=[jax.ShapeDtypeStruct(f.shape, F32) for f in fs],
        input_output_aliases={i: i for i in range(n)},
        scratch_shapes=[pltpu.SemaphoreType.DMA((n,)), pltpu.SemaphoreType.DMA((n,))],
        compiler_params=COMM,
    )(*fs)


def _allreduce_small(v):
    r, W = v.shape

    def body(v_ref, o_ref, land_ref, send_sems, recv_sems):
        x, y, c, _ = _place()
        me = 4 * x + 2 * y + c
        cps = []
        for m in range(1, 8):
            to = (x ^ (m >> 2), y ^ ((m >> 1) & 1), c ^ (m & 1))
            cps.append(pltpu.make_async_remote_copy(
                src_ref=v_ref, dst_ref=land_ref.at[m - 1], send_sem=send_sems.at[m - 1], recv_sem=recv_sems.at[m - 1],
                device_id=to, device_id_type=MESH))
        for cp in cps:
            cp.start()
        for cp in cps:
            cp.wait()
        total = jnp.zeros((r, W), F32)
        for d in range(8):
            slot = jnp.maximum((me ^ d) - 1, 0)
            total = total + jnp.where(me == d, v_ref[...], land_ref[slot])
        o_ref[...] = total

    return pl.pallas_call(
        body, name="allreduce_small",
        in_specs=[pl.BlockSpec(memory_space=pltpu.VMEM)], out_specs=pl.BlockSpec(memory_space=pltpu.VMEM),
        out_shape=jax.ShapeDtypeStruct((r, W), F32),
        scratch_shapes=[pltpu.VMEM((7, r, W), F32), pltpu.SemaphoreType.DMA((7,)), pltpu.SemaphoreType.DMA((7,))],
        compiler_params=pltpu.CompilerParams(has_side_effects=True, vmem_limit_bytes=VMEM_LIMIT),
    )(v)


def kernel(x, ffn1_w_gate, ffn1_w_up, ffn1_w_down, ffn2_w_gate, ffn2_w_up, ffn2_w_down, w_in, pool_w, pool_scale, conv_w, rpb, w_out, ln_g, ln_b, loss_target, m_ffn1_w_gate, m_ffn1_w_up, m_ffn1_w_down, m_ffn2_w_gate, m_ffn2_w_up, m_ffn2_w_down, m_w_in, m_pool_w, m_pool_scale, m_conv_w, m_rpb, m_w_out, m_ln_g, m_ln_b, v_ffn1_w_gate, v_ffn1_w_up, v_ffn1_w_down, v_ffn2_w_gate, v_ffn2_w_up, v_ffn2_w_down, v_w_in, v_pool_w, v_pool_scale, v_conv_w, v_rpb, v_w_out, v_ln_g, v_ln_b):
    weights = dict(ffn1_w_gate=ffn1_w_gate, ffn1_w_up=ffn1_w_up, ffn1_w_down=ffn1_w_down, ffn2_w_gate=ffn2_w_gate,
                   ffn2_w_up=ffn2_w_up, ffn2_w_down=ffn2_w_down, w_in=w_in, pool_w=pool_w, pool_scale=pool_scale,
                   conv_w=conv_w, rpb=rpb, w_out=w_out, ln_g=ln_g, ln_b=ln_b)
    mom_m = dict(ffn1_w_gate=m_ffn1_w_gate, ffn1_w_up=m_ffn1_w_up, ffn1_w_down=m_ffn1_w_down, ffn2_w_gate=m_ffn2_w_gate,
                 ffn2_w_up=m_ffn2_w_up, ffn2_w_down=m_ffn2_w_down, w_in=m_w_in, pool_w=m_pool_w,
                 pool_scale=m_pool_scale, conv_w=m_conv_w, rpb=m_rpb, w_out=m_w_out, ln_g=m_ln_g, ln_b=m_ln_b)
    mom_v = dict(ffn1_w_gate=v_ffn1_w_gate, ffn1_w_up=v_ffn1_w_up, ffn1_w_down=v_ffn1_w_down, ffn2_w_gate=v_ffn2_w_gate,
                 ffn2_w_up=v_ffn2_w_up, ffn2_w_down=v_ffn2_w_down, w_in=v_w_in, pool_w=v_pool_w,
                 pool_scale=v_pool_scale, conv_w=v_conv_w, rpb=v_rpb, w_out=v_w_out, ln_g=v_ln_g, ln_b=v_ln_b)
    order = list(weights)
    L = ffn1_w_gate.shape[0]
    xi, yi, ci = lax.axis_index("x"), lax.axis_index("y"), lax.axis_index("c")
    q_me = 2 * xi + yi
    x2 = x[0]
    target = loss_target[0]
    D = x2.shape[1]
    n_in = w_in.shape[-1]

    small = jnp.concatenate([ln_g.reshape(-1), ln_b.reshape(-1), conv_w.reshape(-1)])
    n_small = small.shape[0]
    small_rows = -(-n_small // (8 * 128)) * 8
    small = jnp.pad(small, (0, small_rows * 128 - n_small)).reshape(small_rows, 128)
    small_all = _gather_small(small).reshape(NQ, small_rows * 128)[:, :n_small]
    dq4 = D // NQ
    n_ln = L * 3 * dq4
    ln_g_all = small_all[:, :n_ln].reshape(NQ, L, 3, dq4).transpose(1, 2, 0, 3).reshape(L, 3, D)
    ln_b_all = small_all[:, n_ln:2 * n_ln].reshape(NQ, L, 3, dq4).transpose(1, 2, 0, 3).reshape(L, 3, D)
    conv_all = small_all[:, 2 * n_ln:].reshape(NQ, L, 3, D_CONV // NQ).transpose(1, 2, 0, 3).reshape(L, 3, D_CONV)

    def layer_shards(l):
        return [w[l].astype(BF) for w in (ffn1_w_gate, ffn1_w_up, ffn2_w_gate, ffn2_w_up, ffn1_w_down, ffn2_w_down,
                                          w_in, w_out)]

    def own_quarter(bufs, shards):
        return [lax.dynamic_update_slice(b, s[None], (q_me,) + (0,) * s.ndim) for b, s in zip(bufs, shards)]

    first = layer_shards(0)
    landed = _run_alone(_gather_ici(first), "gather_ici")
    weights_of = [own_quarter(_run_alone(_gather_d2d(landed), "gather_d2d"), first)] + [None] * (L - 1)

    ng = len(POOL_WINDOWS)
    pg = D_POOL // ng
    saved = []
    h = x2
    hb = x2.astype(BF)
    for l in range(L):
        wg1, wu1, wg2, wu2, wd1, wd2, wc, wo = weights_of[l]
        nxt = layer_shards(l + 1) if l + 1 < L else None
        eye = jnp.eye(ng, dtype=F32)
        wblk = (pool_w[l][:, :, None, :] * eye[:, None, :, None]).reshape(D_POOL, D_POOL).astype(BF)
        vec = jnp.concatenate([pool_scale[l][None], conv_all[l], jnp.zeros((4, D_POOL), F32)], axis=0)
        bias = _bias_table(rpb[l])
        lg = [ln_g_all[l, j][None] for j in range(3)]
        lb = [ln_b_all[l, j][None] for j in range(3)]
        (x1, x1b, z1, g1, u1), got_a = _ffn_fwd(h, wg1, wu1, wd1, lg[0], lb[0],
                                                rider=_gather_ici(nxt[:4]) if nxt else None)
        pabc, qkv = _proj(x1b, wc)
        yab = _mixab_fwd(pabc, wblk, vec)
        (yc,), got = _attn_fwd(qkv, bias, rider=_merge(_gather_d2d(got_a), _gather_ici(nxt[4:])) if nxt else None)
        xm, xmb, zm = _mixout_fwd(yab, yc, x1, wo, lg[1], lb[1])
        (x3, x3b, z3, g3, u3), got_rest = _ffn_fwd(xm, wg2, wu2, wd2, lg[2], lb[2],
                                                   rider=_gather_d2d(got[4:]) if nxt else None)
        if nxt:
            weights_of[l + 1] = own_quarter(got[:4] + got_rest, nxt)
        saved.append(dict(wblk=wblk, vec=vec, bias=bias, lg=lg, hb=hb, z1=z1, g1=g1, u1=u1, x1b=x1b, pabc=pabc,
                          qkv=qkv, yab=yab, yc=yc, zm=zm, xmb=xmb, z3=z3, g3=g3, u3=u3))
        h, hb = x3, x3b

    loss_tile, dh = _loss_head(h, target)
    loss = lax.psum(loss_tile[0, 0], ("x", "y", "c"))

    def add_chip(arrs, recv):
        chip = [_add_chip(g, r) for g, r in zip(arrs, recv)]
        return [cf for cf, _ in chip], [cb for _, cb in chip]

    def rs_end(chip_f, from_chips):
        return _join_halves([_add_final(cf, r) for cf, r in zip(chip_f, from_chips)])

    per_layer = [[None] * 6 for _ in range(L)]
    g_small = dict(pool_w=[None] * L, pool_scale=[None] * L, conv_w=[None] * L, rpb=[None] * L, ln_g=[None] * L,
                   ln_b=[None] * L)
    ffn1_g = None
    for l in reversed(range(L)):
        sv = saved[l]
        wg1, wu1, wg2, wu2, wd1, wd2, wc, wo = weights_of[l]
        (dxm, df, dg, du, a, ln3), got = _ffn_bwd(dh, sv["z3"], sv["g3"], sv["u3"], wg2, wu2, wd2, sv["lg"][2],
                                                  rider=_swap_halves(ffn1_g) if ffn1_g else None)
        if ffn1_g:
            ffn1_f, ffn1_b = add_chip(ffn1_g, got)
        ffn2_g = [_wgrad_gate_up(sv["xmb"], dg, du), _wgrad_down(a, df)]
        (dres, dzb, dycat, ln2), got = _mixout_bwd(dxm, sv["zm"], wo, sv["lg"][1], rider=_swap_halves(ffn2_g))
        ffn2_f, ffn2_b = add_chip(ffn2_g, got)
        g_o = _wgrad_out(sv["yab"], sv["yc"], dzb)
        dpabc, dwblk, dvec = _mixab_bwd(sv["pabc"], dycat, sv["wblk"], sv["vec"])
        (dq, dk, dv, dbias), got = _attn_bwd(sv["qkv"], sv["bias"], dycat,
                                             rider=_scatter_chips(ffn1_b) if ffn1_g else None)
        if ffn1_g:
            per_layer[l + 1][0:2] = rs_end(ffn1_f, got)
        dparts = [dpabc, dq, dk, dv]
        mix_g = [_wgrad_in(sv["x1b"], dparts, n_in), g_o]
        (dx1,), got = _proj_bwd(dres, dparts, wc, rider=_swap_halves(mix_g))
        mix_f, mix_b = add_chip(mix_g, got)
        (dh, df, dg, du, a, ln1), got = _ffn_bwd(dx1, sv["z1"], sv["g1"], sv["u1"], wg1, wu1, wd1, sv["lg"][0],
                                                 rider=_scatter_chips(ffn2_b + mix_b))
        per_layer[l][2:6] = rs_end(ffn2_f + mix_f, got)
        ffn1_g =[_wgrad_gate_up(sv["hb"], dg, du), _wgrad_down(a, df)]
        g_small["pool_w"][l] = jnp.stack([dwblk[gi * pg:(gi + 1) * pg, gi * pg:(gi + 1) * pg] for gi in range(ng)])
        g_small["pool_scale"][l] = dvec[0]
        g_small["conv_w"][l] = dvec[1:4]
        g_small["rpb"][l] = _bias_grad(dbias)
        g_small["ln_g"][l] = jnp.stack([ln1[0], ln2[0], ln3[0]])
        g_small["ln_b"][l] = jnp.stack([ln1[1], ln2[1], ln3[1]])
    ffn1_f, ffn1_b = add_chip(ffn1_g, _run_alone(_swap_halves(ffn1_g), "rs_swap_halves"))
    per_layer[0][0:2] = rs_end(ffn1_f, _run_alone(_scatter_chips(ffn1_b), "rs_scatter_chips"))
    grad_x = dh[None]

    def stacked(i, rows=None):
        parts = [per_layer[l][i] if rows is None else per_layer[l][i][rows[0]:rows[1]] for l in range(L)]
        return jnp.stack(parts)

    grads = dict(ffn1_w_gate=stacked(0, (0, D)), ffn1_w_up=stacked(0, (D, 2 * D)), ffn1_w_down=stacked(1),
                 ffn2_w_gate=stacked(2, (0, D)), ffn2_w_up=stacked(2, (D, 2 * D)), ffn2_w_down=stacked(3),
                 w_in=stacked(4), w_out=stacked(5))

    small_names = ("pool_w", "pool_scale", "conv_w", "rpb", "ln_g", "ln_b")
    small_full = {n: jnp.stack(g_small[n]) for n in small_names}
    vflat = jnp.concatenate([small_full[n].reshape(-1) for n in small_names])
    n_v = vflat.shape[0]
    v_cols = 1024
    v_rows = -(-n_v // (8 * v_cols)) * 8
    vsum = _allreduce_small(jnp.pad(vflat, (0, v_rows * v_cols - n_v)).reshape(v_rows, v_cols)).reshape(-1)
    off = 0
    for n in small_names:
        sz = int(np.prod(small_full[n].shape))
        grads[n] = vsum[off:off + sz].reshape(small_full[n].shape)
        off += sz
    for n in ("conv_w", "ln_g", "ln_b"):
        width = weights[n].shape[-1]
        grads[n] = lax.dynamic_slice_in_dim(grads[n], q_me * width, width, axis=2)

    delta, new_m, new_v = {}, {}, {}
    for n in order:
        delta[n], new_m[n], new_v[n] = _adamw(weights[n], grads[n], mom_m[n], mom_v[n])
    return (loss, grad_x, *[grads[n] for n in order], *[delta[n] for n in order], *[new_m[n] for n in order],
            *[new_v[n] for n in order])
```

```python
import numpy as np
import jax
import jax.numpy as jnp
from jax import lax
from jax.experimental import pallas as pl
from jax.experimental.pallas import tpu as pltpu

BF = jnp.bfloat16
F32 = jnp.float32
MESH = pl.DeviceIdType.MESH

DEPTH = 4
ALPHA = (2.0 * DEPTH) ** 0.25
LN_EPS = 1e-5
NEG_INF = -1e30
GRID_W = 64
NA_ROWS = 8
NA_COLS = 16
NA_HEADS = 8
HEAD_DIM = 64
D_POOL = 256
D_CONV = 256
D_NA = 512
HG = 4
LW = HG * HEAD_DIM
POOL_WINDOWS = (2, 4, 8, 16)
HALO = 8
ADAM_LR, ADAM_B1, ADAM_B2, ADAM_EPS, ADAM_WD, ADAM_STEP = 0.001, 0.9, 0.999, 1e-08, 0.01, 10
VMEM_LIMIT = 56 * 1024 * 1024
NQ = 4
WGRAD_TOKENS = 2048


def _cp(n_axes):
    return pltpu.CompilerParams(dimension_semantics=("arbitrary",) * n_axes, vmem_limit_bytes=VMEM_LIMIT)


def _full(shape):
    nd = len(shape)
    return pl.BlockSpec(shape, lambda *_: (0,) * nd)


def _quarters(arr):
    return pl.BlockSpec(arr.shape, lambda *_: (0, 0, 0), pipeline_mode=pl.Buffered(1))


ANY = pl.BlockSpec(memory_space=pl.ANY)


class _Rider:
    def __init__(self, tag, ins, outs, aliases, n, copies):
        self.tag, self.ins, self.outs, self.aliases, self.n, self.copies = tag, list(ins), list(outs), aliases, n, copies


def _merge(*riders):
    ins, outs, aliases, spans, n = [], [], {}, [], 0
    for r in riders:
        spans.append((len(ins), len(outs), n))
        aliases.update({len(ins) + i: len(outs) + j for i, j in r.aliases.items()})
        ins += r.ins
        outs += r.outs
        n += r.n

    def copies(r_in, r_out, ssem, rsem, base):
        cps = []
        for r, (i0, o0, s0) in zip(riders, spans):
            cps += r.copies(r_in[i0:i0 + len(r.ins)], r_out[o0:o0 + len(r.outs)], ssem, rsem, base + s0)
        return cps

    return _Rider("_".join(r.tag for r in riders), ins, outs, aliases, n, copies)


def _pcall(body, operands, *, name, grid, in_specs, out_specs, out_shape, scratch=(), rider=None, edges=None):
    n_in, n_out = len(in_specs), len(out_specs)
    params = dict(dimension_semantics=("arbitrary",) * len(grid), vmem_limit_bytes=VMEM_LIMIT)
    if rider is None:
        outs = pl.pallas_call(body, name=name, grid=grid, in_specs=in_specs, out_specs=out_specs, out_shape=out_shape,
                              scratch_shapes=list(scratch), compiler_params=pltpu.CompilerParams(**params))(*operands)
        return list(outs), []
    ni, no = len(rider.ins), len(rider.outs)
    first, last = edges

    def riding(*refs):
        rest = refs[n_in + ni + n_out + no:]
        cps = rider.copies(refs[n_in:n_in + ni], refs[n_in + ni + n_out:n_in + ni + n_out + no], rest[-2], rest[-1], 0)

        @pl.when(first())
        def _():
            for cp in cps:
                cp.start()

        body(*refs[:n_in], *refs[n_in + ni:n_in + ni + n_out], *rest[:-2])

        @pl.when(last())
        def _():
            for cp in cps:
                cp.wait()

    outs = pl.pallas_call(
        riding, name=f"{name}_{rider.tag}", grid=grid, in_specs=list(in_specs) + [ANY] * ni,
        out_specs=list(out_specs) + [ANY] * no, out_shape=list(out_shape) + rider.outs,
        scratch_shapes=list(scratch) + [pltpu.SemaphoreType.DMA((rider.n,)), pltpu.SemaphoreType.DMA((rider.n,))],
        input_output_aliases={n_in + i: n_out + j for i, j in rider.aliases.items()},
        compiler_params=pltpu.CompilerParams(has_side_effects=True, **params),
    )(*operands, *rider.ins)
    return list(outs[:n_out]), list(outs[n_out:])


def _edges_1d(n):
    return (lambda: pl.program_id(0) == 0), (lambda: pl.program_id(0) == n - 1)


def _edges_2d(n0, n1):
    return ((lambda: (pl.program_id(0) == 0) & (pl.program_id(1) == 0)),
            (lambda: (pl.program_id(0) == n0 - 1) & (pl.program_id(1) == n1 - 1)))


def _nt(a, b):
    return lax.dot_general(a, b, (((1,), (1,)), ((), ())), preferred_element_type=F32)


def _tn(a, b):
    return lax.dot_general(a, b, (((0,), (0,)), ((), ())), preferred_element_type=F32)


def _nn(a, b):
    return jnp.dot(a, b, preferred_element_type=F32)


def _ln_fwd(z, g, b):
    mu = jnp.mean(z, axis=-1, keepdims=True)
    zc = z - mu
    var = jnp.mean(zc * zc, axis=-1, keepdims=True)
    return zc * lax.rsqrt(var + LN_EPS) * g + b


def _ln_bwd(dy, z, g):
    mu = jnp.mean(z, axis=-1, keepdims=True)
    zc = z - mu
    var = jnp.mean(zc * zc, axis=-1, keepdims=True)
    rstd = lax.rsqrt(var + LN_EPS)
    xhat = zc * rstd
    gdy = dy * g
    m1 = jnp.mean(gdy, axis=-1, keepdims=True)
    m2 = jnp.mean(gdy * xhat, axis=-1, keepdims=True)
    return rstd * (gdy - m1 - xhat * m2), xhat


def _acc_ln_grads(acc_ref, first, dy, xhat):
    @pl.when(first)
    def _():
        acc_ref[...] = jnp.zeros_like(acc_ref)
    acc_ref[0:1, :] += jnp.sum(dy * xhat, axis=0, keepdims=True)
    acc_ref[1:2, :] += jnp.sum(dy, axis=0, keepdims=True)


def _ffn_fwd(x, wg, wu, wd, lg, lb, rider=None):
    S, D = x.shape
    fq = wg.shape[-1]
    tm = min(512, S)

    def body(x_ref, wg_ref, wu_ref, wd_ref, lg_ref, lb_ref, xo_ref, xb_ref, z_ref, g_ref, u_ref):
        x = x_ref[...]
        xb = x.astype(BF)
        acc = jnp.zeros((tm, D), F32)
        for q in range(NQ):
            g = _nn(xb, wg_ref[q])
            u = _nn(xb, wu_ref[q])
            g_ref[q] = g.astype(BF)
            u_ref[q] = u.astype(BF)
            a = g * jax.nn.sigmoid(g) * u
            acc = acc + _nn(a.astype(BF), wd_ref[q])
        z = ALPHA * x + 0.5 * acc
        xo = _ln_fwd(z, lg_ref[...], lb_ref[...])
        z_ref[...] = z
        xo_ref[...] = xo
        xb_ref[...] = xo.astype(BF)

    row = pl.BlockSpec((tm, D), lambda i: (i, 0))
    qrow = pl.BlockSpec((NQ, tm, fq), lambda i: (0, i, 0))
    return _pcall(
        body, [x, wg, wu, wd, lg, lb], name="ffn_fwd", grid=(S // tm,),
        in_specs=[row, _quarters(wg), _quarters(wu), _quarters(wd), _full((1, D)), _full((1, D))],
        out_specs=[row, row, row, qrow, qrow],
        out_shape=[jax.ShapeDtypeStruct((S, D), F32), jax.ShapeDtypeStruct((S, D), BF),
                   jax.ShapeDtypeStruct((S, D), F32), jax.ShapeDtypeStruct((NQ, S, fq), BF),
                   jax.ShapeDtypeStruct((NQ, S, fq), BF)],
        rider=rider, edges=_edges_1d(S // tm))


def _ffn_bwd(dxo, z, g, u, wg, wu, wd, lg, rider=None):
    S, D = dxo.shape
    fq = wg.shape[-1]
    tm = min(256, S)
    nt = S // tm

    def body(dxo0_ref, z0_ref, dxo1_ref, z1_ref, g_ref, u_ref, wg_ref, wu_ref, wd_ref, lg_ref,
             dx_ref, df_ref, dg_ref, du_ref, a_ref, ln_ref, dz_ref):
        i = pl.program_id(0)

        @pl.when(i == 0)
        def _():
            dy0 = dxo0_ref[...]
            dz0, xhat0 = _ln_bwd(dy0, z0_ref[...], lg_ref[...])
            dz_ref[...] = dz0
            ln_ref[...] = jnp.zeros_like(ln_ref)
            ln_ref[0:1, :] += jnp.sum(dy0 * xhat0, axis=0, keepdims=True)
            ln_ref[1:2, :] += jnp.sum(dy0, axis=0, keepdims=True)

        dz = dz_ref[...]
        dfb = (0.5 * dz).astype(BF)
        df_ref[...] = dfb
        acc = ALPHA * dz
        for q in range(NQ):
            da = _nt(dfb, wd_ref[q])
            gg = g_ref[q].astype(F32)
            uu = u_ref[q].astype(F32)
            sg = jax.nn.sigmoid(gg)
            silu = gg * sg
            a_ref[q] = (silu * uu).astype(BF)
            dgb = (da * uu * (sg * (1.0 + gg * (1.0 - sg)))).astype(BF)
            dub = (da * silu).astype(BF)
            dg_ref[q] = dgb
            du_ref[q] = dub
            acc = acc + _nt(dgb, wg_ref[q]) + _nt(dub, wu_ref[q])
        dx_ref[...] = acc
        dy1 = dxo1_ref[...]
        dz1, xhat1 = _ln_bwd(dy1, z1_ref[...], lg_ref[...])
        real = (i < nt - 1).astype(F32)
        ln_ref[0:1, :] += real * jnp.sum(dy1 * xhat1, axis=0, keepdims=True)
        ln_ref[1:2, :] += real * jnp.sum(dy1, axis=0, keepdims=True)
        dz_ref[...] = dz1

    row = pl.BlockSpec((tm, D), lambda i: (i, 0))
    first = pl.BlockSpec((tm, D), lambda i: (0, 0))
    nxt = pl.BlockSpec((tm, D), lambda i: (jnp.minimum(i + 1, nt - 1), 0))
    qrow = pl.BlockSpec((NQ, tm, fq), lambda i: (0, i, 0))
    qshape = jax.ShapeDtypeStruct((NQ, S, fq), BF)
    return _pcall(
        body, [dxo, z, dxo, z, g, u, wg, wu, wd, lg], name="ffn_bwd", grid=(nt,),
        in_specs=[first, first, nxt, nxt, qrow, qrow, _quarters(wg), _quarters(wu), _quarters(wd), _full((1, D))],
        out_specs=[row, row, qrow, qrow, qrow, _full((8, D))],
        out_shape=[jax.ShapeDtypeStruct((S, D), F32), jax.ShapeDtypeStruct((S, D), BF), qshape, qshape, qshape,
                   jax.ShapeDtypeStruct((8, D), F32)],
        scratch=[pltpu.VMEM((tm, D), F32)],
        rider=rider, edges=_edges_1d(nt))


def _wgrad_gate_up(a, dg, du):
    S, K = a.shape
    n = dg.shape[-1]
    ts = min(WGRAD_TOKENS, S)

    def body(a_ref, g_ref, u_ref, o_ref):
        @pl.when(pl.program_id(1) == 0)
        def _():
            o_ref[...] = jnp.zeros_like(o_ref)
        av = a_ref[...]
        o_ref[0:K, :] += _tn(av, g_ref[...])
        o_ref[K:2 * K, :] += _tn(av, u_ref[...])

    bspec = pl.BlockSpec((None, ts, n), lambda q, s: (q, s, 0))
    return pl.pallas_call(
        body, name="wgrad_gate_up", grid=(NQ, S // ts),
        in_specs=[pl.BlockSpec((ts, K), lambda q, s: (s, 0)), bspec, bspec],
        out_specs=pl.BlockSpec((None, 2 * K, n), lambda q, s: (q, 0, 0)),
        out_shape=jax.ShapeDtypeStruct((NQ, 2 * K, n), F32),
        compiler_params=_cp(2),
    )(a, dg, du)


def _wgrad_down(a, df):
    _, S, k = a.shape
    N = df.shape[1]
    ts = min(WGRAD_TOKENS, S)

    def body(a_ref, b_ref, o_ref):
        @pl.when(pl.program_id(1) == 0)
        def _():
            o_ref[...] = jnp.zeros_like(o_ref)
        o_ref[...] += _tn(a_ref[...], b_ref[...])

    return pl.pallas_call(
        body, name="wgrad_down", grid=(NQ, S // ts),
        in_specs=[pl.BlockSpec((None, ts, k), lambda q, s: (q, s, 0)), pl.BlockSpec((ts, N), lambda q, s: (s, 0))],
        out_specs=pl.BlockSpec((None, k, N), lambda q, s: (q, 0, 0)),
        out_shape=jax.ShapeDtypeStruct((NQ, k, N), F32),
        compiler_params=_cp(2),
    )(a, df)


def _wgrad_out(yab, yc, dzb):
    S, h = yab.shape
    D = dzb.shape[1]
    k = h // 2
    ts = min(WGRAD_TOKENS, S)

    def body(yab_ref, yc_ref, b_ref, o_ref):
        @pl.when(pl.program_id(0) == 0)
        def _():
            o_ref[...] = jnp.zeros_like(o_ref)
        b = b_ref[...]
        o_ref[0] += _tn(yab_ref[:, 0:k], b)
        o_ref[1] += _tn(yab_ref[:, k:h], b)
        o_ref[2] += _tn(yc_ref[:, 0:k], b)
        o_ref[3] += _tn(yc_ref[:, k:h], b)

    row = lambda w: pl.BlockSpec((ts, w), lambda s: (s, 0))
    return pl.pallas_call(
        body, name="wgrad_out", grid=(S // ts,),
        in_specs=[row(h), row(h), row(D)], out_specs=_full((NQ, k, D)),
        out_shape=jax.ShapeDtypeStruct((NQ, k, D), F32),
        compiler_params=_cp(1),
    )(yab, yc, dzb)


def _proj(xb, wc):
    S, D = xb.shape
    n = wc.shape[-1]
    n1 = D_POOL + 3 * D_CONV
    n2 = NQ * n - n1
    tm = min(1024, S)

    def body(x_ref, w_ref, p_ref, qkv_ref):
        x = x_ref[...]
        for q in range(NQ):
            r = _nn(x, w_ref[q])
            lo, hi = q * n, (q + 1) * n
            if hi <= n1:
                p_ref[:, lo:hi] = r
            elif lo >= n1:
                qkv_ref[:, lo - n1:hi - n1] = r.astype(BF)
            else:
                p_ref[:, lo:n1] = r[:, 0:n1 - lo]
                qkv_ref[:, 0:hi - n1] = r[:, n1 - lo:n].astype(BF)

    row = lambda w: pl.BlockSpec((tm, w), lambda i: (i, 0))
    return pl.pallas_call(
        body, name="mix_proj", grid=(S // tm,),
        in_specs=[row(D), _quarters(wc)],
        out_specs=[row(n1), row(n2)],
        out_shape=[jax.ShapeDtypeStruct((S, n1), F32), jax.ShapeDtypeStruct((S, n2), BF)],
        compiler_params=_cp(1),
    )(xb, wc)


def _mm_exact(a, b, name):
    def body(a_ref, b_ref, o_ref):
        o_ref[...] = jnp.dot(a_ref[...], b_ref[...], preferred_element_type=F32, precision=lax.Precision.HIGHEST)

    return pl.pallas_call(
        body, name=name, in_specs=[_full(a.shape), _full(b.shape)], out_specs=_full((a.shape[0], b.shape[1])),
        out_shape=jax.ShapeDtypeStruct((a.shape[0], b.shape[1]), F32),
        compiler_params=pltpu.CompilerParams(vmem_limit_bytes=VMEM_LIMIT),
    )(a, b)


def _bias_constants():
    c = np.arange(GRID_W)
    col_start = np.clip(c - NA_COLS // 2, 0, GRID_W - NA_COLS)
    valid = (c[None, :] >= col_start[:, None]) & (c[None, :] < col_start[:, None] + NA_COLS)
    dc = np.clip(c[None, :] - c[:, None], -(NA_COLS - 1), NA_COLS - 1) + (NA_COLS - 1)
    onehot = np.zeros((32, GRID_W * GRID_W), np.float32)
    onehot[dc.reshape(-1), np.arange(GRID_W * GRID_W)] = 1.0
    mask_kq = np.where(valid.T, 0.0, NEG_INF).astype(np.float32)
    mask = np.tile(mask_kq, (2 * NA_ROWS - 1, HG))
    return onehot, mask


def _bias_table(rpb):
    onehot, mask = _bias_constants()
    nr = 2 * NA_ROWS - 1
    r2 = jnp.pad(rpb.reshape(NA_HEADS * nr, 2 * NA_COLS - 1), ((0, 0), (0, 1)))
    t = _mm_exact(r2, jnp.asarray(onehot), "bias_expand")
    t = t.reshape(NA_HEADS // HG, HG, nr, GRID_W, GRID_W).transpose(0, 2, 4, 1, 3)
    return t.reshape(NA_HEADS // HG, nr * GRID_W, LW) + jnp.asarray(mask)[None]


def _bias_grad(dt):
    onehot, _ = _bias_constants()
    nr = 2 * NA_ROWS - 1
    d = dt.reshape(NA_HEADS // HG, nr, GRID_W, HG, GRID_W).transpose(0, 3, 1, 4, 2).reshape(NA_HEADS * nr, -1)
    g = _mm_exact(d, jnp.asarray(onehot.T.copy()), "bias_reduce")
    return g[:, :2 * NA_COLS - 1].reshape(NA_HEADS, nr, 2 * NA_COLS - 1)


def _attn_rows(S):
    rows = S // GRID_W
    rb = min(16, rows)
    return rows, rb


def _head_masks():
    lane = lax.broadcasted_iota(jnp.int32, (GRID_W, LW), 1)
    return [(lane >= HEAD_DIM * h) & (lane < HEAD_DIM * (h + 1)) for h in range(HG)]


def _stack_heads(x, masks):
    zero = jnp.zeros_like(x)
    return jnp.concatenate([jnp.where(m, x, zero) for m in masks], axis=0)


def _unstack_heads(x2, masks):
    out = x2[0:GRID_W]
    for h in range(1, HG):
        out = jnp.where(masks[h], x2[h * GRID_W:(h + 1) * GRID_W], out)
    return out


def _attn_step(r, rows, q, k_ref, v_ref, b_ref, masks):
    rs = jnp.clip(r - NA_ROWS // 2, 0, rows - NA_ROWS)
    s0 = rs - r + (NA_ROWS - 1)
    q2 = _stack_heads(q, masks)
    ks = pl.ds(pl.multiple_of(rs * GRID_W, GRID_W), NA_ROWS * GRID_W)
    kb = k_ref[ks, :]
    vb = v_ref[ks, :]
    bs = pl.ds(pl.multiple_of(s0 * GRID_W, GRID_W), NA_ROWS * GRID_W)
    s = _nt(kb, q2) * (HEAD_DIM ** -0.5) + b_ref[0, bs, :]
    m = jnp.max(s, axis=0, keepdims=True)
    p = jnp.exp(s - m)
    p = p / jnp.sum(p, axis=0, keepdims=True)
    return p, q2, kb, vb, ks, bs


def _attn_fwd(qkv, bias, rider=None):
    S = qkv.shape[0]
    rows, rb = _attn_rows(S)
    tq = rb * GRID_W
    ngr = NA_HEADS // HG

    def body(q_ref, k_ref, v_ref, b_ref, o_ref):
        base = pl.program_id(1) * rb
        masks = _head_masks()

        def step(i, carry):
            qs = pl.ds(pl.multiple_of(i * GRID_W, GRID_W), GRID_W)
            p, _, _, vb, _, _ = _attn_step(base + i, rows, q_ref[qs, :], k_ref, v_ref, b_ref, masks)
            o_ref[qs, :] = _unstack_heads(_tn(p.astype(BF), vb), masks).astype(BF)
            return carry

        lax.fori_loop(0, rb, step, 0, unroll=2)

    return _pcall(
        body, [qkv, qkv, qkv, bias], name="attn_fwd", grid=(ngr, rows // rb),
        in_specs=[pl.BlockSpec((tq, LW), lambda h, r: (r, h)),
                  pl.BlockSpec((S, LW), lambda h, r: (0, ngr + h)),
                  pl.BlockSpec((S, LW), lambda h, r: (0, 2 * ngr + h)),
                  pl.BlockSpec((1, bias.shape[1], LW), lambda h, r: (h, 0, 0))],
        out_specs=[pl.BlockSpec((tq, LW), lambda h, r: (r, h))],
        out_shape=[jax.ShapeDtypeStruct((S, D_NA), BF)],
        rider=rider, edges=_edges_2d(ngr, rows // rb))


def _attn_bwd(qkv, bias, dycat, rider=None):
    S = qkv.shape[0]
    rows, rb = _attn_rows(S)
    tq = rb * GRID_W
    ngr = NA_HEADS // HG
    scale = HEAD_DIM ** -0.5

    def body(q_ref, k_ref, v_ref, b_ref, do_ref, dq_ref, dk_ref, dv_ref, db_ref, dka_ref, dva_ref):
        base = pl.program_id(1) * rb
        last = pl.program_id(1) == pl.num_programs(1) - 1
        masks = _head_masks()

        @pl.when(pl.program_id(1) == 0)
        def _():
            dka_ref[...] = jnp.zeros_like(dka_ref)
            dva_ref[...] = jnp.zeros_like(dva_ref)
            db_ref[...] = jnp.zeros_like(db_ref)

        def step(i, carry):
            qs = pl.ds(pl.multiple_of(i * GRID_W, GRID_W), GRID_W)
            p, q2, kb, vb, ks, bs = _attn_step(base + i, rows, q_ref[qs, :], k_ref, v_ref, b_ref, masks)
            do2 = _stack_heads(do_ref[qs, :].astype(BF), masks)
            dp = _nt(vb, do2)
            ds = p * (dp - jnp.sum(p * dp, axis=0, keepdims=True))
            db_ref[0, bs, :] += ds
            dsb = ds.astype(BF)
            dq_ref[qs, :] = _unstack_heads(_tn(dsb, kb) * scale, masks).astype(BF)
            dka_ref[ks, :] += _nn(dsb, q2) * scale
            dva_ref[ks, :] += _nn(p.astype(BF), do2)
            return carry

        lax.fori_loop(0, rb, step, 0, unroll=2)

        @pl.when(last)
        def _():
            dk_ref[...] = dka_ref[...].astype(BF)
            dv_ref[...] = dva_ref[...].astype(BF)

    nb = bias.shape[1]
    once = dict(pipeline_mode=pl.Buffered(1))
    nd = D_NA // LW
    return _pcall(
        body, [qkv, qkv, qkv, bias, dycat], name="attn_bwd", grid=(ngr, rows // rb),
        in_specs=[pl.BlockSpec((tq, LW), lambda h, r: (r, h)),
                  pl.BlockSpec((S, LW), lambda h, r: (0, ngr + h), **once),
                  pl.BlockSpec((S, LW), lambda h, r: (0, 2 * ngr + h), **once),
                  pl.BlockSpec((1, nb, LW), lambda h, r: (h, 0, 0)),
                  pl.BlockSpec((tq, LW), lambda h, r: (r, nd + h))],
        out_specs=[pl.BlockSpec((tq, LW), lambda h, r: (r, h)),
                   pl.BlockSpec((S, LW), lambda h, r: (0, h)),
                   pl.BlockSpec((S, LW), lambda h, r: (0, h)),
                   pl.BlockSpec((1, nb, LW), lambda h, r: (h, 0, 0))],
        out_shape=[jax.ShapeDtypeStruct((S, D_NA), BF)] * 3 + [jax.ShapeDtypeStruct((ngr, nb, LW), F32)],
        scratch=[pltpu.VMEM((S, LW), F32), pltpu.VMEM((S, LW), F32)],
        rider=rider, edges=_edges_2d(ngr, rows // rb))


def _halo_specs(tm, width, S):
    hb = tm // HALO
    last = S // HALO - 1
    return [pl.BlockSpec((tm, width), lambda i: (i, 0)),
            pl.BlockSpec((HALO, width), lambda i: (jnp.maximum(i * hb - 1, 0), 0)),
            pl.BlockSpec((HALO, width), lambda i: (jnp.minimum((i + 1) * hb, last), 0))]


def _with_halo(cur_ref, prev_ref, next_ref, i, nt):
    prev = jnp.where(i > 0, prev_ref[...], 0.0)
    nxt = jnp.where(i < nt - 1, next_ref[...], 0.0)
    return jnp.concatenate([prev, cur_ref[...], nxt], axis=0)


def _shift(a, k):
    n = a.shape[0]
    return pltpu.roll(a, k % n, 0)


def _pool_lanes(n):
    lane = lax.broadcasted_iota(jnp.int32, (n, D_POOL), 1)
    group = D_POOL // len(POOL_WINDOWS)
    return [lane < group * (j + 1) for j in range(len(POOL_WINDOWS) - 1)]


def _by_window(lanes, vals):
    return jnp.where(lanes[0], vals[0], jnp.where(lanes[1], vals[1], jnp.where(lanes[2], vals[2], vals[3])))


def _pool_count(lanes, t, S):
    back = _by_window(lanes, tuple(w // 2 for w in POOL_WINDOWS))
    lo = jnp.maximum(t - back, 0)
    hi = jnp.minimum(t + back, S)
    return jnp.maximum(hi - lo, 1).astype(F32)


def _pool_p(u, lanes, cnt):
    a = u + _shift(u, 1)
    b = _shift(a, 1) + _shift(a, -1)
    c = _shift(b, 2) + _shift(b, -2)
    d = _shift(c, 4) + _shift(c, -4)
    return _by_window(lanes, (a, b, c, d)) / cnt - u


def _mixab_fwd(pabc, wblk, vec):
    S = pabc.shape[0]
    tm = min(512, S)
    nt = S // tm
    n = tm + 2 * HALO
    tile = slice(HALO, HALO + tm)

    def body(cur_ref, prev_ref, next_ref, w_ref, vec_ref, o_ref):
        i = pl.program_id(0)
        ext = _with_halo(cur_ref, prev_ref, next_ref, i, nt)
        lanes = _pool_lanes(n)
        t = i * tm - HALO + lax.broadcasted_iota(jnp.int32, (n, D_POOL), 0)
        p = _pool_p(ext[:, 0:D_POOL], lanes, _pool_count(lanes, t, S))[tile]
        o_ref[:, 0:D_POOL] = (_nn(p.astype(BF), w_ref[...]) * vec_ref[0:1, :]).astype(BF)
        zc = ext[:, 512:768] * ext[:, 768:1024]
        conv = vec_ref[1:2, :] * _shift(zc, 1) + vec_ref[2:3, :] * zc + vec_ref[3:4, :] * _shift(zc, -1)
        o_ref[:, D_POOL:D_POOL + D_CONV] = (ext[tile, 256:512] * conv[tile]).astype(BF)

    return pl.pallas_call(
        body, name="mixab_fwd", grid=(nt,),
        in_specs=_halo_specs(tm, 1024, S) + [_full((D_POOL, D_POOL)), _full((8, D_POOL))],
        out_specs=pl.BlockSpec((tm, D_POOL + D_CONV), lambda i: (i, 0)),
        out_shape=jax.ShapeDtypeStruct((S, D_POOL + D_CONV), BF),
        compiler_params=_cp(1),
    )(pabc, pabc, pabc, wblk, vec)


def _mixab_bwd(pabc, dycat, wblk, vec):
    S = pabc.shape[0]
    tm = min(512, S)
    nt = S // tm
    n = tm + 2 * HALO
    tile = slice(HALO, HALO + tm)

    def body(cur_ref, prev_ref, next_ref, dcur_ref, dprev_ref, dnext_ref, w_ref, vec_ref, o_ref, dw_ref, dvec_ref):
        i = pl.program_id(0)

        @pl.when(i == 0)
        def _():
            dw_ref[...] = jnp.zeros_like(dw_ref)
            dvec_ref[...] = jnp.zeros_like(dvec_ref)

        ext = _with_halo(cur_ref, prev_ref, next_ref, i, nt)
        dext = _with_halo(dcur_ref, dprev_ref, dnext_ref, i, nt)
        lanes = _pool_lanes(n)
        t = i * tm - HALO + lax.broadcasted_iota(jnp.int32, (n, D_POOL), 0)
        cnt = _pool_count(lanes, t, S)
        w = w_ref[...]
        scale = vec_ref[0:1, :]
        pb = _pool_p(ext[:, 0:D_POOL], lanes, cnt)[tile].astype(BF)
        dya = dext[:, 0:D_POOL]
        dvec_ref[0:1, :] += jnp.sum(dya[tile] * _nn(pb, w), axis=0, keepdims=True)
        dqb = (dya * scale).astype(BF)
        dw_ref[...] += _tn(pb, dqb[tile])
        dp = _nt(dqb, w)
        r = dp / cnt
        a = r + _shift(r, -1)
        b = _shift(a, 1) + _shift(a, -1)
        c = _shift(b, 2) + _shift(b, -2)
        d = _shift(c, 4) + _shift(c, -4)
        o_ref[:, 0:256] = (_by_window(lanes, (a, b, c, d)) - dp)[tile].astype(BF)
        gb, gc, hh = ext[:, 256:512], ext[:, 512:768], ext[:, 768:1024]
        zc = gc * hh
        zm, zp = _shift(zc, 1), _shift(zc, -1)
        w0, w1, w2 = vec_ref[1:2, :], vec_ref[2:3, :], vec_ref[3:4, :]
        dyb = dext[:, D_POOL:D_POOL + D_CONV]
        dconv = dyb * gb
        o_ref[:, 256:512] = (dyb * (w0 * zm + w1 * zc + w2 * zp))[tile].astype(BF)
        dzc = w0 * _shift(dconv, -1) + w1 * dconv + w2 * _shift(dconv, 1)
        o_ref[:, 512:768] = (dzc * hh)[tile].astype(BF)
        o_ref[:, 768:1024] = (dzc * gc)[tile].astype(BF)
        dct = dconv[tile]
        dvec_ref[1:2, :] += jnp.sum(dct * zm[tile], axis=0, keepdims=True)
        dvec_ref[2:3, :] += jnp.sum(dct * zc[tile], axis=0, keepdims=True)
        dvec_ref[3:4, :] += jnp.sum(dct * zp[tile], axis=0, keepdims=True)

    return pl.pallas_call(
        body, name="mixab_bwd", grid=(nt,),
        in_specs=_halo_specs(tm, 1024, S) + _halo_specs(tm, 512, S) + [_full((D_POOL, D_POOL)), _full((8, D_POOL))],
        out_specs=[pl.BlockSpec((tm, 1024), lambda i: (i, 0)), _full((D_POOL, D_POOL)), _full((8, D_POOL))],
        out_shape=[jax.ShapeDtypeStruct((S, 1024), BF), jax.ShapeDtypeStruct((D_POOL, D_POOL), F32),
                   jax.ShapeDtypeStruct((8, D_POOL), F32)],
        compiler_params=_cp(1),
    )(pabc, pabc, pabc, dycat, dycat, dycat, wblk, vec)


def _mixout_fwd(yab, yc, x, wo, lg, lb):
    S, D = x.shape
    tm = min(512, S)
    h = yab.shape[1]
    k = h // 2

    def body(yab_ref, yc_ref, x_ref, w_ref, lg_ref, lb_ref, xo_ref, xb_ref, z_ref):
        y = (_nn(yab_ref[:, 0:k], w_ref[0]) + _nn(yab_ref[:, k:h], w_ref[1])
             + _nn(yc_ref[:, 0:k], w_ref[2]) + _nn(yc_ref[:, k:h], w_ref[3]))
        z = ALPHA * x_ref[...] + y
        xo = _ln_fwd(z, lg_ref[...], lb_ref[...])
        z_ref[...] = z
        xo_ref[...] = xo
        xb_ref[...] = xo.astype(BF)

    row = lambda w: pl.BlockSpec((tm, w), lambda i: (i, 0))
    return pl.pallas_call(
        body, name="mixout_fwd", grid=(S // tm,),
        in_specs=[row(h), row(h), row(D), _quarters(wo), _full((1, D)), _full((1, D))],
        out_specs=[row(D), row(D), row(D)],
        out_shape=[jax.ShapeDtypeStruct((S, D), F32), jax.ShapeDtypeStruct((S, D), BF),
                   jax.ShapeDtypeStruct((S, D), F32)],
        compiler_params=_cp(1),
    )(yab, yc, x, wo, lg, lb)


def _mixout_bwd(dxo, z, wo, lg, rider=None):
    S, D = dxo.shape
    k = wo.shape[-2]
    tm = min(512, S)

    def body(dxo_ref, z_ref, w_ref, lg_ref, dres_ref, dzb_ref, dy_ref, ln_ref):
        dy = dxo_ref[...]
        dz, xhat = _ln_bwd(dy, z_ref[...], lg_ref[...])
        _acc_ln_grads(ln_ref, pl.program_id(0) == 0, dy, xhat)
        dzb = dz.astype(BF)
        dres_ref[...] = ALPHA * dz
        dzb_ref[...] = dzb
        for q in range(NQ):
            dy_ref[:, q * k:(q + 1) * k] = _nt(dzb, w_ref[q])

    row = lambda w: pl.BlockSpec((tm, w), lambda i: (i, 0))
    return _pcall(
        body, [dxo, z, wo, lg], name="mixout_bwd", grid=(S // tm,),
        in_specs=[row(D), row(D), _quarters(wo), _full((1, D))],
        out_specs=[row(D), row(D), row(NQ * k), _full((8, D))],
        out_shape=[jax.ShapeDtypeStruct((S, D), F32), jax.ShapeDtypeStruct((S, D), BF),
                   jax.ShapeDtypeStruct((S, NQ * k), F32), jax.ShapeDtypeStruct((8, D), F32)],
        rider=rider, edges=_edges_1d(S // tm))


def _take_cols(refs, lo, hi):
    parts, off = [], 0
    for r in refs:
        w = r.shape[1]
        a, b = max(lo, off), min(hi, off + w)
        if a < b:
            parts.append(r[:, a - off:b - off])
        off += w
    return parts[0] if len(parts) == 1 else jnp.concatenate(parts, axis=1)


def _proj_bwd(dres, dparts, wc, rider=None):
    S, D = dres.shape
    n = wc.shape[-1]
    tm = min(1024, S)
    np_ = len(dparts)

    def body(*refs):
        dres_ref, d_refs, w_ref, dx_ref = refs[0], refs[1:1 + np_], refs[1 + np_], refs[2 + np_]
        acc = dres_ref[...]
        for q in range(NQ):
            acc = acc + _nt(_take_cols(d_refs, q * n, (q + 1) * n), w_ref[q])
        dx_ref[...] = acc

    row = lambda w: pl.BlockSpec((tm, w), lambda i: (i, 0))
    return _pcall(
        body, [dres, *dparts, wc], name="mix_proj_bwd", grid=(S // tm,),
        in_specs=[row(D)] + [row(d.shape[1]) for d in dparts] + [_quarters(wc)],
        out_specs=[row(D)],
        out_shape=[jax.ShapeDtypeStruct((S, D), F32)],
        rider=rider, edges=_edges_1d(S // tm))


def _wgrad_in(a, dparts, n):
    S, K = a.shape
    ts = min(WGRAD_TOKENS // 2, S)
    np_ = len(dparts)

    def body(*refs):
        a_ref, d_refs, o_ref = refs[0], refs[1:1 + np_], refs[1 + np_]

        @pl.when(pl.program_id(0) == 0)
        def _():
            o_ref[...] = jnp.zeros_like(o_ref)
        av = a_ref[...]
        for q in range(NQ):
            o_ref[q] += _tn(av, _take_cols(d_refs, q * n, (q + 1) * n))

    row = lambda w: pl.BlockSpec((ts, w), lambda s: (s, 0))
    return pl.pallas_call(
        body, name="wgrad_in", grid=(S // ts,),
        in_specs=[row(K)] + [row(d.shape[1]) for d in dparts], out_specs=_full((NQ, K, n)),
        out_shape=jax.ShapeDtypeStruct((NQ, K, n), F32),
        compiler_params=_cp(1),
    )(a, *dparts)


def _loss_head(y, target):
    S, D = y.shape
    tm = min(512, S)

    def body(y_ref, t_ref, l_ref, dy_ref):
        @pl.when(pl.program_id(0) == 0)
        def _():
            l_ref[...] = jnp.zeros_like(l_ref)
        e = y_ref[...] - t_ref[...]
        dy_ref[...] = e * (1.0 / D)
        part = jnp.sum(jnp.sum(e * e, axis=1, keepdims=True) * (1.0 / D), axis=0, keepdims=True)
        l_ref[...] += 0.5 * part

    row = pl.BlockSpec((tm, D), lambda i: (i, 0))
    return pl.pallas_call(
        body, name="loss_head", grid=(S // tm,),
        in_specs=[row, row], out_specs=[_full((8, 128)), row],
        out_shape=[jax.ShapeDtypeStruct((8, 128), F32), jax.ShapeDtypeStruct((S, D), F32)],
        compiler_params=_cp(1),
    )(y, target)


def _adamw(w, g, m, v):
    shape = w.shape
    cols = shape[-1]
    rows = int(np.prod(shape[:-1]))
    w2, g2, m2, v2 = (a.reshape(rows, cols) for a in (w, g, m, v))
    tr = rows
    for cand in (512, 352, 256):
        if rows > cand and rows % cand == 0:
            tr = cand
            break

    def body(w_ref, g_ref, m_ref, v_ref, d_ref, mo_ref, vo_ref):
        g = g_ref[...]
        mn = ADAM_B1 * m_ref[...] + (1.0 - ADAM_B1) * g
        vn = ADAM_B2 * v_ref[...] + (1.0 - ADAM_B2) * (g * g)
        m_hat = mn / (1.0 - ADAM_B1 ** ADAM_STEP)
        v_hat = vn / (1.0 - ADAM_B2 ** ADAM_STEP)
        d_ref[...] = -ADAM_LR * (m_hat / (jnp.sqrt(v_hat) + ADAM_EPS) + ADAM_WD * w_ref[...])
        mo_ref[...] = mn
        vo_ref[...] = vn

    spec = pl.BlockSpec((tr, cols), lambda i: (i, 0))
    outs = pl.pallas_call(
        body, name=f"adamw_{rows}x{cols}", grid=(rows // tr,),
        in_specs=[spec] * 4, out_specs=[spec] * 3,
        out_shape=[jax.ShapeDtypeStruct((rows, cols), F32)] * 3,
        compiler_params=_cp(1),
    )(w2, g2, m2, v2)
    return tuple(o.reshape(shape) for o in outs)


def _half_tile(h):
    return h if h <= 512 else 512


def _add_chip(g, recv):
    _, R, C = g.shape
    h = R // 2
    tr = _half_tile(h)
    nb = h // tr

    def body(a_ref, b_ref, o_ref, ob_ref):
        s = a_ref[...] + b_ref[...]
        o_ref[...] = s
        ob_ref[...] = s.astype(BF)

    half = pl.BlockSpec((1, tr, C), lambda q, i: (q, i, 0))
    mine = pl.BlockSpec((1, tr, C), lambda q, i: (q, lax.axis_index("c") * nb + i, 0))
    return pl.pallas_call(
        body, name=f"rs_add_chip_{R}x{C}", grid=(NQ, nb), in_specs=[mine, half], out_specs=[half, half],
        out_shape=[jax.ShapeDtypeStruct((NQ, h, C), F32), jax.ShapeDtypeStruct((NQ, h, C), BF)],
        compiler_params=_cp(2),
    )(g, recv)


def _add_final(chip, recv):
    _, h, C = chip.shape
    tr = _half_tile(h)
    nb = h // tr

    def body(a_ref, b_ref, o_ref):
        s = a_ref[0]
        for j in range(3):
            s = s + b_ref[j].astype(F32)
        o_ref[...] = s

    return pl.pallas_call(
        body, name=f"rs_add_final_{h}x{C}", grid=(nb,),
        in_specs=[pl.BlockSpec((1, tr, C), lambda i: (2 * lax.axis_index("x") + lax.axis_index("y"), i, 0)),
                  pl.BlockSpec((3, tr, C), lambda i: (0, i, 0))],
        out_specs=pl.BlockSpec((tr, C), lambda i: (lax.axis_index("c") * nb + i, 0)),
        out_shape=jax.ShapeDtypeStruct((2 * h, C), F32),
        compiler_params=_cp(1),
    )(chip, recv)


COMM = pltpu.CompilerParams(has_side_effects=True)


def _place():
    x, y, c = lax.axis_index("x"), lax.axis_index("y"), lax.axis_index("c")
    chips = [(1 - x, y), (x, 1 - y), (1 - x, 1 - y)]
    return x, y, c, chips


def _half0(ref, c):
    n = ref.shape[0] // 2
    return ref.at[pl.ds(c * n, n)]


def _gather_ici(shards):
    n = len(shards)

    def copies(r_in, r_out, ssem, rsem, base):
        x, y, c, chips = _place()
        q = 2 * x + y
        return [pltpu.make_async_remote_copy(
            src_ref=_half0(r_in[i], c), dst_ref=_half0(r_out[i].at[q], c), send_sem=ssem.at[base + 3 * i + j],
            recv_sem=rsem.at[base + 3 * i + j], device_id=(*chip, c), device_id_type=MESH)
            for i in range(n) for j, chip in enumerate(chips)]

    return _Rider("ici", shards, [jax.ShapeDtypeStruct((NQ,) + s.shape, BF) for s in shards], {}, 3 * n, copies)


def _gather_d2d(bufs):
    n = len(bufs)

    def copies(r_in, r_out, ssem, rsem, base):
        x, y, c, chips = _place()
        return [pltpu.make_async_remote_copy(
            src_ref=_half0(r_in[i].at[2 * cx + cy], c), dst_ref=_half0(r_out[i].at[2 * cx + cy], c),
            send_sem=ssem.at[base + 3 * i + j], recv_sem=rsem.at[base + 3 * i + j], device_id=(x, y, 1 - c),
            device_id_type=MESH) for i in range(n) for j, (cx, cy) in enumerate(chips)]

    return _Rider("d2d", bufs, [jax.ShapeDtypeStruct(b.shape, b.dtype) for b in bufs], {i: i for i in range(n)},
                  3 * n, copies)


def _gather_small(small):
    sr = small.shape[0]

    def body(s_ref, o_ref, send_sems, recv_sems):
        x, y, c, chips = _place()
        o_ref[2 * x + y] = s_ref[...]
        cps = [pltpu.make_async_remote_copy(
            src_ref=s_ref, dst_ref=o_ref.at[2 * x + y], send_sem=send_sems.at[j], recv_sem=recv_sems.at[j],
            device_id=(*chip, c), device_id_type=MESH) for j, chip in enumerate(chips)]
        for cp in cps:
            cp.start()
        for j, (cx, cy) in enumerate(chips):
            pltpu.make_async_remote_copy(
                src_ref=s_ref, dst_ref=o_ref.at[2 * cx + cy], send_sem=send_sems.at[j], recv_sem=recv_sems.at[j],
                device_id=(cx, cy, c), device_id_type=MESH).wait_recv()
        for cp in cps:
            cp.wait_send()

    vm = pl.BlockSpec(memory_space=pltpu.VMEM)
    return pl.pallas_call(
        body, name="gather_small", in_specs=[vm], out_specs=vm,
        out_shape=jax.ShapeDtypeStruct((NQ, sr, 128), F32),
        scratch_shapes=[pltpu.SemaphoreType.DMA((3,)), pltpu.SemaphoreType.DMA((3,))],
        compiler_params=COMM,
    )(small)


def _swap_halves(gs):
    n = len(gs)

    def copies(r_in, r_out, ssem, rsem, base):
        x, y, c, _ = _place()
        cps = []
        for i in range(n):
            h = r_in[i].shape[1] // 2
            cps.append(pltpu.make_async_remote_copy(
                src_ref=r_in[i].at[:, pl.ds((1 - c) * h, h), :], dst_ref=r_out[i], send_sem=ssem.at[base + i],
                recv_sem=rsem.at[base + i], device_id=(x, y, 1 - c), device_id_type=MESH))
        return cps

    return _Rider("swap", gs, [jax.ShapeDtypeStruct((NQ, g.shape[1] // 2, g.shape[2]), F32) for g in gs], {}, n,
                  copies)


def _scatter_chips(chips_b):
    n = len(chips_b)

    def copies(r_in, r_out, ssem, rsem, base):
        x, y, c, chips = _place()
        return [pltpu.make_async_remote_copy(
            src_ref=r_in[i].at[2 * cx + cy], dst_ref=r_out[i].at[j], send_sem=ssem.at[base + 3 * i + j],
            recv_sem=rsem.at[base + 3 * i + j], device_id=(cx, cy, c), device_id_type=MESH)
            for i in range(n) for j, (cx, cy) in enumerate(chips)]

    return _Rider("scatter", chips_b, [jax.ShapeDtypeStruct((3,) + s.shape[1:], BF) for s in chips_b], {}, 3 * n,
                  copies)


def _run_alone(rider, name):
    ni, no = len(rider.ins), len(rider.outs)

    def body(*refs):
        cps = rider.copies(refs[:ni], refs[ni:ni + no], refs[ni + no], refs[ni + no + 1], 0)
        for cp in cps:
            cp.start()
        for cp in cps:
            cp.wait()

    return list(pl.pallas_call(
        body, name=name, in_specs=[ANY] * ni, out_specs=[ANY] * no, out_shape=rider.outs,
        input_output_aliases=dict(rider.aliases),
        scratch_shapes=[pltpu.SemaphoreType.DMA((rider.n,)), pltpu.SemaphoreType.DMA((rider.n,))],
        compiler_params=COMM,
    )(*rider.ins))


def _join_halves(fs):
    n = len(fs)

    def body(*refs):
        f_refs, o_refs, send_sems, recv_sems = refs[:n], refs[n:2 * n], refs[2 * n], refs[2 * n + 1]
        x, y, c, _ = _place()
        cps = []
        for i in range(n):
            h = f_refs[i].shape[0] // 2
            rows = pl.ds(c * h, h)
            cps.append(pltpu.make_async_remote_copy(
                src_ref=f_refs[i].at[rows, :], dst_ref=o_refs[i].at[rows, :], send_sem=send_sems.at[i],
                recv_sem=recv_sems.at[i], device_id=(x, y, 1 - c), device_id_type=MESH))
        for cp in cps:
            cp.start()
        for i in range(n):
            h = f_refs[i].shape[0] // 2
            theirs = o_refs[i].at[pl.ds((1 - c) * h, h), :]
            pltpu.make_async_remote_copy(
                src_ref=theirs, dst_ref=theirs, send_sem=send_sems.at[i], recv_sem=recv_sems.at[i],
                device_id=(x, y, 1 - c), device_id_type=MESH).wait_recv()
        for cp in cps:
            cp.wait_send()

    return pl.pallas_call(
        body, name="rs_join_halves", in_specs=[ANY] * n, out_specs=[ANY] * n,
        out_shape=[jax.ShapeDtypeStruct(f.shape, F32) for f in fs],
        input_output_aliases={i: i for i in range(n)},
        scratch_shapes=[pltpu.SemaphoreType.DMA((n,)), pltpu.SemaphoreType.DMA((n,))],
        compiler_params=COMM,
    )(*fs)


def _allreduce_small(v):
    r, W = v.shape

    def body(v_ref, o_ref, land_ref, send_sems, recv_sems):
        x, y, c, _ = _place()
        me = 4 * x + 2 * y + c
        cps = []
        for m in range(1, 8):
            to = (x ^ (m >> 2), y ^ ((m >> 1) & 1), c ^ (m & 1))
            cps.append(pltpu.make_async_remote_copy(
                src_ref=v_ref, dst_ref=land_ref.at[m - 1], send_sem=send_sems.at[m - 1], recv_sem=recv_sems.at[m - 1],
                device_id=to, device_id_type=MESH))
        for cp in cps:
            cp.start()
        for cp in cps:
            cp.wait()
        total = jnp.zeros((r, W), F32)
        for d in range(8):
            slot = jnp.maximum((me ^ d) - 1, 0)
            total = total + jnp.where(me == d, v_ref[...], land_ref[slot])
        o_ref[...] = total

    return pl.pallas_call(
        body, name="allreduce_small",
        in_specs=[pl.BlockSpec(memory_space=pltpu.VMEM)], out_specs=pl.BlockSpec(memory_space=pltpu.VMEM),
        out_shape=jax.ShapeDtypeStruct((r, W), F32),
        scratch_shapes=[pltpu.VMEM((7, r, W), F32), pltpu.SemaphoreType.DMA((7,)), pltpu.SemaphoreType.DMA((7,))],
        compiler_params=pltpu.CompilerParams(has_side_effects=True, vmem_limit_bytes=VMEM_LIMIT),
    )(v)


def kernel(x, ffn1_w_gate, ffn1_w_up, ffn1_w_down, ffn2_w_gate, ffn2_w_up, ffn2_w_down, w_in, pool_w, pool_scale, conv_w, rpb, w_out, ln_g, ln_b, loss_target, m_ffn1_w_gate, m_ffn1_w_up, m_ffn1_w_down, m_ffn2_w_gate, m_ffn2_w_up, m_ffn2_w_down, m_w_in, m_pool_w, m_pool_scale, m_conv_w, m_rpb, m_w_out, m_ln_g, m_ln_b, v_ffn1_w_gate, v_ffn1_w_up, v_ffn1_w_down, v_ffn2_w_gate, v_ffn2_w_up, v_ffn2_w_down, v_w_in, v_pool_w, v_pool_scale, v_conv_w, v_rpb, v_w_out, v_ln_g, v_ln_b):
    weights = dict(ffn1_w_gate=ffn1_w_gate, ffn1_w_up=ffn1_w_up, ffn1_w_down=ffn1_w_down, ffn2_w_gate=ffn2_w_gate,
                   ffn2_w_up=ffn2_w_up, ffn2_w_down=ffn2_w_down, w_in=w_in, pool_w=pool_w, pool_scale=pool_scale,
                   conv_w=conv_w, rpb=rpb, w_out=w_out, ln_g=ln_g, ln_b=ln_b)
    mom_m = dict(ffn1_w_gate=m_ffn1_w_gate, ffn1_w_up=m_ffn1_w_up, ffn1_w_down=m_ffn1_w_down, ffn2_w_gate=m_ffn2_w_gate,
                 ffn2_w_up=m_ffn2_w_up, ffn2_w_down=m_ffn2_w_down, w_in=m_w_in, pool_w=m_pool_w,
                 pool_scale=m_pool_scale, conv_w=m_conv_w, rpb=m_rpb, w_out=m_w_out, ln_g=m_ln_g, ln_b=m_ln_b)
    mom_v = dict(ffn1_w_gate=v_ffn1_w_gate, ffn1_w_up=v_ffn1_w_up, ffn1_w_down=v_ffn1_w_down, ffn2_w_gate=v_ffn2_w_gate,
                 ffn2_w_up=v_ffn2_w_up, ffn2_w_down=v_ffn2_w_down, w_in=v_w_in, pool_w=v_pool_w,
                 pool_scale=v_pool_scale, conv_w=v_conv_w, rpb=v_rpb, w_out=v_w_out, ln_g=v_ln_g, ln_b=v_ln_b)
    order = list(weights)
    L = ffn1_w_gate.shape[0]
    xi, yi, ci = lax.axis_index("x"), lax.axis_index("y"), lax.axis_index("c")
    q_me = 2 * xi + yi
    x2 = x[0]
    target = loss_target[0]
    D = x2.shape[1]
    n_in = w_in.shape[-1]

    small = jnp.concatenate([ln_g.reshape(-1), ln_b.reshape(-1), conv_w.reshape(-1)])
    n_small = small.shape[0]
    small_rows = -(-n_small // (8 * 128)) * 8
    small = jnp.pad(small, (0, small_rows * 128 - n_small)).reshape(small_rows, 128)
    small_all = _gather_small(small).reshape(NQ, small_rows * 128)[:, :n_small]
    dq4 = D // NQ
    n_ln = L * 3 * dq4
    ln_g_all = small_all[:, :n_ln].reshape(NQ, L, 3, dq4).transpose(1, 2, 0, 3).reshape(L, 3, D)
    ln_b_all = small_all[:, n_ln:2 * n_ln].reshape(NQ, L, 3, dq4).transpose(1, 2, 0, 3).reshape(L, 3, D)
    conv_all = small_all[:, 2 * n_ln:].reshape(NQ, L, 3, D_CONV // NQ).transpose(1, 2, 0, 3).reshape(L, 3, D_CONV)

    def layer_shards(l):
        return [w[l].astype(BF) for w in (ffn1_w_gate, ffn1_w_up, ffn1_w_down, w_in, w_out, ffn2_w_gate, ffn2_w_up,
                                          ffn2_w_down)]

    def own_quarter(bufs, shards):
        return [lax.dynamic_update_slice(b, s[None], (q_me,) + (0,) * s.ndim) for b, s in zip(bufs, shards)]

    shards = [layer_shards(l) for l in range(L)]
    landed = _run_alone(_gather_ici(shards[0][:3]), "gather_ici")
    weights_of = [own_quarter(_run_alone(_gather_d2d(landed), "gather_d2d"), shards[0][:3])] + [None] * (L - 1)

    ng = len(POOL_WINDOWS)
    pg = D_POOL // ng
    saved = []
    h = x2
    hb = x2.astype(BF)
    for l in range(L):
        nxt = shards[l + 1] if l + 1 < L else None
        wg1, wu1, wd1 = weights_of[l][:3]
        eye = jnp.eye(ng, dtype=F32)
        wblk = (pool_w[l][:, :, None, :] * eye[:, None, :, None]).reshape(D_POOL, D_POOL).astype(BF)
        vec = jnp.concatenate([pool_scale[l][None], conv_all[l], jnp.zeros((4, D_POOL), F32)], axis=0)
        bias = _bias_table(rpb[l])
        lg = [ln_g_all[l, j][None] for j in range(3)]
        lb = [ln_b_all[l, j][None] for j in range(3)]
        if l == 0:
            (x1, x1b, z1, g1, u1), got = _ffn_fwd(h, wg1, wu1, wd1, lg[0], lb[0], rider=_gather_ici(shards[0][3:]))
            weights_of[0] += own_quarter(_run_alone(_gather_d2d(got), "gather_d2d_rest"), shards[0][3:])
            r_attn = _gather_ici(nxt[:3]) if nxt else None
        else:
            (x1, x1b, z1, g1, u1), got = _ffn_fwd(h, wg1, wu1, wd1, lg[0], lb[0],
                                                  rider=_gather_ici(nxt[:3]) if nxt else None)
            r_attn = _merge(_gather_d2d(got), _gather_ici(nxt[3:])) if nxt else None
        wc, wo, wg2, wu2, wd2 = weights_of[l][3:]
        pabc, qkv = _proj(x1b, wc)
        yab = _mixab_fwd(pabc, wblk, vec)
        (yc,), got = _attn_fwd(qkv, bias, rider=r_attn)
        xm, xmb, zm = _mixout_fwd(yab, yc, x1, wo, lg[1], lb[1])
        if l == 0:
            r_ffn2 = _merge(_gather_d2d(got), _gather_ici(nxt[3:])) if nxt else None
        else:
            r_ffn2 = _gather_d2d(got[3:]) if nxt else None
        (x3, x3b, z3, g3, u3), got2 = _ffn_fwd(xm, wg2, wu2, wd2, lg[2], lb[2], rider=r_ffn2)
        if nxt and l == 0:
            got2 = got2[:3] + _run_alone(_gather_d2d(got2[3:]), "gather_d2d_rest")
            weights_of[1] = own_quarter(got2, nxt)
        elif nxt:
            weights_of[l + 1] = own_quarter(got[:3] + got2, nxt)
        saved.append(dict(wblk=wblk, vec=vec, bias=bias, lg=lg, hb=hb, z1=z1, g1=g1, u1=u1, x1b=x1b, pabc=pabc,
                          qkv=qkv, yab=yab, yc=yc, zm=zm, xmb=xmb, z3=z3, g3=g3, u3=u3))
        h, hb = x3, x3b

    loss_tile, dh = _loss_head(h, target)
    loss = lax.psum(loss_tile[0, 0], ("x", "y", "c"))

    def add_chip(arrs, recv):
        chip = [_add_chip(g, r) for g, r in zip(arrs, recv)]
        return [cf for cf, _ in chip], [cb for _, cb in chip]

    def rs_end(chip_f, from_chips):
        return _join_halves([_add_final(cf, r) for cf, r in zip(chip_f, from_chips)])

    per_layer = [[None] * 6 for _ in range(L)]
    g_small = dict(pool_w=[None] * L, pool_scale=[None] * L, conv_w=[None] * L, rpb=[None] * L, ln_g=[None] * L,
                   ln_b=[None] * L)
    ffn1_g = None
    for l in reversed(range(L)):
        sv = saved[l]
        wg1, wu1, wd1, wc, wo, wg2, wu2, wd2 = weights_of[l]
        (dxm, df, dg, du, a, ln3), got = _ffn_bwd(dh, sv["z3"], sv["g3"], sv["u3"], wg2, wu2, wd2, sv["lg"][2],
                                                  rider=_swap_halves(ffn1_g) if ffn1_g else None)
        if ffn1_g:
            ffn1_f, ffn1_b = add_chip(ffn1_g, got)
        ffn2_g = [_wgrad_gate_up(sv["xmb"], dg, du), _wgrad_down(a, df)]
        (dres, dzb, dycat, ln2), got = _mixout_bwd(dxm, sv["zm"], wo, sv["lg"][1], rider=_swap_halves(ffn2_g))
        ffn2_f, ffn2_b = add_chip(ffn2_g, got)
        g_o = _wgrad_out(sv["yab"], sv["yc"], dzb)
        dpabc, dwblk, dvec = _mixab_bwd(sv["pabc"], dycat, sv["wblk"], sv["vec"])
        (dq, dk, dv, dbias), got = _attn_bwd(sv["qkv"], sv["bias"], dycat,
                                             rider=_scatter_chips(ffn1_b) if ffn1_g else None)
        if ffn1_g:
            per_layer[l + 1][0:2] = rs_end(ffn1_f, got)
        dparts = [dpabc, dq, dk, dv]
        mix_g = [_wgrad_in(sv["x1b"], dparts, n_in), g_o]
        (dx1,), got = _proj_bwd(dres, dparts, wc, rider=_swap_halves(mix_g))
        mix_f, mix_b = add_chip(mix_g, got)
        (dh, df, dg, du, a, ln1), got = _ffn_bwd(dx1, sv["z1"], sv["g1"], sv["u1"], wg1, wu1, wd1, sv["lg"][0],
                                                 rider=_scatter_chips(ffn2_b + mix_b))
        per_layer[l][2:6] = rs_end(ffn2_f + mix_f, got)
        ffn1_g =[_wgrad_gate_up(sv["hb"], dg, du), _wgrad_down(a, df)]
        g_small["pool_w"][l] = jnp.stack([dwblk[gi * pg:(gi + 1) * pg, gi * pg:(gi + 1) * pg] for gi in range(ng)])
        g_small["pool_scale"][l] = dvec[0]
        g_small["conv_w"][l] = dvec[1:4]
        g_small["rpb"][l] = _bias_grad(dbias)
        g_small["ln_g"][l] = jnp.stack([ln1[0], ln2[0], ln3[0]])
        g_small["ln_b"][l] = jnp.stack([ln1[1], ln2[1], ln3[1]])
    ffn1_f, ffn1_b = add_chip(ffn1_g, _run_alone(_swap_halves(ffn1_g), "rs_swap_halves"))
    per_layer[0][0:2] = rs_end(ffn1_f, _run_alone(_scatter_chips(ffn1_b), "rs_scatter_chips"))
    grad_x = dh[None]

    def stacked(i, rows=None):
        parts = [per_layer[l][i] if rows is None else per_layer[l][i][rows[0]:rows[1]] for l in range(L)]
        return jnp.stack(parts)

    grads = dict(ffn1_w_gate=stacked(0, (0, D)), ffn1_w_up=stacked(0, (D, 2 * D)), ffn1_w_down=stacked(1),
                 ffn2_w_gate=stacked(2, (0, D)), ffn2_w_up=stacked(2, (D, 2 * D)), ffn2_w_down=stacked(3),
                 w_in=stacked(4), w_out=stacked(5))

    small_names = ("pool_w", "pool_scale", "conv_w", "rpb", "ln_g", "ln_b")
    small_full = {n: jnp.stack(g_small[n]) for n in small_names}
    vflat = jnp.concatenate([small_full[n].reshape(-1) for n in small_names])
    n_v = vflat.shape[0]
    v_cols = 1024
    v_rows = -(-n_v // (8 * v_cols)) * 8
    vsum = _allreduce_small(jnp.pad(vflat, (0, v_rows * v_cols - n_v)).reshape(v_rows, v_cols)).reshape(-1)
    off = 0
    for n in small_names:
        sz = int(np.prod(small_full[n].shape))
        grads[n] = vsum[off:off + sz].reshape(small_full[n].shape)
        off += sz
    for n in ("conv_w", "ln_g", "ln_b"):
        width = weights[n].shape[-1]
        grads[n] = lax.dynamic_slice_in_dim(grads[n], q_me * width, width, axis=2)

    delta, new_m, new_v = {}, {}, {}
    for n in order:
        delta[n], new_m[n], new_v[n] = _adamw(weights[n], grads[n], mom_m[n], mom_v[n])
    return (loss, grad_x, *[grads[n] for n in order], *[delta[n] for n in order], *[new_m[n] for n in order],
            *[new_v[n] for n in order])
```

```python
import numpy as np
import jax
import jax.numpy as jnp
from jax import lax
from jax.experimental import pallas as pl
from jax.experimental.pallas import tpu as pltpu

BF = jnp.bfloat16
F32 = jnp.float32
MESH = pl.DeviceIdType.MESH

DEPTH = 4
ALPHA = (2.0 * DEPTH) ** 0.25
LN_EPS = 1e-5
NEG_INF = -1e30
GRID_W = 64
NA_ROWS = 8
NA_COLS = 16
NA_HEADS = 8
HEAD_DIM = 64
D_POOL = 256
D_CONV = 256
D_NA = 512
HG = 4
LW = HG * HEAD_DIM
POOL_WINDOWS = (2, 4, 8, 16)
HALO = 8
ADAM_LR, ADAM_B1, ADAM_B2, ADAM_EPS, ADAM_WD, ADAM_STEP = 0.001, 0.9, 0.999, 1e-08, 0.01, 10
VMEM_LIMIT = 56 * 1024 * 1024
NQ = 4
WGRAD_TOKENS = 2048


def _cp(n_axes):
    return pltpu.CompilerParams(dimension_semantics=("arbitrary",) * n_axes, vmem_limit_bytes=VMEM_LIMIT)


def _full(shape):
    nd = len(shape)
    return pl.BlockSpec(shape, lambda *_: (0,) * nd)


def _quarters(arr):
    return pl.BlockSpec(arr.shape, lambda *_: (0, 0, 0), pipeline_mode=pl.Buffered(1))


ANY = pl.BlockSpec(memory_space=pl.ANY)


class _Rider:
    def __init__(self, tag, ins, outs, aliases, n, copies):
        self.tag, self.ins, self.outs, self.aliases, self.n, self.copies = tag, list(ins), list(outs), aliases, n, copies


def _merge(*riders):
    ins, outs, aliases, spans, n = [], [], {}, [], 0
    for r in riders:
        spans.append((len(ins), len(outs), n))
        aliases.update({len(ins) + i: len(outs) + j for i, j in r.aliases.items()})
        ins += r.ins
        outs += r.outs
        n += r.n

    def copies(r_in, r_out, ssem, rsem, base):
        cps = []
        for r, (i0, o0, s0) in zip(riders, spans):
            cps += r.copies(r_in[i0:i0 + len(r.ins)], r_out[o0:o0 + len(r.outs)], ssem, rsem, base + s0)
        return cps

    return _Rider("_".join(r.tag for r in riders), ins, outs, aliases, n, copies)


def _pcall(body, operands, *, name, grid, in_specs, out_specs, out_shape, scratch=(), rider=None, edges=None):
    n_in, n_out = len(in_specs), len(out_specs)
    params = dict(dimension_semantics=("arbitrary",) * len(grid), vmem_limit_bytes=VMEM_LIMIT)
    if rider is None:
        outs = pl.pallas_call(body, name=name, grid=grid, in_specs=in_specs, out_specs=out_specs, out_shape=out_shape,
                              scratch_shapes=list(scratch), compiler_params=pltpu.CompilerParams(**params))(*operands)
        return list(outs), []
    ni, no = len(rider.ins), len(rider.outs)
    first, last = edges

    def riding(*refs):
        rest = refs[n_in + ni + n_out + no:]
        cps = rider.copies(refs[n_in:n_in + ni], refs[n_in + ni + n_out:n_in + ni + n_out + no], rest[-2], rest[-1], 0)

        @pl.when(first())
        def _():
            for cp in cps:
                cp.start()

        body(*refs[:n_in], *refs[n_in + ni:n_in + ni + n_out], *rest[:-2])

        @pl.when(last())
        def _():
            for cp in cps:
                cp.wait()

    outs = pl.pallas_call(
        riding, name=f"{name}_{rider.tag}", grid=grid, in_specs=list(in_specs) + [ANY] * ni,
        out_specs=list(out_specs) + [ANY] * no, out_shape=list(out_shape) + rider.outs,
        scratch_shapes=list(scratch) + [pltpu.SemaphoreType.DMA((rider.n,)), pltpu.SemaphoreType.DMA((rider.n,))],
        input_output_aliases={n_in + i: n_out + j for i, j in rider.aliases.items()},
        compiler_params=pltpu.CompilerParams(has_side_effects=True, **params),
    )(*operands, *rider.ins)
    return list(outs[:n_out]), list(outs[n_out:])


def _edges_1d(n):
    return (lambda: pl.program_id(0) == 0), (lambda: pl.program_id(0) == n - 1)


def _edges_2d(n0, n1):
    return ((lambda: (pl.program_id(0) == 0) & (pl.program_id(1) == 0)),
            (lambda: (pl.program_id(0) == n0 - 1) & (pl.program_id(1) == n1 - 1)))


def _nt(a, b):
    return lax.dot_general(a, b, (((1,), (1,)), ((), ())), preferred_element_type=F32)


def _tn(a, b):
    return lax.dot_general(a, b, (((0,), (0,)), ((), ())), preferred_element_type=F32)


def _nn(a, b):
    return jnp.dot(a, b, preferred_element_type=F32)


def _ln_fwd(z, g, b):
    mu = jnp.mean(z, axis=-1, keepdims=True)
    zc = z - mu
    var = jnp.mean(zc * zc, axis=-1, keepdims=True)
    return zc * lax.rsqrt(var + LN_EPS) * g + b


def _ln_bwd(dy, z, g):
    mu = jnp.mean(z, axis=-1, keepdims=True)
    zc = z - mu
    var = jnp.mean(zc * zc, axis=-1, keepdims=True)
    rstd = lax.rsqrt(var + LN_EPS)
    xhat = zc * rstd
    gdy = dy * g
    m1 = jnp.mean(gdy, axis=-1, keepdims=True)
    m2 = jnp.mean(gdy * xhat, axis=-1, keepdims=True)
    return rstd * (gdy - m1 - xhat * m2), xhat


def _ffn_fwd(x, wg, wu, wd, lg, lb, rider=None):
    S, D = x.shape
    fq = wg.shape[-1]
    tm = min(512, S)

    def body(x_ref, wg_ref, wu_ref, wd_ref, lg_ref, lb_ref, xo_ref, xb_ref, z_ref, g_ref, u_ref):
        x = x_ref[...]
        xb = x.astype(BF)
        acc = jnp.zeros((tm, D), F32)
        for q in range(NQ):
            g = _nn(xb, wg_ref[q])
            u = _nn(xb, wu_ref[q])
            g_ref[q] = g.astype(BF)
            u_ref[q] = u.astype(BF)
            a = g * jax.nn.sigmoid(g) * u
            acc = acc + _nn(a.astype(BF), wd_ref[q])
        z = ALPHA * x + 0.5 * acc
        xo = _ln_fwd(z, lg_ref[...], lb_ref[...])
        z_ref[...] = z
        xo_ref[...] = xo
        xb_ref[...] = xo.astype(BF)

    row = pl.BlockSpec((tm, D), lambda i: (i, 0))
    qrow = pl.BlockSpec((NQ, tm, fq), lambda i: (0, i, 0))
    return _pcall(
        body, [x, wg, wu, wd, lg, lb], name="ffn_fwd", grid=(S // tm,),
        in_specs=[row, _quarters(wg), _quarters(wu), _quarters(wd), _full((1, D)), _full((1, D))],
        out_specs=[row, row, row, qrow, qrow],
        out_shape=[jax.ShapeDtypeStruct((S, D), F32), jax.ShapeDtypeStruct((S, D), BF),
                   jax.ShapeDtypeStruct((S, D), F32), jax.ShapeDtypeStruct((NQ, S, fq), BF),
                   jax.ShapeDtypeStruct((NQ, S, fq), BF)],
        rider=rider, edges=_edges_1d(S // tm))


def _ffn_bwd(dxo, z, g, u, wg, wu, wd, lg, rider=None):
    S, D = dxo.shape
    fq = wg.shape[-1]
    tm = min(256, S)
    nt = S // tm

    def body(dxo0_ref, z0_ref, dxo1_ref, z1_ref, g_ref, u_ref, wg_ref, wu_ref, wd_ref, lg_ref,
             dx_ref, df_ref, dg_ref, du_ref, a_ref, ln_ref, dz_ref):
        i = pl.program_id(0)

        @pl.when(i == 0)
        def _():
            dy0 = dxo0_ref[...]
            dz0, xhat0 = _ln_bwd(dy0, z0_ref[...], lg_ref[...])
            dz_ref[...] = dz0
            ln_ref[...] = jnp.zeros_like(ln_ref)
            ln_ref[0:1, :] += jnp.sum(dy0 * xhat0, axis=0, keepdims=True)
            ln_ref[1:2, :] += jnp.sum(dy0, axis=0, keepdims=True)

        dz = dz_ref[...]
        dfb = (0.5 * dz).astype(BF)
        df_ref[...] = dfb
        acc = ALPHA * dz
        for q in range(NQ):
            da = _nt(dfb, wd_ref[q])
            gg = g_ref[q].astype(F32)
            uu = u_ref[q].astype(F32)
            sg = jax.nn.sigmoid(gg)
            silu = gg * sg
            a_ref[q] = (silu * uu).astype(BF)
            dgb = (da * uu * (sg * (1.0 + gg * (1.0 - sg)))).astype(BF)
            dub = (da * silu).astype(BF)
            dg_ref[q] = dgb
            du_ref[q] = dub
            acc = acc + _nt(dgb, wg_ref[q]) + _nt(dub, wu_ref[q])
        dx_ref[...] = acc
        dy1 = dxo1_ref[...]
        dz1, xhat1 = _ln_bwd(dy1, z1_ref[...], lg_ref[...])
        real = (i < nt - 1).astype(F32)
        ln_ref[0:1, :] += real * jnp.sum(dy1 * xhat1, axis=0, keepdims=True)
        ln_ref[1:2, :] += real * jnp.sum(dy1, axis=0, keepdims=True)
        dz_ref[...] = dz1

    row = pl.BlockSpec((tm, D), lambda i: (i, 0))
    first = pl.BlockSpec((tm, D), lambda i: (0, 0))
    nxt = pl.BlockSpec((tm, D), lambda i: (jnp.minimum(i + 1, nt - 1), 0))
    qrow = pl.BlockSpec((NQ, tm, fq), lambda i: (0, i, 0))
    qshape = jax.ShapeDtypeStruct((NQ, S, fq), BF)
    return _pcall(
        body, [dxo, z, dxo, z, g, u, wg, wu, wd, lg], name="ffn_bwd", grid=(nt,),
        in_specs=[first, first, nxt, nxt, qrow, qrow, _quarters(wg), _quarters(wu), _quarters(wd), _full((1, D))],
        out_specs=[row, row, qrow, qrow, qrow, _full((8, D))],
        out_shape=[jax.ShapeDtypeStruct((S, D), F32), jax.ShapeDtypeStruct((S, D), BF), qshape, qshape, qshape,
                   jax.ShapeDtypeStruct((8, D), F32)],
        scratch=[pltpu.VMEM((tm, D), F32)],
        rider=rider, edges=_edges_1d(nt))


def _wgrad_gate_up(a, dg, du):
    S, K = a.shape
    n = dg.shape[-1]
    ts = min(WGRAD_TOKENS, S)

    def body(a_ref, g_ref, u_ref, o_ref):
        @pl.when(pl.program_id(1) == 0)
        def _():
            o_ref[...] = jnp.zeros_like(o_ref)
        av = a_ref[...]
        o_ref[0:K, :] += _tn(av, g_ref[...])
        o_ref[K:2 * K, :] += _tn(av, u_ref[...])

    bspec = pl.BlockSpec((None, ts, n), lambda q, s: (q, s, 0))
    return pl.pallas_call(
        body, name="wgrad_gate_up", grid=(NQ, S // ts),
        in_specs=[pl.BlockSpec((ts, K), lambda q, s: (s, 0)), bspec, bspec],
        out_specs=pl.BlockSpec((None, 2 * K, n), lambda q, s: (q, 0, 0)),
        out_shape=jax.ShapeDtypeStruct((NQ, 2 * K, n), F32),
        compiler_params=_cp(2),
    )(a, dg, du)


def _wgrad_down(a, df):
    _, S, k = a.shape
    N = df.shape[1]
    ts = min(WGRAD_TOKENS, S)

    def body(a_ref, b_ref, o_ref):
        @pl.when(pl.program_id(1) == 0)
        def _():
            o_ref[...] = jnp.zeros_like(o_ref)
        o_ref[...] += _tn(a_ref[...], b_ref[...])

    return pl.pallas_call(
        body, name="wgrad_down", grid=(NQ, S // ts),
        in_specs=[pl.BlockSpec((None, ts, k), lambda q, s: (q, s, 0)), pl.BlockSpec((ts, N), lambda q, s: (s, 0))],
        out_specs=pl.BlockSpec((None, k, N), lambda q, s: (q, 0, 0)),
        out_shape=jax.ShapeDtypeStruct((NQ, k, N), F32),
        compiler_params=_cp(2),
    )(a, df)


def _wgrad_out(yab, yc, dzb):
    S, h = yab.shape
    D = dzb.shape[1]
    k = h // 2
    ts = min(WGRAD_TOKENS, S)

    def body(yab_ref, yc_ref, b_ref, o_ref):
        @pl.when(pl.program_id(0) == 0)
        def _():
            o_ref[...] = jnp.zeros_like(o_ref)
        b = b_ref[...]
        o_ref[0] += _tn(yab_ref[:, 0:k], b)
        o_ref[1] += _tn(yab_ref[:, k:h], b)
        o_ref[2] += _tn(yc_ref[:, 0:k], b)
        o_ref[3] += _tn(yc_ref[:, k:h], b)

    row = lambda w: pl.BlockSpec((ts, w), lambda s: (s, 0))
    return pl.pallas_call(
        body, name="wgrad_out", grid=(S // ts,),
        in_specs=[row(h), row(h), row(D)], out_specs=_full((NQ, k, D)),
        out_shape=jax.ShapeDtypeStruct((NQ, k, D), F32),
        compiler_params=_cp(1),
    )(yab, yc, dzb)


def _proj(xb, wc):
    S, D = xb.shape
    n = wc.shape[-1]
    n1 = D_POOL + 3 * D_CONV
    n2 = NQ * n - n1
    tm = min(1024, S)

    def body(x_ref, w_ref, p_ref, qkv_ref):
        x = x_ref[...]
        for q in range(NQ):
            r = _nn(x, w_ref[q])
            lo, hi = q * n, (q + 1) * n
            if hi <= n1:
                p_ref[:, lo:hi] = r
            elif lo >= n1:
                qkv_ref[:, lo - n1:hi - n1] = r.astype(BF)
            else:
                p_ref[:, lo:n1] = r[:, 0:n1 - lo]
                qkv_ref[:, 0:hi - n1] = r[:, n1 - lo:n].astype(BF)

    row = lambda w: pl.BlockSpec((tm, w), lambda i: (i, 0))
    return pl.pallas_call(
        body, name="mix_proj", grid=(S // tm,),
        in_specs=[row(D), _quarters(wc)],
        out_specs=[row(n1), row(n2)],
        out_shape=[jax.ShapeDtypeStruct((S, n1), F32), jax.ShapeDtypeStruct((S, n2), BF)],
        compiler_params=_cp(1),
    )(xb, wc)


def _mm_exact(a, b, name):
    def body(a_ref, b_ref, o_ref):
        o_ref[...] = jnp.dot(a_ref[...], b_ref[...], preferred_element_type=F32, precision=lax.Precision.HIGHEST)

    return pl.pallas_call(
        body, name=name, in_specs=[_full(a.shape), _full(b.shape)], out_specs=_full((a.shape[0], b.shape[1])),
        out_shape=jax.ShapeDtypeStruct((a.shape[0], b.shape[1]), F32),
        compiler_params=pltpu.CompilerParams(vmem_limit_bytes=VMEM_LIMIT),
    )(a, b)


def _bias_constants():
    c = np.arange(GRID_W)
    col_start = np.clip(c - NA_COLS // 2, 0, GRID_W - NA_COLS)
    valid = (c[None, :] >= col_start[:, None]) & (c[None, :] < col_start[:, None] + NA_COLS)
    dc = np.clip(c[None, :] - c[:, None], -(NA_COLS - 1), NA_COLS - 1) + (NA_COLS - 1)
    onehot = np.zeros((32, GRID_W * GRID_W), np.float32)
    onehot[dc.reshape(-1), np.arange(GRID_W * GRID_W)] = 1.0
    mask_kq = np.where(valid.T, 0.0, NEG_INF).astype(np.float32)
    mask = np.tile(mask_kq, (2 * NA_ROWS - 1, HG))
    return onehot, mask


def _bias_table(rpb):
    onehot, mask = _bias_constants()
    nr = 2 * NA_ROWS - 1
    r2 = jnp.pad(rpb.reshape(NA_HEADS * nr, 2 * NA_COLS - 1), ((0, 0), (0, 1)))
    t = _mm_exact(r2, jnp.asarray(onehot), "bias_expand")
    t = t.reshape(NA_HEADS // HG, HG, nr, GRID_W, GRID_W).transpose(0, 2, 4, 1, 3)
    return t.reshape(NA_HEADS // HG, nr * GRID_W, LW) + jnp.asarray(mask)[None]


def _bias_grad(dt):
    onehot, _ = _bias_constants()
    nr = 2 * NA_ROWS - 1
    d = dt.reshape(NA_HEADS // HG, nr, GRID_W, HG, GRID_W).transpose(0, 3, 1, 4, 2).reshape(NA_HEADS * nr, -1)
    g = _mm_exact(d, jnp.asarray(onehot.T.copy()), "bias_reduce")
    return g[:, :2 * NA_COLS - 1].reshape(NA_HEADS, nr, 2 * NA_COLS - 1)


def _attn_rows(S):
    rows = S // GRID_W
    rb = min(16, rows)
    return rows, rb


def _head_masks():
    lane = lax.broadcasted_iota(jnp.int32, (GRID_W, LW), 1)
    return [(lane >= HEAD_DIM * h) & (lane < HEAD_DIM * (h + 1)) for h in range(HG)]


def _stack_heads(x, masks):
    zero = jnp.zeros_like(x)
    return jnp.concatenate([jnp.where(m, x, zero) for m in masks], axis=0)


def _unstack_heads(x2, masks):
    out = x2[0:GRID_W]
    for h in range(1, HG):
        out = jnp.where(masks[h], x2[h * GRID_W:(h + 1) * GRID_W], out)
    return out


def _attn_step(r, rows, q, k_ref, v_ref, b_ref, masks):
    rs = jnp.clip(r - NA_ROWS // 2, 0, rows - NA_ROWS)
    s0 = rs - r + (NA_ROWS - 1)
    q2 = _stack_heads(q, masks)
    ks = pl.ds(pl.multiple_of(rs * GRID_W, GRID_W), NA_ROWS * GRID_W)
    kb = k_ref[ks, :]
    vb = v_ref[ks, :]
    bs = pl.ds(pl.multiple_of(s0 * GRID_W, GRID_W), NA_ROWS * GRID_W)
    s = _nt(kb, q2) * (HEAD_DIM ** -0.5) + b_ref[0, bs, :]
    m = jnp.max(s, axis=0, keepdims=True)
    p = jnp.exp(s - m)
    p = p / jnp.sum(p, axis=0, keepdims=True)
    return p, q2, kb, vb, ks, bs


def _attn_fwd(qkv, bias, rider=None):
    S = qkv.shape[0]
    rows, rb = _attn_rows(S)
    tq = rb * GRID_W
    ngr = NA_HEADS // HG

    def body(q_ref, k_ref, v_ref, b_ref, o_ref):
        base = pl.program_id(1) * rb
        masks = _head_masks()

        def step(i, carry):
            qs = pl.ds(pl.multiple_of(i * GRID_W, GRID_W), GRID_W)
            p, _, _, vb, _, _ = _attn_step(base + i, rows, q_ref[qs, :], k_ref, v_ref, b_ref, masks)
            o_ref[qs, :] = _unstack_heads(_tn(p.astype(BF), vb), masks).astype(BF)
            return carry

        lax.fori_loop(0, rb, step, 0, unroll=2)

    return _pcall(
        body, [qkv, qkv, qkv, bias], name="attn_fwd", grid=(ngr, rows // rb),
        in_specs=[pl.BlockSpec((tq, LW), lambda h, r: (r, h)),
                  pl.BlockSpec((S, LW), lambda h, r: (0, ngr + h)),
                  pl.BlockSpec((S, LW), lambda h, r: (0, 2 * ngr + h)),
                  pl.BlockSpec((1, bias.shape[1], LW), lambda h, r: (h, 0, 0))],
        out_specs=[pl.BlockSpec((tq, LW), lambda h, r: (r, h))],
        out_shape=[jax.ShapeDtypeStruct((S, D_NA), BF)],
        rider=rider, edges=_edges_2d(ngr, rows // rb))


def _attn_bwd(qkv, bias, dycat, rider=None):
    S = qkv.shape[0]
    rows, rb = _attn_rows(S)
    tq = rb * GRID_W
    ngr = NA_HEADS // HG
    scale = HEAD_DIM ** -0.5

    def body(q_ref, k_ref, v_ref, b_ref, do_ref, dq_ref, dk_ref, dv_ref, db_ref, dka_ref, dva_ref):
        base = pl.program_id(1) * rb
        last = pl.program_id(1) == pl.num_programs(1) - 1
        masks = _head_masks()

        @pl.when(pl.program_id(1) == 0)
        def _():
            dka_ref[...] = jnp.zeros_like(dka_ref)
            dva_ref[...] = jnp.zeros_like(dva_ref)
            db_ref[...] = jnp.zeros_like(db_ref)

        def step(i, carry):
            qs = pl.ds(pl.multiple_of(i * GRID_W, GRID_W), GRID_W)
            p, q2, kb, vb, ks, bs = _attn_step(base + i, rows, q_ref[qs, :], k_ref, v_ref, b_ref, masks)
            do2 = _stack_heads(do_ref[qs, :].astype(BF), masks)
            dp = _nt(vb, do2)
            ds = p * (dp - jnp.sum(p * dp, axis=0, keepdims=True))
            db_ref[0, bs, :] += ds
            dsb = ds.astype(BF)
            dq_ref[qs, :] = _unstack_heads(_tn(dsb, kb) * scale, masks).astype(BF)
            dka_ref[ks, :] += _nn(dsb, q2) * scale
            dva_ref[ks, :] += _nn(p.astype(BF), do2)
            return carry

        lax.fori_loop(0, rb, step, 0, unroll=2)

        @pl.when(last)
        def _():
            dk_ref[...] = dka_ref[...].astype(BF)
            dv_ref[...] = dva_ref[...].astype(BF)

    nb = bias.shape[1]
    once = dict(pipeline_mode=pl.Buffered(1))
    nd = D_NA // LW
    return _pcall(
        body, [qkv, qkv, qkv, bias, dycat], name="attn_bwd", grid=(ngr, rows // rb),
        in_specs=[pl.BlockSpec((tq, LW), lambda h, r: (r, h)),
                  pl.BlockSpec((S, LW), lambda h, r: (0, ngr + h), **once),
                  pl.BlockSpec((S, LW), lambda h, r: (0, 2 * ngr + h), **once),
                  pl.BlockSpec((1, nb, LW), lambda h, r: (h, 0, 0)),
                  pl.BlockSpec((tq, LW), lambda h, r: (r, nd + h))],
        out_specs=[pl.BlockSpec((tq, LW), lambda h, r: (r, h)),
                   pl.BlockSpec((S, LW), lambda h, r: (0, h)),
                   pl.BlockSpec((S, LW), lambda h, r: (0, h)),
                   pl.BlockSpec((1, nb, LW), lambda h, r: (h, 0, 0))],
        out_shape=[jax.ShapeDtypeStruct((S, D_NA), BF)] * 3 + [jax.ShapeDtypeStruct((ngr, nb, LW), F32)],
        scratch=[pltpu.VMEM((S, LW), F32), pltpu.VMEM((S, LW), F32)],
        rider=rider, edges=_edges_2d(ngr, rows // rb))


def _halo_specs(tm, width, S):
    hb = tm // HALO
    last = S // HALO - 1
    return [pl.BlockSpec((tm, width), lambda i: (i, 0)),
            pl.BlockSpec((HALO, width), lambda i: (jnp.maximum(i * hb - 1, 0), 0)),
            pl.BlockSpec((HALO, width), lambda i: (jnp.minimum((i + 1) * hb, last), 0))]


def _with_halo(cur_ref, prev_ref, next_ref, i, nt):
    prev = jnp.where(i > 0, prev_ref[...], 0.0)
    nxt = jnp.where(i < nt - 1, next_ref[...], 0.0)
    return jnp.concatenate([prev, cur_ref[...], nxt], axis=0)


def _shift(a, k):
    n = a.shape[0]
    return pltpu.roll(a, k % n, 0)


def _pool_lanes(n):
    lane = lax.broadcasted_iota(jnp.int32, (n, D_POOL), 1)
    group = D_POOL // len(POOL_WINDOWS)
    return [lane < group * (j + 1) for j in range(len(POOL_WINDOWS) - 1)]


def _by_window(lanes, vals):
    return jnp.where(lanes[0], vals[0], jnp.where(lanes[1], vals[1], jnp.where(lanes[2], vals[2], vals[3])))


def _pool_count(lanes, t, S):
    back = _by_window(lanes, tuple(w // 2 for w in POOL_WINDOWS))
    lo = jnp.maximum(t - back, 0)
    hi = jnp.minimum(t + back, S)
    return jnp.maximum(hi - lo, 1).astype(F32)


def _pool_p(u, lanes, cnt):
    a = u + _shift(u, 1)
    b = _shift(a, 1) + _shift(a, -1)
    c = _shift(b, 2) + _shift(b, -2)
    d = _shift(c, 4) + _shift(c, -4)
    return _by_window(lanes, (a, b, c, d)) / cnt - u


def _mixab_fwd(pabc, wblk, vec):
    S = pabc.shape[0]
    tm = min(512, S)
    nt = S // tm
    n = tm + 2 * HALO
    tile = slice(HALO, HALO + tm)

    def body(cur_ref, prev_ref, next_ref, w_ref, vec_ref, o_ref):
        i = pl.program_id(0)
        ext = _with_halo(cur_ref, prev_ref, next_ref, i, nt)
        lanes = _pool_lanes(n)
        t = i * tm - HALO + lax.broadcasted_iota(jnp.int32, (n, D_POOL), 0)
        p = _pool_p(ext[:, 0:D_POOL], lanes, _pool_count(lanes, t, S))[tile]
        o_ref[:, 0:D_POOL] = (_nn(p.astype(BF), w_ref[...]) * vec_ref[0:1, :]).astype(BF)
        zc = ext[:, 512:768] * ext[:, 768:1024]
        conv = vec_ref[1:2, :] * _shift(zc, 1) + vec_ref[2:3, :] * zc + vec_ref[3:4, :] * _shift(zc, -1)
        o_ref[:, D_POOL:D_POOL + D_CONV] = (ext[tile, 256:512] * conv[tile]).astype(BF)

    return pl.pallas_call(
        body, name="mixab_fwd", grid=(nt,),
        in_specs=_halo_specs(tm, 1024, S) + [_full((D_POOL, D_POOL)), _full((8, D_POOL))],
        out_specs=pl.BlockSpec((tm, D_POOL + D_CONV), lambda i: (i, 0)),
        out_shape=jax.ShapeDtypeStruct((S, D_POOL + D_CONV), BF),
        compiler_params=_cp(1),
    )(pabc, pabc, pabc, wblk, vec)


def _mixab_bwd(pabc, dycat, wblk, vec):
    S = pabc.shape[0]
    tm = min(512, S)
    nt = S // tm
    n = tm + 2 * HALO
    tile = slice(HALO, HALO + tm)

    def body(cur_ref, prev_ref, next_ref, dcur_ref, dprev_ref, dnext_ref, w_ref, vec_ref, o_ref, dw_ref, dvec_ref):
        i = pl.program_id(0)

        @pl.when(i == 0)
        def _():
            dw_ref[...] = jnp.zeros_like(dw_ref)
            dvec_ref[...] = jnp.zeros_like(dvec_ref)

        ext = _with_halo(cur_ref, prev_ref, next_ref, i, nt)
        dext = _with_halo(dcur_ref, dprev_ref, dnext_ref, i, nt)
        lanes = _pool_lanes(n)
        t = i * tm - HALO + lax.broadcasted_iota(jnp.int32, (n, D_POOL), 0)
        cnt = _pool_count(lanes, t, S)
        w = w_ref[...]
        scale = vec_ref[0:1, :]
        pb = _pool_p(ext[:, 0:D_POOL], lanes, cnt)[tile].astype(BF)
        dya = dext[:, 0:D_POOL]
        dvec_ref[0:1, :] += jnp.sum(dya[tile] * _nn(pb, w), axis=0, keepdims=True)
        dqb = (dya * scale).astype(BF)
        dw_ref[...] += _tn(pb, dqb[tile])
        dp = _nt(dqb, w)
        r = dp / cnt
        a = r + _shift(r, -1)
        b = _shift(a, 1) + _shift(a, -1)
        c = _shift(b, 2) + _shift(b, -2)
        d = _shift(c, 4) + _shift(c, -4)
        o_ref[:, 0:256] = (_by_window(lanes, (a, b, c, d)) - dp)[tile].astype(BF)
        gb, gc, hh = ext[:, 256:512], ext[:, 512:768], ext[:, 768:1024]
        zc = gc * hh
        zm, zp = _shift(zc, 1), _shift(zc, -1)
        w0, w1, w2 = vec_ref[1:2, :], vec_ref[2:3, :], vec_ref[3:4, :]
        dyb = dext[:, D_POOL:D_POOL + D_CONV]
        dconv = dyb * gb
        o_ref[:, 256:512] = (dyb * (w0 * zm + w1 * zc + w2 * zp))[tile].astype(BF)
        dzc = w0 * _shift(dconv, -1) + w1 * dconv + w2 * _shift(dconv, 1)
        o_ref[:, 512:768] = (dzc * hh)[tile].astype(BF)
        o_ref[:, 768:1024] = (dzc * gc)[tile].astype(BF)
        dct = dconv[tile]
        dvec_ref[1:2, :] += jnp.sum(dct * zm[tile], axis=0, keepdims=True)
        dvec_ref[2:3, :] += jnp.sum(dct * zc[tile], axis=0, keepdims=True)
        dvec_ref[3:4, :] += jnp.sum(dct * zp[tile], axis=0, keepdims=True)

    return pl.pallas_call(
        body, name="mixab_bwd", grid=(nt,),
        in_specs=_halo_specs(tm, 1024, S) + _halo_specs(tm, 512, S) + [_full((D_POOL, D_POOL)), _full((8, D_POOL))],
        out_specs=[pl.BlockSpec((tm, 1024), lambda i: (i, 0)), _full((D_POOL, D_POOL)), _full((8, D_POOL))],
        out_shape=[jax.ShapeDtypeStruct((S, 1024), BF), jax.ShapeDtypeStruct((D_POOL, D_POOL), F32),
                   jax.ShapeDtypeStruct((8, D_POOL), F32)],
        compiler_params=_cp(1),
    )(pabc, pabc, pabc, dycat, dycat, dycat, wblk, vec)


def _mixout_fwd(yab, yc, x, wo, lg, lb):
    S, D = x.shape
    tm = min(512, S)
    h = yab.shape[1]
    k = h // 2

    def body(yab_ref, yc_ref, x_ref, w_ref, lg_ref, lb_ref, xo_ref, xb_ref, z_ref):
        y = (_nn(yab_ref[:, 0:k], w_ref[0]) + _nn(yab_ref[:, k:h], w_ref[1])
             + _nn(yc_ref[:, 0:k], w_ref[2]) + _nn(yc_ref[:, k:h], w_ref[3]))
        z = ALPHA * x_ref[...] + y
        xo = _ln_fwd(z, lg_ref[...], lb_ref[...])
        z_ref[...] = z
        xo_ref[...] = xo
        xb_ref[...] = xo.astype(BF)

    row = lambda w: pl.BlockSpec((tm, w), lambda i: (i, 0))
    return pl.pallas_call(
        body, name="mixout_fwd", grid=(S // tm,),
        in_specs=[row(h), row(h), row(D), _quarters(wo), _full((1, D)), _full((1, D))],
        out_specs=[row(D), row(D), row(D)],
        out_shape=[jax.ShapeDtypeStruct((S, D), F32), jax.ShapeDtypeStruct((S, D), BF),
                   jax.ShapeDtypeStruct((S, D), F32)],
        compiler_params=_cp(1),
    )(yab, yc, x, wo, lg, lb)


def _mixout_bwd(dxo, z, wo, lg, rider=None):
    S, D = dxo.shape
    k = wo.shape[-2]
    tm = min(512, S)
    nt = S // tm

    def body(dxo0_ref, z0_ref, dxo1_ref, z1_ref, w_ref, lg_ref, dres_ref, dzb_ref, dy_ref, ln_ref, dz_ref):
        i = pl.program_id(0)

        @pl.when(i == 0)
        def _():
            dy0 = dxo0_ref[...]
            dz0, xhat0 = _ln_bwd(dy0, z0_ref[...], lg_ref[...])
            dz_ref[...] = dz0
            ln_ref[...] = jnp.zeros_like(ln_ref)
            ln_ref[0:1, :] += jnp.sum(dy0 * xhat0, axis=0, keepdims=True)
            ln_ref[1:2, :] += jnp.sum(dy0, axis=0, keepdims=True)

        dz = dz_ref[...]
        dzb = dz.astype(BF)
        dres_ref[...] = ALPHA * dz
        dzb_ref[...] = dzb
        for q in range(NQ):
            dy_ref[:, q * k:(q + 1) * k] = _nt(dzb, w_ref[q])
        dy1 = dxo1_ref[...]
        dz1, xhat1 = _ln_bwd(dy1, z1_ref[...], lg_ref[...])
        real = (i < nt - 1).astype(F32)
        ln_ref[0:1, :] += real * jnp.sum(dy1 * xhat1, axis=0, keepdims=True)
        ln_ref[1:2, :] += real * jnp.sum(dy1, axis=0, keepdims=True)
        dz_ref[...] = dz1

    row = lambda w: pl.BlockSpec((tm, w), lambda i: (i, 0))
    first = pl.BlockSpec((tm, D), lambda i: (0, 0))
    nxt = pl.BlockSpec((tm, D), lambda i: (jnp.minimum(i + 1, nt - 1), 0))
    return _pcall(
        body, [dxo, z, dxo, z, wo, lg], name="mixout_bwd", grid=(nt,),
        in_specs=[first, first, nxt, nxt, _quarters(wo), _full((1, D))],
        out_specs=[row(D), row(D), row(NQ * k), _full((8, D))],
        out_shape=[jax.ShapeDtypeStruct((S, D), F32), jax.ShapeDtypeStruct((S, D), BF),
                   jax.ShapeDtypeStruct((S, NQ * k), F32), jax.ShapeDtypeStruct((8, D), F32)],
        scratch=[pltpu.VMEM((tm, D), F32)],
        rider=rider, edges=_edges_1d(nt))


def _take_cols(refs, lo, hi):
    parts, off = [], 0
    for r in refs:
        w = r.shape[1]
        a, b = max(lo, off), min(hi, off + w)
        if a < b:
            parts.append(r[:, a - off:b - off])
        off += w
    return parts[0] if len(parts) == 1 else jnp.concatenate(parts, axis=1)


def _proj_bwd(dres, dparts, wc, rider=None):
    S, D = dres.shape
    n = wc.shape[-1]
    tm = min(1024, S)
    np_ = len(dparts)

    def body(*refs):
        dres_ref, d_refs, w_ref, dx_ref = refs[0], refs[1:1 + np_], refs[1 + np_], refs[2 + np_]
        acc = dres_ref[...]
        for q in range(NQ):
            acc = acc + _nt(_take_cols(d_refs, q * n, (q + 1) * n), w_ref[q])
        dx_ref[...] = acc

    row = lambda w: pl.BlockSpec((tm, w), lambda i: (i, 0))
    return _pcall(
        body, [dres, *dparts, wc], name="mix_proj_bwd", grid=(S // tm,),
        in_specs=[row(D)] + [row(d.shape[1]) for d in dparts] + [_quarters(wc)],
        out_specs=[row(D)],
        out_shape=[jax.ShapeDtypeStruct((S, D), F32)],
        rider=rider, edges=_edges_1d(S // tm))


def _wgrad_in(a, dparts, n):
    S, K = a.shape
    ts = min(WGRAD_TOKENS // 2, S)
    np_ = len(dparts)

    def body(*refs):
        a_ref, d_refs, o_ref = refs[0], refs[1:1 + np_], refs[1 + np_]

        @pl.when(pl.program_id(0) == 0)
        def _():
            o_ref[...] = jnp.zeros_like(o_ref)
        av = a_ref[...]
        for q in range(NQ):
            o_ref[q] += _tn(av, _take_cols(d_refs, q * n, (q + 1) * n))

    row = lambda w: pl.BlockSpec((ts, w), lambda s: (s, 0))
    return pl.pallas_call(
        body, name="wgrad_in", grid=(S // ts,),
        in_specs=[row(K)] + [row(d.shape[1]) for d in dparts], out_specs=_full((NQ, K, n)),
        out_shape=jax.ShapeDtypeStruct((NQ, K, n), F32),
        compiler_params=_cp(1),
    )(a, *dparts)


def _loss_head(y, target):
    S, D = y.shape
    tm = min(512, S)

    def body(y_ref, t_ref, l_ref, dy_ref):
        @pl.when(pl.program_id(0) == 0)
        def _():
            l_ref[...] = jnp.zeros_like(l_ref)
        e = y_ref[...] - t_ref[...]
        dy_ref[...] = e * (1.0 / D)
        part = jnp.sum(jnp.sum(e * e, axis=1, keepdims=True) * (1.0 / D), axis=0, keepdims=True)
        l_ref[...] += 0.5 * part

    row = pl.BlockSpec((tm, D), lambda i: (i, 0))
    return pl.pallas_call(
        body, name="loss_head", grid=(S // tm,),
        in_specs=[row, row], out_specs=[_full((8, 128)), row],
        out_shape=[jax.ShapeDtypeStruct((8, 128), F32), jax.ShapeDtypeStruct((S, D), F32)],
        compiler_params=_cp(1),
    )(y, target)


def _adamw_update(w, g, m, v):
    mn = ADAM_B1 * m + (1.0 - ADAM_B1) * g
    vn = ADAM_B2 * v + (1.0 - ADAM_B2) * (g * g)
    m_hat = mn / (1.0 - ADAM_B1 ** ADAM_STEP)
    v_hat = vn / (1.0 - ADAM_B2 ** ADAM_STEP)
    return -ADAM_LR * (m_hat / (jnp.sqrt(v_hat) + ADAM_EPS) + ADAM_WD * w), mn, vn


def _adamw(w, g, m, v):
    shape = w.shape
    cols = shape[-1]
    rows = int(np.prod(shape[:-1]))
    w2, g2, m2, v2 = (a.reshape(rows, cols) for a in (w, g, m, v))
    tr = rows
    for cand in (512, 352, 256):
        if rows > cand and rows % cand == 0:
            tr = cand
            break

    def body(w_ref, g_ref, m_ref, v_ref, d_ref, mo_ref, vo_ref):
        d_ref[...], mo_ref[...], vo_ref[...] = _adamw_update(w_ref[...], g_ref[...], m_ref[...], v_ref[...])

    spec = pl.BlockSpec((tr, cols), lambda i: (i, 0))
    outs = pl.pallas_call(
        body, name=f"adamw_{rows}x{cols}", grid=(rows // tr,),
        in_specs=[spec] * 4, out_specs=[spec] * 3,
        out_shape=[jax.ShapeDtypeStruct((rows, cols), F32)] * 3,
        compiler_params=_cp(1),
    )(w2, g2, m2, v2)
    return tuple(o.reshape(shape) for o in outs)


ADAMW_GROUP_STEPS = 16


def _adamw_group(items, name, rider=None):
    n = len(items)
    flat, specs, shapes = [], [], []
    for w, g, m, v in items:
        cols = w.shape[-1]
        rows = int(np.prod(w.shape[:-1]))
        tr = rows // ADAMW_GROUP_STEPS
        assert tr * ADAMW_GROUP_STEPS == rows and tr % 8 == 0, (rows, tr)
        flat += [a.reshape(rows, cols) for a in (w, g, m, v)]
        specs.append(pl.BlockSpec((tr, cols), lambda i: (i, 0)))
        shapes.append((w.shape, rows, cols))

    def body(*refs):
        ins, outs = refs[:4 * n], refs[4 * n:]
        for j in range(n):
            w_ref, g_ref, m_ref, v_ref = ins[4 * j:4 * j + 4]
            outs[3 * j][...], outs[3 * j + 1][...], outs[3 * j + 2][...] = _adamw_update(
                w_ref[...], g_ref[...], m_ref[...], v_ref[...])

    outs, got = _pcall(
        body, flat, name=name, grid=(ADAMW_GROUP_STEPS,),
        in_specs=[s for s in specs for _ in range(4)], out_specs=[s for s in specs for _ in range(3)],
        out_shape=[jax.ShapeDtypeStruct((rows, cols), F32) for _, rows, cols in shapes for _ in range(3)],
        rider=rider, edges=_edges_1d(ADAMW_GROUP_STEPS))
    return [tuple(o.reshape(shapes[j][0]) for o in outs[3 * j:3 * j + 3]) for j in range(n)], got


def _half_tile(h):
    return h if h <= 512 else 512


def _add_chip(g, recv):
    _, R, C = g.shape
    h = R // 2
    tr = _half_tile(h)
    nb = h // tr

    def body(a_ref, b_ref, o_ref, ob_ref):
        s = a_ref[...] + b_ref[...]
        ob_ref[...] = s.astype(BF)

        @pl.when(pl.program_id(1) == 2 * lax.axis_index("x") + lax.axis_index("y"))
        def _():
            o_ref[...] = s[0]

    half = pl.BlockSpec((1, tr, C), lambda i, q: (q, i, 0))
    mine = pl.BlockSpec((1, tr, C), lambda i, q: (q, lax.axis_index("c") * nb + i, 0))
    return pl.pallas_call(
        body, name=f"rs_add_chip_{R}x{C}", grid=(nb, NQ), in_specs=[mine, half],
        out_specs=[pl.BlockSpec((tr, C), lambda i, q: (i, 0)), half],
        out_shape=[jax.ShapeDtypeStruct((h, C), F32), jax.ShapeDtypeStruct((NQ, h, C), BF)],
        compiler_params=_cp(2),
    )(g, recv)


def _add_final(chip, recv):
    h, C = chip.shape
    tr = _half_tile(h)
    nb = h // tr

    def body(a_ref, b_ref, o_ref):
        s = a_ref[...]
        for j in range(3):
            s = s + b_ref[j].astype(F32)
        o_ref[...] = s

    return pl.pallas_call(
        body, name=f"rs_add_final_{h}x{C}", grid=(nb,),
        in_specs=[pl.BlockSpec((tr, C), lambda i: (i, 0)), pl.BlockSpec((3, tr, C), lambda i: (0, i, 0))],
        out_specs=pl.BlockSpec((tr, C), lambda i: (lax.axis_index("c") * nb + i, 0)),
        out_shape=jax.ShapeDtypeStruct((2 * h, C), F32),
        compiler_params=_cp(1),
    )(chip, recv)


COMM = pltpu.CompilerParams(has_side_effects=True)


def _place():
    x, y, c = lax.axis_index("x"), lax.axis_index("y"), lax.axis_index("c")
    chips = [(1 - x, y), (x, 1 - y), (1 - x, 1 - y)]
    return x, y, c, chips


def _half0(ref, c):
    n = ref.shape[0] // 2
    return ref.at[pl.ds(c * n, n)]


def _gather_ici(shards):
    n = len(shards)

    def copies(r_in, r_out, ssem, rsem, base):
        x, y, c, chips = _place()
        q = 2 * x + y
        return [pltpu.make_async_remote_copy(
            src_ref=_half0(r_in[i], c), dst_ref=_half0(r_out[i].at[q], c), send_sem=ssem.at[base + 3 * i + j],
            recv_sem=rsem.at[base + 3 * i + j], device_id=(*chip, c), device_id_type=MESH)
            for i in range(n) for j, chip in enumerate(chips)]

    return _Rider("ici", shards, [jax.ShapeDtypeStruct((NQ,) + s.shape, BF) for s in shards], {}, 3 * n, copies)


def _gather_d2d(bufs):
    n = len(bufs)

    def copies(r_in, r_out, ssem, rsem, base):
        x, y, c, chips = _place()
        return [pltpu.make_async_remote_copy(
            src_ref=_half0(r_in[i].at[2 * cx + cy], c), dst_ref=_half0(r_out[i].at[2 * cx + cy], c),
            send_sem=ssem.at[base + 3 * i + j], recv_sem=rsem.at[base + 3 * i + j], device_id=(x, y, 1 - c),
            device_id_type=MESH) for i in range(n) for j, (cx, cy) in enumerate(chips)]

    return _Rider("d2d", bufs, [jax.ShapeDtypeStruct(b.shape, b.dtype) for b in bufs], {i: i for i in range(n)},
                  3 * n, copies)


def _gather_small(small):
    sr = small.shape[0]

    def body(s_ref, o_ref, send_sems, recv_sems):
        x, y, c, chips = _place()
        o_ref[2 * x + y] = s_ref[...]
        cps = [pltpu.make_async_remote_copy(
            src_ref=s_ref, dst_ref=o_ref.at[2 * x + y], send_sem=send_sems.at[j], recv_sem=recv_sems.at[j],
            device_id=(*chip, c), device_id_type=MESH) for j, chip in enumerate(chips)]
        for cp in cps:
            cp.start()
        for j, (cx, cy) in enumerate(chips):
            pltpu.make_async_remote_copy(
                src_ref=s_ref, dst_ref=o_ref.at[2 * cx + cy], send_sem=send_sems.at[j], recv_sem=recv_sems.at[j],
                device_id=(cx, cy, c), device_id_type=MESH).wait_recv()
        for cp in cps:
            cp.wait_send()

    vm = pl.BlockSpec(memory_space=pltpu.VMEM)
    return pl.pallas_call(
        body, name="gather_small", in_specs=[vm], out_specs=vm,
        out_shape=jax.ShapeDtypeStruct((NQ, sr, 128), F32),
        scratch_shapes=[pltpu.SemaphoreType.DMA((3,)), pltpu.SemaphoreType.DMA((3,))],
        compiler_params=COMM,
    )(small)


def _swap_halves(gs):
    n = len(gs)

    def copies(r_in, r_out, ssem, rsem, base):
        x, y, c, _ = _place()
        cps = []
        for i in range(n):
            h = r_in[i].shape[1] // 2
            cps.append(pltpu.make_async_remote_copy(
                src_ref=r_in[i].at[:, pl.ds((1 - c) * h, h), :], dst_ref=r_out[i], send_sem=ssem.at[base + i],
                recv_sem=rsem.at[base + i], device_id=(x, y, 1 - c), device_id_type=MESH))
        return cps

    return _Rider("swap", gs, [jax.ShapeDtypeStruct((NQ, g.shape[1] // 2, g.shape[2]), F32) for g in gs], {}, n,
                  copies)


def _scatter_chips(chips_b):
    n = len(chips_b)

    def copies(r_in, r_out, ssem, rsem, base):
        x, y, c, chips = _place()
        return [pltpu.make_async_remote_copy(
            src_ref=r_in[i].at[2 * cx + cy], dst_ref=r_out[i].at[j], send_sem=ssem.at[base + 3 * i + j],
            recv_sem=rsem.at[base + 3 * i + j], device_id=(cx, cy, c), device_id_type=MESH)
            for i in range(n) for j, (cx, cy) in enumerate(chips)]

    return _Rider("scatter", chips_b, [jax.ShapeDtypeStruct((3,) + s.shape[1:], BF) for s in chips_b], {}, 3 * n,
                  copies)


def _run_alone(rider, name):
    ni, no = len(rider.ins), len(rider.outs)

    def body(*refs):
        cps = rider.copies(refs[:ni], refs[ni:ni + no], refs[ni + no], refs[ni + no + 1], 0)
        for cp in cps:
            cp.start()
        for cp in cps:
            cp.wait()

    return list(pl.pallas_call(
        body, name=name, in_specs=[ANY] * ni, out_specs=[ANY] * no, out_shape=rider.outs,
        input_output_aliases=dict(rider.aliases),
        scratch_shapes=[pltpu.SemaphoreType.DMA((rider.n,)), pltpu.SemaphoreType.DMA((rider.n,))],
        compiler_params=COMM,
    )(*rider.ins))


def _join_halves(fs):
    n = len(fs)

    def body(*refs):
        f_refs, o_refs, send_sems, recv_sems = refs[:n], refs[n:2 * n], refs[2 * n], refs[2 * n + 1]
        x, y, c, _ = _place()
        cps = []
        for i in range(n):
            h = f_refs[i].shape[0] // 2
            rows = pl.ds(c * h, h)
            cps.append(pltpu.make_async_remote_copy(
                src_ref=f_refs[i].at[rows, :], dst_ref=o_refs[i].at[rows, :], send_sem=send_sems.at[i],
                recv_sem=recv_sems.at[i], device_id=(x, y, 1 - c), device_id_type=MESH))
        for cp in cps:
            cp.start()
        for i in range(n):
            h = f_refs[i].shape[0] // 2
            theirs = o_refs[i].at[pl.ds((1 - c) * h, h), :]
            pltpu.make_async_remote_copy(
                src_ref=theirs, dst_ref=theirs, send_sem=send_sems.at[i], recv_sem=recv_sems.at[i],
                device_id=(x, y, 1 - c), device_id_type=MESH).wait_recv()
        for cp in cps:
            cp.wait_send()

    return pl.pallas_call(
        body, name="rs_join_halves", in_specs=[ANY] * n, out_specs=[ANY] * n,
        out_shape=[jax.ShapeDtypeStruct(f.shape, F32) for f in fs],
        input_output_aliases={i: i for i in range(n)},
        scratch_shapes=[pltpu.SemaphoreType.DMA((n,)), pltpu.SemaphoreType.DMA((n,))],
        compiler_params=COMM,
    )(*fs)


def _allreduce_small(v):
    r, W = v.shape

    def body(v_ref, o_ref, land_ref, send_sems, recv_sems):
        x, y, c, _ = _place()
        me = 4 * x + 2 * y + c
        cps = []
        for m in range(1, 8):
            to = (x ^ (m >> 2), y ^ ((m >> 1) & 1), c ^ (m & 1))
            cps.append(pltpu.make_async_remote_copy(
                src_ref=v_ref, dst_ref=land_ref.at[m - 1], send_sem=send_sems.at[m - 1], recv_sem=recv_sems.at[m - 1],
                device_id=to, device_id_type=MESH))
        for cp in cps:
            cp.start()
        for cp in cps:
            cp.wait()
        total = jnp.zeros((r, W), F32)
        for d in range(8):
            slot = jnp.maximum((me ^ d) - 1, 0)
            total = total + jnp.where(me == d, v_ref[...], land_ref[slot])
        o_ref[...] = total

    return pl.pallas_call(
        body, name="allreduce_small",
        in_specs=[pl.BlockSpec(memory_space=pltpu.VMEM)], out_specs=pl.BlockSpec(memory_space=pltpu.VMEM),
        out_shape=jax.ShapeDtypeStruct((r, W), F32),
        scratch_shapes=[pltpu.VMEM((7, r, W), F32), pltpu.SemaphoreType.DMA((7,)), pltpu.SemaphoreType.DMA((7,))],
        compiler_params=pltpu.CompilerParams(has_side_effects=True, vmem_limit_bytes=VMEM_LIMIT),
    )(v)


def kernel(x, ffn1_w_gate, ffn1_w_up, ffn1_w_down, ffn2_w_gate, ffn2_w_up, ffn2_w_down, w_in, pool_w, pool_scale, conv_w, rpb, w_out, ln_g, ln_b, loss_target, m_ffn1_w_gate, m_ffn1_w_up, m_ffn1_w_down, m_ffn2_w_gate, m_ffn2_w_up, m_ffn2_w_down, m_w_in, m_pool_w, m_pool_scale, m_conv_w, m_rpb, m_w_out, m_ln_g, m_ln_b, v_ffn1_w_gate, v_ffn1_w_up, v_ffn1_w_down, v_ffn2_w_gate, v_ffn2_w_up, v_ffn2_w_down, v_w_in, v_pool_w, v_pool_scale, v_conv_w, v_rpb, v_w_out, v_ln_g, v_ln_b):
    weights = dict(ffn1_w_gate=ffn1_w_gate, ffn1_w_up=ffn1_w_up, ffn1_w_down=ffn1_w_down, ffn2_w_gate=ffn2_w_gate,
                   ffn2_w_up=ffn2_w_up, ffn2_w_down=ffn2_w_down, w_in=w_in, pool_w=pool_w, pool_scale=pool_scale,
                   conv_w=conv_w, rpb=rpb, w_out=w_out, ln_g=ln_g, ln_b=ln_b)
    mom_m = dict(ffn1_w_gate=m_ffn1_w_gate, ffn1_w_up=m_ffn1_w_up, ffn1_w_down=m_ffn1_w_down, ffn2_w_gate=m_ffn2_w_gate,
                 ffn2_w_up=m_ffn2_w_up, ffn2_w_down=m_ffn2_w_down, w_in=m_w_in, pool_w=m_pool_w,
                 pool_scale=m_pool_scale, conv_w=m_conv_w, rpb=m_rpb, w_out=m_w_out, ln_g=m_ln_g, ln_b=m_ln_b)
    mom_v = dict(ffn1_w_gate=v_ffn1_w_gate, ffn1_w_up=v_ffn1_w_up, ffn1_w_down=v_ffn1_w_down, ffn2_w_gate=v_ffn2_w_gate,
                 ffn2_w_up=v_ffn2_w_up, ffn2_w_down=v_ffn2_w_down, w_in=v_w_in, pool_w=v_pool_w,
                 pool_scale=v_pool_scale, conv_w=v_conv_w, rpb=v_rpb, w_out=v_w_out, ln_g=v_ln_g, ln_b=v_ln_b)
    order = list(weights)
    L = ffn1_w_gate.shape[0]
    xi, yi, ci = lax.axis_index("x"), lax.axis_index("y"), lax.axis_index("c")
    q_me = 2 * xi + yi
    x2 = x[0]
    target = loss_target[0]
    D = x2.shape[1]
    n_in = w_in.shape[-1]

    small = jnp.concatenate([ln_g.reshape(-1), ln_b.reshape(-1), conv_w.reshape(-1)])
    n_small = small.shape[0]
    small_rows = -(-n_small // (8 * 128)) * 8
    small = jnp.pad(small, (0, small_rows * 128 - n_small)).reshape(small_rows, 128)
    small_all = _gather_small(small).reshape(NQ, small_rows * 128)[:, :n_small]
    dq4 = D // NQ
    n_ln = L * 3 * dq4
    ln_g_all = small_all[:, :n_ln].reshape(NQ, L, 3, dq4).transpose(1, 2, 0, 3).reshape(L, 3, D)
    ln_b_all = small_all[:, n_ln:2 * n_ln].reshape(NQ, L, 3, dq4).transpose(1, 2, 0, 3).reshape(L, 3, D)
    conv_all = small_all[:, 2 * n_ln:].reshape(NQ, L, 3, D_CONV // NQ).transpose(1, 2, 0, 3).reshape(L, 3, D_CONV)

    def layer_shards(l):
        return [w[l].astype(BF) for w in (ffn1_w_gate, ffn1_w_up, ffn1_w_down, w_in, w_out, ffn2_w_gate, ffn2_w_up,
                                          ffn2_w_down)]

    def own_quarter(bufs, shards):
        return [lax.dynamic_update_slice(b, s[None], (q_me,) + (0,) * s.ndim) for b, s in zip(bufs, shards)]

    shards = [layer_shards(l) for l in range(L)]
    landed = _run_alone(_gather_ici(shards[0][:3]), "gather_ici")
    weights_of = [own_quarter(_run_alone(_gather_d2d(landed), "gather_d2d"), shards[0][:3])] + [None] * (L - 1)

    ng = len(POOL_WINDOWS)
    pg = D_POOL // ng
    saved = []
    h = x2
    hb = x2.astype(BF)
    for l in range(L):
        nxt = shards[l + 1] if l + 1 < L else None
        wg1, wu1, wd1 = weights_of[l][:3]
        eye = jnp.eye(ng, dtype=F32)
        wblk = (pool_w[l][:, :, None, :] * eye[:, None, :, None]).reshape(D_POOL, D_POOL).astype(BF)
        vec = jnp.concatenate([pool_scale[l][None], conv_all[l], jnp.zeros((4, D_POOL), F32)], axis=0)
        bias = _bias_table(rpb[l])
        lg = [ln_g_all[l, j][None] for j in range(3)]
        lb = [ln_b_all[l, j][None] for j in range(3)]
        if l == 0:
            (x1, x1b, z1, g1, u1), got = _ffn_fwd(h, wg1, wu1, wd1, lg[0], lb[0], rider=_gather_ici(shards[0][3:]))
            weights_of[0] += own_quarter(_run_alone(_gather_d2d(got), "gather_d2d_rest"), shards[0][3:])
            r_attn = _gather_ici(nxt[:3]) if nxt else None
        else:
            (x1, x1b, z1, g1, u1), got = _ffn_fwd(h, wg1, wu1, wd1, lg[0], lb[0],
                                                  rider=_gather_ici(nxt[:3]) if nxt else None)
            r_attn = _merge(_gather_d2d(got), _gather_ici(nxt[3:])) if nxt else None
        wc, wo, wg2, wu2, wd2 = weights_of[l][3:]
        pabc, qkv = _proj(x1b, wc)
        yab = _mixab_fwd(pabc, wblk, vec)
        (yc,), got = _attn_fwd(qkv, bias, rider=r_attn)
        xm, xmb, zm = _mixout_fwd(yab, yc, x1, wo, lg[1], lb[1])
        if l == 0:
            r_ffn2 = _merge(_gather_d2d(got), _gather_ici(nxt[3:])) if nxt else None
        else:
            r_ffn2 = _gather_d2d(got[3:]) if nxt else None
        (x3, x3b, z3, g3, u3), got2 = _ffn_fwd(xm, wg2, wu2, wd2, lg[2], lb[2], rider=r_ffn2)
        if nxt and l == 0:
            got2 = got2[:3] + _run_alone(_gather_d2d(got2[3:]), "gather_d2d_rest")
            weights_of[1] = own_quarter(got2, nxt)
        elif nxt:
            weights_of[l + 1] = own_quarter(got[:3] + got2, nxt)
        saved.append(dict(wblk=wblk, vec=vec, bias=bias, lg=lg, hb=hb, z1=z1, g1=g1, u1=u1, x1b=x1b, pabc=pabc,
                          qkv=qkv, yab=yab, yc=yc, zm=zm, xmb=xmb, z3=z3, g3=g3, u3=u3))
        h, hb = x3, x3b

    loss_tile, dh = _loss_head(h, target)
    loss = lax.psum(loss_tile[0, 0], ("x", "y", "c"))

    def add_chip(arrs, recv):
        chip = [_add_chip(g, r) for g, r in zip(arrs, recv)]
        return [cf for cf, _ in chip], [cb for _, cb in chip]

    def rs_end(chip_f, from_chips):
        return _join_halves([_add_final(cf, r) for cf, r in zip(chip_f, from_chips)])

    per_layer = [[None] * 6 for _ in range(L)]
    g_small = dict(pool_w=[None] * L, pool_scale=[None] * L, conv_w=[None] * L, rpb=[None] * L, ln_g=[None] * L,
                   ln_b=[None] * L)
    ffn1_g = None
    for l in reversed(range(L)):
        sv = saved[l]
        wg1, wu1, wd1, wc, wo, wg2, wu2, wd2 = weights_of[l]
        (dxm, df, dg, du, a, ln3), got = _ffn_bwd(dh, sv["z3"], sv["g3"], sv["u3"], wg2, wu2, wd2, sv["lg"][2],
                                                  rider=_swap_halves(ffn1_g) if ffn1_g else None)
        if ffn1_g:
            ffn1_f, ffn1_b = add_chip(ffn1_g, got)
        ffn2_g = [_wgrad_gate_up(sv["xmb"], dg, du), _wgrad_down(a, df)]
        (dres, dzb, dycat, ln2), got = _mixout_bwd(dxm, sv["zm"], wo, sv["lg"][1], rider=_swap_halves(ffn2_g))
        ffn2_f, ffn2_b = add_chip(ffn2_g, got)
        g_o = _wgrad_out(sv["yab"], sv["yc"], dzb)
        dpabc, dwblk, dvec = _mixab_bwd(sv["pabc"], dycat, sv["wblk"], sv["vec"])
        (dq, dk, dv, dbias), got = _attn_bwd(sv["qkv"], sv["bias"], dycat,
                                             rider=_scatter_chips(ffn1_b) if ffn1_g else None)
        if ffn1_g:
            per_layer[l + 1][0:2] = rs_end(ffn1_f, got)
        dparts = [dpabc, dq, dk, dv]
        mix_g = [_wgrad_in(sv["x1b"], dparts, n_in), g_o]
        (dx1,), got = _proj_bwd(dres, dparts, wc, rider=_swap_halves(mix_g))
        mix_f, mix_b = add_chip(mix_g, got)
        (dh, df, dg, du, a, ln1), got = _ffn_bwd(dx1, sv["z1"], sv["g1"], sv["u1"], wg1, wu1, wd1, sv["lg"][0],
                                                 rider=_scatter_chips(ffn2_b + mix_b))
        per_layer[l][2:6] = rs_end(ffn2_f + mix_f, got)
        ffn1_g =[_wgrad_gate_up(sv["hb"], dg, du), _wgrad_down(a, df)]
        g_small["pool_w"][l] = jnp.stack([dwblk[gi * pg:(gi + 1) * pg, gi * pg:(gi + 1) * pg] for gi in range(ng)])
        g_small["pool_scale"][l] = dvec[0]
        g_small["conv_w"][l] = dvec[1:4]
        g_small["rpb"][l] = _bias_grad(dbias)
        g_small["ln_g"][l] = jnp.stack([ln1[0], ln2[0], ln3[0]])
        g_small["ln_b"][l] = jnp.stack([ln1[1], ln2[1], ln3[1]])
    grad_x = dh[None]

    def stacked(i, rows=None):
        parts = [per_layer[l][i] if rows is None else per_layer[l][i][rows[0]:rows[1]] for l in range(L)]
        return jnp.stack(parts)

    grads = dict(ffn2_w_gate=stacked(2, (0, D)), ffn2_w_up=stacked(2, (D, 2 * D)), ffn2_w_down=stacked(3),
                 w_in=stacked(4), w_out=stacked(5))
    delta, new_m, new_v = {}, {}, {}

    def adamw_group(names, tag, rider):
        res, got = _adamw_group([(weights[n], grads[n], mom_m[n], mom_v[n]) for n in names], tag, rider)
        for n, r in zip(names, res):
            delta[n], new_m[n], new_v[n] = r
        return got

    ffn1_f, ffn1_b = add_chip(ffn1_g, adamw_group(("w_in", "w_out"), "adamw_mix", _swap_halves(ffn1_g)))
    got = adamw_group(("ffn2_w_gate", "ffn2_w_up", "ffn2_w_down"), "adamw_ffn2", _scatter_chips(ffn1_b))
    per_layer[0][0:2] = rs_end(ffn1_f, got)
    grads.update(ffn1_w_gate=stacked(0, (0, D)), ffn1_w_up=stacked(0, (D, 2 * D)), ffn1_w_down=stacked(1))

    small_names = ("pool_w", "pool_scale", "conv_w", "rpb", "ln_g", "ln_b")
    small_full = {n: jnp.stack(g_small[n]) for n in small_names}
    vflat = jnp.concatenate([small_full[n].reshape(-1) for n in small_names])
    n_v = vflat.shape[0]
    v_cols = 1024
    v_rows = -(-n_v // (8 * v_cols)) * 8
    vsum = _allreduce_small(jnp.pad(vflat, (0, v_rows * v_cols - n_v)).reshape(v_rows, v_cols)).reshape(-1)
    off = 0
    for n in small_names:
        sz = int(np.prod(small_full[n].shape))
        grads[n] = vsum[off:off + sz].reshape(small_full[n].shape)
        off += sz
    for n in ("conv_w", "ln_g", "ln_b"):
        width = weights[n].shape[-1]
        grads[n] = lax.dynamic_slice_in_dim(grads[n], q_me * width, width, axis=2)

    for n in order:
        if n not in delta:
            delta[n], new_m[n], new_v[n] = _adamw(weights[n], grads[n], mom_m[n], mom_v[n])
    return (loss, grad_x, *[grads[n] for n in order], *[delta[n] for n in order], *[new_m[n] for n in order],
            *[new_v[n] for n in order])
```

```python
import numpy as np
import jax
import jax.numpy as jnp
from jax import lax
from jax.experimental import pallas as pl
from jax.experimental.pallas import tpu as pltpu

BF = jnp.bfloat16
F32 = jnp.float32
MESH = pl.DeviceIdType.MESH

DEPTH = 4
ALPHA = (2.0 * DEPTH) ** 0.25
LN_EPS = 1e-5
NEG_INF = -1e30
GRID_W = 64
NA_ROWS = 8
NA_COLS = 16
NA_HEADS = 8
HEAD_DIM = 64
D_POOL = 256
D_CONV = 256
D_NA = 512
HG = 4
LW = HG * HEAD_DIM
POOL_WINDOWS = (2, 4, 8, 16)
HALO = 8
ADAM_LR, ADAM_B1, ADAM_B2, ADAM_EPS, ADAM_WD, ADAM_STEP = 0.001, 0.9, 0.999, 1e-08, 0.01, 10
VMEM_LIMIT = 56 * 1024 * 1024
NQ = 4
WGRAD_TOKENS = 2048


def _cp(n_axes):
    return pltpu.CompilerParams(dimension_semantics=("arbitrary",) * n_axes, vmem_limit_bytes=VMEM_LIMIT)


def _full(shape):
    nd = len(shape)
    return pl.BlockSpec(shape, lambda *_: (0,) * nd)


def _quarters(arr):
    return pl.BlockSpec(arr.shape, lambda *_: (0, 0, 0), pipeline_mode=pl.Buffered(1))


ANY = pl.BlockSpec(memory_space=pl.ANY)


class _Rider:
    def __init__(self, tag, ins, outs, aliases, n, copies):
        self.tag, self.ins, self.outs, self.aliases, self.n, self.copies = tag, list(ins), list(outs), aliases, n, copies


def _merge(*riders):
    ins, outs, aliases, spans, n = [], [], {}, [], 0
    for r in riders:
        spans.append((len(ins), len(outs), n))
        aliases.update({len(ins) + i: len(outs) + j for i, j in r.aliases.items()})
        ins += r.ins
        outs += r.outs
        n += r.n

    def copies(r_in, r_out, ssem, rsem, base):
        cps = []
        for r, (i0, o0, s0) in zip(riders, spans):
            cps += r.copies(r_in[i0:i0 + len(r.ins)], r_out[o0:o0 + len(r.outs)], ssem, rsem, base + s0)
        return cps

    return _Rider("_".join(r.tag for r in riders), ins, outs, aliases, n, copies)


def _pcall(body, operands, *, name, grid, in_specs, out_specs, out_shape, scratch=(), rider=None, edges=None):
    n_in, n_out = len(in_specs), len(out_specs)
    params = dict(dimension_semantics=("arbitrary",) * len(grid), vmem_limit_bytes=VMEM_LIMIT)
    if rider is None:
        outs = pl.pallas_call(body, name=name, grid=grid, in_specs=in_specs, out_specs=out_specs, out_shape=out_shape,
                              scratch_shapes=list(scratch), compiler_params=pltpu.CompilerParams(**params))(*operands)
        return list(outs), []
    ni, no = len(rider.ins), len(rider.outs)
    first, last = edges

    def riding(*refs):
        rest = refs[n_in + ni + n_out + no:]
        cps = rider.copies(refs[n_in:n_in + ni], refs[n_in + ni + n_out:n_in + ni + n_out + no], rest[-2], rest[-1], 0)

        @pl.when(first())
        def _():
            for cp in cps:
                cp.start()

        body(*refs[:n_in], *refs[n_in + ni:n_in + ni + n_out], *rest[:-2])

        @pl.when(last())
        def _():
            for cp in cps:
                cp.wait()

    outs = pl.pallas_call(
        riding, name=f"{name}_{rider.tag}", grid=grid, in_specs=list(in_specs) + [ANY] * ni,
        out_specs=list(out_specs) + [ANY] * no, out_shape=list(out_shape) + rider.outs,
        scratch_shapes=list(scratch) + [pltpu.SemaphoreType.DMA((rider.n,)), pltpu.SemaphoreType.DMA((rider.n,))],
        input_output_aliases={n_in + i: n_out + j for i, j in rider.aliases.items()},
        compiler_params=pltpu.CompilerParams(has_side_effects=True, **params),
    )(*operands, *rider.ins)
    return list(outs[:n_out]), list(outs[n_out:])


def _edges_1d(n):
    return (lambda: pl.program_id(0) == 0), (lambda: pl.program_id(0) == n - 1)


def _edges_2d(n0, n1):
    return ((lambda: (pl.program_id(0) == 0) & (pl.program_id(1) == 0)),
            (lambda: (pl.program_id(0) == n0 - 1) & (pl.program_id(1) == n1 - 1)))


def _nt(a, b):
    return lax.dot_general(a, b, (((1,), (1,)), ((), ())), preferred_element_type=F32)


def _tn(a, b):
    return lax.dot_general(a, b, (((0,), (0,)), ((), ())), preferred_element_type=F32)


def _nn(a, b):
    return jnp.dot(a, b, preferred_element_type=F32)


def _ln_fwd(z, g, b):
    mu = jnp.mean(z, axis=-1, keepdims=True)
    zc = z - mu
    var = jnp.mean(zc * zc, axis=-1, keepdims=True)
    return zc * lax.rsqrt(var + LN_EPS) * g + b


def _ln_bwd(dy, z, g):
    mu = jnp.mean(z, axis=-1, keepdims=True)
    zc = z - mu
    var = jnp.mean(zc * zc, axis=-1, keepdims=True)
    rstd = lax.rsqrt(var + LN_EPS)
    xhat = zc * rstd
    gdy = dy * g
    m1 = jnp.mean(gdy, axis=-1, keepdims=True)
    m2 = jnp.mean(gdy * xhat, axis=-1, keepdims=True)
    return rstd * (gdy - m1 - xhat * m2), xhat


def _ffn_fwd(x, wg, wu, wd, lg, lb, rider=None):
    S, D = x.shape
    fq = wg.shape[-1]
    tm = min(512, S)

    def body(x_ref, wg_ref, wu_ref, wd_ref, lg_ref, lb_ref, xo_ref, xb_ref, z_ref, g_ref, u_ref):
        x = x_ref[...]
        xb = x.astype(BF)
        acc = jnp.zeros((tm, D), F32)
        for q in range(NQ):
            g = _nn(xb, wg_ref[q])
            u = _nn(xb, wu_ref[q])
            g_ref[q] = g.astype(BF)
            u_ref[q] = u.astype(BF)
            a = g * jax.nn.sigmoid(g) * u
            acc = acc + _nn(a.astype(BF), wd_ref[q])
        z = ALPHA * x + 0.5 * acc
        xo = _ln_fwd(z, lg_ref[...], lb_ref[...])
        z_ref[...] = z
        xo_ref[...] = xo
        xb_ref[...] = xo.astype(BF)

    row = pl.BlockSpec((tm, D), lambda i: (i, 0))
    qrow = pl.BlockSpec((NQ, tm, fq), lambda i: (0, i, 0))
    return _pcall(
        body, [x, wg, wu, wd, lg, lb], name="ffn_fwd", grid=(S // tm,),
        in_specs=[row, _quarters(wg), _quarters(wu), _quarters(wd), _full((1, D)), _full((1, D))],
        out_specs=[row, row, row, qrow, qrow],
        out_shape=[jax.ShapeDtypeStruct((S, D), F32), jax.ShapeDtypeStruct((S, D), BF),
                   jax.ShapeDtypeStruct((S, D), F32), jax.ShapeDtypeStruct((NQ, S, fq), BF),
                   jax.ShapeDtypeStruct((NQ, S, fq), BF)],
        rider=rider, edges=_edges_1d(S // tm))


def _ffn_bwd(dxo, z, g, u, wg, wu, wd, lg, rider=None):
    S, D = dxo.shape
    fq = wg.shape[-1]
    tm = min(256, S)
    nt = S // tm

    def body(dxo0_ref, z0_ref, dxo1_ref, z1_ref, g_ref, u_ref, wg_ref, wu_ref, wd_ref, lg_ref,
             dx_ref, df_ref, dg_ref, du_ref, a_ref, ln_ref, dz_ref):
        i = pl.program_id(0)

        @pl.when(i == 0)
        def _():
            dy0 = dxo0_ref[...]
            dz0, xhat0 = _ln_bwd(dy0, z0_ref[...], lg_ref[...])
            dz_ref[...] = dz0
            ln_ref[...] = jnp.zeros_like(ln_ref)
            ln_ref[0:1, :] += jnp.sum(dy0 * xhat0, axis=0, keepdims=True)
            ln_ref[1:2, :] += jnp.sum(dy0, axis=0, keepdims=True)

        dz = dz_ref[...]
        dfb = (0.5 * dz).astype(BF)
        df_ref[...] = dfb
        acc = ALPHA * dz
        for q in range(NQ):
            da = _nt(dfb, wd_ref[q])
            gg = g_ref[q].astype(F32)
            uu = u_ref[q].astype(F32)
            sg = jax.nn.sigmoid(gg)
            silu = gg * sg
            a_ref[q] = (silu * uu).astype(BF)
            dgb = (da * uu * (sg * (1.0 + gg * (1.0 - sg)))).astype(BF)
            dub = (da * silu).astype(BF)
            dg_ref[q] = dgb
            du_ref[q] = dub
            acc = acc + _nt(dgb, wg_ref[q]) + _nt(dub, wu_ref[q])
        dx_ref[...] = acc
        dy1 = dxo1_ref[...]
        dz1, xhat1 = _ln_bwd(dy1, z1_ref[...], lg_ref[...])
        real = (i < nt - 1).astype(F32)
        ln_ref[0:1, :] += real * jnp.sum(dy1 * xhat1, axis=0, keepdims=True)
        ln_ref[1:2, :] += real * jnp.sum(dy1, axis=0, keepdims=True)
        dz_ref[...] = dz1

    row = pl.BlockSpec((tm, D), lambda i: (i, 0))
    first = pl.BlockSpec((tm, D), lambda i: (0, 0))
    nxt = pl.BlockSpec((tm, D), lambda i: (jnp.minimum(i + 1, nt - 1), 0))
    qrow = pl.BlockSpec((NQ, tm, fq), lambda i: (0, i, 0))
    qshape = jax.ShapeDtypeStruct((NQ, S, fq), BF)
    return _pcall(
        body, [dxo, z, dxo, z, g, u, wg, wu, wd, lg], name="ffn_bwd", grid=(nt,),
        in_specs=[first, first, nxt, nxt, qrow, qrow, _quarters(wg), _quarters(wu), _quarters(wd), _full((1, D))],
        out_specs=[row, row, qrow, qrow, qrow, _full((8, D))],
        out_shape=[jax.ShapeDtypeStruct((S, D), F32), jax.ShapeDtypeStruct((S, D), BF), qshape, qshape, qshape,
                   jax.ShapeDtypeStruct((8, D), F32)],
        scratch=[pltpu.VMEM((tm, D), F32)],
        rider=rider, edges=_edges_1d(nt))


def _wgrad_gate_up(a, dg, du, rider=None):
    S, K = a.shape
    n = dg.shape[-1]
    ts = min(WGRAD_TOKENS, S)

    def body(a_ref, g_ref, u_ref, o_ref):
        @pl.when(pl.program_id(1) == 0)
        def _():
            o_ref[...] = jnp.zeros_like(o_ref)
        av = a_ref[...]
        o_ref[0:K, :] += _tn(av, g_ref[...])
        o_ref[K:2 * K, :] += _tn(av, u_ref[...])

    bspec = pl.BlockSpec((None, ts, n), lambda q, s: (q, s, 0))
    (out,), got = _pcall(
        body, [a, dg, du], name="wgrad_gate_up", grid=(NQ, S // ts),
        in_specs=[pl.BlockSpec((ts, K), lambda q, s: (s, 0)), bspec, bspec],
        out_specs=[pl.BlockSpec((None, 2 * K, n), lambda q, s: (q, 0, 0))],
        out_shape=[jax.ShapeDtypeStruct((NQ, 2 * K, n), F32)],
        rider=rider, edges=_edges_2d(NQ, S // ts))
    return out, got


def _wgrad_down(a, df):
    _, S, k = a.shape
    N = df.shape[1]
    ts = min(WGRAD_TOKENS, S)

    def body(a_ref, b_ref, o_ref):
        @pl.when(pl.program_id(1) == 0)
        def _():
            o_ref[...] = jnp.zeros_like(o_ref)
        o_ref[...] += _tn(a_ref[...], b_ref[...])

    return pl.pallas_call(
        body, name="wgrad_down", grid=(NQ, S // ts),
        in_specs=[pl.BlockSpec((None, ts, k), lambda q, s: (q, s, 0)), pl.BlockSpec((ts, N), lambda q, s: (s, 0))],
        out_specs=pl.BlockSpec((None, k, N), lambda q, s: (q, 0, 0)),
        out_shape=jax.ShapeDtypeStruct((NQ, k, N), F32),
        compiler_params=_cp(2),
    )(a, df)


def _wgrad_out(yab, yc, dzb):
    S, h = yab.shape
    D = dzb.shape[1]
    k = h // 2
    ts = min(WGRAD_TOKENS, S)

    def body(yab_ref, yc_ref, b_ref, o_ref):
        @pl.when(pl.program_id(0) == 0)
        def _():
            o_ref[...] = jnp.zeros_like(o_ref)
        b = b_ref[...]
        o_ref[0] += _tn(yab_ref[:, 0:k], b)
        o_ref[1] += _tn(yab_ref[:, k:h], b)
        o_ref[2] += _tn(yc_ref[:, 0:k], b)
        o_ref[3] += _tn(yc_ref[:, k:h], b)

    row = lambda w: pl.BlockSpec((ts, w), lambda s: (s, 0))
    return pl.pallas_call(
        body, name="wgrad_out", grid=(S // ts,),
        in_specs=[row(h), row(h), row(D)], out_specs=_full((NQ, k, D)),
        out_shape=jax.ShapeDtypeStruct((NQ, k, D), F32),
        compiler_params=_cp(1),
    )(yab, yc, dzb)


def _proj(xb, wc):
    S, D = xb.shape
    n = wc.shape[-1]
    n1 = D_POOL + 3 * D_CONV
    n2 = NQ * n - n1
    tm = min(1024, S)

    def body(x_ref, w_ref, p_ref, qkv_ref):
        x = x_ref[...]
        for q in range(NQ):
            r = _nn(x, w_ref[q])
            lo, hi = q * n, (q + 1) * n
            if hi <= n1:
                p_ref[:, lo:hi] = r
            elif lo >= n1:
                qkv_ref[:, lo - n1:hi - n1] = r.astype(BF)
            else:
                p_ref[:, lo:n1] = r[:, 0:n1 - lo]
                qkv_ref[:, 0:hi - n1] = r[:, n1 - lo:n].astype(BF)

    row = lambda w: pl.BlockSpec((tm, w), lambda i: (i, 0))
    return pl.pallas_call(
        body, name="mix_proj", grid=(S // tm,),
        in_specs=[row(D), _quarters(wc)],
        out_specs=[row(n1), row(n2)],
        out_shape=[jax.ShapeDtypeStruct((S, n1), F32), jax.ShapeDtypeStruct((S, n2), BF)],
        compiler_params=_cp(1),
    )(xb, wc)


def _mm_exact(a, b, name):
    def body(a_ref, b_ref, o_ref):
        o_ref[...] = jnp.dot(a_ref[...], b_ref[...].astype(F32), preferred_element_type=F32,
                             precision=lax.Precision.HIGHEST)

    return pl.pallas_call(
        body, name=name, in_specs=[_full(a.shape), _full(b.shape)], out_specs=_full((a.shape[0], b.shape[1])),
        out_shape=jax.ShapeDtypeStruct((a.shape[0], b.shape[1]), F32),
        compiler_params=pltpu.CompilerParams(vmem_limit_bytes=VMEM_LIMIT),
    )(a, b)


NB_ROWS = 2 * NA_ROWS - 1
NB_COLS = 2 * NA_COLS


def _bias_constants():
    c = np.arange(GRID_W)
    col_start = np.clip(c - NA_COLS // 2, 0, GRID_W - NA_COLS)
    valid = (c[None, :] >= col_start[:, None]) & (c[None, :] < col_start[:, None] + NA_COLS)
    dc = np.clip(c[None, :] - c[:, None], -(NA_COLS - 1), NA_COLS - 1) + (NA_COLS - 1)
    cq, ck = np.meshgrid(c, c, indexing="ij")
    onehot = np.zeros((HG, NB_COLS, GRID_W, HG, GRID_W), np.float32)
    for h in range(HG):
        onehot[h, dc[cq, ck], ck, h, cq] = 1.0
    mask_kq = np.where(valid.T, 0.0, NEG_INF).astype(np.float32)
    mask = np.tile(mask_kq, (NB_ROWS, HG))
    return onehot.reshape(HG * NB_COLS, GRID_W * LW), mask


def _bias_table(rpb, onehot, mask):
    ngr = NA_HEADS // HG
    r = rpb.reshape(ngr, HG, NB_ROWS, NB_COLS - 1).transpose(0, 2, 1, 3)
    r = jnp.pad(r, ((0, 0), (0, 0), (0, 0), (0, 1))).reshape(ngr * NB_ROWS, HG * NB_COLS)
    t = _mm_exact(r, onehot, "bias_expand")
    return t.reshape(ngr, NB_ROWS * GRID_W, LW) + mask[None]


def _bias_grad(dt, onehot_t):
    ngr = NA_HEADS // HG
    g = _mm_exact(dt.reshape(ngr * NB_ROWS, GRID_W * LW), onehot_t, "bias_reduce")
    g = g.reshape(ngr, NB_ROWS, HG, NB_COLS)[..., :NB_COLS - 1]
    return g.transpose(0, 2, 1, 3).reshape(NA_HEADS, NB_ROWS, NB_COLS - 1)


def _attn_rows(S):
    rows = S // GRID_W
    rb = min(16, rows)
    return rows, rb


def _head_masks():
    lane = lax.broadcasted_iota(jnp.int32, (GRID_W, LW), 1)
    return [(lane >= HEAD_DIM * h) & (lane < HEAD_DIM * (h + 1)) for h in range(HG)]


def _stack_heads(x, masks):
    zero = jnp.zeros_like(x)
    return jnp.concatenate([jnp.where(m, x, zero) for m in masks], axis=0)


def _unstack_heads(x2, masks):
    out = x2[0:GRID_W]
    for h in range(1, HG):
        out = jnp.where(masks[h], x2[h * GRID_W:(h + 1) * GRID_W], out)
    return out


def _attn_step(r, rows, q, k_ref, v_ref, b_ref, masks):
    rs = jnp.clip(r - NA_ROWS // 2, 0, rows - NA_ROWS)
    s0 = rs - r + (NA_ROWS - 1)
    q2 = _stack_heads(q, masks)
    ks = pl.ds(pl.multiple_of(rs * GRID_W, GRID_W), NA_ROWS * GRID_W)
    kb = k_ref[ks, :]
    vb = v_ref[ks, :]
    bs = pl.ds(pl.multiple_of(s0 * GRID_W, GRID_W), NA_ROWS * GRID_W)
    s = _nt(kb, q2) * (HEAD_DIM ** -0.5) + b_ref[0, bs, :]
    m = jnp.max(s, axis=0, keepdims=True)
    p = jnp.exp(s - m)
    p = p / jnp.sum(p, axis=0, keepdims=True)
    return p, q2, kb, vb, ks, bs


def _attn_fwd(qkv, bias, rider=None):
    S = qkv.shape[0]
    rows, rb = _attn_rows(S)
    tq = rb * GRID_W
    ngr = NA_HEADS // HG

    def body(q_ref, k_ref, v_ref, b_ref, o_ref):
        base = pl.program_id(1) * rb
        masks = _head_masks()

        def step(i, carry):
            qs = pl.ds(pl.multiple_of(i * GRID_W, GRID_W), GRID_W)
            p, _, _, vb, _, _ = _attn_step(base + i, rows, q_ref[qs, :], k_ref, v_ref, b_ref, masks)
            o_ref[qs, :] = _unstack_heads(_tn(p.astype(BF), vb), masks).astype(BF)
            return carry

        lax.fori_loop(0, rb, step, 0, unroll=2)

    return _pcall(
        body, [qkv, qkv, qkv, bias], name="attn_fwd", grid=(ngr, rows // rb),
        in_specs=[pl.BlockSpec((tq, LW), lambda h, r: (r, h)),
                  pl.BlockSpec((S, LW), lambda h, r: (0, ngr + h)),
                  pl.BlockSpec((S, LW), lambda h, r: (0, 2 * ngr + h)),
                  pl.BlockSpec((1, bias.shape[1], LW), lambda h, r: (h, 0, 0))],
        out_specs=[pl.BlockSpec((tq, LW), lambda h, r: (r, h))],
        out_shape=[jax.ShapeDtypeStruct((S, D_NA), BF)],
        rider=rider, edges=_edges_2d(ngr, rows // rb))


def _attn_bwd(qkv, bias, dycat, rider=None):
    S = qkv.shape[0]
    rows, rb = _attn_rows(S)
    tq = rb * GRID_W
    ngr = NA_HEADS // HG
    scale = HEAD_DIM ** -0.5

    def body(q_ref, k_ref, v_ref, b_ref, do_ref, dq_ref, dk_ref, dv_ref, db_ref, dka_ref, dva_ref):
        base = pl.program_id(1) * rb
        last = pl.program_id(1) == pl.num_programs(1) - 1
        masks = _head_masks()

        @pl.when(pl.program_id(1) == 0)
        def _():
            dka_ref[...] = jnp.zeros_like(dka_ref)
            dva_ref[...] = jnp.zeros_like(dva_ref)
            db_ref[...] = jnp.zeros_like(db_ref)

        def step(i, carry):
            qs = pl.ds(pl.multiple_of(i * GRID_W, GRID_W), GRID_W)
            p, q2, kb, vb, ks, bs = _attn_step(base + i, rows, q_ref[qs, :], k_ref, v_ref, b_ref, masks)
            do2 = _stack_heads(do_ref[qs, :].astype(BF), masks)
            dp = _nt(vb, do2)
            ds = p * (dp - jnp.sum(p * dp, axis=0, keepdims=True))
            db_ref[0, bs, :] += ds
            dsb = ds.astype(BF)
            dq_ref[qs, :] = _unstack_heads(_tn(dsb, kb) * scale, masks).astype(BF)
            dka_ref[ks, :] += _nn(dsb, q2) * scale
            dva_ref[ks, :] += _nn(p.astype(BF), do2)
            return carry

        lax.fori_loop(0, rb, step, 0, unroll=2)

        @pl.when(last)
        def _():
            dk_ref[...] = dka_ref[...].astype(BF)
            dv_ref[...] = dva_ref[...].astype(BF)

    nb = bias.shape[1]
    once = dict(pipeline_mode=pl.Buffered(1))
    nd = D_NA // LW
    return _pcall(
        body, [qkv, qkv, qkv, bias, dycat], name="attn_bwd", grid=(ngr, rows // rb),
        in_specs=[pl.BlockSpec((tq, LW), lambda h, r: (r, h)),
                  pl.BlockSpec((S, LW), lambda h, r: (0, ngr + h), **once),
                  pl.BlockSpec((S, LW), lambda h, r: (0, 2 * ngr + h), **once),
                  pl.BlockSpec((1, nb, LW), lambda h, r: (h, 0, 0)),
                  pl.BlockSpec((tq, LW), lambda h, r: (r, nd + h))],
        out_specs=[pl.BlockSpec((tq, LW), lambda h, r: (r, h)),
                   pl.BlockSpec((S, LW), lambda h, r: (0, h)),
                   pl.BlockSpec((S, LW), lambda h, r: (0, h)),
                   pl.BlockSpec((1, nb, LW), lambda h, r: (h, 0, 0))],
        out_shape=[jax.ShapeDtypeStruct((S, D_NA), BF)] * 3 + [jax.ShapeDtypeStruct((ngr, nb, LW), F32)],
        scratch=[pltpu.VMEM((S, LW), F32), pltpu.VMEM((S, LW), F32)],
        rider=rider, edges=_edges_2d(ngr, rows // rb))


def _halo_specs(tm, width, S):
    hb = tm // HALO
    last = S // HALO - 1
    return [pl.BlockSpec((tm, width), lambda i: (i, 0)),
            pl.BlockSpec((HALO, width), lambda i: (jnp.maximum(i * hb - 1, 0), 0)),
            pl.BlockSpec((HALO, width), lambda i: (jnp.minimum((i + 1) * hb, last), 0))]


def _with_halo(cur_ref, prev_ref, next_ref, i, nt):
    prev = jnp.where(i > 0, prev_ref[...], 0.0)
    nxt = jnp.where(i < nt - 1, next_ref[...], 0.0)
    return jnp.concatenate([prev, cur_ref[...], nxt], axis=0)


def _shift(a, k):
    n = a.shape[0]
    return pltpu.roll(a, k % n, 0)


def _pool_lanes(n):
    lane = lax.broadcasted_iota(jnp.int32, (n, D_POOL), 1)
    group = D_POOL // len(POOL_WINDOWS)
    return [lane < group * (j + 1) for j in range(len(POOL_WINDOWS) - 1)]


def _by_window(lanes, vals):
    return jnp.where(lanes[0], vals[0], jnp.where(lanes[1], vals[1], jnp.where(lanes[2], vals[2], vals[3])))


def _pool_count(lanes, t, S):
    back = _by_window(lanes, tuple(w // 2 for w in POOL_WINDOWS))
    lo = jnp.maximum(t - back, 0)
    hi = jnp.minimum(t + back, S)
    return jnp.maximum(hi - lo, 1).astype(F32)


def _pool_p(u, lanes, cnt):
    a = u + _shift(u, 1)
    b = _shift(a, 1) + _shift(a, -1)
    c = _shift(b, 2) + _shift(b, -2)
    d = _shift(c, 4) + _shift(c, -4)
    return _by_window(lanes, (a, b, c, d)) / cnt - u


def _mixab_fwd(pabc, wblk, vec):
    S = pabc.shape[0]
    tm = min(512, S)
    nt = S // tm
    n = tm + 2 * HALO
    tile = slice(HALO, HALO + tm)

    def body(cur_ref, prev_ref, next_ref, w_ref, vec_ref, o_ref):
        i = pl.program_id(0)
        ext = _with_halo(cur_ref, prev_ref, next_ref, i, nt)
        lanes = _pool_lanes(n)
        t = i * tm - HALO + lax.broadcasted_iota(jnp.int32, (n, D_POOL), 0)
        p = _pool_p(ext[:, 0:D_POOL], lanes, _pool_count(lanes, t, S))[tile]
        o_ref[:, 0:D_POOL] = (_nn(p.astype(BF), w_ref[...]) * vec_ref[0:1, :]).astype(BF)
        zc = ext[:, 512:768] * ext[:, 768:1024]
        conv = vec_ref[1:2, :] * _shift(zc, 1) + vec_ref[2:3, :] * zc + vec_ref[3:4, :] * _shift(zc, -1)
        o_ref[:, D_POOL:D_POOL + D_CONV] = (ext[tile, 256:512] * conv[tile]).astype(BF)

    return pl.pallas_call(
        body, name="mixab_fwd", grid=(nt,),
        in_specs=_halo_specs(tm, 1024, S) + [_full((D_POOL, D_POOL)), _full((8, D_POOL))],
        out_specs=pl.BlockSpec((tm, D_POOL + D_CONV), lambda i: (i, 0)),
        out_shape=jax.ShapeDtypeStruct((S, D_POOL + D_CONV), BF),
        compiler_params=_cp(1),
    )(pabc, pabc, pabc, wblk, vec)


def _mixab_bwd(pabc, dycat, wblk, vec):
    S = pabc.shape[0]
    tm = min(512, S)
    nt = S // tm
    n = tm + 2 * HALO
    tile = slice(HALO, HALO + tm)

    def body(cur_ref, prev_ref, next_ref, dcur_ref, dprev_ref, dnext_ref, w_ref, vec_ref, o_ref, dw_ref, dvec_ref):
        i = pl.program_id(0)

        @pl.when(i == 0)
        def _():
            dw_ref[...] = jnp.zeros_like(dw_ref)
            dvec_ref[...] = jnp.zeros_like(dvec_ref)

        ext = _with_halo(cur_ref, prev_ref, next_ref, i, nt)
        dext = _with_halo(dcur_ref, dprev_ref, dnext_ref, i, nt)
        lanes = _pool_lanes(n)
        t = i * tm - HALO + lax.broadcasted_iota(jnp.int32, (n, D_POOL), 0)
        cnt = _pool_count(lanes, t, S)
        w = w_ref[...]
        scale = vec_ref[0:1, :]
        pb = _pool_p(ext[:, 0:D_POOL], lanes, cnt)[tile].astype(BF)
        dya = dext[:, 0:D_POOL]
        dvec_ref[0:1, :] += jnp.sum(dya[tile] * _nn(pb, w), axis=0, keepdims=True)
        dqb = (dya * scale).astype(BF)
        dw_ref[...] += _tn(pb, dqb[tile])
        dp = _nt(dqb, w)
        r = dp / cnt
        a = r + _shift(r, -1)
        b = _shift(a, 1) + _shift(a, -1)
        c = _shift(b, 2) + _shift(b, -2)
        d = _shift(c, 4) + _shift(c, -4)
        o_ref[:, 0:256] = (_by_window(lanes, (a, b, c, d)) - dp)[tile].astype(BF)
        gb, gc, hh = ext[:, 256:512], ext[:, 512:768], ext[:, 768:1024]
        zc = gc * hh
        zm, zp = _shift(zc, 1), _shift(zc, -1)
        w0, w1, w2 = vec_ref[1:2, :], vec_ref[2:3, :], vec_ref[3:4, :]
        dyb = dext[:, D_POOL:D_POOL + D_CONV]
        dconv = dyb * gb
        o_ref[:, 256:512] = (dyb * (w0 * zm + w1 * zc + w2 * zp))[tile].astype(BF)
        dzc = w0 * _shift(dconv, -1) + w1 * dconv + w2 * _shift(dconv, 1)
        o_ref[:, 512:768] = (dzc * hh)[tile].astype(BF)
        o_ref[:, 768:1024] = (dzc * gc)[tile].astype(BF)
        dct = dconv[tile]
        dvec_ref[1:2, :] += jnp.sum(dct * zm[tile], axis=0, keepdims=True)
        dvec_ref[2:3, :] += jnp.sum(dct * zc[tile], axis=0, keepdims=True)
        dvec_ref[3:4, :] += jnp.sum(dct * zp[tile], axis=0, keepdims=True)

    return pl.pallas_call(
        body, name="mixab_bwd", grid=(nt,),
        in_specs=_halo_specs(tm, 1024, S) + _halo_specs(tm, 512, S) + [_full((D_POOL, D_POOL)), _full((8, D_POOL))],
        out_specs=[pl.BlockSpec((tm, 1024), lambda i: (i, 0)), _full((D_POOL, D_POOL)), _full((8, D_POOL))],
        out_shape=[jax.ShapeDtypeStruct((S, 1024), BF), jax.ShapeDtypeStruct((D_POOL, D_POOL), F32),
                   jax.ShapeDtypeStruct((8, D_POOL), F32)],
        compiler_params=_cp(1),
    )(pabc, pabc, pabc, dycat, dycat, dycat, wblk, vec)


def _mixout_fwd(yab, yc, x, wo, lg, lb):
    S, D = x.shape
    tm = min(512, S)
    h = yab.shape[1]
    k = h // 2

    def body(yab_ref, yc_ref, x_ref, w_ref, lg_ref, lb_ref, xo_ref, xb_ref, z_ref):
        y = (_nn(yab_ref[:, 0:k], w_ref[0]) + _nn(yab_ref[:, k:h], w_ref[1])
             + _nn(yc_ref[:, 0:k], w_ref[2]) + _nn(yc_ref[:, k:h], w_ref[3]))
        z = ALPHA * x_ref[...] + y
        xo = _ln_fwd(z, lg_ref[...], lb_ref[...])
        z_ref[...] = z
        xo_ref[...] = xo
        xb_ref[...] = xo.astype(BF)

    row = lambda w: pl.BlockSpec((tm, w), lambda i: (i, 0))
    return pl.pallas_call(
        body, name="mixout_fwd", grid=(S // tm,),
        in_specs=[row(h), row(h), row(D), _quarters(wo), _full((1, D)), _full((1, D))],
        out_specs=[row(D), row(D), row(D)],
        out_shape=[jax.ShapeDtypeStruct((S, D), F32), jax.ShapeDtypeStruct((S, D), BF),
                   jax.ShapeDtypeStruct((S, D), F32)],
        compiler_params=_cp(1),
    )(yab, yc, x, wo, lg, lb)


def _mixout_bwd(dxo, z, wo, lg, rider=None):
    S, D = dxo.shape
    k = wo.shape[-2]
    tm = min(512, S)
    nt = S // tm

    def body(dxo0_ref, z0_ref, dxo1_ref, z1_ref, w_ref, lg_ref, dres_ref, dzb_ref, dy_ref, ln_ref, dz_ref):
        i = pl.program_id(0)

        @pl.when(i == 0)
        def _():
            dy0 = dxo0_ref[...]
            dz0, xhat0 = _ln_bwd(dy0, z0_ref[...], lg_ref[...])
            dz_ref[...] = dz0
            ln_ref[...] = jnp.zeros_like(ln_ref)
            ln_ref[0:1, :] += jnp.sum(dy0 * xhat0, axis=0, keepdims=True)
            ln_ref[1:2, :] += jnp.sum(dy0, axis=0, keepdims=True)

        dz = dz_ref[...]
        dzb = dz.astype(BF)
        dres_ref[...] = ALPHA * dz
        dzb_ref[...] = dzb
        for q in range(NQ):
            dy_ref[:, q * k:(q + 1) * k] = _nt(dzb, w_ref[q])
        dy1 = dxo1_ref[...]
        dz1, xhat1 = _ln_bwd(dy1, z1_ref[...], lg_ref[...])
        real = (i < nt - 1).astype(F32)
        ln_ref[0:1, :] += real * jnp.sum(dy1 * xhat1, axis=0, keepdims=True)
        ln_ref[1:2, :] += real * jnp.sum(dy1, axis=0, keepdims=True)
        dz_ref[...] = dz1

    row = lambda w: pl.BlockSpec((tm, w), lambda i: (i, 0))
    first = pl.BlockSpec((tm, D), lambda i: (0, 0))
    nxt = pl.BlockSpec((tm, D), lambda i: (jnp.minimum(i + 1, nt - 1), 0))
    return _pcall(
        body, [dxo, z, dxo, z, wo, lg], name="mixout_bwd", grid=(nt,),
        in_specs=[first, first, nxt, nxt, _quarters(wo), _full((1, D))],
        out_specs=[row(D), row(D), row(NQ * k), _full((8, D))],
        out_shape=[jax.ShapeDtypeStruct((S, D), F32), jax.ShapeDtypeStruct((S, D), BF),
                   jax.ShapeDtypeStruct((S, NQ * k), F32), jax.ShapeDtypeStruct((8, D), F32)],
        scratch=[pltpu.VMEM((tm, D), F32)],
        rider=rider, edges=_edges_1d(nt))


def _take_cols(refs, lo, hi):
    parts, off = [], 0
    for r in refs:
        w = r.shape[1]
        a, b = max(lo, off), min(hi, off + w)
        if a < b:
            parts.append(r[:, a - off:b - off])
        off += w
    return parts[0] if len(parts) == 1 else jnp.concatenate(parts, axis=1)


def _proj_bwd(dres, dparts, wc, rider=None):
    S, D = dres.shape
    n = wc.shape[-1]
    tm = min(1024, S)
    np_ = len(dparts)

    def body(*refs):
        dres_ref, d_refs, w_ref, dx_ref = refs[0], refs[1:1 + np_], refs[1 + np_], refs[2 + np_]
        acc = dres_ref[...]
        for q in range(NQ):
            acc = acc + _nt(_take_cols(d_refs, q * n, (q + 1) * n), w_ref[q])
        dx_ref[...] = acc

    row = lambda w: pl.BlockSpec((tm, w), lambda i: (i, 0))
    return _pcall(
        body, [dres, *dparts, wc], name="mix_proj_bwd", grid=(S // tm,),
        in_specs=[row(D)] + [row(d.shape[1]) for d in dparts] + [_quarters(wc)],
        out_specs=[row(D)],
        out_shape=[jax.ShapeDtypeStruct((S, D), F32)],
        rider=rider, edges=_edges_1d(S // tm))


def _wgrad_in(a, dparts, n, rider=None):
    S, K = a.shape
    ts = min(WGRAD_TOKENS // 2, S)
    np_ = len(dparts)

    def body(*refs):
        a_ref, d_refs, o_ref = refs[0], refs[1:1 + np_], refs[1 + np_]

        @pl.when(pl.program_id(0) == 0)
        def _():
            o_ref[...] = jnp.zeros_like(o_ref)
        av = a_ref[...]
        for q in range(NQ):
            o_ref[q] += _tn(av, _take_cols(d_refs, q * n, (q + 1) * n))

    row = lambda w: pl.BlockSpec((ts, w), lambda s: (s, 0))
    (out,), got = _pcall(
        body, [a, *dparts], name="wgrad_in", grid=(S // ts,),
        in_specs=[row(K)] + [row(d.shape[1]) for d in dparts], out_specs=[_full((NQ, K, n))],
        out_shape=[jax.ShapeDtypeStruct((NQ, K, n), F32)],
        rider=rider, edges=_edges_1d(S // ts))
    return out, got


def _loss_head(y, target):
    S, D = y.shape
    tm = min(512, S)

    def body(y_ref, t_ref, l_ref, dy_ref):
        @pl.when(pl.program_id(0) == 0)
        def _():
            l_ref[...] = jnp.zeros_like(l_ref)
        e = y_ref[...] - t_ref[...]
        dy_ref[...] = e * (1.0 / D)
        part = jnp.sum(jnp.sum(e * e, axis=1, keepdims=True) * (1.0 / D), axis=0, keepdims=True)
        l_ref[...] += 0.5 * part

    row = pl.BlockSpec((tm, D), lambda i: (i, 0))
    return pl.pallas_call(
        body, name="loss_head", grid=(S // tm,),
        in_specs=[row, row], out_specs=[_full((8, 128)), row],
        out_shape=[jax.ShapeDtypeStruct((8, 128), F32), jax.ShapeDtypeStruct((S, D), F32)],
        compiler_params=_cp(1),
    )(y, target)


def _adamw_update(w, g, m, v):
    mn = ADAM_B1 * m + (1.0 - ADAM_B1) * g
    vn = ADAM_B2 * v + (1.0 - ADAM_B2) * (g * g)
    m_hat = mn / (1.0 - ADAM_B1 ** ADAM_STEP)
    v_hat = vn / (1.0 - ADAM_B2 ** ADAM_STEP)
    return -ADAM_LR * (m_hat / (jnp.sqrt(v_hat) + ADAM_EPS) + ADAM_WD * w), mn, vn


def _adamw(w, g, m, v):
    shape = w.shape
    cols = shape[-1]
    rows = int(np.prod(shape[:-1]))
    w2, g2, m2, v2 = (a.reshape(rows, cols) for a in (w, g, m, v))
    tr = rows
    for cand in (512, 352, 256):
        if rows > cand and rows % cand == 0:
            tr = cand
            break

    def body(w_ref, g_ref, m_ref, v_ref, d_ref, mo_ref, vo_ref):
        d_ref[...], mo_ref[...], vo_ref[...] = _adamw_update(w_ref[...], g_ref[...], m_ref[...], v_ref[...])

    spec = pl.BlockSpec((tr, cols), lambda i: (i, 0))
    outs = pl.pallas_call(
        body, name=f"adamw_{rows}x{cols}", grid=(rows // tr,),
        in_specs=[spec] * 4, out_specs=[spec] * 3,
        out_shape=[jax.ShapeDtypeStruct((rows, cols), F32)] * 3,
        compiler_params=_cp(1),
    )(w2, g2, m2, v2)
    return tuple(o.reshape(shape) for o in outs)


ADAMW_GROUP_STEPS = 16


def _adamw_group(items, name, rider=None):
    n = len(items)
    all_rows = [int(np.prod(w.shape[:-1])) for w, _, _, _ in items]
    steps = next(s for s in (ADAMW_GROUP_STEPS, 8, 4, 2, 1) if all(r % (8 * s) == 0 for r in all_rows))
    flat, specs, shapes = [], [], []
    for (w, g, m, v), rows in zip(items, all_rows):
        cols = w.shape[-1]
        tr = rows // steps
        flat += [a.reshape(rows, cols) for a in (w, g, m, v)]
        specs.append(pl.BlockSpec((tr, cols), lambda i: (i, 0)))
        shapes.append((w.shape, rows, cols))

    def body(*refs):
        ins, outs = refs[:4 * n], refs[4 * n:]
        for j in range(n):
            w_ref, g_ref, m_ref, v_ref = ins[4 * j:4 * j + 4]
            outs[3 * j][...], outs[3 * j + 1][...], outs[3 * j + 2][...] = _adamw_update(
                w_ref[...], g_ref[...], m_ref[...], v_ref[...])

    outs, got = _pcall(
        body, flat, name=name, grid=(steps,),
        in_specs=[s for s in specs for _ in range(4)], out_specs=[s for s in specs for _ in range(3)],
        out_shape=[jax.ShapeDtypeStruct((rows, cols), F32) for _, rows, cols in shapes for _ in range(3)],
        rider=rider, edges=_edges_1d(steps))
    return [tuple(o.reshape(shapes[j][0]) for o in outs[3 * j:3 * j + 3]) for j in range(n)], got


def _half_tile(h):
    return h if h <= 512 else 512


def _add_chip(g, recv):
    _, R, C = g.shape
    h = R // 2
    tr = _half_tile(h)
    nb = h // tr

    def body(a_ref, b_ref, o_ref, ob_ref):
        s = a_ref[...] + b_ref[...]
        ob_ref[...] = s.astype(BF)

        @pl.when(pl.program_id(1) == 2 * lax.axis_index("x") + lax.axis_index("y"))
        def _():
            o_ref[...] = s[0]

    half = pl.BlockSpec((1, tr, C), lambda i, q: (q, i, 0))
    mine = pl.BlockSpec((1, tr, C), lambda i, q: (q, lax.axis_index("c") * nb + i, 0))
    return pl.pallas_call(
        body, name=f"rs_add_chip_{R}x{C}", grid=(nb, NQ), in_specs=[mine, half],
        out_specs=[pl.BlockSpec((tr, C), lambda i, q: (i, 0)), half],
        out_shape=[jax.ShapeDtypeStruct((h, C), F32), jax.ShapeDtypeStruct((NQ, h, C), BF)],
        compiler_params=_cp(2),
    )(g, recv)


def _add_final(chip, recv):
    h, C = chip.shape
    tr = _half_tile(h)
    nb = h // tr

    def body(a_ref, b_ref, o_ref):
        s = a_ref[...]
        for j in range(3):
            s = s + b_ref[j].astype(F32)
        o_ref[...] = s

    return pl.pallas_call(
        body, name=f"rs_add_final_{h}x{C}", grid=(nb,),
        in_specs=[pl.BlockSpec((tr, C), lambda i: (i, 0)), pl.BlockSpec((3, tr, C), lambda i: (0, i, 0))],
        out_specs=pl.BlockSpec((tr, C), lambda i: (lax.axis_index("c") * nb + i, 0)),
        out_shape=jax.ShapeDtypeStruct((2 * h, C), F32),
        compiler_params=_cp(1),
    )(chip, recv)


COMM = pltpu.CompilerParams(has_side_effects=True)


def _place():
    x, y, c = lax.axis_index("x"), lax.axis_index("y"), lax.axis_index("c")
    chips = [(1 - x, y), (x, 1 - y), (1 - x, 1 - y)]
    return x, y, c, chips


def _half0(ref, c):
    n = ref.shape[0] // 2
    return ref.at[pl.ds(c * n, n)]


def _gather_ici(shards):
    n = len(shards)

    def copies(r_in, r_out, ssem, rsem, base):
        x, y, c, chips = _place()
        q = 2 * x + y
        return [pltpu.make_async_remote_copy(
            src_ref=_half0(r_in[i], c), dst_ref=_half0(r_out[i].at[q], c), send_sem=ssem.at[base + 3 * i + j],
            recv_sem=rsem.at[base + 3 * i + j], device_id=(*chip, c), device_id_type=MESH)
            for i in range(n) for j, chip in enumerate(chips)]

    return _Rider("ici", shards, [jax.ShapeDtypeStruct((NQ,) + s.shape, BF) for s in shards], {}, 3 * n, copies)


def _gather_d2d(bufs):
    n = len(bufs)

    def copies(r_in, r_out, ssem, rsem, base):
        x, y, c, chips = _place()
        return [pltpu.make_async_remote_copy(
            src_ref=_half0(r_in[i].at[2 * cx + cy], c), dst_ref=_half0(r_out[i].at[2 * cx + cy], c),
            send_sem=ssem.at[base + 3 * i + j], recv_sem=rsem.at[base + 3 * i + j], device_id=(x, y, 1 - c),
            device_id_type=MESH) for i in range(n) for j, (cx, cy) in enumerate(chips)]

    return _Rider("d2d", bufs, [jax.ShapeDtypeStruct(b.shape, b.dtype) for b in bufs], {i: i for i in range(n)},
                  3 * n, copies)


def _gather_small(small):
    sr = small.shape[0]

    def body(s_ref, o_ref, send_sems, recv_sems):
        x, y, c, chips = _place()
        o_ref[2 * x + y] = s_ref[...]
        cps = [pltpu.make_async_remote_copy(
            src_ref=s_ref, dst_ref=o_ref.at[2 * x + y], send_sem=send_sems.at[j], recv_sem=recv_sems.at[j],
            device_id=(*chip, c), device_id_type=MESH) for j, chip in enumerate(chips)]
        for cp in cps:
            cp.start()
        for j, (cx, cy) in enumerate(chips):
            pltpu.make_async_remote_copy(
                src_ref=s_ref, dst_ref=o_ref.at[2 * cx + cy], send_sem=send_sems.at[j], recv_sem=recv_sems.at[j],
                device_id=(cx, cy, c), device_id_type=MESH).wait_recv()
        for cp in cps:
            cp.wait_send()

    vm = pl.BlockSpec(memory_space=pltpu.VMEM)
    return pl.pallas_call(
        body, name="gather_small", in_specs=[vm], out_specs=vm,
        out_shape=jax.ShapeDtypeStruct((NQ, sr, 128), F32),
        scratch_shapes=[pltpu.SemaphoreType.DMA((3,)), pltpu.SemaphoreType.DMA((3,))],
        compiler_params=COMM,
    )(small)


def _swap_halves(gs):
    n = len(gs)

    def copies(r_in, r_out, ssem, rsem, base):
        x, y, c, _ = _place()
        cps = []
        for i in range(n):
            h = r_in[i].shape[1] // 2
            cps.append(pltpu.make_async_remote_copy(
                src_ref=r_in[i].at[:, pl.ds((1 - c) * h, h), :], dst_ref=r_out[i], send_sem=ssem.at[base + i],
                recv_sem=rsem.at[base + i], device_id=(x, y, 1 - c), device_id_type=MESH))
        return cps

    return _Rider("swap", gs, [jax.ShapeDtypeStruct((NQ, g.shape[1] // 2, g.shape[2]), F32) for g in gs], {}, n,
                  copies)


def _scatter_chips(chips_b):
    n = len(chips_b)

    def copies(r_in, r_out, ssem, rsem, base):
        x, y, c, chips = _place()
        return [pltpu.make_async_remote_copy(
            src_ref=r_in[i].at[2 * cx + cy], dst_ref=r_out[i].at[j], send_sem=ssem.at[base + 3 * i + j],
            recv_sem=rsem.at[base + 3 * i + j], device_id=(cx, cy, c), device_id_type=MESH)
            for i in range(n) for j, (cx, cy) in enumerate(chips)]

    return _Rider("scatter", chips_b, [jax.ShapeDtypeStruct((3,) + s.shape[1:], BF) for s in chips_b], {}, 3 * n,
                  copies)


def _run_alone(rider, name):
    ni, no = len(rider.ins), len(rider.outs)

    def body(*refs):
        cps = rider.copies(refs[:ni], refs[ni:ni + no], refs[ni + no], refs[ni + no + 1], 0)
        for cp in cps:
            cp.start()
        for cp in cps:
            cp.wait()

    return list(pl.pallas_call(
        body, name=name, in_specs=[ANY] * ni, out_specs=[ANY] * no, out_shape=rider.outs,
        input_output_aliases=dict(rider.aliases),
        scratch_shapes=[pltpu.SemaphoreType.DMA((rider.n,)), pltpu.SemaphoreType.DMA((rider.n,))],
        compiler_params=COMM,
    )(*rider.ins))


def _join_halves(fs):
    n = len(fs)

    def copies(r_in, r_out, ssem, rsem, base):
        x, y, c, _ = _place()
        return [pltpu.make_async_remote_copy(
            src_ref=_half0(r_in[i], c), dst_ref=_half0(r_out[i], c), send_sem=ssem.at[base + i],
            recv_sem=rsem.at[base + i], device_id=(x, y, 1 - c), device_id_type=MESH) for i in range(n)]

    return _Rider("join", fs, [jax.ShapeDtypeStruct(f.shape, F32) for f in fs], {i: i for i in range(n)}, n, copies)


def _allreduce_small(v):
    r, W = v.shape

    def body(v_ref, o_ref, land_ref, send_sems, recv_sems):
        x, y, c, _ = _place()
        me = 4 * x + 2 * y + c
        cps = []
        for m in range(1, 8):
            to = (x ^ (m >> 2), y ^ ((m >> 1) & 1), c ^ (m & 1))
            cps.append(pltpu.make_async_remote_copy(
                src_ref=v_ref, dst_ref=land_ref.at[m - 1], send_sem=send_sems.at[m - 1], recv_sem=recv_sems.at[m - 1],
                device_id=to, device_id_type=MESH))
        for cp in cps:
            cp.start()
        for cp in cps:
            cp.wait()
        total = jnp.zeros((r, W), F32)
        for d in range(8):
            slot = jnp.maximum((me ^ d) - 1, 0)
            total = total + jnp.where(me == d, v_ref[...], land_ref[slot])
        o_ref[...] = total

    return pl.pallas_call(
        body, name="allreduce_small",
        in_specs=[pl.BlockSpec(memory_space=pltpu.VMEM)], out_specs=pl.BlockSpec(memory_space=pltpu.VMEM),
        out_shape=jax.ShapeDtypeStruct((r, W), F32),
        scratch_shapes=[pltpu.VMEM((7, r, W), F32), pltpu.SemaphoreType.DMA((7,)), pltpu.SemaphoreType.DMA((7,))],
        compiler_params=pltpu.CompilerParams(has_side_effects=True, vmem_limit_bytes=VMEM_LIMIT),
    )(v)


def kernel(x, ffn1_w_gate, ffn1_w_up, ffn1_w_down, ffn2_w_gate, ffn2_w_up, ffn2_w_down, w_in, pool_w, pool_scale, conv_w, rpb, w_out, ln_g, ln_b, loss_target, m_ffn1_w_gate, m_ffn1_w_up, m_ffn1_w_down, m_ffn2_w_gate, m_ffn2_w_up, m_ffn2_w_down, m_w_in, m_pool_w, m_pool_scale, m_conv_w, m_rpb, m_w_out, m_ln_g, m_ln_b, v_ffn1_w_gate, v_ffn1_w_up, v_ffn1_w_down, v_ffn2_w_gate, v_ffn2_w_up, v_ffn2_w_down, v_w_in, v_pool_w, v_pool_scale, v_conv_w, v_rpb, v_w_out, v_ln_g, v_ln_b):
    weights = dict(ffn1_w_gate=ffn1_w_gate, ffn1_w_up=ffn1_w_up, ffn1_w_down=ffn1_w_down, ffn2_w_gate=ffn2_w_gate,
                   ffn2_w_up=ffn2_w_up, ffn2_w_down=ffn2_w_down, w_in=w_in, pool_w=pool_w, pool_scale=pool_scale,
                   conv_w=conv_w, rpb=rpb, w_out=w_out, ln_g=ln_g, ln_b=ln_b)
    mom_m = dict(ffn1_w_gate=m_ffn1_w_gate, ffn1_w_up=m_ffn1_w_up, ffn1_w_down=m_ffn1_w_down, ffn2_w_gate=m_ffn2_w_gate,
                 ffn2_w_up=m_ffn2_w_up, ffn2_w_down=m_ffn2_w_down, w_in=m_w_in, pool_w=m_pool_w,
                 pool_scale=m_pool_scale, conv_w=m_conv_w, rpb=m_rpb, w_out=m_w_out, ln_g=m_ln_g, ln_b=m_ln_b)
    mom_v = dict(ffn1_w_gate=v_ffn1_w_gate, ffn1_w_up=v_ffn1_w_up, ffn1_w_down=v_ffn1_w_down, ffn2_w_gate=v_ffn2_w_gate,
                 ffn2_w_up=v_ffn2_w_up, ffn2_w_down=v_ffn2_w_down, w_in=v_w_in, pool_w=v_pool_w,
                 pool_scale=v_pool_scale, conv_w=v_conv_w, rpb=v_rpb, w_out=v_w_out, ln_g=v_ln_g, ln_b=v_ln_b)
    order = list(weights)
    L = ffn1_w_gate.shape[0]
    xi, yi, ci = lax.axis_index("x"), lax.axis_index("y"), lax.axis_index("c")
    q_me = 2 * xi + yi
    x2 = x[0]
    target = loss_target[0]
    D = x2.shape[1]
    n_in = w_in.shape[-1]

    small = jnp.concatenate([ln_g.reshape(-1), ln_b.reshape(-1), conv_w.reshape(-1)])
    n_small = small.shape[0]
    small_rows = -(-n_small // (8 * 128)) * 8
    small = jnp.pad(small, (0, small_rows * 128 - n_small)).reshape(small_rows, 128)
    small_all = _gather_small(small).reshape(NQ, small_rows * 128)[:, :n_small]
    dq4 = D // NQ
    n_ln = L * 3 * dq4
    ln_g_all = small_all[:, :n_ln].reshape(NQ, L, 3, dq4).transpose(1, 2, 0, 3).reshape(L, 3, D)
    ln_b_all = small_all[:, n_ln:2 * n_ln].reshape(NQ, L, 3, dq4).transpose(1, 2, 0, 3).reshape(L, 3, D)
    conv_all = small_all[:, 2 * n_ln:].reshape(NQ, L, 3, D_CONV // NQ).transpose(1, 2, 0, 3).reshape(L, 3, D_CONV)

    def layer_shards(l):
        return [w[l].astype(BF) for w in (ffn1_w_gate, ffn1_w_up, ffn1_w_down, w_in, w_out, ffn2_w_gate, ffn2_w_up,
                                          ffn2_w_down)]

    def own_quarter(bufs, shards):
        return [lax.dynamic_update_slice(b, s[None], (q_me,) + (0,) * s.ndim) for b, s in zip(bufs, shards)]

    shards = [layer_shards(l) for l in range(L)]
    landed = _run_alone(_gather_ici(shards[0][:3]), "gather_ici")
    weights_of = [own_quarter(_run_alone(_gather_d2d(landed), "gather_d2d"), shards[0][:3])] + [None] * (L - 1)

    onehot_np, mask_np = _bias_constants()
    onehot, onehot_t, mask = jnp.asarray(onehot_np, BF), jnp.asarray(onehot_np.T.copy(), BF), jnp.asarray(mask_np)
    ng = len(POOL_WINDOWS)
    pg = D_POOL // ng
    saved = []
    h = x2
    hb = x2.astype(BF)
    for l in range(L):
        nxt = shards[l + 1] if l + 1 < L else None
        wg1, wu1, wd1 = weights_of[l][:3]
        eye = jnp.eye(ng, dtype=F32)
        wblk = (pool_w[l][:, :, None, :] * eye[:, None, :, None]).reshape(D_POOL, D_POOL).astype(BF)
        vec = jnp.concatenate([pool_scale[l][None], conv_all[l], jnp.zeros((4, D_POOL), F32)], axis=0)
        bias = _bias_table(rpb[l], onehot, mask)
        lg = [ln_g_all[l, j][None] for j in range(3)]
        lb = [ln_b_all[l, j][None] for j in range(3)]
        if l == 0:
            (x1, x1b, z1, g1, u1), got = _ffn_fwd(h, wg1, wu1, wd1, lg[0], lb[0], rider=_gather_ici(shards[0][3:]))
            weights_of[0] += own_quarter(_run_alone(_gather_d2d(got), "gather_d2d_rest"), shards[0][3:])
            r_attn = _gather_ici(nxt[:3]) if nxt else None
        else:
            (x1, x1b, z1, g1, u1), got = _ffn_fwd(h, wg1, wu1, wd1, lg[0], lb[0],
                                                  rider=_gather_ici(nxt[:3]) if nxt else None)
            r_attn = _merge(_gather_d2d(got), _gather_ici(nxt[3:])) if nxt else None
        wc, wo, wg2, wu2, wd2 = weights_of[l][3:]
        pabc, qkv = _proj(x1b, wc)
        yab = _mixab_fwd(pabc, wblk, vec)
        (yc,), got = _attn_fwd(qkv, bias, rider=r_attn)
        xm, xmb, zm = _mixout_fwd(yab, yc, x1, wo, lg[1], lb[1])
        if l == 0:
            r_ffn2 = _merge(_gather_d2d(got), _gather_ici(nxt[3:])) if nxt else None
        else:
            r_ffn2 = _gather_d2d(got[3:]) if nxt else None
        (x3, x3b, z3, g3, u3), got2 = _ffn_fwd(xm, wg2, wu2, wd2, lg[2], lb[2], rider=r_ffn2)
        if nxt and l == 0:
            got2 = got2[:3] + _run_alone(_gather_d2d(got2[3:]), "gather_d2d_rest")
            weights_of[1] = own_quarter(got2, nxt)
        elif nxt:
            weights_of[l + 1] = own_quarter(got[:3] + got2, nxt)
        saved.append(dict(wblk=wblk, vec=vec, bias=bias, lg=lg, hb=hb, z1=z1, g1=g1, u1=u1, x1b=x1b, pabc=pabc,
                          qkv=qkv, yab=yab, yc=yc, zm=zm, xmb=xmb, z3=z3, g3=g3, u3=u3))
        h, hb = x3, x3b

    loss_tile, dh = _loss_head(h, target)
    loss = lax.psum(loss_tile[0, 0], ("x", "y", "c"))

    def add_chip(arrs, recv):
        chip = [_add_chip(g, r) for g, r in zip(arrs, recv)]
        return [cf for cf, _ in chip], [cb for _, cb in chip]

    def add_final(chip_f, from_chips):
        return _join_halves([_add_final(cf, r) for cf, r in zip(chip_f, from_chips)])

    per_layer = [[None] * 6 for _ in range(L)]
    g_small = dict(pool_w=[None] * L, pool_scale=[None] * L, conv_w=[None] * L, rpb=[None] * L, ln_g=[None] * L,
                   ln_b=[None] * L)
    ffn1_g = None
    for l in reversed(range(L)):
        sv = saved[l]
        wg1, wu1, wd1, wc, wo, wg2, wu2, wd2 = weights_of[l]
        (dxm, df, dg, du, a, ln3), got = _ffn_bwd(dh, sv["z3"], sv["g3"], sv["u3"], wg2, wu2, wd2, sv["lg"][2],
                                                  rider=_swap_halves(ffn1_g) if ffn1_g else None)
        if ffn1_g:
            ffn1_f, ffn1_b = add_chip(ffn1_g, got)
        ffn2_g = [_wgrad_gate_up(sv["xmb"], dg, du)[0], _wgrad_down(a, df)]
        (dres, dzb, dycat, ln2), got = _mixout_bwd(dxm, sv["zm"], wo, sv["lg"][1], rider=_swap_halves(ffn2_g))
        ffn2_f, ffn2_b = add_chip(ffn2_g, got)
        g_o = _wgrad_out(sv["yab"], sv["yc"], dzb)
        dpabc, dwblk, dvec = _mixab_bwd(sv["pabc"], dycat, sv["wblk"], sv["vec"])
        (dq, dk, dv, dbias), got = _attn_bwd(sv["qkv"], sv["bias"], dycat,
                                             rider=_scatter_chips(ffn1_b) if ffn1_g else None)
        dparts = [dpabc, dq, dk, dv]
        g_in, got = _wgrad_in(sv["x1b"], dparts, n_in, rider=add_final(ffn1_f, got) if ffn1_g else None)
        if ffn1_g:
            per_layer[l + 1][0:2] = got
        mix_g = [g_in, g_o]
        (dx1,), got = _proj_bwd(dres, dparts, wc, rider=_swap_halves(mix_g))
        mix_f, mix_b = add_chip(mix_g, got)
        (dh, df, dg, du, a, ln1), got = _ffn_bwd(dx1, sv["z1"], sv["g1"], sv["u1"], wg1, wu1, wd1, sv["lg"][0],
                                                 rider=_scatter_chips(ffn2_b + mix_b))
        g_gu, per_layer[l][2:6] = _wgrad_gate_up(sv["hb"], dg, du, rider=add_final(ffn2_f + mix_f, got))
        ffn1_g = [g_gu, _wgrad_down(a, df)]
        g_small["pool_w"][l] = jnp.stack([dwblk[gi * pg:(gi + 1) * pg, gi * pg:(gi + 1) * pg] for gi in range(ng)])
        g_small["pool_scale"][l] = dvec[0]
        g_small["conv_w"][l] = dvec[1:4]
        g_small["rpb"][l] = _bias_grad(dbias, onehot_t)
        g_small["ln_g"][l] = jnp.stack([ln1[0], ln2[0], ln3[0]])
        g_small["ln_b"][l] = jnp.stack([ln1[1], ln2[1], ln3[1]])
    grad_x = dh[None]

    def stacked(i, rows=None):
        parts = [per_layer[l][i] if rows is None else per_layer[l][i][rows[0]:rows[1]] for l in range(L)]
        return jnp.stack(parts)

    grads = dict(ffn2_w_gate=stacked(2, (0, D)), ffn2_w_up=stacked(2, (D, 2 * D)), ffn2_w_down=stacked(3),
                 w_in=stacked(4), w_out=stacked(5))
    delta, new_m, new_v = {}, {}, {}

    def adamw_group(names, tag, rider):
        res, got = _adamw_group([(weights[n], grads[n], mom_m[n], mom_v[n]) for n in names], tag, rider)
        for n, r in zip(names, res):
            delta[n], new_m[n], new_v[n] = r
        return got

    ffn1_f, ffn1_b = add_chip(ffn1_g, adamw_group(("w_in", "w_out"), "adamw_mix", _swap_halves(ffn1_g)))
    got = adamw_group(("ffn2_w_gate", "ffn2_w_up", "ffn2_w_down"), "adamw_ffn2", _scatter_chips(ffn1_b))
    per_layer[0][0:2] = _run_alone(add_final(ffn1_f, got), "rs_join_halves")
    grads.update(ffn1_w_gate=stacked(0, (0, D)), ffn1_w_up=stacked(0, (D, 2 * D)), ffn1_w_down=stacked(1))

    small_names = ("pool_w", "pool_scale", "conv_w", "rpb", "ln_g", "ln_b")
    small_full = {n: jnp.stack(g_small[n]) for n in small_names}
    vflat = jnp.concatenate([small_full[n].reshape(-1) for n in small_names])
    n_v = vflat.shape[0]
    v_cols = 1024
    v_rows = -(-n_v // (8 * v_cols)) * 8
    vsum = _allreduce_small(jnp.pad(vflat, (0, v_rows * v_cols - n_v)).reshape(v_rows, v_cols)).reshape(-1)
    off = 0
    for n in small_names:
        sz = int(np.prod(small_full[n].shape))
        grads[n] = vsum[off:off + sz].reshape(small_full[n].shape)
        off += sz
    for n in ("conv_w", "ln_g", "ln_b"):
        width = weights[n].shape[-1]
        grads[n] = lax.dynamic_slice_in_dim(grads[n], q_me * width, width, axis=2)

    for n in order:
        if n not in delta:
            delta[n], new_m[n], new_v[n] = _adamw(weights[n], grads[n], mom_m[n], mom_v[n])
    return (loss, grad_x, *[grads[n] for n in order], *[delta[n] for n in order], *[new_m[n] for n in order],
            *[new_v[n] for n in order])
```

```python
import numpy as np
import jax
import jax.numpy as jnp
from jax import lax
from jax.experimental import pallas as pl
from jax.experimental.pallas import tpu as pltpu

BF = jnp.bfloat16
F32 = jnp.float32
MESH = pl.DeviceIdType.MESH

DEPTH = 4
ALPHA = (2.0 * DEPTH) ** 0.25
LN_EPS = 1e-5
NEG_INF = -1e30
GRID_W = 64
NA_ROWS = 8
NA_COLS = 16
NA_HEADS = 8
HEAD_DIM = 64
D_POOL = 256
D_CONV = 256
D_NA = 512
HG = 4
LW = HG * HEAD_DIM
POOL_WINDOWS = (2, 4, 8, 16)
HALO = 8
ADAM_LR, ADAM_B1, ADAM_B2, ADAM_EPS, ADAM_WD, ADAM_STEP = 0.001, 0.9, 0.999, 1e-08, 0.01, 10
VMEM_LIMIT = 56 * 1024 * 1024
NQ = 4
WGRAD_TOKENS = 2048


def _cp(n_axes):
    return pltpu.CompilerParams(dimension_semantics=("arbitrary",) * n_axes, vmem_limit_bytes=VMEM_LIMIT)


def _full(shape):
    nd = len(shape)
    return pl.BlockSpec(shape, lambda *_: (0,) * nd)


def _quarters(arr):
    return pl.BlockSpec(arr.shape, lambda *_: (0, 0, 0), pipeline_mode=pl.Buffered(1))


ANY = pl.BlockSpec(memory_space=pl.ANY)


class _Rider:
    def __init__(self, tag, ins, outs, aliases, n, copies):
        self.tag, self.ins, self.outs, self.aliases, self.n, self.copies = tag, list(ins), list(outs), aliases, n, copies


def _merge(*riders):
    ins, outs, aliases, spans, n = [], [], {}, [], 0
    for r in riders:
        spans.append((len(ins), len(outs), n))
        aliases.update({len(ins) + i: len(outs) + j for i, j in r.aliases.items()})
        ins += r.ins
        outs += r.outs
        n += r.n

    def copies(r_in, r_out, ssem, rsem, base):
        cps = []
        for r, (i0, o0, s0) in zip(riders, spans):
            cps += r.copies(r_in[i0:i0 + len(r.ins)], r_out[o0:o0 + len(r.outs)], ssem, rsem, base + s0)
        return cps

    return _Rider("_".join(r.tag for r in riders), ins, outs, aliases, n, copies)


def _pcall(body, operands, *, name, grid, in_specs, out_specs, out_shape, scratch=(), rider=None, edges=None):
    n_in, n_out = len(in_specs), len(out_specs)
    params = dict(dimension_semantics=("arbitrary",) * len(grid), vmem_limit_bytes=VMEM_LIMIT)
    if rider is None:
        outs = pl.pallas_call(body, name=name, grid=grid, in_specs=in_specs, out_specs=out_specs, out_shape=out_shape,
                              scratch_shapes=list(scratch), compiler_params=pltpu.CompilerParams(**params))(*operands)
        return list(outs), []
    ni, no = len(rider.ins), len(rider.outs)
    first, last = edges

    def riding(*refs):
        rest = refs[n_in + ni + n_out + no:]
        cps = rider.copies(refs[n_in:n_in + ni], refs[n_in + ni + n_out:n_in + ni + n_out + no], rest[-2], rest[-1], 0)

        @pl.when(first())
        def _():
            for cp in cps:
                cp.start()

        body(*refs[:n_in], *refs[n_in + ni:n_in + ni + n_out], *rest[:-2])

        @pl.when(last())
        def _():
            for cp in cps:
                cp.wait()

    outs = pl.pallas_call(
        riding, name=f"{name}_{rider.tag}", grid=grid, in_specs=list(in_specs) + [ANY] * ni,
        out_specs=list(out_specs) + [ANY] * no, out_shape=list(out_shape) + rider.outs,
        scratch_shapes=list(scratch) + [pltpu.SemaphoreType.DMA((rider.n,)), pltpu.SemaphoreType.DMA((rider.n,))],
        input_output_aliases={n_in + i: n_out + j for i, j in rider.aliases.items()},
        compiler_params=pltpu.CompilerParams(has_side_effects=True, **params),
    )(*operands, *rider.ins)
    return list(outs[:n_out]), list(outs[n_out:])


def _edges_1d(n):
    return (lambda: pl.program_id(0) == 0), (lambda: pl.program_id(0) == n - 1)


def _edges_2d(n0, n1):
    return ((lambda: (pl.program_id(0) == 0) & (pl.program_id(1) == 0)),
            (lambda: (pl.program_id(0) == n0 - 1) & (pl.program_id(1) == n1 - 1)))


def _nt(a, b):
    return lax.dot_general(a, b, (((1,), (1,)), ((), ())), preferred_element_type=F32)


def _tn(a, b):
    return lax.dot_general(a, b, (((0,), (0,)), ((), ())), preferred_element_type=F32)


def _nn(a, b):
    return jnp.dot(a, b, preferred_element_type=F32)


def _ln_fwd(z, g, b):
    mu = jnp.mean(z, axis=-1, keepdims=True)
    zc = z - mu
    var = jnp.mean(zc * zc, axis=-1, keepdims=True)
    return zc * lax.rsqrt(var + LN_EPS) * g + b


def _ln_bwd(dy, z, g):
    mu = jnp.mean(z, axis=-1, keepdims=True)
    zc = z - mu
    var = jnp.mean(zc * zc, axis=-1, keepdims=True)
    rstd = lax.rsqrt(var + LN_EPS)
    xhat = zc * rstd
    gdy = dy * g
    m1 = jnp.mean(gdy, axis=-1, keepdims=True)
    m2 = jnp.mean(gdy * xhat, axis=-1, keepdims=True)
    return rstd * (gdy - m1 - xhat * m2), xhat


def _ffn_fwd(x, wg, wu, wd, lg, lb, rider=None):
    S, D = x.shape
    fq = wg.shape[-1]
    tm = min(512, S)

    def body(x_ref, wg_ref, wu_ref, wd_ref, lg_ref, lb_ref, xo_ref, xb_ref, z_ref, g_ref, u_ref):
        x = x_ref[...]
        xb = x.astype(BF)
        acc = jnp.zeros((tm, D), F32)
        for q in range(NQ):
            g = _nn(xb, wg_ref[q])
            u = _nn(xb, wu_ref[q])
            g_ref[q] = g.astype(BF)
            u_ref[q] = u.astype(BF)
            a = g * jax.nn.sigmoid(g) * u
            acc = acc + _nn(a.astype(BF), wd_ref[q])
        z = ALPHA * x + 0.5 * acc
        xo = _ln_fwd(z, lg_ref[...], lb_ref[...])
        z_ref[...] = z
        xo_ref[...] = xo
        xb_ref[...] = xo.astype(BF)

    row = pl.BlockSpec((tm, D), lambda i: (i, 0))
    qrow = pl.BlockSpec((NQ, tm, fq), lambda i: (0, i, 0))
    return _pcall(
        body, [x, wg, wu, wd, lg, lb], name="ffn_fwd", grid=(S // tm,),
        in_specs=[row, _quarters(wg), _quarters(wu), _quarters(wd), _full((1, D)), _full((1, D))],
        out_specs=[row, row, row, qrow, qrow],
        out_shape=[jax.ShapeDtypeStruct((S, D), F32), jax.ShapeDtypeStruct((S, D), BF),
                   jax.ShapeDtypeStruct((S, D), F32), jax.ShapeDtypeStruct((NQ, S, fq), BF),
                   jax.ShapeDtypeStruct((NQ, S, fq), BF)],
        rider=rider, edges=_edges_1d(S // tm))


def _ffn_bwd(dxo, z, g, u, wg, wu, wd, lg, rider=None):
    S, D = dxo.shape
    fq = wg.shape[-1]
    tm = min(256, S)
    nt = S // tm

    def body(dxo0_ref, z0_ref, dxo1_ref, z1_ref, g_ref, u_ref, wg_ref, wu_ref, wd_ref, lg_ref,
             dx_ref, df_ref, dg_ref, du_ref, a_ref, ln_ref, dz_ref):
        i = pl.program_id(0)

        @pl.when(i == 0)
        def _():
            dy0 = dxo0_ref[...]
            dz0, xhat0 = _ln_bwd(dy0, z0_ref[...], lg_ref[...])
            dz_ref[...] = dz0
            ln_ref[...] = jnp.zeros_like(ln_ref)
            ln_ref[0:1, :] += jnp.sum(dy0 * xhat0, axis=0, keepdims=True)
            ln_ref[1:2, :] += jnp.sum(dy0, axis=0, keepdims=True)

        dz = dz_ref[...]
        dfb = (0.5 * dz).astype(BF)
        df_ref[...] = dfb
        acc = ALPHA * dz
        for q in range(NQ):
            da = _nt(dfb, wd_ref[q])
            gg = g_ref[q].astype(F32)
            uu = u_ref[q].astype(F32)
            sg = jax.nn.sigmoid(gg)
            silu = gg * sg
            a_ref[q] = (silu * uu).astype(BF)
            dgb = (da * uu * (sg * (1.0 + gg * (1.0 - sg)))).astype(BF)
            dub = (da * silu).astype(BF)
            dg_ref[q] = dgb
            du_ref[q] = dub
            acc = acc + _nt(dgb, wg_ref[q]) + _nt(dub, wu_ref[q])
        dx_ref[...] = acc
        dy1 = dxo1_ref[...]
        dz1, xhat1 = _ln_bwd(dy1, z1_ref[...], lg_ref[...])
        real = (i < nt - 1).astype(F32)
        ln_ref[0:1, :] += real * jnp.sum(dy1 * xhat1, axis=0, keepdims=True)
        ln_ref[1:2, :] += real * jnp.sum(dy1, axis=0, keepdims=True)
        dz_ref[...] = dz1

    row = pl.BlockSpec((tm, D), lambda i: (i, 0))
    first = pl.BlockSpec((tm, D), lambda i: (0, 0))
    nxt = pl.BlockSpec((tm, D), lambda i: (jnp.minimum(i + 1, nt - 1), 0))
    qrow = pl.BlockSpec((NQ, tm, fq), lambda i: (0, i, 0))
    qshape = jax.ShapeDtypeStruct((NQ, S, fq), BF)
    return _pcall(
        body, [dxo, z, dxo, z, g, u, wg, wu, wd, lg], name="ffn_bwd", grid=(nt,),
        in_specs=[first, first, nxt, nxt, qrow, qrow, _quarters(wg), _quarters(wu), _quarters(wd), _full((1, D))],
        out_specs=[row, row, qrow, qrow, qrow, _full((8, D))],
        out_shape=[jax.ShapeDtypeStruct((S, D), F32), jax.ShapeDtypeStruct((S, D), BF), qshape, qshape, qshape,
                   jax.ShapeDtypeStruct((8, D), F32)],
        scratch=[pltpu.VMEM((tm, D), F32)],
        rider=rider, edges=_edges_1d(nt))


def _wgrad_gate_up(a, dg, du, rider=None):
    S, K = a.shape
    n = dg.shape[-1]
    ts = min(WGRAD_TOKENS, S)

    def body(a_ref, g_ref, u_ref, o_ref):
        @pl.when(pl.program_id(1) == 0)
        def _():
            o_ref[...] = jnp.zeros_like(o_ref)
        av = a_ref[...]
        o_ref[0:K, :] += _tn(av, g_ref[...])
        o_ref[K:2 * K, :] += _tn(av, u_ref[...])

    bspec = pl.BlockSpec((None, ts, n), lambda q, s: (q, s, 0))
    (out,), got = _pcall(
        body, [a, dg, du], name="wgrad_gate_up", grid=(NQ, S // ts),
        in_specs=[pl.BlockSpec((ts, K), lambda q, s: (s, 0)), bspec, bspec],
        out_specs=[pl.BlockSpec((None, 2 * K, n), lambda q, s: (q, 0, 0))],
        out_shape=[jax.ShapeDtypeStruct((NQ, 2 * K, n), F32)],
        rider=rider, edges=_edges_2d(NQ, S // ts))
    return out, got


def _wgrad_down(a, df):
    _, S, k = a.shape
    N = df.shape[1]
    ts = min(WGRAD_TOKENS, S)

    def body(a_ref, b_ref, o_ref):
        @pl.when(pl.program_id(1) == 0)
        def _():
            o_ref[...] = jnp.zeros_like(o_ref)
        o_ref[...] += _tn(a_ref[...], b_ref[...])

    return pl.pallas_call(
        body, name="wgrad_down", grid=(NQ, S // ts),
        in_specs=[pl.BlockSpec((None, ts, k), lambda q, s: (q, s, 0)), pl.BlockSpec((ts, N), lambda q, s: (s, 0))],
        out_specs=pl.BlockSpec((None, k, N), lambda q, s: (q, 0, 0)),
        out_shape=jax.ShapeDtypeStruct((NQ, k, N), F32),
        compiler_params=_cp(2),
    )(a, df)


def _wgrad_out(yab, yc, dzb):
    S, h = yab.shape
    D = dzb.shape[1]
    k = h // 2
    ts = min(WGRAD_TOKENS, S)

    def body(yab_ref, yc_ref, b_ref, o_ref):
        @pl.when(pl.program_id(0) == 0)
        def _():
            o_ref[...] = jnp.zeros_like(o_ref)
        b = b_ref[...]
        o_ref[0] += _tn(yab_ref[:, 0:k], b)
        o_ref[1] += _tn(yab_ref[:, k:h], b)
        o_ref[2] += _tn(yc_ref[:, 0:k], b)
        o_ref[3] += _tn(yc_ref[:, k:h], b)

    row = lambda w: pl.BlockSpec((ts, w), lambda s: (s, 0))
    return pl.pallas_call(
        body, name="wgrad_out", grid=(S // ts,),
        in_specs=[row(h), row(h), row(D)], out_specs=_full((NQ, k, D)),
        out_shape=jax.ShapeDtypeStruct((NQ, k, D), F32),
        compiler_params=_cp(1),
    )(yab, yc, dzb)


def _proj(xb, wc):
    S, D = xb.shape
    n = wc.shape[-1]
    n1 = D_POOL + 3 * D_CONV
    n2 = NQ * n - n1
    tm = min(1024, S)

    def body(x_ref, w_ref, p_ref, qkv_ref):
        x = x_ref[...]
        for q in range(NQ):
            r = _nn(x, w_ref[q])
            lo, hi = q * n, (q + 1) * n
            if hi <= n1:
                p_ref[:, lo:hi] = r
            elif lo >= n1:
                qkv_ref[:, lo - n1:hi - n1] = r.astype(BF)
            else:
                p_ref[:, lo:n1] = r[:, 0:n1 - lo]
                qkv_ref[:, 0:hi - n1] = r[:, n1 - lo:n].astype(BF)

    row = lambda w: pl.BlockSpec((tm, w), lambda i: (i, 0))
    return pl.pallas_call(
        body, name="mix_proj", grid=(S // tm,),
        in_specs=[row(D), _quarters(wc)],
        out_specs=[row(n1), row(n2)],
        out_shape=[jax.ShapeDtypeStruct((S, n1), F32), jax.ShapeDtypeStruct((S, n2), BF)],
        compiler_params=_cp(1),
    )(xb, wc)


def _mm_exact(a, b, name):
    def body(a_ref, b_ref, o_ref):
        o_ref[...] = jnp.dot(a_ref[...], b_ref[...].astype(F32), preferred_element_type=F32,
                             precision=lax.Precision.HIGHEST)

    return pl.pallas_call(
        body, name=name, in_specs=[_full(a.shape), _full(b.shape)], out_specs=_full((a.shape[0], b.shape[1])),
        out_shape=jax.ShapeDtypeStruct((a.shape[0], b.shape[1]), F32),
        compiler_params=pltpu.CompilerParams(vmem_limit_bytes=VMEM_LIMIT),
    )(a, b)


NB_ROWS = 2 * NA_ROWS - 1
NB_COLS = 2 * NA_COLS


def _bias_constants():
    c = np.arange(GRID_W)
    col_start = np.clip(c - NA_COLS // 2, 0, GRID_W - NA_COLS)
    valid = (c[None, :] >= col_start[:, None]) & (c[None, :] < col_start[:, None] + NA_COLS)
    dc = np.clip(c[None, :] - c[:, None], -(NA_COLS - 1), NA_COLS - 1) + (NA_COLS - 1)
    cq, ck = np.meshgrid(c, c, indexing="ij")
    onehot = np.zeros((HG, NB_COLS, GRID_W, HG, GRID_W), np.float32)
    for h in range(HG):
        onehot[h, dc[cq, ck], ck, h, cq] = 1.0
    mask_kq = np.where(valid.T, 0.0, NEG_INF).astype(np.float32)
    mask = np.tile(mask_kq, (NB_ROWS, HG))
    return onehot.reshape(HG * NB_COLS, GRID_W * LW), mask


def _bias_table(rpb, onehot, mask):
    ngr = NA_HEADS // HG
    r = rpb.reshape(ngr, HG, NB_ROWS, NB_COLS - 1).transpose(0, 2, 1, 3)
    r = jnp.pad(r, ((0, 0), (0, 0), (0, 0), (0, 1))).reshape(ngr * NB_ROWS, HG * NB_COLS)
    t = _mm_exact(r, onehot, "bias_expand")
    return t.reshape(ngr, NB_ROWS * GRID_W, LW) + mask[None]


def _bias_grad(dt, onehot_t):
    ngr = NA_HEADS // HG
    g = _mm_exact(dt.reshape(ngr * NB_ROWS, GRID_W * LW), onehot_t, "bias_reduce")
    g = g.reshape(ngr, NB_ROWS, HG, NB_COLS)[..., :NB_COLS - 1]
    return g.transpose(0, 2, 1, 3).reshape(NA_HEADS, NB_ROWS, NB_COLS - 1)


def _attn_rows(S):
    rows = S // GRID_W
    rb = min(16, rows)
    return rows, rb


def _head_masks():
    lane = lax.broadcasted_iota(jnp.int32, (GRID_W, LW), 1)
    return [(lane >= HEAD_DIM * h) & (lane < HEAD_DIM * (h + 1)) for h in range(HG)]


def _stack_heads(x, masks):
    zero = jnp.zeros_like(x)
    return jnp.concatenate([jnp.where(m, x, zero) for m in masks], axis=0)


def _unstack_heads(x2, masks):
    out = x2[0:GRID_W]
    for h in range(1, HG):
        out = jnp.where(masks[h], x2[h * GRID_W:(h + 1) * GRID_W], out)
    return out


def _attn_step(r, rows, q, k_ref, v_ref, b_ref, masks):
    rs = jnp.clip(r - NA_ROWS // 2, 0, rows - NA_ROWS)
    s0 = rs - r + (NA_ROWS - 1)
    q2 = _stack_heads(q, masks)
    ks = pl.ds(pl.multiple_of(rs * GRID_W, GRID_W), NA_ROWS * GRID_W)
    kb = k_ref[ks, :]
    vb = v_ref[ks, :]
    bs = pl.ds(pl.multiple_of(s0 * GRID_W, GRID_W), NA_ROWS * GRID_W)
    s = _nt(kb, q2) * (HEAD_DIM ** -0.5) + b_ref[0, bs, :]
    m = jnp.max(s, axis=0, keepdims=True)
    p = jnp.exp(s - m)
    p = p / jnp.sum(p, axis=0, keepdims=True)
    return p, q2, kb, vb, ks, bs


def _attn_fwd(qkv, bias, rider=None):
    S = qkv.shape[0]
    rows, rb = _attn_rows(S)
    tq = rb * GRID_W
    ngr = NA_HEADS // HG

    def body(q_ref, k_ref, v_ref, b_ref, o_ref):
        base = pl.program_id(1) * rb
        masks = _head_masks()

        def step(i, carry):
            qs = pl.ds(pl.multiple_of(i * GRID_W, GRID_W), GRID_W)
            p, _, _, vb, _, _ = _attn_step(base + i, rows, q_ref[qs, :], k_ref, v_ref, b_ref, masks)
            o_ref[qs, :] = _unstack_heads(_tn(p.astype(BF), vb), masks).astype(BF)
            return carry

        lax.fori_loop(0, rb, step, 0, unroll=rb)

    return _pcall(
        body, [qkv, qkv, qkv, bias], name="attn_fwd", grid=(ngr, rows // rb),
        in_specs=[pl.BlockSpec((tq, LW), lambda h, r: (r, h)),
                  pl.BlockSpec((S, LW), lambda h, r: (0, ngr + h)),
                  pl.BlockSpec((S, LW), lambda h, r: (0, 2 * ngr + h)),
                  pl.BlockSpec((1, bias.shape[1], LW), lambda h, r: (h, 0, 0))],
        out_specs=[pl.BlockSpec((tq, LW), lambda h, r: (r, h))],
        out_shape=[jax.ShapeDtypeStruct((S, D_NA), BF)],
        rider=rider, edges=_edges_2d(ngr, rows // rb))


def _attn_bwd(qkv, bias, dycat, rider=None):
    S = qkv.shape[0]
    rows, rb = _attn_rows(S)
    tq = rb * GRID_W
    ngr = NA_HEADS // HG
    scale = HEAD_DIM ** -0.5

    def body(q_ref, k_ref, v_ref, b_ref, do_ref, dq_ref, dk_ref, dv_ref, db_ref, dka_ref, dva_ref):
        base = pl.program_id(1) * rb
        last = pl.program_id(1) == pl.num_programs(1) - 1
        masks = _head_masks()

        @pl.when(pl.program_id(1) == 0)
        def _():
            dka_ref[...] = jnp.zeros_like(dka_ref)
            dva_ref[...] = jnp.zeros_like(dva_ref)
            db_ref[...] = jnp.zeros_like(db_ref)

        def step(i, carry):
            qs = pl.ds(pl.multiple_of(i * GRID_W, GRID_W), GRID_W)
            p, q2, kb, vb, ks, bs = _attn_step(base + i, rows, q_ref[qs, :], k_ref, v_ref, b_ref, masks)
            do2 = _stack_heads(do_ref[qs, :].astype(BF), masks)
            dp = _nt(vb, do2)
            ds = p * (dp - jnp.sum(p * dp, axis=0, keepdims=True))
            db_ref[0, bs, :] += ds
            dsb = ds.astype(BF)
            dq_ref[qs, :] = _unstack_heads(_tn(dsb, kb) * scale, masks).astype(BF)
            dka_ref[ks, :] += _nn(dsb, q2) * scale
            dva_ref[ks, :] += _nn(p.astype(BF), do2)
            return carry

        lax.fori_loop(0, rb, step, 0, unroll=rb)

        @pl.when(last)
        def _():
            dk_ref[...] = dka_ref[...].astype(BF)
            dv_ref[...] = dva_ref[...].astype(BF)

    nb = bias.shape[1]
    once = dict(pipeline_mode=pl.Buffered(1))
    nd = D_NA // LW
    return _pcall(
        body, [qkv, qkv, qkv, bias, dycat], name="attn_bwd", grid=(ngr, rows // rb),
        in_specs=[pl.BlockSpec((tq, LW), lambda h, r: (r, h)),
                  pl.BlockSpec((S, LW), lambda h, r: (0, ngr + h), **once),
                  pl.BlockSpec((S, LW), lambda h, r: (0, 2 * ngr + h), **once),
                  pl.BlockSpec((1, nb, LW), lambda h, r: (h, 0, 0)),
                  pl.BlockSpec((tq, LW), lambda h, r: (r, nd + h))],
        out_specs=[pl.BlockSpec((tq, LW), lambda h, r: (r, h)),
                   pl.BlockSpec((S, LW), lambda h, r: (0, h)),
                   pl.BlockSpec((S, LW), lambda h, r: (0, h)),
                   pl.BlockSpec((1, nb, LW), lambda h, r: (h, 0, 0))],
        out_shape=[jax.ShapeDtypeStruct((S, D_NA), BF)] * 3 + [jax.ShapeDtypeStruct((ngr, nb, LW), F32)],
        scratch=[pltpu.VMEM((S, LW), F32), pltpu.VMEM((S, LW), F32)],
        rider=rider, edges=_edges_2d(ngr, rows // rb))


def _halo_specs(tm, width, S):
    hb = tm // HALO
    last = S // HALO - 1
    return [pl.BlockSpec((tm, width), lambda i: (i, 0)),
            pl.BlockSpec((HALO, width), lambda i: (jnp.maximum(i * hb - 1, 0), 0)),
            pl.BlockSpec((HALO, width), lambda i: (jnp.minimum((i + 1) * hb, last), 0))]


def _with_halo(cur_ref, prev_ref, next_ref, i, nt):
    prev = jnp.where(i > 0, prev_ref[...], 0.0)
    nxt = jnp.where(i < nt - 1, next_ref[...], 0.0)
    return jnp.concatenate([prev, cur_ref[...], nxt], axis=0)


def _shift(a, k):
    n = a.shape[0]
    return pltpu.roll(a, k % n, 0)


def _pool_lanes(n):
    lane = lax.broadcasted_iota(jnp.int32, (n, D_POOL), 1)
    group = D_POOL // len(POOL_WINDOWS)
    return [lane < group * (j + 1) for j in range(len(POOL_WINDOWS) - 1)]


def _by_window(lanes, vals):
    return jnp.where(lanes[0], vals[0], jnp.where(lanes[1], vals[1], jnp.where(lanes[2], vals[2], vals[3])))


def _pool_count(lanes, t, S):
    back = _by_window(lanes, tuple(w // 2 for w in POOL_WINDOWS))
    lo = jnp.maximum(t - back, 0)
    hi = jnp.minimum(t + back, S)
    return jnp.maximum(hi - lo, 1).astype(F32)


def _pool_p(u, lanes, cnt):
    a = u + _shift(u, 1)
    b = _shift(a, 1) + _shift(a, -1)
    c = _shift(b, 2) + _shift(b, -2)
    d = _shift(c, 4) + _shift(c, -4)
    return _by_window(lanes, (a, b, c, d)) / cnt - u


def _mixab_fwd(pabc, wblk, vec):
    S = pabc.shape[0]
    tm = min(512, S)
    nt = S // tm
    n = tm + 2 * HALO
    tile = slice(HALO, HALO + tm)

    def body(cur_ref, prev_ref, next_ref, w_ref, vec_ref, o_ref):
        i = pl.program_id(0)
        ext = _with_halo(cur_ref, prev_ref, next_ref, i, nt)
        lanes = _pool_lanes(n)
        t = i * tm - HALO + lax.broadcasted_iota(jnp.int32, (n, D_POOL), 0)
        p = _pool_p(ext[:, 0:D_POOL], lanes, _pool_count(lanes, t, S))[tile]
        o_ref[:, 0:D_POOL] = (_nn(p.astype(BF), w_ref[...]) * vec_ref[0:1, :]).astype(BF)
        zc = ext[:, 512:768] * ext[:, 768:1024]
        conv = vec_ref[1:2, :] * _shift(zc, 1) + vec_ref[2:3, :] * zc + vec_ref[3:4, :] * _shift(zc, -1)
        o_ref[:, D_POOL:D_POOL + D_CONV] = (ext[tile, 256:512] * conv[tile]).astype(BF)

    return pl.pallas_call(
        body, name="mixab_fwd", grid=(nt,),
        in_specs=_halo_specs(tm, 1024, S) + [_full((D_POOL, D_POOL)), _full((8, D_POOL))],
        out_specs=pl.BlockSpec((tm, D_POOL + D_CONV), lambda i: (i, 0)),
        out_shape=jax.ShapeDtypeStruct((S, D_POOL + D_CONV), BF),
        compiler_params=_cp(1),
    )(pabc, pabc, pabc, wblk, vec)


def _mixab_bwd(pabc, dycat, wblk, vec):
    S = pabc.shape[0]
    tm = min(512, S)
    nt = S // tm
    n = tm + 2 * HALO
    tile = slice(HALO, HALO + tm)

    def body(cur_ref, prev_ref, next_ref, dcur_ref, dprev_ref, dnext_ref, w_ref, vec_ref, o_ref, dw_ref, dvec_ref):
        i = pl.program_id(0)

        @pl.when(i == 0)
        def _():
            dw_ref[...] = jnp.zeros_like(dw_ref)
            dvec_ref[...] = jnp.zeros_like(dvec_ref)

        ext = _with_halo(cur_ref, prev_ref, next_ref, i, nt)
        dext = _with_halo(dcur_ref, dprev_ref, dnext_ref, i, nt)
        lanes = _pool_lanes(n)
        t = i * tm - HALO + lax.broadcasted_iota(jnp.int32, (n, D_POOL), 0)
        cnt = _pool_count(lanes, t, S)
        w = w_ref[...]
        scale = vec_ref[0:1, :]
        pb = _pool_p(ext[:, 0:D_POOL], lanes, cnt)[tile].astype(BF)
        dya = dext[:, 0:D_POOL]
        dvec_ref[0:1, :] += jnp.sum(dya[tile] * _nn(pb, w), axis=0, keepdims=True)
        dqb = (dya * scale).astype(BF)
        dw_ref[...] += _tn(pb, dqb[tile])
        dp = _nt(dqb, w)
        r = dp / cnt
        a = r + _shift(r, -1)
        b = _shift(a, 1) + _shift(a, -1)
        c = _shift(b, 2) + _shift(b, -2)
        d = _shift(c, 4) + _shift(c, -4)
        o_ref[:, 0:256] = (_by_window(lanes, (a, b, c, d)) - dp)[tile].astype(BF)
        gb, gc, hh = ext[:, 256:512], ext[:, 512:768], ext[:, 768:1024]
        zc = gc * hh
        zm, zp = _shift(zc, 1), _shift(zc, -1)
        w0, w1, w2 = vec_ref[1:2, :], vec_ref[2:3, :], vec_ref[3:4, :]
        dyb = dext[:, D_POOL:D_POOL + D_CONV]
        dconv = dyb * gb
        o_ref[:, 256:512] = (dyb * (w0 * zm + w1 * zc + w2 * zp))[tile].astype(BF)
        dzc = w0 * _shift(dconv, -1) + w1 * dconv + w2 * _shift(dconv, 1)
        o_ref[:, 512:768] = (dzc * hh)[tile].astype(BF)
        o_ref[:, 768:1024] = (dzc * gc)[tile].astype(BF)
        dct = dconv[tile]
        dvec_ref[1:2, :] += jnp.sum(dct * zm[tile], axis=0, keepdims=True)
        dvec_ref[2:3, :] += jnp.sum(dct * zc[tile], axis=0, keepdims=True)
        dvec_ref[3:4, :] += jnp.sum(dct * zp[tile], axis=0, keepdims=True)

    return pl.pallas_call(
        body, name="mixab_bwd", grid=(nt,),
        in_specs=_halo_specs(tm, 1024, S) + _halo_specs(tm, 512, S) + [_full((D_POOL, D_POOL)), _full((8, D_POOL))],
        out_specs=[pl.BlockSpec((tm, 1024), lambda i: (i, 0)), _full((D_POOL, D_POOL)), _full((8, D_POOL))],
        out_shape=[jax.ShapeDtypeStruct((S, 1024), BF), jax.ShapeDtypeStruct((D_POOL, D_POOL), F32),
                   jax.ShapeDtypeStruct((8, D_POOL), F32)],
        compiler_params=_cp(1),
    )(pabc, pabc, pabc, dycat, dycat, dycat, wblk, vec)


def _mixout_fwd(yab, yc, x, wo, lg, lb):
    S, D = x.shape
    tm = min(512, S)
    h = yab.shape[1]
    k = h // 2

    def body(yab_ref, yc_ref, x_ref, w_ref, lg_ref, lb_ref, xo_ref, xb_ref, z_ref):
        y = (_nn(yab_ref[:, 0:k], w_ref[0]) + _nn(yab_ref[:, k:h], w_ref[1])
             + _nn(yc_ref[:, 0:k], w_ref[2]) + _nn(yc_ref[:, k:h], w_ref[3]))
        z = ALPHA * x_ref[...] + y
        xo = _ln_fwd(z, lg_ref[...], lb_ref[...])
        z_ref[...] = z
        xo_ref[...] = xo
        xb_ref[...] = xo.astype(BF)

    row = lambda w: pl.BlockSpec((tm, w), lambda i: (i, 0))
    return pl.pallas_call(
        body, name="mixout_fwd", grid=(S // tm,),
        in_specs=[row(h), row(h), row(D), _quarters(wo), _full((1, D)), _full((1, D))],
        out_specs=[row(D), row(D), row(D)],
        out_shape=[jax.ShapeDtypeStruct((S, D), F32), jax.ShapeDtypeStruct((S, D), BF),
                   jax.ShapeDtypeStruct((S, D), F32)],
        compiler_params=_cp(1),
    )(yab, yc, x, wo, lg, lb)


def _mixout_bwd(dxo, z, wo, lg, rider=None):
    S, D = dxo.shape
    k = wo.shape[-2]
    tm = min(512, S)
    nt = S // tm

    def body(dxo0_ref, z0_ref, dxo1_ref, z1_ref, w_ref, lg_ref, dres_ref, dzb_ref, dy_ref, ln_ref, dz_ref):
        i = pl.program_id(0)

        @pl.when(i == 0)
        def _():
            dy0 = dxo0_ref[...]
            dz0, xhat0 = _ln_bwd(dy0, z0_ref[...], lg_ref[...])
            dz_ref[...] = dz0
            ln_ref[...] = jnp.zeros_like(ln_ref)
            ln_ref[0:1, :] += jnp.sum(dy0 * xhat0, axis=0, keepdims=True)
            ln_ref[1:2, :] += jnp.sum(dy0, axis=0, keepdims=True)

        dz = dz_ref[...]
        dzb = dz.astype(BF)
        dres_ref[...] = ALPHA * dz
        dzb_ref[...] = dzb
        for q in range(NQ):
            dy_ref[:, q * k:(q + 1) * k] = _nt(dzb, w_ref[q])
        dy1 = dxo1_ref[...]
        dz1, xhat1 = _ln_bwd(dy1, z1_ref[...], lg_ref[...])
        real = (i < nt - 1).astype(F32)
        ln_ref[0:1, :] += real * jnp.sum(dy1 * xhat1, axis=0, keepdims=True)
        ln_ref[1:2, :] += real * jnp.sum(dy1, axis=0, keepdims=True)
        dz_ref[...] = dz1

    row = lambda w: pl.BlockSpec((tm, w), lambda i: (i, 0))
    first = pl.BlockSpec((tm, D), lambda i: (0, 0))
    nxt = pl.BlockSpec((tm, D), lambda i: (jnp.minimum(i + 1, nt - 1), 0))
    return _pcall(
        body, [dxo, z, dxo, z, wo, lg], name="mixout_bwd", grid=(nt,),
        in_specs=[first, first, nxt, nxt, _quarters(wo), _full((1, D))],
        out_specs=[row(D), row(D), row(NQ * k), _full((8, D))],
        out_shape=[jax.ShapeDtypeStruct((S, D), F32), jax.ShapeDtypeStruct((S, D), BF),
                   jax.ShapeDtypeStruct((S, NQ * k), F32), jax.ShapeDtypeStruct((8, D), F32)],
        scratch=[pltpu.VMEM((tm, D), F32)],
        rider=rider, edges=_edges_1d(nt))


def _take_cols(refs, lo, hi):
    parts, off = [], 0
    for r in refs:
        w = r.shape[1]
        a, b = max(lo, off), min(hi, off + w)
        if a < b:
            parts.append(r[:, a - off:b - off])
        off += w
    return parts[0] if len(parts) == 1 else jnp.concatenate(parts, axis=1)


def _proj_bwd(dres, dparts, wc, rider=None):
    S, D = dres.shape
    n = wc.shape[-1]
    tm = min(1024, S)
    np_ = len(dparts)

    def body(*refs):
        dres_ref, d_refs, w_ref, dx_ref = refs[0], refs[1:1 + np_], refs[1 + np_], refs[2 + np_]
        acc = dres_ref[...]
        for q in range(NQ):
            acc = acc + _nt(_take_cols(d_refs, q * n, (q + 1) * n), w_ref[q])
        dx_ref[...] = acc

    row = lambda w: pl.BlockSpec((tm, w), lambda i: (i, 0))
    return _pcall(
        body, [dres, *dparts, wc], name="mix_proj_bwd", grid=(S // tm,),
        in_specs=[row(D)] + [row(d.shape[1]) for d in dparts] + [_quarters(wc)],
        out_specs=[row(D)],
        out_shape=[jax.ShapeDtypeStruct((S, D), F32)],
        rider=rider, edges=_edges_1d(S // tm))


def _wgrad_in(a, dparts, n, rider=None):
    S, K = a.shape
    ts = min(WGRAD_TOKENS // 2, S)
    np_ = len(dparts)

    def body(*refs):
        a_ref, d_refs, o_ref = refs[0], refs[1:1 + np_], refs[1 + np_]

        @pl.when(pl.program_id(0) == 0)
        def _():
            o_ref[...] = jnp.zeros_like(o_ref)
        av = a_ref[...]
        for q in range(NQ):
            o_ref[q] += _tn(av, _take_cols(d_refs, q * n, (q + 1) * n))

    row = lambda w: pl.BlockSpec((ts, w), lambda s: (s, 0))
    (out,), got = _pcall(
        body, [a, *dparts], name="wgrad_in", grid=(S // ts,),
        in_specs=[row(K)] + [row(d.shape[1]) for d in dparts], out_specs=[_full((NQ, K, n))],
        out_shape=[jax.ShapeDtypeStruct((NQ, K, n), F32)],
        rider=rider, edges=_edges_1d(S // ts))
    return out, got


def _loss_head(y, target):
    S, D = y.shape
    tm = min(512, S)

    def body(y_ref, t_ref, l_ref, dy_ref):
        @pl.when(pl.program_id(0) == 0)
        def _():
            l_ref[...] = jnp.zeros_like(l_ref)
        e = y_ref[...] - t_ref[...]
        dy_ref[...] = e * (1.0 / D)
        part = jnp.sum(jnp.sum(e * e, axis=1, keepdims=True) * (1.0 / D), axis=0, keepdims=True)
        l_ref[...] += 0.5 * part

    row = pl.BlockSpec((tm, D), lambda i: (i, 0))
    return pl.pallas_call(
        body, name="loss_head", grid=(S // tm,),
        in_specs=[row, row], out_specs=[_full((8, 128)), row],
        out_shape=[jax.ShapeDtypeStruct((8, 128), F32), jax.ShapeDtypeStruct((S, D), F32)],
        compiler_params=_cp(1),
    )(y, target)


def _adamw_update(w, g, m, v):
    mn = ADAM_B1 * m + (1.0 - ADAM_B1) * g
    vn = ADAM_B2 * v + (1.0 - ADAM_B2) * (g * g)
    m_hat = mn / (1.0 - ADAM_B1 ** ADAM_STEP)
    v_hat = vn / (1.0 - ADAM_B2 ** ADAM_STEP)
    return -ADAM_LR * (m_hat / (jnp.sqrt(v_hat) + ADAM_EPS) + ADAM_WD * w), mn, vn


def _adamw(w, g, m, v):
    shape = w.shape
    cols = shape[-1]
    rows = int(np.prod(shape[:-1]))
    w2, g2, m2, v2 = (a.reshape(rows, cols) for a in (w, g, m, v))
    tr = rows
    for cand in (512, 352, 256):
        if rows > cand and rows % cand == 0:
            tr = cand
            break

    def body(w_ref, g_ref, m_ref, v_ref, d_ref, mo_ref, vo_ref):
        d_ref[...], mo_ref[...], vo_ref[...] = _adamw_update(w_ref[...], g_ref[...], m_ref[...], v_ref[...])

    spec = pl.BlockSpec((tr, cols), lambda i: (i, 0))
    outs = pl.pallas_call(
        body, name=f"adamw_{rows}x{cols}", grid=(rows // tr,),
        in_specs=[spec] * 4, out_specs=[spec] * 3,
        out_shape=[jax.ShapeDtypeStruct((rows, cols), F32)] * 3,
        compiler_params=_cp(1),
    )(w2, g2, m2, v2)
    return tuple(o.reshape(shape) for o in outs)


ADAMW_GROUP_STEPS = 16


def _adamw_group(items, name, rider=None):
    n = len(items)
    all_rows = [int(np.prod(w.shape[:-1])) for w, _, _, _ in items]
    steps = next(s for s in (ADAMW_GROUP_STEPS, 8, 4, 2, 1) if all(r % (8 * s) == 0 for r in all_rows))
    flat, specs, shapes = [], [], []
    for (w, g, m, v), rows in zip(items, all_rows):
        cols = w.shape[-1]
        tr = rows // steps
        flat += [a.reshape(rows, cols) for a in (w, g, m, v)]
        specs.append(pl.BlockSpec((tr, cols), lambda i: (i, 0)))
        shapes.append((w.shape, rows, cols))

    def body(*refs):
        ins, outs = refs[:4 * n], refs[4 * n:]
        for j in range(n):
            w_ref, g_ref, m_ref, v_ref = ins[4 * j:4 * j + 4]
            outs[3 * j][...], outs[3 * j + 1][...], outs[3 * j + 2][...] = _adamw_update(
                w_ref[...], g_ref[...], m_ref[...], v_ref[...])

    outs, got = _pcall(
        body, flat, name=name, grid=(steps,),
        in_specs=[s for s in specs for _ in range(4)], out_specs=[s for s in specs for _ in range(3)],
        out_shape=[jax.ShapeDtypeStruct((rows, cols), F32) for _, rows, cols in shapes for _ in range(3)],
        rider=rider, edges=_edges_1d(steps))
    return [tuple(o.reshape(shapes[j][0]) for o in outs[3 * j:3 * j + 3]) for j in range(n)], got


def _half_tile(h):
    return h if h <= 512 else 512


def _add_chip(g, recv):
    _, R, C = g.shape
    h = R // 2
    tr = _half_tile(h)
    nb = h // tr

    def body(a_ref, b_ref, o_ref, ob_ref):
        s = a_ref[...] + b_ref[...]
        ob_ref[...] = s.astype(BF)

        @pl.when(pl.program_id(1) == 2 * lax.axis_index("x") + lax.axis_index("y"))
        def _():
            o_ref[...] = s[0]

    half = pl.BlockSpec((1, tr, C), lambda i, q: (q, i, 0))
    mine = pl.BlockSpec((1, tr, C), lambda i, q: (q, lax.axis_index("c") * nb + i, 0))
    return pl.pallas_call(
        body, name=f"rs_add_chip_{R}x{C}", grid=(nb, NQ), in_specs=[mine, half],
        out_specs=[pl.BlockSpec((tr, C), lambda i, q: (i, 0)), half],
        out_shape=[jax.ShapeDtypeStruct((h, C), F32), jax.ShapeDtypeStruct((NQ, h, C), BF)],
        compiler_params=_cp(2),
    )(g, recv)


def _add_final(chip, recv):
    h, C = chip.shape
    tr = _half_tile(h)
    nb = h // tr

    def body(a_ref, b_ref, o_ref):
        s = a_ref[...]
        for j in range(3):
            s = s + b_ref[j].astype(F32)
        o_ref[...] = s

    return pl.pallas_call(
        body, name=f"rs_add_final_{h}x{C}", grid=(nb,),
        in_specs=[pl.BlockSpec((tr, C), lambda i: (i, 0)), pl.BlockSpec((3, tr, C), lambda i: (0, i, 0))],
        out_specs=pl.BlockSpec((tr, C), lambda i: (lax.axis_index("c") * nb + i, 0)),
        out_shape=jax.ShapeDtypeStruct((2 * h, C), F32),
        compiler_params=_cp(1),
    )(chip, recv)


COMM = pltpu.CompilerParams(has_side_effects=True)


def _place():
    x, y, c = lax.axis_index("x"), lax.axis_index("y"), lax.axis_index("c")
    chips = [(1 - x, y), (x, 1 - y), (1 - x, 1 - y)]
    return x, y, c, chips


def _half0(ref, c):
    n = ref.shape[0] // 2
    return ref.at[pl.ds(c * n, n)]


def _gather_ici(shards):
    n = len(shards)

    def copies(r_in, r_out, ssem, rsem, base):
        x, y, c, chips = _place()
        q = 2 * x + y
        return [pltpu.make_async_remote_copy(
            src_ref=_half0(r_in[i], c), dst_ref=_half0(r_out[i].at[q], c), send_sem=ssem.at[base + 3 * i + j],
            recv_sem=rsem.at[base + 3 * i + j], device_id=(*chip, c), device_id_type=MESH)
            for i in range(n) for j, chip in enumerate(chips)]

    return _Rider("ici", shards, [jax.ShapeDtypeStruct((NQ,) + s.shape, BF) for s in shards], {}, 3 * n, copies)


def _gather_d2d(bufs):
    n = len(bufs)

    def copies(r_in, r_out, ssem, rsem, base):
        x, y, c, chips = _place()
        return [pltpu.make_async_remote_copy(
            src_ref=_half0(r_in[i].at[2 * cx + cy], c), dst_ref=_half0(r_out[i].at[2 * cx + cy], c),
            send_sem=ssem.at[base + 3 * i + j], recv_sem=rsem.at[base + 3 * i + j], device_id=(x, y, 1 - c),
            device_id_type=MESH) for i in range(n) for j, (cx, cy) in enumerate(chips)]

    return _Rider("d2d", bufs, [jax.ShapeDtypeStruct(b.shape, b.dtype) for b in bufs], {i: i for i in range(n)},
                  3 * n, copies)


def _gather_small(small):
    sr = small.shape[0]

    def body(s_ref, o_ref, send_sems, recv_sems):
        x, y, c, chips = _place()
        o_ref[2 * x + y] = s_ref[...]
        cps = [pltpu.make_async_remote_copy(
            src_ref=s_ref, dst_ref=o_ref.at[2 * x + y], send_sem=send_sems.at[j], recv_sem=recv_sems.at[j],
            device_id=(*chip, c), device_id_type=MESH) for j, chip in enumerate(chips)]
        for cp in cps:
            cp.start()
        for j, (cx, cy) in enumerate(chips):
            pltpu.make_async_remote_copy(
                src_ref=s_ref, dst_ref=o_ref.at[2 * cx + cy], send_sem=send_sems.at[j], recv_sem=recv_sems.at[j],
                device_id=(cx, cy, c), device_id_type=MESH).wait_recv()
        for cp in cps:
            cp.wait_send()

    vm = pl.BlockSpec(memory_space=pltpu.VMEM)
    return pl.pallas_call(
        body, name="gather_small", in_specs=[vm], out_specs=vm,
        out_shape=jax.ShapeDtypeStruct((NQ, sr, 128), F32),
        scratch_shapes=[pltpu.SemaphoreType.DMA((3,)), pltpu.SemaphoreType.DMA((3,))],
        compiler_params=COMM,
    )(small)


def _swap_halves(gs):
    n = len(gs)

    def copies(r_in, r_out, ssem, rsem, base):
        x, y, c, _ = _place()
        cps = []
        for i in range(n):
            h = r_in[i].shape[1] // 2
            cps.append(pltpu.make_async_remote_copy(
                src_ref=r_in[i].at[:, pl.ds((1 - c) * h, h), :], dst_ref=r_out[i], send_sem=ssem.at[base + i],
                recv_sem=rsem.at[base + i], device_id=(x, y, 1 - c), device_id_type=MESH))
        return cps

    return _Rider("swap", gs, [jax.ShapeDtypeStruct((NQ, g.shape[1] // 2, g.shape[2]), F32) for g in gs], {}, n,
                  copies)


def _scatter_chips(chips_b):
    n = len(chips_b)

    def copies(r_in, r_out, ssem, rsem, base):
        x, y, c, chips = _place()
        return [pltpu.make_async_remote_copy(
            src_ref=r_in[i].at[2 * cx + cy], dst_ref=r_out[i].at[j], send_sem=ssem.at[base + 3 * i + j],
            recv_sem=rsem.at[base + 3 * i + j], device_id=(cx, cy, c), device_id_type=MESH)
            for i in range(n) for j, (cx, cy) in enumerate(chips)]

    return _Rider("scatter", chips_b, [jax.ShapeDtypeStruct((3,) + s.shape[1:], BF) for s in chips_b], {}, 3 * n,
                  copies)


def _run_alone(rider, name):
    ni, no = len(rider.ins), len(rider.outs)

    def body(*refs):
        cps = rider.copies(refs[:ni], refs[ni:ni + no], refs[ni + no], refs[ni + no + 1], 0)
        for cp in cps:
            cp.start()
        for cp in cps:
            cp.wait()

    return list(pl.pallas_call(
        body, name=name, in_specs=[ANY] * ni, out_specs=[ANY] * no, out_shape=rider.outs,
        input_output_aliases=dict(rider.aliases),
        scratch_shapes=[pltpu.SemaphoreType.DMA((rider.n,)), pltpu.SemaphoreType.DMA((rider.n,))],
        compiler_params=COMM,
    )(*rider.ins))


def _join_halves(fs):
    n = len(fs)

    def copies(r_in, r_out, ssem, rsem, base):
        x, y, c, _ = _place()
        return [pltpu.make_async_remote_copy(
            src_ref=_half0(r_in[i], c), dst_ref=_half0(r_out[i], c), send_sem=ssem.at[base + i],
            recv_sem=rsem.at[base + i], device_id=(x, y, 1 - c), device_id_type=MESH) for i in range(n)]

    return _Rider("join", fs, [jax.ShapeDtypeStruct(f.shape, F32) for f in fs], {i: i for i in range(n)}, n, copies)


def _allreduce_small(v):
    r, W = v.shape

    def body(v_ref, o_ref, land_ref, send_sems, recv_sems):
        x, y, c, _ = _place()
        me = 4 * x + 2 * y + c
        cps = []
        for m in range(1, 8):
            to = (x ^ (m >> 2), y ^ ((m >> 1) & 1), c ^ (m & 1))
            cps.append(pltpu.make_async_remote_copy(
                src_ref=v_ref, dst_ref=land_ref.at[m - 1], send_sem=send_sems.at[m - 1], recv_sem=recv_sems.at[m - 1],
                device_id=to, device_id_type=MESH))
        for cp in cps:
            cp.start()
        for cp in cps:
            cp.wait()
        total = jnp.zeros((r, W), F32)
        for d in range(8):
            slot = jnp.maximum((me ^ d) - 1, 0)
            total = total + jnp.where(me == d, v_ref[...], land_ref[slot])
        o_ref[...] = total

    return pl.pallas_call(
        body, name="allreduce_small",
        in_specs=[pl.BlockSpec(memory_space=pltpu.VMEM)], out_specs=pl.BlockSpec(memory_space=pltpu.VMEM),
        out_shape=jax.ShapeDtypeStruct((r, W), F32),
        scratch_shapes=[pltpu.VMEM((7, r, W), F32), pltpu.SemaphoreType.DMA((7,)), pltpu.SemaphoreType.DMA((7,))],
        compiler_params=pltpu.CompilerParams(has_side_effects=True, vmem_limit_bytes=VMEM_LIMIT),
    )(v)


def kernel(x, ffn1_w_gate, ffn1_w_up, ffn1_w_down, ffn2_w_gate, ffn2_w_up, ffn2_w_down, w_in, pool_w, pool_scale, conv_w, rpb, w_out, ln_g, ln_b, loss_target, m_ffn1_w_gate, m_ffn1_w_up, m_ffn1_w_down, m_ffn2_w_gate, m_ffn2_w_up, m_ffn2_w_down, m_w_in, m_pool_w, m_pool_scale, m_conv_w, m_rpb, m_w_out, m_ln_g, m_ln_b, v_ffn1_w_gate, v_ffn1_w_up, v_ffn1_w_down, v_ffn2_w_gate, v_ffn2_w_up, v_ffn2_w_down, v_w_in, v_pool_w, v_pool_scale, v_conv_w, v_rpb, v_w_out, v_ln_g, v_ln_b):
    weights = dict(ffn1_w_gate=ffn1_w_gate, ffn1_w_up=ffn1_w_up, ffn1_w_down=ffn1_w_down, ffn2_w_gate=ffn2_w_gate,
                   ffn2_w_up=ffn2_w_up, ffn2_w_down=ffn2_w_down, w_in=w_in, pool_w=pool_w, pool_scale=pool_scale,
                   conv_w=conv_w, rpb=rpb, w_out=w_out, ln_g=ln_g, ln_b=ln_b)
    mom_m = dict(ffn1_w_gate=m_ffn1_w_gate, ffn1_w_up=m_ffn1_w_up, ffn1_w_down=m_ffn1_w_down, ffn2_w_gate=m_ffn2_w_gate,
                 ffn2_w_up=m_ffn2_w_up, ffn2_w_down=m_ffn2_w_down, w_in=m_w_in, pool_w=m_pool_w,
                 pool_scale=m_pool_scale, conv_w=m_conv_w, rpb=m_rpb, w_out=m_w_out, ln_g=m_ln_g, ln_b=m_ln_b)
    mom_v = dict(ffn1_w_gate=v_ffn1_w_gate, ffn1_w_up=v_ffn1_w_up, ffn1_w_down=v_ffn1_w_down, ffn2_w_gate=v_ffn2_w_gate,
                 ffn2_w_up=v_ffn2_w_up, ffn2_w_down=v_ffn2_w_down, w_in=v_w_in, pool_w=v_pool_w,
                 pool_scale=v_pool_scale, conv_w=v_conv_w, rpb=v_rpb, w_out=v_w_out, ln_g=v_ln_g, ln_b=v_ln_b)
    order = list(weights)
    L = ffn1_w_gate.shape[0]
    xi, yi, ci = lax.axis_index("x"), lax.axis_index("y"), lax.axis_index("c")
    q_me = 2 * xi + yi
    x2 = x[0]
    target = loss_target[0]
    D = x2.shape[1]
    n_in = w_in.shape[-1]

    small = jnp.concatenate([ln_g.reshape(-1), ln_b.reshape(-1), conv_w.reshape(-1)])
    n_small = small.shape[0]
    small_rows = -(-n_small // (8 * 128)) * 8
    small = jnp.pad(small, (0, small_rows * 128 - n_small)).reshape(small_rows, 128)
    small_all = _gather_small(small).reshape(NQ, small_rows * 128)[:, :n_small]
    dq4 = D // NQ
    n_ln = L * 3 * dq4
    ln_g_all = small_all[:, :n_ln].reshape(NQ, L, 3, dq4).transpose(1, 2, 0, 3).reshape(L, 3, D)
    ln_b_all = small_all[:, n_ln:2 * n_ln].reshape(NQ, L, 3, dq4).transpose(1, 2, 0, 3).reshape(L, 3, D)
    conv_all = small_all[:, 2 * n_ln:].reshape(NQ, L, 3, D_CONV // NQ).transpose(1, 2, 0, 3).reshape(L, 3, D_CONV)

    def layer_shards(l):
        return [w[l].astype(BF) for w in (ffn1_w_gate, ffn1_w_up, ffn1_w_down, w_in, w_out, ffn2_w_gate, ffn2_w_up,
                                          ffn2_w_down)]

    def own_quarter(bufs, shards):
        return [lax.dynamic_update_slice(b, s[None], (q_me,) + (0,) * s.ndim) for b, s in zip(bufs, shards)]

    shards = [layer_shards(l) for l in range(L)]
    landed = _run_alone(_gather_ici(shards[0][:3]), "gather_ici")
    weights_of = [own_quarter(_run_alone(_gather_d2d(landed), "gather_d2d"), shards[0][:3])] + [None] * (L - 1)

    onehot_np, mask_np = _bias_constants()
    onehot, onehot_t, mask = jnp.asarray(onehot_np, BF), jnp.asarray(onehot_np.T.copy(), BF), jnp.asarray(mask_np)
    ng = len(POOL_WINDOWS)
    pg = D_POOL // ng
    saved = []
    h = x2
    hb = x2.astype(BF)
    for l in range(L):
        nxt = shards[l + 1] if l + 1 < L else None
        wg1, wu1, wd1 = weights_of[l][:3]
        eye = jnp.eye(ng, dtype=F32)
        wblk = (pool_w[l][:, :, None, :] * eye[:, None, :, None]).reshape(D_POOL, D_POOL).astype(BF)
        vec = jnp.concatenate([pool_scale[l][None], conv_all[l], jnp.zeros((4, D_POOL), F32)], axis=0)
        bias = _bias_table(rpb[l], onehot, mask)
        lg = [ln_g_all[l, j][None] for j in range(3)]
        lb = [ln_b_all[l, j][None] for j in range(3)]
        if l == 0:
            (x1, x1b, z1, g1, u1), got = _ffn_fwd(h, wg1, wu1, wd1, lg[0], lb[0], rider=_gather_ici(shards[0][3:]))
            weights_of[0] += own_quarter(_run_alone(_gather_d2d(got), "gather_d2d_rest"), shards[0][3:])
            r_attn = _gather_ici(nxt[:3]) if nxt else None
        else:
            (x1, x1b, z1, g1, u1), got = _ffn_fwd(h, wg1, wu1, wd1, lg[0], lb[0],
                                                  rider=_gather_ici(nxt[:3]) if nxt else None)
            r_attn = _merge(_gather_d2d(got), _gather_ici(nxt[3:])) if nxt else None
        wc, wo, wg2, wu2, wd2 = weights_of[l][3:]
        pabc, qkv = _proj(x1b, wc)
        yab = _mixab_fwd(pabc, wblk, vec)
        (yc,), got = _attn_fwd(qkv, bias, rider=r_attn)
        xm, xmb, zm = _mixout_fwd(yab, yc, x1, wo, lg[1], lb[1])
        if l == 0:
            r_ffn2 = _merge(_gather_d2d(got), _gather_ici(nxt[3:])) if nxt else None
        else:
            r_ffn2 = _gather_d2d(got[3:]) if nxt else None
        (x3, x3b, z3, g3, u3), got2 = _ffn_fwd(xm, wg2, wu2, wd2, lg[2], lb[2], rider=r_ffn2)
        if nxt and l == 0:
            got2 = got2[:3] + _run_alone(_gather_d2d(got2[3:]), "gather_d2d_rest")
            weights_of[1] = own_quarter(got2, nxt)
        elif nxt:
            weights_of[l + 1] = own_quarter(got[:3] + got2, nxt)
        saved.append(dict(wblk=wblk, vec=vec, bias=bias, lg=lg, hb=hb, z1=z1, g1=g1, u1=u1, x1b=x1b, pabc=pabc,
                          qkv=qkv, yab=yab, yc=yc, zm=zm, xmb=xmb, z3=z3, g3=g3, u3=u3))
        h, hb = x3, x3b

    loss_tile, dh = _loss_head(h, target)
    loss = lax.psum(loss_tile[0, 0], ("x", "y", "c"))

    def add_chip(arrs, recv):
        chip = [_add_chip(g, r) for g, r in zip(arrs, recv)]
        return [cf for cf, _ in chip], [cb for _, cb in chip]

    def add_final(chip_f, from_chips):
        return _join_halves([_add_final(cf, r) for cf, r in zip(chip_f, from_chips)])

    per_layer = [[None] * 6 for _ in range(L)]
    g_small = dict(pool_w=[None] * L, pool_scale=[None] * L, conv_w=[None] * L, rpb=[None] * L, ln_g=[None] * L,
                   ln_b=[None] * L)
    ffn1_g = None
    for l in reversed(range(L)):
        sv = saved[l]
        wg1, wu1, wd1, wc, wo, wg2, wu2, wd2 = weights_of[l]
        (dxm, df, dg, du, a, ln3), got = _ffn_bwd(dh, sv["z3"], sv["g3"], sv["u3"], wg2, wu2, wd2, sv["lg"][2],
                                                  rider=_swap_halves(ffn1_g) if ffn1_g else None)
        if ffn1_g:
            ffn1_f, ffn1_b = add_chip(ffn1_g, got)
        ffn2_g = [_wgrad_gate_up(sv["xmb"], dg, du)[0], _wgrad_down(a, df)]
        (dres, dzb, dycat, ln2), got = _mixout_bwd(dxm, sv["zm"], wo, sv["lg"][1], rider=_swap_halves(ffn2_g))
        ffn2_f, ffn2_b = add_chip(ffn2_g, got)
        g_o = _wgrad_out(sv["yab"], sv["yc"], dzb)
        dpabc, dwblk, dvec = _mixab_bwd(sv["pabc"], dycat, sv["wblk"], sv["vec"])
        (dq, dk, dv, dbias), got = _attn_bwd(sv["qkv"], sv["bias"], dycat,
                                             rider=_scatter_chips(ffn1_b) if ffn1_g else None)
        dparts = [dpabc, dq, dk, dv]
        g_in, got = _wgrad_in(sv["x1b"], dparts, n_in, rider=add_final(ffn1_f, got) if ffn1_g else None)
        if ffn1_g:
            per_layer[l + 1][0:2] = got
        mix_g = [g_in, g_o]
        (dx1,), got = _proj_bwd(dres, dparts, wc, rider=_swap_halves(mix_g))
        mix_f, mix_b = add_chip(mix_g, got)
        (dh, df, dg, du, a, ln1), got = _ffn_bwd(dx1, sv["z1"], sv["g1"], sv["u1"], wg1, wu1, wd1, sv["lg"][0],
                                                 rider=_scatter_chips(ffn2_b + mix_b))
        g_gu, per_layer[l][2:6] = _wgrad_gate_up(sv["hb"], dg, du, rider=add_final(ffn2_f + mix_f, got))
        ffn1_g = [g_gu, _wgrad_down(a, df)]
        g_small["pool_w"][l] = jnp.stack([dwblk[gi * pg:(gi + 1) * pg, gi * pg:(gi + 1) * pg] for gi in range(ng)])
        g_small["pool_scale"][l] = dvec[0]
        g_small["conv_w"][l] = dvec[1:4]
        g_small["rpb"][l] = _bias_grad(dbias, onehot_t)
        g_small["ln_g"][l] = jnp.stack([ln1[0], ln2[0], ln3[0]])
        g_small["ln_b"][l] = jnp.stack([ln1[1], ln2[1], ln3[1]])
    grad_x = dh[None]

    def stacked(i, rows=None):
        parts = [per_layer[l][i] if rows is None else per_layer[l][i][rows[0]:rows[1]] for l in range(L)]
        return jnp.stack(parts)

    grads = dict(ffn2_w_gate=stacked(2, (0, D)), ffn2_w_up=stacked(2, (D, 2 * D)), ffn2_w_down=stacked(3),
                 w_in=stacked(4), w_out=stacked(5))
    delta, new_m, new_v = {}, {}, {}

    def adamw_group(names, tag, rider):
        res, got = _adamw_group([(weights[n], grads[n], mom_m[n], mom_v[n]) for n in names], tag, rider)
        for n, r in zip(names, res):
            delta[n], new_m[n], new_v[n] = r
        return got

    ffn1_f, ffn1_b = add_chip(ffn1_g, adamw_group(("w_in", "w_out"), "adamw_mix", _swap_halves(ffn1_g)))
    got = adamw_group(("ffn2_w_gate", "ffn2_w_up", "ffn2_w_down"), "adamw_ffn2", _scatter_chips(ffn1_b))
    per_layer[0][0:2] = _run_alone(add_final(ffn1_f, got), "rs_join_halves")
    grads.update(ffn1_w_gate=stacked(0, (0, D)), ffn1_w_up=stacked(0, (D, 2 * D)), ffn1_w_down=stacked(1))

    small_names = ("pool_w", "pool_scale", "conv_w", "rpb", "ln_g", "ln_b")
    small_full = {n: jnp.stack(g_small[n]) for n in small_names}
    vflat = jnp.concatenate([small_full[n].reshape(-1) for n in small_names])
    n_v = vflat.shape[0]
    v_cols = 1024
    v_rows = -(-n_v // (8 * v_cols)) * 8
    vsum = _allreduce_small(jnp.pad(vflat, (0, v_rows * v_cols - n_v)).reshape(v_rows, v_cols)).reshape(-1)
    off = 0
    for n in small_names:
        sz = int(np.prod(small_full[n].shape))
        grads[n] = vsum[off:off + sz].reshape(small_full[n].shape)
        off += sz
    for n in ("conv_w", "ln_g", "ln_b"):
        width = weights[n].shape[-1]
        grads[n] = lax.dynamic_slice_in_dim(grads[n], q_me * width, width, axis=2)

    for n in order:
        if n not in delta:
            delta[n], new_m[n], new_v[n] = _adamw(weights[n], grads[n], mom_m[n], mom_v[n])
    return (loss, grad_x, *[grads[n] for n in order], *[delta[n] for n in order], *[new_m[n] for n in order],
            *[new_v[n] for n in order])
```

```python
import numpy as np
import jax
import jax.numpy as jnp
from jax import lax
from jax.experimental import pallas as pl
from jax.experimental.pallas import tpu as pltpu

BF = jnp.bfloat16
F32 = jnp.float32
MESH = pl.DeviceIdType.MESH

DEPTH = 4
ALPHA = (2.0 * DEPTH) ** 0.25
LN_EPS = 1e-5
NEG_INF = -1e30
GRID_W = 64
NA_ROWS = 8
NA_COLS = 16
NA_HEADS = 8
HEAD_DIM = 64
D_POOL = 256
D_CONV = 256
D_NA = 512
HG = 4
LW = HG * HEAD_DIM
POOL_WINDOWS = (2, 4, 8, 16)
HALO = 8
ADAM_LR, ADAM_B1, ADAM_B2, ADAM_EPS, ADAM_WD, ADAM_STEP = 0.001, 0.9, 0.999, 1e-08, 0.01, 10
VMEM_LIMIT = 56 * 1024 * 1024
NQ = 4
WGRAD_TOKENS = 2048


def _cp(n_axes):
    return pltpu.CompilerParams(dimension_semantics=("arbitrary",) * n_axes, vmem_limit_bytes=VMEM_LIMIT)


def _full(shape):
    nd = len(shape)
    return pl.BlockSpec(shape, lambda *_: (0,) * nd)


def _quarters(arr):
    return pl.BlockSpec(arr.shape, lambda *_: (0, 0, 0), pipeline_mode=pl.Buffered(1))


ANY = pl.BlockSpec(memory_space=pl.ANY)


class _Rider:
    def __init__(self, tag, ins, outs, aliases, n, copies):
        self.tag, self.ins, self.outs, self.aliases, self.n, self.copies = tag, list(ins), list(outs), aliases, n, copies


def _merge(*riders):
    ins, outs, aliases, spans, n = [], [], {}, [], 0
    for r in riders:
        spans.append((len(ins), len(outs), n))
        aliases.update({len(ins) + i: len(outs) + j for i, j in r.aliases.items()})
        ins += r.ins
        outs += r.outs
        n += r.n

    def copies(r_in, r_out, ssem, rsem, base):
        cps = []
        for r, (i0, o0, s0) in zip(riders, spans):
            cps += r.copies(r_in[i0:i0 + len(r.ins)], r_out[o0:o0 + len(r.outs)], ssem, rsem, base + s0)
        return cps

    return _Rider("_".join(r.tag for r in riders), ins, outs, aliases, n, copies)


def _pcall(body, operands, *, name, grid, in_specs, out_specs, out_shape, scratch=(), rider=None, edges=None):
    n_in, n_out = len(in_specs), len(out_specs)
    params = dict(dimension_semantics=("arbitrary",) * len(grid), vmem_limit_bytes=VMEM_LIMIT)
    if rider is None:
        outs = pl.pallas_call(body, name=name, grid=grid, in_specs=in_specs, out_specs=out_specs, out_shape=out_shape,
                              scratch_shapes=list(scratch), compiler_params=pltpu.CompilerParams(**params))(*operands)
        return list(outs), []
    ni, no = len(rider.ins), len(rider.outs)
    first, last = edges

    def riding(*refs):
        rest = refs[n_in + ni + n_out + no:]
        cps = rider.copies(refs[n_in:n_in + ni], refs[n_in + ni + n_out:n_in + ni + n_out + no], rest[-2], rest[-1], 0)

        @pl.when(first())
        def _():
            for cp in cps:
                cp.start()

        body(*refs[:n_in], *refs[n_in + ni:n_in + ni + n_out], *rest[:-2])

        @pl.when(last())
        def _():
            for cp in cps:
                cp.wait()

    outs = pl.pallas_call(
        riding, name=f"{name}_{rider.tag}", grid=grid, in_specs=list(in_specs) + [ANY] * ni,
        out_specs=list(out_specs) + [ANY] * no, out_shape=list(out_shape) + rider.outs,
        scratch_shapes=list(scratch) + [pltpu.SemaphoreType.DMA((rider.n,)), pltpu.SemaphoreType.DMA((rider.n,))],
        input_output_aliases={n_in + i: n_out + j for i, j in rider.aliases.items()},
        compiler_params=pltpu.CompilerParams(has_side_effects=True, **params),
    )(*operands, *rider.ins)
    return list(outs[:n_out]), list(outs[n_out:])


def _edges_1d(n):
    return (lambda: pl.program_id(0) == 0), (lambda: pl.program_id(0) == n - 1)


def _edges_2d(n0, n1):
    return ((lambda: (pl.program_id(0) == 0) & (pl.program_id(1) == 0)),
            (lambda: (pl.program_id(0) == n0 - 1) & (pl.program_id(1) == n1 - 1)))


def _nt(a, b):
    return lax.dot_general(a, b, (((1,), (1,)), ((), ())), preferred_element_type=F32)


def _tn(a, b):
    return lax.dot_general(a, b, (((0,), (0,)), ((), ())), preferred_element_type=F32)


def _nn(a, b):
    return jnp.dot(a, b, preferred_element_type=F32)


def _ln_fwd(z, g, b):
    mu = jnp.mean(z, axis=-1, keepdims=True)
    zc = z - mu
    var = jnp.mean(zc * zc, axis=-1, keepdims=True)
    return zc * lax.rsqrt(var + LN_EPS) * g + b


def _ln_bwd(dy, z, g):
    mu = jnp.mean(z, axis=-1, keepdims=True)
    zc = z - mu
    var = jnp.mean(zc * zc, axis=-1, keepdims=True)
    rstd = lax.rsqrt(var + LN_EPS)
    xhat = zc * rstd
    gdy = dy * g
    m1 = jnp.mean(gdy, axis=-1, keepdims=True)
    m2 = jnp.mean(gdy * xhat, axis=-1, keepdims=True)
    return rstd * (gdy - m1 - xhat * m2), xhat


def _ffn_fwd(x, wg, wu, wd, lg, lb, rider=None):
    S, D = x.shape
    fq = wg.shape[-1]
    tm = min(512, S)

    def body(x_ref, wg_ref, wu_ref, wd_ref, lg_ref, lb_ref, xo_ref, xb_ref, z_ref, g_ref, u_ref):
        x = x_ref[...]
        xb = x.astype(BF)
        acc = jnp.zeros((tm, D), F32)
        for q in range(NQ):
            g = _nn(xb, wg_ref[q])
            u = _nn(xb, wu_ref[q])
            g_ref[q] = g.astype(BF)
            u_ref[q] = u.astype(BF)
            a = g * jax.nn.sigmoid(g) * u
            acc = acc + _nn(a.astype(BF), wd_ref[q])
        z = ALPHA * x + 0.5 * acc
        xo = _ln_fwd(z, lg_ref[...], lb_ref[...])
        z_ref[...] = z
        xo_ref[...] = xo
        xb_ref[...] = xo.astype(BF)

    row = pl.BlockSpec((tm, D), lambda i: (i, 0))
    qrow = pl.BlockSpec((NQ, tm, fq), lambda i: (0, i, 0))
    return _pcall(
        body, [x, wg, wu, wd, lg, lb], name="ffn_fwd", grid=(S // tm,),
        in_specs=[row, _quarters(wg), _quarters(wu), _quarters(wd), _full((1, D)), _full((1, D))],
        out_specs=[row, row, row, qrow, qrow],
        out_shape=[jax.ShapeDtypeStruct((S, D), F32), jax.ShapeDtypeStruct((S, D), BF),
                   jax.ShapeDtypeStruct((S, D), F32), jax.ShapeDtypeStruct((NQ, S, fq), BF),
                   jax.ShapeDtypeStruct((NQ, S, fq), BF)],
        rider=rider, edges=_edges_1d(S // tm))


def _ffn_bwd(dxo, z, g, u, wg, wu, wd, lg, rider=None):
    S, D = dxo.shape
    fq = wg.shape[-1]
    tm = min(256, S)
    nt = S // tm

    def body(dxo0_ref, z0_ref, dxo1_ref, z1_ref, g_ref, u_ref, wg_ref, wu_ref, wd_ref, lg_ref,
             dx_ref, df_ref, dg_ref, du_ref, a_ref, ln_ref, dz_ref):
        i = pl.program_id(0)

        @pl.when(i == 0)
        def _():
            dy0 = dxo0_ref[...]
            dz0, xhat0 = _ln_bwd(dy0, z0_ref[...], lg_ref[...])
            dz_ref[...] = dz0
            ln_ref[...] = jnp.zeros_like(ln_ref)
            ln_ref[0:1, :] += jnp.sum(dy0 * xhat0, axis=0, keepdims=True)
            ln_ref[1:2, :] += jnp.sum(dy0, axis=0, keepdims=True)

        dz = dz_ref[...]
        dfb = (0.5 * dz).astype(BF)
        df_ref[...] = dfb
        acc = ALPHA * dz
        for q in range(NQ):
            da = _nt(dfb, wd_ref[q])
            gg = g_ref[q].astype(F32)
            uu = u_ref[q].astype(F32)
            sg = jax.nn.sigmoid(gg)
            silu = gg * sg
            a_ref[q] = (silu * uu).astype(BF)
            dgb = (da * uu * (sg * (1.0 + gg * (1.0 - sg)))).astype(BF)
            dub = (da * silu).astype(BF)
            dg_ref[q] = dgb
            du_ref[q] = dub
            acc = acc + _nt(dgb, wg_ref[q]) + _nt(dub, wu_ref[q])
        dx_ref[...] = acc
        dy1 = dxo1_ref[...]
        dz1, xhat1 = _ln_bwd(dy1, z1_ref[...], lg_ref[...])
        real = (i < nt - 1).astype(F32)
        ln_ref[0:1, :] += real * jnp.sum(dy1 * xhat1, axis=0, keepdims=True)
        ln_ref[1:2, :] += real * jnp.sum(dy1, axis=0, keepdims=True)
        dz_ref[...] = dz1

    row = pl.BlockSpec((tm, D), lambda i: (i, 0))
    first = pl.BlockSpec((tm, D), lambda i: (0, 0))
    nxt = pl.BlockSpec((tm, D), lambda i: (jnp.minimum(i + 1, nt - 1), 0))
    qrow = pl.BlockSpec((NQ, tm, fq), lambda i: (0, i, 0))
    qshape = jax.ShapeDtypeStruct((NQ, S, fq), BF)
    return _pcall(
        body, [dxo, z, dxo, z, g, u, wg, wu, wd, lg], name="ffn_bwd", grid=(nt,),
        in_specs=[first, first, nxt, nxt, qrow, qrow, _quarters(wg), _quarters(wu), _quarters(wd), _full((1, D))],
        out_specs=[row, row, qrow, qrow, qrow, _full((8, D))],
        out_shape=[jax.ShapeDtypeStruct((S, D), F32), jax.ShapeDtypeStruct((S, D), BF), qshape, qshape, qshape,
                   jax.ShapeDtypeStruct((8, D), F32)],
        scratch=[pltpu.VMEM((tm, D), F32)],
        rider=rider, edges=_edges_1d(nt))


def _wgrad_gate_up(a, dg, du, rider=None):
    S, K = a.shape
    n = dg.shape[-1]
    ts = min(WGRAD_TOKENS, S)

    def body(a_ref, g_ref, u_ref, o_ref):
        @pl.when(pl.program_id(1) == 0)
        def _():
            o_ref[...] = jnp.zeros_like(o_ref)
        av = a_ref[...]
        o_ref[0:K, :] += _tn(av, g_ref[...])
        o_ref[K:2 * K, :] += _tn(av, u_ref[...])

    bspec = pl.BlockSpec((None, ts, n), lambda q, s: (q, s, 0))
    (out,), got = _pcall(
        body, [a, dg, du], name="wgrad_gate_up", grid=(NQ, S // ts),
        in_specs=[pl.BlockSpec((ts, K), lambda q, s: (s, 0)), bspec, bspec],
        out_specs=[pl.BlockSpec((None, 2 * K, n), lambda q, s: (q, 0, 0))],
        out_shape=[jax.ShapeDtypeStruct((NQ, 2 * K, n), F32)],
        rider=rider, edges=_edges_2d(NQ, S // ts))
    return out, got


def _wgrad_down(a, df):
    _, S, k = a.shape
    N = df.shape[1]
    ts = min(WGRAD_TOKENS, S)

    def body(a_ref, b_ref, o_ref):
        @pl.when(pl.program_id(1) == 0)
        def _():
            o_ref[...] = jnp.zeros_like(o_ref)
        o_ref[...] += _tn(a_ref[...], b_ref[...])

    return pl.pallas_call(
        body, name="wgrad_down", grid=(NQ, S // ts),
        in_specs=[pl.BlockSpec((None, ts, k), lambda q, s: (q, s, 0)), pl.BlockSpec((ts, N), lambda q, s: (s, 0))],
        out_specs=pl.BlockSpec((None, k, N), lambda q, s: (q, 0, 0)),
        out_shape=jax.ShapeDtypeStruct((NQ, k, N), F32),
        compiler_params=_cp(2),
    )(a, df)


def _wgrad_out(yab, yc, dzb):
    S, h = yab.shape
    D = dzb.shape[1]
    k = h // 2
    ts = min(WGRAD_TOKENS, S)

    def body(yab_ref, yc_ref, b_ref, o_ref):
        @pl.when(pl.program_id(0) == 0)
        def _():
            o_ref[...] = jnp.zeros_like(o_ref)
        b = b_ref[...]
        o_ref[0] += _tn(yab_ref[:, 0:k], b)
        o_ref[1] += _tn(yab_ref[:, k:h], b)
        o_ref[2] += _tn(yc_ref[:, 0:k], b)
        o_ref[3] += _tn(yc_ref[:, k:h], b)

    row = lambda w: pl.BlockSpec((ts, w), lambda s: (s, 0))
    return pl.pallas_call(
        body, name="wgrad_out", grid=(S // ts,),
        in_specs=[row(h), row(h), row(D)], out_specs=_full((NQ, k, D)),
        out_shape=jax.ShapeDtypeStruct((NQ, k, D), F32),
        compiler_params=_cp(1),
    )(yab, yc, dzb)


def _proj(xb, wc):
    S, D = xb.shape
    n = wc.shape[-1]
    n1 = D_POOL + 3 * D_CONV
    n2 = NQ * n - n1
    tm = min(1024, S)

    def body(x_ref, w_ref, p_ref, qkv_ref):
        x = x_ref[...]
        for q in range(NQ):
            r = _nn(x, w_ref[q])
            lo, hi = q * n, (q + 1) * n
            if hi <= n1:
                p_ref[:, lo:hi] = r
            elif lo >= n1:
                qkv_ref[:, lo - n1:hi - n1] = r.astype(BF)
            else:
                p_ref[:, lo:n1] = r[:, 0:n1 - lo]
                qkv_ref[:, 0:hi - n1] = r[:, n1 - lo:n].astype(BF)

    row = lambda w: pl.BlockSpec((tm, w), lambda i: (i, 0))
    return pl.pallas_call(
        body, name="mix_proj", grid=(S // tm,),
        in_specs=[row(D), _quarters(wc)],
        out_specs=[row(n1), row(n2)],
        out_shape=[jax.ShapeDtypeStruct((S, n1), F32), jax.ShapeDtypeStruct((S, n2), BF)],
        compiler_params=_cp(1),
    )(xb, wc)


def _mm_exact(a, b, name):
    def body(a_ref, b_ref, o_ref):
        o_ref[...] = jnp.dot(a_ref[...], b_ref[...].astype(F32), preferred_element_type=F32,
                             precision=lax.Precision.HIGHEST)

    return pl.pallas_call(
        body, name=name, in_specs=[_full(a.shape), _full(b.shape)], out_specs=_full((a.shape[0], b.shape[1])),
        out_shape=jax.ShapeDtypeStruct((a.shape[0], b.shape[1]), F32),
        compiler_params=pltpu.CompilerParams(vmem_limit_bytes=VMEM_LIMIT),
    )(a, b)


NB_ROWS = 2 * NA_ROWS - 1
NB_COLS = 2 * NA_COLS


def _bias_constants():
    c = np.arange(GRID_W)
    col_start = np.clip(c - NA_COLS // 2, 0, GRID_W - NA_COLS)
    valid = (c[None, :] >= col_start[:, None]) & (c[None, :] < col_start[:, None] + NA_COLS)
    dc = np.clip(c[None, :] - c[:, None], -(NA_COLS - 1), NA_COLS - 1) + (NA_COLS - 1)
    cq, ck = np.meshgrid(c, c, indexing="ij")
    onehot = np.zeros((HG, NB_COLS, GRID_W, HG, GRID_W), np.float32)
    for h in range(HG):
        onehot[h, dc[cq, ck], ck, h, cq] = 1.0
    mask_kq = np.where(valid.T, 0.0, NEG_INF).astype(np.float32)
    mask = np.tile(mask_kq, (NB_ROWS, HG))
    return onehot.reshape(HG * NB_COLS, GRID_W * LW), mask


def _bias_table(rpb, onehot, mask):
    ngr = NA_HEADS // HG
    r = rpb.reshape(ngr, HG, NB_ROWS, NB_COLS - 1).transpose(0, 2, 1, 3)
    r = jnp.pad(r, ((0, 0), (0, 0), (0, 0), (0, 1))).reshape(ngr * NB_ROWS, HG * NB_COLS)
    t = _mm_exact(r, onehot, "bias_expand")
    return t.reshape(ngr, NB_ROWS * GRID_W, LW) + mask[None]


def _bias_grad(dt, onehot_t):
    ngr = NA_HEADS // HG
    g = _mm_exact(dt.reshape(ngr * NB_ROWS, GRID_W * LW), onehot_t, "bias_reduce")
    g = g.reshape(ngr, NB_ROWS, HG, NB_COLS)[..., :NB_COLS - 1]
    return g.transpose(0, 2, 1, 3).reshape(NA_HEADS, NB_ROWS, NB_COLS - 1)


def _attn_rows(S):
    rows = S // GRID_W
    rb = min(16, rows)
    return rows, rb


def _head_masks():
    lane = lax.broadcasted_iota(jnp.int32, (GRID_W, LW), 1)
    return [(lane >= HEAD_DIM * h) & (lane < HEAD_DIM * (h + 1)) for h in range(HG)]


def _stack_heads(x, masks):
    zero = jnp.zeros_like(x)
    return jnp.concatenate([jnp.where(m, x, zero) for m in masks], axis=0)


def _unstack_heads(x2, masks):
    out = x2[0:GRID_W]
    for h in range(1, HG):
        out = jnp.where(masks[h], x2[h * GRID_W:(h + 1) * GRID_W], out)
    return out


def _attn_step(r, rows, q, k_ref, v_ref, b_ref, masks):
    rs = jnp.clip(r - NA_ROWS // 2, 0, rows - NA_ROWS)
    s0 = rs - r + (NA_ROWS - 1)
    q2 = _stack_heads(q, masks)
    ks = pl.ds(pl.multiple_of(rs * GRID_W, GRID_W), NA_ROWS * GRID_W)
    kb = k_ref[ks, :]
    vb = v_ref[ks, :]
    bs = pl.ds(pl.multiple_of(s0 * GRID_W, GRID_W), NA_ROWS * GRID_W)
    s = _nt(kb, q2) * (HEAD_DIM ** -0.5) + b_ref[0, bs, :]
    m = jnp.max(s, axis=0, keepdims=True)
    p = jnp.exp(s - m)
    p = p / jnp.sum(p, axis=0, keepdims=True)
    return p, q2, kb, vb, ks, bs


def _attn_fwd(qkv, bias, rider=None):
    S = qkv.shape[0]
    rows, rb = _attn_rows(S)
    tq = rb * GRID_W
    ngr = NA_HEADS // HG

    def body(q_ref, k_ref, v_ref, b_ref, o_ref):
        base = pl.program_id(1) * rb
        masks = _head_masks()

        def step(i, carry):
            qs = pl.ds(pl.multiple_of(i * GRID_W, GRID_W), GRID_W)
            p, _, _, vb, _, _ = _attn_step(base + i, rows, q_ref[qs, :], k_ref, v_ref, b_ref, masks)
            o_ref[qs, :] = _unstack_heads(_tn(p.astype(BF), vb), masks).astype(BF)
            return carry

        lax.fori_loop(0, rb, step, 0, unroll=rb)

    return _pcall(
        body, [qkv, qkv, qkv, bias], name="attn_fwd", grid=(ngr, rows // rb),
        in_specs=[pl.BlockSpec((tq, LW), lambda h, r: (r, h)),
                  pl.BlockSpec((S, LW), lambda h, r: (0, ngr + h)),
                  pl.BlockSpec((S, LW), lambda h, r: (0, 2 * ngr + h)),
                  pl.BlockSpec((1, bias.shape[1], LW), lambda h, r: (h, 0, 0))],
        out_specs=[pl.BlockSpec((tq, LW), lambda h, r: (r, h))],
        out_shape=[jax.ShapeDtypeStruct((S, D_NA), BF)],
        rider=rider, edges=_edges_2d(ngr, rows // rb))


def _attn_bwd(qkv, bias, dycat, rider=None):
    S = qkv.shape[0]
    rows, rb = _attn_rows(S)
    tq = rb * GRID_W
    ngr = NA_HEADS // HG
    scale = HEAD_DIM ** -0.5

    def body(q_ref, k_ref, v_ref, b_ref, do_ref, dq_ref, dk_ref, dv_ref, db_ref, dka_ref, dva_ref):
        base = pl.program_id(1) * rb
        last = pl.program_id(1) == pl.num_programs(1) - 1
        masks = _head_masks()

        @pl.when(pl.program_id(1) == 0)
        def _():
            dka_ref[...] = jnp.zeros_like(dka_ref)
            dva_ref[...] = jnp.zeros_like(dva_ref)
            db_ref[...] = jnp.zeros_like(db_ref)

        def step(i, carry):
            qs = pl.ds(pl.multiple_of(i * GRID_W, GRID_W), GRID_W)
            p, q2, kb, vb, ks, bs = _attn_step(base + i, rows, q_ref[qs, :], k_ref, v_ref, b_ref, masks)
            do2 = _stack_heads(do_ref[qs, :].astype(BF), masks)
            dp = _nt(vb, do2)
            ds = p * (dp - jnp.sum(p * dp, axis=0, keepdims=True))
            db_ref[0, bs, :] += ds
            dsb = ds.astype(BF)
            dq_ref[qs, :] = _unstack_heads(_tn(dsb, kb) * scale, masks).astype(BF)
            dka_ref[ks, :] += _nn(dsb, q2) * scale
            dva_ref[ks, :] += _nn(p.astype(BF), do2)
            return carry

        lax.fori_loop(0, rb, step, 0, unroll=rb)

        @pl.when(last)
        def _():
            dk_ref[...] = dka_ref[...].astype(BF)
            dv_ref[...] = dva_ref[...].astype(BF)

    nb = bias.shape[1]
    once = dict(pipeline_mode=pl.Buffered(1))
    nd = D_NA // LW
    return _pcall(
        body, [qkv, qkv, qkv, bias, dycat], name="attn_bwd", grid=(ngr, rows // rb),
        in_specs=[pl.BlockSpec((tq, LW), lambda h, r: (r, h)),
                  pl.BlockSpec((S, LW), lambda h, r: (0, ngr + h), **once),
                  pl.BlockSpec((S, LW), lambda h, r: (0, 2 * ngr + h), **once),
                  pl.BlockSpec((1, nb, LW), lambda h, r: (h, 0, 0)),
                  pl.BlockSpec((tq, LW), lambda h, r: (r, nd + h))],
        out_specs=[pl.BlockSpec((tq, LW), lambda h, r: (r, h)),
                   pl.BlockSpec((S, LW), lambda h, r: (0, h)),
                   pl.BlockSpec((S, LW), lambda h, r: (0, h)),
                   pl.BlockSpec((1, nb, LW), lambda h, r: (h, 0, 0))],
        out_shape=[jax.ShapeDtypeStruct((S, D_NA), BF)] * 3 + [jax.ShapeDtypeStruct((ngr, nb, LW), F32)],
        scratch=[pltpu.VMEM((S, LW), F32), pltpu.VMEM((S, LW), F32)],
        rider=rider, edges=_edges_2d(ngr, rows // rb))


def _halo_specs(tm, width, S):
    hb = tm // HALO
    last = S // HALO - 1
    return [pl.BlockSpec((tm, width), lambda i: (i, 0)),
            pl.BlockSpec((HALO, width), lambda i: (jnp.maximum(i * hb - 1, 0), 0)),
            pl.BlockSpec((HALO, width), lambda i: (jnp.minimum((i + 1) * hb, last), 0))]


def _with_halo(cur_ref, prev_ref, next_ref, i, nt):
    prev = jnp.where(i > 0, prev_ref[...], 0.0)
    nxt = jnp.where(i < nt - 1, next_ref[...], 0.0)
    return jnp.concatenate([prev, cur_ref[...], nxt], axis=0)


def _shift(a, k):
    n = a.shape[0]
    return pltpu.roll(a, k % n, 0)


def _pool_lanes(n):
    lane = lax.broadcasted_iota(jnp.int32, (n, D_POOL), 1)
    group = D_POOL // len(POOL_WINDOWS)
    return [lane < group * (j + 1) for j in range(len(POOL_WINDOWS) - 1)]


def _by_window(lanes, vals):
    return jnp.where(lanes[0], vals[0], jnp.where(lanes[1], vals[1], jnp.where(lanes[2], vals[2], vals[3])))


def _pool_count(lanes, t, S):
    back = _by_window(lanes, tuple(w // 2 for w in POOL_WINDOWS))
    lo = jnp.maximum(t - back, 0)
    hi = jnp.minimum(t + back, S)
    return jnp.maximum(hi - lo, 1).astype(F32)


def _pool_p(u, lanes, cnt):
    a = u + _shift(u, 1)
    b = _shift(a, 1) + _shift(a, -1)
    c = _shift(b, 2) + _shift(b, -2)
    d = _shift(c, 4) + _shift(c, -4)
    return _by_window(lanes, (a, b, c, d)) / cnt - u


def _mixab_fwd(pabc, wblk, vec):
    S = pabc.shape[0]
    tm = min(512, S)
    nt = S // tm
    n = tm + 2 * HALO
    tile = slice(HALO, HALO + tm)

    def body(cur_ref, prev_ref, next_ref, w_ref, vec_ref, o_ref):
        i = pl.program_id(0)
        ext = _with_halo(cur_ref, prev_ref, next_ref, i, nt)
        lanes = _pool_lanes(n)
        t = i * tm - HALO + lax.broadcasted_iota(jnp.int32, (n, D_POOL), 0)
        p = _pool_p(ext[:, 0:D_POOL], lanes, _pool_count(lanes, t, S))[tile]
        o_ref[:, 0:D_POOL] = (_nn(p.astype(BF), w_ref[...]) * vec_ref[0:1, :]).astype(BF)
        zc = ext[:, 512:768] * ext[:, 768:1024]
        conv = vec_ref[1:2, :] * _shift(zc, 1) + vec_ref[2:3, :] * zc + vec_ref[3:4, :] * _shift(zc, -1)
        o_ref[:, D_POOL:D_POOL + D_CONV] = (ext[tile, 256:512] * conv[tile]).astype(BF)

    return pl.pallas_call(
        body, name="mixab_fwd", grid=(nt,),
        in_specs=_halo_specs(tm, 1024, S) + [_full((D_POOL, D_POOL)), _full((8, D_POOL))],
        out_specs=pl.BlockSpec((tm, D_POOL + D_CONV), lambda i: (i, 0)),
        out_shape=jax.ShapeDtypeStruct((S, D_POOL + D_CONV), BF),
        compiler_params=_cp(1),
    )(pabc, pabc, pabc, wblk, vec)


def _mixab_bwd(pabc, dycat, wblk, vec):
    S = pabc.shape[0]
    tm = min(512, S)
    nt = S // tm
    n = tm + 2 * HALO
    tile = slice(HALO, HALO + tm)

    def body(cur_ref, prev_ref, next_ref, dcur_ref, dprev_ref, dnext_ref, w_ref, vec_ref, o_ref, dw_ref, dvec_ref):
        i = pl.program_id(0)

        @pl.when(i == 0)
        def _():
            dw_ref[...] = jnp.zeros_like(dw_ref)
            dvec_ref[...] = jnp.zeros_like(dvec_ref)

        ext = _with_halo(cur_ref, prev_ref, next_ref, i, nt)
        dext = _with_halo(dcur_ref, dprev_ref, dnext_ref, i, nt)
        lanes = _pool_lanes(n)
        t = i * tm - HALO + lax.broadcasted_iota(jnp.int32, (n, D_POOL), 0)
        cnt = _pool_count(lanes, t, S)
        w = w_ref[...]
        scale = vec_ref[0:1, :]
        pb = _pool_p(ext[:, 0:D_POOL], lanes, cnt)[tile].astype(BF)
        dya = dext[:, 0:D_POOL]
        dvec_ref[0:1, :] += jnp.sum(dya[tile] * _nn(pb, w), axis=0, keepdims=True)
        dqb = (dya * scale).astype(BF)
        dw_ref[...] += _tn(pb, dqb[tile])
        dp = _nt(dqb, w)
        r = dp / cnt
        a = r + _shift(r, -1)
        b = _shift(a, 1) + _shift(a, -1)
        c = _shift(b, 2) + _shift(b, -2)
        d = _shift(c, 4) + _shift(c, -4)
        o_ref[:, 0:256] = (_by_window(lanes, (a, b, c, d)) - dp)[tile].astype(BF)
        gb, gc, hh = ext[:, 256:512], ext[:, 512:768], ext[:, 768:1024]
        zc = gc * hh
        zm, zp = _shift(zc, 1), _shift(zc, -1)
        w0, w1, w2 = vec_ref[1:2, :], vec_ref[2:3, :], vec_ref[3:4, :]
        dyb = dext[:, D_POOL:D_POOL + D_CONV]
        dconv = dyb * gb
        o_ref[:, 256:512] = (dyb * (w0 * zm + w1 * zc + w2 * zp))[tile].astype(BF)
        dzc = w0 * _shift(dconv, -1) + w1 * dconv + w2 * _shift(dconv, 1)
        o_ref[:, 512:768] = (dzc * hh)[tile].astype(BF)
        o_ref[:, 768:1024] = (dzc * gc)[tile].astype(BF)
        dct = dconv[tile]
        dvec_ref[1:2, :] += jnp.sum(dct * zm[tile], axis=0, keepdims=True)
        dvec_ref[2:3, :] += jnp.sum(dct * zc[tile], axis=0, keepdims=True)
        dvec_ref[3:4, :] += jnp.sum(dct * zp[tile], axis=0, keepdims=True)

    return pl.pallas_call(
        body, name="mixab_bwd", grid=(nt,),
        in_specs=_halo_specs(tm, 1024, S) + _halo_specs(tm, 512, S) + [_full((D_POOL, D_POOL)), _full((8, D_POOL))],
        out_specs=[pl.BlockSpec((tm, 1024), lambda i: (i, 0)), _full((D_POOL, D_POOL)), _full((8, D_POOL))],
        out_shape=[jax.ShapeDtypeStruct((S, 1024), BF), jax.ShapeDtypeStruct((D_POOL, D_POOL), F32),
                   jax.ShapeDtypeStruct((8, D_POOL), F32)],
        compiler_params=_cp(1),
    )(pabc, pabc, pabc, dycat, dycat, dycat, wblk, vec)


def _mixout_fwd(yab, yc, x, wo, lg, lb):
    S, D = x.shape
    tm = min(512, S)
    h = yab.shape[1]
    k = h // 2

    def body(yab_ref, yc_ref, x_ref, w_ref, lg_ref, lb_ref, xo_ref, xb_ref, z_ref):
        y = (_nn(yab_ref[:, 0:k], w_ref[0]) + _nn(yab_ref[:, k:h], w_ref[1])
             + _nn(yc_ref[:, 0:k], w_ref[2]) + _nn(yc_ref[:, k:h], w_ref[3]))
        z = ALPHA * x_ref[...] + y
        xo = _ln_fwd(z, lg_ref[...], lb_ref[...])
        z_ref[...] = z
        xo_ref[...] = xo
        xb_ref[...] = xo.astype(BF)

    row = lambda w: pl.BlockSpec((tm, w), lambda i: (i, 0))
    return pl.pallas_call(
        body, name="mixout_fwd", grid=(S // tm,),
        in_specs=[row(h), row(h), row(D), _quarters(wo), _full((1, D)), _full((1, D))],
        out_specs=[row(D), row(D), row(D)],
        out_shape=[jax.ShapeDtypeStruct((S, D), F32), jax.ShapeDtypeStruct((S, D), BF),
                   jax.ShapeDtypeStruct((S, D), F32)],
        compiler_params=_cp(1),
    )(yab, yc, x, wo, lg, lb)


def _mixout_bwd(dxo, z, wo, lg, rider=None):
    S, D = dxo.shape
    k = wo.shape[-2]
    tm = min(512, S)
    nt = S // tm

    def body(dxo0_ref, z0_ref, dxo1_ref, z1_ref, w_ref, lg_ref, dres_ref, dzb_ref, dy_ref, ln_ref, dz_ref):
        i = pl.program_id(0)

        @pl.when(i == 0)
        def _():
            dy0 = dxo0_ref[...]
            dz0, xhat0 = _ln_bwd(dy0, z0_ref[...], lg_ref[...])
            dz_ref[...] = dz0
            ln_ref[...] = jnp.zeros_like(ln_ref)
            ln_ref[0:1, :] += jnp.sum(dy0 * xhat0, axis=0, keepdims=True)
            ln_ref[1:2, :] += jnp.sum(dy0, axis=0, keepdims=True)

        dz = dz_ref[...]
        dzb = dz.astype(BF)
        dres_ref[...] = ALPHA * dz
        dzb_ref[...] = dzb
        for q in range(NQ):
            dy_ref[:, q * k:(q + 1) * k] = _nt(dzb, w_ref[q])
        dy1 = dxo1_ref[...]
        dz1, xhat1 = _ln_bwd(dy1, z1_ref[...], lg_ref[...])
        real = (i < nt - 1).astype(F32)
        ln_ref[0:1, :] += real * jnp.sum(dy1 * xhat1, axis=0, keepdims=True)
        ln_ref[1:2, :] += real * jnp.sum(dy1, axis=0, keepdims=True)
        dz_ref[...] = dz1

    row = lambda w: pl.BlockSpec((tm, w), lambda i: (i, 0))
    first = pl.BlockSpec((tm, D), lambda i: (0, 0))
    nxt = pl.BlockSpec((tm, D), lambda i: (jnp.minimum(i + 1, nt - 1), 0))
    return _pcall(
        body, [dxo, z, dxo, z, wo, lg], name="mixout_bwd", grid=(nt,),
        in_specs=[first, first, nxt, nxt, _quarters(wo), _full((1, D))],
        out_specs=[row(D), row(D), row(NQ * k), _full((8, D))],
        out_shape=[jax.ShapeDtypeStruct((S, D), F32), jax.ShapeDtypeStruct((S, D), BF),
                   jax.ShapeDtypeStruct((S, NQ * k), F32), jax.ShapeDtypeStruct((8, D), F32)],
        scratch=[pltpu.VMEM((tm, D), F32)],
        rider=rider, edges=_edges_1d(nt))


def _take_cols(refs, lo, hi):
    parts, off = [], 0
    for r in refs:
        w = r.shape[1]
        a, b = max(lo, off), min(hi, off + w)
        if a < b:
            parts.append(r[:, a - off:b - off])
        off += w
    return parts[0] if len(parts) == 1 else jnp.concatenate(parts, axis=1)


def _proj_bwd(dres, dparts, wc, rider=None):
    S, D = dres.shape
    n = wc.shape[-1]
    tm = min(1024, S)
    np_ = len(dparts)

    def body(*refs):
        dres_ref, d_refs, w_ref, dx_ref = refs[0], refs[1:1 + np_], refs[1 + np_], refs[2 + np_]
        acc = dres_ref[...]
        for q in range(NQ):
            acc = acc + _nt(_take_cols(d_refs, q * n, (q + 1) * n), w_ref[q])
        dx_ref[...] = acc

    row = lambda w: pl.BlockSpec((tm, w), lambda i: (i, 0))
    return _pcall(
        body, [dres, *dparts, wc], name="mix_proj_bwd", grid=(S // tm,),
        in_specs=[row(D)] + [row(d.shape[1]) for d in dparts] + [_quarters(wc)],
        out_specs=[row(D)],
        out_shape=[jax.ShapeDtypeStruct((S, D), F32)],
        rider=rider, edges=_edges_1d(S // tm))


def _wgrad_in(a, dparts, n, rider=None):
    S, K = a.shape
    ts = min(WGRAD_TOKENS // 2, S)
    np_ = len(dparts)

    def body(*refs):
        a_ref, d_refs, o_ref = refs[0], refs[1:1 + np_], refs[1 + np_]

        @pl.when(pl.program_id(0) == 0)
        def _():
            o_ref[...] = jnp.zeros_like(o_ref)
        av = a_ref[...]
        for q in range(NQ):
            o_ref[q] += _tn(av, _take_cols(d_refs, q * n, (q + 1) * n))

    row = lambda w: pl.BlockSpec((ts, w), lambda s: (s, 0))
    (out,), got = _pcall(
        body, [a, *dparts], name="wgrad_in", grid=(S // ts,),
        in_specs=[row(K)] + [row(d.shape[1]) for d in dparts], out_specs=[_full((NQ, K, n))],
        out_shape=[jax.ShapeDtypeStruct((NQ, K, n), F32)],
        rider=rider, edges=_edges_1d(S // ts))
    return out, got


def _loss_head(y, target):
    S, D = y.shape
    tm = min(512, S)

    def body(y_ref, t_ref, l_ref, dy_ref):
        @pl.when(pl.program_id(0) == 0)
        def _():
            l_ref[...] = jnp.zeros_like(l_ref)
        e = y_ref[...] - t_ref[...]
        dy_ref[...] = e * (1.0 / D)
        part = jnp.sum(jnp.sum(e * e, axis=1, keepdims=True) * (1.0 / D), axis=0, keepdims=True)
        l_ref[...] += 0.5 * part

    row = pl.BlockSpec((tm, D), lambda i: (i, 0))
    return pl.pallas_call(
        body, name="loss_head", grid=(S // tm,),
        in_specs=[row, row], out_specs=[_full((8, 128)), row],
        out_shape=[jax.ShapeDtypeStruct((8, 128), F32), jax.ShapeDtypeStruct((S, D), F32)],
        compiler_params=_cp(1),
    )(y, target)


def _adamw_update(w, g, m, v):
    mn = ADAM_B1 * m + (1.0 - ADAM_B1) * g
    vn = ADAM_B2 * v + (1.0 - ADAM_B2) * (g * g)
    m_hat = mn / (1.0 - ADAM_B1 ** ADAM_STEP)
    v_hat = vn / (1.0 - ADAM_B2 ** ADAM_STEP)
    return -ADAM_LR * (m_hat / (jnp.sqrt(v_hat) + ADAM_EPS) + ADAM_WD * w), mn, vn


def _adamw(w, g, m, v):
    shape = w.shape
    cols = shape[-1]
    rows = int(np.prod(shape[:-1]))
    w2, g2, m2, v2 = (a.reshape(rows, cols) for a in (w, g, m, v))
    tr = rows
    for cand in (512, 352, 256):
        if rows > cand and rows % cand == 0:
            tr = cand
            break

    def body(w_ref, g_ref, m_ref, v_ref, d_ref, mo_ref, vo_ref):
        d_ref[...], mo_ref[...], vo_ref[...] = _adamw_update(w_ref[...], g_ref[...], m_ref[...], v_ref[...])

    spec = pl.BlockSpec((tr, cols), lambda i: (i, 0))
    outs = pl.pallas_call(
        body, name=f"adamw_{rows}x{cols}", grid=(rows // tr,),
        in_specs=[spec] * 4, out_specs=[spec] * 3,
        out_shape=[jax.ShapeDtypeStruct((rows, cols), F32)] * 3,
        compiler_params=_cp(1),
    )(w2, g2, m2, v2)
    return tuple(o.reshape(shape) for o in outs)


ADAMW_GROUP_STEPS = 16


def _adamw_group(items, name, rider=None):
    n = len(items)
    all_rows = [int(np.prod(w.shape[:-1])) for w, _, _, _ in items]
    steps = next(s for s in (ADAMW_GROUP_STEPS, 8, 4, 2, 1) if all(r % (8 * s) == 0 for r in all_rows))
    flat, specs, shapes = [], [], []
    for (w, g, m, v), rows in zip(items, all_rows):
        cols = w.shape[-1]
        tr = rows // steps
        flat += [a.reshape(rows, cols) for a in (w, g, m, v)]
        specs.append(pl.BlockSpec((tr, cols), lambda i: (i, 0)))
        shapes.append((w.shape, rows, cols))

    def body(*refs):
        ins, outs = refs[:4 * n], refs[4 * n:]
        for j in range(n):
            w_ref, g_ref, m_ref, v_ref = ins[4 * j:4 * j + 4]
            outs[3 * j][...], outs[3 * j + 1][...], outs[3 * j + 2][...] = _adamw_update(
                w_ref[...], g_ref[...], m_ref[...], v_ref[...])

    outs, got = _pcall(
        body, flat, name=name, grid=(steps,),
        in_specs=[s for s in specs for _ in range(4)], out_specs=[s for s in specs for _ in range(3)],
        out_shape=[jax.ShapeDtypeStruct((rows, cols), F32) for _, rows, cols in shapes for _ in range(3)],
        rider=rider, edges=_edges_1d(steps))
    return [tuple(o.reshape(shapes[j][0]) for o in outs[3 * j:3 * j + 3]) for j in range(n)], got


def _half_tile(h):
    return h if h <= 512 else 512


def _add_chip(g, recv):
    _, R, C = g.shape
    h = R // 2
    tr = _half_tile(h)
    nb = h // tr

    def body(a_ref, b_ref, o_ref, ob_ref):
        s = a_ref[...] + b_ref[...]
        ob_ref[...] = s.astype(BF)

        @pl.when(pl.program_id(1) == 2 * lax.axis_index("x") + lax.axis_index("y"))
        def _():
            o_ref[...] = s[0]

    half = pl.BlockSpec((1, tr, C), lambda i, q: (q, i, 0))
    mine = pl.BlockSpec((1, tr, C), lambda i, q: (q, lax.axis_index("c") * nb + i, 0))
    return pl.pallas_call(
        body, name=f"rs_add_chip_{R}x{C}", grid=(nb, NQ), in_specs=[mine, half],
        out_specs=[pl.BlockSpec((tr, C), lambda i, q: (i, 0)), half],
        out_shape=[jax.ShapeDtypeStruct((h, C), F32), jax.ShapeDtypeStruct((NQ, h, C), BF)],
        compiler_params=_cp(2),
    )(g, recv)


def _add_final(chip, recv):
    h, C = chip.shape
    tr = _half_tile(h)
    nb = h // tr

    def body(a_ref, b_ref, o_ref):
        s = a_ref[...]
        for j in range(3):
            s = s + b_ref[j].astype(F32)
        o_ref[...] = s

    return pl.pallas_call(
        body, name=f"rs_add_final_{h}x{C}", grid=(nb,),
        in_specs=[pl.BlockSpec((tr, C), lambda i: (i, 0)), pl.BlockSpec((3, tr, C), lambda i: (0, i, 0))],
        out_specs=pl.BlockSpec((tr, C), lambda i: (lax.axis_index("c") * nb + i, 0)),
        out_shape=jax.ShapeDtypeStruct((2 * h, C), F32),
        compiler_params=_cp(1),
    )(chip, recv)


COMM = pltpu.CompilerParams(has_side_effects=True)


def _place():
    x, y, c = lax.axis_index("x"), lax.axis_index("y"), lax.axis_index("c")
    chips = [(1 - x, y), (x, 1 - y), (1 - x, 1 - y)]
    return x, y, c, chips


def _half0(ref, c):
    n = ref.shape[0] // 2
    return ref.at[pl.ds(c * n, n)]


def _gather_ici(shards):
    n = len(shards)

    def copies(r_in, r_out, ssem, rsem, base):
        x, y, c, chips = _place()
        q = 2 * x + y
        return [pltpu.make_async_remote_copy(
            src_ref=_half0(r_in[i], c), dst_ref=_half0(r_out[i].at[q], c), send_sem=ssem.at[base + 3 * i + j],
            recv_sem=rsem.at[base + 3 * i + j], device_id=(*chip, c), device_id_type=MESH)
            for i in range(n) for j, chip in enumerate(chips)]

    return _Rider("ici", shards, [jax.ShapeDtypeStruct((NQ,) + s.shape, BF) for s in shards], {}, 3 * n, copies)


def _gather_d2d(bufs):
    n = len(bufs)

    def copies(r_in, r_out, ssem, rsem, base):
        x, y, c, chips = _place()
        return [pltpu.make_async_remote_copy(
            src_ref=_half0(r_in[i].at[2 * cx + cy], c), dst_ref=_half0(r_out[i].at[2 * cx + cy], c),
            send_sem=ssem.at[base + 3 * i + j], recv_sem=rsem.at[base + 3 * i + j], device_id=(x, y, 1 - c),
            device_id_type=MESH) for i in range(n) for j, (cx, cy) in enumerate(chips)]

    return _Rider("d2d", bufs, [jax.ShapeDtypeStruct(b.shape, b.dtype) for b in bufs], {i: i for i in range(n)},
                  3 * n, copies)


def _gather_small(small):
    sr = small.shape[0]

    def body(s_ref, o_ref, send_sems, recv_sems):
        x, y, c, chips = _place()
        o_ref[2 * x + y] = s_ref[...]
        cps = [pltpu.make_async_remote_copy(
            src_ref=s_ref, dst_ref=o_ref.at[2 * x + y], send_sem=send_sems.at[j], recv_sem=recv_sems.at[j],
            device_id=(*chip, c), device_id_type=MESH) for j, chip in enumerate(chips)]
        for cp in cps:
            cp.start()
        for j, (cx, cy) in enumerate(chips):
            pltpu.make_async_remote_copy(
                src_ref=s_ref, dst_ref=o_ref.at[2 * cx + cy], send_sem=send_sems.at[j], recv_sem=recv_sems.at[j],
                device_id=(cx, cy, c), device_id_type=MESH).wait_recv()
        for cp in cps:
            cp.wait_send()

    vm = pl.BlockSpec(memory_space=pltpu.VMEM)
    return pl.pallas_call(
        body, name="gather_small", in_specs=[vm], out_specs=vm,
        out_shape=jax.ShapeDtypeStruct((NQ, sr, 128), F32),
        scratch_shapes=[pltpu.SemaphoreType.DMA((3,)), pltpu.SemaphoreType.DMA((3,))],
        compiler_params=COMM,
    )(small)


def _swap_halves(gs):
    n = len(gs)

    def copies(r_in, r_out, ssem, rsem, base):
        x, y, c, _ = _place()
        cps = []
        for i in range(n):
            h = r_in[i].shape[1] // 2
            cps.append(pltpu.make_async_remote_copy(
                src_ref=r_in[i].at[:, pl.ds((1 - c) * h, h), :], dst_ref=r_out[i], send_sem=ssem.at[base + i],
                recv_sem=rsem.at[base + i], device_id=(x, y, 1 - c), device_id_type=MESH))
        return cps

    return _Rider("swap", gs, [jax.ShapeDtypeStruct((NQ, g.shape[1] // 2, g.shape[2]), F32) for g in gs], {}, n,
                  copies)


def _scatter_chips(chips_b):
    n = len(chips_b)

    def copies(r_in, r_out, ssem, rsem, base):
        x, y, c, chips = _place()
        return [pltpu.make_async_remote_copy(
            src_ref=r_in[i].at[2 * cx + cy], dst_ref=r_out[i].at[j], send_sem=ssem.at[base + 3 * i + j],
            recv_sem=rsem.at[base + 3 * i + j], device_id=(cx, cy, c), device_id_type=MESH)
            for i in range(n) for j, (cx, cy) in enumerate(chips)]

    return _Rider("scatter", chips_b, [jax.ShapeDtypeStruct((3,) + s.shape[1:], BF) for s in chips_b], {}, 3 * n,
                  copies)


def _run_alone(rider, name):
    ni, no = len(rider.ins), len(rider.outs)

    def body(*refs):
        cps = rider.copies(refs[:ni], refs[ni:ni + no], refs[ni + no], refs[ni + no + 1], 0)
        for cp in cps:
            cp.start()
        for cp in cps:
            cp.wait()

    return list(pl.pallas_call(
        body, name=name, in_specs=[ANY] * ni, out_specs=[ANY] * no, out_shape=rider.outs,
        input_output_aliases=dict(rider.aliases),
        scratch_shapes=[pltpu.SemaphoreType.DMA((rider.n,)), pltpu.SemaphoreType.DMA((rider.n,))],
        compiler_params=COMM,
    )(*rider.ins))


def _join_halves(fs):
    n = len(fs)

    def copies(r_in, r_out, ssem, rsem, base):
        x, y, c, _ = _place()
        return [pltpu.make_async_remote_copy(
            src_ref=_half0(r_in[i], c), dst_ref=_half0(r_out[i], c), send_sem=ssem.at[base + i],
            recv_sem=rsem.at[base + i], device_id=(x, y, 1 - c), device_id_type=MESH) for i in range(n)]

    return _Rider("join", fs, [jax.ShapeDtypeStruct(f.shape, F32) for f in fs], {i: i for i in range(n)}, n, copies)


def _allreduce_small(v):
    r, W = v.shape

    def body(v_ref, o_ref, land_ref, send_sems, recv_sems):
        x, y, c, _ = _place()
        me = 4 * x + 2 * y + c
        cps = []
        for m in range(1, 8):
            to = (x ^ (m >> 2), y ^ ((m >> 1) & 1), c ^ (m & 1))
            cps.append(pltpu.make_async_remote_copy(
                src_ref=v_ref, dst_ref=land_ref.at[m - 1], send_sem=send_sems.at[m - 1], recv_sem=recv_sems.at[m - 1],
                device_id=to, device_id_type=MESH))
        for cp in cps:
            cp.start()
        for cp in cps:
            cp.wait()
        total = jnp.zeros((r, W), F32)
        for d in range(8):
            slot = jnp.maximum((me ^ d) - 1, 0)
            total = total + jnp.where(me == d, v_ref[...], land_ref[slot])
        o_ref[...] = total

    return pl.pallas_call(
        body, name="allreduce_small",
        in_specs=[pl.BlockSpec(memory_space=pltpu.VMEM)], out_specs=pl.BlockSpec(memory_space=pltpu.VMEM),
        out_shape=jax.ShapeDtypeStruct((r, W), F32),
        scratch_shapes=[pltpu.VMEM((7, r, W), F32), pltpu.SemaphoreType.DMA((7,)), pltpu.SemaphoreType.DMA((7,))],
        compiler_params=pltpu.CompilerParams(has_side_effects=True, vmem_limit_bytes=VMEM_LIMIT),
    )(v)


def kernel(x, ffn1_w_gate, ffn1_w_up, ffn1_w_down, ffn2_w_gate, ffn2_w_up, ffn2_w_down, w_in, pool_w, pool_scale, conv_w, rpb, w_out, ln_g, ln_b, loss_target, m_ffn1_w_gate, m_ffn1_w_up, m_ffn1_w_down, m_ffn2_w_gate, m_ffn2_w_up, m_ffn2_w_down, m_w_in, m_pool_w, m_pool_scale, m_conv_w, m_rpb, m_w_out, m_ln_g, m_ln_b, v_ffn1_w_gate, v_ffn1_w_up, v_ffn1_w_down, v_ffn2_w_gate, v_ffn2_w_up, v_ffn2_w_down, v_w_in, v_pool_w, v_pool_scale, v_conv_w, v_rpb, v_w_out, v_ln_g, v_ln_b):
    weights = dict(ffn1_w_gate=ffn1_w_gate, ffn1_w_up=ffn1_w_up, ffn1_w_down=ffn1_w_down, ffn2_w_gate=ffn2_w_gate,
                   ffn2_w_up=ffn2_w_up, ffn2_w_down=ffn2_w_down, w_in=w_in, pool_w=pool_w, pool_scale=pool_scale,
                   conv_w=conv_w, rpb=rpb, w_out=w_out, ln_g=ln_g, ln_b=ln_b)
    mom_m = dict(ffn1_w_gate=m_ffn1_w_gate, ffn1_w_up=m_ffn1_w_up, ffn1_w_down=m_ffn1_w_down, ffn2_w_gate=m_ffn2_w_gate,
                 ffn2_w_up=m_ffn2_w_up, ffn2_w_down=m_ffn2_w_down, w_in=m_w_in, pool_w=m_pool_w,
                 pool_scale=m_pool_scale, conv_w=m_conv_w, rpb=m_rpb, w_out=m_w_out, ln_g=m_ln_g, ln_b=m_ln_b)
    mom_v = dict(ffn1_w_gate=v_ffn1_w_gate, ffn1_w_up=v_ffn1_w_up, ffn1_w_down=v_ffn1_w_down, ffn2_w_gate=v_ffn2_w_gate,
                 ffn2_w_up=v_ffn2_w_up, ffn2_w_down=v_ffn2_w_down, w_in=v_w_in, pool_w=v_pool_w,
                 pool_scale=v_pool_scale, conv_w=v_conv_w, rpb=v_rpb, w_out=v_w_out, ln_g=v_ln_g, ln_b=v_ln_b)
    order = list(weights)
    L = ffn1_w_gate.shape[0]
    xi, yi, ci = lax.axis_index("x"), lax.axis_index("y"), lax.axis_index("c")
    q_me = 2 * xi + yi
    x2 = x[0]
    target = loss_target[0]
    D = x2.shape[1]
    n_in = w_in.shape[-1]

    small = jnp.concatenate([ln_g.reshape(-1), ln_b.reshape(-1), conv_w.reshape(-1)])
    n_small = small.shape[0]
    small_rows = -(-n_small // (8 * 128)) * 8
    small = jnp.pad(small, (0, small_rows * 128 - n_small)).reshape(small_rows, 128)
    small_all = _gather_small(small).reshape(NQ, small_rows * 128)[:, :n_small]
    dq4 = D // NQ
    n_ln = L * 3 * dq4
    ln_g_all = small_all[:, :n_ln].reshape(NQ, L, 3, dq4).transpose(1, 2, 0, 3).reshape(L, 3, D)
    ln_b_all = small_all[:, n_ln:2 * n_ln].reshape(NQ, L, 3, dq4).transpose(1, 2, 0, 3).reshape(L, 3, D)
    conv_all = small_all[:, 2 * n_ln:].reshape(NQ, L, 3, D_CONV // NQ).transpose(1, 2, 0, 3).reshape(L, 3, D_CONV)

    def layer_shards(l):
        return [w[l].astype(BF) for w in (ffn1_w_gate, ffn1_w_up, ffn1_w_down, w_in, w_out, ffn2_w_gate, ffn2_w_up,
                                          ffn2_w_down)]

    def own_quarter(bufs, shards):
        return [lax.dynamic_update_slice(b, s[None], (q_me,) + (0,) * s.ndim) for b, s in zip(bufs, shards)]

    shards = [layer_shards(l) for l in range(L)]
    landed = _run_alone(_gather_ici(shards[0][:3]), "gather_ici")
    weights_of = [own_quarter(_run_alone(_gather_d2d(landed), "gather_d2d"), shards[0][:3])] + [None] * (L - 1)

    onehot_np, mask_np = _bias_constants()
    onehot, onehot_t, mask = jnp.asarray(onehot_np, BF), jnp.asarray(onehot_np.T.copy(), BF), jnp.asarray(mask_np)
    ng = len(POOL_WINDOWS)
    pg = D_POOL // ng
    saved = []
    h = x2
    hb = x2.astype(BF)
    for l in range(L):
        nxt = shards[l + 1] if l + 1 < L else None
        wg1, wu1, wd1 = weights_of[l][:3]
        eye = jnp.eye(ng, dtype=F32)
        wblk = (pool_w[l][:, :, None, :] * eye[:, None, :, None]).reshape(D_POOL, D_POOL).astype(BF)
        vec = jnp.concatenate([pool_scale[l][None], conv_all[l], jnp.zeros((4, D_POOL), F32)], axis=0)
        bias = _bias_table(rpb[l], onehot, mask)
        lg = [ln_g_all[l, j][None] for j in range(3)]
        lb = [ln_b_all[l, j][None] for j in range(3)]
        if l == 0:
            (x1, x1b, z1, g1, u1), got = _ffn_fwd(h, wg1, wu1, wd1, lg[0], lb[0], rider=_gather_ici(shards[0][3:]))
            weights_of[0] += own_quarter(_run_alone(_gather_d2d(got), "gather_d2d_rest"), shards[0][3:])
            r_attn = _gather_ici(nxt[:3]) if nxt else None
        else:
            riders = [_gather_d2d(landed)] + ([_gather_ici(nxt[:3])] if nxt else [])
            (x1, x1b, z1, g1, u1), got = _ffn_fwd(h, wg1, wu1, wd1, lg[0], lb[0], rider=_merge(*riders))
            weights_of[l] += own_quarter(got[:5], shards[l][3:])
            r_attn = _gather_d2d(got[5:]) if nxt else None
        wc, wo, wg2, wu2, wd2 = weights_of[l][3:]
        pabc, qkv = _proj(x1b, wc)
        yab = _mixab_fwd(pabc, wblk, vec)
        (yc,), got = _attn_fwd(qkv, bias, rider=r_attn)
        xm, xmb, zm = _mixout_fwd(yab, yc, x1, wo, lg[1], lb[1])
        if l == 0:
            r_ffn2 = _merge(_gather_d2d(got), _gather_ici(nxt[3:])) if nxt else None
        else:
            r_ffn2 = _gather_ici(nxt[3:]) if nxt else None
        (x3, x3b, z3, g3, u3), got2 = _ffn_fwd(xm, wg2, wu2, wd2, lg[2], lb[2], rider=r_ffn2)
        if nxt and l == 0:
            weights_of[1], landed = own_quarter(got2[:3], nxt[:3]), got2[3:]
        elif nxt:
            weights_of[l + 1], landed = own_quarter(got, nxt[:3]), got2
        saved.append(dict(wblk=wblk, vec=vec, bias=bias, lg=lg, hb=hb, z1=z1, g1=g1, u1=u1, x1b=x1b, pabc=pabc,
                          qkv=qkv, yab=yab, yc=yc, zm=zm, xmb=xmb, z3=z3, g3=g3, u3=u3))
        h, hb = x3, x3b

    loss_tile, dh = _loss_head(h, target)
    loss = lax.psum(loss_tile[0, 0], ("x", "y", "c"))

    def add_chip(arrs, recv):
        chip = [_add_chip(g, r) for g, r in zip(arrs, recv)]
        return [cf for cf, _ in chip], [cb for _, cb in chip]

    def add_final(chip_f, from_chips):
        return _join_halves([_add_final(cf, r) for cf, r in zip(chip_f, from_chips)])

    per_layer = [[None] * 6 for _ in range(L)]
    g_small = dict(pool_w=[None] * L, pool_scale=[None] * L, conv_w=[None] * L, rpb=[None] * L, ln_g=[None] * L,
                   ln_b=[None] * L)
    ffn1_g = None
    for l in reversed(range(L)):
        sv = saved[l]
        wg1, wu1, wd1, wc, wo, wg2, wu2, wd2 = weights_of[l]
        (dxm, df, dg, du, a, ln3), got = _ffn_bwd(dh, sv["z3"], sv["g3"], sv["u3"], wg2, wu2, wd2, sv["lg"][2],
                                                  rider=_swap_halves(ffn1_g) if ffn1_g else None)
        if ffn1_g:
            ffn1_f, ffn1_b = add_chip(ffn1_g, got)
        ffn2_g = [_wgrad_gate_up(sv["xmb"], dg, du)[0], _wgrad_down(a, df)]
        (dres, dzb, dycat, ln2), got = _mixout_bwd(dxm, sv["zm"], wo, sv["lg"][1], rider=_swap_halves(ffn2_g))
        ffn2_f, ffn2_b = add_chip(ffn2_g, got)
        g_o = _wgrad_out(sv["yab"], sv["yc"], dzb)
        dpabc, dwblk, dvec = _mixab_bwd(sv["pabc"], dycat, sv["wblk"], sv["vec"])
        (dq, dk, dv, dbias), got = _attn_bwd(sv["qkv"], sv["bias"], dycat,
                                             rider=_scatter_chips(ffn1_b) if ffn1_g else None)
        dparts = [dpabc, dq, dk, dv]
        g_in, got = _wgrad_in(sv["x1b"], dparts, n_in, rider=add_final(ffn1_f, got) if ffn1_g else None)
        if ffn1_g:
            per_layer[l + 1][0:2] = got
        mix_g = [g_in, g_o]
        (dx1,), got = _proj_bwd(dres, dparts, wc, rider=_swap_halves(mix_g))
        mix_f, mix_b = add_chip(mix_g, got)
        (dh, df, dg, du, a, ln1), got = _ffn_bwd(dx1, sv["z1"], sv["g1"], sv["u1"], wg1, wu1, wd1, sv["lg"][0],
                                                 rider=_scatter_chips(ffn2_b + mix_b))
        g_gu, per_layer[l][2:6] = _wgrad_gate_up(sv["hb"], dg, du, rider=add_final(ffn2_f + mix_f, got))
        ffn1_g = [g_gu, _wgrad_down(a, df)]
        g_small["pool_w"][l] = jnp.stack([dwblk[gi * pg:(gi + 1) * pg, gi * pg:(gi + 1) * pg] for gi in range(ng)])
        g_small["pool_scale"][l] = dvec[0]
        g_small["conv_w"][l] = dvec[1:4]
        g_small["rpb"][l] = _bias_grad(dbias, onehot_t)
        g_small["ln_g"][l] = jnp.stack([ln1[0], ln2[0], ln3[0]])
        g_small["ln_b"][l] = jnp.stack([ln1[1], ln2[1], ln3[1]])
    grad_x = dh[None]

    def stacked(i, rows=None):
        parts = [per_layer[l][i] if rows is None else per_layer[l][i][rows[0]:rows[1]] for l in range(L)]
        return jnp.stack(parts)

    grads = dict(ffn2_w_gate=stacked(2, (0, D)), ffn2_w_up=stacked(2, (D, 2 * D)), ffn2_w_down=stacked(3),
                 w_in=stacked(4), w_out=stacked(5))
    delta, new_m, new_v = {}, {}, {}

    def adamw_group(names, tag, rider):
        res, got = _adamw_group([(weights[n], grads[n], mom_m[n], mom_v[n]) for n in names], tag, rider)
        for n, r in zip(names, res):
            delta[n], new_m[n], new_v[n] = r
        return got

    ffn1_f, ffn1_b = add_chip(ffn1_g, adamw_group(("w_in", "w_out"), "adamw_mix", _swap_halves(ffn1_g)))
    got = adamw_group(("ffn2_w_gate", "ffn2_w_up", "ffn2_w_down"), "adamw_ffn2", _scatter_chips(ffn1_b))
    per_layer[0][0:2] = _run_alone(add_final(ffn1_f, got), "rs_join_halves")
    grads.update(ffn1_w_gate=stacked(0, (0, D)), ffn1_w_up=stacked(0, (D, 2 * D)), ffn1_w_down=stacked(1))

    small_names = ("pool_w", "pool_scale", "conv_w", "rpb", "ln_g", "ln_b")
    small_full = {n: jnp.stack(g_small[n]) for n in small_names}
    vflat = jnp.concatenate([small_full[n].reshape(-1) for n in small_names])
    n_v = vflat.shape[0]
    v_cols = 1024
    v_rows = -(-n_v // (8 * v_cols)) * 8
    vsum = _allreduce_small(jnp.pad(vflat, (0, v_rows * v_cols - n_v)).reshape(v_rows, v_cols)).reshape(-1)
    off = 0
    for n in small_names:
        sz = int(np.prod(small_full[n].shape))
        grads[n] = vsum[off:off + sz].reshape(small_full[n].shape)
        off += sz
    for n in ("conv_w", "ln_g", "ln_b"):
        width = weights[n].shape[-1]
        grads[n] = lax.dynamic_slice_in_dim(grads[n], q_me * width, width, axis=2)

    for n in order:
        if n not in delta:
            delta[n], new_m[n], new_v[n] = _adamw(weights[n], grads[n], mom_m[n], mom_v[n])
    return (loss, grad_x, *[grads[n] for n in order], *[delta[n] for n in order], *[new_m[n] for n in order],
            *[new_v[n] for n in order])
```

```python
import numpy as np
import jax
import jax.numpy as jnp
from jax import lax
from jax.experimental import pallas as pl
from jax.experimental.pallas import tpu as pltpu

BF = jnp.bfloat16
F32 = jnp.float32
MESH = pl.DeviceIdType.MESH

DEPTH = 4
ALPHA = (2.0 * DEPTH) ** 0.25
LN_EPS = 1e-5
NEG_INF = -1e30
GRID_W = 64
NA_ROWS = 8
NA_COLS = 16
NA_HEADS = 8
HEAD_DIM = 64
D_POOL = 256
D_CONV = 256
D_NA = 512
HG = 4
LW = HG * HEAD_DIM
POOL_WINDOWS = (2, 4, 8, 16)
HALO = 8
ADAM_LR, ADAM_B1, ADAM_B2, ADAM_EPS, ADAM_WD, ADAM_STEP = 0.001, 0.9, 0.999, 1e-08, 0.01, 10
VMEM_LIMIT = 56 * 1024 * 1024
NQ = 4
WGRAD_TOKENS = 2048


def _cp(n_axes):
    return pltpu.CompilerParams(dimension_semantics=("arbitrary",) * n_axes, vmem_limit_bytes=VMEM_LIMIT)


def _full(shape):
    nd = len(shape)
    return pl.BlockSpec(shape, lambda *_: (0,) * nd)


def _quarters(arr):
    return pl.BlockSpec(arr.shape, lambda *_: (0, 0, 0), pipeline_mode=pl.Buffered(1))


ANY = pl.BlockSpec(memory_space=pl.ANY)


class _Rider:
    def __init__(self, tag, ins, outs, aliases, n, copies):
        self.tag, self.ins, self.outs, self.aliases, self.n, self.copies = tag, list(ins), list(outs), aliases, n, copies


def _merge(*riders):
    ins, outs, aliases, spans, n = [], [], {}, [], 0
    for r in riders:
        spans.append((len(ins), len(outs), n))
        aliases.update({len(ins) + i: len(outs) + j for i, j in r.aliases.items()})
        ins += r.ins
        outs += r.outs
        n += r.n

    def copies(r_in, r_out, ssem, rsem, base):
        cps = []
        for r, (i0, o0, s0) in zip(riders, spans):
            cps += r.copies(r_in[i0:i0 + len(r.ins)], r_out[o0:o0 + len(r.outs)], ssem, rsem, base + s0)
        return cps

    return _Rider("_".join(r.tag for r in riders), ins, outs, aliases, n, copies)


def _pcall(body, operands, *, name, grid, in_specs, out_specs, out_shape, scratch=(), rider=None, edges=None):
    n_in, n_out = len(in_specs), len(out_specs)
    params = dict(dimension_semantics=("arbitrary",) * len(grid), vmem_limit_bytes=VMEM_LIMIT)
    if rider is None:
        outs = pl.pallas_call(body, name=name, grid=grid, in_specs=in_specs, out_specs=out_specs, out_shape=out_shape,
                              scratch_shapes=list(scratch), compiler_params=pltpu.CompilerParams(**params))(*operands)
        return list(outs), []
    ni, no = len(rider.ins), len(rider.outs)
    first, last = edges

    def riding(*refs):
        rest = refs[n_in + ni + n_out + no:]
        cps = rider.copies(refs[n_in:n_in + ni], refs[n_in + ni + n_out:n_in + ni + n_out + no], rest[-2], rest[-1], 0)

        @pl.when(first())
        def _():
            for cp in cps:
                cp.start()

        body(*refs[:n_in], *refs[n_in + ni:n_in + ni + n_out], *rest[:-2])

        @pl.when(last())
        def _():
            for cp in cps:
                cp.wait()

    outs = pl.pallas_call(
        riding, name=f"{name}_{rider.tag}", grid=grid, in_specs=list(in_specs) + [ANY] * ni,
        out_specs=list(out_specs) + [ANY] * no, out_shape=list(out_shape) + rider.outs,
        scratch_shapes=list(scratch) + [pltpu.SemaphoreType.DMA((rider.n,)), pltpu.SemaphoreType.DMA((rider.n,))],
        input_output_aliases={n_in + i: n_out + j for i, j in rider.aliases.items()},
        compiler_params=pltpu.CompilerParams(has_side_effects=True, **params),
    )(*operands, *rider.ins)
    return list(outs[:n_out]), list(outs[n_out:])


def _edges_1d(n):
    return (lambda: pl.program_id(0) == 0), (lambda: pl.program_id(0) == n - 1)


def _edges_2d(n0, n1):
    return ((lambda: (pl.program_id(0) == 0) & (pl.program_id(1) == 0)),
            (lambda: (pl.program_id(0) == n0 - 1) & (pl.program_id(1) == n1 - 1)))


def _nt(a, b):
    return lax.dot_general(a, b, (((1,), (1,)), ((), ())), preferred_element_type=F32)


def _tn(a, b):
    return lax.dot_general(a, b, (((0,), (0,)), ((), ())), preferred_element_type=F32)


def _nn(a, b):
    return jnp.dot(a, b, preferred_element_type=F32)


def _ln_fwd(z, g, b):
    mu = jnp.mean(z, axis=-1, keepdims=True)
    zc = z - mu
    var = jnp.mean(zc * zc, axis=-1, keepdims=True)
    return zc * lax.rsqrt(var + LN_EPS) * g + b


def _ln_bwd(dy, z, g):
    mu = jnp.mean(z, axis=-1, keepdims=True)
    zc = z - mu
    var = jnp.mean(zc * zc, axis=-1, keepdims=True)
    rstd = lax.rsqrt(var + LN_EPS)
    xhat = zc * rstd
    gdy = dy * g
    m1 = jnp.mean(gdy, axis=-1, keepdims=True)
    m2 = jnp.mean(gdy * xhat, axis=-1, keepdims=True)
    return rstd * (gdy - m1 - xhat * m2), xhat


def _ffn_fwd(x, wg, wu, wd, lg, lb, rider=None):
    S, D = x.shape
    fq = wg.shape[-1]
    tm = min(512, S)

    def body(x_ref, wg_ref, wu_ref, wd_ref, lg_ref, lb_ref, xo_ref, xb_ref, z_ref, g_ref, u_ref):
        x = x_ref[...]
        xb = x.astype(BF)
        acc = jnp.zeros((tm, D), F32)
        for q in range(NQ):
            g = _nn(xb, wg_ref[q])
            u = _nn(xb, wu_ref[q])
            g_ref[q] = g.astype(BF)
            u_ref[q] = u.astype(BF)
            a = g * jax.nn.sigmoid(g) * u
            acc = acc + _nn(a.astype(BF), wd_ref[q])
        z = ALPHA * x + 0.5 * acc
        xo = _ln_fwd(z, lg_ref[...], lb_ref[...])
        z_ref[...] = z
        xo_ref[...] = xo
        xb_ref[...] = xo.astype(BF)

    row = pl.BlockSpec((tm, D), lambda i: (i, 0))
    qrow = pl.BlockSpec((NQ, tm, fq), lambda i: (0, i, 0))
    return _pcall(
        body, [x, wg, wu, wd, lg, lb], name="ffn_fwd", grid=(S // tm,),
        in_specs=[row, _quarters(wg), _quarters(wu), _quarters(wd), _full((1, D)), _full((1, D))],
        out_specs=[row, row, row, qrow, qrow],
        out_shape=[jax.ShapeDtypeStruct((S, D), F32), jax.ShapeDtypeStruct((S, D), BF),
                   jax.ShapeDtypeStruct((S, D), F32), jax.ShapeDtypeStruct((NQ, S, fq), BF),
                   jax.ShapeDtypeStruct((NQ, S, fq), BF)],
        rider=rider, edges=_edges_1d(S // tm))


def _ffn_bwd(dxo, z, g, u, wg, wu, wd, lg, rider=None):
    S, D = dxo.shape
    fq = wg.shape[-1]
    tm = min(256, S)
    nt = S // tm

    def body(dxo0_ref, z0_ref, dxo1_ref, z1_ref, g_ref, u_ref, wg_ref, wu_ref, wd_ref, lg_ref,
             dx_ref, df_ref, dg_ref, du_ref, a_ref, ln_ref, dz_ref):
        i = pl.program_id(0)

        @pl.when(i == 0)
        def _():
            dy0 = dxo0_ref[...]
            dz0, xhat0 = _ln_bwd(dy0, z0_ref[...], lg_ref[...])
            dz_ref[...] = dz0
            ln_ref[...] = jnp.zeros_like(ln_ref)
            ln_ref[0:1, :] += jnp.sum(dy0 * xhat0, axis=0, keepdims=True)
            ln_ref[1:2, :] += jnp.sum(dy0, axis=0, keepdims=True)

        dz = dz_ref[...]
        dfb = (0.5 * dz).astype(BF)
        df_ref[...] = dfb
        acc = ALPHA * dz
        for q in range(NQ):
            da = _nt(dfb, wd_ref[q])
            gg = g_ref[q].astype(F32)
            uu = u_ref[q].astype(F32)
            sg = jax.nn.sigmoid(gg)
            silu = gg * sg
            a_ref[q] = (silu * uu).astype(BF)
            dgb = (da * uu * (sg * (1.0 + gg * (1.0 - sg)))).astype(BF)
            dub = (da * silu).astype(BF)
            dg_ref[q] = dgb
            du_ref[q] = dub
            acc = acc + _nt(dgb, wg_ref[q]) + _nt(dub, wu_ref[q])
        dx_ref[...] = acc
        dy1 = dxo1_ref[...]
        dz1, xhat1 = _ln_bwd(dy1, z1_ref[...], lg_ref[...])
        real = (i < nt - 1).astype(F32)
        ln_ref[0:1, :] += real * jnp.sum(dy1 * xhat1, axis=0, keepdims=True)
        ln_ref[1:2, :] += real * jnp.sum(dy1, axis=0, keepdims=True)
        dz_ref[...] = dz1

    row = pl.BlockSpec((tm, D), lambda i: (i, 0))
    first = pl.BlockSpec((tm, D), lambda i: (0, 0))
    nxt = pl.BlockSpec((tm, D), lambda i: (jnp.minimum(i + 1, nt - 1), 0))
    qrow = pl.BlockSpec((NQ, tm, fq), lambda i: (0, i, 0))
    qshape = jax.ShapeDtypeStruct((NQ, S, fq), BF)
    return _pcall(
        body, [dxo, z, dxo, z, g, u, wg, wu, wd, lg], name="ffn_bwd", grid=(nt,),
        in_specs=[first, first, nxt, nxt, qrow, qrow, _quarters(wg), _quarters(wu), _quarters(wd), _full((1, D))],
        out_specs=[row, row, qrow, qrow, qrow, _full((8, D))],
        out_shape=[jax.ShapeDtypeStruct((S, D), F32), jax.ShapeDtypeStruct((S, D), BF), qshape, qshape, qshape,
                   jax.ShapeDtypeStruct((8, D), F32)],
        scratch=[pltpu.VMEM((tm, D), F32)],
        rider=rider, edges=_edges_1d(nt))


def _wgrad_gate_up(a, dg, du, rider=None):
    S, K = a.shape
    n = dg.shape[-1]
    ts = min(WGRAD_TOKENS, S)

    def body(a_ref, g_ref, u_ref, o_ref):
        @pl.when(pl.program_id(1) == 0)
        def _():
            o_ref[...] = jnp.zeros_like(o_ref)
        av = a_ref[...]
        o_ref[0:K, :] += _tn(av, g_ref[...])
        o_ref[K:2 * K, :] += _tn(av, u_ref[...])

    bspec = pl.BlockSpec((None, ts, n), lambda q, s: (q, s, 0))
    (out,), got = _pcall(
        body, [a, dg, du], name="wgrad_gate_up", grid=(NQ, S // ts),
        in_specs=[pl.BlockSpec((ts, K), lambda q, s: (s, 0)), bspec, bspec],
        out_specs=[pl.BlockSpec((None, 2 * K, n), lambda q, s: (q, 0, 0))],
        out_shape=[jax.ShapeDtypeStruct((NQ, 2 * K, n), F32)],
        rider=rider, edges=_edges_2d(NQ, S // ts))
    return out, got


def _wgrad_down(a, df):
    _, S, k = a.shape
    N = df.shape[1]
    ts = min(WGRAD_TOKENS, S)

    def body(a_ref, b_ref, o_ref):
        @pl.when(pl.program_id(1) == 0)
        def _():
            o_ref[...] = jnp.zeros_like(o_ref)
        o_ref[...] += _tn(a_ref[...], b_ref[...])

    return pl.pallas_call(
        body, name="wgrad_down", grid=(NQ, S // ts),
        in_specs=[pl.BlockSpec((None, ts, k), lambda q, s: (q, s, 0)), pl.BlockSpec((ts, N), lambda q, s: (s, 0))],
        out_specs=pl.BlockSpec((None, k, N), lambda q, s: (q, 0, 0)),
        out_shape=jax.ShapeDtypeStruct((NQ, k, N), F32),
        compiler_params=_cp(2),
    )(a, df)


def _wgrad_out(yab, yc, dzb):
    S, h = yab.shape
    D = dzb.shape[1]
    k = h // 2
    ts = min(WGRAD_TOKENS, S)

    def body(yab_ref, yc_ref, b_ref, o_ref):
        @pl.when(pl.program_id(0) == 0)
        def _():
            o_ref[...] = jnp.zeros_like(o_ref)
        b = b_ref[...]
        o_ref[0] += _tn(yab_ref[:, 0:k], b)
        o_ref[1] += _tn(yab_ref[:, k:h], b)
        o_ref[2] += _tn(yc_ref[:, 0:k], b)
        o_ref[3] += _tn(yc_ref[:, k:h], b)

    row = lambda w: pl.BlockSpec((ts, w), lambda s: (s, 0))
    return pl.pallas_call(
        body, name="wgrad_out", grid=(S // ts,),
        in_specs=[row(h), row(h), row(D)], out_specs=_full((NQ, k, D)),
        out_shape=jax.ShapeDtypeStruct((NQ, k, D), F32),
        compiler_params=_cp(1),
    )(yab, yc, dzb)


def _proj(xb, wc):
    S, D = xb.shape
    n = wc.shape[-1]
    n1 = D_POOL + 3 * D_CONV
    n2 = NQ * n - n1
    tm = min(1024, S)

    def body(x_ref, w_ref, p_ref, qkv_ref):
        x = x_ref[...]
        for q in range(NQ):
            r = _nn(x, w_ref[q])
            lo, hi = q * n, (q + 1) * n
            if hi <= n1:
                p_ref[:, lo:hi] = r
            elif lo >= n1:
                qkv_ref[:, lo - n1:hi - n1] = r.astype(BF)
            else:
                p_ref[:, lo:n1] = r[:, 0:n1 - lo]
                qkv_ref[:, 0:hi - n1] = r[:, n1 - lo:n].astype(BF)

    row = lambda w: pl.BlockSpec((tm, w), lambda i: (i, 0))
    return pl.pallas_call(
        body, name="mix_proj", grid=(S // tm,),
        in_specs=[row(D), _quarters(wc)],
        out_specs=[row(n1), row(n2)],
        out_shape=[jax.ShapeDtypeStruct((S, n1), F32), jax.ShapeDtypeStruct((S, n2), BF)],
        compiler_params=_cp(1),
    )(xb, wc)


def _mm_exact(a, b, name):
    def body(a_ref, b_ref, o_ref):
        o_ref[...] = jnp.dot(a_ref[...], b_ref[...].astype(F32), preferred_element_type=F32,
                             precision=lax.Precision.HIGHEST)

    return pl.pallas_call(
        body, name=name, in_specs=[_full(a.shape), _full(b.shape)], out_specs=_full((a.shape[0], b.shape[1])),
        out_shape=jax.ShapeDtypeStruct((a.shape[0], b.shape[1]), F32),
        compiler_params=pltpu.CompilerParams(vmem_limit_bytes=VMEM_LIMIT),
    )(a, b)


NB_ROWS = 2 * NA_ROWS - 1
NB_COLS = 2 * NA_COLS


def _bias_constants():
    c = np.arange(GRID_W)
    col_start = np.clip(c - NA_COLS // 2, 0, GRID_W - NA_COLS)
    valid = (c[None, :] >= col_start[:, None]) & (c[None, :] < col_start[:, None] + NA_COLS)
    dc = np.clip(c[None, :] - c[:, None], -(NA_COLS - 1), NA_COLS - 1) + (NA_COLS - 1)
    cq, ck = np.meshgrid(c, c, indexing="ij")
    onehot = np.zeros((HG, NB_COLS, GRID_W, HG, GRID_W), np.float32)
    for h in range(HG):
        onehot[h, dc[cq, ck], ck, h, cq] = 1.0
    mask_kq = np.where(valid.T, 0.0, NEG_INF).astype(np.float32)
    mask = np.tile(mask_kq, (NB_ROWS, HG))
    return onehot.reshape(HG * NB_COLS, GRID_W * LW), mask


def _bias_table(rpb, onehot, mask):
    ngr = NA_HEADS // HG
    r = rpb.reshape(ngr, HG, NB_ROWS, NB_COLS - 1).transpose(0, 2, 1, 3)
    r = jnp.pad(r, ((0, 0), (0, 0), (0, 0), (0, 1))).reshape(ngr * NB_ROWS, HG * NB_COLS)
    t = _mm_exact(r, onehot, "bias_expand")
    return t.reshape(ngr, NB_ROWS * GRID_W, LW) + mask[None]


def _bias_grad(dt, onehot_t):
    ngr = NA_HEADS // HG
    g = _mm_exact(dt.reshape(ngr * NB_ROWS, GRID_W * LW), onehot_t, "bias_reduce")
    g = g.reshape(ngr, NB_ROWS, HG, NB_COLS)[..., :NB_COLS - 1]
    return g.transpose(0, 2, 1, 3).reshape(NA_HEADS, NB_ROWS, NB_COLS - 1)


def _attn_rows(S):
    rows = S // GRID_W
    rb = min(16, rows)
    return rows, rb


def _head_masks():
    lane = lax.broadcasted_iota(jnp.int32, (GRID_W, LW), 1)
    return [(lane >= HEAD_DIM * h) & (lane < HEAD_DIM * (h + 1)) for h in range(HG)]


def _stack_heads(x, masks):
    zero = jnp.zeros_like(x)
    return jnp.concatenate([jnp.where(m, x, zero) for m in masks], axis=0)


def _unstack_heads(x2, masks):
    out = x2[0:GRID_W]
    for h in range(1, HG):
        out = jnp.where(masks[h], x2[h * GRID_W:(h + 1) * GRID_W], out)
    return out


def _attn_step(r, rows, q, k_ref, v_ref, b_ref, masks):
    rs = jnp.clip(r - NA_ROWS // 2, 0, rows - NA_ROWS)
    s0 = rs - r + (NA_ROWS - 1)
    q2 = _stack_heads(q, masks)
    ks = pl.ds(pl.multiple_of(rs * GRID_W, GRID_W), NA_ROWS * GRID_W)
    kb = k_ref[ks, :]
    vb = v_ref[ks, :]
    bs = pl.ds(pl.multiple_of(s0 * GRID_W, GRID_W), NA_ROWS * GRID_W)
    s = _nt(kb, q2) * (HEAD_DIM ** -0.5) + b_ref[0, bs, :]
    m = jnp.max(s, axis=0, keepdims=True)
    p = jnp.exp(s - m)
    p = p / jnp.sum(p, axis=0, keepdims=True)
    return p, q2, kb, vb, ks, bs


def _attn_fwd(qkv, bias, rider=None):
    S = qkv.shape[0]
    rows, rb = _attn_rows(S)
    tq = rb * GRID_W
    ngr = NA_HEADS // HG

    def body(q_ref, k_ref, v_ref, b_ref, o_ref):
        base = pl.program_id(1) * rb
        masks = _head_masks()

        def step(i, carry):
            qs = pl.ds(pl.multiple_of(i * GRID_W, GRID_W), GRID_W)
            p, _, _, vb, _, _ = _attn_step(base + i, rows, q_ref[qs, :], k_ref, v_ref, b_ref, masks)
            o_ref[qs, :] = _unstack_heads(_tn(p.astype(BF), vb), masks).astype(BF)
            return carry

        lax.fori_loop(0, rb, step, 0, unroll=rb)

    return _pcall(
        body, [qkv, qkv, qkv, bias], name="attn_fwd", grid=(ngr, rows // rb),
        in_specs=[pl.BlockSpec((tq, LW), lambda h, r: (r, h)),
                  pl.BlockSpec((S, LW), lambda h, r: (0, ngr + h)),
                  pl.BlockSpec((S, LW), lambda h, r: (0, 2 * ngr + h)),
                  pl.BlockSpec((1, bias.shape[1], LW), lambda h, r: (h, 0, 0))],
        out_specs=[pl.BlockSpec((tq, LW), lambda h, r: (r, h))],
        out_shape=[jax.ShapeDtypeStruct((S, D_NA), BF)],
        rider=rider, edges=_edges_2d(ngr, rows // rb))


def _attn_bwd(qkv, bias, dycat, rider=None):
    S = qkv.shape[0]
    rows, rb = _attn_rows(S)
    tq = rb * GRID_W
    ngr = NA_HEADS // HG
    scale = HEAD_DIM ** -0.5

    def body(q_ref, k_ref, v_ref, b_ref, do_ref, dq_ref, dk_ref, dv_ref, db_ref, dka_ref, dva_ref):
        base = pl.program_id(1) * rb
        last = pl.program_id(1) == pl.num_programs(1) - 1
        masks = _head_masks()

        @pl.when(pl.program_id(1) == 0)
        def _():
            dka_ref[...] = jnp.zeros_like(dka_ref)
            dva_ref[...] = jnp.zeros_like(dva_ref)
            db_ref[...] = jnp.zeros_like(db_ref)

        def step(i, carry):
            qs = pl.ds(pl.multiple_of(i * GRID_W, GRID_W), GRID_W)
            p, q2, kb, vb, ks, bs = _attn_step(base + i, rows, q_ref[qs, :], k_ref, v_ref, b_ref, masks)
            do2 = _stack_heads(do_ref[qs, :].astype(BF), masks)
            dp = _nt(vb, do2)
            ds = p * (dp - jnp.sum(p * dp, axis=0, keepdims=True))
            db_ref[0, bs, :] += ds
            dsb = ds.astype(BF)
            dq_ref[qs, :] = _unstack_heads(_tn(dsb, kb) * scale, masks).astype(BF)
            dka_ref[ks, :] += _nn(dsb, q2) * scale
            dva_ref[ks, :] += _nn(p.astype(BF), do2)
            return carry

        lax.fori_loop(0, rb, step, 0, unroll=rb)

        @pl.when(last)
        def _():
            dk_ref[...] = dka_ref[...].astype(BF)
            dv_ref[...] = dva_ref[...].astype(BF)

    nb = bias.shape[1]
    once = dict(pipeline_mode=pl.Buffered(1))
    nd = D_NA // LW
    return _pcall(
        body, [qkv, qkv, qkv, bias, dycat], name="attn_bwd", grid=(ngr, rows // rb),
        in_specs=[pl.BlockSpec((tq, LW), lambda h, r: (r, h)),
                  pl.BlockSpec((S, LW), lambda h, r: (0, ngr + h), **once),
                  pl.BlockSpec((S, LW), lambda h, r: (0, 2 * ngr + h), **once),
                  pl.BlockSpec((1, nb, LW), lambda h, r: (h, 0, 0)),
                  pl.BlockSpec((tq, LW), lambda h, r: (r, nd + h))],
        out_specs=[pl.BlockSpec((tq, LW), lambda h, r: (r, h)),
                   pl.BlockSpec((S, LW), lambda h, r: (0, h)),
                   pl.BlockSpec((S, LW), lambda h, r: (0, h)),
                   pl.BlockSpec((1, nb, LW), lambda h, r: (h, 0, 0))],
        out_shape=[jax.ShapeDtypeStruct((S, D_NA), BF)] * 3 + [jax.ShapeDtypeStruct((ngr, nb, LW), F32)],
        scratch=[pltpu.VMEM((S, LW), F32), pltpu.VMEM((S, LW), F32)],
        rider=rider, edges=_edges_2d(ngr, rows // rb))


def _halo_specs(tm, width, S):
    hb = tm // HALO
    last = S // HALO - 1
    return [pl.BlockSpec((tm, width), lambda i: (i, 0)),
            pl.BlockSpec((HALO, width), lambda i: (jnp.maximum(i * hb - 1, 0), 0)),
            pl.BlockSpec((HALO, width), lambda i: (jnp.minimum((i + 1) * hb, last), 0))]


def _with_halo(cur_ref, prev_ref, next_ref, i, nt):
    prev = jnp.where(i > 0, prev_ref[...], 0.0)
    nxt = jnp.where(i < nt - 1, next_ref[...], 0.0)
    return jnp.concatenate([prev, cur_ref[...], nxt], axis=0)


def _shift(a, k):
    n = a.shape[0]
    return pltpu.roll(a, k % n, 0)


def _pool_lanes(n):
    lane = lax.broadcasted_iota(jnp.int32, (n, D_POOL), 1)
    group = D_POOL // len(POOL_WINDOWS)
    return [lane < group * (j + 1) for j in range(len(POOL_WINDOWS) - 1)]


def _by_window(lanes, vals):
    return jnp.where(lanes[0], vals[0], jnp.where(lanes[1], vals[1], jnp.where(lanes[2], vals[2], vals[3])))


def _pool_count(lanes, t, S):
    back = _by_window(lanes, tuple(w // 2 for w in POOL_WINDOWS))
    lo = jnp.maximum(t - back, 0)
    hi = jnp.minimum(t + back, S)
    return jnp.maximum(hi - lo, 1).astype(F32)


def _pool_p(u, lanes, cnt):
    a = u + _shift(u, 1)
    b = _shift(a, 1) + _shift(a, -1)
    c = _shift(b, 2) + _shift(b, -2)
    d = _shift(c, 4) + _shift(c, -4)
    return _by_window(lanes, (a, b, c, d)) / cnt - u


def _mixab_fwd(pabc, wblk, vec):
    S = pabc.shape[0]
    tm = min(512, S)
    nt = S // tm
    n = tm + 2 * HALO
    tile = slice(HALO, HALO + tm)

    def body(cur_ref, prev_ref, next_ref, w_ref, vec_ref, o_ref):
        i = pl.program_id(0)
        ext = _with_halo(cur_ref, prev_ref, next_ref, i, nt)
        lanes = _pool_lanes(n)
        t = i * tm - HALO + lax.broadcasted_iota(jnp.int32, (n, D_POOL), 0)
        p = _pool_p(ext[:, 0:D_POOL], lanes, _pool_count(lanes, t, S))[tile]
        o_ref[:, 0:D_POOL] = (_nn(p.astype(BF), w_ref[...]) * vec_ref[0:1, :]).astype(BF)
        zc = ext[:, 512:768] * ext[:, 768:1024]
        conv = vec_ref[1:2, :] * _shift(zc, 1) + vec_ref[2:3, :] * zc + vec_ref[3:4, :] * _shift(zc, -1)
        o_ref[:, D_POOL:D_POOL + D_CONV] = (ext[tile, 256:512] * conv[tile]).astype(BF)

    return pl.pallas_call(
        body, name="mixab_fwd", grid=(nt,),
        in_specs=_halo_specs(tm, 1024, S) + [_full((D_POOL, D_POOL)), _full((8, D_POOL))],
        out_specs=pl.BlockSpec((tm, D_POOL + D_CONV), lambda i: (i, 0)),
        out_shape=jax.ShapeDtypeStruct((S, D_POOL + D_CONV), BF),
        compiler_params=_cp(1),
    )(pabc, pabc, pabc, wblk, vec)


def _mixab_bwd(pabc, dycat, wblk, vec):
    S = pabc.shape[0]
    tm = min(512, S)
    nt = S // tm
    n = tm + 2 * HALO
    tile = slice(HALO, HALO + tm)

    def body(cur_ref, prev_ref, next_ref, dcur_ref, dprev_ref, dnext_ref, w_ref, vec_ref, o_ref, dw_ref, dvec_ref):
        i = pl.program_id(0)

        @pl.when(i == 0)
        def _():
            dw_ref[...] = jnp.zeros_like(dw_ref)
            dvec_ref[...] = jnp.zeros_like(dvec_ref)

        ext = _with_halo(cur_ref, prev_ref, next_ref, i, nt)
        dext = _with_halo(dcur_ref, dprev_ref, dnext_ref, i, nt)
        lanes = _pool_lanes(n)
        t = i * tm - HALO + lax.broadcasted_iota(jnp.int32, (n, D_POOL), 0)
        cnt = _pool_count(lanes, t, S)
        w = w_ref[...]
        scale = vec_ref[0:1, :]
        pb = _pool_p(ext[:, 0:D_POOL], lanes, cnt)[tile].astype(BF)
        dya = dext[:, 0:D_POOL]
        dvec_ref[0:1, :] += jnp.sum(dya[tile] * _nn(pb, w), axis=0, keepdims=True)
        dqb = (dya * scale).astype(BF)
        dw_ref[...] += _tn(pb, dqb[tile])
        dp = _nt(dqb, w)
        r = dp / cnt
        a = r + _shift(r, -1)
        b = _shift(a, 1) + _shift(a, -1)
        c = _shift(b, 2) + _shift(b, -2)
        d = _shift(c, 4) + _shift(c, -4)
        o_ref[:, 0:256] = (_by_window(lanes, (a, b, c, d)) - dp)[tile].astype(BF)
        gb, gc, hh = ext[:, 256:512], ext[:, 512:768], ext[:, 768:1024]
        zc = gc * hh
        zm, zp = _shift(zc, 1), _shift(zc, -1)
        w0, w1, w2 = vec_ref[1:2, :], vec_ref[2:3, :], vec_ref[3:4, :]
        dyb = dext[:, D_POOL:D_POOL + D_CONV]
        dconv = dyb * gb
        o_ref[:, 256:512] = (dyb * (w0 * zm + w1 * zc + w2 * zp))[tile].astype(BF)
        dzc = w0 * _shift(dconv, -1) + w1 * dconv + w2 * _shift(dconv, 1)
        o_ref[:, 512:768] = (dzc * hh)[tile].astype(BF)
        o_ref[:, 768:1024] = (dzc * gc)[tile].astype(BF)
        dct = dconv[tile]
        dvec_ref[1:2, :] += jnp.sum(dct * zm[tile], axis=0, keepdims=True)
        dvec_ref[2:3, :] += jnp.sum(dct * zc[tile], axis=0, keepdims=True)
        dvec_ref[3:4, :] += jnp.sum(dct * zp[tile], axis=0, keepdims=True)

    return pl.pallas_call(
        body, name="mixab_bwd", grid=(nt,),
        in_specs=_halo_specs(tm, 1024, S) + _halo_specs(tm, 512, S) + [_full((D_POOL, D_POOL)), _full((8, D_POOL))],
        out_specs=[pl.BlockSpec((tm, 1024), lambda i: (i, 0)), _full((D_POOL, D_POOL)), _full((8, D_POOL))],
        out_shape=[jax.ShapeDtypeStruct((S, 1024), BF), jax.ShapeDtypeStruct((D_POOL, D_POOL), F32),
                   jax.ShapeDtypeStruct((8, D_POOL), F32)],
        compiler_params=_cp(1),
    )(pabc, pabc, pabc, dycat, dycat, dycat, wblk, vec)


def _mixout_fwd(yab, yc, x, wo, lg, lb):
    S, D = x.shape
    tm = min(512, S)
    h = yab.shape[1]
    k = h // 2

    def body(yab_ref, yc_ref, x_ref, w_ref, lg_ref, lb_ref, xo_ref, xb_ref, z_ref):
        y = (_nn(yab_ref[:, 0:k], w_ref[0]) + _nn(yab_ref[:, k:h], w_ref[1])
             + _nn(yc_ref[:, 0:k], w_ref[2]) + _nn(yc_ref[:, k:h], w_ref[3]))
        z = ALPHA * x_ref[...] + y
        xo = _ln_fwd(z, lg_ref[...], lb_ref[...])
        z_ref[...] = z
        xo_ref[...] = xo
        xb_ref[...] = xo.astype(BF)

    row = lambda w: pl.BlockSpec((tm, w), lambda i: (i, 0))
    return pl.pallas_call(
        body, name="mixout_fwd", grid=(S // tm,),
        in_specs=[row(h), row(h), row(D), _quarters(wo), _full((1, D)), _full((1, D))],
        out_specs=[row(D), row(D), row(D)],
        out_shape=[jax.ShapeDtypeStruct((S, D), F32), jax.ShapeDtypeStruct((S, D), BF),
                   jax.ShapeDtypeStruct((S, D), F32)],
        compiler_params=_cp(1),
    )(yab, yc, x, wo, lg, lb)


def _mixout_bwd(dxo, z, wo, lg, rider=None):
    S, D = dxo.shape
    k = wo.shape[-2]
    tm = min(512, S)
    nt = S // tm

    def body(dxo0_ref, z0_ref, dxo1_ref, z1_ref, w_ref, lg_ref, dres_ref, dzb_ref, dy_ref, ln_ref, dz_ref):
        i = pl.program_id(0)

        @pl.when(i == 0)
        def _():
            dy0 = dxo0_ref[...]
            dz0, xhat0 = _ln_bwd(dy0, z0_ref[...], lg_ref[...])
            dz_ref[...] = dz0
            ln_ref[...] = jnp.zeros_like(ln_ref)
            ln_ref[0:1, :] += jnp.sum(dy0 * xhat0, axis=0, keepdims=True)
            ln_ref[1:2, :] += jnp.sum(dy0, axis=0, keepdims=True)

        dz = dz_ref[...]
        dzb = dz.astype(BF)
        dres_ref[...] = ALPHA * dz
        dzb_ref[...] = dzb
        for q in range(NQ):
            dy_ref[:, q * k:(q + 1) * k] = _nt(dzb, w_ref[q])
        dy1 = dxo1_ref[...]
        dz1, xhat1 = _ln_bwd(dy1, z1_ref[...], lg_ref[...])
        real = (i < nt - 1).astype(F32)
        ln_ref[0:1, :] += real * jnp.sum(dy1 * xhat1, axis=0, keepdims=True)
        ln_ref[1:2, :] += real * jnp.sum(dy1, axis=0, keepdims=True)
        dz_ref[...] = dz1

    row = lambda w: pl.BlockSpec((tm, w), lambda i: (i, 0))
    first = pl.BlockSpec((tm, D), lambda i: (0, 0))
    nxt = pl.BlockSpec((tm, D), lambda i: (jnp.minimum(i + 1, nt - 1), 0))
    return _pcall(
        body, [dxo, z, dxo, z, wo, lg], name="mixout_bwd", grid=(nt,),
        in_specs=[first, first, nxt, nxt, _quarters(wo), _full((1, D))],
        out_specs=[row(D), row(D), row(NQ * k), _full((8, D))],
        out_shape=[jax.ShapeDtypeStruct((S, D), F32), jax.ShapeDtypeStruct((S, D), BF),
                   jax.ShapeDtypeStruct((S, NQ * k), F32), jax.ShapeDtypeStruct((8, D), F32)],
        scratch=[pltpu.VMEM((tm, D), F32)],
        rider=rider, edges=_edges_1d(nt))


def _take_cols(refs, lo, hi):
    parts, off = [], 0
    for r in refs:
        w = r.shape[1]
        a, b = max(lo, off), min(hi, off + w)
        if a < b:
            parts.append(r[:, a - off:b - off])
        off += w
    return parts[0] if len(parts) == 1 else jnp.concatenate(parts, axis=1)


def _proj_bwd(dres, dparts, wc, rider=None):
    S, D = dres.shape
    n = wc.shape[-1]
    tm = min(1024, S)
    np_ = len(dparts)

    def body(*refs):
        dres_ref, d_refs, w_ref, dx_ref = refs[0], refs[1:1 + np_], refs[1 + np_], refs[2 + np_]
        acc = dres_ref[...]
        for q in range(NQ):
            acc = acc + _nt(_take_cols(d_refs, q * n, (q + 1) * n), w_ref[q])
        dx_ref[...] = acc

    row = lambda w: pl.BlockSpec((tm, w), lambda i: (i, 0))
    return _pcall(
        body, [dres, *dparts, wc], name="mix_proj_bwd", grid=(S // tm,),
        in_specs=[row(D)] + [row(d.shape[1]) for d in dparts] + [_quarters(wc)],
        out_specs=[row(D)],
        out_shape=[jax.ShapeDtypeStruct((S, D), F32)],
        rider=rider, edges=_edges_1d(S // tm))


def _wgrad_in(a, dparts, n, rider=None):
    S, K = a.shape
    ts = min(WGRAD_TOKENS // 2, S)
    np_ = len(dparts)

    def body(*refs):
        a_ref, d_refs, o_ref = refs[0], refs[1:1 + np_], refs[1 + np_]

        @pl.when(pl.program_id(0) == 0)
        def _():
            o_ref[...] = jnp.zeros_like(o_ref)
        av = a_ref[...]
        for q in range(NQ):
            o_ref[q] += _tn(av, _take_cols(d_refs, q * n, (q + 1) * n))

    row = lambda w: pl.BlockSpec((ts, w), lambda s: (s, 0))
    (out,), got = _pcall(
        body, [a, *dparts], name="wgrad_in", grid=(S // ts,),
        in_specs=[row(K)] + [row(d.shape[1]) for d in dparts], out_specs=[_full((NQ, K, n))],
        out_shape=[jax.ShapeDtypeStruct((NQ, K, n), F32)],
        rider=rider, edges=_edges_1d(S // ts))
    return out, got


def _loss_head(y, target):
    S, D = y.shape
    tm = min(512, S)

    def body(y_ref, t_ref, l_ref, dy_ref):
        @pl.when(pl.program_id(0) == 0)
        def _():
            l_ref[...] = jnp.zeros_like(l_ref)
        e = y_ref[...] - t_ref[...]
        dy_ref[...] = e * (1.0 / D)
        part = jnp.sum(jnp.sum(e * e, axis=1, keepdims=True) * (1.0 / D), axis=0, keepdims=True)
        l_ref[...] += 0.5 * part

    row = pl.BlockSpec((tm, D), lambda i: (i, 0))
    return pl.pallas_call(
        body, name="loss_head", grid=(S // tm,),
        in_specs=[row, row], out_specs=[_full((8, 128)), row],
        out_shape=[jax.ShapeDtypeStruct((8, 128), F32), jax.ShapeDtypeStruct((S, D), F32)],
        compiler_params=_cp(1),
    )(y, target)


def _adamw_update(w, g, m, v):
    mn = ADAM_B1 * m + (1.0 - ADAM_B1) * g
    vn = ADAM_B2 * v + (1.0 - ADAM_B2) * (g * g)
    m_hat = mn / (1.0 - ADAM_B1 ** ADAM_STEP)
    v_hat = vn / (1.0 - ADAM_B2 ** ADAM_STEP)
    return -ADAM_LR * (m_hat / (jnp.sqrt(v_hat) + ADAM_EPS) + ADAM_WD * w), mn, vn


def _adamw(w, g, m, v):
    shape = w.shape
    cols = shape[-1]
    rows = int(np.prod(shape[:-1]))
    w2, g2, m2, v2 = (a.reshape(rows, cols) for a in (w, g, m, v))
    tr = rows
    for cand in (512, 352, 256):
        if rows > cand and rows % cand == 0:
            tr = cand
            break

    def body(w_ref, g_ref, m_ref, v_ref, d_ref, mo_ref, vo_ref):
        d_ref[...], mo_ref[...], vo_ref[...] = _adamw_update(w_ref[...], g_ref[...], m_ref[...], v_ref[...])

    spec = pl.BlockSpec((tr, cols), lambda i: (i, 0))
    outs = pl.pallas_call(
        body, name=f"adamw_{rows}x{cols}", grid=(rows // tr,),
        in_specs=[spec] * 4, out_specs=[spec] * 3,
        out_shape=[jax.ShapeDtypeStruct((rows, cols), F32)] * 3,
        compiler_params=_cp(1),
    )(w2, g2, m2, v2)
    return tuple(o.reshape(shape) for o in outs)


ADAMW_GROUP_STEPS = 16


def _adamw_group(items, name, rider=None):
    n = len(items)
    all_rows = [int(np.prod(w.shape[:-1])) for w, _, _, _ in items]
    steps = next(s for s in (ADAMW_GROUP_STEPS, 8, 4, 2, 1) if all(r % (8 * s) == 0 for r in all_rows))
    flat, specs, shapes = [], [], []
    for (w, g, m, v), rows in zip(items, all_rows):
        cols = w.shape[-1]
        tr = rows // steps
        flat += [a.reshape(rows, cols) for a in (w, g, m, v)]
        specs.append(pl.BlockSpec((tr, cols), lambda i: (i, 0)))
        shapes.append((w.shape, rows, cols))

    def body(*refs):
        ins, outs = refs[:4 * n], refs[4 * n:]
        for j in range(n):
            w_ref, g_ref, m_ref, v_ref = ins[4 * j:4 * j + 4]
            outs[3 * j][...], outs[3 * j + 1][...], outs[3 * j + 2][...] = _adamw_update(
                w_ref[...], g_ref[...], m_ref[...], v_ref[...])

    outs, got = _pcall(
        body, flat, name=name, grid=(steps,),
        in_specs=[s for s in specs for _ in range(4)], out_specs=[s for s in specs for _ in range(3)],
        out_shape=[jax.ShapeDtypeStruct((rows, cols), F32) for _, rows, cols in shapes for _ in range(3)],
        rider=rider, edges=_edges_1d(steps))
    return [tuple(o.reshape(shapes[j][0]) for o in outs[3 * j:3 * j + 3]) for j in range(n)], got


def _half_tile(h):
    return h if h <= 512 else 512


def _add_chip(g, recv):
    _, R, C = g.shape
    h = R // 2
    tr = _half_tile(h)
    nb = h // tr

    def body(a_ref, b_ref, o_ref, ob_ref):
        s = a_ref[...] + b_ref[...]
        ob_ref[...] = s.astype(BF)

        @pl.when(pl.program_id(1) == 2 * lax.axis_index("x") + lax.axis_index("y"))
        def _():
            o_ref[...] = s[0]

    half = pl.BlockSpec((1, tr, C), lambda i, q: (q, i, 0))
    mine = pl.BlockSpec((1, tr, C), lambda i, q: (q, lax.axis_index("c") * nb + i, 0))
    return pl.pallas_call(
        body, name=f"rs_add_chip_{R}x{C}", grid=(nb, NQ), in_specs=[mine, half],
        out_specs=[pl.BlockSpec((tr, C), lambda i, q: (i, 0)), half],
        out_shape=[jax.ShapeDtypeStruct((h, C), F32), jax.ShapeDtypeStruct((NQ, h, C), BF)],
        compiler_params=_cp(2),
    )(g, recv)


def _add_final(chip, recv):
    h, C = chip.shape
    tr = _half_tile(h)
    nb = h // tr

    def body(a_ref, b_ref, o_ref):
        s = a_ref[...]
        for j in range(3):
            s = s + b_ref[j].astype(F32)
        o_ref[...] = s

    return pl.pallas_call(
        body, name=f"rs_add_final_{h}x{C}", grid=(nb,),
        in_specs=[pl.BlockSpec((tr, C), lambda i: (i, 0)), pl.BlockSpec((3, tr, C), lambda i: (0, i, 0))],
        out_specs=pl.BlockSpec((tr, C), lambda i: (lax.axis_index("c") * nb + i, 0)),
        out_shape=jax.ShapeDtypeStruct((2 * h, C), F32),
        compiler_params=_cp(1),
    )(chip, recv)


COMM = pltpu.CompilerParams(has_side_effects=True)


def _place():
    x, y, c = lax.axis_index("x"), lax.axis_index("y"), lax.axis_index("c")
    chips = [(1 - x, y), (x, 1 - y), (1 - x, 1 - y)]
    return x, y, c, chips


def _half0(ref, c):
    n = ref.shape[0] // 2
    return ref.at[pl.ds(c * n, n)]


def _gather_ici(shards):
    n = len(shards)

    def copies(r_in, r_out, ssem, rsem, base):
        x, y, c, chips = _place()
        q = 2 * x + y
        return [pltpu.make_async_remote_copy(
            src_ref=_half0(r_in[i], c), dst_ref=_half0(r_out[i].at[q], c), send_sem=ssem.at[base + 3 * i + j],
            recv_sem=rsem.at[base + 3 * i + j], device_id=(*chip, c), device_id_type=MESH)
            for i in range(n) for j, chip in enumerate(chips)]

    return _Rider("ici", shards, [jax.ShapeDtypeStruct((NQ,) + s.shape, BF) for s in shards], {}, 3 * n, copies)


def _gather_d2d(bufs):
    n = len(bufs)

    def copies(r_in, r_out, ssem, rsem, base):
        x, y, c, chips = _place()
        return [pltpu.make_async_remote_copy(
            src_ref=_half0(r_in[i].at[2 * cx + cy], c), dst_ref=_half0(r_out[i].at[2 * cx + cy], c),
            send_sem=ssem.at[base + 3 * i + j], recv_sem=rsem.at[base + 3 * i + j], device_id=(x, y, 1 - c),
            device_id_type=MESH) for i in range(n) for j, (cx, cy) in enumerate(chips)]

    return _Rider("d2d", bufs, [jax.ShapeDtypeStruct(b.shape, b.dtype) for b in bufs], {i: i for i in range(n)},
                  3 * n, copies)


def _gather_small(small):
    sr = small.shape[0]

    def body(s_ref, o_ref, send_sems, recv_sems):
        x, y, c, chips = _place()
        o_ref[2 * x + y] = s_ref[...]
        cps = [pltpu.make_async_remote_copy(
            src_ref=s_ref, dst_ref=o_ref.at[2 * x + y], send_sem=send_sems.at[j], recv_sem=recv_sems.at[j],
            device_id=(*chip, c), device_id_type=MESH) for j, chip in enumerate(chips)]
        for cp in cps:
            cp.start()
        for j, (cx, cy) in enumerate(chips):
            pltpu.make_async_remote_copy(
                src_ref=s_ref, dst_ref=o_ref.at[2 * cx + cy], send_sem=send_sems.at[j], recv_sem=recv_sems.at[j],
                device_id=(cx, cy, c), device_id_type=MESH).wait_recv()
        for cp in cps:
            cp.wait_send()

    vm = pl.BlockSpec(memory_space=pltpu.VMEM)
    return pl.pallas_call(
        body, name="gather_small", in_specs=[vm], out_specs=vm,
        out_shape=jax.ShapeDtypeStruct((NQ, sr, 128), F32),
        scratch_shapes=[pltpu.SemaphoreType.DMA((3,)), pltpu.SemaphoreType.DMA((3,))],
        compiler_params=COMM,
    )(small)


def _swap_halves(gs):
    n = len(gs)

    def copies(r_in, r_out, ssem, rsem, base):
        x, y, c, _ = _place()
        cps = []
        for i in range(n):
            h = r_in[i].shape[1] // 2
            cps.append(pltpu.make_async_remote_copy(
                src_ref=r_in[i].at[:, pl.ds((1 - c) * h, h), :], dst_ref=r_out[i], send_sem=ssem.at[base + i],
                recv_sem=rsem.at[base + i], device_id=(x, y, 1 - c), device_id_type=MESH))
        return cps

    return _Rider("swap", gs, [jax.ShapeDtypeStruct((NQ, g.shape[1] // 2, g.shape[2]), F32) for g in gs], {}, n,
                  copies)


def _scatter_chips(chips_b):
    n = len(chips_b)

    def copies(r_in, r_out, ssem, rsem, base):
        x, y, c, chips = _place()
        return [pltpu.make_async_remote_copy(
            src_ref=r_in[i].at[2 * cx + cy], dst_ref=r_out[i].at[j], send_sem=ssem.at[base + 3 * i + j],
            recv_sem=rsem.at[base + 3 * i + j], device_id=(cx, cy, c), device_id_type=MESH)
            for i in range(n) for j, (cx, cy) in enumerate(chips)]

    return _Rider("scatter", chips_b, [jax.ShapeDtypeStruct((3,) + s.shape[1:], BF) for s in chips_b], {}, 3 * n,
                  copies)


def _run_alone(rider, name):
    ni, no = len(rider.ins), len(rider.outs)

    def body(*refs):
        cps = rider.copies(refs[:ni], refs[ni:ni + no], refs[ni + no], refs[ni + no + 1], 0)
        for cp in cps:
            cp.start()
        for cp in cps:
            cp.wait()

    return list(pl.pallas_call(
        body, name=name, in_specs=[ANY] * ni, out_specs=[ANY] * no, out_shape=rider.outs,
        input_output_aliases=dict(rider.aliases),
        scratch_shapes=[pltpu.SemaphoreType.DMA((rider.n,)), pltpu.SemaphoreType.DMA((rider.n,))],
        compiler_params=COMM,
    )(*rider.ins))


def _join_halves(fs):
    n = len(fs)

    def copies(r_in, r_out, ssem, rsem, base):
        x, y, c, _ = _place()
        return [pltpu.make_async_remote_copy(
            src_ref=_half0(r_in[i], c), dst_ref=_half0(r_out[i], c), send_sem=ssem.at[base + i],
            recv_sem=rsem.at[base + i], device_id=(x, y, 1 - c), device_id_type=MESH) for i in range(n)]

    return _Rider("join", fs, [jax.ShapeDtypeStruct(f.shape, F32) for f in fs], {i: i for i in range(n)}, n, copies)


def _allreduce_small(v):
    r, W = v.shape

    def body(v_ref, o_ref, land_ref, send_sems, recv_sems):
        x, y, c, _ = _place()
        me = 4 * x + 2 * y + c
        cps = []
        for m in range(1, 8):
            to = (x ^ (m >> 2), y ^ ((m >> 1) & 1), c ^ (m & 1))
            cps.append(pltpu.make_async_remote_copy(
                src_ref=v_ref, dst_ref=land_ref.at[m - 1], send_sem=send_sems.at[m - 1], recv_sem=recv_sems.at[m - 1],
                device_id=to, device_id_type=MESH))
        for cp in cps:
            cp.start()
        for cp in cps:
            cp.wait()
        total = jnp.zeros((r, W), F32)
        for d in range(8):
            slot = jnp.maximum((me ^ d) - 1, 0)
            total = total + jnp.where(me == d, v_ref[...], land_ref[slot])
        o_ref[...] = total

    return pl.pallas_call(
        body, name="allreduce_small",
        in_specs=[pl.BlockSpec(memory_space=pltpu.VMEM)], out_specs=pl.BlockSpec(memory_space=pltpu.VMEM),
        out_shape=jax.ShapeDtypeStruct((r, W), F32),
        scratch_shapes=[pltpu.VMEM((7, r, W), F32), pltpu.SemaphoreType.DMA((7,)), pltpu.SemaphoreType.DMA((7,))],
        compiler_params=pltpu.CompilerParams(has_side_effects=True, vmem_limit_bytes=VMEM_LIMIT),
    )(v)


def kernel(x, ffn1_w_gate, ffn1_w_up, ffn1_w_down, ffn2_w_gate, ffn2_w_up, ffn2_w_down, w_in, pool_w, pool_scale, conv_w, rpb, w_out, ln_g, ln_b, loss_target, m_ffn1_w_gate, m_ffn1_w_up, m_ffn1_w_down, m_ffn2_w_gate, m_ffn2_w_up, m_ffn2_w_down, m_w_in, m_pool_w, m_pool_scale, m_conv_w, m_rpb, m_w_out, m_ln_g, m_ln_b, v_ffn1_w_gate, v_ffn1_w_up, v_ffn1_w_down, v_ffn2_w_gate, v_ffn2_w_up, v_ffn2_w_down, v_w_in, v_pool_w, v_pool_scale, v_conv_w, v_rpb, v_w_out, v_ln_g, v_ln_b):
    weights = dict(ffn1_w_gate=ffn1_w_gate, ffn1_w_up=ffn1_w_up, ffn1_w_down=ffn1_w_down, ffn2_w_gate=ffn2_w_gate,
                   ffn2_w_up=ffn2_w_up, ffn2_w_down=ffn2_w_down, w_in=w_in, pool_w=pool_w, pool_scale=pool_scale,
                   conv_w=conv_w, rpb=rpb, w_out=w_out, ln_g=ln_g, ln_b=ln_b)
    mom_m = dict(ffn1_w_gate=m_ffn1_w_gate, ffn1_w_up=m_ffn1_w_up, ffn1_w_down=m_ffn1_w_down, ffn2_w_gate=m_ffn2_w_gate,
                 ffn2_w_up=m_ffn2_w_up, ffn2_w_down=m_ffn2_w_down, w_in=m_w_in, pool_w=m_pool_w,
                 pool_scale=m_pool_scale, conv_w=m_conv_w, rpb=m_rpb, w_out=m_w_out, ln_g=m_ln_g, ln_b=m_ln_b)
    mom_v = dict(ffn1_w_gate=v_ffn1_w_gate, ffn1_w_up=v_ffn1_w_up, ffn1_w_down=v_ffn1_w_down, ffn2_w_gate=v_ffn2_w_gate,
                 ffn2_w_up=v_ffn2_w_up, ffn2_w_down=v_ffn2_w_down, w_in=v_w_in, pool_w=v_pool_w,
                 pool_scale=v_pool_scale, conv_w=v_conv_w, rpb=v_rpb, w_out=v_w_out, ln_g=v_ln_g, ln_b=v_ln_b)
    order = list(weights)
    L = ffn1_w_gate.shape[0]
    xi, yi, ci = lax.axis_index("x"), lax.axis_index("y"), lax.axis_index("c")
    q_me = 2 * xi + yi
    x2 = x[0]
    target = loss_target[0]
    D = x2.shape[1]
    n_in = w_in.shape[-1]

    small = jnp.concatenate([ln_g.reshape(-1), ln_b.reshape(-1), conv_w.reshape(-1)])
    n_small = small.shape[0]
    small_rows = -(-n_small // (8 * 128)) * 8
    small = jnp.pad(small, (0, small_rows * 128 - n_small)).reshape(small_rows, 128)
    small_all = _gather_small(small).reshape(NQ, small_rows * 128)[:, :n_small]
    dq4 = D // NQ
    n_ln = L * 3 * dq4
    ln_g_all = small_all[:, :n_ln].reshape(NQ, L, 3, dq4).transpose(1, 2, 0, 3).reshape(L, 3, D)
    ln_b_all = small_all[:, n_ln:2 * n_ln].reshape(NQ, L, 3, dq4).transpose(1, 2, 0, 3).reshape(L, 3, D)
    conv_all = small_all[:, 2 * n_ln:].reshape(NQ, L, 3, D_CONV // NQ).transpose(1, 2, 0, 3).reshape(L, 3, D_CONV)

    def layer_shards(l):
        return [w[l].astype(BF) for w in (ffn1_w_gate, ffn1_w_up, ffn1_w_down, w_in, w_out, ffn2_w_gate, ffn2_w_up,
                                          ffn2_w_down)]

    def own_quarter(bufs, shards):
        return [lax.dynamic_update_slice(b, s[None], (q_me,) + (0,) * s.ndim) for b, s in zip(bufs, shards)]

    shards = [layer_shards(l) for l in range(L)]
    landed = _run_alone(_gather_ici(shards[0][:3]), "gather_ici")
    weights_of = [own_quarter(_run_alone(_gather_d2d(landed), "gather_d2d"), shards[0][:3])] + [None] * (L - 1)

    onehot_np, mask_np = _bias_constants()
    onehot, onehot_t, mask = jnp.asarray(onehot_np, BF), jnp.asarray(onehot_np.T.copy(), BF), jnp.asarray(mask_np)
    ng = len(POOL_WINDOWS)
    pg = D_POOL // ng
    saved = []
    h = x2
    hb = x2.astype(BF)
    for l in range(L):
        nxt = shards[l + 1] if l + 1 < L else None
        wg1, wu1, wd1 = weights_of[l][:3]
        eye = jnp.eye(ng, dtype=F32)
        wblk = (pool_w[l][:, :, None, :] * eye[:, None, :, None]).reshape(D_POOL, D_POOL).astype(BF)
        vec = jnp.concatenate([pool_scale[l][None], conv_all[l], jnp.zeros((4, D_POOL), F32)], axis=0)
        bias = _bias_table(rpb[l], onehot, mask)
        lg = [ln_g_all[l, j][None] for j in range(3)]
        lb = [ln_b_all[l, j][None] for j in range(3)]
        if l == 0:
            (x1, x1b, z1, g1, u1), got = _ffn_fwd(h, wg1, wu1, wd1, lg[0], lb[0], rider=_gather_ici(shards[0][3:]))
            weights_of[0] += own_quarter(_run_alone(_gather_d2d(got), "gather_d2d_rest"), shards[0][3:])
            r_attn = _gather_ici(nxt[:3]) if nxt else None
        else:
            riders = [_gather_d2d(landed)] + ([_gather_ici(nxt[:3])] if nxt else [])
            (x1, x1b, z1, g1, u1), got = _ffn_fwd(h, wg1, wu1, wd1, lg[0], lb[0], rider=_merge(*riders))
            weights_of[l] += own_quarter(got[:5], shards[l][3:])
            r_attn = _gather_d2d(got[5:]) if nxt else None
        wc, wo, wg2, wu2, wd2 = weights_of[l][3:]
        pabc, qkv = _proj(x1b, wc)
        yab = _mixab_fwd(pabc, wblk, vec)
        (yc,), got = _attn_fwd(qkv, bias, rider=r_attn)
        xm, xmb, zm = _mixout_fwd(yab, yc, x1, wo, lg[1], lb[1])
        if l == 0:
            r_ffn2 = _merge(_gather_d2d(got), _gather_ici(nxt[3:])) if nxt else None
        else:
            r_ffn2 = _gather_ici(nxt[3:]) if nxt else None
        (x3, x3b, z3, g3, u3), got2 = _ffn_fwd(xm, wg2, wu2, wd2, lg[2], lb[2], rider=r_ffn2)
        if nxt and l == 0:
            weights_of[1], landed = own_quarter(got2[:3], nxt[:3]), got2[3:]
        elif nxt:
            weights_of[l + 1], landed = own_quarter(got, nxt[:3]), got2
        saved.append(dict(wblk=wblk, vec=vec, bias=bias, lg=lg, hb=hb, z1=z1, g1=g1, u1=u1, x1b=x1b, pabc=pabc,
                          qkv=qkv, yab=yab, yc=yc, zm=zm, xmb=xmb, z3=z3, g3=g3, u3=u3))
        h, hb = x3, x3b

    loss_tile, dh = _loss_head(h, target)
    loss = lax.psum(loss_tile[0, 0], ("x", "y", "c"))

    def add_chip(arrs, recv):
        chip = [_add_chip(g, r) for g, r in zip(arrs, recv)]
        return [cf for cf, _ in chip], [cb for _, cb in chip]

    def add_final(chip_f, from_chips):
        return _join_halves([_add_final(cf, r) for cf, r in zip(chip_f, from_chips)])

    per_layer = [[None] * 6 for _ in range(L)]
    g_small = dict(pool_w=[None] * L, pool_scale=[None] * L, conv_w=[None] * L, rpb=[None] * L, ln_g=[None] * L,
                   ln_b=[None] * L)
    ffn1_g = None
    for l in reversed(range(L)):
        sv = saved[l]
        wg1, wu1, wd1, wc, wo, wg2, wu2, wd2 = weights_of[l]
        (dxm, df, dg, du, a, ln3), got = _ffn_bwd(dh, sv["z3"], sv["g3"], sv["u3"], wg2, wu2, wd2, sv["lg"][2],
                                                  rider=_swap_halves(ffn1_g) if ffn1_g else None)
        if ffn1_g:
            ffn1_f, ffn1_b = add_chip(ffn1_g, got)
        ffn2_g = [_wgrad_gate_up(sv["xmb"], dg, du)[0], _wgrad_down(a, df)]
        (dres, dzb, dycat, ln2), got = _mixout_bwd(dxm, sv["zm"], wo, sv["lg"][1], rider=_swap_halves(ffn2_g))
        ffn2_f, ffn2_b = add_chip(ffn2_g, got)
        g_o = _wgrad_out(sv["yab"], sv["yc"], dzb)
        dpabc, dwblk, dvec = _mixab_bwd(sv["pabc"], dycat, sv["wblk"], sv["vec"])
        (dq, dk, dv, dbias), got = _attn_bwd(sv["qkv"], sv["bias"], dycat,
                                             rider=_scatter_chips(ffn1_b) if ffn1_g else None)
        dparts = [dpabc, dq, dk, dv]
        g_in, got = _wgrad_in(sv["x1b"], dparts, n_in, rider=add_final(ffn1_f, got) if ffn1_g else None)
        if ffn1_g:
            per_layer[l + 1][0:2] = got
        mix_g = [g_in, g_o]
        (dx1,), got = _proj_bwd(dres, dparts, wc, rider=_swap_halves(mix_g))
        mix_f, mix_b = add_chip(mix_g, got)
        (dh, df, dg, du, a, ln1), got = _ffn_bwd(dx1, sv["z1"], sv["g1"], sv["u1"], wg1, wu1, wd1, sv["lg"][0],
                                                 rider=_scatter_chips(ffn2_b + mix_b))
        g_gu, per_layer[l][2:6] = _wgrad_gate_up(sv["hb"], dg, du, rider=add_final(ffn2_f + mix_f, got))
        ffn1_g = [g_gu, _wgrad_down(a, df)]
        g_small["pool_w"][l] = jnp.stack([dwblk[gi * pg:(gi + 1) * pg, gi * pg:(gi + 1) * pg] for gi in range(ng)])
        g_small["pool_scale"][l] = dvec[0]
        g_small["conv_w"][l] = dvec[1:4]
        g_small["rpb"][l] = _bias_grad(dbias, onehot_t)
        g_small["ln_g"][l] = jnp.stack([ln1[0], ln2[0], ln3[0]])
        g_small["ln_b"][l] = jnp.stack([ln1[1], ln2[1], ln3[1]])
    grad_x = dh[None]

    swapped = ("ffn1_w_gate", "ffn1_w_up", "ffn2_w_gate", "ffn2_w_up")

    def view(n, a):
        return jnp.swapaxes(a, 1, 2) if n in swapped else a

    def stacked(i, rows=None, swap=False):
        parts = [per_layer[l][i] if rows is None else per_layer[l][i][rows[0]:rows[1]] for l in range(L)]
        return jnp.stack([p.T for p in parts] if swap else parts)

    grads_v = dict(ffn2_w_gate=stacked(2, (0, D), True), ffn2_w_up=stacked(2, (D, 2 * D), True),
                   ffn2_w_down=stacked(3), w_in=stacked(4), w_out=stacked(5))
    delta, new_m, new_v = {}, {}, {}

    def adamw_group(names, tag, rider):
        res, got = _adamw_group([(view(n, weights[n]), grads_v[n], view(n, mom_m[n]), view(n, mom_v[n]))
                                 for n in names], tag, rider)
        for n, r in zip(names, res):
            delta[n], new_m[n], new_v[n] = (view(n, o) for o in r)
        return got

    ffn1_f, ffn1_b = add_chip(ffn1_g, adamw_group(("w_in", "w_out"), "adamw_mix", _swap_halves(ffn1_g)))
    got = adamw_group(("ffn2_w_gate", "ffn2_w_up", "ffn2_w_down"), "adamw_ffn2", _scatter_chips(ffn1_b))
    per_layer[0][0:2] = _run_alone(add_final(ffn1_f, got), "rs_join_halves")
    grads_v.update(ffn1_w_gate=stacked(0, (0, D), True), ffn1_w_up=stacked(0, (D, 2 * D), True),
                   ffn1_w_down=stacked(1))
    grads = {n: view(n, g) for n, g in grads_v.items()}

    small_names = ("pool_w", "pool_scale", "conv_w", "rpb", "ln_g", "ln_b")
    small_full = {n: jnp.stack(g_small[n]) for n in small_names}
    vflat = jnp.concatenate([small_full[n].reshape(-1) for n in small_names])
    n_v = vflat.shape[0]
    v_cols = 1024
    v_rows = -(-n_v // (8 * v_cols)) * 8
    vsum = _allreduce_small(jnp.pad(vflat, (0, v_rows * v_cols - n_v)).reshape(v_rows, v_cols)).reshape(-1)
    off = 0
    for n in small_names:
        sz = int(np.prod(small_full[n].shape))
        grads[n] = vsum[off:off + sz].reshape(small_full[n].shape)
        off += sz
    for n in ("conv_w", "ln_g", "ln_b"):
        width = weights[n].shape[-1]
        grads[n] = lax.dynamic_slice_in_dim(grads[n], q_me * width, width, axis=2)

    for n in order:
        if n not in delta:
            res = _adamw(view(n, weights[n]), view(n, grads[n]), view(n, mom_m[n]), view(n, mom_v[n]))
            delta[n], new_m[n], new_v[n] = (view(n, o) for o in res)
    return (loss, grad_x, *[grads[n] for n in order], *[delta[n] for n in order], *[new_m[n] for n in order],
            *[new_v[n] for n in order])
```

```python
import numpy as np
import jax
import jax.numpy as jnp
from jax import lax
from jax.experimental import pallas as pl
from jax.experimental.pallas import tpu as pltpu

BF = jnp.bfloat16
F32 = jnp.float32
MESH = pl.DeviceIdType.MESH

DEPTH = 4
ALPHA = (2.0 * DEPTH) ** 0.25
LN_EPS = 1e-5
NEG_INF = -1e30
GRID_W = 64
NA_ROWS = 8
NA_COLS = 16
NA_HEADS = 8
HEAD_DIM = 64
D_POOL = 256
D_CONV = 256
D_NA = 512
HG = 4
LW = HG * HEAD_DIM
POOL_WINDOWS = (2, 4, 8, 16)
HALO = 8
ADAM_LR, ADAM_B1, ADAM_B2, ADAM_EPS, ADAM_WD, ADAM_STEP = 0.001, 0.9, 0.999, 1e-08, 0.01, 10
VMEM_LIMIT = 56 * 1024 * 1024
NQ = 4
WGRAD_TOKENS = 2048


def _cp(n_axes):
    return pltpu.CompilerParams(dimension_semantics=("arbitrary",) * n_axes, vmem_limit_bytes=VMEM_LIMIT)


def _full(shape):
    nd = len(shape)
    return pl.BlockSpec(shape, lambda *_: (0,) * nd)


def _quarters(arr):
    return pl.BlockSpec(arr.shape, lambda *_: (0, 0, 0), pipeline_mode=pl.Buffered(1))


ANY = pl.BlockSpec(memory_space=pl.ANY)


class _Rider:
    def __init__(self, tag, ins, outs, aliases, n, copies):
        self.tag, self.ins, self.outs, self.aliases, self.n, self.copies = tag, list(ins), list(outs), aliases, n, copies


def _merge(*riders):
    ins, outs, aliases, spans, n = [], [], {}, [], 0
    for r in riders:
        spans.append((len(ins), len(outs), n))
        aliases.update({len(ins) + i: len(outs) + j for i, j in r.aliases.items()})
        ins += r.ins
        outs += r.outs
        n += r.n

    def copies(r_in, r_out, ssem, rsem, base):
        cps = []
        for r, (i0, o0, s0) in zip(riders, spans):
            cps += r.copies(r_in[i0:i0 + len(r.ins)], r_out[o0:o0 + len(r.outs)], ssem, rsem, base + s0)
        return cps

    return _Rider("_".join(r.tag for r in riders), ins, outs, aliases, n, copies)


def _pcall(body, operands, *, name, grid, in_specs, out_specs, out_shape, scratch=(), rider=None, edges=None):
    n_in, n_out = len(in_specs), len(out_specs)
    params = dict(dimension_semantics=("arbitrary",) * len(grid), vmem_limit_bytes=VMEM_LIMIT)
    if rider is None:
        outs = pl.pallas_call(body, name=name, grid=grid, in_specs=in_specs, out_specs=out_specs, out_shape=out_shape,
                              scratch_shapes=list(scratch), compiler_params=pltpu.CompilerParams(**params))(*operands)
        return list(outs), []
    ni, no = len(rider.ins), len(rider.outs)
    first, last = edges

    def riding(*refs):
        rest = refs[n_in + ni + n_out + no:]
        cps = rider.copies(refs[n_in:n_in + ni], refs[n_in + ni + n_out:n_in + ni + n_out + no], rest[-2], rest[-1], 0)

        @pl.when(first())
        def _():
            for cp in cps:
                cp.start()

        body(*refs[:n_in], *refs[n_in + ni:n_in + ni + n_out], *rest[:-2])

        @pl.when(last())
        def _():
            for cp in cps:
                cp.wait()

    outs = pl.pallas_call(
        riding, name=f"{name}_{rider.tag}", grid=grid, in_specs=list(in_specs) + [ANY] * ni,
        out_specs=list(out_specs) + [ANY] * no, out_shape=list(out_shape) + rider.outs,
        scratch_shapes=list(scratch) + [pltpu.SemaphoreType.DMA((rider.n,)), pltpu.SemaphoreType.DMA((rider.n,))],
        input_output_aliases={n_in + i: n_out + j for i, j in rider.aliases.items()},
        compiler_params=pltpu.CompilerParams(has_side_effects=True, **params),
    )(*operands, *rider.ins)
    return list(outs[:n_out]), list(outs[n_out:])


def _edges_1d(n):
    return (lambda: pl.program_id(0) == 0), (lambda: pl.program_id(0) == n - 1)


def _edges_2d(n0, n1):
    return ((lambda: (pl.program_id(0) == 0) & (pl.program_id(1) == 0)),
            (lambda: (pl.program_id(0) == n0 - 1) & (pl.program_id(1) == n1 - 1)))


def _nt(a, b):
    return lax.dot_general(a, b, (((1,), (1,)), ((), ())), preferred_element_type=F32)


def _tn(a, b):
    return lax.dot_general(a, b, (((0,), (0,)), ((), ())), preferred_element_type=F32)


def _nn(a, b):
    return jnp.dot(a, b, preferred_element_type=F32)


def _ln_fwd(z, g, b):
    mu = jnp.mean(z, axis=-1, keepdims=True)
    zc = z - mu
    var = jnp.mean(zc * zc, axis=-1, keepdims=True)
    return zc * lax.rsqrt(var + LN_EPS) * g + b


def _ln_bwd(dy, z, g):
    mu = jnp.mean(z, axis=-1, keepdims=True)
    zc = z - mu
    var = jnp.mean(zc * zc, axis=-1, keepdims=True)
    rstd = lax.rsqrt(var + LN_EPS)
    xhat = zc * rstd
    gdy = dy * g
    m1 = jnp.mean(gdy, axis=-1, keepdims=True)
    m2 = jnp.mean(gdy * xhat, axis=-1, keepdims=True)
    return rstd * (gdy - m1 - xhat * m2), xhat


def _ffn_fwd(x, wg, wu, wd, lg, lb, rider=None):
    S, D = x.shape
    fq = wg.shape[-1]
    tm = min(512, S)

    def body(x_ref, wg_ref, wu_ref, wd_ref, lg_ref, lb_ref, xo_ref, xb_ref, z_ref, g_ref, u_ref):
        x = x_ref[...]
        xb = x.astype(BF)
        acc = jnp.zeros((tm, D), F32)
        for q in range(NQ):
            g = _nn(xb, wg_ref[q])
            u = _nn(xb, wu_ref[q])
            g_ref[q] = g.astype(BF)
            u_ref[q] = u.astype(BF)
            a = g * jax.nn.sigmoid(g) * u
            acc = acc + _nn(a.astype(BF), wd_ref[q])
        z = ALPHA * x + 0.5 * acc
        xo = _ln_fwd(z, lg_ref[...], lb_ref[...])
        z_ref[...] = z
        xo_ref[...] = xo
        xb_ref[...] = xo.astype(BF)

    row = pl.BlockSpec((tm, D), lambda i: (i, 0))
    qrow = pl.BlockSpec((NQ, tm, fq), lambda i: (0, i, 0))
    return _pcall(
        body, [x, wg, wu, wd, lg, lb], name="ffn_fwd", grid=(S // tm,),
        in_specs=[row, _quarters(wg), _quarters(wu), _quarters(wd), _full((1, D)), _full((1, D))],
        out_specs=[row, row, row, qrow, qrow],
        out_shape=[jax.ShapeDtypeStruct((S, D), F32), jax.ShapeDtypeStruct((S, D), BF),
                   jax.ShapeDtypeStruct((S, D), F32), jax.ShapeDtypeStruct((NQ, S, fq), BF),
                   jax.ShapeDtypeStruct((NQ, S, fq), BF)],
        rider=rider, edges=_edges_1d(S // tm))


def _ffn_bwd(dxo, z, g, u, wg, wu, wd, lg, rider=None):
    S, D = dxo.shape
    fq = wg.shape[-1]
    tm = min(256, S)
    nt = S // tm

    def body(dxo0_ref, z0_ref, dxo1_ref, z1_ref, g_ref, u_ref, wg_ref, wu_ref, wd_ref, lg_ref,
             dx_ref, df_ref, dg_ref, du_ref, a_ref, ln_ref, dz_ref):
        i = pl.program_id(0)

        @pl.when(i == 0)
        def _():
            dy0 = dxo0_ref[...]
            dz0, xhat0 = _ln_bwd(dy0, z0_ref[...], lg_ref[...])
            dz_ref[...] = dz0
            ln_ref[...] = jnp.zeros_like(ln_ref)
            ln_ref[0:1, :] += jnp.sum(dy0 * xhat0, axis=0, keepdims=True)
            ln_ref[1:2, :] += jnp.sum(dy0, axis=0, keepdims=True)

        dz = dz_ref[...]
        dfb = (0.5 * dz).astype(BF)
        df_ref[...] = dfb
        acc = ALPHA * dz
        for q in range(NQ):
            da = _nt(dfb, wd_ref[q])
            gg = g_ref[q].astype(F32)
            uu = u_ref[q].astype(F32)
            sg = jax.nn.sigmoid(gg)
            silu = gg * sg
            a_ref[q] = (silu * uu).astype(BF)
            dgb = (da * uu * (sg * (1.0 + gg * (1.0 - sg)))).astype(BF)
            dub = (da * silu).astype(BF)
            dg_ref[q] = dgb
            du_ref[q] = dub
            acc = acc + _nt(dgb, wg_ref[q]) + _nt(dub, wu_ref[q])
        dx_ref[...] = acc
        dy1 = dxo1_ref[...]
        dz1, xhat1 = _ln_bwd(dy1, z1_ref[...], lg_ref[...])
        real = (i < nt - 1).astype(F32)
        ln_ref[0:1, :] += real * jnp.sum(dy1 * xhat1, axis=0, keepdims=True)
        ln_ref[1:2, :] += real * jnp.sum(dy1, axis=0, keepdims=True)
        dz_ref[...] = dz1

    row = pl.BlockSpec((tm, D), lambda i: (i, 0))
    first = pl.BlockSpec((tm, D), lambda i: (0, 0))
    nxt = pl.BlockSpec((tm, D), lambda i: (jnp.minimum(i + 1, nt - 1), 0))
    qrow = pl.BlockSpec((NQ, tm, fq), lambda i: (0, i, 0))
    qshape = jax.ShapeDtypeStruct((NQ, S, fq), BF)
    return _pcall(
        body, [dxo, z, dxo, z, g, u, wg, wu, wd, lg], name="ffn_bwd", grid=(nt,),
        in_specs=[first, first, nxt, nxt, qrow, qrow, _quarters(wg), _quarters(wu), _quarters(wd), _full((1, D))],
        out_specs=[row, row, qrow, qrow, qrow, _full((8, D))],
        out_shape=[jax.ShapeDtypeStruct((S, D), F32), jax.ShapeDtypeStruct((S, D), BF), qshape, qshape, qshape,
                   jax.ShapeDtypeStruct((8, D), F32)],
        scratch=[pltpu.VMEM((tm, D), F32)],
        rider=rider, edges=_edges_1d(nt))


def _wgrad_gate_up(a, dg, du, rider=None):
    S, K = a.shape
    n = dg.shape[-1]
    ts = min(WGRAD_TOKENS, S)

    def body(a_ref, g_ref, u_ref, o_ref):
        @pl.when(pl.program_id(1) == 0)
        def _():
            o_ref[...] = jnp.zeros_like(o_ref)
        av = a_ref[...]
        o_ref[0:K, :] += _tn(av, g_ref[...])
        o_ref[K:2 * K, :] += _tn(av, u_ref[...])

    bspec = pl.BlockSpec((None, ts, n), lambda q, s: (q, s, 0))
    (out,), got = _pcall(
        body, [a, dg, du], name="wgrad_gate_up", grid=(NQ, S // ts),
        in_specs=[pl.BlockSpec((ts, K), lambda q, s: (s, 0)), bspec, bspec],
        out_specs=[pl.BlockSpec((None, 2 * K, n), lambda q, s: (q, 0, 0))],
        out_shape=[jax.ShapeDtypeStruct((NQ, 2 * K, n), F32)],
        rider=rider, edges=_edges_2d(NQ, S // ts))
    return out, got


def _wgrad_down(a, df):
    _, S, k = a.shape
    N = df.shape[1]
    ts = min(WGRAD_TOKENS, S)

    def body(a_ref, b_ref, o_ref):
        @pl.when(pl.program_id(1) == 0)
        def _():
            o_ref[...] = jnp.zeros_like(o_ref)
        o_ref[...] += _tn(a_ref[...], b_ref[...])

    return pl.pallas_call(
        body, name="wgrad_down", grid=(NQ, S // ts),
        in_specs=[pl.BlockSpec((None, ts, k), lambda q, s: (q, s, 0)), pl.BlockSpec((ts, N), lambda q, s: (s, 0))],
        out_specs=pl.BlockSpec((None, k, N), lambda q, s: (q, 0, 0)),
        out_shape=jax.ShapeDtypeStruct((NQ, k, N), F32),
        compiler_params=_cp(2),
    )(a, df)


def _wgrad_out(yab, yc, dzb):
    S, h = yab.shape
    D = dzb.shape[1]
    k = h // 2
    ts = min(WGRAD_TOKENS, S)

    def body(yab_ref, yc_ref, b_ref, o_ref):
        @pl.when(pl.program_id(0) == 0)
        def _():
            o_ref[...] = jnp.zeros_like(o_ref)
        b = b_ref[...]
        o_ref[0] += _tn(yab_ref[:, 0:k], b)
        o_ref[1] += _tn(yab_ref[:, k:h], b)
        o_ref[2] += _tn(yc_ref[:, 0:k], b)
        o_ref[3] += _tn(yc_ref[:, k:h], b)

    row = lambda w: pl.BlockSpec((ts, w), lambda s: (s, 0))
    return pl.pallas_call(
        body, name="wgrad_out", grid=(S // ts,),
        in_specs=[row(h), row(h), row(D)], out_specs=_full((NQ, k, D)),
        out_shape=jax.ShapeDtypeStruct((NQ, k, D), F32),
        compiler_params=_cp(1),
    )(yab, yc, dzb)


def _proj(xb, wc):
    S, D = xb.shape
    n = wc.shape[-1]
    n1 = D_POOL + 3 * D_CONV
    n2 = NQ * n - n1
    tm = min(1024, S)

    def body(x_ref, w_ref, p_ref, qkv_ref):
        x = x_ref[...]
        for q in range(NQ):
            r = _nn(x, w_ref[q])
            lo, hi = q * n, (q + 1) * n
            if hi <= n1:
                p_ref[:, lo:hi] = r
            elif lo >= n1:
                qkv_ref[:, lo - n1:hi - n1] = r.astype(BF)
            else:
                p_ref[:, lo:n1] = r[:, 0:n1 - lo]
                qkv_ref[:, 0:hi - n1] = r[:, n1 - lo:n].astype(BF)

    row = lambda w: pl.BlockSpec((tm, w), lambda i: (i, 0))
    return pl.pallas_call(
        body, name="mix_proj", grid=(S // tm,),
        in_specs=[row(D), _quarters(wc)],
        out_specs=[row(n1), row(n2)],
        out_shape=[jax.ShapeDtypeStruct((S, n1), F32), jax.ShapeDtypeStruct((S, n2), BF)],
        compiler_params=_cp(1),
    )(xb, wc)


def _mm_exact(a, b, name):
    def body(a_ref, b_ref, o_ref):
        o_ref[...] = jnp.dot(a_ref[...], b_ref[...].astype(F32), preferred_element_type=F32,
                             precision=lax.Precision.HIGHEST)

    return pl.pallas_call(
        body, name=name, in_specs=[_full(a.shape), _full(b.shape)], out_specs=_full((a.shape[0], b.shape[1])),
        out_shape=jax.ShapeDtypeStruct((a.shape[0], b.shape[1]), F32),
        compiler_params=pltpu.CompilerParams(vmem_limit_bytes=VMEM_LIMIT),
    )(a, b)


NB_ROWS = 2 * NA_ROWS - 1
NB_COLS = 2 * NA_COLS


def _bias_constants():
    c = np.arange(GRID_W)
    col_start = np.clip(c - NA_COLS // 2, 0, GRID_W - NA_COLS)
    valid = (c[None, :] >= col_start[:, None]) & (c[None, :] < col_start[:, None] + NA_COLS)
    dc = np.clip(c[None, :] - c[:, None], -(NA_COLS - 1), NA_COLS - 1) + (NA_COLS - 1)
    cq, ck = np.meshgrid(c, c, indexing="ij")
    onehot = np.zeros((HG, NB_COLS, GRID_W, HG, GRID_W), np.float32)
    for h in range(HG):
        onehot[h, dc[cq, ck], ck, h, cq] = 1.0
    mask_kq = np.where(valid.T, 0.0, NEG_INF).astype(np.float32)
    mask = np.tile(mask_kq, (NB_ROWS, HG))
    return onehot.reshape(HG * NB_COLS, GRID_W * LW), mask


def _bias_table(rpb, onehot, mask):
    ngr = NA_HEADS // HG
    r = rpb.reshape(ngr, HG, NB_ROWS, NB_COLS - 1).transpose(0, 2, 1, 3)
    r = jnp.pad(r, ((0, 0), (0, 0), (0, 0), (0, 1))).reshape(ngr * NB_ROWS, HG * NB_COLS)
    t = _mm_exact(r, onehot, "bias_expand")
    return t.reshape(ngr, NB_ROWS * GRID_W, LW) + mask[None]


def _bias_grad(dt, onehot_t):
    ngr = NA_HEADS // HG
    g = _mm_exact(dt.reshape(ngr * NB_ROWS, GRID_W * LW), onehot_t, "bias_reduce")
    g = g.reshape(ngr, NB_ROWS, HG, NB_COLS)[..., :NB_COLS - 1]
    return g.transpose(0, 2, 1, 3).reshape(NA_HEADS, NB_ROWS, NB_COLS - 1)


def _attn_rows(S):
    rows = S // GRID_W
    rb = min(16, rows)
    return rows, rb


def _head_masks():
    lane = lax.broadcasted_iota(jnp.int32, (GRID_W, LW), 1)
    return [(lane >= HEAD_DIM * h) & (lane < HEAD_DIM * (h + 1)) for h in range(HG)]


def _stack_heads(x, masks):
    zero = jnp.zeros_like(x)
    return jnp.concatenate([jnp.where(m, x, zero) for m in masks], axis=0)


def _unstack_heads(x2, masks):
    out = x2[0:GRID_W]
    for h in range(1, HG):
        out = jnp.where(masks[h], x2[h * GRID_W:(h + 1) * GRID_W], out)
    return out


def _attn_step(r, rows, q, k_ref, v_ref, b_ref, masks):
    rs = jnp.clip(r - NA_ROWS // 2, 0, rows - NA_ROWS)
    s0 = rs - r + (NA_ROWS - 1)
    q2 = _stack_heads(q, masks)
    ks = pl.ds(pl.multiple_of(rs * GRID_W, GRID_W), NA_ROWS * GRID_W)
    kb = k_ref[ks, :]
    vb = v_ref[ks, :]
    bs = pl.ds(pl.multiple_of(s0 * GRID_W, GRID_W), NA_ROWS * GRID_W)
    s = _nt(kb, q2) * (HEAD_DIM ** -0.5) + b_ref[0, bs, :]
    m = jnp.max(s, axis=0, keepdims=True)
    p = jnp.exp(s - m)
    p = p / jnp.sum(p, axis=0, keepdims=True)
    return p, q2, kb, vb, ks, bs


def _attn_fwd(qkv, bias, rider=None):
    S = qkv.shape[0]
    rows, rb = _attn_rows(S)
    tq = rb * GRID_W
    ngr = NA_HEADS // HG

    def body(q_ref, k_ref, v_ref, b_ref, o_ref):
        base = pl.program_id(1) * rb
        masks = _head_masks()

        def step(i, carry):
            qs = pl.ds(pl.multiple_of(i * GRID_W, GRID_W), GRID_W)
            p, _, _, vb, _, _ = _attn_step(base + i, rows, q_ref[qs, :], k_ref, v_ref, b_ref, masks)
            o_ref[qs, :] = _unstack_heads(_tn(p.astype(BF), vb), masks).astype(BF)
            return carry

        lax.fori_loop(0, rb, step, 0, unroll=rb)

    return _pcall(
        body, [qkv, qkv, qkv, bias], name="attn_fwd", grid=(ngr, rows // rb),
        in_specs=[pl.BlockSpec((tq, LW), lambda h, r: (r, h)),
                  pl.BlockSpec((S, LW), lambda h, r: (0, ngr + h)),
                  pl.BlockSpec((S, LW), lambda h, r: (0, 2 * ngr + h)),
                  pl.BlockSpec((1, bias.shape[1], LW), lambda h, r: (h, 0, 0))],
        out_specs=[pl.BlockSpec((tq, LW), lambda h, r: (r, h))],
        out_shape=[jax.ShapeDtypeStruct((S, D_NA), BF)],
        rider=rider, edges=_edges_2d(ngr, rows // rb))


def _attn_bwd(qkv, bias, dycat, rider=None):
    S = qkv.shape[0]
    rows, rb = _attn_rows(S)
    tq = rb * GRID_W
    ngr = NA_HEADS // HG
    scale = HEAD_DIM ** -0.5

    def body(q_ref, k_ref, v_ref, b_ref, do_ref, dq_ref, dk_ref, dv_ref, db_ref, dka_ref, dva_ref):
        base = pl.program_id(1) * rb
        last = pl.program_id(1) == pl.num_programs(1) - 1
        masks = _head_masks()

        @pl.when(pl.program_id(1) == 0)
        def _():
            dka_ref[...] = jnp.zeros_like(dka_ref)
            dva_ref[...] = jnp.zeros_like(dva_ref)
            db_ref[...] = jnp.zeros_like(db_ref)

        def step(i, carry):
            qs = pl.ds(pl.multiple_of(i * GRID_W, GRID_W), GRID_W)
            p, q2, kb, vb, ks, bs = _attn_step(base + i, rows, q_ref[qs, :], k_ref, v_ref, b_ref, masks)
            do2 = _stack_heads(do_ref[qs, :].astype(BF), masks)
            dp = _nt(vb, do2)
            ds = p * (dp - jnp.sum(p * dp, axis=0, keepdims=True))
            db_ref[0, bs, :] += ds
            dsb = ds.astype(BF)
            dq_ref[qs, :] = _unstack_heads(_tn(dsb, kb) * scale, masks).astype(BF)
            dka_ref[ks, :] += _nn(dsb, q2) * scale
            dva_ref[ks, :] += _nn(p.astype(BF), do2)
            return carry

        lax.fori_loop(0, rb, step, 0, unroll=rb)

        @pl.when(last)
        def _():
            dk_ref[...] = dka_ref[...].astype(BF)
            dv_ref[...] = dva_ref[...].astype(BF)

    nb = bias.shape[1]
    once = dict(pipeline_mode=pl.Buffered(1))
    nd = D_NA // LW
    return _pcall(
        body, [qkv, qkv, qkv, bias, dycat], name="attn_bwd", grid=(ngr, rows // rb),
        in_specs=[pl.BlockSpec((tq, LW), lambda h, r: (r, h)),
                  pl.BlockSpec((S, LW), lambda h, r: (0, ngr + h), **once),
                  pl.BlockSpec((S, LW), lambda h, r: (0, 2 * ngr + h), **once),
                  pl.BlockSpec((1, nb, LW), lambda h, r: (h, 0, 0)),
                  pl.BlockSpec((tq, LW), lambda h, r: (r, nd + h))],
        out_specs=[pl.BlockSpec((tq, LW), lambda h, r: (r, h)),
                   pl.BlockSpec((S, LW), lambda h, r: (0, h)),
                   pl.BlockSpec((S, LW), lambda h, r: (0, h)),
                   pl.BlockSpec((1, nb, LW), lambda h, r: (h, 0, 0))],
        out_shape=[jax.ShapeDtypeStruct((S, D_NA), BF)] * 3 + [jax.ShapeDtypeStruct((ngr, nb, LW), F32)],
        scratch=[pltpu.VMEM((S, LW), F32), pltpu.VMEM((S, LW), F32)],
        rider=rider, edges=_edges_2d(ngr, rows // rb))


def _halo_specs(tm, width, S):
    hb = tm // HALO
    last = S // HALO - 1
    return [pl.BlockSpec((tm, width), lambda i: (i, 0)),
            pl.BlockSpec((HALO, width), lambda i: (jnp.maximum(i * hb - 1, 0), 0)),
            pl.BlockSpec((HALO, width), lambda i: (jnp.minimum((i + 1) * hb, last), 0))]


def _with_halo(cur_ref, prev_ref, next_ref, i, nt):
    prev = jnp.where(i > 0, prev_ref[...], 0.0)
    nxt = jnp.where(i < nt - 1, next_ref[...], 0.0)
    return jnp.concatenate([prev, cur_ref[...], nxt], axis=0)


def _shift(a, k):
    n = a.shape[0]
    return pltpu.roll(a, k % n, 0)


def _pool_lanes(n):
    lane = lax.broadcasted_iota(jnp.int32, (n, D_POOL), 1)
    group = D_POOL // len(POOL_WINDOWS)
    return [lane < group * (j + 1) for j in range(len(POOL_WINDOWS) - 1)]


def _by_window(lanes, vals):
    return jnp.where(lanes[0], vals[0], jnp.where(lanes[1], vals[1], jnp.where(lanes[2], vals[2], vals[3])))


def _pool_count(lanes, t, S):
    back = _by_window(lanes, tuple(w // 2 for w in POOL_WINDOWS))
    lo = jnp.maximum(t - back, 0)
    hi = jnp.minimum(t + back, S)
    return jnp.maximum(hi - lo, 1).astype(F32)


def _pool_p(u, lanes, cnt):
    a = u + _shift(u, 1)
    b = _shift(a, 1) + _shift(a, -1)
    c = _shift(b, 2) + _shift(b, -2)
    d = _shift(c, 4) + _shift(c, -4)
    return _by_window(lanes, (a, b, c, d)) / cnt - u


def _mixab_fwd(pabc, wblk, vec):
    S = pabc.shape[0]
    tm = min(512, S)
    nt = S // tm
    n = tm + 2 * HALO
    tile = slice(HALO, HALO + tm)

    def body(cur_ref, prev_ref, next_ref, w_ref, vec_ref, o_ref):
        i = pl.program_id(0)
        ext = _with_halo(cur_ref, prev_ref, next_ref, i, nt)
        lanes = _pool_lanes(n)
        t = i * tm - HALO + lax.broadcasted_iota(jnp.int32, (n, D_POOL), 0)
        p = _pool_p(ext[:, 0:D_POOL], lanes, _pool_count(lanes, t, S))[tile]
        o_ref[:, 0:D_POOL] = (_nn(p.astype(BF), w_ref[...]) * vec_ref[0:1, :]).astype(BF)
        zc = ext[:, 512:768] * ext[:, 768:1024]
        conv = vec_ref[1:2, :] * _shift(zc, 1) + vec_ref[2:3, :] * zc + vec_ref[3:4, :] * _shift(zc, -1)
        o_ref[:, D_POOL:D_POOL + D_CONV] = (ext[tile, 256:512] * conv[tile]).astype(BF)

    return pl.pallas_call(
        body, name="mixab_fwd", grid=(nt,),
        in_specs=_halo_specs(tm, 1024, S) + [_full((D_POOL, D_POOL)), _full((8, D_POOL))],
        out_specs=pl.BlockSpec((tm, D_POOL + D_CONV), lambda i: (i, 0)),
        out_shape=jax.ShapeDtypeStruct((S, D_POOL + D_CONV), BF),
        compiler_params=_cp(1),
    )(pabc, pabc, pabc, wblk, vec)


def _mixab_bwd(pabc, dycat, wblk, vec):
    S = pabc.shape[0]
    tm = min(512, S)
    nt = S // tm
    n = tm + 2 * HALO
    tile = slice(HALO, HALO + tm)

    def body(cur_ref, prev_ref, next_ref, dcur_ref, dprev_ref, dnext_ref, w_ref, vec_ref, o_ref, dw_ref, dvec_ref):
        i = pl.program_id(0)

        @pl.when(i == 0)
        def _():
            dw_ref[...] = jnp.zeros_like(dw_ref)
            dvec_ref[...] = jnp.zeros_like(dvec_ref)

        ext = _with_halo(cur_ref, prev_ref, next_ref, i, nt)
        dext = _with_halo(dcur_ref, dprev_ref, dnext_ref, i, nt)
        lanes = _pool_lanes(n)
        t = i * tm - HALO + lax.broadcasted_iota(jnp.int32, (n, D_POOL), 0)
        cnt = _pool_count(lanes, t, S)
        w = w_ref[...]
        scale = vec_ref[0:1, :]
        pb = _pool_p(ext[:, 0:D_POOL], lanes, cnt)[tile].astype(BF)
        dya = dext[:, 0:D_POOL]
        dvec_ref[0:1, :] += jnp.sum(dya[tile] * _nn(pb, w), axis=0, keepdims=True)
        dqb = (dya * scale).astype(BF)
        dw_ref[...] += _tn(pb, dqb[tile])
        dp = _nt(dqb, w)
        r = dp / cnt
        a = r + _shift(r, -1)
        b = _shift(a, 1) + _shift(a, -1)
        c = _shift(b, 2) + _shift(b, -2)
        d = _shift(c, 4) + _shift(c, -4)
        o_ref[:, 0:256] = (_by_window(lanes, (a, b, c, d)) - dp)[tile].astype(BF)
        gb, gc, hh = ext[:, 256:512], ext[:, 512:768], ext[:, 768:1024]
        zc = gc * hh
        zm, zp = _shift(zc, 1), _shift(zc, -1)
        w0, w1, w2 = vec_ref[1:2, :], vec_ref[2:3, :], vec_ref[3:4, :]
        dyb = dext[:, D_POOL:D_POOL + D_CONV]
        dconv = dyb * gb
        o_ref[:, 256:512] = (dyb * (w0 * zm + w1 * zc + w2 * zp))[tile].astype(BF)
        dzc = w0 * _shift(dconv, -1) + w1 * dconv + w2 * _shift(dconv, 1)
        o_ref[:, 512:768] = (dzc * hh)[tile].astype(BF)
        o_ref[:, 768:1024] = (dzc * gc)[tile].astype(BF)
        dct = dconv[tile]
        dvec_ref[1:2, :] += jnp.sum(dct * zm[tile], axis=0, keepdims=True)
        dvec_ref[2:3, :] += jnp.sum(dct * zc[tile], axis=0, keepdims=True)
        dvec_ref[3:4, :] += jnp.sum(dct * zp[tile], axis=0, keepdims=True)

    return pl.pallas_call(
        body, name="mixab_bwd", grid=(nt,),
        in_specs=_halo_specs(tm, 1024, S) + _halo_specs(tm, 512, S) + [_full((D_POOL, D_POOL)), _full((8, D_POOL))],
        out_specs=[pl.BlockSpec((tm, 1024), lambda i: (i, 0)), _full((D_POOL, D_POOL)), _full((8, D_POOL))],
        out_shape=[jax.ShapeDtypeStruct((S, 1024), BF), jax.ShapeDtypeStruct((D_POOL, D_POOL), F32),
                   jax.ShapeDtypeStruct((8, D_POOL), F32)],
        compiler_params=_cp(1),
    )(pabc, pabc, pabc, dycat, dycat, dycat, wblk, vec)


def _mixout_fwd(yab, yc, x, wo, lg, lb):
    S, D = x.shape
    tm = min(512, S)
    h = yab.shape[1]
    k = h // 2

    def body(yab_ref, yc_ref, x_ref, w_ref, lg_ref, lb_ref, xo_ref, xb_ref, z_ref):
        y = (_nn(yab_ref[:, 0:k], w_ref[0]) + _nn(yab_ref[:, k:h], w_ref[1])
             + _nn(yc_ref[:, 0:k], w_ref[2]) + _nn(yc_ref[:, k:h], w_ref[3]))
        z = ALPHA * x_ref[...] + y
        xo = _ln_fwd(z, lg_ref[...], lb_ref[...])
        z_ref[...] = z
        xo_ref[...] = xo
        xb_ref[...] = xo.astype(BF)

    row = lambda w: pl.BlockSpec((tm, w), lambda i: (i, 0))
    return pl.pallas_call(
        body, name="mixout_fwd", grid=(S // tm,),
        in_specs=[row(h), row(h), row(D), _quarters(wo), _full((1, D)), _full((1, D))],
        out_specs=[row(D), row(D), row(D)],
        out_shape=[jax.ShapeDtypeStruct((S, D), F32), jax.ShapeDtypeStruct((S, D), BF),
                   jax.ShapeDtypeStruct((S, D), F32)],
        compiler_params=_cp(1),
    )(yab, yc, x, wo, lg, lb)


def _mixout_bwd(dxo, z, wo, lg, rider=None):
    S, D = dxo.shape
    k = wo.shape[-2]
    tm = min(512, S)
    nt = S // tm

    def body(dxo0_ref, z0_ref, dxo1_ref, z1_ref, w_ref, lg_ref, dres_ref, dzb_ref, dy_ref, ln_ref, dz_ref):
        i = pl.program_id(0)

        @pl.when(i == 0)
        def _():
            dy0 = dxo0_ref[...]
            dz0, xhat0 = _ln_bwd(dy0, z0_ref[...], lg_ref[...])
            dz_ref[...] = dz0
            ln_ref[...] = jnp.zeros_like(ln_ref)
            ln_ref[0:1, :] += jnp.sum(dy0 * xhat0, axis=0, keepdims=True)
            ln_ref[1:2, :] += jnp.sum(dy0, axis=0, keepdims=True)

        dz = dz_ref[...]
        dzb = dz.astype(BF)
        dres_ref[...] = ALPHA * dz
        dzb_ref[...] = dzb
        for q in range(NQ):
            dy_ref[:, q * k:(q + 1) * k] = _nt(dzb, w_ref[q])
        dy1 = dxo1_ref[...]
        dz1, xhat1 = _ln_bwd(dy1, z1_ref[...], lg_ref[...])
        real = (i < nt - 1).astype(F32)
        ln_ref[0:1, :] += real * jnp.sum(dy1 * xhat1, axis=0, keepdims=True)
        ln_ref[1:2, :] += real * jnp.sum(dy1, axis=0, keepdims=True)
        dz_ref[...] = dz1

    row = lambda w: pl.BlockSpec((tm, w), lambda i: (i, 0))
    first = pl.BlockSpec((tm, D), lambda i: (0, 0))
    nxt = pl.BlockSpec((tm, D), lambda i: (jnp.minimum(i + 1, nt - 1), 0))
    return _pcall(
        body, [dxo, z, dxo, z, wo, lg], name="mixout_bwd", grid=(nt,),
        in_specs=[first, first, nxt, nxt, _quarters(wo), _full((1, D))],
        out_specs=[row(D), row(D), row(NQ * k), _full((8, D))],
        out_shape=[jax.ShapeDtypeStruct((S, D), F32), jax.ShapeDtypeStruct((S, D), BF),
                   jax.ShapeDtypeStruct((S, NQ * k), F32), jax.ShapeDtypeStruct((8, D), F32)],
        scratch=[pltpu.VMEM((tm, D), F32)],
        rider=rider, edges=_edges_1d(nt))


def _take_cols(refs, lo, hi):
    parts, off = [], 0
    for r in refs:
        w = r.shape[1]
        a, b = max(lo, off), min(hi, off + w)
        if a < b:
            parts.append(r[:, a - off:b - off])
        off += w
    return parts[0] if len(parts) == 1 else jnp.concatenate(parts, axis=1)


def _proj_bwd(dres, dparts, wc, rider=None):
    S, D = dres.shape
    n = wc.shape[-1]
    tm = min(1024, S)
    np_ = len(dparts)

    def body(*refs):
        dres_ref, d_refs, w_ref, dx_ref = refs[0], refs[1:1 + np_], refs[1 + np_], refs[2 + np_]
        acc = dres_ref[...]
        for q in range(NQ):
            acc = acc + _nt(_take_cols(d_refs, q * n, (q + 1) * n), w_ref[q])
        dx_ref[...] = acc

    row = lambda w: pl.BlockSpec((tm, w), lambda i: (i, 0))
    return _pcall(
        body, [dres, *dparts, wc], name="mix_proj_bwd", grid=(S // tm,),
        in_specs=[row(D)] + [row(d.shape[1]) for d in dparts] + [_quarters(wc)],
        out_specs=[row(D)],
        out_shape=[jax.ShapeDtypeStruct((S, D), F32)],
        rider=rider, edges=_edges_1d(S // tm))


def _wgrad_in(a, dparts, n, rider=None):
    S, K = a.shape
    ts = min(WGRAD_TOKENS // 2, S)
    np_ = len(dparts)

    def body(*refs):
        a_ref, d_refs, o_ref = refs[0], refs[1:1 + np_], refs[1 + np_]

        @pl.when(pl.program_id(0) == 0)
        def _():
            o_ref[...] = jnp.zeros_like(o_ref)
        av = a_ref[...]
        for q in range(NQ):
            o_ref[q] += _tn(av, _take_cols(d_refs, q * n, (q + 1) * n))

    row = lambda w: pl.BlockSpec((ts, w), lambda s: (s, 0))
    (out,), got = _pcall(
        body, [a, *dparts], name="wgrad_in", grid=(S // ts,),
        in_specs=[row(K)] + [row(d.shape[1]) for d in dparts], out_specs=[_full((NQ, K, n))],
        out_shape=[jax.ShapeDtypeStruct((NQ, K, n), F32)],
        rider=rider, edges=_edges_1d(S // ts))
    return out, got


def _loss_head(y, target):
    S, D = y.shape
    tm = min(512, S)

    def body(y_ref, t_ref, l_ref, dy_ref):
        @pl.when(pl.program_id(0) == 0)
        def _():
            l_ref[...] = jnp.zeros_like(l_ref)
        e = y_ref[...] - t_ref[...]
        dy_ref[...] = e * (1.0 / D)
        part = jnp.sum(jnp.sum(e * e, axis=1, keepdims=True) * (1.0 / D), axis=0, keepdims=True)
        l_ref[...] += 0.5 * part

    row = pl.BlockSpec((tm, D), lambda i: (i, 0))
    return pl.pallas_call(
        body, name="loss_head", grid=(S // tm,),
        in_specs=[row, row], out_specs=[_full((8, 128)), row],
        out_shape=[jax.ShapeDtypeStruct((8, 128), F32), jax.ShapeDtypeStruct((S, D), F32)],
        compiler_params=_cp(1),
    )(y, target)


def _adamw_update(w, g, m, v):
    mn = ADAM_B1 * m + (1.0 - ADAM_B1) * g
    vn = ADAM_B2 * v + (1.0 - ADAM_B2) * (g * g)
    m_hat = mn / (1.0 - ADAM_B1 ** ADAM_STEP)
    v_hat = vn / (1.0 - ADAM_B2 ** ADAM_STEP)
    return -ADAM_LR * (m_hat / (jnp.sqrt(v_hat) + ADAM_EPS) + ADAM_WD * w), mn, vn


def _adamw(w, g, m, v):
    shape = w.shape
    cols = shape[-1]
    rows = int(np.prod(shape[:-1]))
    w2, g2, m2, v2 = (a.reshape(rows, cols) for a in (w, g, m, v))
    tr = rows
    for cand in (512, 352, 256):
        if rows > cand and rows % cand == 0:
            tr = cand
            break

    def body(w_ref, g_ref, m_ref, v_ref, d_ref, mo_ref, vo_ref):
        d_ref[...], mo_ref[...], vo_ref[...] = _adamw_update(w_ref[...], g_ref[...], m_ref[...], v_ref[...])

    spec = pl.BlockSpec((tr, cols), lambda i: (i, 0))
    outs = pl.pallas_call(
        body, name=f"adamw_{rows}x{cols}", grid=(rows // tr,),
        in_specs=[spec] * 4, out_specs=[spec] * 3,
        out_shape=[jax.ShapeDtypeStruct((rows, cols), F32)] * 3,
        compiler_params=_cp(1),
    )(w2, g2, m2, v2)
    return tuple(o.reshape(shape) for o in outs)


def _half_tile(h):
    return h if h <= 512 else 512


def _add_chip(g, recv):
    _, R, C = g.shape
    h = R // 2
    tr = _half_tile(h)
    nb = h // tr

    def body(a_ref, b_ref, o_ref, ob_ref):
        s = a_ref[...] + b_ref[...]
        ob_ref[...] = s.astype(BF)

        @pl.when(pl.program_id(1) == 2 * lax.axis_index("x") + lax.axis_index("y"))
        def _():
            o_ref[...] = s[0]

    half = pl.BlockSpec((1, tr, C), lambda i, q: (q, i, 0))
    mine = pl.BlockSpec((1, tr, C), lambda i, q: (q, lax.axis_index("c") * nb + i, 0))
    return pl.pallas_call(
        body, name=f"rs_add_chip_{R}x{C}", grid=(nb, NQ), in_specs=[mine, half],
        out_specs=[pl.BlockSpec((tr, C), lambda i, q: (i, 0)), half],
        out_shape=[jax.ShapeDtypeStruct((h, C), F32), jax.ShapeDtypeStruct((NQ, h, C), BF)],
        compiler_params=_cp(2),
    )(g, recv)


def _add_final(chip, recv):
    h, C = chip.shape
    tr = _half_tile(h)
    nb = h // tr

    def body(a_ref, b_ref, o_ref):
        s = a_ref[...]
        for j in range(3):
            s = s + b_ref[j].astype(F32)
        o_ref[...] = s

    return pl.pallas_call(
        body, name=f"rs_add_final_{h}x{C}", grid=(nb,),
        in_specs=[pl.BlockSpec((tr, C), lambda i: (i, 0)), pl.BlockSpec((3, tr, C), lambda i: (0, i, 0))],
        out_specs=pl.BlockSpec((tr, C), lambda i: (lax.axis_index("c") * nb + i, 0)),
        out_shape=jax.ShapeDtypeStruct((2 * h, C), F32),
        compiler_params=_cp(1),
    )(chip, recv)


COMM = pltpu.CompilerParams(has_side_effects=True)


def _place():
    x, y, c = lax.axis_index("x"), lax.axis_index("y"), lax.axis_index("c")
    chips = [(1 - x, y), (x, 1 - y), (1 - x, 1 - y)]
    return x, y, c, chips


def _half0(ref, c):
    n = ref.shape[0] // 2
    return ref.at[pl.ds(c * n, n)]


def _gather_ici(shards):
    n = len(shards)

    def copies(r_in, r_out, ssem, rsem, base):
        x, y, c, chips = _place()
        q = 2 * x + y
        return [pltpu.make_async_remote_copy(
            src_ref=_half0(r_in[i], c), dst_ref=_half0(r_out[i].at[q], c), send_sem=ssem.at[base + 3 * i + j],
            recv_sem=rsem.at[base + 3 * i + j], device_id=(*chip, c), device_id_type=MESH)
            for i in range(n) for j, chip in enumerate(chips)]

    return _Rider("ici", shards, [jax.ShapeDtypeStruct((NQ,) + s.shape, BF) for s in shards], {}, 3 * n, copies)


def _gather_d2d(bufs):
    n = len(bufs)

    def copies(r_in, r_out, ssem, rsem, base):
        x, y, c, chips = _place()
        return [pltpu.make_async_remote_copy(
            src_ref=_half0(r_in[i].at[2 * cx + cy], c), dst_ref=_half0(r_out[i].at[2 * cx + cy], c),
            send_sem=ssem.at[base + 3 * i + j], recv_sem=rsem.at[base + 3 * i + j], device_id=(x, y, 1 - c),
            device_id_type=MESH) for i in range(n) for j, (cx, cy) in enumerate(chips)]

    return _Rider("d2d", bufs, [jax.ShapeDtypeStruct(b.shape, b.dtype) for b in bufs], {i: i for i in range(n)},
                  3 * n, copies)


def _gather_small(small):
    sr = small.shape[0]

    def body(s_ref, o_ref, send_sems, recv_sems):
        x, y, c, chips = _place()
        o_ref[2 * x + y] = s_ref[...]
        cps = [pltpu.make_async_remote_copy(
            src_ref=s_ref, dst_ref=o_ref.at[2 * x + y], send_sem=send_sems.at[j], recv_sem=recv_sems.at[j],
            device_id=(*chip, c), device_id_type=MESH) for j, chip in enumerate(chips)]
        for cp in cps:
            cp.start()
        for j, (cx, cy) in enumerate(chips):
            pltpu.make_async_remote_copy(
                src_ref=s_ref, dst_ref=o_ref.at[2 * cx + cy], send_sem=send_sems.at[j], recv_sem=recv_sems.at[j],
                device_id=(cx, cy, c), device_id_type=MESH).wait_recv()
        for cp in cps:
            cp.wait_send()

    vm = pl.BlockSpec(memory_space=pltpu.VMEM)
    return pl.pallas_call(
        body, name="gather_small", in_specs=[vm], out_specs=vm,
        out_shape=jax.ShapeDtypeStruct((NQ, sr, 128), F32),
        scratch_shapes=[pltpu.SemaphoreType.DMA((3,)), pltpu.SemaphoreType.DMA((3,))],
        compiler_params=COMM,
    )(small)


def _swap_halves(gs):
    n = len(gs)

    def copies(r_in, r_out, ssem, rsem, base):
        x, y, c, _ = _place()
        cps = []
        for i in range(n):
            h = r_in[i].shape[1] // 2
            cps.append(pltpu.make_async_remote_copy(
                src_ref=r_in[i].at[:, pl.ds((1 - c) * h, h), :], dst_ref=r_out[i], send_sem=ssem.at[base + i],
                recv_sem=rsem.at[base + i], device_id=(x, y, 1 - c), device_id_type=MESH))
        return cps

    return _Rider("swap", gs, [jax.ShapeDtypeStruct((NQ, g.shape[1] // 2, g.shape[2]), F32) for g in gs], {}, n,
                  copies)


def _scatter_chips(chips_b):
    n = len(chips_b)

    def copies(r_in, r_out, ssem, rsem, base):
        x, y, c, chips = _place()
        return [pltpu.make_async_remote_copy(
            src_ref=r_in[i].at[2 * cx + cy], dst_ref=r_out[i].at[j], send_sem=ssem.at[base + 3 * i + j],
            recv_sem=rsem.at[base + 3 * i + j], device_id=(cx, cy, c), device_id_type=MESH)
            for i in range(n) for j, (cx, cy) in enumerate(chips)]

    return _Rider("scatter", chips_b, [jax.ShapeDtypeStruct((3,) + s.shape[1:], BF) for s in chips_b], {}, 3 * n,
                  copies)


def _run_alone(rider, name):
    ni, no = len(rider.ins), len(rider.outs)

    def body(*refs):
        cps = rider.copies(refs[:ni], refs[ni:ni + no], refs[ni + no], refs[ni + no + 1], 0)
        for cp in cps:
            cp.start()
        for cp in cps:
            cp.wait()

    return list(pl.pallas_call(
        body, name=name, in_specs=[ANY] * ni, out_specs=[ANY] * no, out_shape=rider.outs,
        input_output_aliases=dict(rider.aliases),
        scratch_shapes=[pltpu.SemaphoreType.DMA((rider.n,)), pltpu.SemaphoreType.DMA((rider.n,))],
        compiler_params=COMM,
    )(*rider.ins))


def _join_halves(fs):
    n = len(fs)

    def copies(r_in, r_out, ssem, rsem, base):
        x, y, c, _ = _place()
        return [pltpu.make_async_remote_copy(
            src_ref=_half0(r_in[i], c), dst_ref=_half0(r_out[i], c), send_sem=ssem.at[base + i],
            recv_sem=rsem.at[base + i], device_id=(x, y, 1 - c), device_id_type=MESH) for i in range(n)]

    return _Rider("join", fs, [jax.ShapeDtypeStruct(f.shape, F32) for f in fs], {i: i for i in range(n)}, n, copies)


def _allreduce_small(v):
    r, W = v.shape

    def body(v_ref, o_ref, land_ref, send_sems, recv_sems):
        x, y, c, _ = _place()
        me = 4 * x + 2 * y + c
        cps = []
        for m in range(1, 8):
            to = (x ^ (m >> 2), y ^ ((m >> 1) & 1), c ^ (m & 1))
            cps.append(pltpu.make_async_remote_copy(
                src_ref=v_ref, dst_ref=land_ref.at[m - 1], send_sem=send_sems.at[m - 1], recv_sem=recv_sems.at[m - 1],
                device_id=to, device_id_type=MESH))
        for cp in cps:
            cp.start()
        for cp in cps:
            cp.wait()
        total = jnp.zeros((r, W), F32)
        for d in range(8):
            slot = jnp.maximum((me ^ d) - 1, 0)
            total = total + jnp.where(me == d, v_ref[...], land_ref[slot])
        o_ref[...] = total

    return pl.pallas_call(
        body, name="allreduce_small",
        in_specs=[pl.BlockSpec(memory_space=pltpu.VMEM)], out_specs=pl.BlockSpec(memory_space=pltpu.VMEM),
        out_shape=jax.ShapeDtypeStruct((r, W), F32),
        scratch_shapes=[pltpu.VMEM((7, r, W), F32), pltpu.SemaphoreType.DMA((7,)), pltpu.SemaphoreType.DMA((7,))],
        compiler_params=pltpu.CompilerParams(has_side_effects=True, vmem_limit_bytes=VMEM_LIMIT),
    )(v)


def kernel(x, ffn1_w_gate, ffn1_w_up, ffn1_w_down, ffn2_w_gate, ffn2_w_up, ffn2_w_down, w_in, pool_w, pool_scale, conv_w, rpb, w_out, ln_g, ln_b, loss_target, m_ffn1_w_gate, m_ffn1_w_up, m_ffn1_w_down, m_ffn2_w_gate, m_ffn2_w_up, m_ffn2_w_down, m_w_in, m_pool_w, m_pool_scale, m_conv_w, m_rpb, m_w_out, m_ln_g, m_ln_b, v_ffn1_w_gate, v_ffn1_w_up, v_ffn1_w_down, v_ffn2_w_gate, v_ffn2_w_up, v_ffn2_w_down, v_w_in, v_pool_w, v_pool_scale, v_conv_w, v_rpb, v_w_out, v_ln_g, v_ln_b):
    weights = dict(ffn1_w_gate=ffn1_w_gate, ffn1_w_up=ffn1_w_up, ffn1_w_down=ffn1_w_down, ffn2_w_gate=ffn2_w_gate,
                   ffn2_w_up=ffn2_w_up, ffn2_w_down=ffn2_w_down, w_in=w_in, pool_w=pool_w, pool_scale=pool_scale,
                   conv_w=conv_w, rpb=rpb, w_out=w_out, ln_g=ln_g, ln_b=ln_b)
    mom_m = dict(ffn1_w_gate=m_ffn1_w_gate, ffn1_w_up=m_ffn1_w_up, ffn1_w_down=m_ffn1_w_down, ffn2_w_gate=m_ffn2_w_gate,
                 ffn2_w_up=m_ffn2_w_up, ffn2_w_down=m_ffn2_w_down, w_in=m_w_in, pool_w=m_pool_w,
                 pool_scale=m_pool_scale, conv_w=m_conv_w, rpb=m_rpb, w_out=m_w_out, ln_g=m_ln_g, ln_b=m_ln_b)
    mom_v = dict(ffn1_w_gate=v_ffn1_w_gate, ffn1_w_up=v_ffn1_w_up, ffn1_w_down=v_ffn1_w_down, ffn2_w_gate=v_ffn2_w_gate,
                 ffn2_w_up=v_ffn2_w_up, ffn2_w_down=v_ffn2_w_down, w_in=v_w_in, pool_w=v_pool_w,
                 pool_scale=v_pool_scale, conv_w=v_conv_w, rpb=v_rpb, w_out=v_w_out, ln_g=v_ln_g, ln_b=v_ln_b)
    order = list(weights)
    L = ffn1_w_gate.shape[0]
    xi, yi, ci = lax.axis_index("x"), lax.axis_index("y"), lax.axis_index("c")
    q_me = 2 * xi + yi
    x2 = x[0]
    target = loss_target[0]
    D = x2.shape[1]
    n_in = w_in.shape[-1]

    small = jnp.concatenate([ln_g.reshape(-1), ln_b.reshape(-1), conv_w.reshape(-1)])
    n_small = small.shape[0]
    small_rows = -(-n_small // (8 * 128)) * 8
    small = jnp.pad(small, (0, small_rows * 128 - n_small)).reshape(small_rows, 128)
    small_all = _gather_small(small).reshape(NQ, small_rows * 128)[:, :n_small]
    dq4 = D // NQ
    n_ln = L * 3 * dq4
    ln_g_all = small_all[:, :n_ln].reshape(NQ, L, 3, dq4).transpose(1, 2, 0, 3).reshape(L, 3, D)
    ln_b_all = small_all[:, n_ln:2 * n_ln].reshape(NQ, L, 3, dq4).transpose(1, 2, 0, 3).reshape(L, 3, D)
    conv_all = small_all[:, 2 * n_ln:].reshape(NQ, L, 3, D_CONV // NQ).transpose(1, 2, 0, 3).reshape(L, 3, D_CONV)

    def layer_shards(l):
        return [w[l].astype(BF) for w in (ffn1_w_gate, ffn1_w_up, ffn1_w_down, w_in, w_out, ffn2_w_gate, ffn2_w_up,
                                          ffn2_w_down)]

    def own_quarter(bufs, shards):
        return [lax.dynamic_update_slice(b, s[None], (q_me,) + (0,) * s.ndim) for b, s in zip(bufs, shards)]

    shards = [layer_shards(l) for l in range(L)]
    landed = _run_alone(_gather_ici(shards[0][:3]), "gather_ici")
    weights_of = [own_quarter(_run_alone(_gather_d2d(landed), "gather_d2d"), shards[0][:3])] + [None] * (L - 1)

    onehot_np, mask_np = _bias_constants()
    onehot, onehot_t, mask = jnp.asarray(onehot_np, BF), jnp.asarray(onehot_np.T.copy(), BF), jnp.asarray(mask_np)
    ng = len(POOL_WINDOWS)
    pg = D_POOL // ng
    saved = []
    h = x2
    hb = x2.astype(BF)
    for l in range(L):
        nxt = shards[l + 1] if l + 1 < L else None
        wg1, wu1, wd1 = weights_of[l][:3]
        eye = jnp.eye(ng, dtype=F32)
        wblk = (pool_w[l][:, :, None, :] * eye[:, None, :, None]).reshape(D_POOL, D_POOL).astype(BF)
        vec = jnp.concatenate([pool_scale[l][None], conv_all[l], jnp.zeros((4, D_POOL), F32)], axis=0)
        bias = _bias_table(rpb[l], onehot, mask)
        lg = [ln_g_all[l, j][None] for j in range(3)]
        lb = [ln_b_all[l, j][None] for j in range(3)]
        if l == 0:
            (x1, x1b, z1, g1, u1), got = _ffn_fwd(h, wg1, wu1, wd1, lg[0], lb[0], rider=_gather_ici(shards[0][3:]))
            weights_of[0] += own_quarter(_run_alone(_gather_d2d(got), "gather_d2d_rest"), shards[0][3:])
            r_attn = _gather_ici(nxt[:3]) if nxt else None
        else:
            riders = [_gather_d2d(landed)] + ([_gather_ici(nxt[:3])] if nxt else [])
            (x1, x1b, z1, g1, u1), got = _ffn_fwd(h, wg1, wu1, wd1, lg[0], lb[0], rider=_merge(*riders))
            weights_of[l] += own_quarter(got[:5], shards[l][3:])
            r_attn = _gather_d2d(got[5:]) if nxt else None
        wc, wo, wg2, wu2, wd2 = weights_of[l][3:]
        pabc, qkv = _proj(x1b, wc)
        yab = _mixab_fwd(pabc, wblk, vec)
        (yc,), got = _attn_fwd(qkv, bias, rider=r_attn)
        xm, xmb, zm = _mixout_fwd(yab, yc, x1, wo, lg[1], lb[1])
        if l == 0:
            r_ffn2 = _merge(_gather_d2d(got), _gather_ici(nxt[3:])) if nxt else None
        else:
            r_ffn2 = _gather_ici(nxt[3:]) if nxt else None
        (x3, x3b, z3, g3, u3), got2 = _ffn_fwd(xm, wg2, wu2, wd2, lg[2], lb[2], rider=r_ffn2)
        if nxt and l == 0:
            weights_of[1], landed = own_quarter(got2[:3], nxt[:3]), got2[3:]
        elif nxt:
            weights_of[l + 1], landed = own_quarter(got, nxt[:3]), got2
        saved.append(dict(wblk=wblk, vec=vec, bias=bias, lg=lg, hb=hb, z1=z1, g1=g1, u1=u1, x1b=x1b, pabc=pabc,
                          qkv=qkv, yab=yab, yc=yc, zm=zm, xmb=xmb, z3=z3, g3=g3, u3=u3))
        h, hb = x3, x3b

    loss_tile, dh = _loss_head(h, target)
    loss = lax.psum(loss_tile[0, 0], ("x", "y", "c"))

    def add_chip(arrs, recv):
        chip = [_add_chip(g, r) for g, r in zip(arrs, recv)]
        return [cf for cf, _ in chip], [cb for _, cb in chip]

    def add_final(chip_f, from_chips):
        return _join_halves([_add_final(cf, r) for cf, r in zip(chip_f, from_chips)])

    per_layer = [[None] * 6 for _ in range(L)]
    g_small = dict(pool_w=[None] * L, pool_scale=[None] * L, conv_w=[None] * L, rpb=[None] * L, ln_g=[None] * L,
                   ln_b=[None] * L)
    ffn1_g = None
    for l in reversed(range(L)):
        sv = saved[l]
        wg1, wu1, wd1, wc, wo, wg2, wu2, wd2 = weights_of[l]
        (dxm, df, dg, du, a, ln3), got = _ffn_bwd(dh, sv["z3"], sv["g3"], sv["u3"], wg2, wu2, wd2, sv["lg"][2],
                                                  rider=_swap_halves(ffn1_g) if ffn1_g else None)
        if ffn1_g:
            ffn1_f, ffn1_b = add_chip(ffn1_g, got)
        ffn2_g = [_wgrad_gate_up(sv["xmb"], dg, du)[0], _wgrad_down(a, df)]
        (dres, dzb, dycat, ln2), got = _mixout_bwd(dxm, sv["zm"], wo, sv["lg"][1], rider=_swap_halves(ffn2_g))
        ffn2_f, ffn2_b = add_chip(ffn2_g, got)
        g_o = _wgrad_out(sv["yab"], sv["yc"], dzb)
        dpabc, dwblk, dvec = _mixab_bwd(sv["pabc"], dycat, sv["wblk"], sv["vec"])
        (dq, dk, dv, dbias), got = _attn_bwd(sv["qkv"], sv["bias"], dycat,
                                             rider=_scatter_chips(ffn1_b) if ffn1_g else None)
        dparts = [dpabc, dq, dk, dv]
        g_in, got = _wgrad_in(sv["x1b"], dparts, n_in, rider=add_final(ffn1_f, got) if ffn1_g else None)
        if ffn1_g:
            per_layer[l + 1][0:2] = got
        mix_g = [g_in, g_o]
        (dx1,), got = _proj_bwd(dres, dparts, wc, rider=_swap_halves(mix_g))
        mix_f, mix_b = add_chip(mix_g, got)
        (dh, df, dg, du, a, ln1), got = _ffn_bwd(dx1, sv["z1"], sv["g1"], sv["u1"], wg1, wu1, wd1, sv["lg"][0],
                                                 rider=_scatter_chips(ffn2_b + mix_b))
        g_gu, per_layer[l][2:6] = _wgrad_gate_up(sv["hb"], dg, du, rider=add_final(ffn2_f + mix_f, got))
        ffn1_g = [g_gu, _wgrad_down(a, df)]
        g_small["pool_w"][l] = jnp.stack([dwblk[gi * pg:(gi + 1) * pg, gi * pg:(gi + 1) * pg] for gi in range(ng)])
        g_small["pool_scale"][l] = dvec[0]
        g_small["conv_w"][l] = dvec[1:4]
        g_small["rpb"][l] = _bias_grad(dbias, onehot_t)
        g_small["ln_g"][l] = jnp.stack([ln1[0], ln2[0], ln3[0]])
        g_small["ln_b"][l] = jnp.stack([ln1[1], ln2[1], ln3[1]])
    grad_x = dh[None]

    swapped = ("ffn1_w_gate", "ffn1_w_up", "ffn2_w_gate", "ffn2_w_up")

    def view(n, a):
        return jnp.swapaxes(a, 1, 2) if n in swapped else a

    def stacked(i, rows=None, swap=False):
        parts = [per_layer[l][i] if rows is None else per_layer[l][i][rows[0]:rows[1]] for l in range(L)]
        return jnp.stack([p.T for p in parts] if swap else parts)

    ffn1_f, ffn1_b = add_chip(ffn1_g, _run_alone(_swap_halves(ffn1_g), "rs_swap_halves"))
    got = _run_alone(_scatter_chips(ffn1_b), "rs_scatter_chips")
    per_layer[0][0:2] = _run_alone(add_final(ffn1_f, got), "rs_join_halves")
    grads_v = dict(ffn1_w_gate=stacked(0, (0, D), True), ffn1_w_up=stacked(0, (D, 2 * D), True),
                   ffn1_w_down=stacked(1), ffn2_w_gate=stacked(2, (0, D), True),
                   ffn2_w_up=stacked(2, (D, 2 * D), True), ffn2_w_down=stacked(3), w_in=stacked(4), w_out=stacked(5))
    grads = {n: view(n, g) for n, g in grads_v.items()}
    delta, new_m, new_v = {}, {}, {}

    small_names = ("pool_w", "pool_scale", "conv_w", "rpb", "ln_g", "ln_b")
    small_full = {n: jnp.stack(g_small[n]) for n in small_names}
    vflat = jnp.concatenate([small_full[n].reshape(-1) for n in small_names])
    n_v = vflat.shape[0]
    v_cols = 1024
    v_rows = -(-n_v // (8 * v_cols)) * 8
    vsum = _allreduce_small(jnp.pad(vflat, (0, v_rows * v_cols - n_v)).reshape(v_rows, v_cols)).reshape(-1)
    off = 0
    for n in small_names:
        sz = int(np.prod(small_full[n].shape))
        grads[n] = vsum[off:off + sz].reshape(small_full[n].shape)
        off += sz
    for n in ("conv_w", "ln_g", "ln_b"):
        width = weights[n].shape[-1]
        grads[n] = lax.dynamic_slice_in_dim(grads[n], q_me * width, width, axis=2)

    for n in order:
        res = _adamw(view(n, weights[n]), view(n, grads[n]), view(n, mom_m[n]), view(n, mom_v[n]))
        delta[n], new_m[n], new_v[n] = (view(n, o) for o in res)
    return (loss, grad_x, *[grads[n] for n in order], *[delta[n] for n in order], *[new_m[n] for n in order],
            *[new_v[n] for n in order])
```

```python
import numpy as np
import jax
import jax.numpy as jnp
from jax import lax
from jax.experimental import pallas as pl
from jax.experimental.pallas import tpu as pltpu

BF = jnp.bfloat16
F32 = jnp.float32
MESH = pl.DeviceIdType.MESH

DEPTH = 4
ALPHA = (2.0 * DEPTH) ** 0.25
LN_EPS = 1e-5
NEG_INF = -1e30
GRID_W = 64
NA_ROWS = 8
NA_COLS = 16
NA_HEADS = 8
HEAD_DIM = 64
D_POOL = 256
D_CONV = 256
D_NA = 512
HG = 4
LW = HG * HEAD_DIM
POOL_WINDOWS = (2, 4, 8, 16)
HALO = 8
ADAM_LR, ADAM_B1, ADAM_B2, ADAM_EPS, ADAM_WD, ADAM_STEP = 0.001, 0.9, 0.999, 1e-08, 0.01, 10
VMEM_LIMIT = 56 * 1024 * 1024
NQ = 4
WGRAD_TOKENS = 2048


def _cp(n_axes):
    return pltpu.CompilerParams(dimension_semantics=("arbitrary",) * n_axes, vmem_limit_bytes=VMEM_LIMIT)


def _full(shape):
    nd = len(shape)
    return pl.BlockSpec(shape, lambda *_: (0,) * nd)


def _quarters(arr):
    return pl.BlockSpec(arr.shape, lambda *_: (0, 0, 0), pipeline_mode=pl.Buffered(1))


ANY = pl.BlockSpec(memory_space=pl.ANY)


class _Rider:
    def __init__(self, tag, ins, outs, aliases, n, copies):
        self.tag, self.ins, self.outs, self.aliases, self.n, self.copies = tag, list(ins), list(outs), aliases, n, copies


def _merge(*riders):
    ins, outs, aliases, spans, n = [], [], {}, [], 0
    for r in riders:
        spans.append((len(ins), len(outs), n))
        aliases.update({len(ins) + i: len(outs) + j for i, j in r.aliases.items()})
        ins += r.ins
        outs += r.outs
        n += r.n

    def copies(r_in, r_out, ssem, rsem, base):
        cps = []
        for r, (i0, o0, s0) in zip(riders, spans):
            cps += r.copies(r_in[i0:i0 + len(r.ins)], r_out[o0:o0 + len(r.outs)], ssem, rsem, base + s0)
        return cps

    return _Rider("_".join(r.tag for r in riders), ins, outs, aliases, n, copies)


def _pcall(body, operands, *, name, grid, in_specs, out_specs, out_shape, scratch=(), rider=None, edges=None):
    n_in, n_out = len(in_specs), len(out_specs)
    params = dict(dimension_semantics=("arbitrary",) * len(grid), vmem_limit_bytes=VMEM_LIMIT)
    if rider is None:
        outs = pl.pallas_call(body, name=name, grid=grid, in_specs=in_specs, out_specs=out_specs, out_shape=out_shape,
                              scratch_shapes=list(scratch), compiler_params=pltpu.CompilerParams(**params))(*operands)
        return list(outs), []
    ni, no = len(rider.ins), len(rider.outs)
    first, last = edges

    def riding(*refs):
        rest = refs[n_in + ni + n_out + no:]
        cps = rider.copies(refs[n_in:n_in + ni], refs[n_in + ni + n_out:n_in + ni + n_out + no], rest[-2], rest[-1], 0)

        @pl.when(first())
        def _():
            for cp in cps:
                cp.start()

        body(*refs[:n_in], *refs[n_in + ni:n_in + ni + n_out], *rest[:-2])

        @pl.when(last())
        def _():
            for cp in cps:
                cp.wait()

    outs = pl.pallas_call(
        riding, name=f"{name}_{rider.tag}", grid=grid, in_specs=list(in_specs) + [ANY] * ni,
        out_specs=list(out_specs) + [ANY] * no, out_shape=list(out_shape) + rider.outs,
        scratch_shapes=list(scratch) + [pltpu.SemaphoreType.DMA((rider.n,)), pltpu.SemaphoreType.DMA((rider.n,))],
        input_output_aliases={n_in + i: n_out + j for i, j in rider.aliases.items()},
        compiler_params=pltpu.CompilerParams(has_side_effects=True, **params),
    )(*operands, *rider.ins)
    return list(outs[:n_out]), list(outs[n_out:])


def _edges_1d(n):
    return (lambda: pl.program_id(0) == 0), (lambda: pl.program_id(0) == n - 1)


def _edges_2d(n0, n1):
    return ((lambda: (pl.program_id(0) == 0) & (pl.program_id(1) == 0)),
            (lambda: (pl.program_id(0) == n0 - 1) & (pl.program_id(1) == n1 - 1)))


def _nt(a, b):
    return lax.dot_general(a, b, (((1,), (1,)), ((), ())), preferred_element_type=F32)


def _tn(a, b):
    return lax.dot_general(a, b, (((0,), (0,)), ((), ())), preferred_element_type=F32)


def _nn(a, b):
    return jnp.dot(a, b, preferred_element_type=F32)


def _ln_fwd(z, g, b):
    mu = jnp.mean(z, axis=-1, keepdims=True)
    zc = z - mu
    var = jnp.mean(zc * zc, axis=-1, keepdims=True)
    return zc * lax.rsqrt(var + LN_EPS) * g + b


def _ln_bwd(dy, z, g):
    mu = jnp.mean(z, axis=-1, keepdims=True)
    zc = z - mu
    var = jnp.mean(zc * zc, axis=-1, keepdims=True)
    rstd = lax.rsqrt(var + LN_EPS)
    xhat = zc * rstd
    gdy = dy * g
    m1 = jnp.mean(gdy, axis=-1, keepdims=True)
    m2 = jnp.mean(gdy * xhat, axis=-1, keepdims=True)
    return rstd * (gdy - m1 - xhat * m2), xhat


def _ffn_fwd(x, wg, wu, wd, lg, lb, rider=None):
    S, D = x.shape
    fq = wg.shape[-1]
    tm = min(512, S)

    def body(x_ref, wg_ref, wu_ref, wd_ref, lg_ref, lb_ref, xo_ref, xb_ref, z_ref, g_ref, u_ref):
        x = x_ref[...]
        xb = x.astype(BF)
        acc = jnp.zeros((tm, D), F32)
        for q in range(NQ):
            g = _nn(xb, wg_ref[q])
            u = _nn(xb, wu_ref[q])
            g_ref[q] = g.astype(BF)
            u_ref[q] = u.astype(BF)
            a = g * jax.nn.sigmoid(g) * u
            acc = acc + _nn(a.astype(BF), wd_ref[q])
        z = ALPHA * x + 0.5 * acc
        xo = _ln_fwd(z, lg_ref[...], lb_ref[...])
        z_ref[...] = z
        xo_ref[...] = xo
        xb_ref[...] = xo.astype(BF)

    row = pl.BlockSpec((tm, D), lambda i: (i, 0))
    qrow = pl.BlockSpec((NQ, tm, fq), lambda i: (0, i, 0))
    return _pcall(
        body, [x, wg, wu, wd, lg, lb], name="ffn_fwd", grid=(S // tm,),
        in_specs=[row, _quarters(wg), _quarters(wu), _quarters(wd), _full((1, D)), _full((1, D))],
        out_specs=[row, row, row, qrow, qrow],
        out_shape=[jax.ShapeDtypeStruct((S, D), F32), jax.ShapeDtypeStruct((S, D), BF),
                   jax.ShapeDtypeStruct((S, D), F32), jax.ShapeDtypeStruct((NQ, S, fq), BF),
                   jax.ShapeDtypeStruct((NQ, S, fq), BF)],
        rider=rider, edges=_edges_1d(S // tm))


def _ffn_bwd(dxo, z, g, u, wg, wu, wd, lg, rider=None):
    S, D = dxo.shape
    fq = wg.shape[-1]
    tm = min(256, S)
    nt = S // tm

    def body(dxo0_ref, z0_ref, dxo1_ref, z1_ref, g_ref, u_ref, wg_ref, wu_ref, wd_ref, lg_ref,
             dx_ref, df_ref, dg_ref, du_ref, a_ref, ln_ref, dz_ref):
        i = pl.program_id(0)

        @pl.when(i == 0)
        def _():
            dy0 = dxo0_ref[...]
            dz0, xhat0 = _ln_bwd(dy0, z0_ref[...], lg_ref[...])
            dz_ref[...] = dz0
            ln_ref[...] = jnp.zeros_like(ln_ref)
            ln_ref[0:1, :] += jnp.sum(dy0 * xhat0, axis=0, keepdims=True)
            ln_ref[1:2, :] += jnp.sum(dy0, axis=0, keepdims=True)

        dz = dz_ref[...]
        dfb = (0.5 * dz).astype(BF)
        df_ref[...] = dfb
        acc = ALPHA * dz
        for q in range(NQ):
            da = _nt(dfb, wd_ref[q])
            gg = g_ref[q].astype(F32)
            uu = u_ref[q].astype(F32)
            sg = jax.nn.sigmoid(gg)
            silu = gg * sg
            a_ref[q] = (silu * uu).astype(BF)
            dgb = (da * uu * (sg * (1.0 + gg * (1.0 - sg)))).astype(BF)
            dub = (da * silu).astype(BF)
            dg_ref[q] = dgb
            du_ref[q] = dub
            acc = acc + _nt(dgb, wg_ref[q]) + _nt(dub, wu_ref[q])
        dx_ref[...] = acc
        dy1 = dxo1_ref[...]
        dz1, xhat1 = _ln_bwd(dy1, z1_ref[...], lg_ref[...])
        real = (i < nt - 1).astype(F32)
        ln_ref[0:1, :] += real * jnp.sum(dy1 * xhat1, axis=0, keepdims=True)
        ln_ref[1:2, :] += real * jnp.sum(dy1, axis=0, keepdims=True)
        dz_ref[...] = dz1

    row = pl.BlockSpec((tm, D), lambda i: (i, 0))
    first = pl.BlockSpec((tm, D), lambda i: (0, 0))
    nxt = pl.BlockSpec((tm, D), lambda i: (jnp.minimum(i + 1, nt - 1), 0))
    qrow = pl.BlockSpec((NQ, tm, fq), lambda i: (0, i, 0))
    qshape = jax.ShapeDtypeStruct((NQ, S, fq), BF)
    return _pcall(
        body, [dxo, z, dxo, z, g, u, wg, wu, wd, lg], name="ffn_bwd", grid=(nt,),
        in_specs=[first, first, nxt, nxt, qrow, qrow, _quarters(wg), _quarters(wu), _quarters(wd), _full((1, D))],
        out_specs=[row, row, qrow, qrow, qrow, _full((8, D))],
        out_shape=[jax.ShapeDtypeStruct((S, D), F32), jax.ShapeDtypeStruct((S, D), BF), qshape, qshape, qshape,
                   jax.ShapeDtypeStruct((8, D), F32)],
        scratch=[pltpu.VMEM((tm, D), F32)],
        rider=rider, edges=_edges_1d(nt))


def _wgrad_gate_up(a, dg, du, rider=None):
    S, K = a.shape
    n = dg.shape[-1]
    ts = min(WGRAD_TOKENS, S)

    def body(a_ref, g_ref, u_ref, o_ref):
        @pl.when(pl.program_id(1) == 0)
        def _():
            o_ref[...] = jnp.zeros_like(o_ref)
        av = a_ref[...]
        o_ref[0:K, :] += _tn(av, g_ref[...])
        o_ref[K:2 * K, :] += _tn(av, u_ref[...])

    bspec = pl.BlockSpec((None, ts, n), lambda q, s: (q, s, 0))
    (out,), got = _pcall(
        body, [a, dg, du], name="wgrad_gate_up", grid=(NQ, S // ts),
        in_specs=[pl.BlockSpec((ts, K), lambda q, s: (s, 0)), bspec, bspec],
        out_specs=[pl.BlockSpec((None, 2 * K, n), lambda q, s: (q, 0, 0))],
        out_shape=[jax.ShapeDtypeStruct((NQ, 2 * K, n), F32)],
        rider=rider, edges=_edges_2d(NQ, S // ts))
    return out, got


def _wgrad_down(a, df):
    _, S, k = a.shape
    N = df.shape[1]
    ts = min(WGRAD_TOKENS, S)

    def body(a_ref, b_ref, o_ref):
        @pl.when(pl.program_id(1) == 0)
        def _():
            o_ref[...] = jnp.zeros_like(o_ref)
        o_ref[...] += _tn(a_ref[...], b_ref[...])

    return pl.pallas_call(
        body, name="wgrad_down", grid=(NQ, S // ts),
        in_specs=[pl.BlockSpec((None, ts, k), lambda q, s: (q, s, 0)), pl.BlockSpec((ts, N), lambda q, s: (s, 0))],
        out_specs=pl.BlockSpec((None, k, N), lambda q, s: (q, 0, 0)),
        out_shape=jax.ShapeDtypeStruct((NQ, k, N), F32),
        compiler_params=_cp(2),
    )(a, df)


def _wgrad_out(yab, yc, dzb):
    S, h = yab.shape
    D = dzb.shape[1]
    k = h // 2
    ts = min(WGRAD_TOKENS, S)

    def body(yab_ref, yc_ref, b_ref, o_ref):
        @pl.when(pl.program_id(0) == 0)
        def _():
            o_ref[...] = jnp.zeros_like(o_ref)
        b = b_ref[...]
        o_ref[0] += _tn(yab_ref[:, 0:k], b)
        o_ref[1] += _tn(yab_ref[:, k:h], b)
        o_ref[2] += _tn(yc_ref[:, 0:k], b)
        o_ref[3] += _tn(yc_ref[:, k:h], b)

    row = lambda w: pl.BlockSpec((ts, w), lambda s: (s, 0))
    return pl.pallas_call(
        body, name="wgrad_out", grid=(S // ts,),
        in_specs=[row(h), row(h), row(D)], out_specs=_full((NQ, k, D)),
        out_shape=jax.ShapeDtypeStruct((NQ, k, D), F32),
        compiler_params=_cp(1),
    )(yab, yc, dzb)


def _proj(xb, wc):
    S, D = xb.shape
    n = wc.shape[-1]
    n1 = D_POOL + 3 * D_CONV
    n2 = NQ * n - n1
    tm = min(1024, S)

    def body(x_ref, w_ref, p_ref, qkv_ref):
        x = x_ref[...]
        for q in range(NQ):
            r = _nn(x, w_ref[q])
            lo, hi = q * n, (q + 1) * n
            if hi <= n1:
                p_ref[:, lo:hi] = r
            elif lo >= n1:
                qkv_ref[:, lo - n1:hi - n1] = r.astype(BF)
            else:
                p_ref[:, lo:n1] = r[:, 0:n1 - lo]
                qkv_ref[:, 0:hi - n1] = r[:, n1 - lo:n].astype(BF)

    row = lambda w: pl.BlockSpec((tm, w), lambda i: (i, 0))
    return pl.pallas_call(
        body, name="mix_proj", grid=(S // tm,),
        in_specs=[row(D), _quarters(wc)],
        out_specs=[row(n1), row(n2)],
        out_shape=[jax.ShapeDtypeStruct((S, n1), F32), jax.ShapeDtypeStruct((S, n2), BF)],
        compiler_params=_cp(1),
    )(xb, wc)


def _mm_exact(a, b, name):
    def body(a_ref, b_ref, o_ref):
        o_ref[...] = jnp.dot(a_ref[...], b_ref[...].astype(F32), preferred_element_type=F32,
                             precision=lax.Precision.HIGHEST)

    return pl.pallas_call(
        body, name=name, in_specs=[_full(a.shape), _full(b.shape)], out_specs=_full((a.shape[0], b.shape[1])),
        out_shape=jax.ShapeDtypeStruct((a.shape[0], b.shape[1]), F32),
        compiler_params=pltpu.CompilerParams(vmem_limit_bytes=VMEM_LIMIT),
    )(a, b)


NB_ROWS = 2 * NA_ROWS - 1
NB_COLS = 2 * NA_COLS


def _bias_constants():
    c = np.arange(GRID_W)
    col_start = np.clip(c - NA_COLS // 2, 0, GRID_W - NA_COLS)
    valid = (c[None, :] >= col_start[:, None]) & (c[None, :] < col_start[:, None] + NA_COLS)
    dc = np.clip(c[None, :] - c[:, None], -(NA_COLS - 1), NA_COLS - 1) + (NA_COLS - 1)
    cq, ck = np.meshgrid(c, c, indexing="ij")
    onehot = np.zeros((HG, NB_COLS, GRID_W, HG, GRID_W), np.float32)
    for h in range(HG):
        onehot[h, dc[cq, ck], ck, h, cq] = 1.0
    mask_kq = np.where(valid.T, 0.0, NEG_INF).astype(np.float32)
    mask = np.tile(mask_kq, (NB_ROWS, HG))
    return onehot.reshape(HG * NB_COLS, GRID_W * LW), mask


def _bias_table(rpb, onehot, mask):
    ngr = NA_HEADS // HG
    r = rpb.reshape(ngr, HG, NB_ROWS, NB_COLS - 1).transpose(0, 2, 1, 3)
    r = jnp.pad(r, ((0, 0), (0, 0), (0, 0), (0, 1))).reshape(ngr * NB_ROWS, HG * NB_COLS)
    t = _mm_exact(r, onehot, "bias_expand")
    return t.reshape(ngr, NB_ROWS * GRID_W, LW) + mask[None]


def _bias_grad(dt, onehot_t):
    ngr = NA_HEADS // HG
    g = _mm_exact(dt.reshape(ngr * NB_ROWS, GRID_W * LW), onehot_t, "bias_reduce")
    g = g.reshape(ngr, NB_ROWS, HG, NB_COLS)[..., :NB_COLS - 1]
    return g.transpose(0, 2, 1, 3).reshape(NA_HEADS, NB_ROWS, NB_COLS - 1)


def _attn_rows(S):
    rows = S // GRID_W
    rb = min(32, rows)
    return rows, rb


def _head_masks():
    lane = lax.broadcasted_iota(jnp.int32, (GRID_W, LW), 1)
    return [(lane >= HEAD_DIM * h) & (lane < HEAD_DIM * (h + 1)) for h in range(HG)]


def _stack_heads(x, masks):
    zero = jnp.zeros_like(x)
    return jnp.concatenate([jnp.where(m, x, zero) for m in masks], axis=0)


def _unstack_heads(x2, masks):
    out = x2[0:GRID_W]
    for h in range(1, HG):
        out = jnp.where(masks[h], x2[h * GRID_W:(h + 1) * GRID_W], out)
    return out


def _attn_step(r, rows, q, k_ref, v_ref, b_ref, masks):
    rs = jnp.clip(r - NA_ROWS // 2, 0, rows - NA_ROWS)
    s0 = rs - r + (NA_ROWS - 1)
    q2 = _stack_heads(q, masks)
    ks = pl.ds(pl.multiple_of(rs * GRID_W, GRID_W), NA_ROWS * GRID_W)
    kb = k_ref[ks, :]
    vb = v_ref[ks, :]
    bs = pl.ds(pl.multiple_of(s0 * GRID_W, GRID_W), NA_ROWS * GRID_W)
    s = _nt(kb, q2) * (HEAD_DIM ** -0.5) + b_ref[0, bs, :]
    m = jnp.max(s, axis=0, keepdims=True)
    p = jnp.exp(s - m)
    p = p / jnp.sum(p, axis=0, keepdims=True)
    return p, q2, kb, vb, ks, bs


def _attn_fwd(qkv, bias, rider=None):
    S = qkv.shape[0]
    rows, rb = _attn_rows(S)
    tq = rb * GRID_W
    ngr = NA_HEADS // HG

    def body(q_ref, k_ref, v_ref, b_ref, o_ref):
        base = pl.program_id(1) * rb
        masks = _head_masks()

        def step(i, carry):
            qs = pl.ds(pl.multiple_of(i * GRID_W, GRID_W), GRID_W)
            p, _, _, vb, _, _ = _attn_step(base + i, rows, q_ref[qs, :], k_ref, v_ref, b_ref, masks)
            o_ref[qs, :] = _unstack_heads(_tn(p.astype(BF), vb), masks).astype(BF)
            return carry

        lax.fori_loop(0, rb, step, 0, unroll=rb)

    return _pcall(
        body, [qkv, qkv, qkv, bias], name="attn_fwd", grid=(ngr, rows // rb),
        in_specs=[pl.BlockSpec((tq, LW), lambda h, r: (r, h)),
                  pl.BlockSpec((S, LW), lambda h, r: (0, ngr + h)),
                  pl.BlockSpec((S, LW), lambda h, r: (0, 2 * ngr + h)),
                  pl.BlockSpec((1, bias.shape[1], LW), lambda h, r: (h, 0, 0))],
        out_specs=[pl.BlockSpec((tq, LW), lambda h, r: (r, h))],
        out_shape=[jax.ShapeDtypeStruct((S, D_NA), BF)],
        rider=rider, edges=_edges_2d(ngr, rows // rb))


def _attn_bwd(qkv, bias, dycat, rider=None):
    S = qkv.shape[0]
    rows, rb = _attn_rows(S)
    tq = rb * GRID_W
    ngr = NA_HEADS // HG
    scale = HEAD_DIM ** -0.5

    def body(q_ref, k_ref, v_ref, b_ref, do_ref, dq_ref, dk_ref, dv_ref, db_ref, dka_ref, dva_ref):
        base = pl.program_id(1) * rb
        last = pl.program_id(1) == pl.num_programs(1) - 1
        masks = _head_masks()

        @pl.when(pl.program_id(1) == 0)
        def _():
            dka_ref[...] = jnp.zeros_like(dka_ref)
            dva_ref[...] = jnp.zeros_like(dva_ref)
            db_ref[...] = jnp.zeros_like(db_ref)

        def step(i, carry):
            qs = pl.ds(pl.multiple_of(i * GRID_W, GRID_W), GRID_W)
            p, q2, kb, vb, ks, bs = _attn_step(base + i, rows, q_ref[qs, :], k_ref, v_ref, b_ref, masks)
            do2 = _stack_heads(do_ref[qs, :].astype(BF), masks)
            dp = _nt(vb, do2)
            ds = p * (dp - jnp.sum(p * dp, axis=0, keepdims=True))
            db_ref[0, bs, :] += ds
            dsb = ds.astype(BF)
            dq_ref[qs, :] = _unstack_heads(_tn(dsb, kb) * scale, masks).astype(BF)
            dka_ref[ks, :] += _nn(dsb, q2) * scale
            dva_ref[ks, :] += _nn(p.astype(BF), do2)
            return carry

        lax.fori_loop(0, rb, step, 0, unroll=rb)

        @pl.when(last)
        def _():
            dk_ref[...] = dka_ref[...].astype(BF)
            dv_ref[...] = dva_ref[...].astype(BF)

    nb = bias.shape[1]
    once = dict(pipeline_mode=pl.Buffered(1))
    nd = D_NA // LW
    return _pcall(
        body, [qkv, qkv, qkv, bias, dycat], name="attn_bwd", grid=(ngr, rows // rb),
        in_specs=[pl.BlockSpec((tq, LW), lambda h, r: (r, h)),
                  pl.BlockSpec((S, LW), lambda h, r: (0, ngr + h), **once),
                  pl.BlockSpec((S, LW), lambda h, r: (0, 2 * ngr + h), **once),
                  pl.BlockSpec((1, nb, LW), lambda h, r: (h, 0, 0)),
                  pl.BlockSpec((tq, LW), lambda h, r: (r, nd + h))],
        out_specs=[pl.BlockSpec((tq, LW), lambda h, r: (r, h)),
                   pl.BlockSpec((S, LW), lambda h, r: (0, h)),
                   pl.BlockSpec((S, LW), lambda h, r: (0, h)),
                   pl.BlockSpec((1, nb, LW), lambda h, r: (h, 0, 0))],
        out_shape=[jax.ShapeDtypeStruct((S, D_NA), BF)] * 3 + [jax.ShapeDtypeStruct((ngr, nb, LW), F32)],
        scratch=[pltpu.VMEM((S, LW), F32), pltpu.VMEM((S, LW), F32)],
        rider=rider, edges=_edges_2d(ngr, rows // rb))


def _halo_specs(tm, width, S):
    hb = tm // HALO
    last = S // HALO - 1
    return [pl.BlockSpec((tm, width), lambda i: (i, 0)),
            pl.BlockSpec((HALO, width), lambda i: (jnp.maximum(i * hb - 1, 0), 0)),
            pl.BlockSpec((HALO, width), lambda i: (jnp.minimum((i + 1) * hb, last), 0))]


def _with_halo(cur_ref, prev_ref, next_ref, i, nt):
    prev = jnp.where(i > 0, prev_ref[...], 0.0)
    nxt = jnp.where(i < nt - 1, next_ref[...], 0.0)
    return jnp.concatenate([prev, cur_ref[...], nxt], axis=0)


def _shift(a, k):
    n = a.shape[0]
    return pltpu.roll(a, k % n, 0)


def _pool_lanes(n):
    lane = lax.broadcasted_iota(jnp.int32, (n, D_POOL), 1)
    group = D_POOL // len(POOL_WINDOWS)
    return [lane < group * (j + 1) for j in range(len(POOL_WINDOWS) - 1)]


def _by_window(lanes, vals):
    return jnp.where(lanes[0], vals[0], jnp.where(lanes[1], vals[1], jnp.where(lanes[2], vals[2], vals[3])))


def _pool_count(lanes, t, S):
    back = _by_window(lanes, tuple(w // 2 for w in POOL_WINDOWS))
    lo = jnp.maximum(t - back, 0)
    hi = jnp.minimum(t + back, S)
    return jnp.maximum(hi - lo, 1).astype(F32)


def _pool_p(u, lanes, cnt):
    a = u + _shift(u, 1)
    b = _shift(a, 1) + _shift(a, -1)
    c = _shift(b, 2) + _shift(b, -2)
    d = _shift(c, 4) + _shift(c, -4)
    return _by_window(lanes, (a, b, c, d)) / cnt - u


def _mixab_fwd(pabc, wblk, vec):
    S = pabc.shape[0]
    tm = min(512, S)
    nt = S // tm
    n = tm + 2 * HALO
    tile = slice(HALO, HALO + tm)

    def body(cur_ref, prev_ref, next_ref, w_ref, vec_ref, o_ref):
        i = pl.program_id(0)
        ext = _with_halo(cur_ref, prev_ref, next_ref, i, nt)
        lanes = _pool_lanes(n)
        t = i * tm - HALO + lax.broadcasted_iota(jnp.int32, (n, D_POOL), 0)
        p = _pool_p(ext[:, 0:D_POOL], lanes, _pool_count(lanes, t, S))[tile]
        o_ref[:, 0:D_POOL] = (_nn(p.astype(BF), w_ref[...]) * vec_ref[0:1, :]).astype(BF)
        zc = ext[:, 512:768] * ext[:, 768:1024]
        conv = vec_ref[1:2, :] * _shift(zc, 1) + vec_ref[2:3, :] * zc + vec_ref[3:4, :] * _shift(zc, -1)
        o_ref[:, D_POOL:D_POOL + D_CONV] = (ext[tile, 256:512] * conv[tile]).astype(BF)

    return pl.pallas_call(
        body, name="mixab_fwd", grid=(nt,),
        in_specs=_halo_specs(tm, 1024, S) + [_full((D_POOL, D_POOL)), _full((8, D_POOL))],
        out_specs=pl.BlockSpec((tm, D_POOL + D_CONV), lambda i: (i, 0)),
        out_shape=jax.ShapeDtypeStruct((S, D_POOL + D_CONV), BF),
        compiler_params=_cp(1),
    )(pabc, pabc, pabc, wblk, vec)


def _mixab_bwd(pabc, dycat, wblk, vec):
    S = pabc.shape[0]
    tm = min(512, S)
    nt = S // tm
    n = tm + 2 * HALO
    tile = slice(HALO, HALO + tm)

    def body(cur_ref, prev_ref, next_ref, dcur_ref, dprev_ref, dnext_ref, w_ref, vec_ref, o_ref, dw_ref, dvec_ref):
        i = pl.program_id(0)

        @pl.when(i == 0)
        def _():
            dw_ref[...] = jnp.zeros_like(dw_ref)
            dvec_ref[...] = jnp.zeros_like(dvec_ref)

        ext = _with_halo(cur_ref, prev_ref, next_ref, i, nt)
        dext = _with_halo(dcur_ref, dprev_ref, dnext_ref, i, nt)
        lanes = _pool_lanes(n)
        t = i * tm - HALO + lax.broadcasted_iota(jnp.int32, (n, D_POOL), 0)
        cnt = _pool_count(lanes, t, S)
        w = w_ref[...]
        scale = vec_ref[0:1, :]
        pb = _pool_p(ext[:, 0:D_POOL], lanes, cnt)[tile].astype(BF)
        dya = dext[:, 0:D_POOL]
        dvec_ref[0:1, :] += jnp.sum(dya[tile] * _nn(pb, w), axis=0, keepdims=True)
        dqb = (dya * scale).astype(BF)
        dw_ref[...] += _tn(pb, dqb[tile])
        dp = _nt(dqb, w)
        r = dp / cnt
        a = r + _shift(r, -1)
        b = _shift(a, 1) + _shift(a, -1)
        c = _shift(b, 2) + _shift(b, -2)
        d = _shift(c, 4) + _shift(c, -4)
        o_ref[:, 0:256] = (_by_window(lanes, (a, b, c, d)) - dp)[tile].astype(BF)
        gb, gc, hh = ext[:, 256:512], ext[:, 512:768], ext[:, 768:1024]
        zc = gc * hh
        zm, zp = _shift(zc, 1), _shift(zc, -1)
        w0, w1, w2 = vec_ref[1:2, :], vec_ref[2:3, :], vec_ref[3:4, :]
        dyb = dext[:, D_POOL:D_POOL + D_CONV]
        dconv = dyb * gb
        o_ref[:, 256:512] = (dyb * (w0 * zm + w1 * zc + w2 * zp))[tile].astype(BF)
        dzc = w0 * _shift(dconv, -1) + w1 * dconv + w2 * _shift(dconv, 1)
        o_ref[:, 512:768] = (dzc * hh)[tile].astype(BF)
        o_ref[:, 768:1024] = (dzc * gc)[tile].astype(BF)
        dct = dconv[tile]
        dvec_ref[1:2, :] += jnp.sum(dct * zm[tile], axis=0, keepdims=True)
        dvec_ref[2:3, :] += jnp.sum(dct * zc[tile], axis=0, keepdims=True)
        dvec_ref[3:4, :] += jnp.sum(dct * zp[tile], axis=0, keepdims=True)

    return pl.pallas_call(
        body, name="mixab_bwd", grid=(nt,),
        in_specs=_halo_specs(tm, 1024, S) + _halo_specs(tm, 512, S) + [_full((D_POOL, D_POOL)), _full((8, D_POOL))],
        out_specs=[pl.BlockSpec((tm, 1024), lambda i: (i, 0)), _full((D_POOL, D_POOL)), _full((8, D_POOL))],
        out_shape=[jax.ShapeDtypeStruct((S, 1024), BF), jax.ShapeDtypeStruct((D_POOL, D_POOL), F32),
                   jax.ShapeDtypeStruct((8, D_POOL), F32)],
        compiler_params=_cp(1),
    )(pabc, pabc, pabc, dycat, dycat, dycat, wblk, vec)


def _mixout_fwd(yab, yc, x, wo, lg, lb):
    S, D = x.shape
    tm = min(512, S)
    h = yab.shape[1]
    k = h // 2

    def body(yab_ref, yc_ref, x_ref, w_ref, lg_ref, lb_ref, xo_ref, xb_ref, z_ref):
        y = (_nn(yab_ref[:, 0:k], w_ref[0]) + _nn(yab_ref[:, k:h], w_ref[1])
             + _nn(yc_ref[:, 0:k], w_ref[2]) + _nn(yc_ref[:, k:h], w_ref[3]))
        z = ALPHA * x_ref[...] + y
        xo = _ln_fwd(z, lg_ref[...], lb_ref[...])
        z_ref[...] = z
        xo_ref[...] = xo
        xb_ref[...] = xo.astype(BF)

    row = lambda w: pl.BlockSpec((tm, w), lambda i: (i, 0))
    return pl.pallas_call(
        body, name="mixout_fwd", grid=(S // tm,),
        in_specs=[row(h), row(h), row(D), _quarters(wo), _full((1, D)), _full((1, D))],
        out_specs=[row(D), row(D), row(D)],
        out_shape=[jax.ShapeDtypeStruct((S, D), F32), jax.ShapeDtypeStruct((S, D), BF),
                   jax.ShapeDtypeStruct((S, D), F32)],
        compiler_params=_cp(1),
    )(yab, yc, x, wo, lg, lb)


def _mixout_bwd(dxo, z, wo, lg, rider=None):
    S, D = dxo.shape
    k = wo.shape[-2]
    tm = min(512, S)
    nt = S // tm

    def body(dxo0_ref, z0_ref, dxo1_ref, z1_ref, w_ref, lg_ref, dres_ref, dzb_ref, dy_ref, ln_ref, dz_ref):
        i = pl.program_id(0)

        @pl.when(i == 0)
        def _():
            dy0 = dxo0_ref[...]
            dz0, xhat0 = _ln_bwd(dy0, z0_ref[...], lg_ref[...])
            dz_ref[...] = dz0
            ln_ref[...] = jnp.zeros_like(ln_ref)
            ln_ref[0:1, :] += jnp.sum(dy0 * xhat0, axis=0, keepdims=True)
            ln_ref[1:2, :] += jnp.sum(dy0, axis=0, keepdims=True)

        dz = dz_ref[...]
        dzb = dz.astype(BF)
        dres_ref[...] = ALPHA * dz
        dzb_ref[...] = dzb
        for q in range(NQ):
            dy_ref[:, q * k:(q + 1) * k] = _nt(dzb, w_ref[q])
        dy1 = dxo1_ref[...]
        dz1, xhat1 = _ln_bwd(dy1, z1_ref[...], lg_ref[...])
        real = (i < nt - 1).astype(F32)
        ln_ref[0:1, :] += real * jnp.sum(dy1 * xhat1, axis=0, keepdims=True)
        ln_ref[1:2, :] += real * jnp.sum(dy1, axis=0, keepdims=True)
        dz_ref[...] = dz1

    row = lambda w: pl.BlockSpec((tm, w), lambda i: (i, 0))
    first = pl.BlockSpec((tm, D), lambda i: (0, 0))
    nxt = pl.BlockSpec((tm, D), lambda i: (jnp.minimum(i + 1, nt - 1), 0))
    return _pcall(
        body, [dxo, z, dxo, z, wo, lg], name="mixout_bwd", grid=(nt,),
        in_specs=[first, first, nxt, nxt, _quarters(wo), _full((1, D))],
        out_specs=[row(D), row(D), row(NQ * k), _full((8, D))],
        out_shape=[jax.ShapeDtypeStruct((S, D), F32), jax.ShapeDtypeStruct((S, D), BF),
                   jax.ShapeDtypeStruct((S, NQ * k), F32), jax.ShapeDtypeStruct((8, D), F32)],
        scratch=[pltpu.VMEM((tm, D), F32)],
        rider=rider, edges=_edges_1d(nt))


def _take_cols(refs, lo, hi):
    parts, off = [], 0
    for r in refs:
        w = r.shape[1]
        a, b = max(lo, off), min(hi, off + w)
        if a < b:
            parts.append(r[:, a - off:b - off])
        off += w
    return parts[0] if len(parts) == 1 else jnp.concatenate(parts, axis=1)


def _proj_bwd(dres, dparts, wc, rider=None):
    S, D = dres.shape
    n = wc.shape[-1]
    tm = min(1024, S)
    np_ = len(dparts)

    def body(*refs):
        dres_ref, d_refs, w_ref, dx_ref = refs[0], refs[1:1 + np_], refs[1 + np_], refs[2 + np_]
        acc = dres_ref[...]
        for q in range(NQ):
            acc = acc + _nt(_take_cols(d_refs, q * n, (q + 1) * n), w_ref[q])
        dx_ref[...] = acc

    row = lambda w: pl.BlockSpec((tm, w), lambda i: (i, 0))
    return _pcall(
        body, [dres, *dparts, wc], name="mix_proj_bwd", grid=(S // tm,),
        in_specs=[row(D)] + [row(d.shape[1]) for d in dparts] + [_quarters(wc)],
        out_specs=[row(D)],
        out_shape=[jax.ShapeDtypeStruct((S, D), F32)],
        rider=rider, edges=_edges_1d(S // tm))


def _wgrad_in(a, dparts, n, rider=None):
    S, K = a.shape
    ts = min(WGRAD_TOKENS // 2, S)
    np_ = len(dparts)

    def body(*refs):
        a_ref, d_refs, o_ref = refs[0], refs[1:1 + np_], refs[1 + np_]

        @pl.when(pl.program_id(0) == 0)
        def _():
            o_ref[...] = jnp.zeros_like(o_ref)
        av = a_ref[...]
        for q in range(NQ):
            o_ref[q] += _tn(av, _take_cols(d_refs, q * n, (q + 1) * n))

    row = lambda w: pl.BlockSpec((ts, w), lambda s: (s, 0))
    (out,), got = _pcall(
        body, [a, *dparts], name="wgrad_in", grid=(S // ts,),
        in_specs=[row(K)] + [row(d.shape[1]) for d in dparts], out_specs=[_full((NQ, K, n))],
        out_shape=[jax.ShapeDtypeStruct((NQ, K, n), F32)],
        rider=rider, edges=_edges_1d(S // ts))
    return out, got


def _loss_head(y, target):
    S, D = y.shape
    tm = min(512, S)

    def body(y_ref, t_ref, l_ref, dy_ref):
        @pl.when(pl.program_id(0) == 0)
        def _():
            l_ref[...] = jnp.zeros_like(l_ref)
        e = y_ref[...] - t_ref[...]
        dy_ref[...] = e * (1.0 / D)
        part = jnp.sum(jnp.sum(e * e, axis=1, keepdims=True) * (1.0 / D), axis=0, keepdims=True)
        l_ref[...] += 0.5 * part

    row = pl.BlockSpec((tm, D), lambda i: (i, 0))
    return pl.pallas_call(
        body, name="loss_head", grid=(S // tm,),
        in_specs=[row, row], out_specs=[_full((8, 128)), row],
        out_shape=[jax.ShapeDtypeStruct((8, 128), F32), jax.ShapeDtypeStruct((S, D), F32)],
        compiler_params=_cp(1),
    )(y, target)


def _adamw_update(w, g, m, v):
    mn = ADAM_B1 * m + (1.0 - ADAM_B1) * g
    vn = ADAM_B2 * v + (1.0 - ADAM_B2) * (g * g)
    m_hat = mn / (1.0 - ADAM_B1 ** ADAM_STEP)
    v_hat = vn / (1.0 - ADAM_B2 ** ADAM_STEP)
    return -ADAM_LR * (m_hat / (jnp.sqrt(v_hat) + ADAM_EPS) + ADAM_WD * w), mn, vn


def _adamw(w, g, m, v):
    shape = w.shape
    cols = shape[-1]
    rows = int(np.prod(shape[:-1]))
    w2, g2, m2, v2 = (a.reshape(rows, cols) for a in (w, g, m, v))
    tr = rows
    for cand in (512, 352, 256):
        if rows > cand and rows % cand == 0:
            tr = cand
            break

    def body(w_ref, g_ref, m_ref, v_ref, d_ref, mo_ref, vo_ref):
        d_ref[...], mo_ref[...], vo_ref[...] = _adamw_update(w_ref[...], g_ref[...], m_ref[...], v_ref[...])

    spec = pl.BlockSpec((tr, cols), lambda i: (i, 0))
    outs = pl.pallas_call(
        body, name=f"adamw_{rows}x{cols}", grid=(rows // tr,),
        in_specs=[spec] * 4, out_specs=[spec] * 3,
        out_shape=[jax.ShapeDtypeStruct((rows, cols), F32)] * 3,
        compiler_params=_cp(1),
    )(w2, g2, m2, v2)
    return tuple(o.reshape(shape) for o in outs)


def _half_tile(h):
    return h if h <= 512 else 512


def _add_chip(g, recv):
    _, R, C = g.shape
    h = R // 2
    tr = _half_tile(h)
    nb = h // tr

    def body(a_ref, b_ref, o_ref, ob_ref):
        s = a_ref[...] + b_ref[...]
        ob_ref[...] = s.astype(BF)

        @pl.when(pl.program_id(1) == 2 * lax.axis_index("x") + lax.axis_index("y"))
        def _():
            o_ref[...] = s[0]

    half = pl.BlockSpec((1, tr, C), lambda i, q: (q, i, 0))
    mine = pl.BlockSpec((1, tr, C), lambda i, q: (q, lax.axis_index("c") * nb + i, 0))
    return pl.pallas_call(
        body, name=f"rs_add_chip_{R}x{C}", grid=(nb, NQ), in_specs=[mine, half],
        out_specs=[pl.BlockSpec((tr, C), lambda i, q: (i, 0)), half],
        out_shape=[jax.ShapeDtypeStruct((h, C), F32), jax.ShapeDtypeStruct((NQ, h, C), BF)],
        compiler_params=_cp(2),
    )(g, recv)


def _add_final(chip, recv):
    h, C = chip.shape
    tr = _half_tile(h)
    nb = h // tr

    def body(a_ref, b_ref, o_ref):
        s = a_ref[...]
        for j in range(3):
            s = s + b_ref[j].astype(F32)
        o_ref[...] = s

    return pl.pallas_call(
        body, name=f"rs_add_final_{h}x{C}", grid=(nb,),
        in_specs=[pl.BlockSpec((tr, C), lambda i: (i, 0)), pl.BlockSpec((3, tr, C), lambda i: (0, i, 0))],
        out_specs=pl.BlockSpec((tr, C), lambda i: (lax.axis_index("c") * nb + i, 0)),
        out_shape=jax.ShapeDtypeStruct((2 * h, C), F32),
        compiler_params=_cp(1),
    )(chip, recv)


COMM = pltpu.CompilerParams(has_side_effects=True)


def _place():
    x, y, c = lax.axis_index("x"), lax.axis_index("y"), lax.axis_index("c")
    chips = [(1 - x, y), (x, 1 - y), (1 - x, 1 - y)]
    return x, y, c, chips


def _half0(ref, c):
    n = ref.shape[0] // 2
    return ref.at[pl.ds(c * n, n)]


def _gather_ici(shards):
    n = len(shards)

    def copies(r_in, r_out, ssem, rsem, base):
        x, y, c, chips = _place()
        q = 2 * x + y
        return [pltpu.make_async_remote_copy(
            src_ref=_half0(r_in[i], c), dst_ref=_half0(r_out[i].at[q], c), send_sem=ssem.at[base + 3 * i + j],
            recv_sem=rsem.at[base + 3 * i + j], device_id=(*chip, c), device_id_type=MESH)
            for i in range(n) for j, chip in enumerate(chips)]

    return _Rider("ici", shards, [jax.ShapeDtypeStruct((NQ,) + s.shape, BF) for s in shards], {}, 3 * n, copies)


def _gather_d2d(bufs):
    n = len(bufs)

    def copies(r_in, r_out, ssem, rsem, base):
        x, y, c, chips = _place()
        return [pltpu.make_async_remote_copy(
            src_ref=_half0(r_in[i].at[2 * cx + cy], c), dst_ref=_half0(r_out[i].at[2 * cx + cy], c),
            send_sem=ssem.at[base + 3 * i + j], recv_sem=rsem.at[base + 3 * i + j], device_id=(x, y, 1 - c),
            device_id_type=MESH) for i in range(n) for j, (cx, cy) in enumerate(chips)]

    return _Rider("d2d", bufs, [jax.ShapeDtypeStruct(b.shape, b.dtype) for b in bufs], {i: i for i in range(n)},
                  3 * n, copies)


def _gather_small(small):
    sr = small.shape[0]

    def body(s_ref, o_ref, send_sems, recv_sems):
        x, y, c, chips = _place()
        o_ref[2 * x + y] = s_ref[...]
        cps = [pltpu.make_async_remote_copy(
            src_ref=s_ref, dst_ref=o_ref.at[2 * x + y], send_sem=send_sems.at[j], recv_sem=recv_sems.at[j],
            device_id=(*chip, c), device_id_type=MESH) for j, chip in enumerate(chips)]
        for cp in cps:
            cp.start()
        for j, (cx, cy) in enumerate(chips):
            pltpu.make_async_remote_copy(
                src_ref=s_ref, dst_ref=o_ref.at[2 * cx + cy], send_sem=send_sems.at[j], recv_sem=recv_sems.at[j],
                device_id=(cx, cy, c), device_id_type=MESH).wait_recv()
        for cp in cps:
            cp.wait_send()

    vm = pl.BlockSpec(memory_space=pltpu.VMEM)
    return pl.pallas_call(
        body, name="gather_small", in_specs=[vm], out_specs=vm,
        out_shape=jax.ShapeDtypeStruct((NQ, sr, 128), F32),
        scratch_shapes=[pltpu.SemaphoreType.DMA((3,)), pltpu.SemaphoreType.DMA((3,))],
        compiler_params=COMM,
    )(small)


def _swap_halves(gs):
    n = len(gs)

    def copies(r_in, r_out, ssem, rsem, base):
        x, y, c, _ = _place()
        cps = []
        for i in range(n):
            h = r_in[i].shape[1] // 2
            cps.append(pltpu.make_async_remote_copy(
                src_ref=r_in[i].at[:, pl.ds((1 - c) * h, h), :], dst_ref=r_out[i], send_sem=ssem.at[base + i],
                recv_sem=rsem.at[base + i], device_id=(x, y, 1 - c), device_id_type=MESH))
        return cps

    return _Rider("swap", gs, [jax.ShapeDtypeStruct((NQ, g.shape[1] // 2, g.shape[2]), F32) for g in gs], {}, n,
                  copies)


def _scatter_chips(chips_b):
    n = len(chips_b)

    def copies(r_in, r_out, ssem, rsem, base):
        x, y, c, chips = _place()
        return [pltpu.make_async_remote_copy(
            src_ref=r_in[i].at[2 * cx + cy], dst_ref=r_out[i].at[j], send_sem=ssem.at[base + 3 * i + j],
            recv_sem=rsem.at[base + 3 * i + j], device_id=(cx, cy, c), device_id_type=MESH)
            for i in range(n) for j, (cx, cy) in enumerate(chips)]

    return _Rider("scatter", chips_b, [jax.ShapeDtypeStruct((3,) + s.shape[1:], BF) for s in chips_b], {}, 3 * n,
                  copies)


def _run_alone(rider, name):
    ni, no = len(rider.ins), len(rider.outs)

    def body(*refs):
        cps = rider.copies(refs[:ni], refs[ni:ni + no], refs[ni + no], refs[ni + no + 1], 0)
        for cp in cps:
            cp.start()
        for cp in cps:
            cp.wait()

    return list(pl.pallas_call(
        body, name=name, in_specs=[ANY] * ni, out_specs=[ANY] * no, out_shape=rider.outs,
        input_output_aliases=dict(rider.aliases),
        scratch_shapes=[pltpu.SemaphoreType.DMA((rider.n,)), pltpu.SemaphoreType.DMA((rider.n,))],
        compiler_params=COMM,
    )(*rider.ins))


def _join_halves(fs):
    n = len(fs)

    def copies(r_in, r_out, ssem, rsem, base):
        x, y, c, _ = _place()
        return [pltpu.make_async_remote_copy(
            src_ref=_half0(r_in[i], c), dst_ref=_half0(r_out[i], c), send_sem=ssem.at[base + i],
            recv_sem=rsem.at[base + i], device_id=(x, y, 1 - c), device_id_type=MESH) for i in range(n)]

    return _Rider("join", fs, [jax.ShapeDtypeStruct(f.shape, F32) for f in fs], {i: i for i in range(n)}, n, copies)


def _allreduce_small(v):
    r, W = v.shape

    def body(v_ref, o_ref, land_ref, send_sems, recv_sems):
        x, y, c, _ = _place()
        me = 4 * x + 2 * y + c
        cps = []
        for m in range(1, 8):
            to = (x ^ (m >> 2), y ^ ((m >> 1) & 1), c ^ (m & 1))
            cps.append(pltpu.make_async_remote_copy(
                src_ref=v_ref, dst_ref=land_ref.at[m - 1], send_sem=send_sems.at[m - 1], recv_sem=recv_sems.at[m - 1],
                device_id=to, device_id_type=MESH))
        for cp in cps:
            cp.start()
        for cp in cps:
            cp.wait()
        total = jnp.zeros((r, W), F32)
        for d in range(8):
            slot = jnp.maximum((me ^ d) - 1, 0)
            total = total + jnp.where(me == d, v_ref[...], land_ref[slot])
        o_ref[...] = total

    return pl.pallas_call(
        body, name="allreduce_small",
        in_specs=[pl.BlockSpec(memory_space=pltpu.VMEM)], out_specs=pl.BlockSpec(memory_space=pltpu.VMEM),
        out_shape=jax.ShapeDtypeStruct((r, W), F32),
        scratch_shapes=[pltpu.VMEM((7, r, W), F32), pltpu.SemaphoreType.DMA((7,)), pltpu.SemaphoreType.DMA((7,))],
        compiler_params=pltpu.CompilerParams(has_side_effects=True, vmem_limit_bytes=VMEM_LIMIT),
    )(v)


def kernel(x, ffn1_w_gate, ffn1_w_up, ffn1_w_down, ffn2_w_gate, ffn2_w_up, ffn2_w_down, w_in, pool_w, pool_scale, conv_w, rpb, w_out, ln_g, ln_b, loss_target, m_ffn1_w_gate, m_ffn1_w_up, m_ffn1_w_down, m_ffn2_w_gate, m_ffn2_w_up, m_ffn2_w_down, m_w_in, m_pool_w, m_pool_scale, m_conv_w, m_rpb, m_w_out, m_ln_g, m_ln_b, v_ffn1_w_gate, v_ffn1_w_up, v_ffn1_w_down, v_ffn2_w_gate, v_ffn2_w_up, v_ffn2_w_down, v_w_in, v_pool_w, v_pool_scale, v_conv_w, v_rpb, v_w_out, v_ln_g, v_ln_b):
    weights = dict(ffn1_w_gate=ffn1_w_gate, ffn1_w_up=ffn1_w_up, ffn1_w_down=ffn1_w_down, ffn2_w_gate=ffn2_w_gate,
                   ffn2_w_up=ffn2_w_up, ffn2_w_down=ffn2_w_down, w_in=w_in, pool_w=pool_w, pool_scale=pool_scale,
                   conv_w=conv_w, rpb=rpb, w_out=w_out, ln_g=ln_g, ln_b=ln_b)
    mom_m = dict(ffn1_w_gate=m_ffn1_w_gate, ffn1_w_up=m_ffn1_w_up, ffn1_w_down=m_ffn1_w_down, ffn2_w_gate=m_ffn2_w_gate,
                 ffn2_w_up=m_ffn2_w_up, ffn2_w_down=m_ffn2_w_down, w_in=m_w_in, pool_w=m_pool_w,
                 pool_scale=m_pool_scale, conv_w=m_conv_w, rpb=m_rpb, w_out=m_w_out, ln_g=m_ln_g, ln_b=m_ln_b)
    mom_v = dict(ffn1_w_gate=v_ffn1_w_gate, ffn1_w_up=v_ffn1_w_up, ffn1_w_down=v_ffn1_w_down, ffn2_w_gate=v_ffn2_w_gate,
                 ffn2_w_up=v_ffn2_w_up, ffn2_w_down=v_ffn2_w_down, w_in=v_w_in, pool_w=v_pool_w,
                 pool_scale=v_pool_scale, conv_w=v_conv_w, rpb=v_rpb, w_out=v_w_out, ln_g=v_ln_g, ln_b=v_ln_b)
    order = list(weights)
    L = ffn1_w_gate.shape[0]
    xi, yi, ci = lax.axis_index("x"), lax.axis_index("y"), lax.axis_index("c")
    q_me = 2 * xi + yi
    x2 = x[0]
    target = loss_target[0]
    D = x2.shape[1]
    n_in = w_in.shape[-1]

    small = jnp.concatenate([ln_g.reshape(-1), ln_b.reshape(-1), conv_w.reshape(-1)])
    n_small = small.shape[0]
    small_rows = -(-n_small // (8 * 128)) * 8
    small = jnp.pad(small, (0, small_rows * 128 - n_small)).reshape(small_rows, 128)
    small_all = _gather_small(small).reshape(NQ, small_rows * 128)[:, :n_small]
    dq4 = D // NQ
    n_ln = L * 3 * dq4
    ln_g_all = small_all[:, :n_ln].reshape(NQ, L, 3, dq4).transpose(1, 2, 0, 3).reshape(L, 3, D)
    ln_b_all = small_all[:, n_ln:2 * n_ln].reshape(NQ, L, 3, dq4).transpose(1, 2, 0, 3).reshape(L, 3, D)
    conv_all = small_all[:, 2 * n_ln:].reshape(NQ, L, 3, D_CONV // NQ).transpose(1, 2, 0, 3).reshape(L, 3, D_CONV)

    def layer_shards(l):
        return [w[l].astype(BF) for w in (ffn1_w_gate, ffn1_w_up, ffn1_w_down, w_in, w_out, ffn2_w_gate, ffn2_w_up,
                                          ffn2_w_down)]

    def own_quarter(bufs, shards):
        return [lax.dynamic_update_slice(b, s[None], (q_me,) + (0,) * s.ndim) for b, s in zip(bufs, shards)]

    shards = [layer_shards(l) for l in range(L)]
    landed = _run_alone(_gather_ici(shards[0][:3]), "gather_ici")
    weights_of = [own_quarter(_run_alone(_gather_d2d(landed), "gather_d2d"), shards[0][:3])] + [None] * (L - 1)

    onehot_np, mask_np = _bias_constants()
    onehot, onehot_t, mask = jnp.asarray(onehot_np, BF), jnp.asarray(onehot_np.T.copy(), BF), jnp.asarray(mask_np)
    ng = len(POOL_WINDOWS)
    pg = D_POOL // ng
    saved = []
    h = x2
    hb = x2.astype(BF)
    for l in range(L):
        nxt = shards[l + 1] if l + 1 < L else None
        wg1, wu1, wd1 = weights_of[l][:3]
        eye = jnp.eye(ng, dtype=F32)
        wblk = (pool_w[l][:, :, None, :] * eye[:, None, :, None]).reshape(D_POOL, D_POOL).astype(BF)
        vec = jnp.concatenate([pool_scale[l][None], conv_all[l], jnp.zeros((4, D_POOL), F32)], axis=0)
        bias = _bias_table(rpb[l], onehot, mask)
        lg = [ln_g_all[l, j][None] for j in range(3)]
        lb = [ln_b_all[l, j][None] for j in range(3)]
        if l == 0:
            (x1, x1b, z1, g1, u1), got = _ffn_fwd(h, wg1, wu1, wd1, lg[0], lb[0], rider=_gather_ici(shards[0][3:]))
            weights_of[0] += own_quarter(_run_alone(_gather_d2d(got), "gather_d2d_rest"), shards[0][3:])
            r_attn = _gather_ici(nxt[:3]) if nxt else None
        else:
            riders = [_gather_d2d(landed)] + ([_gather_ici(nxt[:3])] if nxt else [])
            (x1, x1b, z1, g1, u1), got = _ffn_fwd(h, wg1, wu1, wd1, lg[0], lb[0], rider=_merge(*riders))
            weights_of[l] += own_quarter(got[:5], shards[l][3:])
            r_attn = _gather_d2d(got[5:]) if nxt else None
        wc, wo, wg2, wu2, wd2 = weights_of[l][3:]
        pabc, qkv = _proj(x1b, wc)
        yab = _mixab_fwd(pabc, wblk, vec)
        (yc,), got = _attn_fwd(qkv, bias, rider=r_attn)
        xm, xmb, zm = _mixout_fwd(yab, yc, x1, wo, lg[1], lb[1])
        if l == 0:
            r_ffn2 = _merge(_gather_d2d(got), _gather_ici(nxt[3:])) if nxt else None
        else:
            r_ffn2 = _gather_ici(nxt[3:]) if nxt else None
        (x3, x3b, z3, g3, u3), got2 = _ffn_fwd(xm, wg2, wu2, wd2, lg[2], lb[2], rider=r_ffn2)
        if nxt and l == 0:
            weights_of[1], landed = own_quarter(got2[:3], nxt[:3]), got2[3:]
        elif nxt:
            weights_of[l + 1], landed = own_quarter(got, nxt[:3]), got2
        saved.append(dict(wblk=wblk, vec=vec, bias=bias, lg=lg, hb=hb, z1=z1, g1=g1, u1=u1, x1b=x1b, pabc=pabc,
                          qkv=qkv, yab=yab, yc=yc, zm=zm, xmb=xmb, z3=z3, g3=g3, u3=u3))
        h, hb = x3, x3b

    loss_tile, dh = _loss_head(h, target)
    loss = lax.psum(loss_tile[0, 0], ("x", "y", "c"))

    def add_chip(arrs, recv):
        chip = [_add_chip(g, r) for g, r in zip(arrs, recv)]
        return [cf for cf, _ in chip], [cb for _, cb in chip]

    def add_final(chip_f, from_chips):
        return _join_halves([_add_final(cf, r) for cf, r in zip(chip_f, from_chips)])

    per_layer = [[None] * 6 for _ in range(L)]
    g_small = dict(pool_w=[None] * L, pool_scale=[None] * L, conv_w=[None] * L, rpb=[None] * L, ln_g=[None] * L,
                   ln_b=[None] * L)
    ffn1_g = None
    for l in reversed(range(L)):
        sv = saved[l]
        wg1, wu1, wd1, wc, wo, wg2, wu2, wd2 = weights_of[l]
        (dxm, df, dg, du, a, ln3), got = _ffn_bwd(dh, sv["z3"], sv["g3"], sv["u3"], wg2, wu2, wd2, sv["lg"][2],
                                                  rider=_swap_halves(ffn1_g) if ffn1_g else None)
        if ffn1_g:
            ffn1_f, ffn1_b = add_chip(ffn1_g, got)
        ffn2_g = [_wgrad_gate_up(sv["xmb"], dg, du)[0], _wgrad_down(a, df)]
        (dres, dzb, dycat, ln2), got = _mixout_bwd(dxm, sv["zm"], wo, sv["lg"][1], rider=_swap_halves(ffn2_g))
        ffn2_f, ffn2_b = add_chip(ffn2_g, got)
        g_o = _wgrad_out(sv["yab"], sv["yc"], dzb)
        dpabc, dwblk, dvec = _mixab_bwd(sv["pabc"], dycat, sv["wblk"], sv["vec"])
        (dq, dk, dv, dbias), got = _attn_bwd(sv["qkv"], sv["bias"], dycat,
                                             rider=_scatter_chips(ffn1_b) if ffn1_g else None)
        dparts = [dpabc, dq, dk, dv]
        g_in, got = _wgrad_in(sv["x1b"], dparts, n_in, rider=add_final(ffn1_f, got) if ffn1_g else None)
        if ffn1_g:
            per_layer[l + 1][0:2] = got
        mix_g = [g_in, g_o]
        (dx1,), got = _proj_bwd(dres, dparts, wc, rider=_swap_halves(mix_g))
        mix_f, mix_b = add_chip(mix_g, got)
        (dh, df, dg, du, a, ln1), got = _ffn_bwd(dx1, sv["z1"], sv["g1"], sv["u1"], wg1, wu1, wd1, sv["lg"][0],
                                                 rider=_scatter_chips(ffn2_b + mix_b))
        g_gu, per_layer[l][2:6] = _wgrad_gate_up(sv["hb"], dg, du, rider=add_final(ffn2_f + mix_f, got))
        ffn1_g = [g_gu, _wgrad_down(a, df)]
        g_small["pool_w"][l] = jnp.stack([dwblk[gi * pg:(gi + 1) * pg, gi * pg:(gi + 1) * pg] for gi in range(ng)])
        g_small["pool_scale"][l] = dvec[0]
        g_small["conv_w"][l] = dvec[1:4]
        g_small["rpb"][l] = _bias_grad(dbias, onehot_t)
        g_small["ln_g"][l] = jnp.stack([ln1[0], ln2[0], ln3[0]])
        g_small["ln_b"][l] = jnp.stack([ln1[1], ln2[1], ln3[1]])
    grad_x = dh[None]

    swapped = ("ffn1_w_gate", "ffn1_w_up", "ffn2_w_gate", "ffn2_w_up")

    def view(n, a):
        return jnp.swapaxes(a, 1, 2) if n in swapped else a

    def stacked(i, rows=None, swap=False):
        parts = [per_layer[l][i] if rows is None else per_layer[l][i][rows[0]:rows[1]] for l in range(L)]
        return jnp.stack([p.T for p in parts] if swap else parts)

    ffn1_f, ffn1_b = add_chip(ffn1_g, _run_alone(_swap_halves(ffn1_g), "rs_swap_halves"))
    got = _run_alone(_scatter_chips(ffn1_b), "rs_scatter_chips")
    per_layer[0][0:2] = _run_alone(add_final(ffn1_f, got), "rs_join_halves")
    grads_v = dict(ffn1_w_gate=stacked(0, (0, D), True), ffn1_w_up=stacked(0, (D, 2 * D), True),
                   ffn1_w_down=stacked(1), ffn2_w_gate=stacked(2, (0, D), True),
                   ffn2_w_up=stacked(2, (D, 2 * D), True), ffn2_w_down=stacked(3), w_in=stacked(4), w_out=stacked(5))
    grads = {n: view(n, g) for n, g in grads_v.items()}
    delta, new_m, new_v = {}, {}, {}

    small_names = ("pool_w", "pool_scale", "conv_w", "rpb", "ln_g", "ln_b")
    small_full = {n: jnp.stack(g_small[n]) for n in small_names}
    vflat = jnp.concatenate([small_full[n].reshape(-1) for n in small_names])
    n_v = vflat.shape[0]
    v_cols = 1024
    v_rows = -(-n_v // (8 * v_cols)) * 8
    vsum = _allreduce_small(jnp.pad(vflat, (0, v_rows * v_cols - n_v)).reshape(v_rows, v_cols)).reshape(-1)
    off = 0
    for n in small_names:
        sz = int(np.prod(small_full[n].shape))
        grads[n] = vsum[off:off + sz].reshape(small_full[n].shape)
        off += sz
    for n in ("conv_w", "ln_g", "ln_b"):
        width = weights[n].shape[-1]
        grads[n] = lax.dynamic_slice_in_dim(grads[n], q_me * width, width, axis=2)

    for n in order:
        res = _adamw(view(n, weights[n]), view(n, grads[n]), view(n, mom_m[n]), view(n, mom_v[n]))
        delta[n], new_m[n], new_v[n] = (view(n, o) for o in res)
    return (loss, grad_x, *[grads[n] for n in order], *[delta[n] for n in order], *[new_m[n] for n in order],
            *[new_v[n] for n in order])
```

```python
import numpy as np
import jax
import jax.numpy as jnp
from jax import lax
from jax.experimental import pallas as pl
from jax.experimental.pallas import tpu as pltpu

BF = jnp.bfloat16
F32 = jnp.float32
MESH = pl.DeviceIdType.MESH

DEPTH = 4
ALPHA = (2.0 * DEPTH) ** 0.25
LN_EPS = 1e-5
NEG_INF = -1e30
GRID_W = 64
NA_ROWS = 8
NA_COLS = 16
NA_HEADS = 8
HEAD_DIM = 64
D_POOL = 256
D_CONV = 256
D_NA = 512
HG = 4
LW = HG * HEAD_DIM
POOL_WINDOWS = (2, 4, 8, 16)
HALO = 8
ADAM_LR, ADAM_B1, ADAM_B2, ADAM_EPS, ADAM_WD, ADAM_STEP = 0.001, 0.9, 0.999, 1e-08, 0.01, 10
VMEM_LIMIT = 56 * 1024 * 1024
NQ = 4
WGRAD_TOKENS = 2048


def _cp(n_axes):
    return pltpu.CompilerParams(dimension_semantics=("arbitrary",) * n_axes, vmem_limit_bytes=VMEM_LIMIT)


def _full(shape):
    nd = len(shape)
    return pl.BlockSpec(shape, lambda *_: (0,) * nd)


def _quarters(arr):
    return pl.BlockSpec(arr.shape, lambda *_: (0, 0, 0), pipeline_mode=pl.Buffered(1))


ANY = pl.BlockSpec(memory_space=pl.ANY)


class _Rider:
    def __init__(self, tag, ins, outs, aliases, n, copies):
        self.tag, self.ins, self.outs, self.aliases, self.n, self.copies = tag, list(ins), list(outs), aliases, n, copies


def _merge(*riders):
    ins, outs, aliases, spans, n = [], [], {}, [], 0
    for r in riders:
        spans.append((len(ins), len(outs), n))
        aliases.update({len(ins) + i: len(outs) + j for i, j in r.aliases.items()})
        ins += r.ins
        outs += r.outs
        n += r.n

    def copies(r_in, r_out, ssem, rsem, base):
        cps = []
        for r, (i0, o0, s0) in zip(riders, spans):
            cps += r.copies(r_in[i0:i0 + len(r.ins)], r_out[o0:o0 + len(r.outs)], ssem, rsem, base + s0)
        return cps

    return _Rider("_".join(r.tag for r in riders), ins, outs, aliases, n, copies)


def _pcall(body, operands, *, name, grid, in_specs, out_specs, out_shape, scratch=(), rider=None, edges=None):
    n_in, n_out = len(in_specs), len(out_specs)
    params = dict(dimension_semantics=("arbitrary",) * len(grid), vmem_limit_bytes=VMEM_LIMIT)
    if rider is None:
        outs = pl.pallas_call(body, name=name, grid=grid, in_specs=in_specs, out_specs=out_specs, out_shape=out_shape,
                              scratch_shapes=list(scratch), compiler_params=pltpu.CompilerParams(**params))(*operands)
        return list(outs), []
    ni, no = len(rider.ins), len(rider.outs)
    first, last = edges

    def riding(*refs):
        rest = refs[n_in + ni + n_out + no:]
        cps = rider.copies(refs[n_in:n_in + ni], refs[n_in + ni + n_out:n_in + ni + n_out + no], rest[-2], rest[-1], 0)

        @pl.when(first())
        def _():
            for cp in cps:
                cp.start()

        body(*refs[:n_in], *refs[n_in + ni:n_in + ni + n_out], *rest[:-2])

        @pl.when(last())
        def _():
            for cp in cps:
                cp.wait()

    outs = pl.pallas_call(
        riding, name=f"{name}_{rider.tag}", grid=grid, in_specs=list(in_specs) + [ANY] * ni,
        out_specs=list(out_specs) + [ANY] * no, out_shape=list(out_shape) + rider.outs,
        scratch_shapes=list(scratch) + [pltpu.SemaphoreType.DMA((rider.n,)), pltpu.SemaphoreType.DMA((rider.n,))],
        input_output_aliases={n_in + i: n_out + j for i, j in rider.aliases.items()},
        compiler_params=pltpu.CompilerParams(has_side_effects=True, **params),
    )(*operands, *rider.ins)
    return list(outs[:n_out]), list(outs[n_out:])


def _edges_1d(n):
    return (lambda: pl.program_id(0) == 0), (lambda: pl.program_id(0) == n - 1)


def _edges_2d(n0, n1):
    return ((lambda: (pl.program_id(0) == 0) & (pl.program_id(1) == 0)),
            (lambda: (pl.program_id(0) == n0 - 1) & (pl.program_id(1) == n1 - 1)))


def _nt(a, b):
    return lax.dot_general(a, b, (((1,), (1,)), ((), ())), preferred_element_type=F32)


def _tn(a, b):
    return lax.dot_general(a, b, (((0,), (0,)), ((), ())), preferred_element_type=F32)


def _nn(a, b):
    return jnp.dot(a, b, preferred_element_type=F32)


def _ln_fwd(z, g, b):
    mu = jnp.mean(z, axis=-1, keepdims=True)
    zc = z - mu
    var = jnp.mean(zc * zc, axis=-1, keepdims=True)
    return zc * lax.rsqrt(var + LN_EPS) * g + b


def _ln_bwd(dy, z, g):
    mu = jnp.mean(z, axis=-1, keepdims=True)
    zc = z - mu
    var = jnp.mean(zc * zc, axis=-1, keepdims=True)
    rstd = lax.rsqrt(var + LN_EPS)
    xhat = zc * rstd
    gdy = dy * g
    m1 = jnp.mean(gdy, axis=-1, keepdims=True)
    m2 = jnp.mean(gdy * xhat, axis=-1, keepdims=True)
    return rstd * (gdy - m1 - xhat * m2), xhat


def _ffn_fwd(x, wg, wu, wd, lg, lb, rider=None):
    S, D = x.shape
    fq = wg.shape[-1]
    tm = min(512, S)

    def body(x_ref, wg_ref, wu_ref, wd_ref, lg_ref, lb_ref, xo_ref, xb_ref, z_ref, g_ref, u_ref):
        x = x_ref[...]
        xb = x.astype(BF)
        acc = jnp.zeros((tm, D), F32)
        for q in range(NQ):
            g = _nn(xb, wg_ref[q])
            u = _nn(xb, wu_ref[q])
            g_ref[q] = g.astype(BF)
            u_ref[q] = u.astype(BF)
            a = g * jax.nn.sigmoid(g) * u
            acc = acc + _nn(a.astype(BF), wd_ref[q])
        z = ALPHA * x + 0.5 * acc
        xo = _ln_fwd(z, lg_ref[...], lb_ref[...])
        z_ref[...] = z
        xo_ref[...] = xo
        xb_ref[...] = xo.astype(BF)

    row = pl.BlockSpec((tm, D), lambda i: (i, 0))
    qrow = pl.BlockSpec((NQ, tm, fq), lambda i: (0, i, 0))
    return _pcall(
        body, [x, wg, wu, wd, lg, lb], name="ffn_fwd", grid=(S // tm,),
        in_specs=[row, _quarters(wg), _quarters(wu), _quarters(wd), _full((1, D)), _full((1, D))],
        out_specs=[row, row, row, qrow, qrow],
        out_shape=[jax.ShapeDtypeStruct((S, D), F32), jax.ShapeDtypeStruct((S, D), BF),
                   jax.ShapeDtypeStruct((S, D), F32), jax.ShapeDtypeStruct((NQ, S, fq), BF),
                   jax.ShapeDtypeStruct((NQ, S, fq), BF)],
        rider=rider, edges=_edges_1d(S // tm))


def _ffn_bwd(dxo, z, g, u, wg, wu, wd, lg, rider=None):
    S, D = dxo.shape
    fq = wg.shape[-1]
    tm = min(256, S)
    nt = S // tm

    def body(dxo0_ref, z0_ref, dxo1_ref, z1_ref, g_ref, u_ref, wg_ref, wu_ref, wd_ref, lg_ref,
             dx_ref, df_ref, dg_ref, du_ref, a_ref, ln_ref, dz_ref):
        i = pl.program_id(0)

        @pl.when(i == 0)
        def _():
            dy0 = dxo0_ref[...]
            dz0, xhat0 = _ln_bwd(dy0, z0_ref[...], lg_ref[...])
            dz_ref[...] = dz0
            ln_ref[...] = jnp.zeros_like(ln_ref)
            ln_ref[0:1, :] += jnp.sum(dy0 * xhat0, axis=0, keepdims=True)
            ln_ref[1:2, :] += jnp.sum(dy0, axis=0, keepdims=True)

        dz = dz_ref[...]
        dfb = (0.5 * dz).astype(BF)
        df_ref[...] = dfb
        acc = ALPHA * dz
        for q in range(NQ):
            da = _nt(dfb, wd_ref[q])
            gg = g_ref[q].astype(F32)
            uu = u_ref[q].astype(F32)
            sg = jax.nn.sigmoid(gg)
            silu = gg * sg
            a_ref[q] = (silu * uu).astype(BF)
            dgb = (da * uu * (sg * (1.0 + gg * (1.0 - sg)))).astype(BF)
            dub = (da * silu).astype(BF)
            dg_ref[q] = dgb
            du_ref[q] = dub
            acc = acc + _nt(dgb, wg_ref[q]) + _nt(dub, wu_ref[q])
        dx_ref[...] = acc
        dy1 = dxo1_ref[...]
        dz1, xhat1 = _ln_bwd(dy1, z1_ref[...], lg_ref[...])
        real = (i < nt - 1).astype(F32)
        ln_ref[0:1, :] += real * jnp.sum(dy1 * xhat1, axis=0, keepdims=True)
        ln_ref[1:2, :] += real * jnp.sum(dy1, axis=0, keepdims=True)
        dz_ref[...] = dz1

    row = pl.BlockSpec((tm, D), lambda i: (i, 0))
    first = pl.BlockSpec((tm, D), lambda i: (0, 0))
    nxt = pl.BlockSpec((tm, D), lambda i: (jnp.minimum(i + 1, nt - 1), 0))
    qrow = pl.BlockSpec((NQ, tm, fq), lambda i: (0, i, 0))
    qshape = jax.ShapeDtypeStruct((NQ, S, fq), BF)
    return _pcall(
        body, [dxo, z, dxo, z, g, u, wg, wu, wd, lg], name="ffn_bwd", grid=(nt,),
        in_specs=[first, first, nxt, nxt, qrow, qrow, _quarters(wg), _quarters(wu), _quarters(wd), _full((1, D))],
        out_specs=[row, row, qrow, qrow, qrow, _full((8, D))],
        out_shape=[jax.ShapeDtypeStruct((S, D), F32), jax.ShapeDtypeStruct((S, D), BF), qshape, qshape, qshape,
                   jax.ShapeDtypeStruct((8, D), F32)],
        scratch=[pltpu.VMEM((tm, D), F32)],
        rider=rider, edges=_edges_1d(nt))


def _wgrad_gate_up(a, dg, du, rider=None):
    S, K = a.shape
    n = dg.shape[-1]
    ts = min(WGRAD_TOKENS, S)

    def body(a_ref, g_ref, u_ref, o_ref):
        @pl.when(pl.program_id(1) == 0)
        def _():
            o_ref[...] = jnp.zeros_like(o_ref)
        av = a_ref[...]
        o_ref[0:K, :] += _tn(av, g_ref[...])
        o_ref[K:2 * K, :] += _tn(av, u_ref[...])

    bspec = pl.BlockSpec((None, ts, n), lambda q, s: (q, s, 0))
    (out,), got = _pcall(
        body, [a, dg, du], name="wgrad_gate_up", grid=(NQ, S // ts),
        in_specs=[pl.BlockSpec((ts, K), lambda q, s: (s, 0)), bspec, bspec],
        out_specs=[pl.BlockSpec((None, 2 * K, n), lambda q, s: (q, 0, 0))],
        out_shape=[jax.ShapeDtypeStruct((NQ, 2 * K, n), F32)],
        rider=rider, edges=_edges_2d(NQ, S // ts))
    return out, got


def _wgrad_down(a, df):
    _, S, k = a.shape
    N = df.shape[1]
    ts = min(WGRAD_TOKENS, S)

    def body(a_ref, b_ref, o_ref):
        @pl.when(pl.program_id(1) == 0)
        def _():
            o_ref[...] = jnp.zeros_like(o_ref)
        o_ref[...] += _tn(a_ref[...], b_ref[...])

    return pl.pallas_call(
        body, name="wgrad_down", grid=(NQ, S // ts),
        in_specs=[pl.BlockSpec((None, ts, k), lambda q, s: (q, s, 0)), pl.BlockSpec((ts, N), lambda q, s: (s, 0))],
        out_specs=pl.BlockSpec((None, k, N), lambda q, s: (q, 0, 0)),
        out_shape=jax.ShapeDtypeStruct((NQ, k, N), F32),
        compiler_params=_cp(2),
    )(a, df)


def _wgrad_out(yab, yc, dzb):
    S, h = yab.shape
    D = dzb.shape[1]
    k = h // 2
    ts = min(WGRAD_TOKENS, S)

    def body(yab_ref, yc_ref, b_ref, o_ref):
        @pl.when(pl.program_id(0) == 0)
        def _():
            o_ref[...] = jnp.zeros_like(o_ref)
        b = b_ref[...]
        o_ref[0] += _tn(yab_ref[:, 0:k], b)
        o_ref[1] += _tn(yab_ref[:, k:h], b)
        o_ref[2] += _tn(yc_ref[:, 0:k], b)
        o_ref[3] += _tn(yc_ref[:, k:h], b)

    row = lambda w: pl.BlockSpec((ts, w), lambda s: (s, 0))
    return pl.pallas_call(
        body, name="wgrad_out", grid=(S // ts,),
        in_specs=[row(h), row(h), row(D)], out_specs=_full((NQ, k, D)),
        out_shape=jax.ShapeDtypeStruct((NQ, k, D), F32),
        compiler_params=_cp(1),
    )(yab, yc, dzb)


def _proj(xb, wc):
    S, D = xb.shape
    n = wc.shape[-1]
    n1 = D_POOL + 3 * D_CONV
    n2 = NQ * n - n1
    tm = min(1024, S)

    def body(x_ref, w_ref, p_ref, qkv_ref):
        x = x_ref[...]
        for q in range(NQ):
            r = _nn(x, w_ref[q])
            lo, hi = q * n, (q + 1) * n
            if hi <= n1:
                p_ref[:, lo:hi] = r
            elif lo >= n1:
                qkv_ref[:, lo - n1:hi - n1] = r.astype(BF)
            else:
                p_ref[:, lo:n1] = r[:, 0:n1 - lo]
                qkv_ref[:, 0:hi - n1] = r[:, n1 - lo:n].astype(BF)

    row = lambda w: pl.BlockSpec((tm, w), lambda i: (i, 0))
    return pl.pallas_call(
        body, name="mix_proj", grid=(S // tm,),
        in_specs=[row(D), _quarters(wc)],
        out_specs=[row(n1), row(n2)],
        out_shape=[jax.ShapeDtypeStruct((S, n1), F32), jax.ShapeDtypeStruct((S, n2), BF)],
        compiler_params=_cp(1),
    )(xb, wc)


def _mm_exact(a, b, name):
    def body(a_ref, b_ref, o_ref):
        o_ref[...] = jnp.dot(a_ref[...], b_ref[...].astype(F32), preferred_element_type=F32,
                             precision=lax.Precision.HIGHEST)

    return pl.pallas_call(
        body, name=name, in_specs=[_full(a.shape), _full(b.shape)], out_specs=_full((a.shape[0], b.shape[1])),
        out_shape=jax.ShapeDtypeStruct((a.shape[0], b.shape[1]), F32),
        compiler_params=pltpu.CompilerParams(vmem_limit_bytes=VMEM_LIMIT),
    )(a, b)


NB_ROWS = 2 * NA_ROWS - 1
NB_COLS = 2 * NA_COLS


def _bias_constants():
    c = np.arange(GRID_W)
    col_start = np.clip(c - NA_COLS // 2, 0, GRID_W - NA_COLS)
    valid = (c[None, :] >= col_start[:, None]) & (c[None, :] < col_start[:, None] + NA_COLS)
    dc = np.clip(c[None, :] - c[:, None], -(NA_COLS - 1), NA_COLS - 1) + (NA_COLS - 1)
    cq, ck = np.meshgrid(c, c, indexing="ij")
    onehot = np.zeros((HG, NB_COLS, GRID_W, HG, GRID_W), np.float32)
    for h in range(HG):
        onehot[h, dc[cq, ck], ck, h, cq] = 1.0
    mask_kq = np.where(valid.T, 0.0, NEG_INF).astype(np.float32)
    mask = np.tile(mask_kq, (NB_ROWS, HG))
    return onehot.reshape(HG * NB_COLS, GRID_W * LW), mask


def _bias_table(rpb, onehot, mask):
    ngr = NA_HEADS // HG
    r = rpb.reshape(ngr, HG, NB_ROWS, NB_COLS - 1).transpose(0, 2, 1, 3)
    r = jnp.pad(r, ((0, 0), (0, 0), (0, 0), (0, 1))).reshape(ngr * NB_ROWS, HG * NB_COLS)
    t = _mm_exact(r, onehot, "bias_expand")
    return t.reshape(ngr, NB_ROWS * GRID_W, LW) + mask[None]


def _bias_grad(dt, onehot_t):
    ngr = NA_HEADS // HG
    g = _mm_exact(dt.reshape(ngr * NB_ROWS, GRID_W * LW), onehot_t, "bias_reduce")
    g = g.reshape(ngr, NB_ROWS, HG, NB_COLS)[..., :NB_COLS - 1]
    return g.transpose(0, 2, 1, 3).reshape(NA_HEADS, NB_ROWS, NB_COLS - 1)


def _attn_rows(S):
    rows = S // GRID_W
    rb = min(32, rows)
    return rows, rb


def _head_masks():
    lane = lax.broadcasted_iota(jnp.int32, (GRID_W, LW), 1)
    return [(lane >= HEAD_DIM * h) & (lane < HEAD_DIM * (h + 1)) for h in range(HG)]


def _stack_heads(x, masks):
    zero = jnp.zeros_like(x)
    return jnp.concatenate([jnp.where(m, x, zero) for m in masks], axis=0)


def _unstack_heads(x2, masks):
    out = x2[0:GRID_W]
    for h in range(1, HG):
        out = jnp.where(masks[h], x2[h * GRID_W:(h + 1) * GRID_W], out)
    return out


def _attn_step(r, rows, q, k_ref, v_ref, b_ref, masks):
    rs = jnp.clip(r - NA_ROWS // 2, 0, rows - NA_ROWS)
    s0 = rs - r + (NA_ROWS - 1)
    q2 = _stack_heads(q, masks)
    ks = pl.ds(pl.multiple_of(rs * GRID_W, GRID_W), NA_ROWS * GRID_W)
    kb = k_ref[ks, :]
    vb = v_ref[ks, :]
    bs = pl.ds(pl.multiple_of(s0 * GRID_W, GRID_W), NA_ROWS * GRID_W)
    s = _nt(kb, q2) * (HEAD_DIM ** -0.5) + b_ref[0, bs, :]
    m = jnp.max(s, axis=0, keepdims=True)
    p = jnp.exp(s - m)
    p = p / jnp.sum(p, axis=0, keepdims=True)
    return p, q2, kb, vb, ks, bs


def _attn_fwd(qkv, bias, rider=None):
    S = qkv.shape[0]
    rows, rb = _attn_rows(S)
    tq = rb * GRID_W
    ngr = NA_HEADS // HG

    def body(q_ref, k_ref, v_ref, b_ref, o_ref):
        base = pl.program_id(1) * rb
        masks = _head_masks()

        def step(i, carry):
            qs = pl.ds(pl.multiple_of(i * GRID_W, GRID_W), GRID_W)
            p, _, _, vb, _, _ = _attn_step(base + i, rows, q_ref[qs, :], k_ref, v_ref, b_ref, masks)
            o_ref[qs, :] = _unstack_heads(_tn(p.astype(BF), vb), masks).astype(BF)
            return carry

        lax.fori_loop(0, rb, step, 0, unroll=rb)

    return _pcall(
        body, [qkv, qkv, qkv, bias], name="attn_fwd", grid=(ngr, rows // rb),
        in_specs=[pl.BlockSpec((tq, LW), lambda h, r: (r, h)),
                  pl.BlockSpec((S, LW), lambda h, r: (0, ngr + h)),
                  pl.BlockSpec((S, LW), lambda h, r: (0, 2 * ngr + h)),
                  pl.BlockSpec((1, bias.shape[1], LW), lambda h, r: (h, 0, 0))],
        out_specs=[pl.BlockSpec((tq, LW), lambda h, r: (r, h))],
        out_shape=[jax.ShapeDtypeStruct((S, D_NA), BF)],
        rider=rider, edges=_edges_2d(ngr, rows // rb))


def _attn_bwd(qkv, bias, dycat, rider=None):
    S = qkv.shape[0]
    rows, rb = _attn_rows(S)
    tq = rb * GRID_W
    ngr = NA_HEADS // HG
    scale = HEAD_DIM ** -0.5

    def body(q_ref, k_ref, v_ref, b_ref, do_ref, dq_ref, dk_ref, dv_ref, db_ref, dka_ref, dva_ref):
        base = pl.program_id(1) * rb
        last = pl.program_id(1) == pl.num_programs(1) - 1
        masks = _head_masks()

        @pl.when(pl.program_id(1) == 0)
        def _():
            dka_ref[...] = jnp.zeros_like(dka_ref)
            dva_ref[...] = jnp.zeros_like(dva_ref)
            db_ref[...] = jnp.zeros_like(db_ref)

        def step(i, carry):
            qs = pl.ds(pl.multiple_of(i * GRID_W, GRID_W), GRID_W)
            p, q2, kb, vb, ks, bs = _attn_step(base + i, rows, q_ref[qs, :], k_ref, v_ref, b_ref, masks)
            do2 = _stack_heads(do_ref[qs, :].astype(BF), masks)
            dp = _nt(vb, do2)
            ds = p * (dp - jnp.sum(p * dp, axis=0, keepdims=True))
            db_ref[0, bs, :] += ds
            dsb = ds.astype(BF)
            dq_ref[qs, :] = _unstack_heads(_tn(dsb, kb) * scale, masks).astype(BF)
            dka_ref[ks, :] += _nn(dsb, q2) * scale
            dva_ref[ks, :] += _nn(p.astype(BF), do2)
            return carry

        lax.fori_loop(0, rb, step, 0, unroll=rb)

        @pl.when(last)
        def _():
            dk_ref[...] = dka_ref[...].astype(BF)
            dv_ref[...] = dva_ref[...].astype(BF)

    nb = bias.shape[1]
    once = dict(pipeline_mode=pl.Buffered(1))
    nd = D_NA // LW
    return _pcall(
        body, [qkv, qkv, qkv, bias, dycat], name="attn_bwd", grid=(ngr, rows // rb),
        in_specs=[pl.BlockSpec((tq, LW), lambda h, r: (r, h)),
                  pl.BlockSpec((S, LW), lambda h, r: (0, ngr + h), **once),
                  pl.BlockSpec((S, LW), lambda h, r: (0, 2 * ngr + h), **once),
                  pl.BlockSpec((1, nb, LW), lambda h, r: (h, 0, 0)),
                  pl.BlockSpec((tq, LW), lambda h, r: (r, nd + h))],
        out_specs=[pl.BlockSpec((tq, LW), lambda h, r: (r, h)),
                   pl.BlockSpec((S, LW), lambda h, r: (0, h)),
                   pl.BlockSpec((S, LW), lambda h, r: (0, h)),
                   pl.BlockSpec((1, nb, LW), lambda h, r: (h, 0, 0))],
        out_shape=[jax.ShapeDtypeStruct((S, D_NA), BF)] * 3 + [jax.ShapeDtypeStruct((ngr, nb, LW), F32)],
        scratch=[pltpu.VMEM((S, LW), F32), pltpu.VMEM((S, LW), F32)],
        rider=rider, edges=_edges_2d(ngr, rows // rb))


def _halo_specs(tm, width, S):
    hb = tm // HALO
    last = S // HALO - 1
    return [pl.BlockSpec((tm, width), lambda i: (i, 0)),
            pl.BlockSpec((HALO, width), lambda i: (jnp.maximum(i * hb - 1, 0), 0)),
            pl.BlockSpec((HALO, width), lambda i: (jnp.minimum((i + 1) * hb, last), 0))]


def _with_halo(cur_ref, prev_ref, next_ref, i, nt):
    prev = jnp.where(i > 0, prev_ref[...], 0.0)
    nxt = jnp.where(i < nt - 1, next_ref[...], 0.0)
    return jnp.concatenate([prev, cur_ref[...], nxt], axis=0)


def _shift(a, k):
    n = a.shape[0]
    return pltpu.roll(a, k % n, 0)


def _pool_lanes(n):
    lane = lax.broadcasted_iota(jnp.int32, (n, D_POOL), 1)
    group = D_POOL // len(POOL_WINDOWS)
    return [lane < group * (j + 1) for j in range(len(POOL_WINDOWS) - 1)]


def _by_window(lanes, vals):
    return jnp.where(lanes[0], vals[0], jnp.where(lanes[1], vals[1], jnp.where(lanes[2], vals[2], vals[3])))


def _pool_count(lanes, t, S):
    back = _by_window(lanes, tuple(w // 2 for w in POOL_WINDOWS))
    lo = jnp.maximum(t - back, 0)
    hi = jnp.minimum(t + back, S)
    return jnp.maximum(hi - lo, 1).astype(F32)


def _pool_p(u, lanes, cnt):
    a = u + _shift(u, 1)
    b = _shift(a, 1) + _shift(a, -1)
    c = _shift(b, 2) + _shift(b, -2)
    d = _shift(c, 4) + _shift(c, -4)
    return _by_window(lanes, (a, b, c, d)) / cnt - u


def _mixab_fwd(pabc, wblk, vec):
    S = pabc.shape[0]
    tm = min(512, S)
    nt = S // tm
    n = tm + 2 * HALO
    tile = slice(HALO, HALO + tm)

    def body(cur_ref, prev_ref, next_ref, w_ref, vec_ref, o_ref):
        i = pl.program_id(0)
        ext = _with_halo(cur_ref, prev_ref, next_ref, i, nt)
        lanes = _pool_lanes(n)
        t = i * tm - HALO + lax.broadcasted_iota(jnp.int32, (n, D_POOL), 0)
        p = _pool_p(ext[:, 0:D_POOL], lanes, _pool_count(lanes, t, S))[tile]
        o_ref[:, 0:D_POOL] = (_nn(p.astype(BF), w_ref[...]) * vec_ref[0:1, :]).astype(BF)
        zc = ext[:, 512:768] * ext[:, 768:1024]
        conv = vec_ref[1:2, :] * _shift(zc, 1) + vec_ref[2:3, :] * zc + vec_ref[3:4, :] * _shift(zc, -1)
        o_ref[:, D_POOL:D_POOL + D_CONV] = (ext[tile, 256:512] * conv[tile]).astype(BF)

    return pl.pallas_call(
        body, name="mixab_fwd", grid=(nt,),
        in_specs=_halo_specs(tm, 1024, S) + [_full((D_POOL, D_POOL)), _full((8, D_POOL))],
        out_specs=pl.BlockSpec((tm, D_POOL + D_CONV), lambda i: (i, 0)),
        out_shape=jax.ShapeDtypeStruct((S, D_POOL + D_CONV), BF),
        compiler_params=_cp(1),
    )(pabc, pabc, pabc, wblk, vec)


def _mixab_bwd(pabc, dycat, wblk, vec):
    S = pabc.shape[0]
    tm = min(512, S)
    nt = S // tm
    n = tm + 2 * HALO
    tile = slice(HALO, HALO + tm)

    def body(cur_ref, prev_ref, next_ref, dcur_ref, dprev_ref, dnext_ref, w_ref, vec_ref, o_ref, dw_ref, dvec_ref):
        i = pl.program_id(0)

        @pl.when(i == 0)
        def _():
            dw_ref[...] = jnp.zeros_like(dw_ref)
            dvec_ref[...] = jnp.zeros_like(dvec_ref)

        ext = _with_halo(cur_ref, prev_ref, next_ref, i, nt)
        dext = _with_halo(dcur_ref, dprev_ref, dnext_ref, i, nt)
        lanes = _pool_lanes(n)
        t = i * tm - HALO + lax.broadcasted_iota(jnp.int32, (n, D_POOL), 0)
        cnt = _pool_count(lanes, t, S)
        w = w_ref[...]
        scale = vec_ref[0:1, :]
        pb = _pool_p(ext[:, 0:D_POOL], lanes, cnt)[tile].astype(BF)
        dya = dext[:, 0:D_POOL]
        dvec_ref[0:1, :] += jnp.sum(dya[tile] * _nn(pb, w), axis=0, keepdims=True)
        dqb = (dya * scale).astype(BF)
        dw_ref[...] += _tn(pb, dqb[tile])
        dp = _nt(dqb, w)
        r = dp / cnt
        a = r + _shift(r, -1)
        b = _shift(a, 1) + _shift(a, -1)
        c = _shift(b, 2) + _shift(b, -2)
        d = _shift(c, 4) + _shift(c, -4)
        o_ref[:, 0:256] = (_by_window(lanes, (a, b, c, d)) - dp)[tile].astype(BF)
        gb, gc, hh = ext[:, 256:512], ext[:, 512:768], ext[:, 768:1024]
        zc = gc * hh
        zm, zp = _shift(zc, 1), _shift(zc, -1)
        w0, w1, w2 = vec_ref[1:2, :], vec_ref[2:3, :], vec_ref[3:4, :]
        dyb = dext[:, D_POOL:D_POOL + D_CONV]
        dconv = dyb * gb
        o_ref[:, 256:512] = (dyb * (w0 * zm + w1 * zc + w2 * zp))[tile].astype(BF)
        dzc = w0 * _shift(dconv, -1) + w1 * dconv + w2 * _shift(dconv, 1)
        o_ref[:, 512:768] = (dzc * hh)[tile].astype(BF)
        o_ref[:, 768:1024] = (dzc * gc)[tile].astype(BF)
        dct = dconv[tile]
        dvec_ref[1:2, :] += jnp.sum(dct * zm[tile], axis=0, keepdims=True)
        dvec_ref[2:3, :] += jnp.sum(dct * zc[tile], axis=0, keepdims=True)
        dvec_ref[3:4, :] += jnp.sum(dct * zp[tile], axis=0, keepdims=True)

    return pl.pallas_call(
        body, name="mixab_bwd", grid=(nt,),
        in_specs=_halo_specs(tm, 1024, S) + _halo_specs(tm, 512, S) + [_full((D_POOL, D_POOL)), _full((8, D_POOL))],
        out_specs=[pl.BlockSpec((tm, 1024), lambda i: (i, 0)), _full((D_POOL, D_POOL)), _full((8, D_POOL))],
        out_shape=[jax.ShapeDtypeStruct((S, 1024), BF), jax.ShapeDtypeStruct((D_POOL, D_POOL), F32),
                   jax.ShapeDtypeStruct((8, D_POOL), F32)],
        compiler_params=_cp(1),
    )(pabc, pabc, pabc, dycat, dycat, dycat, wblk, vec)


def _mixout_fwd(yab, yc, x, wo, lg, lb):
    S, D = x.shape
    tm = min(512, S)
    h = yab.shape[1]
    k = h // 2

    def body(yab_ref, yc_ref, x_ref, w_ref, lg_ref, lb_ref, xo_ref, xb_ref, z_ref):
        y = (_nn(yab_ref[:, 0:k], w_ref[0]) + _nn(yab_ref[:, k:h], w_ref[1])
             + _nn(yc_ref[:, 0:k], w_ref[2]) + _nn(yc_ref[:, k:h], w_ref[3]))
        z = ALPHA * x_ref[...] + y
        xo = _ln_fwd(z, lg_ref[...], lb_ref[...])
        z_ref[...] = z
        xo_ref[...] = xo
        xb_ref[...] = xo.astype(BF)

    row = lambda w: pl.BlockSpec((tm, w), lambda i: (i, 0))
    return pl.pallas_call(
        body, name="mixout_fwd", grid=(S // tm,),
        in_specs=[row(h), row(h), row(D), _quarters(wo), _full((1, D)), _full((1, D))],
        out_specs=[row(D), row(D), row(D)],
        out_shape=[jax.ShapeDtypeStruct((S, D), F32), jax.ShapeDtypeStruct((S, D), BF),
                   jax.ShapeDtypeStruct((S, D), F32)],
        compiler_params=_cp(1),
    )(yab, yc, x, wo, lg, lb)


def _mixout_bwd(dxo, z, wo, lg, rider=None):
    S, D = dxo.shape
    k = wo.shape[-2]
    tm = min(512, S)
    nt = S // tm

    def body(dxo0_ref, z0_ref, dxo1_ref, z1_ref, w_ref, lg_ref, dres_ref, dzb_ref, dy_ref, ln_ref, dz_ref):
        i = pl.program_id(0)

        @pl.when(i == 0)
        def _():
            dy0 = dxo0_ref[...]
            dz0, xhat0 = _ln_bwd(dy0, z0_ref[...], lg_ref[...])
            dz_ref[...] = dz0
            ln_ref[...] = jnp.zeros_like(ln_ref)
            ln_ref[0:1, :] += jnp.sum(dy0 * xhat0, axis=0, keepdims=True)
            ln_ref[1:2, :] += jnp.sum(dy0, axis=0, keepdims=True)

        dz = dz_ref[...]
        dzb = dz.astype(BF)
        dres_ref[...] = ALPHA * dz
        dzb_ref[...] = dzb
        for q in range(NQ):
            dy_ref[:, q * k:(q + 1) * k] = _nt(dzb, w_ref[q])
        dy1 = dxo1_ref[...]
        dz1, xhat1 = _ln_bwd(dy1, z1_ref[...], lg_ref[...])
        real = (i < nt - 1).astype(F32)
        ln_ref[0:1, :] += real * jnp.sum(dy1 * xhat1, axis=0, keepdims=True)
        ln_ref[1:2, :] += real * jnp.sum(dy1, axis=0, keepdims=True)
        dz_ref[...] = dz1

    row = lambda w: pl.BlockSpec((tm, w), lambda i: (i, 0))
    first = pl.BlockSpec((tm, D), lambda i: (0, 0))
    nxt = pl.BlockSpec((tm, D), lambda i: (jnp.minimum(i + 1, nt - 1), 0))
    return _pcall(
        body, [dxo, z, dxo, z, wo, lg], name="mixout_bwd", grid=(nt,),
        in_specs=[first, first, nxt, nxt, _quarters(wo), _full((1, D))],
        out_specs=[row(D), row(D), row(NQ * k), _full((8, D))],
        out_shape=[jax.ShapeDtypeStruct((S, D), F32), jax.ShapeDtypeStruct((S, D), BF),
                   jax.ShapeDtypeStruct((S, NQ * k), F32), jax.ShapeDtypeStruct((8, D), F32)],
        scratch=[pltpu.VMEM((tm, D), F32)],
        rider=rider, edges=_edges_1d(nt))


def _take_cols(refs, lo, hi):
    parts, off = [], 0
    for r in refs:
        w = r.shape[1]
        a, b = max(lo, off), min(hi, off + w)
        if a < b:
            parts.append(r[:, a - off:b - off])
        off += w
    return parts[0] if len(parts) == 1 else jnp.concatenate(parts, axis=1)


def _proj_bwd(dres, dparts, wc, rider=None):
    S, D = dres.shape
    n = wc.shape[-1]
    tm = min(1024, S)
    np_ = len(dparts)

    def body(*refs):
        dres_ref, d_refs, w_ref, dx_ref = refs[0], refs[1:1 + np_], refs[1 + np_], refs[2 + np_]
        acc = dres_ref[...]
        for q in range(NQ):
            acc = acc + _nt(_take_cols(d_refs, q * n, (q + 1) * n), w_ref[q])
        dx_ref[...] = acc

    row = lambda w: pl.BlockSpec((tm, w), lambda i: (i, 0))
    return _pcall(
        body, [dres, *dparts, wc], name="mix_proj_bwd", grid=(S // tm,),
        in_specs=[row(D)] + [row(d.shape[1]) for d in dparts] + [_quarters(wc)],
        out_specs=[row(D)],
        out_shape=[jax.ShapeDtypeStruct((S, D), F32)],
        rider=rider, edges=_edges_1d(S // tm))


def _wgrad_in(a, dparts, n, rider=None):
    S, K = a.shape
    ts = min(WGRAD_TOKENS // 2, S)
    np_ = len(dparts)

    def body(*refs):
        a_ref, d_refs, o_ref = refs[0], refs[1:1 + np_], refs[1 + np_]

        @pl.when(pl.program_id(0) == 0)
        def _():
            o_ref[...] = jnp.zeros_like(o_ref)
        av = a_ref[...]
        for q in range(NQ):
            o_ref[q] += _tn(av, _take_cols(d_refs, q * n, (q + 1) * n))

    row = lambda w: pl.BlockSpec((ts, w), lambda s: (s, 0))
    (out,), got = _pcall(
        body, [a, *dparts], name="wgrad_in", grid=(S // ts,),
        in_specs=[row(K)] + [row(d.shape[1]) for d in dparts], out_specs=[_full((NQ, K, n))],
        out_shape=[jax.ShapeDtypeStruct((NQ, K, n), F32)],
        rider=rider, edges=_edges_1d(S // ts))
    return out, got


def _loss_head(y, target):
    S, D = y.shape
    tm = min(512, S)

    def body(y_ref, t_ref, l_ref, dy_ref):
        @pl.when(pl.program_id(0) == 0)
        def _():
            l_ref[...] = jnp.zeros_like(l_ref)
        e = y_ref[...] - t_ref[...]
        dy_ref[...] = e * (1.0 / D)
        part = jnp.sum(jnp.sum(e * e, axis=1, keepdims=True) * (1.0 / D), axis=0, keepdims=True)
        l_ref[...] += 0.5 * part

    row = pl.BlockSpec((tm, D), lambda i: (i, 0))
    return pl.pallas_call(
        body, name="loss_head", grid=(S // tm,),
        in_specs=[row, row], out_specs=[_full((8, 128)), row],
        out_shape=[jax.ShapeDtypeStruct((8, 128), F32), jax.ShapeDtypeStruct((S, D), F32)],
        compiler_params=_cp(1),
    )(y, target)


def _adamw_update(w, g, m, v):
    mn = ADAM_B1 * m + (1.0 - ADAM_B1) * g
    vn = ADAM_B2 * v + (1.0 - ADAM_B2) * (g * g)
    m_hat = mn / (1.0 - ADAM_B1 ** ADAM_STEP)
    v_hat = vn / (1.0 - ADAM_B2 ** ADAM_STEP)
    return -ADAM_LR * (m_hat / (jnp.sqrt(v_hat) + ADAM_EPS) + ADAM_WD * w), mn, vn


def _adamw(w, g, m, v):
    shape = w.shape
    cols = shape[-1]
    rows = int(np.prod(shape[:-1]))
    w2, g2, m2, v2 = (a.reshape(rows, cols) for a in (w, g, m, v))
    tr = rows
    for cand in (512, 352, 256):
        if rows > cand and rows % cand == 0:
            tr = cand
            break

    def body(w_ref, g_ref, m_ref, v_ref, d_ref, mo_ref, vo_ref):
        d_ref[...], mo_ref[...], vo_ref[...] = _adamw_update(w_ref[...], g_ref[...], m_ref[...], v_ref[...])

    spec = pl.BlockSpec((tr, cols), lambda i: (i, 0))
    outs = pl.pallas_call(
        body, name=f"adamw_{rows}x{cols}", grid=(rows // tr,),
        in_specs=[spec] * 4, out_specs=[spec] * 3,
        out_shape=[jax.ShapeDtypeStruct((rows, cols), F32)] * 3,
        compiler_params=_cp(1),
    )(w2, g2, m2, v2)
    return tuple(o.reshape(shape) for o in outs)


def _half_tile(h):
    return h if h <= 512 else 512


def _add_chip(g, recv):
    _, R, C = g.shape
    h = R // 2
    tr = _half_tile(h)
    nb = h // tr

    def body(a_ref, b_ref, o_ref, ob_ref):
        s = a_ref[...] + b_ref[...]
        ob_ref[...] = s.astype(BF)

        @pl.when(pl.program_id(1) == 2 * lax.axis_index("x") + lax.axis_index("y"))
        def _():
            o_ref[...] = s[0]

    half = pl.BlockSpec((1, tr, C), lambda i, q: (q, i, 0))
    mine = pl.BlockSpec((1, tr, C), lambda i, q: (q, lax.axis_index("c") * nb + i, 0))
    return pl.pallas_call(
        body, name=f"rs_add_chip_{R}x{C}", grid=(nb, NQ), in_specs=[mine, half],
        out_specs=[pl.BlockSpec((tr, C), lambda i, q: (i, 0)), half],
        out_shape=[jax.ShapeDtypeStruct((h, C), F32), jax.ShapeDtypeStruct((NQ, h, C), BF)],
        compiler_params=_cp(2),
    )(g, recv)


def _add_final(chip, recv):
    h, C = chip.shape
    tr = _half_tile(h)
    nb = h // tr

    def body(a_ref, b_ref, o_ref):
        s = a_ref[...]
        for j in range(3):
            s = s + b_ref[j].astype(F32)
        o_ref[...] = s

    return pl.pallas_call(
        body, name=f"rs_add_final_{h}x{C}", grid=(nb,),
        in_specs=[pl.BlockSpec((tr, C), lambda i: (i, 0)), pl.BlockSpec((3, tr, C), lambda i: (0, i, 0))],
        out_specs=pl.BlockSpec((tr, C), lambda i: (lax.axis_index("c") * nb + i, 0)),
        out_shape=jax.ShapeDtypeStruct((2 * h, C), F32),
        compiler_params=_cp(1),
    )(chip, recv)


COMM = pltpu.CompilerParams(has_side_effects=True)


def _place():
    x, y, c = lax.axis_index("x"), lax.axis_index("y"), lax.axis_index("c")
    chips = [(1 - x, y), (x, 1 - y), (1 - x, 1 - y)]
    return x, y, c, chips


def _half0(ref, c):
    n = ref.shape[0] // 2
    return ref.at[pl.ds(c * n, n)]


def _gather_ici(shards):
    n = len(shards)

    def copies(r_in, r_out, ssem, rsem, base):
        x, y, c, chips = _place()
        q = 2 * x + y
        return [pltpu.make_async_remote_copy(
            src_ref=_half0(r_in[i], c), dst_ref=_half0(r_out[i].at[q], c), send_sem=ssem.at[base + 3 * i + j],
            recv_sem=rsem.at[base + 3 * i + j], device_id=(*chip, c), device_id_type=MESH)
            for i in range(n) for j, chip in enumerate(chips)]

    return _Rider("ici", shards, [jax.ShapeDtypeStruct((NQ,) + s.shape, BF) for s in shards], {}, 3 * n, copies)


def _gather_d2d(bufs):
    n = len(bufs)

    def copies(r_in, r_out, ssem, rsem, base):
        x, y, c, chips = _place()
        return [pltpu.make_async_remote_copy(
            src_ref=_half0(r_in[i].at[2 * cx + cy], c), dst_ref=_half0(r_out[i].at[2 * cx + cy], c),
            send_sem=ssem.at[base + 3 * i + j], recv_sem=rsem.at[base + 3 * i + j], device_id=(x, y, 1 - c),
            device_id_type=MESH) for i in range(n) for j, (cx, cy) in enumerate(chips)]

    return _Rider("d2d", bufs, [jax.ShapeDtypeStruct(b.shape, b.dtype) for b in bufs], {i: i for i in range(n)},
                  3 * n, copies)


def _gather_small(small):
    sr = small.shape[0]

    def body(s_ref, o_ref, send_sems, recv_sems):
        x, y, c, chips = _place()
        o_ref[2 * x + y] = s_ref[...]
        cps = [pltpu.make_async_remote_copy(
            src_ref=s_ref, dst_ref=o_ref.at[2 * x + y], send_sem=send_sems.at[j], recv_sem=recv_sems.at[j],
            device_id=(*chip, c), device_id_type=MESH) for j, chip in enumerate(chips)]
        for cp in cps:
            cp.start()
        for j, (cx, cy) in enumerate(chips):
            pltpu.make_async_remote_copy(
                src_ref=s_ref, dst_ref=o_ref.at[2 * cx + cy], send_sem=send_sems.at[j], recv_sem=recv_sems.at[j],
                device_id=(cx, cy, c), device_id_type=MESH).wait_recv()
        for cp in cps:
            cp.wait_send()

    vm = pl.BlockSpec(memory_space=pltpu.VMEM)
    return pl.pallas_call(
        body, name="gather_small", in_specs=[vm], out_specs=vm,
        out_shape=jax.ShapeDtypeStruct((NQ, sr, 128), F32),
        scratch_shapes=[pltpu.SemaphoreType.DMA((3,)), pltpu.SemaphoreType.DMA((3,))],
        compiler_params=COMM,
    )(small)


def _swap_halves(gs):
    n = len(gs)

    def copies(r_in, r_out, ssem, rsem, base):
        x, y, c, _ = _place()
        cps = []
        for i in range(n):
            h = r_in[i].shape[1] // 2
            cps.append(pltpu.make_async_remote_copy(
                src_ref=r_in[i].at[:, pl.ds((1 - c) * h, h), :], dst_ref=r_out[i], send_sem=ssem.at[base + i],
                recv_sem=rsem.at[base + i], device_id=(x, y, 1 - c), device_id_type=MESH))
        return cps

    return _Rider("swap", gs, [jax.ShapeDtypeStruct((NQ, g.shape[1] // 2, g.shape[2]), F32) for g in gs], {}, n,
                  copies)


def _scatter_chips(chips_b):
    n = len(chips_b)

    def copies(r_in, r_out, ssem, rsem, base):
        x, y, c, chips = _place()
        return [pltpu.make_async_remote_copy(
            src_ref=r_in[i].at[2 * cx + cy], dst_ref=r_out[i].at[j], send_sem=ssem.at[base + 3 * i + j],
            recv_sem=rsem.at[base + 3 * i + j], device_id=(cx, cy, c), device_id_type=MESH)
            for i in range(n) for j, (cx, cy) in enumerate(chips)]

    return _Rider("scatter", chips_b, [jax.ShapeDtypeStruct((3,) + s.shape[1:], BF) for s in chips_b], {}, 3 * n,
                  copies)


def _run_alone(rider, name):
    ni, no = len(rider.ins), len(rider.outs)

    def body(*refs):
        cps = rider.copies(refs[:ni], refs[ni:ni + no], refs[ni + no], refs[ni + no + 1], 0)
        for cp in cps:
            cp.start()
        for cp in cps:
            cp.wait()

    return list(pl.pallas_call(
        body, name=name, in_specs=[ANY] * ni, out_specs=[ANY] * no, out_shape=rider.outs,
        input_output_aliases=dict(rider.aliases),
        scratch_shapes=[pltpu.SemaphoreType.DMA((rider.n,)), pltpu.SemaphoreType.DMA((rider.n,))],
        compiler_params=COMM,
    )(*rider.ins))


def _join_halves(fs):
    n = len(fs)

    def copies(r_in, r_out, ssem, rsem, base):
        x, y, c, _ = _place()
        return [pltpu.make_async_remote_copy(
            src_ref=_half0(r_in[i], c), dst_ref=_half0(r_out[i], c), send_sem=ssem.at[base + i],
            recv_sem=rsem.at[base + i], device_id=(x, y, 1 - c), device_id_type=MESH) for i in range(n)]

    return _Rider("join", fs, [jax.ShapeDtypeStruct(f.shape, F32) for f in fs], {i: i for i in range(n)}, n, copies)


def _exchange_small(v):
    r, W = v.shape

    def copies(r_in, r_out, ssem, rsem, base):
        x, y, c, _ = _place()
        return [pltpu.make_async_remote_copy(
            src_ref=r_in[0], dst_ref=r_out[0].at[m - 1], send_sem=ssem.at[base + m - 1],
            recv_sem=rsem.at[base + m - 1], device_id=(x ^ (m >> 2), y ^ ((m >> 1) & 1), c ^ (m & 1)),
            device_id_type=MESH) for m in range(1, 8)]

    return _Rider("small", [v], [jax.ShapeDtypeStruct((7, r, W), F32)], {}, 7, copies)


def _sum_small(v, landed):
    r, W = v.shape

    def body(v_ref, land_ref, o_ref):
        x, y, c, _ = _place()
        me = 4 * x + 2 * y + c
        total = jnp.zeros((r, W), F32)
        for d in range(8):
            slot = jnp.maximum((me ^ d) - 1, 0)
            total = total + jnp.where(me == d, v_ref[...], land_ref[slot])
        o_ref[...] = total

    vm = pl.BlockSpec(memory_space=pltpu.VMEM)
    return pl.pallas_call(
        body, name="sum_small", in_specs=[vm, vm], out_specs=vm, out_shape=jax.ShapeDtypeStruct((r, W), F32),
        compiler_params=pltpu.CompilerParams(vmem_limit_bytes=VMEM_LIMIT),
    )(v, landed)


def kernel(x, ffn1_w_gate, ffn1_w_up, ffn1_w_down, ffn2_w_gate, ffn2_w_up, ffn2_w_down, w_in, pool_w, pool_scale, conv_w, rpb, w_out, ln_g, ln_b, loss_target, m_ffn1_w_gate, m_ffn1_w_up, m_ffn1_w_down, m_ffn2_w_gate, m_ffn2_w_up, m_ffn2_w_down, m_w_in, m_pool_w, m_pool_scale, m_conv_w, m_rpb, m_w_out, m_ln_g, m_ln_b, v_ffn1_w_gate, v_ffn1_w_up, v_ffn1_w_down, v_ffn2_w_gate, v_ffn2_w_up, v_ffn2_w_down, v_w_in, v_pool_w, v_pool_scale, v_conv_w, v_rpb, v_w_out, v_ln_g, v_ln_b):
    weights = dict(ffn1_w_gate=ffn1_w_gate, ffn1_w_up=ffn1_w_up, ffn1_w_down=ffn1_w_down, ffn2_w_gate=ffn2_w_gate,
                   ffn2_w_up=ffn2_w_up, ffn2_w_down=ffn2_w_down, w_in=w_in, pool_w=pool_w, pool_scale=pool_scale,
                   conv_w=conv_w, rpb=rpb, w_out=w_out, ln_g=ln_g, ln_b=ln_b)
    mom_m = dict(ffn1_w_gate=m_ffn1_w_gate, ffn1_w_up=m_ffn1_w_up, ffn1_w_down=m_ffn1_w_down, ffn2_w_gate=m_ffn2_w_gate,
                 ffn2_w_up=m_ffn2_w_up, ffn2_w_down=m_ffn2_w_down, w_in=m_w_in, pool_w=m_pool_w,
                 pool_scale=m_pool_scale, conv_w=m_conv_w, rpb=m_rpb, w_out=m_w_out, ln_g=m_ln_g, ln_b=m_ln_b)
    mom_v = dict(ffn1_w_gate=v_ffn1_w_gate, ffn1_w_up=v_ffn1_w_up, ffn1_w_down=v_ffn1_w_down, ffn2_w_gate=v_ffn2_w_gate,
                 ffn2_w_up=v_ffn2_w_up, ffn2_w_down=v_ffn2_w_down, w_in=v_w_in, pool_w=v_pool_w,
                 pool_scale=v_pool_scale, conv_w=v_conv_w, rpb=v_rpb, w_out=v_w_out, ln_g=v_ln_g, ln_b=v_ln_b)
    order = list(weights)
    L = ffn1_w_gate.shape[0]
    xi, yi, ci = lax.axis_index("x"), lax.axis_index("y"), lax.axis_index("c")
    q_me = 2 * xi + yi
    x2 = x[0]
    target = loss_target[0]
    D = x2.shape[1]
    n_in = w_in.shape[-1]

    small = jnp.concatenate([ln_g.reshape(-1), ln_b.reshape(-1), conv_w.reshape(-1)])
    n_small = small.shape[0]
    small_rows = -(-n_small // (8 * 128)) * 8
    small = jnp.pad(small, (0, small_rows * 128 - n_small)).reshape(small_rows, 128)
    small_all = _gather_small(small).reshape(NQ, small_rows * 128)[:, :n_small]
    dq4 = D // NQ
    n_ln = L * 3 * dq4
    ln_g_all = small_all[:, :n_ln].reshape(NQ, L, 3, dq4).transpose(1, 2, 0, 3).reshape(L, 3, D)
    ln_b_all = small_all[:, n_ln:2 * n_ln].reshape(NQ, L, 3, dq4).transpose(1, 2, 0, 3).reshape(L, 3, D)
    conv_all = small_all[:, 2 * n_ln:].reshape(NQ, L, 3, D_CONV // NQ).transpose(1, 2, 0, 3).reshape(L, 3, D_CONV)

    def layer_shards(l):
        return [w[l].astype(BF) for w in (ffn1_w_gate, ffn1_w_up, ffn1_w_down, w_in, w_out, ffn2_w_gate, ffn2_w_up,
                                          ffn2_w_down)]

    def own_quarter(bufs, shards):
        return [lax.dynamic_update_slice(b, s[None], (q_me,) + (0,) * s.ndim) for b, s in zip(bufs, shards)]

    shards = [layer_shards(l) for l in range(L)]
    landed = _run_alone(_gather_ici(shards[0][:3]), "gather_ici")
    weights_of = [own_quarter(_run_alone(_gather_d2d(landed), "gather_d2d"), shards[0][:3])] + [None] * (L - 1)

    onehot_np, mask_np = _bias_constants()
    onehot, onehot_t, mask = jnp.asarray(onehot_np, BF), jnp.asarray(onehot_np.T.copy(), BF), jnp.asarray(mask_np)
    ng = len(POOL_WINDOWS)
    pg = D_POOL // ng
    saved = []
    h = x2
    hb = x2.astype(BF)
    for l in range(L):
        nxt = shards[l + 1] if l + 1 < L else None
        wg1, wu1, wd1 = weights_of[l][:3]
        eye = jnp.eye(ng, dtype=F32)
        wblk = (pool_w[l][:, :, None, :] * eye[:, None, :, None]).reshape(D_POOL, D_POOL).astype(BF)
        vec = jnp.concatenate([pool_scale[l][None], conv_all[l], jnp.zeros((4, D_POOL), F32)], axis=0)
        bias = _bias_table(rpb[l], onehot, mask)
        lg = [ln_g_all[l, j][None] for j in range(3)]
        lb = [ln_b_all[l, j][None] for j in range(3)]
        if l == 0:
            (x1, x1b, z1, g1, u1), got = _ffn_fwd(h, wg1, wu1, wd1, lg[0], lb[0], rider=_gather_ici(shards[0][3:]))
            weights_of[0] += own_quarter(_run_alone(_gather_d2d(got), "gather_d2d_rest"), shards[0][3:])
            r_attn = _gather_ici(nxt[:3]) if nxt else None
        else:
            riders = [_gather_d2d(landed)] + ([_gather_ici(nxt[:3])] if nxt else [])
            (x1, x1b, z1, g1, u1), got = _ffn_fwd(h, wg1, wu1, wd1, lg[0], lb[0], rider=_merge(*riders))
            weights_of[l] += own_quarter(got[:5], shards[l][3:])
            r_attn = _gather_d2d(got[5:]) if nxt else None
        wc, wo, wg2, wu2, wd2 = weights_of[l][3:]
        pabc, qkv = _proj(x1b, wc)
        yab = _mixab_fwd(pabc, wblk, vec)
        (yc,), got = _attn_fwd(qkv, bias, rider=r_attn)
        xm, xmb, zm = _mixout_fwd(yab, yc, x1, wo, lg[1], lb[1])
        if l == 0:
            r_ffn2 = _merge(_gather_d2d(got), _gather_ici(nxt[3:])) if nxt else None
        else:
            r_ffn2 = _gather_ici(nxt[3:]) if nxt else None
        (x3, x3b, z3, g3, u3), got2 = _ffn_fwd(xm, wg2, wu2, wd2, lg[2], lb[2], rider=r_ffn2)
        if nxt and l == 0:
            weights_of[1], landed = own_quarter(got2[:3], nxt[:3]), got2[3:]
        elif nxt:
            weights_of[l + 1], landed = own_quarter(got, nxt[:3]), got2
        saved.append(dict(wblk=wblk, vec=vec, bias=bias, lg=lg, hb=hb, z1=z1, g1=g1, u1=u1, x1b=x1b, pabc=pabc,
                          qkv=qkv, yab=yab, yc=yc, zm=zm, xmb=xmb, z3=z3, g3=g3, u3=u3))
        h, hb = x3, x3b

    loss_tile, dh = _loss_head(h, target)
    loss = lax.psum(loss_tile[0, 0], ("x", "y", "c"))

    def add_chip(arrs, recv):
        chip = [_add_chip(g, r) for g, r in zip(arrs, recv)]
        return [cf for cf, _ in chip], [cb for _, cb in chip]

    def add_final(chip_f, from_chips):
        return _join_halves([_add_final(cf, r) for cf, r in zip(chip_f, from_chips)])

    per_layer = [[None] * 6 for _ in range(L)]
    g_small = dict(pool_w=[None] * L, pool_scale=[None] * L, conv_w=[None] * L, rpb=[None] * L, ln_g=[None] * L,
                   ln_b=[None] * L)
    ffn1_g = None
    for l in reversed(range(L)):
        sv = saved[l]
        wg1, wu1, wd1, wc, wo, wg2, wu2, wd2 = weights_of[l]
        (dxm, df, dg, du, a, ln3), got = _ffn_bwd(dh, sv["z3"], sv["g3"], sv["u3"], wg2, wu2, wd2, sv["lg"][2],
                                                  rider=_swap_halves(ffn1_g) if ffn1_g else None)
        if ffn1_g:
            ffn1_f, ffn1_b = add_chip(ffn1_g, got)
        ffn2_g = [_wgrad_gate_up(sv["xmb"], dg, du)[0], _wgrad_down(a, df)]
        (dres, dzb, dycat, ln2), got = _mixout_bwd(dxm, sv["zm"], wo, sv["lg"][1], rider=_swap_halves(ffn2_g))
        ffn2_f, ffn2_b = add_chip(ffn2_g, got)
        g_o = _wgrad_out(sv["yab"], sv["yc"], dzb)
        dpabc, dwblk, dvec = _mixab_bwd(sv["pabc"], dycat, sv["wblk"], sv["vec"])
        (dq, dk, dv, dbias), got = _attn_bwd(sv["qkv"], sv["bias"], dycat,
                                             rider=_scatter_chips(ffn1_b) if ffn1_g else None)
        dparts = [dpabc, dq, dk, dv]
        g_in, got = _wgrad_in(sv["x1b"], dparts, n_in, rider=add_final(ffn1_f, got) if ffn1_g else None)
        if ffn1_g:
            per_layer[l + 1][0:2] = got
        mix_g = [g_in, g_o]
        (dx1,), got = _proj_bwd(dres, dparts, wc, rider=_swap_halves(mix_g))
        mix_f, mix_b = add_chip(mix_g, got)
        (dh, df, dg, du, a, ln1), got = _ffn_bwd(dx1, sv["z1"], sv["g1"], sv["u1"], wg1, wu1, wd1, sv["lg"][0],
                                                 rider=_scatter_chips(ffn2_b + mix_b))
        g_gu, per_layer[l][2:6] = _wgrad_gate_up(sv["hb"], dg, du, rider=add_final(ffn2_f + mix_f, got))
        ffn1_g = [g_gu, _wgrad_down(a, df)]
        g_small["pool_w"][l] = jnp.stack([dwblk[gi * pg:(gi + 1) * pg, gi * pg:(gi + 1) * pg] for gi in range(ng)])
        g_small["pool_scale"][l] = dvec[0]
        g_small["conv_w"][l] = dvec[1:4]
        g_small["rpb"][l] = _bias_grad(dbias, onehot_t)
        g_small["ln_g"][l] = jnp.stack([ln1[0], ln2[0], ln3[0]])
        g_small["ln_b"][l] = jnp.stack([ln1[1], ln2[1], ln3[1]])
    grad_x = dh[None]

    swapped = ("ffn1_w_gate", "ffn1_w_up", "ffn2_w_gate", "ffn2_w_up")

    def view(n, a):
        return jnp.swapaxes(a, 1, 2) if n in swapped else a

    def stacked(i, rows=None, swap=False):
        parts = [per_layer[l][i] if rows is None else per_layer[l][i][rows[0]:rows[1]] for l in range(L)]
        return jnp.stack([p.T for p in parts] if swap else parts)

    small_names = ("pool_w", "pool_scale", "conv_w", "rpb", "ln_g", "ln_b")
    small_full = {n: jnp.stack(g_small[n]) for n in small_names}
    vflat = jnp.concatenate([small_full[n].reshape(-1) for n in small_names])
    n_v = vflat.shape[0]
    v_cols = 1024
    v_rows = -(-n_v // (8 * v_cols)) * 8
    vpad = jnp.pad(vflat, (0, v_rows * v_cols - n_v)).reshape(v_rows, v_cols)
    ffn1_f, ffn1_b = add_chip(ffn1_g, _run_alone(_swap_halves(ffn1_g), "rs_swap_halves"))
    got = _run_alone(_merge(_scatter_chips(ffn1_b), _exchange_small(vpad)), "rs_scatter_chips")
    per_layer[0][0:2] = _run_alone(add_final(ffn1_f, got[:2]), "rs_join_halves")
    vsum = _sum_small(vpad, got[2]).reshape(-1)
    grads_v = dict(ffn1_w_gate=stacked(0, (0, D), True), ffn1_w_up=stacked(0, (D, 2 * D), True),
                   ffn1_w_down=stacked(1), ffn2_w_gate=stacked(2, (0, D), True),
                   ffn2_w_up=stacked(2, (D, 2 * D), True), ffn2_w_down=stacked(3), w_in=stacked(4), w_out=stacked(5))
    grads = {n: view(n, g) for n, g in grads_v.items()}
    delta, new_m, new_v = {}, {}, {}

    off = 0
    for n in small_names:
        sz = int(np.prod(small_full[n].shape))
        grads[n] = vsum[off:off + sz].reshape(small_full[n].shape)
        off += sz
    for n in ("conv_w", "ln_g", "ln_b"):
        width = weights[n].shape[-1]
        grads[n] = lax.dynamic_slice_in_dim(grads[n], q_me * width, width, axis=2)

    for n in order:
        res = _adamw(view(n, weights[n]), view(n, grads[n]), view(n, mom_m[n]), view(n, mom_v[n]))
        delta[n], new_m[n], new_v[n] = (view(n, o) for o in res)
    return (loss, grad_x, *[grads[n] for n in order], *[delta[n] for n in order], *[new_m[n] for n in order],
            *[new_v[n] for n in order])
```
